```python
import math
import jax, jax.numpy as jnp
from jax import lax
import numpy as np

D_MODEL = 2048
BATCH = 8
SEQ = 2048
DEPTH = 2

N_MIXERS = 2
N_A = (DEPTH + 1) // 2
N_B = DEPTH // 2
S5_WIDTH = D_MODEL
S5_GROUP = 16
S5_GROUPS = S5_WIDTH // S5_GROUP
S5_STATE = 64
S5_DT_MIN = 1e-3
S5_DT_MAX = 1e-1
LRU_WIDTH = ((4 * D_MODEL // 3 + 255) // 256) * 256
LRU_BLOCKS = 16
LRU_BLOCK = LRU_WIDTH // LRU_BLOCKS
LRU_C = 8.0
CONV_WIDTH = 4
FFN_HIDDEN = ((8 * D_MODEL // 3 + 255) // 256) * 256
N_MOD = 6
EPS = 1e-6

kernel_name = "adaln_hybrid_s5_rglru_swiglu"


def rmsnorm(x, gain):
    xf = x.astype(jnp.float32)
    y = xf * lax.rsqrt(jnp.mean(xf * xf, axis=-1, keepdims=True) + EPS) * gain.astype(jnp.float32)
    return y.astype(x.dtype)


def _complex_affine_combine(left, right):
    a1r, a1i, b1r, b1i = left
    a2r, a2i, b2r, b2i = right
    return (a2r * a1r - a2i * a1i,
            a2r * a1i + a2i * a1r,
            a2r * b1r - a2i * b1i + b2r,
            a2r * b1i + a2i * b1r + b2i)


def _real_affine_combine(left, right):
    a1, b1 = left
    a2, b2 = right
    return (a2 * a1, a2 * b1 + b2)


def s5_mixer(h, w_in, lam_re, lam_im, log_dt, b_re, b_im, c_re, c_im, d_skip, w_glu):
    f32 = jnp.float32
    bsz, seq, _ = h.shape
    u = h @ w_in
    uf = u.astype(f32)
    ug = uf.reshape(bsz, seq, S5_GROUPS, S5_GROUP)
    dt = jnp.exp(log_dt.astype(f32))[:, None]
    lr = lam_re.astype(f32)
    li = lam_im.astype(f32)
    mag = jnp.exp(lr * dt)
    ab_re = mag * jnp.cos(li * dt)
    ab_im = mag * jnp.sin(li * dt)
    nr, ni = ab_re - 1.0, ab_im
    den = lr * lr + li * li
    f_re = (nr * lr + ni * li) / den
    f_im = (ni * lr - nr * li) / den
    br, bi = b_re.astype(f32), b_im.astype(f32)
    bb_re = f_re[..., None] * br - f_im[..., None] * bi
    bb_im = f_re[..., None] * bi + f_im[..., None] * br
    bu_re = jnp.einsum('blgc,gpc->blgp', ug, bb_re)
    bu_im = jnp.einsum('blgc,gpc->blgp', ug, bb_im)
    a_re = jnp.broadcast_to(ab_re[None, None], (1, seq, S5_GROUPS, S5_STATE))
    a_im = jnp.broadcast_to(ab_im[None, None], (1, seq, S5_GROUPS, S5_STATE))
    _, _, s_re, s_im = lax.associative_scan(
        _complex_affine_combine, (a_re, a_im, bu_re, bu_im), axis=1)
    y = (jnp.einsum('blgp,gcp->blgc', s_re, c_re.astype(f32))
         - jnp.einsum('blgp,gcp->blgc', s_im, c_im.astype(f32)))
    y = y.reshape(bsz, seq, S5_WIDTH) + d_skip.astype(f32) * uf
    y = jax.nn.gelu(y).astype(h.dtype)
    val, gate = jnp.split(y @ w_glu, 2, axis=-1)
    return val * jax.nn.sigmoid(gate)


def causal_depthwise_conv(x, w, b):
    y = lax.conv_general_dilated(
        x, w.astype(x.dtype), window_strides=(1,), padding=[(CONV_WIDTH - 1, 0)],
        dimension_numbers=('NWC', 'WIO', 'NWC'), feature_group_count=x.shape[-1])
    return y + b.astype(x.dtype)


def rglru_mixer(h, w_in, conv_w, conv_b, w_rg, b_rg, w_ig, b_ig, lam, w_out):
    f32 = jnp.float32
    bsz, seq, _ = h.shape
    gate_branch, xb = jnp.split(h @ w_in, 2, axis=-1)
    xb = causal_depthwise_conv(xb, conv_w, conv_b).astype(f32)
    xblk = xb.reshape(bsz, seq, LRU_BLOCKS, LRU_BLOCK)
    r = jax.nn.sigmoid(jnp.einsum('blhi,hij->blhj', xblk, w_rg.astype(f32)).reshape(bsz, seq, LRU_WIDTH)
                       + b_rg.astype(f32))
    ig = jax.nn.sigmoid(jnp.einsum('blhi,hij->blhj', xblk, w_ig.astype(f32)).reshape(bsz, seq, LRU_WIDTH)
                        + b_ig.astype(f32))
    log_a = -LRU_C * r * jax.nn.softplus(-lam.astype(f32))
    a = jnp.exp(log_a)
    mult = jnp.sqrt(-jnp.expm1(2.0 * log_a))
    _, hs = lax.associative_scan(_real_affine_combine, (a, mult * (ig * xb)), axis=1)
    y = hs * jax.nn.gelu(gate_branch.astype(f32))
    return y.astype(h.dtype) @ w_out


def swiglu(h, w_gu, w_down):
    g, u = jnp.split(h @ w_gu, 2, axis=-1)
    return (jax.nn.silu(g) * u) @ w_down


def _fwd_setup_inputs(seed: int = 0) -> dict:
    key = jax.random.key(seed)
    ks = jax.random.split(key, 32)
    f32 = jnp.float32
    nrm = lambda k, shape, s: jax.random.normal(k, shape, f32) * s
    D = D_MODEL
    x = nrm(ks[0], (BATCH, SEQ, D), 1.0)
    c = nrm(ks[1], (BATCH, D), 1.0)
    norm_g = 1.0 + nrm(ks[2], (DEPTH, 2, D), 0.02)
    w_ada = nrm(ks[3], (DEPTH, D, N_MOD * D), 0.5 * D ** -0.5)
    b_ada = nrm(ks[4], (DEPTH, N_MOD * D), 0.02)
    s5_w_in = nrm(ks[5], (N_A, D, S5_WIDTH), D ** -0.5)
    n = jnp.arange(S5_STATE, dtype=f32)
    s5_lam_re = -0.5 + nrm(ks[6], (N_A, S5_GROUPS, S5_STATE), 0.01)
    s5_lam_im = math.pi * n + nrm(ks[7], (N_A, S5_GROUPS, S5_STATE), 0.01)
    s5_log_dt = jax.random.uniform(ks[8], (N_A, S5_GROUPS), f32,
                                   math.log(S5_DT_MIN), math.log(S5_DT_MAX))
    s5_b_re = nrm(ks[9], (N_A, S5_GROUPS, S5_STATE, S5_GROUP), (2 * S5_GROUP) ** -0.5)
    s5_b_im = nrm(ks[10], (N_A, S5_GROUPS, S5_STATE, S5_GROUP), (2 * S5_GROUP) ** -0.5)
    s5_c_re = nrm(ks[11], (N_A, S5_GROUPS, S5_GROUP, S5_STATE), (2 * S5_STATE) ** -0.5)
    s5_c_im = nrm(ks[12], (N_A, S5_GROUPS, S5_GROUP, S5_STATE), (2 * S5_STATE) ** -0.5)
    s5_d = nrm(ks[13], (N_A, S5_WIDTH), 1.0)
    s5_w_glu = nrm(ks[14], (N_A, S5_WIDTH, 2 * D), S5_WIDTH ** -0.5)
    lru_w_in = nrm(ks[15], (N_B, D, 2 * LRU_WIDTH), D ** -0.5)
    lru_conv_w = nrm(ks[16], (N_B, CONV_WIDTH, 1, LRU_WIDTH), CONV_WIDTH ** -0.5)
    lru_conv_b = nrm(ks[17], (N_B, LRU_WIDTH), 0.02)
    lru_w_rg = nrm(ks[18], (N_B, LRU_BLOCKS, LRU_BLOCK, LRU_BLOCK), LRU_BLOCK ** -0.5)
    lru_b_rg = nrm(ks[19], (N_B, LRU_WIDTH), 0.1)
    lru_w_ig = nrm(ks[20], (N_B, LRU_BLOCKS, LRU_BLOCK, LRU_BLOCK), LRU_BLOCK ** -0.5)
    lru_b_ig = nrm(ks[21], (N_B, LRU_WIDTH), 0.1)
    a_pow = jax.random.uniform(ks[22], (N_B, LRU_WIDTH), f32, 0.9, 0.999)
    a0 = a_pow ** (1.0 / LRU_C)
    lru_lam = jnp.log(a0) - jnp.log1p(-a0)
    lru_w_out = nrm(ks[23], (N_B, LRU_WIDTH, D), LRU_WIDTH ** -0.5)
    ffn_w_gu = nrm(ks[24], (DEPTH, D, 2 * FFN_HIDDEN), D ** -0.5)
    ffn_w_down = nrm(ks[25], (DEPTH, FFN_HIDDEN, D), FFN_HIDDEN ** -0.5)
    final_g = 1.0 + nrm(ks[26], (D,), 0.02)
    return {"x": x, "c": c, "norm_g": norm_g, "w_ada": w_ada, "b_ada": b_ada,
            "s5_w_in": s5_w_in, "s5_lam_re": s5_lam_re, "s5_lam_im": s5_lam_im,
            "s5_log_dt": s5_log_dt, "s5_b_re": s5_b_re, "s5_b_im": s5_b_im,
            "s5_c_re": s5_c_re, "s5_c_im": s5_c_im, "s5_d": s5_d, "s5_w_glu": s5_w_glu,
            "lru_w_in": lru_w_in, "lru_conv_w": lru_conv_w, "lru_conv_b": lru_conv_b,
            "lru_w_rg": lru_w_rg, "lru_b_rg": lru_b_rg, "lru_w_ig": lru_w_ig,
            "lru_b_ig": lru_b_ig, "lru_lam": lru_lam, "lru_w_out": lru_w_out,
            "ffn_w_gu": ffn_w_gu, "ffn_w_down": ffn_w_down, "final_g": final_g}


def _fwd_reference(x, c, norm_g, w_ada, b_ada,
              s5_w_in, s5_lam_re, s5_lam_im, s5_log_dt, s5_b_re, s5_b_im,
              s5_c_re, s5_c_im, s5_d, s5_w_glu,
              lru_w_in, lru_conv_w, lru_conv_b, lru_w_rg, lru_b_rg, lru_w_ig,
              lru_b_ig, lru_lam, lru_w_out,
              ffn_w_gu, ffn_w_down, final_g):
    cond = jax.nn.silu(c)
    for i in range(DEPTH):
        mod = cond @ w_ada[i] + b_ada[i]
        sh1, sc1, g1, sh2, sc2, g2 = [m[:, None, :] for m in jnp.split(mod, N_MOD, axis=-1)]
        h = rmsnorm(x, norm_g[i, 0]) * (1.0 + sc1) + sh1
        j = i // N_MIXERS
        if i % N_MIXERS == 0:
            y = s5_mixer(h, s5_w_in[j], s5_lam_re[j], s5_lam_im[j], s5_log_dt[j],
                         s5_b_re[j], s5_b_im[j], s5_c_re[j], s5_c_im[j], s5_d[j], s5_w_glu[j])
        else:
            y = rglru_mixer(h, lru_w_in[j], lru_conv_w[j], lru_conv_b[j], lru_w_rg[j],
                            lru_b_rg[j], lru_w_ig[j], lru_b_ig[j], lru_lam[j], lru_w_out[j])
        x = x + g1 * y
        h = rmsnorm(x, norm_g[i, 1]) * (1.0 + sc2) + sh2
        x = x + g2 * swiglu(h, ffn_w_gu[i], ffn_w_down[i])
    return rmsnorm(x, final_g)


import jax as _jax
import jax.numpy as _jnp

TWIN_FORMAT = 'train_step'
FWD_PARAMS = ['x', 'c', 'norm_g', 'w_ada', 'b_ada', 's5_w_in', 's5_lam_re', 's5_lam_im', 's5_log_dt', 's5_b_re', 's5_b_im', 's5_c_re', 's5_c_im', 's5_d', 's5_w_glu', 'lru_w_in', 'lru_conv_w', 'lru_conv_b', 'lru_w_rg', 'lru_b_rg', 'lru_w_ig', 'lru_b_ig', 'lru_lam', 'lru_w_out', 'ffn_w_gu', 'ffn_w_down', 'final_g']
TWIN_WEIGHTS = ['norm_g', 'w_ada', 'b_ada', 's5_w_in', 's5_lam_re', 's5_lam_im', 's5_log_dt', 's5_b_re', 's5_b_im', 's5_c_re', 's5_c_im', 's5_d', 's5_w_glu', 'lru_w_in', 'lru_conv_w', 'lru_conv_b', 'lru_w_rg', 'lru_b_rg', 'lru_w_ig', 'lru_b_ig', 'lru_lam', 'lru_w_out', 'ffn_w_gu', 'ffn_w_down', 'final_g']
TWIN_DIFF_INPUT = 'x'
TWIN_INPUTS = ['x', 'c', 'norm_g', 'w_ada', 'b_ada', 's5_w_in', 's5_lam_re', 's5_lam_im', 's5_log_dt', 's5_b_re', 's5_b_im', 's5_c_re', 's5_c_im', 's5_d', 's5_w_glu', 'lru_w_in', 'lru_conv_w', 'lru_conv_b', 'lru_w_rg', 'lru_b_rg', 'lru_w_ig', 'lru_b_ig', 'lru_lam', 'lru_w_out', 'ffn_w_gu', 'ffn_w_down', 'final_g', 'loss_target', 'm_norm_g', 'm_w_ada', 'm_b_ada', 'm_s5_w_in', 'm_s5_lam_re', 'm_s5_lam_im', 'm_s5_log_dt', 'm_s5_b_re', 'm_s5_b_im', 'm_s5_c_re', 'm_s5_c_im', 'm_s5_d', 'm_s5_w_glu', 'm_lru_w_in', 'm_lru_conv_w', 'm_lru_conv_b', 'm_lru_w_rg', 'm_lru_b_rg', 'm_lru_w_ig', 'm_lru_b_ig', 'm_lru_lam', 'm_lru_w_out', 'm_ffn_w_gu', 'm_ffn_w_down', 'm_final_g', 'v_norm_g', 'v_w_ada', 'v_b_ada', 'v_s5_w_in', 'v_s5_lam_re', 'v_s5_lam_im', 'v_s5_log_dt', 'v_s5_b_re', 'v_s5_b_im', 'v_s5_c_re', 'v_s5_c_im', 'v_s5_d', 'v_s5_w_glu', 'v_lru_w_in', 'v_lru_conv_w', 'v_lru_conv_b', 'v_lru_w_rg', 'v_lru_b_rg', 'v_lru_w_ig', 'v_lru_b_ig', 'v_lru_lam', 'v_lru_w_out', 'v_ffn_w_gu', 'v_ffn_w_down', 'v_final_g']
TWIN_OUTPUTS = ['loss', 'grad_x', 'grad_norm_g', 'grad_w_ada', 'grad_b_ada', 'grad_s5_w_in', 'grad_s5_lam_re', 'grad_s5_lam_im', 'grad_s5_log_dt', 'grad_s5_b_re', 'grad_s5_b_im', 'grad_s5_c_re', 'grad_s5_c_im', 'grad_s5_d', 'grad_s5_w_glu', 'grad_lru_w_in', 'grad_lru_conv_w', 'grad_lru_conv_b', 'grad_lru_w_rg', 'grad_lru_b_rg', 'grad_lru_w_ig', 'grad_lru_b_ig', 'grad_lru_lam', 'grad_lru_w_out', 'grad_ffn_w_gu', 'grad_ffn_w_down', 'grad_final_g', 'delta_norm_g', 'delta_w_ada', 'delta_b_ada', 'delta_s5_w_in', 'delta_s5_lam_re', 'delta_s5_lam_im', 'delta_s5_log_dt', 'delta_s5_b_re', 'delta_s5_b_im', 'delta_s5_c_re', 'delta_s5_c_im', 'delta_s5_d', 'delta_s5_w_glu', 'delta_lru_w_in', 'delta_lru_conv_w', 'delta_lru_conv_b', 'delta_lru_w_rg', 'delta_lru_b_rg', 'delta_lru_w_ig', 'delta_lru_b_ig', 'delta_lru_lam', 'delta_lru_w_out', 'delta_ffn_w_gu', 'delta_ffn_w_down', 'delta_final_g', 'new_m_norm_g', 'new_m_w_ada', 'new_m_b_ada', 'new_m_s5_w_in', 'new_m_s5_lam_re', 'new_m_s5_lam_im', 'new_m_s5_log_dt', 'new_m_s5_b_re', 'new_m_s5_b_im', 'new_m_s5_c_re', 'new_m_s5_c_im', 'new_m_s5_d', 'new_m_s5_w_glu', 'new_m_lru_w_in', 'new_m_lru_conv_w', 'new_m_lru_conv_b', 'new_m_lru_w_rg', 'new_m_lru_b_rg', 'new_m_lru_w_ig', 'new_m_lru_b_ig', 'new_m_lru_lam', 'new_m_lru_w_out', 'new_m_ffn_w_gu', 'new_m_ffn_w_down', 'new_m_final_g', 'new_v_norm_g', 'new_v_w_ada', 'new_v_b_ada', 'new_v_s5_w_in', 'new_v_s5_lam_re', 'new_v_s5_lam_im', 'new_v_s5_log_dt', 'new_v_s5_b_re', 'new_v_s5_b_im', 'new_v_s5_c_re', 'new_v_s5_c_im', 'new_v_s5_d', 'new_v_s5_w_glu', 'new_v_lru_w_in', 'new_v_lru_conv_w', 'new_v_lru_conv_b', 'new_v_lru_w_rg', 'new_v_lru_b_rg', 'new_v_lru_w_ig', 'new_v_lru_b_ig', 'new_v_lru_lam', 'new_v_lru_w_out', 'new_v_ffn_w_gu', 'new_v_ffn_w_down', 'new_v_final_g']
TWIN_LEAF_KINDS = {'loss': 'loss', 'grad_x': 'grad_x', 'grad_norm_g': 'grad_w', 'grad_w_ada': 'grad_w', 'grad_b_ada': 'grad_w', 'grad_s5_w_in': 'grad_w', 'grad_s5_lam_re': 'grad_w', 'grad_s5_lam_im': 'grad_w', 'grad_s5_log_dt': 'grad_w', 'grad_s5_b_re': 'grad_w', 'grad_s5_b_im': 'grad_w', 'grad_s5_c_re': 'grad_w', 'grad_s5_c_im': 'grad_w', 'grad_s5_d': 'grad_w', 'grad_s5_w_glu': 'grad_w', 'grad_lru_w_in': 'grad_w', 'grad_lru_conv_w': 'grad_w', 'grad_lru_conv_b': 'grad_w', 'grad_lru_w_rg': 'grad_w', 'grad_lru_b_rg': 'grad_w', 'grad_lru_w_ig': 'grad_w', 'grad_lru_b_ig': 'grad_w', 'grad_lru_lam': 'grad_w', 'grad_lru_w_out': 'grad_w', 'grad_ffn_w_gu': 'grad_w', 'grad_ffn_w_down': 'grad_w', 'grad_final_g': 'grad_w', 'delta_norm_g': 'delta_w', 'delta_w_ada': 'delta_w', 'delta_b_ada': 'delta_w', 'delta_s5_w_in': 'delta_w', 'delta_s5_lam_re': 'delta_w', 'delta_s5_lam_im': 'delta_w', 'delta_s5_log_dt': 'delta_w', 'delta_s5_b_re': 'delta_w', 'delta_s5_b_im': 'delta_w', 'delta_s5_c_re': 'delta_w', 'delta_s5_c_im': 'delta_w', 'delta_s5_d': 'delta_w', 'delta_s5_w_glu': 'delta_w', 'delta_lru_w_in': 'delta_w', 'delta_lru_conv_w': 'delta_w', 'delta_lru_conv_b': 'delta_w', 'delta_lru_w_rg': 'delta_w', 'delta_lru_b_rg': 'delta_w', 'delta_lru_w_ig': 'delta_w', 'delta_lru_b_ig': 'delta_w', 'delta_lru_lam': 'delta_w', 'delta_lru_w_out': 'delta_w', 'delta_ffn_w_gu': 'delta_w', 'delta_ffn_w_down': 'delta_w', 'delta_final_g': 'delta_w', 'new_m_norm_g': 'new_m', 'new_m_w_ada': 'new_m', 'new_m_b_ada': 'new_m', 'new_m_s5_w_in': 'new_m', 'new_m_s5_lam_re': 'new_m', 'new_m_s5_lam_im': 'new_m', 'new_m_s5_log_dt': 'new_m', 'new_m_s5_b_re': 'new_m', 'new_m_s5_b_im': 'new_m', 'new_m_s5_c_re': 'new_m', 'new_m_s5_c_im': 'new_m', 'new_m_s5_d': 'new_m', 'new_m_s5_w_glu': 'new_m', 'new_m_lru_w_in': 'new_m', 'new_m_lru_conv_w': 'new_m', 'new_m_lru_conv_b': 'new_m', 'new_m_lru_w_rg': 'new_m', 'new_m_lru_b_rg': 'new_m', 'new_m_lru_w_ig': 'new_m', 'new_m_lru_b_ig': 'new_m', 'new_m_lru_lam': 'new_m', 'new_m_lru_w_out': 'new_m', 'new_m_ffn_w_gu': 'new_m', 'new_m_ffn_w_down': 'new_m', 'new_m_final_g': 'new_m', 'new_v_norm_g': 'new_v', 'new_v_w_ada': 'new_v', 'new_v_b_ada': 'new_v', 'new_v_s5_w_in': 'new_v', 'new_v_s5_lam_re': 'new_v', 'new_v_s5_lam_im': 'new_v', 'new_v_s5_log_dt': 'new_v', 'new_v_s5_b_re': 'new_v', 'new_v_s5_b_im': 'new_v', 'new_v_s5_c_re': 'new_v', 'new_v_s5_c_im': 'new_v', 'new_v_s5_d': 'new_v', 'new_v_s5_w_glu': 'new_v', 'new_v_lru_w_in': 'new_v', 'new_v_lru_conv_w': 'new_v', 'new_v_lru_conv_b': 'new_v', 'new_v_lru_w_rg': 'new_v', 'new_v_lru_b_rg': 'new_v', 'new_v_lru_w_ig': 'new_v', 'new_v_lru_b_ig': 'new_v', 'new_v_lru_lam': 'new_v', 'new_v_lru_w_out': 'new_v', 'new_v_ffn_w_gu': 'new_v', 'new_v_ffn_w_down': 'new_v', 'new_v_final_g': 'new_v'}


def _forward(args):
    return _fwd_reference(*[args[k] for k in FWD_PARAMS])


def _output_shape():
    out = _jax.eval_shape(lambda: _forward(_fwd_setup_inputs(0)))
    return out.shape, out.dtype

N_MICROBATCH = 1
ADAM_LR = 0.001
ADAM_B1 = 0.9
ADAM_B2 = 0.999
ADAM_EPS = 1e-08
ADAM_WD = 0.01
ADAM_STEP = 10
PER_EXAMPLE_BATCH_AXIS = {'x': 0, 'c': 0, 'loss_target': 0}
SHARED_INPUTS = []
_WEIGHT_DTYPES = {'norm_g': _jnp.float32, 'w_ada': _jnp.float32, 'b_ada': _jnp.float32, 's5_w_in': _jnp.float32, 's5_lam_re': _jnp.float32, 's5_lam_im': _jnp.float32, 's5_log_dt': _jnp.float32, 's5_b_re': _jnp.float32, 's5_b_im': _jnp.float32, 's5_c_re': _jnp.float32, 's5_c_im': _jnp.float32, 's5_d': _jnp.float32, 's5_w_glu': _jnp.float32, 'lru_w_in': _jnp.float32, 'lru_conv_w': _jnp.float32, 'lru_conv_b': _jnp.float32, 'lru_w_rg': _jnp.float32, 'lru_b_rg': _jnp.float32, 'lru_w_ig': _jnp.float32, 'lru_b_ig': _jnp.float32, 'lru_lam': _jnp.float32, 'lru_w_out': _jnp.float32, 'ffn_w_gu': _jnp.float32, 'ffn_w_down': _jnp.float32, 'final_g': _jnp.float32}
MOMENT_SCALE = {'norm_g': 1.844419e-02, 'w_ada': 2.680589e-02, 'b_ada': 4.636102e-02, 's5_w_in': 8.591555e-03, 's5_lam_re': 5.569862e-04, 's5_lam_im': 6.705642e-04, 's5_log_dt': 2.892170e-01, 's5_b_re': 3.809095e-04, 's5_b_im': 3.619911e-04, 's5_c_re': 6.938008e-04, 's5_c_im': 6.873233e-04, 's5_d': 9.280327e-03, 's5_w_glu': 6.642573e-03, 'lru_w_in': 2.368851e-02, 'lru_conv_w': 2.523925e-02, 'lru_conv_b': 7.586587e-02, 'lru_w_rg': 2.444541e-03, 'lru_b_rg': 4.336705e-03, 'lru_w_ig': 4.852606e-03, 'lru_b_ig': 9.556701e-03, 'lru_lam': 1.106487e-02, 'lru_w_out': 2.717695e-02, 'ffn_w_gu': 7.996534e-03, 'ffn_w_down': 1.306826e-02, 'final_g': 8.034976e+00}


def _to_microbatches(a, axis):
    t = _jnp.moveaxis(a, axis, 0)
    t = t.reshape((N_MICROBATCH, t.shape[0] // N_MICROBATCH) + t.shape[1:])
    return _jnp.moveaxis(t, 1, axis + 1)


def setup_inputs(seed: int = 0) -> dict:
    inp = _fwd_setup_inputs(seed)
    key = _jax.random.fold_in(_jax.random.key(seed), 7919)
    shape, _ = _output_shape()
    out = dict(inp)
    out["loss_target"] = _jax.random.normal(_jax.random.fold_in(key, 0), shape, _jnp.float32)
    for i, name in enumerate(TWIN_WEIGHTS):
        w = inp[name].astype(_jnp.float32)
        if MOMENT_SCALE is None:
            s = _jnp.sqrt(_jnp.mean(_jnp.square(w)) + 1e-30)
        else:
            s = MOMENT_SCALE[name]
        km, kv = _jax.random.split(_jax.random.fold_in(key, i + 1))
        out[name] = w
        out["m_" + name] = s * _jax.random.normal(km, w.shape, _jnp.float32)
        out["v_" + name] = (s * s) * _jax.random.uniform(kv, w.shape, _jnp.float32, 0.5, 1.5)
    if N_MICROBATCH > 1:
        for name, axis in PER_EXAMPLE_BATCH_AXIS.items():
            out[name] = _to_microbatches(out[name], axis)
    return {'x': out['x'], 'c': out['c'], 'norm_g': out['norm_g'], 'w_ada': out['w_ada'], 'b_ada': out['b_ada'], 's5_w_in': out['s5_w_in'], 's5_lam_re': out['s5_lam_re'], 's5_lam_im': out['s5_lam_im'], 's5_log_dt': out['s5_log_dt'], 's5_b_re': out['s5_b_re'], 's5_b_im': out['s5_b_im'], 's5_c_re': out['s5_c_re'], 's5_c_im': out['s5_c_im'], 's5_d': out['s5_d'], 's5_w_glu': out['s5_w_glu'], 'lru_w_in': out['lru_w_in'], 'lru_conv_w': out['lru_conv_w'], 'lru_conv_b': out['lru_conv_b'], 'lru_w_rg': out['lru_w_rg'], 'lru_b_rg': out['lru_b_rg'], 'lru_w_ig': out['lru_w_ig'], 'lru_b_ig': out['lru_b_ig'], 'lru_lam': out['lru_lam'], 'lru_w_out': out['lru_w_out'], 'ffn_w_gu': out['ffn_w_gu'], 'ffn_w_down': out['ffn_w_down'], 'final_g': out['final_g'], 'loss_target': out['loss_target'], 'm_norm_g': out['m_norm_g'], 'm_w_ada': out['m_w_ada'], 'm_b_ada': out['m_b_ada'], 'm_s5_w_in': out['m_s5_w_in'], 'm_s5_lam_re': out['m_s5_lam_re'], 'm_s5_lam_im': out['m_s5_lam_im'], 'm_s5_log_dt': out['m_s5_log_dt'], 'm_s5_b_re': out['m_s5_b_re'], 'm_s5_b_im': out['m_s5_b_im'], 'm_s5_c_re': out['m_s5_c_re'], 'm_s5_c_im': out['m_s5_c_im'], 'm_s5_d': out['m_s5_d'], 'm_s5_w_glu': out['m_s5_w_glu'], 'm_lru_w_in': out['m_lru_w_in'], 'm_lru_conv_w': out['m_lru_conv_w'], 'm_lru_conv_b': out['m_lru_conv_b'], 'm_lru_w_rg': out['m_lru_w_rg'], 'm_lru_b_rg': out['m_lru_b_rg'], 'm_lru_w_ig': out['m_lru_w_ig'], 'm_lru_b_ig': out['m_lru_b_ig'], 'm_lru_lam': out['m_lru_lam'], 'm_lru_w_out': out['m_lru_w_out'], 'm_ffn_w_gu': out['m_ffn_w_gu'], 'm_ffn_w_down': out['m_ffn_w_down'], 'm_final_g': out['m_final_g'], 'v_norm_g': out['v_norm_g'], 'v_w_ada': out['v_w_ada'], 'v_b_ada': out['v_b_ada'], 'v_s5_w_in': out['v_s5_w_in'], 'v_s5_lam_re': out['v_s5_lam_re'], 'v_s5_lam_im': out['v_s5_lam_im'], 'v_s5_log_dt': out['v_s5_log_dt'], 'v_s5_b_re': out['v_s5_b_re'], 'v_s5_b_im': out['v_s5_b_im'], 'v_s5_c_re': out['v_s5_c_re'], 'v_s5_c_im': out['v_s5_c_im'], 'v_s5_d': out['v_s5_d'], 'v_s5_w_glu': out['v_s5_w_glu'], 'v_lru_w_in': out['v_lru_w_in'], 'v_lru_conv_w': out['v_lru_conv_w'], 'v_lru_conv_b': out['v_lru_conv_b'], 'v_lru_w_rg': out['v_lru_w_rg'], 'v_lru_b_rg': out['v_lru_b_rg'], 'v_lru_w_ig': out['v_lru_w_ig'], 'v_lru_b_ig': out['v_lru_b_ig'], 'v_lru_lam': out['v_lru_lam'], 'v_lru_w_out': out['v_lru_w_out'], 'v_ffn_w_gu': out['v_ffn_w_gu'], 'v_ffn_w_down': out['v_ffn_w_down'], 'v_final_g': out['v_final_g']}


def _loss(weights, diff, rest, loss_target):
    with _jax.named_scope("forward"):
        args = {**rest, TWIN_DIFF_INPUT: diff, **{k: w.astype(_WEIGHT_DTYPES[k]) for k, w in weights.items()}}
        y = _forward(args)
    with _jax.named_scope("loss_head"):
        err = _jnp.square(y.astype(_jnp.float32) - loss_target)
        return 0.5 * _jnp.sum(_jnp.mean(err, axis=-1)) if err.ndim else 0.5 * err


def _adamw(w, g, m, v):
    m = ADAM_B1 * m + (1.0 - ADAM_B1) * g
    v = ADAM_B2 * v + (1.0 - ADAM_B2) * _jnp.square(g)
    m_hat = m / (1.0 - ADAM_B1 ** ADAM_STEP)
    v_hat = v / (1.0 - ADAM_B2 ** ADAM_STEP)
    delta = -ADAM_LR * (m_hat / (_jnp.sqrt(v_hat) + ADAM_EPS) + ADAM_WD * w)
    return delta, m, v


def reference(x, c, norm_g, w_ada, b_ada, s5_w_in, s5_lam_re, s5_lam_im, s5_log_dt, s5_b_re, s5_b_im, s5_c_re, s5_c_im, s5_d, s5_w_glu, lru_w_in, lru_conv_w, lru_conv_b, lru_w_rg, lru_b_rg, lru_w_ig, lru_b_ig, lru_lam, lru_w_out, ffn_w_gu, ffn_w_down, final_g, loss_target, m_norm_g, m_w_ada, m_b_ada, m_s5_w_in, m_s5_lam_re, m_s5_lam_im, m_s5_log_dt, m_s5_b_re, m_s5_b_im, m_s5_c_re, m_s5_c_im, m_s5_d, m_s5_w_glu, m_lru_w_in, m_lru_conv_w, m_lru_conv_b, m_lru_w_rg, m_lru_b_rg, m_lru_w_ig, m_lru_b_ig, m_lru_lam, m_lru_w_out, m_ffn_w_gu, m_ffn_w_down, m_final_g, v_norm_g, v_w_ada, v_b_ada, v_s5_w_in, v_s5_lam_re, v_s5_lam_im, v_s5_log_dt, v_s5_b_re, v_s5_b_im, v_s5_c_re, v_s5_c_im, v_s5_d, v_s5_w_glu, v_lru_w_in, v_lru_conv_w, v_lru_conv_b, v_lru_w_rg, v_lru_b_rg, v_lru_w_ig, v_lru_b_ig, v_lru_lam, v_lru_w_out, v_ffn_w_gu, v_ffn_w_down, v_final_g):
    given = dict(x=x, c=c, norm_g=norm_g, w_ada=w_ada, b_ada=b_ada, s5_w_in=s5_w_in, s5_lam_re=s5_lam_re, s5_lam_im=s5_lam_im, s5_log_dt=s5_log_dt, s5_b_re=s5_b_re, s5_b_im=s5_b_im, s5_c_re=s5_c_re, s5_c_im=s5_c_im, s5_d=s5_d, s5_w_glu=s5_w_glu, lru_w_in=lru_w_in, lru_conv_w=lru_conv_w, lru_conv_b=lru_conv_b, lru_w_rg=lru_w_rg, lru_b_rg=lru_b_rg, lru_w_ig=lru_w_ig, lru_b_ig=lru_b_ig, lru_lam=lru_lam, lru_w_out=lru_w_out, ffn_w_gu=ffn_w_gu, ffn_w_down=ffn_w_down, final_g=final_g, loss_target=loss_target, m_norm_g=m_norm_g, m_w_ada=m_w_ada, m_b_ada=m_b_ada, m_s5_w_in=m_s5_w_in, m_s5_lam_re=m_s5_lam_re, m_s5_lam_im=m_s5_lam_im, m_s5_log_dt=m_s5_log_dt, m_s5_b_re=m_s5_b_re, m_s5_b_im=m_s5_b_im, m_s5_c_re=m_s5_c_re, m_s5_c_im=m_s5_c_im, m_s5_d=m_s5_d, m_s5_w_glu=m_s5_w_glu, m_lru_w_in=m_lru_w_in, m_lru_conv_w=m_lru_conv_w, m_lru_conv_b=m_lru_conv_b, m_lru_w_rg=m_lru_w_rg, m_lru_b_rg=m_lru_b_rg, m_lru_w_ig=m_lru_w_ig, m_lru_b_ig=m_lru_b_ig, m_lru_lam=m_lru_lam, m_lru_w_out=m_lru_w_out, m_ffn_w_gu=m_ffn_w_gu, m_ffn_w_down=m_ffn_w_down, m_final_g=m_final_g, v_norm_g=v_norm_g, v_w_ada=v_w_ada, v_b_ada=v_b_ada, v_s5_w_in=v_s5_w_in, v_s5_lam_re=v_s5_lam_re, v_s5_lam_im=v_s5_lam_im, v_s5_log_dt=v_s5_log_dt, v_s5_b_re=v_s5_b_re, v_s5_b_im=v_s5_b_im, v_s5_c_re=v_s5_c_re, v_s5_c_im=v_s5_c_im, v_s5_d=v_s5_d, v_s5_w_glu=v_s5_w_glu, v_lru_w_in=v_lru_w_in, v_lru_conv_w=v_lru_conv_w, v_lru_conv_b=v_lru_conv_b, v_lru_w_rg=v_lru_w_rg, v_lru_b_rg=v_lru_b_rg, v_lru_w_ig=v_lru_w_ig, v_lru_b_ig=v_lru_b_ig, v_lru_lam=v_lru_lam, v_lru_w_out=v_lru_w_out, v_ffn_w_gu=v_ffn_w_gu, v_ffn_w_down=v_ffn_w_down, v_final_g=v_final_g)
    weights = {n: given[n] for n in TWIN_WEIGHTS}
    shared = {n: given[n] for n in SHARED_INPUTS}
    per_example = {n: given[n] for n in ['x', 'c']}
    grad_fn = _jax.value_and_grad(_loss, argnums=(0, 1))

    def one_microbatch(ex, loss_target):
        ex = dict(ex)
        diff = ex.pop(TWIN_DIFF_INPUT)
        return grad_fn(weights, diff, {**shared, **ex}, loss_target)

    if N_MICROBATCH == 1:
        loss, (grad_w, grad_x) = one_microbatch(per_example, given["loss_target"])
    else:
        def body(carry, xs):
            loss_sum, grad_sum = carry
            l_k, (gw_k, gx_k) = one_microbatch(xs[0], xs[1])
            with _jax.named_scope("update"):
                return (loss_sum + l_k, _jax.tree.map(_jnp.add, grad_sum, gw_k)), gx_k

        init = (_jnp.zeros((), _jnp.float32), _jax.tree.map(_jnp.zeros_like, weights))
        (loss, grad_w), grad_x = _jax.lax.scan(body, init, (per_example, given["loss_target"]))
    with _jax.named_scope("update"):
        delta_w, new_m, new_v = {}, {}, {}
        for n in TWIN_WEIGHTS:
            delta_w[n], new_m[n], new_v[n] = _adamw(weights[n], grad_w[n], given["m_" + n], given["v_" + n])
    return (loss, grad_x, *[grad_w[n] for n in TWIN_WEIGHTS], *[delta_w[n] for n in TWIN_WEIGHTS],
            *[new_m[n] for n in TWIN_WEIGHTS], *[new_v[n] for n in TWIN_WEIGHTS])
```

```python
import functools
import math

import jax
import jax.numpy as jnp
from jax import lax
from jax.experimental import pallas as pl
from jax.experimental.pallas import tpu as pltpu

F32 = jnp.float32
MXU_DTYPE = jnp.bfloat16
WIRE_DTYPE = jnp.bfloat16
N_DEV = 8
EPS = 1e-6
LRU_C = 8.0
S5_GROUP = 16
S5_STATE = 64
S5_SUPER = 256
LRU_BLOCKS_PER_CHUNK = 4
ADAM_LR, ADAM_B1, ADAM_B2, ADAM_EPS, ADAM_WD, ADAM_STEP = 0.001, 0.9, 0.999, 1e-08, 0.01, 10
VMEM_LIMIT_BYTES = 56 * 1024 * 1024
LANE = 128
SUBLANE = 8

WEIGHTS = ['norm_g', 'w_ada', 'b_ada', 's5_w_in', 's5_lam_re', 's5_lam_im', 's5_log_dt', 's5_b_re', 's5_b_im',
           's5_c_re', 's5_c_im', 's5_d', 's5_w_glu', 'lru_w_in', 'lru_conv_w', 'lru_conv_b', 'lru_w_rg', 'lru_b_rg',
           'lru_w_ig', 'lru_b_ig', 'lru_lam', 'lru_w_out', 'ffn_w_gu', 'ffn_w_down', 'final_g']
BIG = ('w_ada', 's5_w_in', 's5_w_glu', 'lru_w_in', 'lru_w_out', 'ffn_w_gu', 'ffn_w_down')
SMALL = tuple(n for n in WEIGHTS if n not in BIG)
SMALL_SHARDED = ('norm_g', 'lru_conv_w', 'lru_conv_b', 'lru_b_rg', 'lru_b_ig', 'lru_lam')

NN = (((1,), (0,)), ((), ()))
NT = (((1,), (1,)), ((), ()))
TN = (((0,), (0,)), ((), ()))


def _params(n_grid):
    return pltpu.CompilerParams(dimension_semantics=("arbitrary",) * n_grid, vmem_limit_bytes=VMEM_LIMIT_BYTES)


def _tile(dim, pref, align=LANE):
    if dim <= pref:
        return dim
    t = (pref // align) * align
    while t >= align:
        if dim % t == 0:
            return t
        t -= align
    return dim


def _dot(a, b, dims):
    return lax.dot_general(a.astype(MXU_DTYPE), b.astype(MXU_DTYPE), dims, preferred_element_type=F32)


def _gelu(x):
    k = math.sqrt(2.0 / math.pi)
    return 0.5 * x * (1.0 + jnp.tanh(k * (x + 0.044715 * (x * x * x))))


def _gelu_and_grad(x):
    k = math.sqrt(2.0 / math.pi)
    th = jnp.tanh(k * (x + 0.044715 * (x * x * x)))
    g = 0.5 * x * (1.0 + th)
    dg = 0.5 * (1.0 + th) + 0.5 * x * (1.0 - th * th) * (k * (1.0 + 3.0 * 0.044715 * (x * x)))
    return g, dg


def _neg_expm1(x):
    series = -x * (1.0 + x * (0.5 + x * (1.0 / 6.0 + x * (1.0 / 24.0 + x * (1.0 / 120.0)))))
    return jnp.where(x > -0.01, series, 1.0 - jnp.exp(x))


def _exchange(name, xs, scatter):
    n = len(xs)
    chunk = xs[0].shape[1:] if scatter else xs[0].shape
    assert all(a.shape == xs[0].shape and a.dtype == xs[0].dtype for a in xs)

    def body(*refs):
        ins, out = refs[:n], refs[n]
        send_sems, recv_sems, local_sems = refs[n + 1:]
        x, y, c = lax.axis_index("x"), lax.axis_index("y"), lax.axis_index("c")
        me = 4 * x + 2 * y + c
        src = lambda l, dst: ins[l].at[dst] if scatter else ins[l]
        local = [pltpu.make_async_copy(src(l, me), out.at[me, l], local_sems.at[l]) for l in range(n)]
        for cp in local:
            cp.start()
        sent = []
        for l in range(n):
            for k in range(1, N_DEV):
                px = 1 - x if (k >> 2) & 1 else x
                py = 1 - y if (k >> 1) & 1 else y
                pc = 1 - c if k & 1 else c
                peer = 4 * px + 2 * py + pc

                def copy(land_at, l=l, k=k, peer=peer, pid=(px, py, pc)):
                    return pltpu.make_async_remote_copy(
                        src_ref=src(l, peer), dst_ref=out.at[land_at, l], send_sem=send_sems.at[l * N_DEV + k],
                        recv_sem=recv_sems.at[l * N_DEV + k], device_id=pid, device_id_type=pl.DeviceIdType.MESH)

                copy(me).start()
                sent.append((copy, peer))
        for copy, peer in sent:
            copy(me).wait_send()
            copy(peer).wait_recv()
        for cp in local:
            cp.wait()

    hbm = pl.BlockSpec(memory_space=pltpu.HBM)
    return pl.pallas_call(
        body, name=name, out_shape=jax.ShapeDtypeStruct((N_DEV, n) + chunk, xs[0].dtype),
        in_specs=[hbm] * n, out_specs=hbm,
        scratch_shapes=[pltpu.SemaphoreType.DMA((n * N_DEV,)), pltpu.SemaphoreType.DMA((n * N_DEV,)),
                        pltpu.SemaphoreType.DMA((n,))],
    )(*xs)


def _all_gather(name, x):
    return _exchange(name, [x], False).reshape((N_DEV,) + x.shape)


def _chunk_exchange(name, xs):
    got = _exchange(name, xs, True)
    return got.reshape((N_DEV, len(xs) * xs[0].shape[1]) + xs[0].shape[2:])


def _mm(name, a, b, out_shape, out_dtype, grid, a_spec, b_spec, o_spec, dims, n_red, acc_shape):
    red = tuple(range(len(grid) - n_red, len(grid)))

    def body(a_ref, b_ref, o_ref, acc_ref):
        first = functools.reduce(jnp.logical_and, [pl.program_id(ax) == 0 for ax in red])
        last = functools.reduce(jnp.logical_and, [pl.program_id(ax) == grid[ax] - 1 for ax in red])

        @pl.when(first)
        def _():
            acc_ref[...] = jnp.zeros_like(acc_ref)

        acc_ref[...] += _dot(a_ref[...], b_ref[...], dims)

        @pl.when(last)
        def _():
            o_ref[...] = acc_ref[...].astype(o_ref.dtype)

    return pl.pallas_call(
        body, name=name, out_shape=jax.ShapeDtypeStruct(out_shape, out_dtype), grid=grid,
        in_specs=[a_spec, b_spec], out_specs=o_spec, scratch_shapes=[pltpu.VMEM(acc_shape, F32)],
        compiler_params=_params(len(grid)))(a, b)


def _mm_col(name, a, b, out_dtype=F32):
    m, k = a.shape
    j, _, n = b.shape
    bm, bk = _tile(m, 1024), _tile(k, 512)
    return _mm(name, a, b, (j, m, n), out_dtype, (j, m // bm, k // bk),
               pl.BlockSpec((bm, bk), lambda jj, mm, kk: (mm, kk)),
               pl.BlockSpec((None, bk, n), lambda jj, mm, kk: (jj, kk, 0)),
               pl.BlockSpec((None, bm, n), lambda jj, mm, kk: (jj, mm, 0)), NN, 1, (bm, n))


def _mm_col_da(name, do, b):
    j, m, n = do.shape
    k = b.shape[1]
    bm, bk = _tile(m, 1024), _tile(k, 1024)
    return _mm(name, do, b, (m, k), F32, (m // bm, k // bk, j),
               pl.BlockSpec((None, bm, n), lambda mm, kk, jj: (jj, mm, 0)),
               pl.BlockSpec((None, bk, n), lambda mm, kk, jj: (jj, kk, 0)),
               pl.BlockSpec((bm, bk), lambda mm, kk, jj: (mm, kk)), NT, 1, (bm, bk))


def _mm_col_db(name, a, do, out_dtype):
    m, k = a.shape
    j, _, n = do.shape
    bm, bk = _tile(m, 1024), _tile(k, 512)
    return _mm(name, a, do, (j, k, n), out_dtype, (j, k // bk, m // bm),
               pl.BlockSpec((bm, bk), lambda jj, kk, mm: (mm, kk)),
               pl.BlockSpec((None, bm, n), lambda jj, kk, mm: (jj, mm, 0)),
               pl.BlockSpec((None, bk, n), lambda jj, kk, mm: (jj, kk, 0)), TN, 1, (bk, n))


def _row_bk(kq):
    return kq if (kq % LANE or kq // LANE in (11,)) else _tile(kq, 512)


def _mm_row(name, a, b, out_dtype=F32):
    q, m, kq = a.shape
    n = b.shape[1]
    bm, bn, bk = _tile(m, 1024), _tile(n, 1024), _row_bk(kq)
    nk = kq // bk
    return _mm(name, a, b, (m, n), out_dtype, (m // bm, n // bn, q, nk),
               pl.BlockSpec((None, bm, bk), lambda mm, nn, qq, kk: (qq, mm, kk)),
               pl.BlockSpec((bk, bn), lambda mm, nn, qq, kk: (qq * nk + kk, nn)),
               pl.BlockSpec((bm, bn), lambda mm, nn, qq, kk: (mm, nn)), NN, 2, (bm, bn))


def _mm_row_da(name, do, b, q):
    m, n = do.shape
    kq = b.shape[0] // q
    bm, bn = _tile(m, 1024), _tile(n, 1024)
    return _mm(name, do, b, (q, m, kq), F32, (q, m // bm, n // bn),
               pl.BlockSpec((bm, bn), lambda qq, mm, nn: (mm, nn)),
               pl.BlockSpec((kq, bn), lambda qq, mm, nn: (qq, nn)),
               pl.BlockSpec((None, bm, kq), lambda qq, mm, nn: (qq, mm, 0)), NT, 1, (bm, kq))


def _mm_row_db(name, a, do, out_dtype):
    q, m, kq = a.shape
    n = do.shape[1]
    bm, bn = _tile(m, 1024), _tile(n, 512)
    return _mm(name, a, do, (q * kq, n), out_dtype, (q, n // bn, m // bm),
               pl.BlockSpec((None, bm, kq), lambda qq, nn, mm: (qq, mm, 0)),
               pl.BlockSpec((bm, bn), lambda qq, nn, mm: (mm, nn)),
               pl.BlockSpec((kq, bn), lambda qq, nn, mm: (qq, nn)), TN, 1, (kq, bn))


def _row_spec(bm, d):
    return pl.BlockSpec((bm, d), lambda i: (i, 0))


def _vec_spec(d):
    return pl.BlockSpec((1, d), lambda i: (0, 0))


def _norm_mod_fwd(name, x, gain, sc, sh):
    t, d = x.shape
    bm = _tile(t, 256, SUBLANE)

    def body(x_ref, g_ref, sc_ref, sh_ref, h_ref):
        xv = x_ref[...]
        rstd = lax.rsqrt(jnp.mean(xv * xv, axis=-1, keepdims=True) + EPS)
        h_ref[...] = ((xv * rstd) * g_ref[...] * (1.0 + sc_ref[...]) + sh_ref[...]).astype(h_ref.dtype)

    return pl.pallas_call(
        body, name=name, out_shape=jax.ShapeDtypeStruct((t, d), MXU_DTYPE), grid=(t // bm,),
        in_specs=[_row_spec(bm, d), _vec_spec(d), _vec_spec(d), _vec_spec(d)], out_specs=_row_spec(bm, d),
        compiler_params=_params(1))(x, gain, sc, sh)


def _norm_mod_bwd(name, x, dh, dres, gain, sc):
    t, d = x.shape
    bm = _tile(t, 256, SUBLANE)

    def body(x_ref, dh_ref, dres_ref, g_ref, sc_ref, dx_ref, dg_ref, dsc_ref, dsh_ref):
        @pl.when(pl.program_id(0) == 0)
        def _():
            dg_ref[...] = jnp.zeros_like(dg_ref)
            dsc_ref[...] = jnp.zeros_like(dsc_ref)
            dsh_ref[...] = jnp.zeros_like(dsh_ref)

        xv, dh_ = x_ref[...], dh_ref[...]
        rstd = lax.rsqrt(jnp.mean(xv * xv, axis=-1, keepdims=True) + EPS)
        nrm = xv * rstd
        gain_ = g_ref[...]
        dsh_ref[...] += jnp.sum(dh_, axis=0, keepdims=True)
        dsc_ref[...] += jnp.sum(dh_ * (nrm * gain_), axis=0, keepdims=True)
        dhn = dh_ * (1.0 + sc_ref[...])
        dg_ref[...] += jnp.sum(dhn * nrm, axis=0, keepdims=True)
        dn = dhn * gain_
        dx_ref[...] = dres_ref[...] + rstd * (dn - nrm * jnp.mean(dn * nrm, axis=-1, keepdims=True))

    vec = jax.ShapeDtypeStruct((1, d), F32)
    return pl.pallas_call(
        body, name=name, out_shape=[jax.ShapeDtypeStruct((t, d), F32), vec, vec, vec], grid=(t // bm,),
        in_specs=[_row_spec(bm, d), _row_spec(bm, d), _row_spec(bm, d), _vec_spec(d), _vec_spec(d)],
        out_specs=[_row_spec(bm, d), _vec_spec(d), _vec_spec(d), _vec_spec(d)],
        compiler_params=_params(1))(x, dh, dres, gain, sc)


def _loss_bwd(name, x, target, gain):
    t, d = x.shape
    bm = _tile(t, 256, SUBLANE)

    def body(x_ref, t_ref, g_ref, dx_ref, loss_ref, dg_ref):
        @pl.when(pl.program_id(0) == 0)
        def _():
            loss_ref[...] = jnp.zeros_like(loss_ref)
            dg_ref[...] = jnp.zeros_like(dg_ref)

        xv = x_ref[...]
        rstd = lax.rsqrt(jnp.mean(xv * xv, axis=-1, keepdims=True) + EPS)
        nrm = xv * rstd
        gain_ = g_ref[...]
        err = nrm * gain_ - t_ref[...]
        per_tok = jnp.mean(err * err, axis=-1, keepdims=True)
        loss_ref[...] += 0.5 * jnp.sum(per_tok, axis=0, keepdims=True)
        dout = err * (1.0 / d)
        dg_ref[...] += jnp.sum(dout * nrm, axis=0, keepdims=True)
        dn = dout * gain_
        dx_ref[...] = rstd * (dn - nrm * jnp.mean(dn * nrm, axis=-1, keepdims=True))

    return pl.pallas_call(
        body, name=name,
        out_shape=[jax.ShapeDtypeStruct((t, d), F32), jax.ShapeDtypeStruct((1, 1), F32),
                   jax.ShapeDtypeStruct((1, d), F32)],
        grid=(t // bm,), in_specs=[_row_spec(bm, d), _row_spec(bm, d), _vec_spec(d)],
        out_specs=[_row_spec(bm, d), pl.BlockSpec((1, 1), lambda i: (0, 0)), _vec_spec(d)],
        compiler_params=_params(1))(x, target, gain)


def _resid(name, x, y, g):
    t, d = x.shape
    bm = _tile(t, 256, SUBLANE)

    def body(x_ref, y_ref, g_ref, o_ref):
        o_ref[...] = x_ref[...] + g_ref[...] * y_ref[...]

    return pl.pallas_call(
        body, name=name, out_shape=jax.ShapeDtypeStruct((t, d), F32), grid=(t // bm,),
        in_specs=[_row_spec(bm, d), _row_spec(bm, d), _vec_spec(d)], out_specs=_row_spec(bm, d),
        compiler_params=_params(1))(x, y, g)


def _gate_bwd(name, dx, y, g):
    t, d = dx.shape
    bm = _tile(t, 256, SUBLANE)

    def body(dx_ref, y_ref, g_ref, dy_ref, dg_ref):
        @pl.when(pl.program_id(0) == 0)
        def _():
            dg_ref[...] = jnp.zeros_like(dg_ref)

        dxv = dx_ref[...]
        dy_ref[...] = (g_ref[...] * dxv).astype(dy_ref.dtype)
        dg_ref[...] += jnp.sum(dxv * y_ref[...], axis=0, keepdims=True)

    return pl.pallas_call(
        body, name=name, out_shape=[jax.ShapeDtypeStruct((t, d), MXU_DTYPE), jax.ShapeDtypeStruct((1, d), F32)],
        grid=(t // bm,), in_specs=[_row_spec(bm, d), _row_spec(bm, d), _vec_spec(d)],
        out_specs=[_row_spec(bm, d), _vec_spec(d)], compiler_params=_params(1))(dx, y, g)


def _glu_resid_fwd(name, z, x, g):
    _, t, n = z.shape
    d = x.shape[1]
    half = N_DEV // 2
    bm = _tile(t, 256, SUBLANE)

    def body(v_ref, gt_ref, x_ref, g_ref, o_ref):
        o_ref[...] = x_ref[...] + g_ref[...] * (v_ref[...] * jax.nn.sigmoid(gt_ref[...]))

    return pl.pallas_call(
        body, name=name, out_shape=jax.ShapeDtypeStruct((t, d), F32), grid=(half, t // bm),
        in_specs=[pl.BlockSpec((None, bm, n), lambda q, i: (q, i, 0)),
                  pl.BlockSpec((None, bm, n), lambda q, i: (q + half, i, 0)),
                  pl.BlockSpec((bm, n), lambda q, i: (i, q)), pl.BlockSpec((1, n), lambda q, i: (0, q))],
        out_specs=pl.BlockSpec((bm, n), lambda q, i: (i, q)), compiler_params=_params(2))(z, z, x, g)


def _glu_resid_bwd(name, z, dx, g):
    _, t, n = z.shape
    d = dx.shape[1]
    half = N_DEV // 2
    bm = _tile(t, 256, SUBLANE)

    def body(v_ref, gt_ref, dx_ref, g_ref, dz_ref, dg_ref):
        part, i = pl.program_id(1), pl.program_id(2)
        v, dxv = v_ref[...], dx_ref[...]
        sig = jax.nn.sigmoid(gt_ref[...])
        dout = g_ref[...] * dxv

        @pl.when(part == 0)
        def _():
            @pl.when(i == 0)
            def _():
                dg_ref[...] = jnp.zeros_like(dg_ref)

            dg_ref[...] += jnp.sum(dxv * (v * sig), axis=0, keepdims=True)
            dz_ref[...] = (dout * sig).astype(dz_ref.dtype)

        @pl.when(part == 1)
        def _():
            dz_ref[...] = (dout * v * (sig * (1.0 - sig))).astype(dz_ref.dtype)

    return pl.pallas_call(
        body, name=name,
        out_shape=[jax.ShapeDtypeStruct((N_DEV, t, n), MXU_DTYPE), jax.ShapeDtypeStruct((1, d), F32)],
        grid=(half, 2, t // bm),
        in_specs=[pl.BlockSpec((None, bm, n), lambda q, p, i: (q, i, 0)),
                  pl.BlockSpec((None, bm, n), lambda q, p, i: (q + half, i, 0)),
                  pl.BlockSpec((bm, n), lambda q, p, i: (i, q)), pl.BlockSpec((1, n), lambda q, p, i: (0, q))],
        out_specs=[pl.BlockSpec((None, bm, n), lambda q, p, i: (q + half * p, i, 0)),
                   pl.BlockSpec((1, n), lambda q, p, i: (0, q))],
        compiler_params=_params(3))(z, z, dx, g)


def _swiglu_act_fwd(name, gu):
    _, t, n = gu.shape
    half = N_DEV // 2
    bm = _tile(t, 256, SUBLANE)

    def body(g_ref, u_ref, o_ref):
        gv = g_ref[...]
        o_ref[...] = (gv * jax.nn.sigmoid(gv) * u_ref[...]).astype(o_ref.dtype)

    return pl.pallas_call(
        body, name=name, out_shape=jax.ShapeDtypeStruct((half, t, n), MXU_DTYPE), grid=(half, t // bm),
        in_specs=[pl.BlockSpec((None, bm, n), lambda q, i: (q, i, 0)),
                  pl.BlockSpec((None, bm, n), lambda q, i: (q + half, i, 0))],
        out_specs=pl.BlockSpec((None, bm, n), lambda q, i: (q, i, 0)), compiler_params=_params(2))(gu, gu)


def _swiglu_act_bwd(name, gu, dact):
    _, t, n = gu.shape
    half = N_DEV // 2
    bm = _tile(t, 256, SUBLANE)

    def body(g_ref, u_ref, da_ref, o_ref):
        part = pl.program_id(1)
        gv, da = g_ref[...], da_ref[...]
        sig = jax.nn.sigmoid(gv)

        @pl.when(part == 0)
        def _():
            o_ref[...] = (da * u_ref[...] * (sig * (1.0 + gv * (1.0 - sig)))).astype(o_ref.dtype)

        @pl.when(part == 1)
        def _():
            o_ref[...] = (da * (gv * sig)).astype(o_ref.dtype)

    return pl.pallas_call(
        body, name=name, out_shape=jax.ShapeDtypeStruct((N_DEV, t, n), MXU_DTYPE), grid=(half, 2, t // bm),
        in_specs=[pl.BlockSpec((None, bm, n), lambda q, p, i: (q, i, 0)),
                  pl.BlockSpec((None, bm, n), lambda q, p, i: (q + half, i, 0)),
                  pl.BlockSpec((None, bm, n), lambda q, p, i: (q, i, 0))],
        out_specs=pl.BlockSpec((None, bm, n), lambda q, p, i: (q + half * p, i, 0)),
        compiler_params=_params(3))(gu, gu, dact)


def _ada_fwd(name, c16, w_ada, b_loc):
    nl, d, n = w_ada.shape
    bn = _tile(n, 512)

    def body(c_ref, w_ref, b_ref, o_ref):
        cv = c_ref[...]
        o_ref[...] = _dot(cv * jax.nn.sigmoid(cv), w_ref[...], NN) + b_ref[...]

    return pl.pallas_call(
        body, name=name, out_shape=jax.ShapeDtypeStruct((nl, c16.shape[0], n), F32), grid=(nl, n // bn),
        in_specs=[pl.BlockSpec(c16.shape, lambda i, j: (0, 0)), pl.BlockSpec((None, d, bn), lambda i, j: (i, 0, j)),
                  pl.BlockSpec((None, 1, bn), lambda i, j: (i, 0, j))],
        out_specs=pl.BlockSpec((None, c16.shape[0], bn), lambda i, j: (i, 0, j)),
        compiler_params=_params(2))(c16, w_ada, b_loc)


def _adam_update(g, w, m, v):
    m = ADAM_B1 * m + (1.0 - ADAM_B1) * g
    v = ADAM_B2 * v + (1.0 - ADAM_B2) * (g * g)
    m_hat = m / (1.0 - ADAM_B1 ** ADAM_STEP)
    v_hat = v / (1.0 - ADAM_B2 ** ADAM_STEP)
    delta = -ADAM_LR * (m_hat / (jnp.sqrt(v_hat) + ADAM_EPS) + ADAM_WD * w)
    return delta, m, v


def _adamw_w_ada(name, c16, dmod16, w, m, v):
    nl, d, n = w.shape
    br = _tile(d, 256)

    def body(c_ref, dm_ref, w_ref, m_ref, v_ref, g_ref, dl_ref, mo_ref, vo_ref):
        cv = c_ref[...]
        g = _dot(cv * jax.nn.sigmoid(cv), dm_ref[...], TN)
        g_ref[...] = g
        dl_ref[...], mo_ref[...], vo_ref[...] = _adam_update(g, w_ref[...], m_ref[...], v_ref[...])

    blk = pl.BlockSpec((None, br, n), lambda i, r: (i, r, 0))
    shp = jax.ShapeDtypeStruct(w.shape, F32)
    return pl.pallas_call(
        body, name=name, out_shape=[shp] * 4, grid=(nl, d // br),
        in_specs=[pl.BlockSpec((c16.shape[0], br), lambda i, r: (0, r)),
                  pl.BlockSpec((None, dmod16.shape[1], n), lambda i, r: (i, 0, 0)), blk, blk, blk],
        out_specs=[blk] * 4, compiler_params=_params(2))(c16, dmod16, w, m, v)


def _adamw_sum(name, parts, w, m, v):
    p, r, c = parts.shape
    br = _tile(r, 128, 2 * SUBLANE)

    def body(p_ref, w_ref, m_ref, v_ref, g_ref, dl_ref, mo_ref, vo_ref):
        g = p_ref[0].astype(F32)
        for s in range(1, p):
            g = g + p_ref[s].astype(F32)
        g_ref[...] = g
        dl_ref[...], mo_ref[...], vo_ref[...] = _adam_update(g, w_ref[...], m_ref[...], v_ref[...])

    blk = pl.BlockSpec((br, c), lambda i: (i, 0))
    shp = jax.ShapeDtypeStruct((r, c), F32)
    return pl.pallas_call(
        body, name=name, out_shape=[shp] * 4, grid=(r // br,),
        in_specs=[pl.BlockSpec((p, br, c), lambda i: (0, i, 0)), blk, blk, blk], out_specs=[blk] * 4,
        compiler_params=_params(1))(parts, w, m, v)


def _sum_parts(name, parts):
    p, r, c = parts.shape

    def body(p_ref, o_ref):
        g = p_ref[0]
        for s in range(1, p):
            g = g + p_ref[s]
        o_ref[...] = g

    return pl.pallas_call(body, name=name, out_shape=jax.ShapeDtypeStruct((r, c), F32))(parts)


def _s5_disc(name, lam_re, lam_im, log_dt, b_re, b_im):
    def body(lr_ref, li_ref, ld_ref, br_ref, bi_ref, ar_ref, ai_ref, bbr_ref, bbi_ref):
        lr, li = lr_ref[...], li_ref[...]
        dt = jnp.exp(ld_ref[...])
        mag = jnp.exp(lr * dt)
        a_re, a_im = mag * jnp.cos(li * dt), mag * jnp.sin(li * dt)
        nr, ni = a_re - 1.0, a_im
        den = lr * lr + li * li
        f_re, f_im = (nr * lr + ni * li) / den, (ni * lr - nr * li) / den
        br, bi = br_ref[...], bi_ref[...]
        ar_ref[...], ai_ref[...] = a_re, a_im
        bbr_ref[...] = f_re * br - f_im * bi
        bbi_ref[...] = f_re * bi + f_im * br

    s_a, s_b = jax.ShapeDtypeStruct(lam_re.shape, F32), jax.ShapeDtypeStruct(b_re.shape, F32)
    return pl.pallas_call(body, name=name, out_shape=[s_a, s_a, s_b, s_b])(lam_re, lam_im, log_dt, b_re, b_im)


def _s5_disc_bwd(name, lam_re, lam_im, log_dt, b_re, b_im, dab_re, dab_im, dbb_re, dbb_im):
    def body(lr_ref, li_ref, ld_ref, br_ref, bi_ref, dar_ref, dai_ref, dbbr_ref, dbbi_ref,
             dlr_ref, dli_ref, dld_ref, dbr_ref, dbi_ref):
        lr, li = lr_ref[...], li_ref[...]
        dt = jnp.exp(ld_ref[...])
        mag = jnp.exp(lr * dt)
        a_re, a_im = mag * jnp.cos(li * dt), mag * jnp.sin(li * dt)
        nr, ni = a_re - 1.0, a_im
        den = lr * lr + li * li
        f_re, f_im = (nr * lr + ni * li) / den, (ni * lr - nr * li) / den
        br, bi = br_ref[...], bi_ref[...]
        dbbr, dbbi = dbbr_ref[...], dbbi_ref[...]
        dbr_ref[...] = f_re * dbbr + f_im * dbbi
        dbi_ref[...] = f_re * dbbi - f_im * dbbr
        df_re = jnp.sum(dbbr * br + dbbi * bi, axis=1, keepdims=True)
        df_im = jnp.sum(dbbi * br - dbbr * bi, axis=1, keepdims=True)
        dnr = (df_re * lr - df_im * li) / den
        dni = (df_re * li + df_im * lr) / den
        dden = -(df_re * f_re + df_im * f_im) / den
        dlr = (df_re * nr + df_im * ni) / den + 2.0 * lr * dden
        dli = (df_re * ni - df_im * nr) / den + 2.0 * li * dden
        da_re, da_im = dar_ref[...] + dnr, dai_ref[...] + dni
        dmag_mag = da_re * a_re + da_im * a_im
        dth = da_im * a_re - da_re * a_im
        dlr_ref[...] = dlr + dmag_mag * dt
        dli_ref[...] = dli + dth * dt
        ddt = jnp.sum(dmag_mag * lr + dth * li, axis=2, keepdims=True)
        dld_ref[...] = ddt * dt

    s_a, s_b = jax.ShapeDtypeStruct(lam_re.shape, F32), jax.ShapeDtypeStruct(b_re.shape, F32)
    return pl.pallas_call(
        body, name=name, out_shape=[s_a, s_a, jax.ShapeDtypeStruct(log_dt.shape, F32), s_b, s_b],
    )(lam_re, lam_im, log_dt, b_re, b_im, dab_re, dab_im, dbb_re, dbb_im)


def _s5_time_block(t):
    return _tile(t, 128, SUBLANE)


def _s5_scan_fwd(name, u, bb_re, bb_im, ab_re, ab_im):
    t, d = u.shape
    nsg, cs, ns = bb_re.shape
    tb = _s5_time_block(t)

    def body(u_ref, bbr_hbm, bbi_hbm, ar_ref, ai_ref, sr_ref, si_ref, srm_ref, sim_ref, bbr, bbi, cr_ref, ci_ref):
        @pl.when(pl.program_id(0) == 0)
        def _():
            pltpu.sync_copy(bbr_hbm, bbr)
            pltpu.sync_copy(bbi_hbm, bbi)
            cr_ref[...] = jnp.zeros_like(cr_ref)
            ci_ref[...] = jnp.zeros_like(ci_ref)

        for sg in range(nsg):
            us = u_ref[:, sg * cs:(sg + 1) * cs]
            sr_ref[:, sg, :] = _dot(us, bbr[sg], NN)
            si_ref[:, sg, :] = _dot(us, bbi[sg], NN)
        ar, ai = ar_ref[...], ai_ref[...]

        def step(i, carry):
            cr, ci = carry
            nr = ar * cr - ai * ci + sr_ref[i]
            ni = ar * ci + ai * cr + si_ref[i]
            sr_ref[i] = nr
            si_ref[i] = ni
            return nr, ni

        cr, ci = lax.fori_loop(0, tb, step, (cr_ref[...], ci_ref[...]), unroll=2)
        cr_ref[...], ci_ref[...] = cr, ci
        for sg in range(nsg):
            srm_ref[sg] = sr_ref[:, sg, :].astype(MXU_DTYPE)
            sim_ref[sg] = si_ref[:, sg, :].astype(MXU_DTYPE)

    scan = jax.ShapeDtypeStruct((t, nsg, ns), F32)
    mxu = jax.ShapeDtypeStruct((nsg, t, ns), MXU_DTYPE)
    hbm = pl.BlockSpec(memory_space=pltpu.HBM)
    full = pl.BlockSpec((nsg, ns), lambda i: (0, 0))
    return pl.pallas_call(
        body, name=name, out_shape=[scan, scan, mxu, mxu], grid=(t // tb,),
        in_specs=[_row_spec(tb, d), hbm, hbm, full, full],
        out_specs=[pl.BlockSpec((tb, nsg, ns), lambda i: (i, 0, 0))] * 2 + [pl.BlockSpec((nsg, tb, ns), lambda i: (0, i, 0))] * 2,
        scratch_shapes=[pltpu.VMEM(bb_re.shape, bb_re.dtype), pltpu.VMEM(bb_im.shape, bb_im.dtype),
                        pltpu.VMEM((nsg, ns), F32), pltpu.VMEM((nsg, ns), F32)],
        compiler_params=_params(1))(u, bb_re, bb_im, ab_re, ab_im)


def _s5_out_fwd(name, s_re, s_im, cc_re, cc_im, u, dskip):
    nsg, t, ns = s_re.shape
    d = u.shape[1]
    cs = cc_re.shape[2]
    tb = _tile(t, 512, SUBLANE)

    def body(sr_ref, si_ref, cr_ref, ci_ref, u_ref, d_ref, yp_ref, ya_ref):
        y = _dot(sr_ref[...], cr_ref[...], NN) - _dot(si_ref[...], ci_ref[...], NN) + d_ref[...] * u_ref[...]
        yp_ref[...] = y
        ya_ref[...] = _gelu(y).astype(ya_ref.dtype)

    s_spec = pl.BlockSpec((None, tb, ns), lambda sg, i: (sg, i, 0))
    c_spec = pl.BlockSpec((None, ns, cs), lambda sg, i: (sg, 0, 0))
    col = pl.BlockSpec((tb, cs), lambda sg, i: (i, sg))
    return pl.pallas_call(
        body, name=name, out_shape=[jax.ShapeDtypeStruct((t, d), F32), jax.ShapeDtypeStruct((t, d), MXU_DTYPE)],
        grid=(nsg, t // tb), in_specs=[s_spec, s_spec, c_spec, c_spec, col, pl.BlockSpec((1, cs), lambda sg, i: (0, sg))],
        out_specs=[col, col], compiler_params=_params(2))(s_re, s_im, cc_re, cc_im, u, dskip)


def _gelu_bwd(name, dy, ypre):
    t, d = dy.shape
    bm = _tile(t, 256, SUBLANE)

    def body(dy_ref, yp_ref, o_ref):
        o_ref[...] = (dy_ref[...] * _gelu_and_grad(yp_ref[...])[1]).astype(o_ref.dtype)

    return pl.pallas_call(
        body, name=name, out_shape=jax.ShapeDtypeStruct((t, d), MXU_DTYPE), grid=(t // bm,),
        in_specs=[_row_spec(bm, d), _row_spec(bm, d)], out_specs=_row_spec(bm, d), compiler_params=_params(1))(dy, ypre)


def _s5_scan_bwd(name, dyp, cc_re, cc_im, ab_re, ab_im, s_re, s_im):
    t, d = dyp.shape
    nsg, ns, cs = cc_re.shape
    tb = _s5_time_block(t)
    nb = t // tb

    def body(dy_ref, ccr_hbm, cci_hbm, ar_ref, ai_ref, sr_ref, si_ref, lrm_ref, lim_ref, dar_ref, dai_ref,
             ccr, cci, lr_ref, li_ref, cr_ref, ci_ref):
        @pl.when(pl.program_id(0) == 0)
        def _():
            pltpu.sync_copy(ccr_hbm, ccr)
            pltpu.sync_copy(cci_hbm, cci)
            cr_ref[...] = jnp.zeros_like(cr_ref)
            ci_ref[...] = jnp.zeros_like(ci_ref)
            dar_ref[...] = jnp.zeros_like(dar_ref)
            dai_ref[...] = jnp.zeros_like(dai_ref)

        for sg in range(nsg):
            dys = dy_ref[:, sg * cs:(sg + 1) * cs]
            lr_ref[:, sg, :] = _dot(dys, ccr[sg], NT)
            li_ref[:, sg, :] = -_dot(dys, cci[sg], NT)
        ar, ai = ar_ref[...], ai_ref[...]

        def step(i, carry):
            cr, ci, dar, dai = carry
            j = tb - 1 - i
            sr, si = sr_ref[j], si_ref[j]
            dar = dar + (cr * sr + ci * si)
            dai = dai + (ci * sr - cr * si)
            nr = lr_ref[j] + (ar * cr + ai * ci)
            ni = li_ref[j] + (ar * ci - ai * cr)
            lr_ref[j] = nr
            li_ref[j] = ni
            return nr, ni, dar, dai

        cr, ci, dar, dai = lax.fori_loop(0, tb, step, (cr_ref[...], ci_ref[...], dar_ref[...], dai_ref[...]))
        cr_ref[...], ci_ref[...] = cr, ci
        dar_ref[...], dai_ref[...] = dar, dai
        for sg in range(nsg):
            lrm_ref[sg] = lr_ref[:, sg, :].astype(MXU_DTYPE)
            lim_ref[sg] = li_ref[:, sg, :].astype(MXU_DTYPE)

    hbm = pl.BlockSpec(memory_space=pltpu.HBM)
    full = pl.BlockSpec((nsg, ns), lambda i: (0, 0))
    mxu = jax.ShapeDtypeStruct((nsg, t, ns), MXU_DTYPE)
    acc = jax.ShapeDtypeStruct((nsg, ns), F32)
    scan_spec = pl.BlockSpec((tb, nsg, ns), lambda i: (nb - 1 - i, 0, 0))
    return pl.pallas_call(
        body, name=name, out_shape=[mxu, mxu, acc, acc], grid=(nb,),
        in_specs=[pl.BlockSpec((tb, d), lambda i: (nb - 1 - i, 0)), hbm, hbm, full, full, scan_spec, scan_spec],
        out_specs=[pl.BlockSpec((nsg, tb, ns), lambda i: (0, nb - 1 - i, 0))] * 2 + [full, full],
        scratch_shapes=[pltpu.VMEM(cc_re.shape, cc_re.dtype), pltpu.VMEM(cc_im.shape, cc_im.dtype),
                        pltpu.VMEM((tb, nsg, ns), F32), pltpu.VMEM((tb, nsg, ns), F32),
                        pltpu.VMEM((nsg, ns), F32), pltpu.VMEM((nsg, ns), F32)],
        compiler_params=_params(1))(dyp, cc_re, cc_im, ab_re, ab_im, s_re, s_im)


def _s5_grads(name, lam_re, lam_im, s_re, s_im, u, dyp, bb_re, bb_im, dskip):
    nsg, t, ns = lam_re.shape
    d = u.shape[1]
    cs = bb_re.shape[1]
    tb = _tile(t, 512, SUBLANE)

    def body(lr_ref, li_ref, sr_ref, si_ref, u_ref, dy_ref, bbr_ref, bbi_ref, d_ref,
             du_ref, dbbr_ref, dbbi_ref, dccr_ref, dcci_ref, dd_ref):
        @pl.when(pl.program_id(1) == 0)
        def _():
            for r in (dbbr_ref, dbbi_ref, dccr_ref, dcci_ref, dd_ref):
                r[...] = jnp.zeros_like(r)

        lr, li, uv, dy = lr_ref[...], li_ref[...], u_ref[...], dy_ref[...]
        dyf = dy.astype(F32)
        du_ref[...] = _dot(lr, bbr_ref[...], NT) + _dot(li, bbi_ref[...], NT) + d_ref[...] * dyf
        dbbr_ref[...] += _dot(uv, lr, TN)
        dbbi_ref[...] += _dot(uv, li, TN)
        dccr_ref[...] += _dot(sr_ref[...], dy, TN)
        dcci_ref[...] -= _dot(si_ref[...], dy, TN)
        dd_ref[...] += jnp.sum(dyf * uv, axis=0, keepdims=True)

    s_spec = pl.BlockSpec((None, tb, ns), lambda sg, i: (sg, i, 0))
    col = pl.BlockSpec((tb, cs), lambda sg, i: (i, sg))
    b_spec = pl.BlockSpec((None, cs, ns), lambda sg, i: (sg, 0, 0))
    c_spec = pl.BlockSpec((None, ns, cs), lambda sg, i: (sg, 0, 0))
    vec = pl.BlockSpec((1, cs), lambda sg, i: (0, sg))
    return pl.pallas_call(
        body, name=name,
        out_shape=[jax.ShapeDtypeStruct((t, d), F32), jax.ShapeDtypeStruct(bb_re.shape, F32),
                   jax.ShapeDtypeStruct(bb_re.shape, F32), jax.ShapeDtypeStruct((nsg, ns, cs), F32),
                   jax.ShapeDtypeStruct((nsg, ns, cs), F32), jax.ShapeDtypeStruct((1, d), F32)],
        grid=(nsg, t // tb), in_specs=[s_spec, s_spec, s_spec, s_spec, col, col, b_spec, b_spec, vec],
        out_specs=[col, b_spec, b_spec, c_spec, c_spec, vec],
        compiler_params=_params(2))(lam_re, lam_im, s_re, s_im, u, dyp, bb_re, bb_im, dskip)


def _shift_down(x, k, prev8):
    if k == 0:
        return x
    ext = jnp.concatenate([prev8, x], axis=0)
    return ext[SUBLANE - k:SUBLANE - k + x.shape[0]]


def _shift_up(x, k, next8):
    if k == 0:
        return x
    ext = jnp.concatenate([x, next8], axis=0)
    return ext[k:k + x.shape[0]]


def _lru_time_block(t):
    return _tile(t, 256, SUBLANE)


def _lru_gates(xp, prev8, cv_ref, wrg, wig):
    taps = cv_ref.shape[0] - 4
    row = lambda k: cv_ref[k:k + 1, :]
    xs = [_shift_down(xp, taps - 1 - k, prev8) for k in range(taps)]
    xb = row(taps)
    for k in range(taps):
        xb = xb + row(k) * xs[k]
    r = jax.nn.sigmoid(_dot(xb, wrg, NN) + row(taps + 1))
    ig = jax.nn.sigmoid(_dot(xb, wig, NN) + row(taps + 2))
    sp = jax.nn.softplus(-row(taps + 3))
    log_a = -LRU_C * r * sp
    a = jnp.exp(log_a)
    mult = jnp.sqrt(_neg_expm1(2.0 * log_a))
    return xs, xb, r, ig, sp, a, mult


def _lru_fwd(name, zz, cvec, wrg, wig):
    _, t, w = zz.shape
    half = N_DEV // 2
    tb = _lru_time_block(t)

    def body(gb_ref, xp_ref, xprev_ref, cv_ref, wrg_ref, wig_ref, hs_ref, y_ref, a_scr, b_scr, carry):
        i = pl.program_id(1)

        @pl.when(i == 0)
        def _():
            carry[...] = jnp.zeros_like(carry)

        prev8 = jnp.where(i > 0, xprev_ref[...], 0.0)
        _, xb, _, ig, _, a, mult = _lru_gates(xp_ref[...], prev8, cv_ref, wrg_ref[...], wig_ref[...])
        a_scr[...] = a
        b_scr[...] = mult * (ig * xb)

        def step(j, h):
            h = a_scr[pl.ds(j, 1), :] * h + b_scr[pl.ds(j, 1), :]
            hs_ref[pl.ds(j, 1), :] = h
            return h

        carry[0:1, :] = lax.fori_loop(0, tb, step, carry[0:1, :], unroll=8)
        y_ref[...] = (hs_ref[...] * _gelu(gb_ref[...])).astype(y_ref.dtype)

    nrow = tb // SUBLANE
    blk = lambda off: pl.BlockSpec((None, tb, w), lambda q, i: (q + off, i, 0))
    return pl.pallas_call(
        body, name=name,
        out_shape=[jax.ShapeDtypeStruct((half, t, w), F32), jax.ShapeDtypeStruct((half, t, w), MXU_DTYPE)],
        grid=(half, t // tb),
        in_specs=[blk(0), blk(half),
                  pl.BlockSpec((None, SUBLANE, w), lambda q, i: (q + half, jnp.maximum(i * nrow - 1, 0), 0)),
                  pl.BlockSpec((None,) + cvec.shape[1:], lambda q, i: (q, 0, 0)),
                  pl.BlockSpec((None, w, w), lambda q, i: (q, 0, 0)), pl.BlockSpec((None, w, w), lambda q, i: (q, 0, 0))],
        out_specs=[blk(0), blk(0)],
        scratch_shapes=[pltpu.VMEM((tb, w), F32), pltpu.VMEM((tb, w), F32), pltpu.VMEM((SUBLANE, w), F32)],
        compiler_params=_params(2))(zz, zz, zz, cvec, wrg, wig)


def _lru_bwd(name, zz, hs, dy, cvec, wrg, wig):
    _, t, w = zz.shape
    half = N_DEV // 2
    tb = _lru_time_block(t)
    nb = t // tb
    taps = cvec.shape[1] - 4

    def body(gb_ref, xp_ref, xprev_ref, hs_ref, hprev_ref, dy_ref, cv_ref, wrg_ref, wig_ref,
             dgb_ref, dxp_ref, dcv_ref, dwrg_ref, dwig_ref, a_scr, l_scr, carry, dxb_next):
        i = pl.program_id(1)

        @pl.when(i == 0)
        def _():
            for r_ in (carry, dxb_next, dcv_ref, dwrg_ref, dwig_ref):
                r_[...] = jnp.zeros_like(r_)

        has_prev = i < nb - 1
        row = lambda k: cv_ref[k:k + 1, :]
        prev8 = jnp.where(has_prev, xprev_ref[...], 0.0)
        xs, xb, r, ig, sp, a, mult = _lru_gates(xp_ref[...], prev8, cv_ref, wrg_ref[...], wig_ref[...])
        hs_ = hs_ref[...]
        hs_m1 = _shift_down(hs_, 1, jnp.where(has_prev, hprev_ref[...], 0.0))
        gel, dgel = _gelu_and_grad(gb_ref[...])
        dy_ = dy_ref[...]
        dgb_ref[...] = (dy_ * hs_ * dgel).astype(dgb_ref.dtype)
        a_scr[...] = a
        l_scr[...] = dy_ * gel

        def step(k, c):
            j = tb - 1 - k
            lam = l_scr[pl.ds(j, 1), :] + c
            l_scr[pl.ds(j, 1), :] = lam
            return a_scr[pl.ds(j, 1), :] * lam

        carry[0:1, :] = lax.fori_loop(0, tb, step, carry[0:1, :], unroll=8)
        lam = l_scr[...]
        dmult = lam * (ig * xb)
        dig = lam * (mult * xb)
        dxb = lam * (mult * ig)
        dlog_a = (lam * hs_m1) * a - dmult * (a * a) / mult
        dr = dlog_a * (-LRU_C * sp)
        dsp = jnp.sum(dlog_a * (-LRU_C * r), axis=0, keepdims=True)
        dpr = dr * (r * (1.0 - r))
        dpi = dig * (ig * (1.0 - ig))
        dwrg_ref[...] += _dot(xb, dpr, TN)
        dwig_ref[...] += _dot(xb, dpi, TN)
        dxb = dxb + _dot(dpr, wrg_ref[...], NT) + _dot(dpi, wig_ref[...], NT)
        for k in range(taps):
            dcv_ref[k:k + 1, :] += jnp.sum(dxb * xs[k], axis=0, keepdims=True)
        dcv_ref[taps:taps + 1, :] += jnp.sum(dxb, axis=0, keepdims=True)
        dcv_ref[taps + 1:taps + 2, :] += jnp.sum(dpr, axis=0, keepdims=True)
        dcv_ref[taps + 2:taps + 3, :] += jnp.sum(dpi, axis=0, keepdims=True)
        dcv_ref[taps + 3:taps + 4, :] += dsp * (-jax.nn.sigmoid(-row(taps + 3)))
        nxt8 = dxb_next[...]
        dxp = row(taps - 1) * dxb
        for k in range(taps - 1):
            dxp = dxp + row(k) * _shift_up(dxb, taps - 1 - k, nxt8)
        dxp_ref[...] = dxp.astype(dxp_ref.dtype)
        dxb_next[...] = dxb[0:SUBLANE]

    nrow = tb // SUBLANE
    blk = lambda off: pl.BlockSpec((None, tb, w), lambda q, i: (q + off, nb - 1 - i, 0))
    halo = lambda off: pl.BlockSpec((None, SUBLANE, w), lambda q, i: (q + off, jnp.maximum((nb - 1 - i) * nrow - 1, 0), 0))
    wspec = pl.BlockSpec((None, w, w), lambda q, i: (q, 0, 0))
    cspec = pl.BlockSpec((None,) + cvec.shape[1:], lambda q, i: (q, 0, 0))
    act = jax.ShapeDtypeStruct((half, t, w), MXU_DTYPE)
    return pl.pallas_call(
        body, name=name,
        out_shape=[act, act, jax.ShapeDtypeStruct(cvec.shape, F32), jax.ShapeDtypeStruct(wrg.shape, F32),
                   jax.ShapeDtypeStruct(wig.shape, F32)],
        grid=(half, nb),
        in_specs=[blk(0), blk(half), halo(half), blk(0), halo(0), blk(0), cspec, wspec, wspec],
        out_specs=[blk(0), blk(0), cspec, wspec, wspec],
        scratch_shapes=[pltpu.VMEM((tb, w), F32), pltpu.VMEM((tb, w), F32), pltpu.VMEM((SUBLANE, w), F32),
                        pltpu.VMEM((SUBLANE, w), F32)],
        compiler_params=_params(2))(zz, zz, zz, hs, hs, dy, cvec, wrg, wig)


def _band(blocks, per):
    n, a, b = blocks.shape
    eye = jnp.eye(per, dtype=blocks.dtype)
    x = blocks.reshape(n // per, per, a, b)
    return jnp.einsum('sgab,gh->sgahb', x, eye).reshape(n // per, per * a, per * b)


def _unband(bands, per):
    s, pa, pb = bands.shape
    a, b = pa // per, pb // per
    x = bands.reshape(s, per, a, per, b)
    idx = jnp.arange(per)
    return x[:, idx, :, idx, :].transpose(1, 0, 2, 3).reshape(s * per, a, b)


def _pack(arrays, rows_multiple, lanes=LANE):
    flat = jnp.concatenate([a.reshape(-1).astype(F32) for a in arrays])
    rows = -(-flat.shape[0] // (lanes * rows_multiple)) * rows_multiple
    return jnp.pad(flat, (0, rows * lanes - flat.shape[0])).reshape(rows, lanes)


def _unpack(packed, shapes):
    flat = packed.reshape(-1)
    out, off = [], 0
    for s in shapes:
        n = math.prod(s)
        out.append(flat[off:off + n].reshape(s))
        off += n
    return out


def kernel(x, c, norm_g, w_ada, b_ada, s5_w_in, s5_lam_re, s5_lam_im, s5_log_dt, s5_b_re, s5_b_im, s5_c_re, s5_c_im, s5_d, s5_w_glu, lru_w_in, lru_conv_w, lru_conv_b, lru_w_rg, lru_b_rg, lru_w_ig, lru_b_ig, lru_lam, lru_w_out, ffn_w_gu, ffn_w_down, final_g, loss_target, m_norm_g, m_w_ada, m_b_ada, m_s5_w_in, m_s5_lam_re, m_s5_lam_im, m_s5_log_dt, m_s5_b_re, m_s5_b_im, m_s5_c_re, m_s5_c_im, m_s5_d, m_s5_w_glu, m_lru_w_in, m_lru_conv_w, m_lru_conv_b, m_lru_w_rg, m_lru_b_rg, m_lru_w_ig, m_lru_b_ig, m_lru_lam, m_lru_w_out, m_ffn_w_gu, m_ffn_w_down, m_final_g, v_norm_g, v_w_ada, v_b_ada, v_s5_w_in, v_s5_lam_re, v_s5_lam_im, v_s5_log_dt, v_s5_b_re, v_s5_b_im, v_s5_c_re, v_s5_c_im, v_s5_d, v_s5_w_glu, v_lru_w_in, v_lru_conv_w, v_lru_conv_b, v_lru_w_rg, v_lru_b_rg, v_lru_w_ig, v_lru_b_ig, v_lru_lam, v_lru_w_out, v_ffn_w_gu, v_ffn_w_down, v_final_g):
    wv = dict(zip(WEIGHTS, (norm_g, w_ada, b_ada, s5_w_in, s5_lam_re, s5_lam_im, s5_log_dt, s5_b_re, s5_b_im, s5_c_re, s5_c_im, s5_d, s5_w_glu, lru_w_in, lru_conv_w, lru_conv_b, lru_w_rg, lru_b_rg, lru_w_ig, lru_b_ig, lru_lam, lru_w_out, ffn_w_gu, ffn_w_down, final_g)))
    mv = dict(zip(WEIGHTS, (m_norm_g, m_w_ada, m_b_ada, m_s5_w_in, m_s5_lam_re, m_s5_lam_im, m_s5_log_dt, m_s5_b_re, m_s5_b_im, m_s5_c_re, m_s5_c_im, m_s5_d, m_s5_w_glu, m_lru_w_in, m_lru_conv_w, m_lru_conv_b, m_lru_w_rg, m_lru_b_rg, m_lru_w_ig, m_lru_b_ig, m_lru_lam, m_lru_w_out, m_ffn_w_gu, m_ffn_w_down, m_final_g)))
    vv = dict(zip(WEIGHTS, (v_norm_g, v_w_ada, v_b_ada, v_s5_w_in, v_s5_lam_re, v_s5_lam_im, v_s5_log_dt, v_s5_b_re, v_s5_b_im, v_s5_c_re, v_s5_c_im, v_s5_d, v_s5_w_glu, v_lru_w_in, v_lru_conv_w, v_lru_conv_b, v_lru_w_rg, v_lru_b_rg, v_lru_w_ig, v_lru_b_ig, v_lru_lam, v_lru_w_out, v_ffn_w_gu, v_ffn_w_down, v_final_g)))

    me = 4 * lax.axis_index("x") + 2 * lax.axis_index("y") + lax.axis_index("c")
    x0 = x[0]
    tgt = loss_target[0]
    t, d = x0.shape
    depth = norm_g.shape[0]
    n_mod = w_ada.shape[2] * N_DEV // d
    groups, states = s5_lam_re.shape[1], s5_lam_re.shape[2]
    per_sg = S5_SUPER // S5_GROUP
    nsg = groups // per_sg
    lw = lru_lam.shape[1] * N_DEV
    lwc = lw // (N_DEV // 2)
    half = N_DEV // 2

    wire = lambda a: a.astype(WIRE_DTYPE)
    g_s5_in = _all_gather("ag_s5_w_in", wire(s5_w_in[0])).reshape(d, d)
    g_s5_glu = _all_gather("ag_s5_w_glu", wire(s5_w_glu[0]))
    g_lru_in = _all_gather("ag_lru_w_in", wire(lru_w_in[0]))
    g_lru_out = _all_gather("ag_lru_w_out", wire(lru_w_out[0])).reshape(lw, d)
    g_gu = [_all_gather(f"ag_ffn_w_gu{i}", wire(ffn_w_gu[i])) for i in range(depth)]
    g_down = [_all_gather(f"ag_ffn_w_down{i}", wire(ffn_w_down[i])).reshape(-1, d) for i in range(depth)]
    sh_shapes = [wv[n].shape for n in SMALL_SHARDED] + [c.shape]
    sh_all = _all_gather("ag_small", _pack([wv[n] for n in SMALL_SHARDED] + [c], SUBLANE))
    sh_parts = [jnp.stack(p) for p in zip(*[_unpack(sh_all[s], sh_shapes) for s in range(N_DEV)])]
    full = {}
    for n, p in zip(SMALL_SHARDED, sh_parts[:-1]):
        full[n] = jnp.moveaxis(p, 0, -2).reshape(p.shape[1:-1] + (-1,))
    c_all = sh_parts[-1].reshape(N_DEV, d)
    c16 = jnp.pad(c_all, ((0, 2 * SUBLANE - N_DEV), (0, 0)))

    n_loc = w_ada.shape[2]
    b_loc = lax.dynamic_slice_in_dim(b_ada, me * n_loc, n_loc, axis=1)[:, None, :]
    mod_part = _ada_fwd("ada_fwd", c16, w_ada, b_loc)[:, :N_DEV]
    mod_mine = _chunk_exchange("x_mod", [mod_part.transpose(1, 0, 2)])
    mod = mod_mine.transpose(1, 0, 2).reshape(depth, n_mod, 1, d)

    lam3 = lambda a: a[0][:, None, :]
    p_lr, p_li, p_ld = lam3(s5_lam_re), lam3(s5_lam_im), s5_log_dt[0][:, None, None]
    p_br, p_bi = s5_b_re[0].transpose(0, 2, 1), s5_b_im[0].transpose(0, 2, 1)
    ab_re3, ab_im3, bb_re3, bb_im3 = _s5_disc("s5_disc", p_lr, p_li, p_ld, p_br, p_bi)
    ab_re, ab_im = ab_re3.reshape(nsg, per_sg * states), ab_im3.reshape(nsg, per_sg * states)
    bb_re, bb_im = _band(wire(bb_re3), per_sg), _band(wire(bb_im3), per_sg)
    cc_re = _band(wire(s5_c_re[0].transpose(0, 2, 1)), per_sg)
    cc_im = _band(wire(s5_c_im[0].transpose(0, 2, 1)), per_sg)

    taps = lru_conv_w.shape[1]
    cvec = jnp.concatenate([full['lru_conv_w'].reshape(taps, lw), full['lru_conv_b'], full['lru_b_rg'],
                            full['lru_b_ig'], full['lru_lam']], axis=0)
    cvec = cvec.reshape(taps + 4, half, lwc).transpose(1, 0, 2)
    wrg = _band(wire(lru_w_rg[0]), LRU_BLOCKS_PER_CHUNK)
    wig = _band(wire(lru_w_ig[0]), LRU_BLOCKS_PER_CHUNK)

    saved = []
    xc = x0
    for i in range(depth):
        sh1, sc1, g1, sh2, sc2, g2 = [mod[i, k] for k in range(n_mod)]
        gn = full['norm_g'][i]
        h1 = _norm_mod_fwd(f"norm1_fwd{i}", xc, gn[0:1], sc1, sh1)
        if i % 2 == 0:
            u = _mm_row(f"s5_in{i}", h1[None], g_s5_in)
            s_re, s_im, s_rem, s_imm = _s5_scan_fwd(f"s5_scan{i}", u, bb_re, bb_im, ab_re, ab_im)
            ypre, yact = _s5_out_fwd(f"s5_out{i}", s_rem, s_imm, cc_re, cc_im, u, s5_d)
            z = _mm_col(f"s5_glu{i}", yact, g_s5_glu)
            x1 = _glu_resid_fwd(f"s5_resid{i}", z, xc, g1)
            mix = (u, s_re, s_im, s_rem, s_imm, ypre, yact, z)
        else:
            zz = _mm_col(f"lru_in{i}", h1, g_lru_in)
            hs, ylru = _lru_fwd(f"lru_core{i}", zz, cvec, wrg, wig)
            o = _mm_row(f"lru_out{i}", ylru, g_lru_out)
            x1 = _resid(f"lru_resid{i}", xc, o, g1)
            mix = (zz, hs, ylru, o)
        h2 = _norm_mod_fwd(f"norm2_fwd{i}", x1, gn[1:2], sc2, sh2)
        gu = _mm_col(f"ffn_gu{i}", h2, g_gu[i])
        act = _swiglu_act_fwd(f"ffn_act{i}", gu)
        f = _mm_row(f"ffn_down{i}", act, g_down[i])
        x2 = _resid(f"ffn_resid{i}", x1, f, g2)
        saved.append((xc, h1, mix, x1, h2, gu, act, f))
        xc = x2

    dx, loss_part, d_final_g = _loss_bwd("loss", xc, tgt, final_g[None])
    loss = lax.psum(loss_part[0, 0], ("x", "y", "c"))

    grads = {}
    dmod = [None] * depth
    d_norm_g = [None] * depth
    d_gu, d_down = [None] * depth, [None] * depth
    for i in reversed(range(depth)):
        xin, h1, mix, x1, h2, gu, act, f = saved[i]
        sh1, sc1, g1, sh2, sc2, g2 = [mod[i, k] for k in range(n_mod)]
        gn = full['norm_g'][i]
        df, dg2 = _gate_bwd(f"ffn_gate_bwd{i}", dx, f, g2)
        dact = _mm_row_da(f"ffn_down_da{i}", df, g_down[i], half)
        d_down[i] = _mm_row_db(f"ffn_down_db{i}", act, df, WIRE_DTYPE)
        dgu = _swiglu_act_bwd(f"ffn_act_bwd{i}", gu, dact)
        dh2 = _mm_col_da(f"ffn_gu_da{i}", dgu, g_gu[i])
        d_gu[i] = _mm_col_db(f"ffn_gu_db{i}", h2, dgu, WIRE_DTYPE)
        dx, dgn2, dsc2, dsh2 = _norm_mod_bwd(f"norm2_bwd{i}", x1, dh2, dx, gn[1:2], sc2)
        if i % 2 == 0:
            u, s_re, s_im, s_rem, s_imm, ypre, yact, z = mix
            dz, dg1 = _glu_resid_bwd(f"s5_resid_bwd{i}", z, dx, g1)
            dyact = _mm_col_da(f"s5_glu_da{i}", dz, g_s5_glu)
            grads['s5_w_glu'] = _mm_col_db(f"s5_glu_db{i}", yact, dz, WIRE_DTYPE)
            dyp = _gelu_bwd(f"s5_gelu_bwd{i}", dyact, ypre)
            l_rem, l_imm, dab_re, dab_im = _s5_scan_bwd(f"s5_scan_bwd{i}", dyp, cc_re, cc_im, ab_re, ab_im, s_re, s_im)
            du, dbb_re, dbb_im, dcc_re, dcc_im, dd = _s5_grads(f"s5_grads{i}", l_rem, l_imm, s_rem, s_imm, u, dyp,
                                                                bb_re, bb_im, s5_d)
            dlr, dli, dld, dbr, dbi = _s5_disc_bwd(
                "s5_disc_bwd", p_lr, p_li, p_ld, p_br, p_bi, dab_re.reshape(groups, 1, states),
                dab_im.reshape(groups, 1, states), _unband(dbb_re, per_sg), _unband(dbb_im, per_sg))
            grads['s5_lam_re'], grads['s5_lam_im'], grads['s5_log_dt'] = dlr[:, 0][None], dli[:, 0][None], dld[:, 0, 0][None]
            grads['s5_b_re'], grads['s5_b_im'] = dbr.transpose(0, 2, 1)[None], dbi.transpose(0, 2, 1)[None]
            grads['s5_c_re'] = _unband(dcc_re, per_sg).transpose(0, 2, 1)[None]
            grads['s5_c_im'] = _unband(dcc_im, per_sg).transpose(0, 2, 1)[None]
            grads['s5_d'] = dd
            dub = du.astype(MXU_DTYPE)
            dh1 = _mm_row_da(f"s5_in_da{i}", dub, g_s5_in, 1)[0]
            grads['s5_w_in'] = _mm_row_db(f"s5_in_db{i}", h1[None], dub, WIRE_DTYPE)
        else:
            zz, hs, ylru, o = mix
            do, dg1 = _gate_bwd(f"lru_gate_bwd{i}", dx, o, g1)
            dyl = _mm_row_da(f"lru_out_da{i}", do, g_lru_out, half)
            grads['lru_w_out'] = _mm_row_db(f"lru_out_db{i}", ylru, do, WIRE_DTYPE)
            dgb, dxp, dcv, dwrg, dwig = _lru_bwd(f"lru_core_bwd{i}", zz, hs, dyl, cvec, wrg, wig)
            dzz = jnp.concatenate([dgb, dxp], axis=0)
            dh1 = _mm_col_da(f"lru_in_da{i}", dzz, g_lru_in)
            grads['lru_w_in'] = _mm_col_db(f"lru_in_db{i}", h1, dzz, WIRE_DTYPE)
            dcv = dcv.transpose(1, 0, 2).reshape(taps + 4, lw)
            grads['lru_conv_w'] = dcv[:taps].reshape(1, taps, 1, lw)
            grads['lru_conv_b'], grads['lru_b_rg'] = dcv[taps:taps + 1], dcv[taps + 1:taps + 2]
            grads['lru_b_ig'], grads['lru_lam'] = dcv[taps + 2:taps + 3], dcv[taps + 3:taps + 4]
            grads['lru_w_rg'] = _unband(dwrg, LRU_BLOCKS_PER_CHUNK)[None]
            grads['lru_w_ig'] = _unband(dwig, LRU_BLOCKS_PER_CHUNK)[None]
        dx, dgn1, dsc1, dsh1 = _norm_mod_bwd(f"norm1_bwd{i}", xin, dh1, dx, gn[0:1], sc1)
        dmod[i] = jnp.concatenate([dsh1, dsc1, dg1, dsh2, dsc2, dg2], axis=1)
        d_norm_g[i] = jnp.concatenate([dgn1, dgn2], axis=0)
    grad_x = dx[None]
    dmod = jnp.concatenate(dmod, axis=0)
    grads['norm_g'] = jnp.stack(d_norm_g)
    grads['b_ada'] = dmod
    grads['final_g'] = d_final_g[0]

    small_partial = _pack([grads[n] for n in SMALL], SUBLANE * N_DEV)
    rows8 = small_partial.shape[0] // N_DEV
    got = _chunk_exchange("x_small", [small_partial.reshape(N_DEV, rows8, LANE)])
    summed = _sum_parts("sum_small", got)
    small_total = _all_gather("ag_small_sum", summed).reshape(-1, LANE)
    small_grad = dict(zip(SMALL, _unpack(small_total, [grads[n].shape for n in SMALL])))
    for n in SMALL_SHARDED:
        shard = wv[n].shape[-1]
        small_grad[n] = lax.dynamic_slice_in_dim(small_grad[n], me * shard, shard, axis=small_grad[n].ndim - 1)
    small_shapes = [wv[n].shape for n in SMALL]
    pk = lambda dct: _pack([dct[n] for n in SMALL], 2 * SUBLANE, 8 * LANE)
    s_out = _adamw_sum("adamw_small", pk(small_grad)[None], pk(wv), pk(mv), pk(vv))
    out = {n: r for n, *r in zip(SMALL, *[_unpack(o, small_shapes) for o in s_out])}

    dmod_all = _all_gather("ag_dmod", dmod)
    dmod_loc = lax.dynamic_slice_in_dim(dmod_all, me * n_loc, n_loc, axis=2).transpose(1, 0, 2)
    dmod16 = jnp.pad(dmod_loc, ((0, 0), (0, 2 * SUBLANE - N_DEV), (0, 0)))
    out['w_ada'] = _adamw_w_ada("adamw_w_ada", c16, dmod16, w_ada, m_w_ada, v_w_ada)

    def big(name, partials):
        w = wv[name]
        rows, cols = w.shape[-2] * w.shape[0], w.shape[-1]
        parts = _chunk_exchange("x_" + name, [p.reshape(N_DEV, -1, cols) for p in partials])
        res = _adamw_sum("adamw_" + name, parts, w.reshape(rows, cols), mv[name].reshape(rows, cols),
                         vv[name].reshape(rows, cols))
        out[name] = [r.reshape(w.shape) for r in res]

    big('ffn_w_gu', d_gu)
    big('ffn_w_down', d_down)
    big('lru_w_in', [grads['lru_w_in']])
    big('lru_w_out', [grads['lru_w_out']])
    big('s5_w_glu', [grads['s5_w_glu']])
    big('s5_w_in', [grads['s5_w_in']])

    return (loss, grad_x, *[out[n][0] for n in WEIGHTS], *[out[n][1] for n in WEIGHTS],
            *[out[n][2] for n in WEIGHTS], *[out[n][3] for n in WEIGHTS])
```

```python
import functools
import math

import jax
import jax.numpy as jnp
from jax import lax
from jax.experimental import pallas as pl
from jax.experimental.pallas import tpu as pltpu

F32 = jnp.float32
MXU_DTYPE = jnp.bfloat16
WIRE_DTYPE = jnp.bfloat16
N_DEV = 8
EPS = 1e-6
LRU_C = 8.0
S5_GROUP = 16
S5_STATE = 64
S5_SUPER = 256
LRU_BLOCKS_PER_CHUNK = 4
ADAM_LR, ADAM_B1, ADAM_B2, ADAM_EPS, ADAM_WD, ADAM_STEP = 0.001, 0.9, 0.999, 1e-08, 0.01, 10
VMEM_LIMIT_BYTES = 56 * 1024 * 1024
LANE = 128
SUBLANE = 8

WEIGHTS = ['norm_g', 'w_ada', 'b_ada', 's5_w_in', 's5_lam_re', 's5_lam_im', 's5_log_dt', 's5_b_re', 's5_b_im',
           's5_c_re', 's5_c_im', 's5_d', 's5_w_glu', 'lru_w_in', 'lru_conv_w', 'lru_conv_b', 'lru_w_rg', 'lru_b_rg',
           'lru_w_ig', 'lru_b_ig', 'lru_lam', 'lru_w_out', 'ffn_w_gu', 'ffn_w_down', 'final_g']
BIG = ('w_ada', 's5_w_in', 's5_w_glu', 'lru_w_in', 'lru_w_out', 'ffn_w_gu', 'ffn_w_down')
SMALL = tuple(n for n in WEIGHTS if n not in BIG)
SMALL_SHARDED = ('norm_g', 'lru_conv_w', 'lru_conv_b', 'lru_b_rg', 'lru_b_ig', 'lru_lam')

NN = (((1,), (0,)), ((), ()))
NT = (((1,), (1,)), ((), ()))
TN = (((0,), (0,)), ((), ()))


def _params(n_grid):
    return pltpu.CompilerParams(dimension_semantics=("arbitrary",) * n_grid, vmem_limit_bytes=VMEM_LIMIT_BYTES)


def _tile(dim, pref, align=LANE):
    if dim <= pref:
        return dim
    t = (pref // align) * align
    while t >= align:
        if dim % t == 0:
            return t
        t -= align
    return dim


def _dot(a, b, dims):
    return lax.dot_general(a.astype(MXU_DTYPE), b.astype(MXU_DTYPE), dims, preferred_element_type=F32)


def _gelu(x):
    k = math.sqrt(2.0 / math.pi)
    return 0.5 * x * (1.0 + jnp.tanh(k * (x + 0.044715 * (x * x * x))))


def _gelu_and_grad(x):
    k = math.sqrt(2.0 / math.pi)
    th = jnp.tanh(k * (x + 0.044715 * (x * x * x)))
    g = 0.5 * x * (1.0 + th)
    dg = 0.5 * (1.0 + th) + 0.5 * x * (1.0 - th * th) * (k * (1.0 + 3.0 * 0.044715 * (x * x)))
    return g, dg


def _neg_expm1(x):
    series = -x * (1.0 + x * (0.5 + x * (1.0 / 6.0 + x * (1.0 / 24.0 + x * (1.0 / 120.0)))))
    return jnp.where(x > -0.01, series, 1.0 - jnp.exp(x))


MESH = pl.DeviceIdType.MESH
N_CHIP = N_DEV // 2
ALL, SAME_CORE = 7, 6


def _place():
    x, y, c = lax.axis_index("x"), lax.axis_index("y"), lax.axis_index("c")
    return x, y, c


def _flip(place, k):
    x, y, c = place
    return (1 - x if (k >> 2) & 1 else x, 1 - y if (k >> 1) & 1 else y, 1 - c if k & 1 else c)


def _chunk_exchange(name, xs, group):
    n = len(xs)
    members, r, c_ = xs[0].shape
    assert members == {ALL: N_DEV, SAME_CORE: N_CHIP}[group]
    assert all(a.shape == xs[0].shape and a.dtype == xs[0].dtype for a in xs)
    ks = [k for k in range(1, N_DEV) if not k & ~group]
    member = (lambda p: 4 * p[0] + 2 * p[1] + p[2]) if group == ALL else (lambda p: 2 * p[0] + p[1])

    def body(*refs):
        ins, out = refs[:n], refs[n]
        send_sems, recv_sems, local_sems = refs[n + 1:]
        place = _place()
        me = member(place)
        local = [pltpu.make_async_copy(ins[l].at[me], out.at[me, l], local_sems.at[l]) for l in range(n)]
        for cp in local:
            cp.start()
        sent = []
        for l in range(n):
            for k in ks:
                pid = _flip(place, k)
                peer = member(pid)

                def copy(land_at, l=l, k=k, peer=peer, pid=pid):
                    return pltpu.make_async_remote_copy(
                        src_ref=ins[l].at[peer], dst_ref=out.at[land_at, l], send_sem=send_sems.at[l * N_DEV + k],
                        recv_sem=recv_sems.at[l * N_DEV + k], device_id=pid, device_id_type=MESH)

                copy(me).start()
                sent.append((copy, peer))
        for copy, peer in sent:
            copy(me).wait_send()
            copy(peer).wait_recv()
        for cp in local:
            cp.wait()

    hbm = pl.BlockSpec(memory_space=pltpu.HBM)
    got = pl.pallas_call(
        body, name=name, out_shape=jax.ShapeDtypeStruct((members, n, r, c_), xs[0].dtype),
        in_specs=[hbm] * n, out_specs=hbm,
        scratch_shapes=[pltpu.SemaphoreType.DMA((n * N_DEV,)), pltpu.SemaphoreType.DMA((n * N_DEV,)),
                        pltpu.SemaphoreType.DMA((n,))],
    )(*xs)
    return got.reshape(members, n * r, c_)


def _all_gather(name, x):
    def body(x_ref, out_ref, send_sems, recv_sems, local_sem):
        place = _place()
        xx, yy, c = place
        chip = 2 * xx + yy
        chip_flips = (2, 4, 6)
        slot = lambda p: out_ref.at[2 * p[0] + p[1], p[2]]

        def copy(k, block, to, src=None):
            return pltpu.make_async_remote_copy(
                src_ref=slot(block) if src is None else src, dst_ref=slot(block), send_sem=send_sems.at[k],
                recv_sem=recv_sems.at[k], device_id=to, device_id_type=MESH)

        sibling = _flip(place, 1)
        mine = pltpu.make_async_copy(x_ref, slot(place), local_sem)
        mine.start()
        first = [copy(0, place, sibling, src=x_ref)]
        first += [copy(1 + j, place, _flip(place, k), src=x_ref) for j, k in enumerate(chip_flips)]
        for cp in first:
            cp.start()
        passed = [copy(4 + j, _flip(place, k), sibling) for j, k in enumerate(chip_flips)]
        for j, k in enumerate(chip_flips):
            copy(1 + j, _flip(place, k), place).wait_recv()
            passed[j].start()
        copy(0, sibling, place).wait_recv()
        for j, k in enumerate(chip_flips):
            copy(4 + j, _flip(sibling, k), place).wait_recv()
        for cp in first + passed:
            cp.wait_send()
        mine.wait()

    hbm = pl.BlockSpec(memory_space=pltpu.HBM)
    out = pl.pallas_call(
        body, name=name, out_shape=jax.ShapeDtypeStruct((N_CHIP, 2) + x.shape, x.dtype), in_specs=[hbm], out_specs=hbm,
        scratch_shapes=[pltpu.SemaphoreType.DMA((7,)), pltpu.SemaphoreType.DMA((7,)), pltpu.SemaphoreType.DMA],
    )(x)
    return out.reshape((N_DEV,) + x.shape)


def _sibling_exchange(name, xs):
    n = len(xs)
    _, r, c_ = xs[0].shape
    assert all(a.shape == xs[0].shape and a.dtype == xs[0].dtype for a in xs)

    def body(*refs):
        ins, out = refs[:n], refs[n]
        send_sems, recv_sems = refs[n + 1:]
        place = _place()
        c = place[2]
        sibling = _flip(place, 1)
        copies = []
        for l in range(n):
            for chip in range(N_CHIP):
                cp = pltpu.make_async_remote_copy(
                    src_ref=ins[l].at[2 * chip + (1 - c)], dst_ref=out.at[chip, l], send_sem=send_sems.at[l * N_CHIP + chip],
                    recv_sem=recv_sems.at[l * N_CHIP + chip], device_id=sibling, device_id_type=MESH)
                cp.start()
                copies.append(cp)
        for cp in copies:
            cp.wait()

    hbm = pl.BlockSpec(memory_space=pltpu.HBM)
    return pl.pallas_call(
        body, name=name, out_shape=jax.ShapeDtypeStruct((N_CHIP, n, r, c_), xs[0].dtype), in_specs=[hbm] * n, out_specs=hbm,
        scratch_shapes=[pltpu.SemaphoreType.DMA((n * N_CHIP,)), pltpu.SemaphoreType.DMA((n * N_CHIP,))],
    )(*xs)


def _pair_sum(name, x, got, l, core):
    _, r, c_ = x.shape
    br = _tile(r, 256, 2 * SUBLANE)

    def body(core_ref, x_ref, g_ref, o_ref):
        o_ref[...] = (x_ref[...].astype(F32) + g_ref[...].astype(F32)).astype(o_ref.dtype)

    return pl.pallas_call(
        body, name=name, out_shape=jax.ShapeDtypeStruct((N_CHIP, r, c_), x.dtype),
        grid_spec=pltpu.PrefetchScalarGridSpec(
            num_scalar_prefetch=1, grid=(N_CHIP, r // br),
            in_specs=[pl.BlockSpec((None, br, c_), lambda ch, i, core_ref: (2 * ch + core_ref[0], i, 0)),
                      pl.BlockSpec((None, None, br, c_), lambda ch, i, core_ref: (ch, l, i, 0))],
            out_specs=pl.BlockSpec((None, br, c_), lambda ch, i, core_ref: (ch, i, 0))),
        compiler_params=_params(2))(core, x, got)


def _grad_exchange(name, xs):
    got = _sibling_exchange(name + "_d2d", xs)
    core = lax.axis_index("c").astype(jnp.int32).reshape(1)
    chip_sums = [_pair_sum(f"{name}_pair{l}", x, got, l, core) for l, x in enumerate(xs)]
    return _chunk_exchange(name + "_ici", chip_sums, SAME_CORE)


def _mm(name, a, b, out_shape, out_dtype, grid, a_spec, b_spec, o_spec, dims, n_red, acc_shape):
    red = tuple(range(len(grid) - n_red, len(grid)))

    def body(a_ref, b_ref, o_ref, acc_ref):
        first = functools.reduce(jnp.logical_and, [pl.program_id(ax) == 0 for ax in red])
        last = functools.reduce(jnp.logical_and, [pl.program_id(ax) == grid[ax] - 1 for ax in red])

        @pl.when(first)
        def _():
            acc_ref[...] = jnp.zeros_like(acc_ref)

        acc_ref[...] += _dot(a_ref[...], b_ref[...], dims)

        @pl.when(last)
        def _():
            o_ref[...] = acc_ref[...].astype(o_ref.dtype)

    return pl.pallas_call(
        body, name=name, out_shape=jax.ShapeDtypeStruct(out_shape, out_dtype), grid=grid,
        in_specs=[a_spec, b_spec], out_specs=o_spec, scratch_shapes=[pltpu.VMEM(acc_shape, F32)],
        compiler_params=_params(len(grid)))(a, b)


def _mm_col(name, a, b, out_dtype=F32):
    m, k = a.shape
    j, _, n = b.shape
    bm, bk = _tile(m, 1024), _tile(k, 512)
    return _mm(name, a, b, (j, m, n), out_dtype, (j, m // bm, k // bk),
               pl.BlockSpec((bm, bk), lambda jj, mm, kk: (mm, kk)),
               pl.BlockSpec((None, bk, n), lambda jj, mm, kk: (jj, kk, 0)),
               pl.BlockSpec((None, bm, n), lambda jj, mm, kk: (jj, mm, 0)), NN, 1, (bm, n))


def _mm_col_da(name, do, b):
    j, m, n = do.shape
    k = b.shape[1]
    bm, bk = _tile(m, 1024), _tile(k, 1024)
    return _mm(name, do, b, (m, k), F32, (m // bm, k // bk, j),
               pl.BlockSpec((None, bm, n), lambda mm, kk, jj: (jj, mm, 0)),
               pl.BlockSpec((None, bk, n), lambda mm, kk, jj: (jj, kk, 0)),
               pl.BlockSpec((bm, bk), lambda mm, kk, jj: (mm, kk)), NT, 1, (bm, bk))


def _mm_col_db(name, a, do, out_dtype):
    m, k = a.shape
    j, _, n = do.shape
    bm, bk = _tile(m, 1024), _tile(k, 512)
    return _mm(name, a, do, (j, k, n), out_dtype, (j, k // bk, m // bm),
               pl.BlockSpec((bm, bk), lambda jj, kk, mm: (mm, kk)),
               pl.BlockSpec((None, bm, n), lambda jj, kk, mm: (jj, mm, 0)),
               pl.BlockSpec((None, bk, n), lambda jj, kk, mm: (jj, kk, 0)), TN, 1, (bk, n))


def _row_bk(kq):
    return kq if (kq % LANE or kq // LANE in (11,)) else _tile(kq, 512)


def _mm_row(name, a, b, out_dtype=F32):
    q, m, kq = a.shape
    n = b.shape[1]
    bm, bn, bk = _tile(m, 1024), _tile(n, 1024), _row_bk(kq)
    nk = kq // bk
    return _mm(name, a, b, (m, n), out_dtype, (m // bm, n // bn, q, nk),
               pl.BlockSpec((None, bm, bk), lambda mm, nn, qq, kk: (qq, mm, kk)),
               pl.BlockSpec((bk, bn), lambda mm, nn, qq, kk: (qq * nk + kk, nn)),
               pl.BlockSpec((bm, bn), lambda mm, nn, qq, kk: (mm, nn)), NN, 2, (bm, bn))


def _mm_row_da(name, do, b, q):
    m, n = do.shape
    kq = b.shape[0] // q
    bm, bn = _tile(m, 1024), _tile(n, 1024)
    return _mm(name, do, b, (q, m, kq), F32, (q, m // bm, n // bn),
               pl.BlockSpec((bm, bn), lambda qq, mm, nn: (mm, nn)),
               pl.BlockSpec((kq, bn), lambda qq, mm, nn: (qq, nn)),
               pl.BlockSpec((None, bm, kq), lambda qq, mm, nn: (qq, mm, 0)), NT, 1, (bm, kq))


def _mm_row_db(name, a, do, out_dtype):
    q, m, kq = a.shape
    n = do.shape[1]
    bm, bn = _tile(m, 1024), _tile(n, 512)
    return _mm(name, a, do, (q * kq, n), out_dtype, (q, n // bn, m // bm),
               pl.BlockSpec((None, bm, kq), lambda qq, nn, mm: (qq, mm, 0)),
               pl.BlockSpec((bm, bn), lambda qq, nn, mm: (mm, nn)),
               pl.BlockSpec((kq, bn), lambda qq, nn, mm: (qq, nn)), TN, 1, (kq, bn))


def _row_spec(bm, d):
    return pl.BlockSpec((bm, d), lambda i: (i, 0))


def _vec_spec(d):
    return pl.BlockSpec((1, d), lambda i: (0, 0))


def _norm_mod_fwd(name, x, gain, sc, sh):
    t, d = x.shape
    bm = _tile(t, 256, SUBLANE)

    def body(x_ref, g_ref, sc_ref, sh_ref, h_ref):
        xv = x_ref[...]
        rstd = lax.rsqrt(jnp.mean(xv * xv, axis=-1, keepdims=True) + EPS)
        h_ref[...] = ((xv * rstd) * g_ref[...] * (1.0 + sc_ref[...]) + sh_ref[...]).astype(h_ref.dtype)

    return pl.pallas_call(
        body, name=name, out_shape=jax.ShapeDtypeStruct((t, d), MXU_DTYPE), grid=(t // bm,),
        in_specs=[_row_spec(bm, d), _vec_spec(d), _vec_spec(d), _vec_spec(d)], out_specs=_row_spec(bm, d),
        compiler_params=_params(1))(x, gain, sc, sh)


def _norm_mod_bwd(name, x, dh, dres, gain, sc):
    t, d = x.shape
    bm = _tile(t, 256, SUBLANE)

    def body(x_ref, dh_ref, dres_ref, g_ref, sc_ref, dx_ref, dg_ref, dsc_ref, dsh_ref):
        @pl.when(pl.program_id(0) == 0)
        def _():
            dg_ref[...] = jnp.zeros_like(dg_ref)
            dsc_ref[...] = jnp.zeros_like(dsc_ref)
            dsh_ref[...] = jnp.zeros_like(dsh_ref)

        xv, dh_ = x_ref[...], dh_ref[...]
        rstd = lax.rsqrt(jnp.mean(xv * xv, axis=-1, keepdims=True) + EPS)
        nrm = xv * rstd
        gain_ = g_ref[...]
        dsh_ref[...] += jnp.sum(dh_, axis=0, keepdims=True)
        dsc_ref[...] += jnp.sum(dh_ * (nrm * gain_), axis=0, keepdims=True)
        dhn = dh_ * (1.0 + sc_ref[...])
        dg_ref[...] += jnp.sum(dhn * nrm, axis=0, keepdims=True)
        dn = dhn * gain_
        dx_ref[...] = dres_ref[...] + rstd * (dn - nrm * jnp.mean(dn * nrm, axis=-1, keepdims=True))

    vec = jax.ShapeDtypeStruct((1, d), F32)
    return pl.pallas_call(
        body, name=name, out_shape=[jax.ShapeDtypeStruct((t, d), F32), vec, vec, vec], grid=(t // bm,),
        in_specs=[_row_spec(bm, d), _row_spec(bm, d), _row_spec(bm, d), _vec_spec(d), _vec_spec(d)],
        out_specs=[_row_spec(bm, d), _vec_spec(d), _vec_spec(d), _vec_spec(d)],
        compiler_params=_params(1))(x, dh, dres, gain, sc)


def _loss_bwd(name, x, target, gain):
    t, d = x.shape
    bm = _tile(t, 256, SUBLANE)

    def body(x_ref, t_ref, g_ref, dx_ref, loss_ref, dg_ref):
        @pl.when(pl.program_id(0) == 0)
        def _():
            loss_ref[...] = jnp.zeros_like(loss_ref)
            dg_ref[...] = jnp.zeros_like(dg_ref)

        xv = x_ref[...]
        rstd = lax.rsqrt(jnp.mean(xv * xv, axis=-1, keepdims=True) + EPS)
        nrm = xv * rstd
        gain_ = g_ref[...]
        err = nrm * gain_ - t_ref[...]
        per_tok = jnp.mean(err * err, axis=-1, keepdims=True)
        loss_ref[...] += 0.5 * jnp.sum(per_tok, axis=0, keepdims=True)
        dout = err * (1.0 / d)
        dg_ref[...] += jnp.sum(dout * nrm, axis=0, keepdims=True)
        dn = dout * gain_
        dx_ref[...] = rstd * (dn - nrm * jnp.mean(dn * nrm, axis=-1, keepdims=True))

    return pl.pallas_call(
        body, name=name,
        out_shape=[jax.ShapeDtypeStruct((t, d), F32), jax.ShapeDtypeStruct((1, 1), F32),
                   jax.ShapeDtypeStruct((1, d), F32)],
        grid=(t // bm,), in_specs=[_row_spec(bm, d), _row_spec(bm, d), _vec_spec(d)],
        out_specs=[_row_spec(bm, d), pl.BlockSpec((1, 1), lambda i: (0, 0)), _vec_spec(d)],
        compiler_params=_params(1))(x, target, gain)


def _resid(name, x, y, g):
    t, d = x.shape
    bm = _tile(t, 256, SUBLANE)

    def body(x_ref, y_ref, g_ref, o_ref):
        o_ref[...] = x_ref[...] + g_ref[...] * y_ref[...]

    return pl.pallas_call(
        body, name=name, out_shape=jax.ShapeDtypeStruct((t, d), F32), grid=(t // bm,),
        in_specs=[_row_spec(bm, d), _row_spec(bm, d), _vec_spec(d)], out_specs=_row_spec(bm, d),
        compiler_params=_params(1))(x, y, g)


def _gate_bwd(name, dx, y, g):
    t, d = dx.shape
    bm = _tile(t, 256, SUBLANE)

    def body(dx_ref, y_ref, g_ref, dy_ref, dg_ref):
        @pl.when(pl.program_id(0) == 0)
        def _():
            dg_ref[...] = jnp.zeros_like(dg_ref)

        dxv = dx_ref[...]
        dy_ref[...] = (g_ref[...] * dxv).astype(dy_ref.dtype)
        dg_ref[...] += jnp.sum(dxv * y_ref[...], axis=0, keepdims=True)

    return pl.pallas_call(
        body, name=name, out_shape=[jax.ShapeDtypeStruct((t, d), MXU_DTYPE), jax.ShapeDtypeStruct((1, d), F32)],
        grid=(t // bm,), in_specs=[_row_spec(bm, d), _row_spec(bm, d), _vec_spec(d)],
        out_specs=[_row_spec(bm, d), _vec_spec(d)], compiler_params=_params(1))(dx, y, g)


def _glu_resid_fwd(name, z, x, g):
    _, t, n = z.shape
    d = x.shape[1]
    half = N_DEV // 2
    bm = _tile(t, 256, SUBLANE)

    def body(v_ref, gt_ref, x_ref, g_ref, o_ref):
        o_ref[...] = x_ref[...] + g_ref[...] * (v_ref[...] * jax.nn.sigmoid(gt_ref[...]))

    return pl.pallas_call(
        body, name=name, out_shape=jax.ShapeDtypeStruct((t, d), F32), grid=(half, t // bm),
        in_specs=[pl.BlockSpec((None, bm, n), lambda q, i: (q, i, 0)),
                  pl.BlockSpec((None, bm, n), lambda q, i: (q + half, i, 0)),
                  pl.BlockSpec((bm, n), lambda q, i: (i, q)), pl.BlockSpec((1, n), lambda q, i: (0, q))],
        out_specs=pl.BlockSpec((bm, n), lambda q, i: (i, q)), compiler_params=_params(2))(z, z, x, g)


def _glu_resid_bwd(name, z, dx, g):
    _, t, n = z.shape
    d = dx.shape[1]
    half = N_DEV // 2
    bm = _tile(t, 256, SUBLANE)

    def body(v_ref, gt_ref, dx_ref, g_ref, dz_ref, dg_ref):
        part, i = pl.program_id(1), pl.program_id(2)
        v, dxv = v_ref[...], dx_ref[...]
        sig = jax.nn.sigmoid(gt_ref[...])
        dout = g_ref[...] * dxv

        @pl.when(part == 0)
        def _():
            @pl.when(i == 0)
            def _():
                dg_ref[...] = jnp.zeros_like(dg_ref)

            dg_ref[...] += jnp.sum(dxv * (v * sig), axis=0, keepdims=True)
            dz_ref[...] = (dout * sig).astype(dz_ref.dtype)

        @pl.when(part == 1)
        def _():
            dz_ref[...] = (dout * v * (sig * (1.0 - sig))).astype(dz_ref.dtype)

    return pl.pallas_call(
        body, name=name,
        out_shape=[jax.ShapeDtypeStruct((N_DEV, t, n), MXU_DTYPE), jax.ShapeDtypeStruct((1, d), F32)],
        grid=(half, 2, t // bm),
        in_specs=[pl.BlockSpec((None, bm, n), lambda q, p, i: (q, i, 0)),
                  pl.BlockSpec((None, bm, n), lambda q, p, i: (q + half, i, 0)),
                  pl.BlockSpec((bm, n), lambda q, p, i: (i, q)), pl.BlockSpec((1, n), lambda q, p, i: (0, q))],
        out_specs=[pl.BlockSpec((None, bm, n), lambda q, p, i: (q + half * p, i, 0)),
                   pl.BlockSpec((1, n), lambda q, p, i: (0, q))],
        compiler_params=_params(3))(z, z, dx, g)


def _swiglu_act_fwd(name, gu):
    _, t, n = gu.shape
    half = N_DEV // 2
    bm = _tile(t, 256, SUBLANE)

    def body(g_ref, u_ref, o_ref):
        gv = g_ref[...]
        o_ref[...] = (gv * jax.nn.sigmoid(gv) * u_ref[...]).astype(o_ref.dtype)

    return pl.pallas_call(
        body, name=name, out_shape=jax.ShapeDtypeStruct((half, t, n), MXU_DTYPE), grid=(half, t // bm),
        in_specs=[pl.BlockSpec((None, bm, n), lambda q, i: (q, i, 0)),
                  pl.BlockSpec((None, bm, n), lambda q, i: (q + half, i, 0))],
        out_specs=pl.BlockSpec((None, bm, n), lambda q, i: (q, i, 0)), compiler_params=_params(2))(gu, gu)


def _swiglu_act_bwd(name, gu, dact):
    _, t, n = gu.shape
    half = N_DEV // 2
    bm = _tile(t, 256, SUBLANE)

    def body(g_ref, u_ref, da_ref, o_ref):
        part = pl.program_id(1)
        gv, da = g_ref[...], da_ref[...]
        sig = jax.nn.sigmoid(gv)

        @pl.when(part == 0)
        def _():
            o_ref[...] = (da * u_ref[...] * (sig * (1.0 + gv * (1.0 - sig)))).astype(o_ref.dtype)

        @pl.when(part == 1)
        def _():
            o_ref[...] = (da * (gv * sig)).astype(o_ref.dtype)

    return pl.pallas_call(
        body, name=name, out_shape=jax.ShapeDtypeStruct((N_DEV, t, n), MXU_DTYPE), grid=(half, 2, t // bm),
        in_specs=[pl.BlockSpec((None, bm, n), lambda q, p, i: (q, i, 0)),
                  pl.BlockSpec((None, bm, n), lambda q, p, i: (q + half, i, 0)),
                  pl.BlockSpec((None, bm, n), lambda q, p, i: (q, i, 0))],
        out_specs=pl.BlockSpec((None, bm, n), lambda q, p, i: (q + half * p, i, 0)),
        compiler_params=_params(3))(gu, gu, dact)


def _ada_fwd(name, c16, w_ada, b_loc):
    nl, d, n = w_ada.shape
    bn = _tile(n, 512)

    def body(c_ref, w_ref, b_ref, o_ref):
        cv = c_ref[...]
        o_ref[...] = _dot(cv * jax.nn.sigmoid(cv), w_ref[...], NN) + b_ref[...]

    return pl.pallas_call(
        body, name=name, out_shape=jax.ShapeDtypeStruct((nl, c16.shape[0], n), F32), grid=(nl, n // bn),
        in_specs=[pl.BlockSpec(c16.shape, lambda i, j: (0, 0)), pl.BlockSpec((None, d, bn), lambda i, j: (i, 0, j)),
                  pl.BlockSpec((None, 1, bn), lambda i, j: (i, 0, j))],
        out_specs=pl.BlockSpec((None, c16.shape[0], bn), lambda i, j: (i, 0, j)),
        compiler_params=_params(2))(c16, w_ada, b_loc)


def _adam_update(g, w, m, v):
    m = ADAM_B1 * m + (1.0 - ADAM_B1) * g
    v = ADAM_B2 * v + (1.0 - ADAM_B2) * (g * g)
    m_hat = m / (1.0 - ADAM_B1 ** ADAM_STEP)
    v_hat = v / (1.0 - ADAM_B2 ** ADAM_STEP)
    delta = -ADAM_LR * (m_hat / (jnp.sqrt(v_hat) + ADAM_EPS) + ADAM_WD * w)
    return delta, m, v


def _adamw_w_ada(name, c16, dmod16, w, m, v):
    nl, d, n = w.shape
    br = _tile(d, 256)

    def body(c_ref, dm_ref, w_ref, m_ref, v_ref, g_ref, dl_ref, mo_ref, vo_ref):
        cv = c_ref[...]
        g = _dot(cv * jax.nn.sigmoid(cv), dm_ref[...], TN)
        g_ref[...] = g
        dl_ref[...], mo_ref[...], vo_ref[...] = _adam_update(g, w_ref[...], m_ref[...], v_ref[...])

    blk = pl.BlockSpec((None, br, n), lambda i, r: (i, r, 0))
    shp = jax.ShapeDtypeStruct(w.shape, F32)
    return pl.pallas_call(
        body, name=name, out_shape=[shp] * 4, grid=(nl, d // br),
        in_specs=[pl.BlockSpec((c16.shape[0], br), lambda i, r: (0, r)),
                  pl.BlockSpec((None, dmod16.shape[1], n), lambda i, r: (i, 0, 0)), blk, blk, blk],
        out_specs=[blk] * 4, compiler_params=_params(2))(c16, dmod16, w, m, v)


def _adamw_sum(name, parts, w, m, v):
    p, r, c = parts.shape
    br = _tile(r, 128, 2 * SUBLANE)

    def body(p_ref, w_ref, m_ref, v_ref, g_ref, dl_ref, mo_ref, vo_ref):
        g = p_ref[0].astype(F32)
        for s in range(1, p):
            g = g + p_ref[s].astype(F32)
        g_ref[...] = g
        dl_ref[...], mo_ref[...], vo_ref[...] = _adam_update(g, w_ref[...], m_ref[...], v_ref[...])

    blk = pl.BlockSpec((br, c), lambda i: (i, 0))
    shp = jax.ShapeDtypeStruct((r, c), F32)
    return pl.pallas_call(
        body, name=name, out_shape=[shp] * 4, grid=(r // br,),
        in_specs=[pl.BlockSpec((p, br, c), lambda i: (0, i, 0)), blk, blk, blk], out_specs=[blk] * 4,
        compiler_params=_params(1))(parts, w, m, v)


def _sum_parts(name, parts):
    p, r, c = parts.shape

    def body(p_ref, o_ref):
        g = p_ref[0]
        for s in range(1, p):
            g = g + p_ref[s]
        o_ref[...] = g

    return pl.pallas_call(body, name=name, out_shape=jax.ShapeDtypeStruct((r, c), F32))(parts)


def _s5_disc(name, lam_re, lam_im, log_dt, b_re, b_im):
    def body(lr_ref, li_ref, ld_ref, br_ref, bi_ref, ar_ref, ai_ref, bbr_ref, bbi_ref):
        lr, li = lr_ref[...], li_ref[...]
        dt = jnp.exp(ld_ref[...])
        mag = jnp.exp(lr * dt)
        a_re, a_im = mag * jnp.cos(li * dt), mag * jnp.sin(li * dt)
        nr, ni = a_re - 1.0, a_im
        den = lr * lr + li * li
        f_re, f_im = (nr * lr + ni * li) / den, (ni * lr - nr * li) / den
        br, bi = br_ref[...], bi_ref[...]
        ar_ref[...], ai_ref[...] = a_re, a_im
        bbr_ref[...] = f_re * br - f_im * bi
        bbi_ref[...] = f_re * bi + f_im * br

    s_a, s_b = jax.ShapeDtypeStruct(lam_re.shape, F32), jax.ShapeDtypeStruct(b_re.shape, F32)
    return pl.pallas_call(body, name=name, out_shape=[s_a, s_a, s_b, s_b])(lam_re, lam_im, log_dt, b_re, b_im)


def _s5_disc_bwd(name, lam_re, lam_im, log_dt, b_re, b_im, dab_re, dab_im, dbb_re, dbb_im):
    def body(lr_ref, li_ref, ld_ref, br_ref, bi_ref, dar_ref, dai_ref, dbbr_ref, dbbi_ref,
             dlr_ref, dli_ref, dld_ref, dbr_ref, dbi_ref):
        lr, li = lr_ref[...], li_ref[...]
        dt = jnp.exp(ld_ref[...])
        mag = jnp.exp(lr * dt)
        a_re, a_im = mag * jnp.cos(li * dt), mag * jnp.sin(li * dt)
        nr, ni = a_re - 1.0, a_im
        den = lr * lr + li * li
        f_re, f_im = (nr * lr + ni * li) / den, (ni * lr - nr * li) / den
        br, bi = br_ref[...], bi_ref[...]
        dbbr, dbbi = dbbr_ref[...], dbbi_ref[...]
        dbr_ref[...] = f_re * dbbr + f_im * dbbi
        dbi_ref[...] = f_re * dbbi - f_im * dbbr
        df_re = jnp.sum(dbbr * br + dbbi * bi, axis=1, keepdims=True)
        df_im = jnp.sum(dbbi * br - dbbr * bi, axis=1, keepdims=True)
        dnr = (df_re * lr - df_im * li) / den
        dni = (df_re * li + df_im * lr) / den
        dden = -(df_re * f_re + df_im * f_im) / den
        dlr = (df_re * nr + df_im * ni) / den + 2.0 * lr * dden
        dli = (df_re * ni - df_im * nr) / den + 2.0 * li * dden
        da_re, da_im = dar_ref[...] + dnr, dai_ref[...] + dni
        dmag_mag = da_re * a_re + da_im * a_im
        dth = da_im * a_re - da_re * a_im
        dlr_ref[...] = dlr + dmag_mag * dt
        dli_ref[...] = dli + dth * dt
        ddt = jnp.sum(dmag_mag * lr + dth * li, axis=2, keepdims=True)
        dld_ref[...] = ddt * dt

    s_a, s_b = jax.ShapeDtypeStruct(lam_re.shape, F32), jax.ShapeDtypeStruct(b_re.shape, F32)
    return pl.pallas_call(
        body, name=name, out_shape=[s_a, s_a, jax.ShapeDtypeStruct(log_dt.shape, F32), s_b, s_b],
    )(lam_re, lam_im, log_dt, b_re, b_im, dab_re, dab_im, dbb_re, dbb_im)


def _s5_time_block(t):
    return _tile(t, 128, SUBLANE)


def _s5_scan_fwd(name, u, bb_re, bb_im, ab_re, ab_im):
    t, d = u.shape
    nsg, cs, ns = bb_re.shape
    tb = _s5_time_block(t)

    def body(u_ref, bbr_hbm, bbi_hbm, ar_ref, ai_ref, sr_ref, si_ref, srm_ref, sim_ref, bbr, bbi, cr_ref, ci_ref):
        @pl.when(pl.program_id(0) == 0)
        def _():
            pltpu.sync_copy(bbr_hbm, bbr)
            pltpu.sync_copy(bbi_hbm, bbi)
            cr_ref[...] = jnp.zeros_like(cr_ref)
            ci_ref[...] = jnp.zeros_like(ci_ref)

        for sg in range(nsg):
            us = u_ref[:, sg * cs:(sg + 1) * cs]
            sr_ref[:, sg, :] = _dot(us, bbr[sg], NN)
            si_ref[:, sg, :] = _dot(us, bbi[sg], NN)
        ar, ai = ar_ref[...], ai_ref[...]

        def step(i, carry):
            cr, ci = carry
            nr = ar * cr - ai * ci + sr_ref[i]
            ni = ar * ci + ai * cr + si_ref[i]
            sr_ref[i] = nr
            si_ref[i] = ni
            return nr, ni

        cr, ci = lax.fori_loop(0, tb, step, (cr_ref[...], ci_ref[...]), unroll=2)
        cr_ref[...], ci_ref[...] = cr, ci
        for sg in range(nsg):
            srm_ref[sg] = sr_ref[:, sg, :].astype(MXU_DTYPE)
            sim_ref[sg] = si_ref[:, sg, :].astype(MXU_DTYPE)

    scan = jax.ShapeDtypeStruct((t, nsg, ns), F32)
    mxu = jax.ShapeDtypeStruct((nsg, t, ns), MXU_DTYPE)
    hbm = pl.BlockSpec(memory_space=pltpu.HBM)
    full = pl.BlockSpec((nsg, ns), lambda i: (0, 0))
    return pl.pallas_call(
        body, name=name, out_shape=[scan, scan, mxu, mxu], grid=(t // tb,),
        in_specs=[_row_spec(tb, d), hbm, hbm, full, full],
        out_specs=[pl.BlockSpec((tb, nsg, ns), lambda i: (i, 0, 0))] * 2 + [pl.BlockSpec((nsg, tb, ns), lambda i: (0, i, 0))] * 2,
        scratch_shapes=[pltpu.VMEM(bb_re.shape, bb_re.dtype), pltpu.VMEM(bb_im.shape, bb_im.dtype),
                        pltpu.VMEM((nsg, ns), F32), pltpu.VMEM((nsg, ns), F32)],
        compiler_params=_params(1))(u, bb_re, bb_im, ab_re, ab_im)


def _s5_out_fwd(name, s_re, s_im, cc_re, cc_im, u, dskip):
    nsg, t, ns = s_re.shape
    d = u.shape[1]
    cs = cc_re.shape[2]
    tb = _tile(t, 512, SUBLANE)

    def body(sr_ref, si_ref, cr_ref, ci_ref, u_ref, d_ref, yp_ref, ya_ref):
        y = _dot(sr_ref[...], cr_ref[...], NN) - _dot(si_ref[...], ci_ref[...], NN) + d_ref[...] * u_ref[...]
        yp_ref[...] = y
        ya_ref[...] = _gelu(y).astype(ya_ref.dtype)

    s_spec = pl.BlockSpec((None, tb, ns), lambda sg, i: (sg, i, 0))
    c_spec = pl.BlockSpec((None, ns, cs), lambda sg, i: (sg, 0, 0))
    col = pl.BlockSpec((tb, cs), lambda sg, i: (i, sg))
    return pl.pallas_call(
        body, name=name, out_shape=[jax.ShapeDtypeStruct((t, d), F32), jax.ShapeDtypeStruct((t, d), MXU_DTYPE)],
        grid=(nsg, t // tb), in_specs=[s_spec, s_spec, c_spec, c_spec, col, pl.BlockSpec((1, cs), lambda sg, i: (0, sg))],
        out_specs=[col, col], compiler_params=_params(2))(s_re, s_im, cc_re, cc_im, u, dskip)


def _gelu_bwd(name, dy, ypre):
    t, d = dy.shape
    bm = _tile(t, 256, SUBLANE)

    def body(dy_ref, yp_ref, o_ref):
        o_ref[...] = (dy_ref[...] * _gelu_and_grad(yp_ref[...])[1]).astype(o_ref.dtype)

    return pl.pallas_call(
        body, name=name, out_shape=jax.ShapeDtypeStruct((t, d), MXU_DTYPE), grid=(t // bm,),
        in_specs=[_row_spec(bm, d), _row_spec(bm, d)], out_specs=_row_spec(bm, d), compiler_params=_params(1))(dy, ypre)


def _s5_scan_bwd(name, dyp, cc_re, cc_im, ab_re, ab_im, s_re, s_im):
    t, d = dyp.shape
    nsg, ns, cs = cc_re.shape
    tb = _s5_time_block(t)
    nb = t // tb

    def body(dy_ref, ccr_hbm, cci_hbm, ar_ref, ai_ref, sr_ref, si_ref, lrm_ref, lim_ref, dar_ref, dai_ref,
             ccr, cci, lr_ref, li_ref, cr_ref, ci_ref):
        @pl.when(pl.program_id(0) == 0)
        def _():
            pltpu.sync_copy(ccr_hbm, ccr)
            pltpu.sync_copy(cci_hbm, cci)
            cr_ref[...] = jnp.zeros_like(cr_ref)
            ci_ref[...] = jnp.zeros_like(ci_ref)
            dar_ref[...] = jnp.zeros_like(dar_ref)
            dai_ref[...] = jnp.zeros_like(dai_ref)

        for sg in range(nsg):
            dys = dy_ref[:, sg * cs:(sg + 1) * cs]
            lr_ref[:, sg, :] = _dot(dys, ccr[sg], NT)
            li_ref[:, sg, :] = -_dot(dys, cci[sg], NT)
        ar, ai = ar_ref[...], ai_ref[...]

        def step(i, carry):
            cr, ci, dar, dai = carry
            j = tb - 1 - i
            sr, si = sr_ref[j], si_ref[j]
            dar = dar + (cr * sr + ci * si)
            dai = dai + (ci * sr - cr * si)
            nr = lr_ref[j] + (ar * cr + ai * ci)
            ni = li_ref[j] + (ar * ci - ai * cr)
            lr_ref[j] = nr
            li_ref[j] = ni
            return nr, ni, dar, dai

        cr, ci, dar, dai = lax.fori_loop(0, tb, step, (cr_ref[...], ci_ref[...], dar_ref[...], dai_ref[...]))
        cr_ref[...], ci_ref[...] = cr, ci
        dar_ref[...], dai_ref[...] = dar, dai
        for sg in range(nsg):
            lrm_ref[sg] = lr_ref[:, sg, :].astype(MXU_DTYPE)
            lim_ref[sg] = li_ref[:, sg, :].astype(MXU_DTYPE)

    hbm = pl.BlockSpec(memory_space=pltpu.HBM)
    full = pl.BlockSpec((nsg, ns), lambda i: (0, 0))
    mxu = jax.ShapeDtypeStruct((nsg, t, ns), MXU_DTYPE)
    acc = jax.ShapeDtypeStruct((nsg, ns), F32)
    scan_spec = pl.BlockSpec((tb, nsg, ns), lambda i: (nb - 1 - i, 0, 0))
    return pl.pallas_call(
        body, name=name, out_shape=[mxu, mxu, acc, acc], grid=(nb,),
        in_specs=[pl.BlockSpec((tb, d), lambda i: (nb - 1 - i, 0)), hbm, hbm, full, full, scan_spec, scan_spec],
        out_specs=[pl.BlockSpec((nsg, tb, ns), lambda i: (0, nb - 1 - i, 0))] * 2 + [full, full],
        scratch_shapes=[pltpu.VMEM(cc_re.shape, cc_re.dtype), pltpu.VMEM(cc_im.shape, cc_im.dtype),
                        pltpu.VMEM((tb, nsg, ns), F32), pltpu.VMEM((tb, nsg, ns), F32),
                        pltpu.VMEM((nsg, ns), F32), pltpu.VMEM((nsg, ns), F32)],
        compiler_params=_params(1))(dyp, cc_re, cc_im, ab_re, ab_im, s_re, s_im)


def _s5_grads(name, lam_re, lam_im, s_re, s_im, u, dyp, bb_re, bb_im, dskip):
    nsg, t, ns = lam_re.shape
    d = u.shape[1]
    cs = bb_re.shape[1]
    tb = _tile(t, 512, SUBLANE)

    def body(lr_ref, li_ref, sr_ref, si_ref, u_ref, dy_ref, bbr_ref, bbi_ref, d_ref,
             du_ref, dbbr_ref, dbbi_ref, dccr_ref, dcci_ref, dd_ref):
        @pl.when(pl.program_id(1) == 0)
        def _():
            for r in (dbbr_ref, dbbi_ref, dccr_ref, dcci_ref, dd_ref):
                r[...] = jnp.zeros_like(r)

        lr, li, uv, dy = lr_ref[...], li_ref[...], u_ref[...], dy_ref[...]
        dyf = dy.astype(F32)
        du_ref[...] = _dot(lr, bbr_ref[...], NT) + _dot(li, bbi_ref[...], NT) + d_ref[...] * dyf
        dbbr_ref[...] += _dot(uv, lr, TN)
        dbbi_ref[...] += _dot(uv, li, TN)
        dccr_ref[...] += _dot(sr_ref[...], dy, TN)
        dcci_ref[...] -= _dot(si_ref[...], dy, TN)
        dd_ref[...] += jnp.sum(dyf * uv, axis=0, keepdims=True)

    s_spec = pl.BlockSpec((None, tb, ns), lambda sg, i: (sg, i, 0))
    col = pl.BlockSpec((tb, cs), lambda sg, i: (i, sg))
    b_spec = pl.BlockSpec((None, cs, ns), lambda sg, i: (sg, 0, 0))
    c_spec = pl.BlockSpec((None, ns, cs), lambda sg, i: (sg, 0, 0))
    vec = pl.BlockSpec((1, cs), lambda sg, i: (0, sg))
    return pl.pallas_call(
        body, name=name,
        out_shape=[jax.ShapeDtypeStruct((t, d), F32), jax.ShapeDtypeStruct(bb_re.shape, F32),
                   jax.ShapeDtypeStruct(bb_re.shape, F32), jax.ShapeDtypeStruct((nsg, ns, cs), F32),
                   jax.ShapeDtypeStruct((nsg, ns, cs), F32), jax.ShapeDtypeStruct((1, d), F32)],
        grid=(nsg, t // tb), in_specs=[s_spec, s_spec, s_spec, s_spec, col, col, b_spec, b_spec, vec],
        out_specs=[col, b_spec, b_spec, c_spec, c_spec, vec],
        compiler_params=_params(2))(lam_re, lam_im, s_re, s_im, u, dyp, bb_re, bb_im, dskip)


def _shift_down(x, k, prev8):
    if k == 0:
        return x
    ext = jnp.concatenate([prev8, x], axis=0)
    return ext[SUBLANE - k:SUBLANE - k + x.shape[0]]


def _shift_up(x, k, next8):
    if k == 0:
        return x
    ext = jnp.concatenate([x, next8], axis=0)
    return ext[k:k + x.shape[0]]


def _lru_time_block(t):
    return _tile(t, 256, SUBLANE)


def _lru_gates(xp, prev8, cv_ref, wrg, wig):
    taps = cv_ref.shape[0] - 4
    row = lambda k: cv_ref[k:k + 1, :]
    xs = [_shift_down(xp, taps - 1 - k, prev8) for k in range(taps)]
    xb = row(taps)
    for k in range(taps):
        xb = xb + row(k) * xs[k]
    r = jax.nn.sigmoid(_dot(xb, wrg, NN) + row(taps + 1))
    ig = jax.nn.sigmoid(_dot(xb, wig, NN) + row(taps + 2))
    sp = jax.nn.softplus(-row(taps + 3))
    log_a = -LRU_C * r * sp
    a = jnp.exp(log_a)
    mult = jnp.sqrt(_neg_expm1(2.0 * log_a))
    return xs, xb, r, ig, sp, a, mult


def _lru_fwd(name, zz, cvec, wrg, wig):
    _, t, w = zz.shape
    half = N_DEV // 2
    tb = _lru_time_block(t)

    def body(gb_ref, xp_ref, xprev_ref, cv_ref, wrg_ref, wig_ref, hs_ref, y_ref, a_scr, b_scr, carry):
        i = pl.program_id(1)

        @pl.when(i == 0)
        def _():
            carry[...] = jnp.zeros_like(carry)

        prev8 = jnp.where(i > 0, xprev_ref[...], 0.0)
        _, xb, _, ig, _, a, mult = _lru_gates(xp_ref[...], prev8, cv_ref, wrg_ref[...], wig_ref[...])
        a_scr[...] = a
        b_scr[...] = mult * (ig * xb)

        def step(j, h):
            h = a_scr[pl.ds(j, 1), :] * h + b_scr[pl.ds(j, 1), :]
            hs_ref[pl.ds(j, 1), :] = h
            return h

        carry[0:1, :] = lax.fori_loop(0, tb, step, carry[0:1, :], unroll=8)
        y_ref[...] = (hs_ref[...] * _gelu(gb_ref[...])).astype(y_ref.dtype)

    nrow = tb // SUBLANE
    blk = lambda off: pl.BlockSpec((None, tb, w), lambda q, i: (q + off, i, 0))
    return pl.pallas_call(
        body, name=name,
        out_shape=[jax.ShapeDtypeStruct((half, t, w), F32), jax.ShapeDtypeStruct((half, t, w), MXU_DTYPE)],
        grid=(half, t // tb),
        in_specs=[blk(0), blk(half),
                  pl.BlockSpec((None, SUBLANE, w), lambda q, i: (q + half, jnp.maximum(i * nrow - 1, 0), 0)),
                  pl.BlockSpec((None,) + cvec.shape[1:], lambda q, i: (q, 0, 0)),
                  pl.BlockSpec((None, w, w), lambda q, i: (q, 0, 0)), pl.BlockSpec((None, w, w), lambda q, i: (q, 0, 0))],
        out_specs=[blk(0), blk(0)],
        scratch_shapes=[pltpu.VMEM((tb, w), F32), pltpu.VMEM((tb, w), F32), pltpu.VMEM((SUBLANE, w), F32)],
        compiler_params=_params(2))(zz, zz, zz, cvec, wrg, wig)


def _lru_bwd(name, zz, hs, dy, cvec, wrg, wig):
    _, t, w = zz.shape
    half = N_DEV // 2
    tb = _lru_time_block(t)
    nb = t // tb
    taps = cvec.shape[1] - 4

    def body(gb_ref, xp_ref, xprev_ref, hs_ref, hprev_ref, dy_ref, cv_ref, wrg_ref, wig_ref,
             dgb_ref, dxp_ref, dcv_ref, dwrg_ref, dwig_ref, a_scr, l_scr, carry, dxb_next):
        i = pl.program_id(1)

        @pl.when(i == 0)
        def _():
            for r_ in (carry, dxb_next, dcv_ref, dwrg_ref, dwig_ref):
                r_[...] = jnp.zeros_like(r_)

        has_prev = i < nb - 1
        row = lambda k: cv_ref[k:k + 1, :]
        prev8 = jnp.where(has_prev, xprev_ref[...], 0.0)
        xs, xb, r, ig, sp, a, mult = _lru_gates(xp_ref[...], prev8, cv_ref, wrg_ref[...], wig_ref[...])
        hs_ = hs_ref[...]
        hs_m1 = _shift_down(hs_, 1, jnp.where(has_prev, hprev_ref[...], 0.0))
        gel, dgel = _gelu_and_grad(gb_ref[...])
        dy_ = dy_ref[...]
        dgb_ref[...] = (dy_ * hs_ * dgel).astype(dgb_ref.dtype)
        a_scr[...] = a
        l_scr[...] = dy_ * gel

        def step(k, c):
            j = tb - 1 - k
            lam = l_scr[pl.ds(j, 1), :] + c
            l_scr[pl.ds(j, 1), :] = lam
            return a_scr[pl.ds(j, 1), :] * lam

        carry[0:1, :] = lax.fori_loop(0, tb, step, carry[0:1, :], unroll=8)
        lam = l_scr[...]
        dmult = lam * (ig * xb)
        dig = lam * (mult * xb)
        dxb = lam * (mult * ig)
        dlog_a = (lam * hs_m1) * a - dmult * (a * a) / mult
        dr = dlog_a * (-LRU_C * sp)
        dsp = jnp.sum(dlog_a * (-LRU_C * r), axis=0, keepdims=True)
        dpr = dr * (r * (1.0 - r))
        dpi = dig * (ig * (1.0 - ig))
        dwrg_ref[...] += _dot(xb, dpr, TN)
        dwig_ref[...] += _dot(xb, dpi, TN)
        dxb = dxb + _dot(dpr, wrg_ref[...], NT) + _dot(dpi, wig_ref[...], NT)
        for k in range(taps):
            dcv_ref[k:k + 1, :] += jnp.sum(dxb * xs[k], axis=0, keepdims=True)
        dcv_ref[taps:taps + 1, :] += jnp.sum(dxb, axis=0, keepdims=True)
        dcv_ref[taps + 1:taps + 2, :] += jnp.sum(dpr, axis=0, keepdims=True)
        dcv_ref[taps + 2:taps + 3, :] += jnp.sum(dpi, axis=0, keepdims=True)
        dcv_ref[taps + 3:taps + 4, :] += dsp * (-jax.nn.sigmoid(-row(taps + 3)))
        nxt8 = dxb_next[...]
        dxp = row(taps - 1) * dxb
        for k in range(taps - 1):
            dxp = dxp + row(k) * _shift_up(dxb, taps - 1 - k, nxt8)
        dxp_ref[...] = dxp.astype(dxp_ref.dtype)
        dxb_next[...] = dxb[0:SUBLANE]

    nrow = tb // SUBLANE
    blk = lambda off: pl.BlockSpec((None, tb, w), lambda q, i: (q + off, nb - 1 - i, 0))
    halo = lambda off: pl.BlockSpec((None, SUBLANE, w), lambda q, i: (q + off, jnp.maximum((nb - 1 - i) * nrow - 1, 0), 0))
    wspec = pl.BlockSpec((None, w, w), lambda q, i: (q, 0, 0))
    cspec = pl.BlockSpec((None,) + cvec.shape[1:], lambda q, i: (q, 0, 0))
    act = jax.ShapeDtypeStruct((half, t, w), MXU_DTYPE)
    return pl.pallas_call(
        body, name=name,
        out_shape=[act, act, jax.ShapeDtypeStruct(cvec.shape, F32), jax.ShapeDtypeStruct(wrg.shape, F32),
                   jax.ShapeDtypeStruct(wig.shape, F32)],
        grid=(half, nb),
        in_specs=[blk(0), blk(half), halo(half), blk(0), halo(0), blk(0), cspec, wspec, wspec],
        out_specs=[blk(0), blk(0), cspec, wspec, wspec],
        scratch_shapes=[pltpu.VMEM((tb, w), F32), pltpu.VMEM((tb, w), F32), pltpu.VMEM((SUBLANE, w), F32),
                        pltpu.VMEM((SUBLANE, w), F32)],
        compiler_params=_params(2))(zz, zz, zz, hs, hs, dy, cvec, wrg, wig)


def _band(blocks, per):
    n, a, b = blocks.shape
    eye = jnp.eye(per, dtype=blocks.dtype)
    x = blocks.reshape(n // per, per, a, b)
    return jnp.einsum('sgab,gh->sgahb', x, eye).reshape(n // per, per * a, per * b)


def _unband(bands, per):
    s, pa, pb = bands.shape
    a, b = pa // per, pb // per
    x = bands.reshape(s, per, a, per, b)
    idx = jnp.arange(per)
    return x[:, idx, :, idx, :].transpose(1, 0, 2, 3).reshape(s * per, a, b)


def _pack(arrays, rows_multiple, lanes=LANE):
    flat = jnp.concatenate([a.reshape(-1).astype(F32) for a in arrays])
    rows = -(-flat.shape[0] // (lanes * rows_multiple)) * rows_multiple
    return jnp.pad(flat, (0, rows * lanes - flat.shape[0])).reshape(rows, lanes)


def _unpack(packed, shapes):
    flat = packed.reshape(-1)
    out, off = [], 0
    for s in shapes:
        n = math.prod(s)
        out.append(flat[off:off + n].reshape(s))
        off += n
    return out


def kernel(x, c, norm_g, w_ada, b_ada, s5_w_in, s5_lam_re, s5_lam_im, s5_log_dt, s5_b_re, s5_b_im, s5_c_re, s5_c_im, s5_d, s5_w_glu, lru_w_in, lru_conv_w, lru_conv_b, lru_w_rg, lru_b_rg, lru_w_ig, lru_b_ig, lru_lam, lru_w_out, ffn_w_gu, ffn_w_down, final_g, loss_target, m_norm_g, m_w_ada, m_b_ada, m_s5_w_in, m_s5_lam_re, m_s5_lam_im, m_s5_log_dt, m_s5_b_re, m_s5_b_im, m_s5_c_re, m_s5_c_im, m_s5_d, m_s5_w_glu, m_lru_w_in, m_lru_conv_w, m_lru_conv_b, m_lru_w_rg, m_lru_b_rg, m_lru_w_ig, m_lru_b_ig, m_lru_lam, m_lru_w_out, m_ffn_w_gu, m_ffn_w_down, m_final_g, v_norm_g, v_w_ada, v_b_ada, v_s5_w_in, v_s5_lam_re, v_s5_lam_im, v_s5_log_dt, v_s5_b_re, v_s5_b_im, v_s5_c_re, v_s5_c_im, v_s5_d, v_s5_w_glu, v_lru_w_in, v_lru_conv_w, v_lru_conv_b, v_lru_w_rg, v_lru_b_rg, v_lru_w_ig, v_lru_b_ig, v_lru_lam, v_lru_w_out, v_ffn_w_gu, v_ffn_w_down, v_final_g):
    wv = dict(zip(WEIGHTS, (norm_g, w_ada, b_ada, s5_w_in, s5_lam_re, s5_lam_im, s5_log_dt, s5_b_re, s5_b_im, s5_c_re, s5_c_im, s5_d, s5_w_glu, lru_w_in, lru_conv_w, lru_conv_b, lru_w_rg, lru_b_rg, lru_w_ig, lru_b_ig, lru_lam, lru_w_out, ffn_w_gu, ffn_w_down, final_g)))
    mv = dict(zip(WEIGHTS, (m_norm_g, m_w_ada, m_b_ada, m_s5_w_in, m_s5_lam_re, m_s5_lam_im, m_s5_log_dt, m_s5_b_re, m_s5_b_im, m_s5_c_re, m_s5_c_im, m_s5_d, m_s5_w_glu, m_lru_w_in, m_lru_conv_w, m_lru_conv_b, m_lru_w_rg, m_lru_b_rg, m_lru_w_ig, m_lru_b_ig, m_lru_lam, m_lru_w_out, m_ffn_w_gu, m_ffn_w_down, m_final_g)))
    vv = dict(zip(WEIGHTS, (v_norm_g, v_w_ada, v_b_ada, v_s5_w_in, v_s5_lam_re, v_s5_lam_im, v_s5_log_dt, v_s5_b_re, v_s5_b_im, v_s5_c_re, v_s5_c_im, v_s5_d, v_s5_w_glu, v_lru_w_in, v_lru_conv_w, v_lru_conv_b, v_lru_w_rg, v_lru_b_rg, v_lru_w_ig, v_lru_b_ig, v_lru_lam, v_lru_w_out, v_ffn_w_gu, v_ffn_w_down, v_final_g)))

    me = 4 * lax.axis_index("x") + 2 * lax.axis_index("y") + lax.axis_index("c")
    x0 = x[0]
    tgt = loss_target[0]
    t, d = x0.shape
    depth = norm_g.shape[0]
    n_mod = w_ada.shape[2] * N_DEV // d
    groups, states = s5_lam_re.shape[1], s5_lam_re.shape[2]
    per_sg = S5_SUPER // S5_GROUP
    nsg = groups // per_sg
    lw = lru_lam.shape[1] * N_DEV
    lwc = lw // (N_DEV // 2)
    half = N_DEV // 2

    wire = lambda a: a.astype(WIRE_DTYPE)
    g_s5_in = _all_gather("ag_s5_w_in", wire(s5_w_in[0])).reshape(d, d)
    g_s5_glu = _all_gather("ag_s5_w_glu", wire(s5_w_glu[0]))
    g_lru_in = _all_gather("ag_lru_w_in", wire(lru_w_in[0]))
    g_lru_out = _all_gather("ag_lru_w_out", wire(lru_w_out[0])).reshape(lw, d)
    g_gu = [_all_gather(f"ag_ffn_w_gu{i}", wire(ffn_w_gu[i])) for i in range(depth)]
    g_down = [_all_gather(f"ag_ffn_w_down{i}", wire(ffn_w_down[i])).reshape(-1, d) for i in range(depth)]
    sh_shapes = [wv[n].shape for n in SMALL_SHARDED] + [c.shape]
    sh_all = _all_gather("ag_small", _pack([wv[n] for n in SMALL_SHARDED] + [c], SUBLANE))
    sh_parts = [jnp.stack(p) for p in zip(*[_unpack(sh_all[s], sh_shapes) for s in range(N_DEV)])]
    full = {}
    for n, p in zip(SMALL_SHARDED, sh_parts[:-1]):
        full[n] = jnp.moveaxis(p, 0, -2).reshape(p.shape[1:-1] + (-1,))
    c_all = sh_parts[-1].reshape(N_DEV, d)
    c16 = jnp.pad(c_all, ((0, 2 * SUBLANE - N_DEV), (0, 0)))

    n_loc = w_ada.shape[2]
    b_loc = lax.dynamic_slice_in_dim(b_ada, me * n_loc, n_loc, axis=1)[:, None, :]
    mod_part = _ada_fwd("ada_fwd", c16, w_ada, b_loc)[:, :N_DEV]
    mod_mine = _chunk_exchange("x_mod", [mod_part.transpose(1, 0, 2)], ALL)
    mod = mod_mine.transpose(1, 0, 2).reshape(depth, n_mod, 1, d)

    lam3 = lambda a: a[0][:, None, :]
    p_lr, p_li, p_ld = lam3(s5_lam_re), lam3(s5_lam_im), s5_log_dt[0][:, None, None]
    p_br, p_bi = s5_b_re[0].transpose(0, 2, 1), s5_b_im[0].transpose(0, 2, 1)
    ab_re3, ab_im3, bb_re3, bb_im3 = _s5_disc("s5_disc", p_lr, p_li, p_ld, p_br, p_bi)
    ab_re, ab_im = ab_re3.reshape(nsg, per_sg * states), ab_im3.reshape(nsg, per_sg * states)
    bb_re, bb_im = _band(wire(bb_re3), per_sg), _band(wire(bb_im3), per_sg)
    cc_re = _band(wire(s5_c_re[0].transpose(0, 2, 1)), per_sg)
    cc_im = _band(wire(s5_c_im[0].transpose(0, 2, 1)), per_sg)

    taps = lru_conv_w.shape[1]
    cvec = jnp.concatenate([full['lru_conv_w'].reshape(taps, lw), full['lru_conv_b'], full['lru_b_rg'],
                            full['lru_b_ig'], full['lru_lam']], axis=0)
    cvec = cvec.reshape(taps + 4, half, lwc).transpose(1, 0, 2)
    wrg = _band(wire(lru_w_rg[0]), LRU_BLOCKS_PER_CHUNK)
    wig = _band(wire(lru_w_ig[0]), LRU_BLOCKS_PER_CHUNK)

    saved = []
    xc = x0
    for i in range(depth):
        sh1, sc1, g1, sh2, sc2, g2 = [mod[i, k] for k in range(n_mod)]
        gn = full['norm_g'][i]
        h1 = _norm_mod_fwd(f"norm1_fwd{i}", xc, gn[0:1], sc1, sh1)
        if i % 2 == 0:
            u = _mm_row(f"s5_in{i}", h1[None], g_s5_in)
            s_re, s_im, s_rem, s_imm = _s5_scan_fwd(f"s5_scan{i}", u, bb_re, bb_im, ab_re, ab_im)
            ypre, yact = _s5_out_fwd(f"s5_out{i}", s_rem, s_imm, cc_re, cc_im, u, s5_d)
            z = _mm_col(f"s5_glu{i}", yact, g_s5_glu)
            x1 = _glu_resid_fwd(f"s5_resid{i}", z, xc, g1)
            mix = (u, s_re, s_im, s_rem, s_imm, ypre, yact, z)
        else:
            zz = _mm_col(f"lru_in{i}", h1, g_lru_in)
            hs, ylru = _lru_fwd(f"lru_core{i}", zz, cvec, wrg, wig)
            o = _mm_row(f"lru_out{i}", ylru, g_lru_out)
            x1 = _resid(f"lru_resid{i}", xc, o, g1)
            mix = (zz, hs, ylru, o)
        h2 = _norm_mod_fwd(f"norm2_fwd{i}", x1, gn[1:2], sc2, sh2)
        gu = _mm_col(f"ffn_gu{i}", h2, g_gu[i])
        act = _swiglu_act_fwd(f"ffn_act{i}", gu)
        f = _mm_row(f"ffn_down{i}", act, g_down[i])
        x2 = _resid(f"ffn_resid{i}", x1, f, g2)
        saved.append((xc, h1, mix, x1, h2, gu, act, f))
        xc = x2

    dx, loss_part, d_final_g = _loss_bwd("loss", xc, tgt, final_g[None])
    loss = lax.psum(loss_part[0, 0], ("x", "y", "c"))

    grads = {}
    dmod = [None] * depth
    d_norm_g = [None] * depth
    d_gu, d_down = [None] * depth, [None] * depth
    for i in reversed(range(depth)):
        xin, h1, mix, x1, h2, gu, act, f = saved[i]
        sh1, sc1, g1, sh2, sc2, g2 = [mod[i, k] for k in range(n_mod)]
        gn = full['norm_g'][i]
        df, dg2 = _gate_bwd(f"ffn_gate_bwd{i}", dx, f, g2)
        dact = _mm_row_da(f"ffn_down_da{i}", df, g_down[i], half)
        d_down[i] = _mm_row_db(f"ffn_down_db{i}", act, df, WIRE_DTYPE)
        dgu = _swiglu_act_bwd(f"ffn_act_bwd{i}", gu, dact)
        dh2 = _mm_col_da(f"ffn_gu_da{i}", dgu, g_gu[i])
        d_gu[i] = _mm_col_db(f"ffn_gu_db{i}", h2, dgu, WIRE_DTYPE)
        dx, dgn2, dsc2, dsh2 = _norm_mod_bwd(f"norm2_bwd{i}", x1, dh2, dx, gn[1:2], sc2)
        if i % 2 == 0:
            u, s_re, s_im, s_rem, s_imm, ypre, yact, z = mix
            dz, dg1 = _glu_resid_bwd(f"s5_resid_bwd{i}", z, dx, g1)
            dyact = _mm_col_da(f"s5_glu_da{i}", dz, g_s5_glu)
            grads['s5_w_glu'] = _mm_col_db(f"s5_glu_db{i}", yact, dz, WIRE_DTYPE)
            dyp = _gelu_bwd(f"s5_gelu_bwd{i}", dyact, ypre)
            l_rem, l_imm, dab_re, dab_im = _s5_scan_bwd(f"s5_scan_bwd{i}", dyp, cc_re, cc_im, ab_re, ab_im, s_re, s_im)
            du, dbb_re, dbb_im, dcc_re, dcc_im, dd = _s5_grads(f"s5_grads{i}", l_rem, l_imm, s_rem, s_imm, u, dyp,
                                                                bb_re, bb_im, s5_d)
            dlr, dli, dld, dbr, dbi = _s5_disc_bwd(
                "s5_disc_bwd", p_lr, p_li, p_ld, p_br, p_bi, dab_re.reshape(groups, 1, states),
                dab_im.reshape(groups, 1, states), _unband(dbb_re, per_sg), _unband(dbb_im, per_sg))
            grads['s5_lam_re'], grads['s5_lam_im'], grads['s5_log_dt'] = dlr[:, 0][None], dli[:, 0][None], dld[:, 0, 0][None]
            grads['s5_b_re'], grads['s5_b_im'] = dbr.transpose(0, 2, 1)[None], dbi.transpose(0, 2, 1)[None]
            grads['s5_c_re'] = _unband(dcc_re, per_sg).transpose(0, 2, 1)[None]
            grads['s5_c_im'] = _unband(dcc_im, per_sg).transpose(0, 2, 1)[None]
            grads['s5_d'] = dd
            dub = du.astype(MXU_DTYPE)
            dh1 = _mm_row_da(f"s5_in_da{i}", dub, g_s5_in, 1)[0]
            grads['s5_w_in'] = _mm_row_db(f"s5_in_db{i}", h1[None], dub, WIRE_DTYPE)
        else:
            zz, hs, ylru, o = mix
            do, dg1 = _gate_bwd(f"lru_gate_bwd{i}", dx, o, g1)
            dyl = _mm_row_da(f"lru_out_da{i}", do, g_lru_out, half)
            grads['lru_w_out'] = _mm_row_db(f"lru_out_db{i}", ylru, do, WIRE_DTYPE)
            dgb, dxp, dcv, dwrg, dwig = _lru_bwd(f"lru_core_bwd{i}", zz, hs, dyl, cvec, wrg, wig)
            dzz = jnp.concatenate([dgb, dxp], axis=0)
            dh1 = _mm_col_da(f"lru_in_da{i}", dzz, g_lru_in)
            grads['lru_w_in'] = _mm_col_db(f"lru_in_db{i}", h1, dzz, WIRE_DTYPE)
            dcv = dcv.transpose(1, 0, 2).reshape(taps + 4, lw)
            grads['lru_conv_w'] = dcv[:taps].reshape(1, taps, 1, lw)
            grads['lru_conv_b'], grads['lru_b_rg'] = dcv[taps:taps + 1], dcv[taps + 1:taps + 2]
            grads['lru_b_ig'], grads['lru_lam'] = dcv[taps + 2:taps + 3], dcv[taps + 3:taps + 4]
            grads['lru_w_rg'] = _unband(dwrg, LRU_BLOCKS_PER_CHUNK)[None]
            grads['lru_w_ig'] = _unband(dwig, LRU_BLOCKS_PER_CHUNK)[None]
        dx, dgn1, dsc1, dsh1 = _norm_mod_bwd(f"norm1_bwd{i}", xin, dh1, dx, gn[0:1], sc1)
        dmod[i] = jnp.concatenate([dsh1, dsc1, dg1, dsh2, dsc2, dg2], axis=1)
        d_norm_g[i] = jnp.concatenate([dgn1, dgn2], axis=0)
    grad_x = dx[None]
    dmod = jnp.concatenate(dmod, axis=0)
    grads['norm_g'] = jnp.stack(d_norm_g)
    grads['b_ada'] = dmod
    grads['final_g'] = d_final_g[0]

    small_partial = _pack([grads[n] for n in SMALL], SUBLANE * N_DEV)
    rows8 = small_partial.shape[0] // N_DEV
    got = _grad_exchange("x_small", [small_partial.reshape(N_DEV, rows8, LANE)])
    summed = _sum_parts("sum_small", got)
    small_total = _all_gather("ag_small_sum", summed).reshape(-1, LANE)
    small_grad = dict(zip(SMALL, _unpack(small_total, [grads[n].shape for n in SMALL])))
    for n in SMALL_SHARDED:
        shard = wv[n].shape[-1]
        small_grad[n] = lax.dynamic_slice_in_dim(small_grad[n], me * shard, shard, axis=small_grad[n].ndim - 1)
    small_shapes = [wv[n].shape for n in SMALL]
    pk = lambda dct: _pack([dct[n] for n in SMALL], 2 * SUBLANE, 8 * LANE)
    s_out = _adamw_sum("adamw_small", pk(small_grad)[None], pk(wv), pk(mv), pk(vv))
    out = {n: r for n, *r in zip(SMALL, *[_unpack(o, small_shapes) for o in s_out])}

    dmod_all = _all_gather("ag_dmod", dmod)
    dmod_loc = lax.dynamic_slice_in_dim(dmod_all, me * n_loc, n_loc, axis=2).transpose(1, 0, 2)
    dmod16 = jnp.pad(dmod_loc, ((0, 0), (0, 2 * SUBLANE - N_DEV), (0, 0)))
    out['w_ada'] = _adamw_w_ada("adamw_w_ada", c16, dmod16, w_ada, m_w_ada, v_w_ada)

    def big(name, partials):
        w = wv[name]
        rows, cols = w.shape[-2] * w.shape[0], w.shape[-1]
        parts = _grad_exchange("x_" + name, [p.reshape(N_DEV, -1, cols) for p in partials])
        res = _adamw_sum("adamw_" + name, parts, w.reshape(rows, cols), mv[name].reshape(rows, cols),
                         vv[name].reshape(rows, cols))
        out[name] = [r.reshape(w.shape) for r in res]

    big('ffn_w_gu', d_gu)
    big('ffn_w_down', d_down)
    big('lru_w_in', [grads['lru_w_in']])
    big('lru_w_out', [grads['lru_w_out']])
    big('s5_w_glu', [grads['s5_w_glu']])
    big('s5_w_in', [grads['s5_w_in']])

    return (loss, grad_x, *[out[n][0] for n in WEIGHTS], *[out[n][1] for n in WEIGHTS],
            *[out[n][2] for n in WEIGHTS], *[out[n][3] for n in WEIGHTS])
```

```python
import functools
import math

import jax
import jax.numpy as jnp
from jax import lax
from jax.experimental import pallas as pl
from jax.experimental.pallas import tpu as pltpu

F32 = jnp.float32
MXU_DTYPE = jnp.bfloat16
WIRE_DTYPE = jnp.bfloat16
N_DEV = 8
EPS = 1e-6
LRU_C = 8.0
S5_GROUP = 16
S5_STATE = 64
S5_SUPER = 256
LRU_BLOCKS_PER_CHUNK = 4
ADAM_LR, ADAM_B1, ADAM_B2, ADAM_EPS, ADAM_WD, ADAM_STEP = 0.001, 0.9, 0.999, 1e-08, 0.01, 10
VMEM_LIMIT_BYTES = 56 * 1024 * 1024
LANE = 128
SUBLANE = 8

WEIGHTS = ['norm_g', 'w_ada', 'b_ada', 's5_w_in', 's5_lam_re', 's5_lam_im', 's5_log_dt', 's5_b_re', 's5_b_im',
           's5_c_re', 's5_c_im', 's5_d', 's5_w_glu', 'lru_w_in', 'lru_conv_w', 'lru_conv_b', 'lru_w_rg', 'lru_b_rg',
           'lru_w_ig', 'lru_b_ig', 'lru_lam', 'lru_w_out', 'ffn_w_gu', 'ffn_w_down', 'final_g']
BIG = ('w_ada', 's5_w_in', 's5_w_glu', 'lru_w_in', 'lru_w_out', 'ffn_w_gu', 'ffn_w_down')
SMALL = tuple(n for n in WEIGHTS if n not in BIG)
SMALL_SHARDED = ('norm_g', 'lru_conv_w', 'lru_conv_b', 'lru_b_rg', 'lru_b_ig', 'lru_lam')

NN = (((1,), (0,)), ((), ()))
NT = (((1,), (1,)), ((), ()))
TN = (((0,), (0,)), ((), ()))


def _params(n_grid):
    return pltpu.CompilerParams(dimension_semantics=("arbitrary",) * n_grid, vmem_limit_bytes=VMEM_LIMIT_BYTES)


def _tile(dim, pref, align=LANE):
    if dim <= pref:
        return dim
    t = (pref // align) * align
    while t >= align:
        if dim % t == 0:
            return t
        t -= align
    return dim


def _dot(a, b, dims):
    return lax.dot_general(a.astype(MXU_DTYPE), b.astype(MXU_DTYPE), dims, preferred_element_type=F32)


def _gelu(x):
    k = math.sqrt(2.0 / math.pi)
    return 0.5 * x * (1.0 + jnp.tanh(k * (x + 0.044715 * (x * x * x))))


def _gelu_and_grad(x):
    k = math.sqrt(2.0 / math.pi)
    th = jnp.tanh(k * (x + 0.044715 * (x * x * x)))
    g = 0.5 * x * (1.0 + th)
    dg = 0.5 * (1.0 + th) + 0.5 * x * (1.0 - th * th) * (k * (1.0 + 3.0 * 0.044715 * (x * x)))
    return g, dg


def _neg_expm1(x):
    series = -x * (1.0 + x * (0.5 + x * (1.0 / 6.0 + x * (1.0 / 24.0 + x * (1.0 / 120.0)))))
    return jnp.where(x > -0.01, series, 1.0 - jnp.exp(x))


MESH = pl.DeviceIdType.MESH
N_CHIP = N_DEV // 2
ALL, SAME_CORE = 7, 6


def _place():
    x, y, c = lax.axis_index("x"), lax.axis_index("y"), lax.axis_index("c")
    return x, y, c


def _flip(place, k):
    x, y, c = place
    return (1 - x if (k >> 2) & 1 else x, 1 - y if (k >> 1) & 1 else y, 1 - c if k & 1 else c)


def _chunk_exchange(name, xs, group):
    return _ride_alone(name, _chunk_rider(xs, group))


class _Rider:
    def __init__(self, arrays, out_shape, scratch, start, finish, post):
        self.arrays, self.out_shape, self.scratch = list(arrays), list(out_shape), list(scratch)
        self.start, self.finish, self.post = start, finish, post


def _chunk_rider(xs, group):
    n = len(xs)
    members, r, c_ = xs[0].shape
    assert members == {ALL: N_DEV, SAME_CORE: N_CHIP}[group]
    assert all(a.shape == xs[0].shape and a.dtype == xs[0].dtype for a in xs)
    ks = [k for k in range(1, N_DEV) if not k & ~group]
    member = (lambda p: 4 * p[0] + 2 * p[1] + p[2]) if group == ALL else (lambda p: 2 * p[0] + p[1])

    def copies(ins, outs, scratch):
        out = outs[0]
        send_sems, recv_sems, local_sems = scratch
        place = _place()
        me = member(place)
        local = [pltpu.make_async_copy(ins[l].at[me], out.at[me, l], local_sems.at[l]) for l in range(n)]
        remote = []
        for l in range(n):
            for k in ks:
                pid = _flip(place, k)
                peer = member(pid)

                def copy(land_at, l=l, k=k, peer=peer, pid=pid):
                    return pltpu.make_async_remote_copy(
                        src_ref=ins[l].at[peer], dst_ref=out.at[land_at, l], send_sem=send_sems.at[l * N_DEV + k],
                        recv_sem=recv_sems.at[l * N_DEV + k], device_id=pid, device_id_type=MESH)

                remote.append((copy, me, peer))
        return local, remote

    def start(ins, outs, scratch):
        local, remote = copies(ins, outs, scratch)
        for cp in local:
            cp.start()
        for copy, me, _ in remote:
            copy(me).start()

    def finish(ins, outs, scratch):
        local, remote = copies(ins, outs, scratch)
        for copy, me, peer in remote:
            copy(me).wait_send()
            copy(peer).wait_recv()
        for cp in local:
            cp.wait()

    return _Rider(
        xs, [jax.ShapeDtypeStruct((members, n, r, c_), xs[0].dtype)],
        [pltpu.SemaphoreType.DMA((n * N_DEV,)), pltpu.SemaphoreType.DMA((n * N_DEV,)), pltpu.SemaphoreType.DMA((n,))],
        start, finish, lambda outs: outs[0].reshape(members, n * r, c_))


def _gather_rider(xs):
    n = len(xs)
    chip_flips = (2, 4, 6)
    per = 1 + 2 * len(chip_flips)

    def plan(ins, outs, scratch):
        send_sems, recv_sems, local_sems = scratch
        place = _place()
        sibling = _flip(place, 1)
        jobs = []
        for l in range(n):
            slot = lambda p, l=l: outs[l].at[2 * p[0] + p[1], p[2]]

            def copy(k, block, to, src=None, l=l, slot=slot):
                return pltpu.make_async_remote_copy(
                    src_ref=slot(block) if src is None else src, dst_ref=slot(block), send_sem=send_sems.at[l * per + k],
                    recv_sem=recv_sems.at[l * per + k], device_id=to, device_id_type=MESH)

            mine = pltpu.make_async_copy(ins[l], slot(place), local_sems.at[l])
            first = [copy(0, place, sibling, src=ins[l])]
            first += [copy(1 + j, place, _flip(place, k), src=ins[l]) for j, k in enumerate(chip_flips)]
            jobs.append((copy, mine, first))
        return place, sibling, jobs

    def start(ins, outs, scratch):
        _, _, jobs = plan(ins, outs, scratch)
        for _, mine, first in jobs:
            mine.start()
            for cp in first:
                cp.start()

    def finish(ins, outs, scratch):
        place, sibling, jobs = plan(ins, outs, scratch)
        passed = []
        for copy, _, _ in jobs:
            for j, k in enumerate(chip_flips):
                copy(1 + j, _flip(place, k), place).wait_recv()
                fwd = copy(4 + j, _flip(place, k), sibling)
                fwd.start()
                passed.append(fwd)
        for copy, mine, first in jobs:
            copy(0, sibling, place).wait_recv()
            for j, k in enumerate(chip_flips):
                copy(4 + j, _flip(sibling, k), place).wait_recv()
            for cp in first:
                cp.wait_send()
            mine.wait()
        for cp in passed:
            cp.wait_send()

    return _Rider(
        xs, [jax.ShapeDtypeStruct((N_CHIP, 2) + x.shape, x.dtype) for x in xs],
        [pltpu.SemaphoreType.DMA((n * per,)), pltpu.SemaphoreType.DMA((n * per,)), pltpu.SemaphoreType.DMA((n,))],
        start, finish, lambda outs: [o.reshape((N_DEV,) + x.shape) for o, x in zip(outs, xs)])


HBM_SPEC = pl.BlockSpec(memory_space=pltpu.HBM)


def _ride_alone(name, rider):
    n_in, n_out = len(rider.arrays), len(rider.out_shape)

    def body(*refs):
        parts = refs[:n_in], refs[n_in:n_in + n_out], refs[n_in + n_out:]
        rider.start(*parts)
        rider.finish(*parts)

    outs = pl.pallas_call(body, name=name, out_shape=rider.out_shape, in_specs=[HBM_SPEC] * n_in,
                          out_specs=[HBM_SPEC] * n_out, scratch_shapes=rider.scratch)(*rider.arrays)
    return rider.post(list(outs))


def _call(body, *, name, grid, in_specs, out_specs, out_shape, scratch_shapes=(), args, rider=None):
    single = not isinstance(out_shape, (list, tuple))
    out_shape = [out_shape] if single else list(out_shape)
    out_specs = [out_specs] if single else list(out_specs)
    scratch_shapes = list(scratch_shapes)
    unwrap = lambda outs: outs[0] if single else list(outs)
    if rider is None:
        outs = pl.pallas_call(body, name=name, grid=grid, in_specs=list(in_specs), out_specs=out_specs, out_shape=out_shape,
                              scratch_shapes=scratch_shapes, compiler_params=_params(len(grid)))(*args)
        return unwrap(outs)
    n_in, n_out, n_scr = len(in_specs), len(out_shape), len(scratch_shapes)
    r_in, r_out = len(rider.arrays), len(rider.out_shape)

    def carried(*refs):
        ins, refs = refs[:n_in], refs[n_in:]
        r_ins, refs = refs[:r_in], refs[r_in:]
        outs, refs = refs[:n_out], refs[n_out:]
        r_outs, refs = refs[:r_out], refs[r_out:]
        scr, r_scr = refs[:n_scr], refs[n_scr:]
        steps = [pl.program_id(ax) for ax in range(len(grid))]

        @pl.when(functools.reduce(jnp.logical_and, [s == 0 for s in steps]))
        def _():
            rider.start(r_ins, r_outs, r_scr)

        body(*ins, *outs, *scr)

        @pl.when(functools.reduce(jnp.logical_and, [s == g - 1 for s, g in zip(steps, grid)]))
        def _():
            rider.finish(r_ins, r_outs, r_scr)

    outs = pl.pallas_call(
        carried, name=name, grid=grid, in_specs=list(in_specs) + [HBM_SPEC] * r_in, out_specs=out_specs + [HBM_SPEC] * r_out,
        out_shape=out_shape + rider.out_shape, scratch_shapes=scratch_shapes + rider.scratch,
        compiler_params=_params(len(grid)))(*args, *rider.arrays)
    return unwrap(outs[:n_out]), rider.post(list(outs[n_out:]))


def _all_gather(name, x):
    return _ride_alone(name, _gather_rider([x]))[0]


def _sibling_exchange(name, xs):
    n = len(xs)
    _, r, c_ = xs[0].shape
    assert all(a.shape == xs[0].shape and a.dtype == xs[0].dtype for a in xs)

    def body(*refs):
        ins, out = refs[:n], refs[n]
        send_sems, recv_sems = refs[n + 1:]
        place = _place()
        c = place[2]
        sibling = _flip(place, 1)
        copies = []
        for l in range(n):
            for chip in range(N_CHIP):
                cp = pltpu.make_async_remote_copy(
                    src_ref=ins[l].at[2 * chip + (1 - c)], dst_ref=out.at[chip, l], send_sem=send_sems.at[l * N_CHIP + chip],
                    recv_sem=recv_sems.at[l * N_CHIP + chip], device_id=sibling, device_id_type=MESH)
                cp.start()
                copies.append(cp)
        for cp in copies:
            cp.wait()

    hbm = pl.BlockSpec(memory_space=pltpu.HBM)
    return pl.pallas_call(
        body, name=name, out_shape=jax.ShapeDtypeStruct((N_CHIP, n, r, c_), xs[0].dtype), in_specs=[hbm] * n, out_specs=hbm,
        scratch_shapes=[pltpu.SemaphoreType.DMA((n * N_CHIP,)), pltpu.SemaphoreType.DMA((n * N_CHIP,))],
    )(*xs)


def _pair_sum(name, x, got, l, core):
    _, r, c_ = x.shape
    br = _tile(r, 256, 2 * SUBLANE)

    def body(core_ref, x_ref, g_ref, o_ref):
        o_ref[...] = (x_ref[...].astype(F32) + g_ref[...].astype(F32)).astype(o_ref.dtype)

    return pl.pallas_call(
        body, name=name, out_shape=jax.ShapeDtypeStruct((N_CHIP, r, c_), x.dtype),
        grid_spec=pltpu.PrefetchScalarGridSpec(
            num_scalar_prefetch=1, grid=(N_CHIP, r // br),
            in_specs=[pl.BlockSpec((None, br, c_), lambda ch, i, core_ref: (2 * ch + core_ref[0], i, 0)),
                      pl.BlockSpec((None, None, br, c_), lambda ch, i, core_ref: (ch, l, i, 0))],
            out_specs=pl.BlockSpec((None, br, c_), lambda ch, i, core_ref: (ch, i, 0))),
        compiler_params=_params(2))(core, x, got)


def _chip_sums(name, x):
    got = _sibling_exchange(name + "_d2d", [x])
    core = lax.axis_index("c").astype(jnp.int32).reshape(1)
    return _pair_sum(name + "_pair", x, got, 0, core)


def _mm(name, a, b, out_shape, out_dtype, grid, a_spec, b_spec, o_spec, dims, n_red, acc_shape, rider=None):
    red = tuple(range(len(grid) - n_red, len(grid)))

    def body(a_ref, b_ref, o_ref, acc_ref):
        first = functools.reduce(jnp.logical_and, [pl.program_id(ax) == 0 for ax in red])
        last = functools.reduce(jnp.logical_and, [pl.program_id(ax) == grid[ax] - 1 for ax in red])

        @pl.when(first)
        def _():
            acc_ref[...] = jnp.zeros_like(acc_ref)

        acc_ref[...] += _dot(a_ref[...], b_ref[...], dims)

        @pl.when(last)
        def _():
            o_ref[...] = acc_ref[...].astype(o_ref.dtype)

    return _call(body, name=name, out_shape=jax.ShapeDtypeStruct(out_shape, out_dtype), grid=grid,
                 in_specs=[a_spec, b_spec], out_specs=o_spec, scratch_shapes=[pltpu.VMEM(acc_shape, F32)],
                 args=(a, b), rider=rider)


def _mm_col(name, a, b, out_dtype=F32, rider=None):
    m, k = a.shape
    j, _, n = b.shape
    bm, bk = _tile(m, 1024), _tile(k, 512)
    return _mm(name, a, b, (j, m, n), out_dtype, (j, m // bm, k // bk),
               pl.BlockSpec((bm, bk), lambda jj, mm, kk: (mm, kk)),
               pl.BlockSpec((None, bk, n), lambda jj, mm, kk: (jj, kk, 0)),
               pl.BlockSpec((None, bm, n), lambda jj, mm, kk: (jj, mm, 0)), NN, 1, (bm, n), rider)


def _mm_col_da(name, do, b, rider=None):
    j, m, n = do.shape
    k = b.shape[1]
    bm, bk = _tile(m, 1024), _tile(k, 1024)
    return _mm(name, do, b, (m, k), F32, (m // bm, k // bk, j),
               pl.BlockSpec((None, bm, n), lambda mm, kk, jj: (jj, mm, 0)),
               pl.BlockSpec((None, bk, n), lambda mm, kk, jj: (jj, kk, 0)),
               pl.BlockSpec((bm, bk), lambda mm, kk, jj: (mm, kk)), NT, 1, (bm, bk), rider)


def _mm_col_db(name, a, do, out_dtype):
    m, k = a.shape
    j, _, n = do.shape
    bm, bk = _tile(m, 1024), _tile(k, 512)
    return _mm(name, a, do, (j, k, n), out_dtype, (j, k // bk, m // bm),
               pl.BlockSpec((bm, bk), lambda jj, kk, mm: (mm, kk)),
               pl.BlockSpec((None, bm, n), lambda jj, kk, mm: (jj, mm, 0)),
               pl.BlockSpec((None, bk, n), lambda jj, kk, mm: (jj, kk, 0)), TN, 1, (bk, n))


def _row_bk(kq):
    return kq if (kq % LANE or kq // LANE in (11,)) else _tile(kq, 512)


def _mm_row(name, a, b, out_dtype=F32, rider=None):
    q, m, kq = a.shape
    n = b.shape[1]
    bm, bn, bk = _tile(m, 1024), _tile(n, 1024), _row_bk(kq)
    nk = kq // bk
    return _mm(name, a, b, (m, n), out_dtype, (m // bm, n // bn, q, nk),
               pl.BlockSpec((None, bm, bk), lambda mm, nn, qq, kk: (qq, mm, kk)),
               pl.BlockSpec((bk, bn), lambda mm, nn, qq, kk: (qq * nk + kk, nn)),
               pl.BlockSpec((bm, bn), lambda mm, nn, qq, kk: (mm, nn)), NN, 2, (bm, bn), rider)


def _mm_row_da(name, do, b, q):
    m, n = do.shape
    kq = b.shape[0] // q
    bm, bn = _tile(m, 1024), _tile(n, 1024)
    return _mm(name, do, b, (q, m, kq), F32, (q, m // bm, n // bn),
               pl.BlockSpec((bm, bn), lambda qq, mm, nn: (mm, nn)),
               pl.BlockSpec((kq, bn), lambda qq, mm, nn: (qq, nn)),
               pl.BlockSpec((None, bm, kq), lambda qq, mm, nn: (qq, mm, 0)), NT, 1, (bm, kq))


def _mm_row_db(name, a, do, out_dtype):
    q, m, kq = a.shape
    n = do.shape[1]
    bm, bn = _tile(m, 1024), _tile(n, 512)
    return _mm(name, a, do, (q * kq, n), out_dtype, (q, n // bn, m // bm),
               pl.BlockSpec((None, bm, kq), lambda qq, nn, mm: (qq, mm, 0)),
               pl.BlockSpec((bm, bn), lambda qq, nn, mm: (mm, nn)),
               pl.BlockSpec((kq, bn), lambda qq, nn, mm: (qq, nn)), TN, 1, (kq, bn))


def _row_spec(bm, d):
    return pl.BlockSpec((bm, d), lambda i: (i, 0))


def _vec_spec(d):
    return pl.BlockSpec((1, d), lambda i: (0, 0))


def _norm_mod_fwd(name, x, gain, sc, sh):
    t, d = x.shape
    bm = _tile(t, 256, SUBLANE)

    def body(x_ref, g_ref, sc_ref, sh_ref, h_ref):
        xv = x_ref[...]
        rstd = lax.rsqrt(jnp.mean(xv * xv, axis=-1, keepdims=True) + EPS)
        h_ref[...] = ((xv * rstd) * g_ref[...] * (1.0 + sc_ref[...]) + sh_ref[...]).astype(h_ref.dtype)

    return pl.pallas_call(
        body, name=name, out_shape=jax.ShapeDtypeStruct((t, d), MXU_DTYPE), grid=(t // bm,),
        in_specs=[_row_spec(bm, d), _vec_spec(d), _vec_spec(d), _vec_spec(d)], out_specs=_row_spec(bm, d),
        compiler_params=_params(1))(x, gain, sc, sh)


def _norm_mod_bwd(name, x, dh, dres, gain, sc):
    t, d = x.shape
    bm = _tile(t, 256, SUBLANE)

    def body(x_ref, dh_ref, dres_ref, g_ref, sc_ref, dx_ref, dg_ref, dsc_ref, dsh_ref):
        @pl.when(pl.program_id(0) == 0)
        def _():
            dg_ref[...] = jnp.zeros_like(dg_ref)
            dsc_ref[...] = jnp.zeros_like(dsc_ref)
            dsh_ref[...] = jnp.zeros_like(dsh_ref)

        xv, dh_ = x_ref[...], dh_ref[...]
        rstd = lax.rsqrt(jnp.mean(xv * xv, axis=-1, keepdims=True) + EPS)
        nrm = xv * rstd
        gain_ = g_ref[...]
        dsh_ref[...] += jnp.sum(dh_, axis=0, keepdims=True)
        dsc_ref[...] += jnp.sum(dh_ * (nrm * gain_), axis=0, keepdims=True)
        dhn = dh_ * (1.0 + sc_ref[...])
        dg_ref[...] += jnp.sum(dhn * nrm, axis=0, keepdims=True)
        dn = dhn * gain_
        dx_ref[...] = dres_ref[...] + rstd * (dn - nrm * jnp.mean(dn * nrm, axis=-1, keepdims=True))

    vec = jax.ShapeDtypeStruct((1, d), F32)
    return pl.pallas_call(
        body, name=name, out_shape=[jax.ShapeDtypeStruct((t, d), F32), vec, vec, vec], grid=(t // bm,),
        in_specs=[_row_spec(bm, d), _row_spec(bm, d), _row_spec(bm, d), _vec_spec(d), _vec_spec(d)],
        out_specs=[_row_spec(bm, d), _vec_spec(d), _vec_spec(d), _vec_spec(d)],
        compiler_params=_params(1))(x, dh, dres, gain, sc)


def _loss_bwd(name, x, target, gain):
    t, d = x.shape
    bm = _tile(t, 256, SUBLANE)

    def body(x_ref, t_ref, g_ref, dx_ref, loss_ref, dg_ref):
        @pl.when(pl.program_id(0) == 0)
        def _():
            loss_ref[...] = jnp.zeros_like(loss_ref)
            dg_ref[...] = jnp.zeros_like(dg_ref)

        xv = x_ref[...]
        rstd = lax.rsqrt(jnp.mean(xv * xv, axis=-1, keepdims=True) + EPS)
        nrm = xv * rstd
        gain_ = g_ref[...]
        err = nrm * gain_ - t_ref[...]
        per_tok = jnp.mean(err * err, axis=-1, keepdims=True)
        loss_ref[...] += 0.5 * jnp.sum(per_tok, axis=0, keepdims=True)
        dout = err * (1.0 / d)
        dg_ref[...] += jnp.sum(dout * nrm, axis=0, keepdims=True)
        dn = dout * gain_
        dx_ref[...] = rstd * (dn - nrm * jnp.mean(dn * nrm, axis=-1, keepdims=True))

    return pl.pallas_call(
        body, name=name,
        out_shape=[jax.ShapeDtypeStruct((t, d), F32), jax.ShapeDtypeStruct((1, 1), F32),
                   jax.ShapeDtypeStruct((1, d), F32)],
        grid=(t // bm,), in_specs=[_row_spec(bm, d), _row_spec(bm, d), _vec_spec(d)],
        out_specs=[_row_spec(bm, d), pl.BlockSpec((1, 1), lambda i: (0, 0)), _vec_spec(d)],
        compiler_params=_params(1))(x, target, gain)


def _resid(name, x, y, g):
    t, d = x.shape
    bm = _tile(t, 256, SUBLANE)

    def body(x_ref, y_ref, g_ref, o_ref):
        o_ref[...] = x_ref[...] + g_ref[...] * y_ref[...]

    return pl.pallas_call(
        body, name=name, out_shape=jax.ShapeDtypeStruct((t, d), F32), grid=(t // bm,),
        in_specs=[_row_spec(bm, d), _row_spec(bm, d), _vec_spec(d)], out_specs=_row_spec(bm, d),
        compiler_params=_params(1))(x, y, g)


def _gate_bwd(name, dx, y, g):
    t, d = dx.shape
    bm = _tile(t, 256, SUBLANE)

    def body(dx_ref, y_ref, g_ref, dy_ref, dg_ref):
        @pl.when(pl.program_id(0) == 0)
        def _():
            dg_ref[...] = jnp.zeros_like(dg_ref)

        dxv = dx_ref[...]
        dy_ref[...] = (g_ref[...] * dxv).astype(dy_ref.dtype)
        dg_ref[...] += jnp.sum(dxv * y_ref[...], axis=0, keepdims=True)

    return pl.pallas_call(
        body, name=name, out_shape=[jax.ShapeDtypeStruct((t, d), MXU_DTYPE), jax.ShapeDtypeStruct((1, d), F32)],
        grid=(t // bm,), in_specs=[_row_spec(bm, d), _row_spec(bm, d), _vec_spec(d)],
        out_specs=[_row_spec(bm, d), _vec_spec(d)], compiler_params=_params(1))(dx, y, g)


def _glu_resid_fwd(name, z, x, g):
    _, t, n = z.shape
    d = x.shape[1]
    half = N_DEV // 2
    bm = _tile(t, 256, SUBLANE)

    def body(v_ref, gt_ref, x_ref, g_ref, o_ref):
        o_ref[...] = x_ref[...] + g_ref[...] * (v_ref[...] * jax.nn.sigmoid(gt_ref[...]))

    return pl.pallas_call(
        body, name=name, out_shape=jax.ShapeDtypeStruct((t, d), F32), grid=(half, t // bm),
        in_specs=[pl.BlockSpec((None, bm, n), lambda q, i: (q, i, 0)),
                  pl.BlockSpec((None, bm, n), lambda q, i: (q + half, i, 0)),
                  pl.BlockSpec((bm, n), lambda q, i: (i, q)), pl.BlockSpec((1, n), lambda q, i: (0, q))],
        out_specs=pl.BlockSpec((bm, n), lambda q, i: (i, q)), compiler_params=_params(2))(z, z, x, g)


def _glu_resid_bwd(name, z, dx, g):
    _, t, n = z.shape
    d = dx.shape[1]
    half = N_DEV // 2
    bm = _tile(t, 256, SUBLANE)

    def body(v_ref, gt_ref, dx_ref, g_ref, dz_ref, dg_ref):
        part, i = pl.program_id(1), pl.program_id(2)
        v, dxv = v_ref[...], dx_ref[...]
        sig = jax.nn.sigmoid(gt_ref[...])
        dout = g_ref[...] * dxv

        @pl.when(part == 0)
        def _():
            @pl.when(i == 0)
            def _():
                dg_ref[...] = jnp.zeros_like(dg_ref)

            dg_ref[...] += jnp.sum(dxv * (v * sig), axis=0, keepdims=True)
            dz_ref[...] = (dout * sig).astype(dz_ref.dtype)

        @pl.when(part == 1)
        def _():
            dz_ref[...] = (dout * v * (sig * (1.0 - sig))).astype(dz_ref.dtype)

    return pl.pallas_call(
        body, name=name,
        out_shape=[jax.ShapeDtypeStruct((N_DEV, t, n), MXU_DTYPE), jax.ShapeDtypeStruct((1, d), F32)],
        grid=(half, 2, t // bm),
        in_specs=[pl.BlockSpec((None, bm, n), lambda q, p, i: (q, i, 0)),
                  pl.BlockSpec((None, bm, n), lambda q, p, i: (q + half, i, 0)),
                  pl.BlockSpec((bm, n), lambda q, p, i: (i, q)), pl.BlockSpec((1, n), lambda q, p, i: (0, q))],
        out_specs=[pl.BlockSpec((None, bm, n), lambda q, p, i: (q + half * p, i, 0)),
                   pl.BlockSpec((1, n), lambda q, p, i: (0, q))],
        compiler_params=_params(3))(z, z, dx, g)


def _swiglu_act_fwd(name, gu):
    _, t, n = gu.shape
    half = N_DEV // 2
    bm = _tile(t, 256, SUBLANE)

    def body(g_ref, u_ref, o_ref):
        gv = g_ref[...]
        o_ref[...] = (gv * jax.nn.sigmoid(gv) * u_ref[...]).astype(o_ref.dtype)

    return pl.pallas_call(
        body, name=name, out_shape=jax.ShapeDtypeStruct((half, t, n), MXU_DTYPE), grid=(half, t // bm),
        in_specs=[pl.BlockSpec((None, bm, n), lambda q, i: (q, i, 0)),
                  pl.BlockSpec((None, bm, n), lambda q, i: (q + half, i, 0))],
        out_specs=pl.BlockSpec((None, bm, n), lambda q, i: (q, i, 0)), compiler_params=_params(2))(gu, gu)


def _swiglu_act_bwd(name, gu, dact, rider=None):
    _, t, n = gu.shape
    half = N_DEV // 2
    bm = _tile(t, 256, SUBLANE)

    def body(g_ref, u_ref, da_ref, o_ref):
        part = pl.program_id(1)
        gv, da = g_ref[...], da_ref[...]
        sig = jax.nn.sigmoid(gv)

        @pl.when(part == 0)
        def _():
            o_ref[...] = (da * u_ref[...] * (sig * (1.0 + gv * (1.0 - sig)))).astype(o_ref.dtype)

        @pl.when(part == 1)
        def _():
            o_ref[...] = (da * (gv * sig)).astype(o_ref.dtype)

    return _call(
        body, name=name, out_shape=jax.ShapeDtypeStruct((N_DEV, t, n), MXU_DTYPE), grid=(half, 2, t // bm),
        in_specs=[pl.BlockSpec((None, bm, n), lambda q, p, i: (q, i, 0)),
                  pl.BlockSpec((None, bm, n), lambda q, p, i: (q + half, i, 0)),
                  pl.BlockSpec((None, bm, n), lambda q, p, i: (q, i, 0))],
        out_specs=pl.BlockSpec((None, bm, n), lambda q, p, i: (q + half * p, i, 0)),
        args=(gu, gu, dact), rider=rider)


def _ada_fwd(name, c16, w_ada, b_loc):
    nl, d, n = w_ada.shape
    bn = _tile(n, 512)

    def body(c_ref, w_ref, b_ref, o_ref):
        cv = c_ref[...]
        o_ref[...] = _dot(cv * jax.nn.sigmoid(cv), w_ref[...], NN) + b_ref[...]

    return pl.pallas_call(
        body, name=name, out_shape=jax.ShapeDtypeStruct((nl, c16.shape[0], n), F32), grid=(nl, n // bn),
        in_specs=[pl.BlockSpec(c16.shape, lambda i, j: (0, 0)), pl.BlockSpec((None, d, bn), lambda i, j: (i, 0, j)),
                  pl.BlockSpec((None, 1, bn), lambda i, j: (i, 0, j))],
        out_specs=pl.BlockSpec((None, c16.shape[0], bn), lambda i, j: (i, 0, j)),
        compiler_params=_params(2))(c16, w_ada, b_loc)


def _adam_update(g, w, m, v):
    m = ADAM_B1 * m + (1.0 - ADAM_B1) * g
    v = ADAM_B2 * v + (1.0 - ADAM_B2) * (g * g)
    m_hat = m / (1.0 - ADAM_B1 ** ADAM_STEP)
    v_hat = v / (1.0 - ADAM_B2 ** ADAM_STEP)
    delta = -ADAM_LR * (m_hat / (jnp.sqrt(v_hat) + ADAM_EPS) + ADAM_WD * w)
    return delta, m, v


def _adamw_w_ada(name, c16, dmod16, w, m, v, rider=None):
    nl, d, n = w.shape
    br = _tile(d, 256)

    def body(c_ref, dm_ref, w_ref, m_ref, v_ref, g_ref, dl_ref, mo_ref, vo_ref):
        cv = c_ref[...]
        g = _dot(cv * jax.nn.sigmoid(cv), dm_ref[...], TN)
        g_ref[...] = g
        dl_ref[...], mo_ref[...], vo_ref[...] = _adam_update(g, w_ref[...], m_ref[...], v_ref[...])

    blk = pl.BlockSpec((None, br, n), lambda i, r: (i, r, 0))
    shp = jax.ShapeDtypeStruct(w.shape, F32)
    return _call(
        body, name=name, out_shape=[shp] * 4, grid=(nl, d // br),
        in_specs=[pl.BlockSpec((c16.shape[0], br), lambda i, r: (0, r)),
                  pl.BlockSpec((None, dmod16.shape[1], n), lambda i, r: (i, 0, 0)), blk, blk, blk],
        out_specs=[blk] * 4, args=(c16, dmod16, w, m, v), rider=rider)


def _adamw_sum(name, parts, w, m, v, rider=None):
    nl = len(parts)
    p, r, c = parts[0].shape
    br = _tile(r, 128, 2 * SUBLANE)
    nb = r // br

    def body(*refs):
        p_refs, (w_ref, m_ref, v_ref, g_ref, dl_ref, mo_ref, vo_ref) = refs[:nl], refs[nl:]
        layer = pl.program_id(0)
        g = None
        for l, p_ref in enumerate(p_refs):
            gl = p_ref[0].astype(F32)
            for s in range(1, p):
                gl = gl + p_ref[s].astype(F32)
            g = gl if g is None else jnp.where(layer == l, gl, g)
        g_ref[...] = g
        dl_ref[...], mo_ref[...], vo_ref[...] = _adam_update(g, w_ref[...], m_ref[...], v_ref[...])

    blk = pl.BlockSpec((br, c), lambda l, i: (l * nb + i, 0))
    shp = jax.ShapeDtypeStruct((nl * r, c), F32)
    return _call(
        body, name=name, out_shape=[shp] * 4, grid=(nl, nb),
        in_specs=[pl.BlockSpec((p, br, c), lambda l, i: (0, i, 0))] * nl + [blk, blk, blk], out_specs=[blk] * 4,
        args=(*parts, w, m, v), rider=rider)


def _sum_parts(name, parts):
    p, r, c = parts.shape

    def body(p_ref, o_ref):
        g = p_ref[0]
        for s in range(1, p):
            g = g + p_ref[s]
        o_ref[...] = g

    return pl.pallas_call(body, name=name, out_shape=jax.ShapeDtypeStruct((r, c), F32))(parts)


def _s5_disc(name, lam_re, lam_im, log_dt, b_re, b_im):
    def body(lr_ref, li_ref, ld_ref, br_ref, bi_ref, ar_ref, ai_ref, bbr_ref, bbi_ref):
        lr, li = lr_ref[...], li_ref[...]
        dt = jnp.exp(ld_ref[...])
        mag = jnp.exp(lr * dt)
        a_re, a_im = mag * jnp.cos(li * dt), mag * jnp.sin(li * dt)
        nr, ni = a_re - 1.0, a_im
        den = lr * lr + li * li
        f_re, f_im = (nr * lr + ni * li) / den, (ni * lr - nr * li) / den
        br, bi = br_ref[...], bi_ref[...]
        ar_ref[...], ai_ref[...] = a_re, a_im
        bbr_ref[...] = f_re * br - f_im * bi
        bbi_ref[...] = f_re * bi + f_im * br

    s_a, s_b = jax.ShapeDtypeStruct(lam_re.shape, F32), jax.ShapeDtypeStruct(b_re.shape, F32)
    return pl.pallas_call(body, name=name, out_shape=[s_a, s_a, s_b, s_b])(lam_re, lam_im, log_dt, b_re, b_im)


def _s5_disc_bwd(name, lam_re, lam_im, log_dt, b_re, b_im, dab_re, dab_im, dbb_re, dbb_im):
    def body(lr_ref, li_ref, ld_ref, br_ref, bi_ref, dar_ref, dai_ref, dbbr_ref, dbbi_ref,
             dlr_ref, dli_ref, dld_ref, dbr_ref, dbi_ref):
        lr, li = lr_ref[...], li_ref[...]
        dt = jnp.exp(ld_ref[...])
        mag = jnp.exp(lr * dt)
        a_re, a_im = mag * jnp.cos(li * dt), mag * jnp.sin(li * dt)
        nr, ni = a_re - 1.0, a_im
        den = lr * lr + li * li
        f_re, f_im = (nr * lr + ni * li) / den, (ni * lr - nr * li) / den
        br, bi = br_ref[...], bi_ref[...]
        dbbr, dbbi = dbbr_ref[...], dbbi_ref[...]
        dbr_ref[...] = f_re * dbbr + f_im * dbbi
        dbi_ref[...] = f_re * dbbi - f_im * dbbr
        df_re = jnp.sum(dbbr * br + dbbi * bi, axis=1, keepdims=True)
        df_im = jnp.sum(dbbi * br - dbbr * bi, axis=1, keepdims=True)
        dnr = (df_re * lr - df_im * li) / den
        dni = (df_re * li + df_im * lr) / den
        dden = -(df_re * f_re + df_im * f_im) / den
        dlr = (df_re * nr + df_im * ni) / den + 2.0 * lr * dden
        dli = (df_re * ni - df_im * nr) / den + 2.0 * li * dden
        da_re, da_im = dar_ref[...] + dnr, dai_ref[...] + dni
        dmag_mag = da_re * a_re + da_im * a_im
        dth = da_im * a_re - da_re * a_im
        dlr_ref[...] = dlr + dmag_mag * dt
        dli_ref[...] = dli + dth * dt
        ddt = jnp.sum(dmag_mag * lr + dth * li, axis=2, keepdims=True)
        dld_ref[...] = ddt * dt

    s_a, s_b = jax.ShapeDtypeStruct(lam_re.shape, F32), jax.ShapeDtypeStruct(b_re.shape, F32)
    return pl.pallas_call(
        body, name=name, out_shape=[s_a, s_a, jax.ShapeDtypeStruct(log_dt.shape, F32), s_b, s_b],
    )(lam_re, lam_im, log_dt, b_re, b_im, dab_re, dab_im, dbb_re, dbb_im)


def _s5_time_block(t):
    return _tile(t, 128, SUBLANE)


def _s5_scan_fwd(name, u, bb_re, bb_im, ab_re, ab_im, rider=None):
    t, d = u.shape
    nsg, cs, ns = bb_re.shape
    tb = _s5_time_block(t)

    def body(u_ref, bbr_hbm, bbi_hbm, ar_ref, ai_ref, sr_ref, si_ref, srm_ref, sim_ref, bbr, bbi, cr_ref, ci_ref):
        @pl.when(pl.program_id(0) == 0)
        def _():
            pltpu.sync_copy(bbr_hbm, bbr)
            pltpu.sync_copy(bbi_hbm, bbi)
            cr_ref[...] = jnp.zeros_like(cr_ref)
            ci_ref[...] = jnp.zeros_like(ci_ref)

        for sg in range(nsg):
            us = u_ref[:, sg * cs:(sg + 1) * cs]
            sr_ref[:, sg, :] = _dot(us, bbr[sg], NN)
            si_ref[:, sg, :] = _dot(us, bbi[sg], NN)
        ar, ai = ar_ref[...], ai_ref[...]

        def step(i, carry):
            cr, ci = carry
            nr = ar * cr - ai * ci + sr_ref[i]
            ni = ar * ci + ai * cr + si_ref[i]
            sr_ref[i] = nr
            si_ref[i] = ni
            return nr, ni

        cr, ci = lax.fori_loop(0, tb, step, (cr_ref[...], ci_ref[...]), unroll=2)
        cr_ref[...], ci_ref[...] = cr, ci
        for sg in range(nsg):
            srm_ref[sg] = sr_ref[:, sg, :].astype(MXU_DTYPE)
            sim_ref[sg] = si_ref[:, sg, :].astype(MXU_DTYPE)

    scan = jax.ShapeDtypeStruct((t, nsg, ns), F32)
    mxu = jax.ShapeDtypeStruct((nsg, t, ns), MXU_DTYPE)
    hbm = pl.BlockSpec(memory_space=pltpu.HBM)
    full = pl.BlockSpec((nsg, ns), lambda i: (0, 0))
    return _call(
        body, name=name, out_shape=[scan, scan, mxu, mxu], grid=(t // tb,),
        in_specs=[_row_spec(tb, d), hbm, hbm, full, full],
        out_specs=[pl.BlockSpec((tb, nsg, ns), lambda i: (i, 0, 0))] * 2 + [pl.BlockSpec((nsg, tb, ns), lambda i: (0, i, 0))] * 2,
        scratch_shapes=[pltpu.VMEM(bb_re.shape, bb_re.dtype), pltpu.VMEM(bb_im.shape, bb_im.dtype),
                        pltpu.VMEM((nsg, ns), F32), pltpu.VMEM((nsg, ns), F32)],
        args=(u, bb_re, bb_im, ab_re, ab_im), rider=rider)


def _s5_out_fwd(name, s_re, s_im, cc_re, cc_im, u, dskip):
    nsg, t, ns = s_re.shape
    d = u.shape[1]
    cs = cc_re.shape[2]
    tb = _tile(t, 512, SUBLANE)

    def body(sr_ref, si_ref, cr_ref, ci_ref, u_ref, d_ref, yp_ref, ya_ref):
        y = _dot(sr_ref[...], cr_ref[...], NN) - _dot(si_ref[...], ci_ref[...], NN) + d_ref[...] * u_ref[...]
        yp_ref[...] = y
        ya_ref[...] = _gelu(y).astype(ya_ref.dtype)

    s_spec = pl.BlockSpec((None, tb, ns), lambda sg, i: (sg, i, 0))
    c_spec = pl.BlockSpec((None, ns, cs), lambda sg, i: (sg, 0, 0))
    col = pl.BlockSpec((tb, cs), lambda sg, i: (i, sg))
    return pl.pallas_call(
        body, name=name, out_shape=[jax.ShapeDtypeStruct((t, d), F32), jax.ShapeDtypeStruct((t, d), MXU_DTYPE)],
        grid=(nsg, t // tb), in_specs=[s_spec, s_spec, c_spec, c_spec, col, pl.BlockSpec((1, cs), lambda sg, i: (0, sg))],
        out_specs=[col, col], compiler_params=_params(2))(s_re, s_im, cc_re, cc_im, u, dskip)


def _gelu_bwd(name, dy, ypre):
    t, d = dy.shape
    bm = _tile(t, 256, SUBLANE)

    def body(dy_ref, yp_ref, o_ref):
        o_ref[...] = (dy_ref[...] * _gelu_and_grad(yp_ref[...])[1]).astype(o_ref.dtype)

    return pl.pallas_call(
        body, name=name, out_shape=jax.ShapeDtypeStruct((t, d), MXU_DTYPE), grid=(t // bm,),
        in_specs=[_row_spec(bm, d), _row_spec(bm, d)], out_specs=_row_spec(bm, d), compiler_params=_params(1))(dy, ypre)


def _s5_scan_bwd(name, dyp, cc_re, cc_im, ab_re, ab_im, s_re, s_im, rider=None):
    t, d = dyp.shape
    nsg, ns, cs = cc_re.shape
    tb = _s5_time_block(t)
    nb = t // tb

    def body(dy_ref, ccr_hbm, cci_hbm, ar_ref, ai_ref, sr_ref, si_ref, lrm_ref, lim_ref, dar_ref, dai_ref,
             ccr, cci, lr_ref, li_ref, cr_ref, ci_ref):
        @pl.when(pl.program_id(0) == 0)
        def _():
            pltpu.sync_copy(ccr_hbm, ccr)
            pltpu.sync_copy(cci_hbm, cci)
            cr_ref[...] = jnp.zeros_like(cr_ref)
            ci_ref[...] = jnp.zeros_like(ci_ref)
            dar_ref[...] = jnp.zeros_like(dar_ref)
            dai_ref[...] = jnp.zeros_like(dai_ref)

        for sg in range(nsg):
            dys = dy_ref[:, sg * cs:(sg + 1) * cs]
            lr_ref[:, sg, :] = _dot(dys, ccr[sg], NT)
            li_ref[:, sg, :] = -_dot(dys, cci[sg], NT)
        ar, ai = ar_ref[...], ai_ref[...]

        def step(i, carry):
            cr, ci, dar, dai = carry
            j = tb - 1 - i
            sr, si = sr_ref[j], si_ref[j]
            dar = dar + (cr * sr + ci * si)
            dai = dai + (ci * sr - cr * si)
            nr = lr_ref[j] + (ar * cr + ai * ci)
            ni = li_ref[j] + (ar * ci - ai * cr)
            lr_ref[j] = nr
            li_ref[j] = ni
            return nr, ni, dar, dai

        cr, ci, dar, dai = lax.fori_loop(0, tb, step, (cr_ref[...], ci_ref[...], dar_ref[...], dai_ref[...]))
        cr_ref[...], ci_ref[...] = cr, ci
        dar_ref[...], dai_ref[...] = dar, dai
        for sg in range(nsg):
            lrm_ref[sg] = lr_ref[:, sg, :].astype(MXU_DTYPE)
            lim_ref[sg] = li_ref[:, sg, :].astype(MXU_DTYPE)

    hbm = pl.BlockSpec(memory_space=pltpu.HBM)
    full = pl.BlockSpec((nsg, ns), lambda i: (0, 0))
    mxu = jax.ShapeDtypeStruct((nsg, t, ns), MXU_DTYPE)
    acc = jax.ShapeDtypeStruct((nsg, ns), F32)
    scan_spec = pl.BlockSpec((tb, nsg, ns), lambda i: (nb - 1 - i, 0, 0))
    return _call(
        body, name=name, out_shape=[mxu, mxu, acc, acc], grid=(nb,),
        in_specs=[pl.BlockSpec((tb, d), lambda i: (nb - 1 - i, 0)), hbm, hbm, full, full, scan_spec, scan_spec],
        out_specs=[pl.BlockSpec((nsg, tb, ns), lambda i: (0, nb - 1 - i, 0))] * 2 + [full, full],
        scratch_shapes=[pltpu.VMEM(cc_re.shape, cc_re.dtype), pltpu.VMEM(cc_im.shape, cc_im.dtype),
                        pltpu.VMEM((tb, nsg, ns), F32), pltpu.VMEM((tb, nsg, ns), F32),
                        pltpu.VMEM((nsg, ns), F32), pltpu.VMEM((nsg, ns), F32)],
        args=(dyp, cc_re, cc_im, ab_re, ab_im, s_re, s_im), rider=rider)


def _s5_grads(name, lam_re, lam_im, s_re, s_im, u, dyp, bb_re, bb_im, dskip, rider=None):
    nsg, t, ns = lam_re.shape
    d = u.shape[1]
    cs = bb_re.shape[1]
    tb = _tile(t, 512, SUBLANE)

    def body(lr_ref, li_ref, sr_ref, si_ref, u_ref, dy_ref, bbr_ref, bbi_ref, d_ref,
             du_ref, dbbr_ref, dbbi_ref, dccr_ref, dcci_ref, dd_ref):
        @pl.when(pl.program_id(1) == 0)
        def _():
            for r in (dbbr_ref, dbbi_ref, dccr_ref, dcci_ref, dd_ref):
                r[...] = jnp.zeros_like(r)

        lr, li, uv, dy = lr_ref[...], li_ref[...], u_ref[...], dy_ref[...]
        dyf = dy.astype(F32)
        du_ref[...] = _dot(lr, bbr_ref[...], NT) + _dot(li, bbi_ref[...], NT) + d_ref[...] * dyf
        dbbr_ref[...] += _dot(uv, lr, TN)
        dbbi_ref[...] += _dot(uv, li, TN)
        dccr_ref[...] += _dot(sr_ref[...], dy, TN)
        dcci_ref[...] -= _dot(si_ref[...], dy, TN)
        dd_ref[...] += jnp.sum(dyf * uv, axis=0, keepdims=True)

    s_spec = pl.BlockSpec((None, tb, ns), lambda sg, i: (sg, i, 0))
    col = pl.BlockSpec((tb, cs), lambda sg, i: (i, sg))
    b_spec = pl.BlockSpec((None, cs, ns), lambda sg, i: (sg, 0, 0))
    c_spec = pl.BlockSpec((None, ns, cs), lambda sg, i: (sg, 0, 0))
    vec = pl.BlockSpec((1, cs), lambda sg, i: (0, sg))
    return _call(
        body, name=name,
        out_shape=[jax.ShapeDtypeStruct((t, d), F32), jax.ShapeDtypeStruct(bb_re.shape, F32),
                   jax.ShapeDtypeStruct(bb_re.shape, F32), jax.ShapeDtypeStruct((nsg, ns, cs), F32),
                   jax.ShapeDtypeStruct((nsg, ns, cs), F32), jax.ShapeDtypeStruct((1, d), F32)],
        grid=(nsg, t // tb), in_specs=[s_spec, s_spec, s_spec, s_spec, col, col, b_spec, b_spec, vec],
        out_specs=[col, b_spec, b_spec, c_spec, c_spec, vec],
        args=(lam_re, lam_im, s_re, s_im, u, dyp, bb_re, bb_im, dskip), rider=rider)


def _shift_down(x, k, prev8):
    if k == 0:
        return x
    ext = jnp.concatenate([prev8, x], axis=0)
    return ext[SUBLANE - k:SUBLANE - k + x.shape[0]]


def _shift_up(x, k, next8):
    if k == 0:
        return x
    ext = jnp.concatenate([x, next8], axis=0)
    return ext[k:k + x.shape[0]]


def _lru_time_block(t):
    return _tile(t, 256, SUBLANE)


def _lru_gates(xp, prev8, cv_ref, wrg, wig):
    taps = cv_ref.shape[0] - 4
    row = lambda k: cv_ref[k:k + 1, :]
    xs = [_shift_down(xp, taps - 1 - k, prev8) for k in range(taps)]
    xb = row(taps)
    for k in range(taps):
        xb = xb + row(k) * xs[k]
    r = jax.nn.sigmoid(_dot(xb, wrg, NN) + row(taps + 1))
    ig = jax.nn.sigmoid(_dot(xb, wig, NN) + row(taps + 2))
    sp = jax.nn.softplus(-row(taps + 3))
    log_a = -LRU_C * r * sp
    a = jnp.exp(log_a)
    mult = jnp.sqrt(_neg_expm1(2.0 * log_a))
    return xs, xb, r, ig, sp, a, mult


def _lru_fwd(name, zz, cvec, wrg, wig, rider=None):
    _, t, w = zz.shape
    half = N_DEV // 2
    tb = _lru_time_block(t)

    def body(gb_ref, xp_ref, xprev_ref, cv_ref, wrg_ref, wig_ref, hs_ref, y_ref, a_scr, b_scr, carry):
        i = pl.program_id(1)

        @pl.when(i == 0)
        def _():
            carry[...] = jnp.zeros_like(carry)

        prev8 = jnp.where(i > 0, xprev_ref[...], 0.0)
        _, xb, _, ig, _, a, mult = _lru_gates(xp_ref[...], prev8, cv_ref, wrg_ref[...], wig_ref[...])
        a_scr[...] = a
        b_scr[...] = mult * (ig * xb)

        def step(j, h):
            h = a_scr[pl.ds(j, 1), :] * h + b_scr[pl.ds(j, 1), :]
            hs_ref[pl.ds(j, 1), :] = h
            return h

        carry[0:1, :] = lax.fori_loop(0, tb, step, carry[0:1, :], unroll=8)
        y_ref[...] = (hs_ref[...] * _gelu(gb_ref[...])).astype(y_ref.dtype)

    nrow = tb // SUBLANE
    blk = lambda off: pl.BlockSpec((None, tb, w), lambda q, i: (q + off, i, 0))
    return _call(
        body, name=name,
        out_shape=[jax.ShapeDtypeStruct((half, t, w), F32), jax.ShapeDtypeStruct((half, t, w), MXU_DTYPE)],
        grid=(half, t // tb),
        in_specs=[blk(0), blk(half),
                  pl.BlockSpec((None, SUBLANE, w), lambda q, i: (q + half, jnp.maximum(i * nrow - 1, 0), 0)),
                  pl.BlockSpec((None,) + cvec.shape[1:], lambda q, i: (q, 0, 0)),
                  pl.BlockSpec((None, w, w), lambda q, i: (q, 0, 0)), pl.BlockSpec((None, w, w), lambda q, i: (q, 0, 0))],
        out_specs=[blk(0), blk(0)],
        scratch_shapes=[pltpu.VMEM((tb, w), F32), pltpu.VMEM((tb, w), F32), pltpu.VMEM((SUBLANE, w), F32)],
        args=(zz, zz, zz, cvec, wrg, wig), rider=rider)


def _lru_bwd(name, zz, hs, dy, cvec, wrg, wig, rider=None):
    _, t, w = zz.shape
    half = N_DEV // 2
    tb = _lru_time_block(t)
    nb = t // tb
    taps = cvec.shape[1] - 4

    def body(gb_ref, xp_ref, xprev_ref, hs_ref, hprev_ref, dy_ref, cv_ref, wrg_ref, wig_ref,
             dgb_ref, dxp_ref, dcv_ref, dwrg_ref, dwig_ref, a_scr, l_scr, carry, dxb_next):
        i = pl.program_id(1)

        @pl.when(i == 0)
        def _():
            for r_ in (carry, dxb_next, dcv_ref, dwrg_ref, dwig_ref):
                r_[...] = jnp.zeros_like(r_)

        has_prev = i < nb - 1
        row = lambda k: cv_ref[k:k + 1, :]
        prev8 = jnp.where(has_prev, xprev_ref[...], 0.0)
        xs, xb, r, ig, sp, a, mult = _lru_gates(xp_ref[...], prev8, cv_ref, wrg_ref[...], wig_ref[...])
        hs_ = hs_ref[...]
        hs_m1 = _shift_down(hs_, 1, jnp.where(has_prev, hprev_ref[...], 0.0))
        gel, dgel = _gelu_and_grad(gb_ref[...])
        dy_ = dy_ref[...]
        dgb_ref[...] = (dy_ * hs_ * dgel).astype(dgb_ref.dtype)
        a_scr[...] = a
        l_scr[...] = dy_ * gel

        def step(k, c):
            j = tb - 1 - k
            lam = l_scr[pl.ds(j, 1), :] + c
            l_scr[pl.ds(j, 1), :] = lam
            return a_scr[pl.ds(j, 1), :] * lam

        carry[0:1, :] = lax.fori_loop(0, tb, step, carry[0:1, :], unroll=8)
        lam = l_scr[...]
        dmult = lam * (ig * xb)
        dig = lam * (mult * xb)
        dxb = lam * (mult * ig)
        dlog_a = (lam * hs_m1) * a - dmult * (a * a) / mult
        dr = dlog_a * (-LRU_C * sp)
        dsp = jnp.sum(dlog_a * (-LRU_C * r), axis=0, keepdims=True)
        dpr = dr * (r * (1.0 - r))
        dpi = dig * (ig * (1.0 - ig))
        dwrg_ref[...] += _dot(xb, dpr, TN)
        dwig_ref[...] += _dot(xb, dpi, TN)
        dxb = dxb + _dot(dpr, wrg_ref[...], NT) + _dot(dpi, wig_ref[...], NT)
        for k in range(taps):
            dcv_ref[k:k + 1, :] += jnp.sum(dxb * xs[k], axis=0, keepdims=True)
        dcv_ref[taps:taps + 1, :] += jnp.sum(dxb, axis=0, keepdims=True)
        dcv_ref[taps + 1:taps + 2, :] += jnp.sum(dpr, axis=0, keepdims=True)
        dcv_ref[taps + 2:taps + 3, :] += jnp.sum(dpi, axis=0, keepdims=True)
        dcv_ref[taps + 3:taps + 4, :] += dsp * (-jax.nn.sigmoid(-row(taps + 3)))
        nxt8 = dxb_next[...]
        dxp = row(taps - 1) * dxb
        for k in range(taps - 1):
            dxp = dxp + row(k) * _shift_up(dxb, taps - 1 - k, nxt8)
        dxp_ref[...] = dxp.astype(dxp_ref.dtype)
        dxb_next[...] = dxb[0:SUBLANE]

    nrow = tb // SUBLANE
    blk = lambda off: pl.BlockSpec((None, tb, w), lambda q, i: (q + off, nb - 1 - i, 0))
    halo = lambda off: pl.BlockSpec((None, SUBLANE, w), lambda q, i: (q + off, jnp.maximum((nb - 1 - i) * nrow - 1, 0), 0))
    wspec = pl.BlockSpec((None, w, w), lambda q, i: (q, 0, 0))
    cspec = pl.BlockSpec((None,) + cvec.shape[1:], lambda q, i: (q, 0, 0))
    act = jax.ShapeDtypeStruct((half, t, w), MXU_DTYPE)
    return _call(
        body, name=name,
        out_shape=[act, act, jax.ShapeDtypeStruct(cvec.shape, F32), jax.ShapeDtypeStruct(wrg.shape, F32),
                   jax.ShapeDtypeStruct(wig.shape, F32)],
        grid=(half, nb),
        in_specs=[blk(0), blk(half), halo(half), blk(0), halo(0), blk(0), cspec, wspec, wspec],
        out_specs=[blk(0), blk(0), cspec, wspec, wspec],
        scratch_shapes=[pltpu.VMEM((tb, w), F32), pltpu.VMEM((tb, w), F32), pltpu.VMEM((SUBLANE, w), F32),
                        pltpu.VMEM((SUBLANE, w), F32)],
        args=(zz, zz, zz, hs, hs, dy, cvec, wrg, wig), rider=rider)


def _band(blocks, per):
    n, a, b = blocks.shape
    eye = jnp.eye(per, dtype=blocks.dtype)
    x = blocks.reshape(n // per, per, a, b)
    return jnp.einsum('sgab,gh->sgahb', x, eye).reshape(n // per, per * a, per * b)


def _unband(bands, per):
    s, pa, pb = bands.shape
    a, b = pa // per, pb // per
    x = bands.reshape(s, per, a, per, b)
    idx = jnp.arange(per)
    return x[:, idx, :, idx, :].transpose(1, 0, 2, 3).reshape(s * per, a, b)


def _pack(arrays, rows_multiple, lanes=LANE):
    flat = jnp.concatenate([a.reshape(-1).astype(F32) for a in arrays])
    rows = -(-flat.shape[0] // (lanes * rows_multiple)) * rows_multiple
    return jnp.pad(flat, (0, rows * lanes - flat.shape[0])).reshape(rows, lanes)


def _unpack(packed, shapes):
    flat = packed.reshape(-1)
    out, off = [], 0
    for s in shapes:
        n = math.prod(s)
        out.append(flat[off:off + n].reshape(s))
        off += n
    return out


def kernel(x, c, norm_g, w_ada, b_ada, s5_w_in, s5_lam_re, s5_lam_im, s5_log_dt, s5_b_re, s5_b_im, s5_c_re, s5_c_im, s5_d, s5_w_glu, lru_w_in, lru_conv_w, lru_conv_b, lru_w_rg, lru_b_rg, lru_w_ig, lru_b_ig, lru_lam, lru_w_out, ffn_w_gu, ffn_w_down, final_g, loss_target, m_norm_g, m_w_ada, m_b_ada, m_s5_w_in, m_s5_lam_re, m_s5_lam_im, m_s5_log_dt, m_s5_b_re, m_s5_b_im, m_s5_c_re, m_s5_c_im, m_s5_d, m_s5_w_glu, m_lru_w_in, m_lru_conv_w, m_lru_conv_b, m_lru_w_rg, m_lru_b_rg, m_lru_w_ig, m_lru_b_ig, m_lru_lam, m_lru_w_out, m_ffn_w_gu, m_ffn_w_down, m_final_g, v_norm_g, v_w_ada, v_b_ada, v_s5_w_in, v_s5_lam_re, v_s5_lam_im, v_s5_log_dt, v_s5_b_re, v_s5_b_im, v_s5_c_re, v_s5_c_im, v_s5_d, v_s5_w_glu, v_lru_w_in, v_lru_conv_w, v_lru_conv_b, v_lru_w_rg, v_lru_b_rg, v_lru_w_ig, v_lru_b_ig, v_lru_lam, v_lru_w_out, v_ffn_w_gu, v_ffn_w_down, v_final_g):
    wv = dict(zip(WEIGHTS, (norm_g, w_ada, b_ada, s5_w_in, s5_lam_re, s5_lam_im, s5_log_dt, s5_b_re, s5_b_im, s5_c_re, s5_c_im, s5_d, s5_w_glu, lru_w_in, lru_conv_w, lru_conv_b, lru_w_rg, lru_b_rg, lru_w_ig, lru_b_ig, lru_lam, lru_w_out, ffn_w_gu, ffn_w_down, final_g)))
    mv = dict(zip(WEIGHTS, (m_norm_g, m_w_ada, m_b_ada, m_s5_w_in, m_s5_lam_re, m_s5_lam_im, m_s5_log_dt, m_s5_b_re, m_s5_b_im, m_s5_c_re, m_s5_c_im, m_s5_d, m_s5_w_glu, m_lru_w_in, m_lru_conv_w, m_lru_conv_b, m_lru_w_rg, m_lru_b_rg, m_lru_w_ig, m_lru_b_ig, m_lru_lam, m_lru_w_out, m_ffn_w_gu, m_ffn_w_down, m_final_g)))
    vv = dict(zip(WEIGHTS, (v_norm_g, v_w_ada, v_b_ada, v_s5_w_in, v_s5_lam_re, v_s5_lam_im, v_s5_log_dt, v_s5_b_re, v_s5_b_im, v_s5_c_re, v_s5_c_im, v_s5_d, v_s5_w_glu, v_lru_w_in, v_lru_conv_w, v_lru_conv_b, v_lru_w_rg, v_lru_b_rg, v_lru_w_ig, v_lru_b_ig, v_lru_lam, v_lru_w_out, v_ffn_w_gu, v_ffn_w_down, v_final_g)))

    me = 4 * lax.axis_index("x") + 2 * lax.axis_index("y") + lax.axis_index("c")
    x0 = x[0]
    tgt = loss_target[0]
    t, d = x0.shape
    depth = norm_g.shape[0]
    n_mod = w_ada.shape[2] * N_DEV // d
    groups, states = s5_lam_re.shape[1], s5_lam_re.shape[2]
    per_sg = S5_SUPER // S5_GROUP
    nsg = groups // per_sg
    lw = lru_lam.shape[1] * N_DEV
    lwc = lw // (N_DEV // 2)
    half = N_DEV // 2

    assert depth == 2, "the ride schedule below is written for one S5 layer followed by one RG-LRU layer"
    wire = lambda a: a.astype(WIRE_DTYPE)
    gw = {'s5_in': _all_gather("ag_s5_w_in", wire(s5_w_in[0]))}

    def riding(job, fn, *args):
        res, (got,) = fn(*args, rider=_gather_rider([wire(job[1])]))
        gw[job[0]] = got
        return res

    sh_shapes = [wv[n].shape for n in SMALL_SHARDED] + [c.shape]
    sh_all = _all_gather("ag_small", _pack([wv[n] for n in SMALL_SHARDED] + [c], SUBLANE))
    sh_parts = [jnp.stack(p) for p in zip(*[_unpack(sh_all[s], sh_shapes) for s in range(N_DEV)])]
    full = {}
    for n, p in zip(SMALL_SHARDED, sh_parts[:-1]):
        full[n] = jnp.moveaxis(p, 0, -2).reshape(p.shape[1:-1] + (-1,))
    c_all = sh_parts[-1].reshape(N_DEV, d)
    c16 = jnp.pad(c_all, ((0, 2 * SUBLANE - N_DEV), (0, 0)))

    n_loc = w_ada.shape[2]
    b_loc = lax.dynamic_slice_in_dim(b_ada, me * n_loc, n_loc, axis=1)[:, None, :]
    mod_part = _ada_fwd("ada_fwd", c16, w_ada, b_loc)[:, :N_DEV]
    mod_mine = _chunk_exchange("x_mod", [mod_part.transpose(1, 0, 2)], ALL)
    mod = mod_mine.transpose(1, 0, 2).reshape(depth, n_mod, 1, d)

    lam3 = lambda a: a[0][:, None, :]
    p_lr, p_li, p_ld = lam3(s5_lam_re), lam3(s5_lam_im), s5_log_dt[0][:, None, None]
    p_br, p_bi = s5_b_re[0].transpose(0, 2, 1), s5_b_im[0].transpose(0, 2, 1)
    ab_re3, ab_im3, bb_re3, bb_im3 = _s5_disc("s5_disc", p_lr, p_li, p_ld, p_br, p_bi)
    ab_re, ab_im = ab_re3.reshape(nsg, per_sg * states), ab_im3.reshape(nsg, per_sg * states)
    bb_re, bb_im = _band(wire(bb_re3), per_sg), _band(wire(bb_im3), per_sg)
    cc_re = _band(wire(s5_c_re[0].transpose(0, 2, 1)), per_sg)
    cc_im = _band(wire(s5_c_im[0].transpose(0, 2, 1)), per_sg)

    taps = lru_conv_w.shape[1]
    cvec = jnp.concatenate([full['lru_conv_w'].reshape(taps, lw), full['lru_conv_b'], full['lru_b_rg'],
                            full['lru_b_ig'], full['lru_lam']], axis=0)
    cvec = cvec.reshape(taps + 4, half, lwc).transpose(1, 0, 2)
    wrg = _band(wire(lru_w_rg[0]), LRU_BLOCKS_PER_CHUNK)
    wig = _band(wire(lru_w_ig[0]), LRU_BLOCKS_PER_CHUNK)

    saved = []
    xc = x0
    for i in range(depth):
        sh1, sc1, g1, sh2, sc2, g2 = [mod[i, k] for k in range(n_mod)]
        gn = full['norm_g'][i]
        h1 = _norm_mod_fwd(f"norm1_fwd{i}", xc, gn[0:1], sc1, sh1)
        if i % 2 == 0:
            u = riding(('s5_glu', s5_w_glu[0]), _mm_row, f"s5_in{i}", h1[None], gw['s5_in'].reshape(d, d))
            s_re, s_im, s_rem, s_imm = riding((('gu', i), ffn_w_gu[i]), _s5_scan_fwd, f"s5_scan{i}", u, bb_re, bb_im,
                                              ab_re, ab_im)
            ypre, yact = _s5_out_fwd(f"s5_out{i}", s_rem, s_imm, cc_re, cc_im, u, s5_d)
            z = riding((('down', i), ffn_w_down[i]), _mm_col, f"s5_glu{i}", yact, gw['s5_glu'])
            x1 = _glu_resid_fwd(f"s5_resid{i}", z, xc, g1)
            mix = (u, s_re, s_im, s_rem, s_imm, ypre, yact, z)
        else:
            zz = _mm_col(f"lru_in{i}", h1, gw['lru_in'])
            hs, ylru = riding((('gu', i), ffn_w_gu[i]), _lru_fwd, f"lru_core{i}", zz, cvec, wrg, wig)
            o = _mm_row(f"lru_out{i}", ylru, gw['lru_out'].reshape(lw, d))
            x1 = _resid(f"lru_resid{i}", xc, o, g1)
            mix = (zz, hs, ylru, o)
        h2 = _norm_mod_fwd(f"norm2_fwd{i}", x1, gn[1:2], sc2, sh2)
        if i % 2 == 0:
            gu = riding(('lru_in', lru_w_in[0]), _mm_col, f"ffn_gu{i}", h2, gw['gu', i])
            act = _swiglu_act_fwd(f"ffn_act{i}", gu)
            f = riding(('lru_out', lru_w_out[0]), _mm_row, f"ffn_down{i}", act, gw['down', i].reshape(-1, d))
        else:
            gu = riding((('down', i), ffn_w_down[i]), _mm_col, f"ffn_gu{i}", h2, gw['gu', i])
            act = _swiglu_act_fwd(f"ffn_act{i}", gu)
            f = _mm_row(f"ffn_down{i}", act, gw['down', i].reshape(-1, d))
        x2 = _resid(f"ffn_resid{i}", x1, f, g2)
        saved.append((xc, h1, mix, x1, h2, gu, act, f))
        xc = x2

    dx, loss_part, d_final_g = _loss_bwd("loss", xc, tgt, final_g[None])
    loss = lax.psum(loss_part[0, 0], ("x", "y", "c"))

    grads = {}
    parts = {}
    dmod = [None] * depth
    d_norm_g = [None] * depth

    def chip_sums(name, partial):
        return _chip_sums(name, partial.reshape(N_DEV, -1, partial.shape[-1]))

    def riding_x(key, sums, fn, *args):
        res, parts[key] = fn(*args, rider=_chunk_rider([sums], SAME_CORE))
        return res

    waiting = None
    for i in reversed(range(depth)):
        xin, h1, mix, x1, h2, gu, act, f = saved[i]
        sh1, sc1, g1, sh2, sc2, g2 = [mod[i, k] for k in range(n_mod)]
        gn = full['norm_g'][i]
        g_down = gw['down', i].reshape(-1, d)
        df, dg2 = _gate_bwd(f"ffn_gate_bwd{i}", dx, f, g2)
        dact = _mm_row_da(f"ffn_down_da{i}", df, g_down, half)
        s_down = chip_sums(f"x_ffn_w_down{i}", _mm_row_db(f"ffn_down_db{i}", act, df, WIRE_DTYPE))
        if waiting is None:
            dgu = riding_x(('ffn_w_down', i), s_down, _swiglu_act_bwd, f"ffn_act_bwd{i}", gu, dact)
            dh2 = _mm_col_da(f"ffn_gu_da{i}", dgu, gw['gu', i])
        else:
            dgu = riding_x(*waiting, _swiglu_act_bwd, f"ffn_act_bwd{i}", gu, dact)
            dh2 = riding_x(('ffn_w_down', i), s_down, _mm_col_da, f"ffn_gu_da{i}", dgu, gw['gu', i])
        s_gu = chip_sums(f"x_ffn_w_gu{i}", _mm_col_db(f"ffn_gu_db{i}", h2, dgu, WIRE_DTYPE))
        dx, dgn2, dsc2, dsh2 = _norm_mod_bwd(f"norm2_bwd{i}", x1, dh2, dx, gn[1:2], sc2)
        if i % 2 == 0:
            u, s_re, s_im, s_rem, s_imm, ypre, yact, z = mix
            dz, dg1 = _glu_resid_bwd(f"s5_resid_bwd{i}", z, dx, g1)
            dyact = _mm_col_da(f"s5_glu_da{i}", dz, gw['s5_glu'])
            s_glu = chip_sums("x_s5_w_glu", _mm_col_db(f"s5_glu_db{i}", yact, dz, WIRE_DTYPE))
            dyp = _gelu_bwd(f"s5_gelu_bwd{i}", dyact, ypre)
            l_rem, l_imm, dab_re, dab_im = riding_x(('ffn_w_gu', i), s_gu, _s5_scan_bwd, f"s5_scan_bwd{i}", dyp, cc_re,
                                                    cc_im, ab_re, ab_im, s_re, s_im)
            du, dbb_re, dbb_im, dcc_re, dcc_im, dd = riding_x(('s5_w_glu', 0), s_glu, _s5_grads, f"s5_grads{i}", l_rem,
                                                              l_imm, s_rem, s_imm, u, dyp, bb_re, bb_im, s5_d)
            dlr, dli, dld, dbr, dbi = _s5_disc_bwd(
                "s5_disc_bwd", p_lr, p_li, p_ld, p_br, p_bi, dab_re.reshape(groups, 1, states),
                dab_im.reshape(groups, 1, states), _unband(dbb_re, per_sg), _unband(dbb_im, per_sg))
            grads['s5_lam_re'], grads['s5_lam_im'], grads['s5_log_dt'] = dlr[:, 0][None], dli[:, 0][None], dld[:, 0, 0][None]
            grads['s5_b_re'], grads['s5_b_im'] = dbr.transpose(0, 2, 1)[None], dbi.transpose(0, 2, 1)[None]
            grads['s5_c_re'] = _unband(dcc_re, per_sg).transpose(0, 2, 1)[None]
            grads['s5_c_im'] = _unband(dcc_im, per_sg).transpose(0, 2, 1)[None]
            grads['s5_d'] = dd
            dub = du.astype(MXU_DTYPE)
            dh1 = _mm_row_da(f"s5_in_da{i}", dub, gw['s5_in'].reshape(d, d), 1)[0]
            s_s5_in = chip_sums("x_s5_w_in", _mm_row_db(f"s5_in_db{i}", h1[None], dub, WIRE_DTYPE))
        else:
            zz, hs, ylru, o = mix
            g_lru_out = gw['lru_out'].reshape(lw, d)
            do, dg1 = _gate_bwd(f"lru_gate_bwd{i}", dx, o, g1)
            dyl = _mm_row_da(f"lru_out_da{i}", do, g_lru_out, half)
            s_lru_out = chip_sums("x_lru_w_out", _mm_row_db(f"lru_out_db{i}", ylru, do, WIRE_DTYPE))
            dgb, dxp, dcv, dwrg, dwig = riding_x(('ffn_w_gu', i), s_gu, _lru_bwd, f"lru_core_bwd{i}", zz, hs, dyl, cvec,
                                                 wrg, wig)
            dzz = jnp.concatenate([dgb, dxp], axis=0)
            dh1 = riding_x(('lru_w_out', 0), s_lru_out, _mm_col_da, f"lru_in_da{i}", dzz, gw['lru_in'])
            waiting = (('lru_w_in', 0), chip_sums("x_lru_w_in", _mm_col_db(f"lru_in_db{i}", h1, dzz, WIRE_DTYPE)))
            dcv = dcv.transpose(1, 0, 2).reshape(taps + 4, lw)
            grads['lru_conv_w'] = dcv[:taps].reshape(1, taps, 1, lw)
            grads['lru_conv_b'], grads['lru_b_rg'] = dcv[taps:taps + 1], dcv[taps + 1:taps + 2]
            grads['lru_b_ig'], grads['lru_lam'] = dcv[taps + 2:taps + 3], dcv[taps + 3:taps + 4]
            grads['lru_w_rg'] = _unband(dwrg, LRU_BLOCKS_PER_CHUNK)[None]
            grads['lru_w_ig'] = _unband(dwig, LRU_BLOCKS_PER_CHUNK)[None]
        dx, dgn1, dsc1, dsh1 = _norm_mod_bwd(f"norm1_bwd{i}", xin, dh1, dx, gn[0:1], sc1)
        dmod[i] = jnp.concatenate([dsh1, dsc1, dg1, dsh2, dsc2, dg2], axis=1)
        d_norm_g[i] = jnp.concatenate([dgn1, dgn2], axis=0)
    grad_x = dx[None]
    dmod = jnp.concatenate(dmod, axis=0)
    grads['norm_g'] = jnp.stack(d_norm_g)
    grads['b_ada'] = dmod
    grads['final_g'] = d_final_g[0]

    small_partial = _pack([grads[n] for n in SMALL], SUBLANE * N_DEV)
    rows8 = small_partial.shape[0] // N_DEV
    s_small = _chip_sums("x_small", small_partial.reshape(N_DEV, rows8, LANE))

    out = {}
    dmod_all = _all_gather("ag_dmod", dmod)
    dmod_loc = lax.dynamic_slice_in_dim(dmod_all, me * n_loc, n_loc, axis=2).transpose(1, 0, 2)
    dmod16 = jnp.pad(dmod_loc, ((0, 0), (0, 2 * SUBLANE - N_DEV), (0, 0)))
    out['w_ada'] = riding_x(('s5_w_in', 0), s_s5_in, _adamw_w_ada, "adamw_w_ada", c16, dmod16, w_ada, m_w_ada, v_w_ada)

    def big(name, rider=None):
        w = wv[name]
        rows, cols = w.shape[-2] * w.shape[0], w.shape[-1]
        flat = lambda a: a.reshape(rows, cols)
        res = _adamw_sum("adamw_" + name, [parts[name, l] for l in range(w.shape[0])], flat(w), flat(mv[name]),
                         flat(vv[name]), rider=rider)
        res, extra = res if rider is not None else (res, None)
        out[name] = [r.reshape(w.shape) for r in res]
        return extra

    got = big('ffn_w_gu', _chunk_rider([s_small], SAME_CORE))
    summed = _sum_parts("sum_small", got)
    small_total = big('ffn_w_down', _gather_rider([summed]))[0].reshape(-1, LANE)
    big('lru_w_in')
    big('lru_w_out')
    big('s5_w_glu')
    big('s5_w_in')
    small_grad = dict(zip(SMALL, _unpack(small_total, [grads[n].shape for n in SMALL])))
    for n in SMALL_SHARDED:
        shard = wv[n].shape[-1]
        small_grad[n] = lax.dynamic_slice_in_dim(small_grad[n], me * shard, shard, axis=small_grad[n].ndim - 1)
    small_shapes = [wv[n].shape for n in SMALL]
    pk = lambda dct: _pack([dct[n] for n in SMALL], 2 * SUBLANE, 8 * LANE)
    s_out = _adamw_sum("adamw_small", [pk(small_grad)[None]], pk(wv), pk(mv), pk(vv))
    out.update({n: r for n, *r in zip(SMALL, *[_unpack(o, small_shapes) for o in s_out])})

    return (loss, grad_x, *[out[n][0] for n in WEIGHTS], *[out[n][1] for n in WEIGHTS],
            *[out[n][2] for n in WEIGHTS], *[out[n][3] for n in WEIGHTS])
```

```python
import functools
import math

import jax
import jax.numpy as jnp
from jax import lax
from jax.experimental import pallas as pl
from jax.experimental.pallas import tpu as pltpu

F32 = jnp.float32
MXU_DTYPE = jnp.bfloat16
WIRE_DTYPE = jnp.bfloat16
N_DEV = 8
EPS = 1e-6
LRU_C = 8.0
S5_GROUP = 16
S5_STATE = 64
S5_SUPER = 256
LRU_BLOCKS_PER_CHUNK = 4
ADAM_LR, ADAM_B1, ADAM_B2, ADAM_EPS, ADAM_WD, ADAM_STEP = 0.001, 0.9, 0.999, 1e-08, 0.01, 10
VMEM_LIMIT_BYTES = 56 * 1024 * 1024
LANE = 128
SUBLANE = 8

WEIGHTS = ['norm_g', 'w_ada', 'b_ada', 's5_w_in', 's5_lam_re', 's5_lam_im', 's5_log_dt', 's5_b_re', 's5_b_im',
           's5_c_re', 's5_c_im', 's5_d', 's5_w_glu', 'lru_w_in', 'lru_conv_w', 'lru_conv_b', 'lru_w_rg', 'lru_b_rg',
           'lru_w_ig', 'lru_b_ig', 'lru_lam', 'lru_w_out', 'ffn_w_gu', 'ffn_w_down', 'final_g']
BIG = ('w_ada', 's5_w_in', 's5_w_glu', 'lru_w_in', 'lru_w_out', 'ffn_w_gu', 'ffn_w_down')
SMALL = tuple(n for n in WEIGHTS if n not in BIG)
SMALL_SHARDED = ('norm_g', 'lru_conv_w', 'lru_conv_b', 'lru_b_rg', 'lru_b_ig', 'lru_lam')

NN = (((1,), (0,)), ((), ()))
NT = (((1,), (1,)), ((), ()))
TN = (((0,), (0,)), ((), ()))


def _params(n_grid):
    return pltpu.CompilerParams(dimension_semantics=("arbitrary",) * n_grid, vmem_limit_bytes=VMEM_LIMIT_BYTES)


def _tile(dim, pref, align=LANE):
    if dim <= pref:
        return dim
    t = (pref // align) * align
    while t >= align:
        if dim % t == 0:
            return t
        t -= align
    return dim


def _dot(a, b, dims):
    return lax.dot_general(a.astype(MXU_DTYPE), b.astype(MXU_DTYPE), dims, preferred_element_type=F32)


def _gelu(x):
    k = math.sqrt(2.0 / math.pi)
    return 0.5 * x * (1.0 + jnp.tanh(k * (x + 0.044715 * (x * x * x))))


def _gelu_and_grad(x):
    k = math.sqrt(2.0 / math.pi)
    th = jnp.tanh(k * (x + 0.044715 * (x * x * x)))
    g = 0.5 * x * (1.0 + th)
    dg = 0.5 * (1.0 + th) + 0.5 * x * (1.0 - th * th) * (k * (1.0 + 3.0 * 0.044715 * (x * x)))
    return g, dg


def _neg_expm1(x):
    series = -x * (1.0 + x * (0.5 + x * (1.0 / 6.0 + x * (1.0 / 24.0 + x * (1.0 / 120.0)))))
    return jnp.where(x > -0.01, series, 1.0 - jnp.exp(x))


MESH = pl.DeviceIdType.MESH
N_CHIP = N_DEV // 2
ALL, SAME_CORE = 7, 6


def _place():
    x, y, c = lax.axis_index("x"), lax.axis_index("y"), lax.axis_index("c")
    return x, y, c


def _flip(place, k):
    x, y, c = place
    return (1 - x if (k >> 2) & 1 else x, 1 - y if (k >> 1) & 1 else y, 1 - c if k & 1 else c)


def _chunk_exchange(name, xs, group):
    return _ride_alone(name, _chunk_rider(xs, group))


class _Rider:
    def __init__(self, arrays, out_shape, scratch, start, finish, post):
        self.arrays, self.out_shape, self.scratch = list(arrays), list(out_shape), list(scratch)
        self.start, self.finish, self.post = start, finish, post


def _chunk_rider(xs, group):
    n = len(xs)
    members, r, c_ = xs[0].shape
    assert members == {ALL: N_DEV, SAME_CORE: N_CHIP}[group]
    assert all(a.shape == xs[0].shape and a.dtype == xs[0].dtype for a in xs)
    ks = [k for k in range(1, N_DEV) if not k & ~group]
    member = (lambda p: 4 * p[0] + 2 * p[1] + p[2]) if group == ALL else (lambda p: 2 * p[0] + p[1])

    def copies(ins, outs, scratch):
        out = outs[0]
        send_sems, recv_sems, local_sems = scratch
        place = _place()
        me = member(place)
        local = [pltpu.make_async_copy(ins[l].at[me], out.at[me, l], local_sems.at[l]) for l in range(n)]
        remote = []
        for l in range(n):
            for k in ks:
                pid = _flip(place, k)
                peer = member(pid)

                def copy(land_at, l=l, k=k, peer=peer, pid=pid):
                    return pltpu.make_async_remote_copy(
                        src_ref=ins[l].at[peer], dst_ref=out.at[land_at, l], send_sem=send_sems.at[l * N_DEV + k],
                        recv_sem=recv_sems.at[l * N_DEV + k], device_id=pid, device_id_type=MESH)

                remote.append((copy, me, peer))
        return local, remote

    def start(ins, outs, scratch):
        local, remote = copies(ins, outs, scratch)
        for cp in local:
            cp.start()
        for copy, me, _ in remote:
            copy(me).start()

    def finish(ins, outs, scratch):
        local, remote = copies(ins, outs, scratch)
        for copy, me, peer in remote:
            copy(me).wait_send()
            copy(peer).wait_recv()
        for cp in local:
            cp.wait()

    return _Rider(
        xs, [jax.ShapeDtypeStruct((members, n, r, c_), xs[0].dtype)],
        [pltpu.SemaphoreType.DMA((n * N_DEV,)), pltpu.SemaphoreType.DMA((n * N_DEV,)), pltpu.SemaphoreType.DMA((n,))],
        start, finish, lambda outs: outs[0].reshape(members, n * r, c_))


def _gather_rider(xs):
    n = len(xs)
    chip_flips = (2, 4, 6)
    per = 1 + 2 * len(chip_flips)

    def plan(ins, outs, scratch):
        send_sems, recv_sems, local_sems = scratch
        place = _place()
        sibling = _flip(place, 1)
        jobs = []
        for l in range(n):
            slot = lambda p, l=l: outs[l].at[2 * p[0] + p[1], p[2]]

            def copy(k, block, to, src=None, l=l, slot=slot):
                return pltpu.make_async_remote_copy(
                    src_ref=slot(block) if src is None else src, dst_ref=slot(block), send_sem=send_sems.at[l * per + k],
                    recv_sem=recv_sems.at[l * per + k], device_id=to, device_id_type=MESH)

            mine = pltpu.make_async_copy(ins[l], slot(place), local_sems.at[l])
            first = [copy(0, place, sibling, src=ins[l])]
            first += [copy(1 + j, place, _flip(place, k), src=ins[l]) for j, k in enumerate(chip_flips)]
            jobs.append((copy, mine, first))
        return place, sibling, jobs

    def start(ins, outs, scratch):
        _, _, jobs = plan(ins, outs, scratch)
        for _, mine, first in jobs:
            mine.start()
            for cp in first:
                cp.start()

    def finish(ins, outs, scratch):
        place, sibling, jobs = plan(ins, outs, scratch)
        passed = []
        for copy, _, _ in jobs:
            for j, k in enumerate(chip_flips):
                copy(1 + j, _flip(place, k), place).wait_recv()
                fwd = copy(4 + j, _flip(place, k), sibling)
                fwd.start()
                passed.append(fwd)
        for copy, mine, first in jobs:
            copy(0, sibling, place).wait_recv()
            for j, k in enumerate(chip_flips):
                copy(4 + j, _flip(sibling, k), place).wait_recv()
            for cp in first:
                cp.wait_send()
            mine.wait()
        for cp in passed:
            cp.wait_send()

    return _Rider(
        xs, [jax.ShapeDtypeStruct((N_CHIP, 2) + x.shape, x.dtype) for x in xs],
        [pltpu.SemaphoreType.DMA((n * per,)), pltpu.SemaphoreType.DMA((n * per,)), pltpu.SemaphoreType.DMA((n,))],
        start, finish, lambda outs: [o.reshape((N_DEV,) + x.shape) for o, x in zip(outs, xs)])


HBM_SPEC = pl.BlockSpec(memory_space=pltpu.HBM)


def _ride_alone(name, rider):
    n_in, n_out = len(rider.arrays), len(rider.out_shape)

    def body(*refs):
        parts = refs[:n_in], refs[n_in:n_in + n_out], refs[n_in + n_out:]
        rider.start(*parts)
        rider.finish(*parts)

    outs = pl.pallas_call(body, name=name, out_shape=rider.out_shape, in_specs=[HBM_SPEC] * n_in,
                          out_specs=[HBM_SPEC] * n_out, scratch_shapes=rider.scratch)(*rider.arrays)
    return rider.post(list(outs))


def _call(body, *, name, grid, in_specs, out_specs, out_shape, scratch_shapes=(), args, rider=None):
    single = not isinstance(out_shape, (list, tuple))
    out_shape = [out_shape] if single else list(out_shape)
    out_specs = [out_specs] if single else list(out_specs)
    scratch_shapes = list(scratch_shapes)
    unwrap = lambda outs: outs[0] if single else list(outs)
    if rider is None:
        outs = pl.pallas_call(body, name=name, grid=grid, in_specs=list(in_specs), out_specs=out_specs, out_shape=out_shape,
                              scratch_shapes=scratch_shapes, compiler_params=_params(len(grid)))(*args)
        return unwrap(outs)
    n_in, n_out, n_scr = len(in_specs), len(out_shape), len(scratch_shapes)
    r_in, r_out = len(rider.arrays), len(rider.out_shape)

    def carried(*refs):
        ins, refs = refs[:n_in], refs[n_in:]
        r_ins, refs = refs[:r_in], refs[r_in:]
        outs, refs = refs[:n_out], refs[n_out:]
        r_outs, refs = refs[:r_out], refs[r_out:]
        scr, r_scr = refs[:n_scr], refs[n_scr:]
        steps = [pl.program_id(ax) for ax in range(len(grid))]

        @pl.when(functools.reduce(jnp.logical_and, [s == 0 for s in steps]))
        def _():
            rider.start(r_ins, r_outs, r_scr)

        body(*ins, *outs, *scr)

        @pl.when(functools.reduce(jnp.logical_and, [s == g - 1 for s, g in zip(steps, grid)]))
        def _():
            rider.finish(r_ins, r_outs, r_scr)

    outs = pl.pallas_call(
        carried, name=name, grid=grid, in_specs=list(in_specs) + [HBM_SPEC] * r_in, out_specs=out_specs + [HBM_SPEC] * r_out,
        out_shape=out_shape + rider.out_shape, scratch_shapes=scratch_shapes + rider.scratch,
        compiler_params=_params(len(grid)))(*args, *rider.arrays)
    return unwrap(outs[:n_out]), rider.post(list(outs[n_out:]))


def _all_gather(name, x):
    return _ride_alone(name, _gather_rider([x]))[0]


def _sibling_exchange(name, xs):
    n = len(xs)
    _, r, c_ = xs[0].shape
    assert all(a.shape == xs[0].shape and a.dtype == xs[0].dtype for a in xs)

    def body(*refs):
        ins, out = refs[:n], refs[n]
        send_sems, recv_sems = refs[n + 1:]
        place = _place()
        c = place[2]
        sibling = _flip(place, 1)
        copies = []
        for l in range(n):
            for chip in range(N_CHIP):
                cp = pltpu.make_async_remote_copy(
                    src_ref=ins[l].at[2 * chip + (1 - c)], dst_ref=out.at[chip, l], send_sem=send_sems.at[l * N_CHIP + chip],
                    recv_sem=recv_sems.at[l * N_CHIP + chip], device_id=sibling, device_id_type=MESH)
                cp.start()
                copies.append(cp)
        for cp in copies:
            cp.wait()

    hbm = pl.BlockSpec(memory_space=pltpu.HBM)
    return pl.pallas_call(
        body, name=name, out_shape=jax.ShapeDtypeStruct((N_CHIP, n, r, c_), xs[0].dtype), in_specs=[hbm] * n, out_specs=hbm,
        scratch_shapes=[pltpu.SemaphoreType.DMA((n * N_CHIP,)), pltpu.SemaphoreType.DMA((n * N_CHIP,))],
    )(*xs)


def _pair_sum(name, x, got, l, core):
    _, r, c_ = x.shape
    br = _tile(r, 256, 2 * SUBLANE)

    def body(core_ref, x_ref, g_ref, o_ref):
        o_ref[...] = (x_ref[...].astype(F32) + g_ref[...].astype(F32)).astype(o_ref.dtype)

    return pl.pallas_call(
        body, name=name, out_shape=jax.ShapeDtypeStruct((N_CHIP, r, c_), x.dtype),
        grid_spec=pltpu.PrefetchScalarGridSpec(
            num_scalar_prefetch=1, grid=(N_CHIP, r // br),
            in_specs=[pl.BlockSpec((None, br, c_), lambda ch, i, core_ref: (2 * ch + core_ref[0], i, 0)),
                      pl.BlockSpec((None, None, br, c_), lambda ch, i, core_ref: (ch, l, i, 0))],
            out_specs=pl.BlockSpec((None, br, c_), lambda ch, i, core_ref: (ch, i, 0))),
        compiler_params=_params(2))(core, x, got)


def _chip_sums(name, x):
    got = _sibling_exchange(name + "_d2d", [x])
    core = lax.axis_index("c").astype(jnp.int32).reshape(1)
    return _pair_sum(name + "_pair", x, got, 0, core)


def _mm(name, a, b, out_shape, out_dtype, grid, a_spec, b_spec, o_spec, dims, n_red, acc_shape, rider=None):
    red = tuple(range(len(grid) - n_red, len(grid)))

    def body(a_ref, b_ref, o_ref, acc_ref):
        first = functools.reduce(jnp.logical_and, [pl.program_id(ax) == 0 for ax in red])
        last = functools.reduce(jnp.logical_and, [pl.program_id(ax) == grid[ax] - 1 for ax in red])

        @pl.when(first)
        def _():
            acc_ref[...] = jnp.zeros_like(acc_ref)

        acc_ref[...] += _dot(a_ref[...], b_ref[...], dims)

        @pl.when(last)
        def _():
            o_ref[...] = acc_ref[...].astype(o_ref.dtype)

    return _call(body, name=name, out_shape=jax.ShapeDtypeStruct(out_shape, out_dtype), grid=grid,
                 in_specs=[a_spec, b_spec], out_specs=o_spec, scratch_shapes=[pltpu.VMEM(acc_shape, F32)],
                 args=(a, b), rider=rider)


def _mm_col(name, a, b, out_dtype=F32, rider=None):
    m, k = a.shape
    j, _, n = b.shape
    bm, bk = _tile(m, 1024), _tile(k, 512)
    return _mm(name, a, b, (j, m, n), out_dtype, (j, m // bm, k // bk),
               pl.BlockSpec((bm, bk), lambda jj, mm, kk: (mm, kk)),
               pl.BlockSpec((None, bk, n), lambda jj, mm, kk: (jj, kk, 0)),
               pl.BlockSpec((None, bm, n), lambda jj, mm, kk: (jj, mm, 0)), NN, 1, (bm, n), rider)


def _mm_col_da(name, do, b, rider=None):
    j, m, n = do.shape
    k = b.shape[1]
    bm, bk = _tile(m, 1024), _tile(k, 1024)
    return _mm(name, do, b, (m, k), F32, (m // bm, k // bk, j),
               pl.BlockSpec((None, bm, n), lambda mm, kk, jj: (jj, mm, 0)),
               pl.BlockSpec((None, bk, n), lambda mm, kk, jj: (jj, kk, 0)),
               pl.BlockSpec((bm, bk), lambda mm, kk, jj: (mm, kk)), NT, 1, (bm, bk), rider)


def _mm_col_db(name, a, do, out_dtype):
    m, k = a.shape
    j, _, n = do.shape
    bm, bk = _tile(m, 1024), _tile(k, 512)
    return _mm(name, a, do, (j, k, n), out_dtype, (j, k // bk, m // bm),
               pl.BlockSpec((bm, bk), lambda jj, kk, mm: (mm, kk)),
               pl.BlockSpec((None, bm, n), lambda jj, kk, mm: (jj, mm, 0)),
               pl.BlockSpec((None, bk, n), lambda jj, kk, mm: (jj, kk, 0)), TN, 1, (bk, n))


def _row_bk(kq):
    return kq if (kq % LANE or kq // LANE in (11,)) else _tile(kq, 512)


def _mm_row(name, a, b, out_dtype=F32, rider=None):
    q, m, kq = a.shape
    n = b.shape[1]
    bm, bn, bk = _tile(m, 1024), _tile(n, 1024), _row_bk(kq)
    nk = kq // bk
    return _mm(name, a, b, (m, n), out_dtype, (m // bm, n // bn, q, nk),
               pl.BlockSpec((None, bm, bk), lambda mm, nn, qq, kk: (qq, mm, kk)),
               pl.BlockSpec((bk, bn), lambda mm, nn, qq, kk: (qq * nk + kk, nn)),
               pl.BlockSpec((bm, bn), lambda mm, nn, qq, kk: (mm, nn)), NN, 2, (bm, bn), rider)


def _mm_row_da(name, do, b, q, rider=None):
    m, n = do.shape
    kq = b.shape[0] // q
    bm, bn = _tile(m, 1024), _tile(n, 1024)
    return _mm(name, do, b, (q, m, kq), F32, (q, m // bm, n // bn),
               pl.BlockSpec((bm, bn), lambda qq, mm, nn: (mm, nn)),
               pl.BlockSpec((kq, bn), lambda qq, mm, nn: (qq, nn)),
               pl.BlockSpec((None, bm, kq), lambda qq, mm, nn: (qq, mm, 0)), NT, 1, (bm, kq), rider)


def _mm_row_db(name, a, do, out_dtype):
    q, m, kq = a.shape
    n = do.shape[1]
    bm, bn = _tile(m, 1024), _tile(n, 512)
    return _mm(name, a, do, (q * kq, n), out_dtype, (q, n // bn, m // bm),
               pl.BlockSpec((None, bm, kq), lambda qq, nn, mm: (qq, mm, 0)),
               pl.BlockSpec((bm, bn), lambda qq, nn, mm: (mm, nn)),
               pl.BlockSpec((kq, bn), lambda qq, nn, mm: (qq, nn)), TN, 1, (kq, bn))


def _row_spec(bm, d):
    return pl.BlockSpec((bm, d), lambda i: (i, 0))


def _vec_spec(d):
    return pl.BlockSpec((1, d), lambda i: (0, 0))


def _norm_mod_fwd(name, x, gain, sc, sh):
    t, d = x.shape
    bm = _tile(t, 256, SUBLANE)

    def body(x_ref, g_ref, sc_ref, sh_ref, h_ref):
        xv = x_ref[...]
        rstd = lax.rsqrt(jnp.mean(xv * xv, axis=-1, keepdims=True) + EPS)
        h_ref[...] = ((xv * rstd) * g_ref[...] * (1.0 + sc_ref[...]) + sh_ref[...]).astype(h_ref.dtype)

    return pl.pallas_call(
        body, name=name, out_shape=jax.ShapeDtypeStruct((t, d), MXU_DTYPE), grid=(t // bm,),
        in_specs=[_row_spec(bm, d), _vec_spec(d), _vec_spec(d), _vec_spec(d)], out_specs=_row_spec(bm, d),
        compiler_params=_params(1))(x, gain, sc, sh)


def _norm_mod_bwd(name, x, dh, dres, gain, sc):
    t, d = x.shape
    bm = _tile(t, 256, SUBLANE)

    def body(x_ref, dh_ref, dres_ref, g_ref, sc_ref, dx_ref, dg_ref, dsc_ref, dsh_ref):
        @pl.when(pl.program_id(0) == 0)
        def _():
            dg_ref[...] = jnp.zeros_like(dg_ref)
            dsc_ref[...] = jnp.zeros_like(dsc_ref)
            dsh_ref[...] = jnp.zeros_like(dsh_ref)

        xv, dh_ = x_ref[...], dh_ref[...]
        rstd = lax.rsqrt(jnp.mean(xv * xv, axis=-1, keepdims=True) + EPS)
        nrm = xv * rstd
        gain_ = g_ref[...]
        dsh_ref[...] += jnp.sum(dh_, axis=0, keepdims=True)
        dsc_ref[...] += jnp.sum(dh_ * (nrm * gain_), axis=0, keepdims=True)
        dhn = dh_ * (1.0 + sc_ref[...])
        dg_ref[...] += jnp.sum(dhn * nrm, axis=0, keepdims=True)
        dn = dhn * gain_
        dx_ref[...] = dres_ref[...] + rstd * (dn - nrm * jnp.mean(dn * nrm, axis=-1, keepdims=True))

    vec = jax.ShapeDtypeStruct((1, d), F32)
    return pl.pallas_call(
        body, name=name, out_shape=[jax.ShapeDtypeStruct((t, d), F32), vec, vec, vec], grid=(t // bm,),
        in_specs=[_row_spec(bm, d), _row_spec(bm, d), _row_spec(bm, d), _vec_spec(d), _vec_spec(d)],
        out_specs=[_row_spec(bm, d), _vec_spec(d), _vec_spec(d), _vec_spec(d)],
        compiler_params=_params(1))(x, dh, dres, gain, sc)


def _loss_bwd(name, x, target, gain):
    t, d = x.shape
    bm = _tile(t, 256, SUBLANE)

    def body(x_ref, t_ref, g_ref, dx_ref, loss_ref, dg_ref):
        @pl.when(pl.program_id(0) == 0)
        def _():
            loss_ref[...] = jnp.zeros_like(loss_ref)
            dg_ref[...] = jnp.zeros_like(dg_ref)

        xv = x_ref[...]
        rstd = lax.rsqrt(jnp.mean(xv * xv, axis=-1, keepdims=True) + EPS)
        nrm = xv * rstd
        gain_ = g_ref[...]
        err = nrm * gain_ - t_ref[...]
        per_tok = jnp.mean(err * err, axis=-1, keepdims=True)
        loss_ref[...] += 0.5 * jnp.sum(per_tok, axis=0, keepdims=True)
        dout = err * (1.0 / d)
        dg_ref[...] += jnp.sum(dout * nrm, axis=0, keepdims=True)
        dn = dout * gain_
        dx_ref[...] = rstd * (dn - nrm * jnp.mean(dn * nrm, axis=-1, keepdims=True))

    return pl.pallas_call(
        body, name=name,
        out_shape=[jax.ShapeDtypeStruct((t, d), F32), jax.ShapeDtypeStruct((1, 1), F32),
                   jax.ShapeDtypeStruct((1, d), F32)],
        grid=(t // bm,), in_specs=[_row_spec(bm, d), _row_spec(bm, d), _vec_spec(d)],
        out_specs=[_row_spec(bm, d), pl.BlockSpec((1, 1), lambda i: (0, 0)), _vec_spec(d)],
        compiler_params=_params(1))(x, target, gain)


def _resid(name, x, y, g):
    t, d = x.shape
    bm = _tile(t, 256, SUBLANE)

    def body(x_ref, y_ref, g_ref, o_ref):
        o_ref[...] = x_ref[...] + g_ref[...] * y_ref[...]

    return pl.pallas_call(
        body, name=name, out_shape=jax.ShapeDtypeStruct((t, d), F32), grid=(t // bm,),
        in_specs=[_row_spec(bm, d), _row_spec(bm, d), _vec_spec(d)], out_specs=_row_spec(bm, d),
        compiler_params=_params(1))(x, y, g)


def _gate_bwd(name, dx, y, g):
    t, d = dx.shape
    bm = _tile(t, 256, SUBLANE)

    def body(dx_ref, y_ref, g_ref, dy_ref, dg_ref):
        @pl.when(pl.program_id(0) == 0)
        def _():
            dg_ref[...] = jnp.zeros_like(dg_ref)

        dxv = dx_ref[...]
        dy_ref[...] = (g_ref[...] * dxv).astype(dy_ref.dtype)
        dg_ref[...] += jnp.sum(dxv * y_ref[...], axis=0, keepdims=True)

    return pl.pallas_call(
        body, name=name, out_shape=[jax.ShapeDtypeStruct((t, d), MXU_DTYPE), jax.ShapeDtypeStruct((1, d), F32)],
        grid=(t // bm,), in_specs=[_row_spec(bm, d), _row_spec(bm, d), _vec_spec(d)],
        out_specs=[_row_spec(bm, d), _vec_spec(d)], compiler_params=_params(1))(dx, y, g)


def _glu_resid_fwd(name, z, x, g):
    _, t, n = z.shape
    d = x.shape[1]
    half = N_DEV // 2
    bm = _tile(t, 256, SUBLANE)

    def body(v_ref, gt_ref, x_ref, g_ref, o_ref):
        o_ref[...] = x_ref[...] + g_ref[...] * (v_ref[...] * jax.nn.sigmoid(gt_ref[...]))

    return pl.pallas_call(
        body, name=name, out_shape=jax.ShapeDtypeStruct((t, d), F32), grid=(half, t // bm),
        in_specs=[pl.BlockSpec((None, bm, n), lambda q, i: (q, i, 0)),
                  pl.BlockSpec((None, bm, n), lambda q, i: (q + half, i, 0)),
                  pl.BlockSpec((bm, n), lambda q, i: (i, q)), pl.BlockSpec((1, n), lambda q, i: (0, q))],
        out_specs=pl.BlockSpec((bm, n), lambda q, i: (i, q)), compiler_params=_params(2))(z, z, x, g)


def _glu_resid_bwd(name, z, dx, g):
    _, t, n = z.shape
    d = dx.shape[1]
    half = N_DEV // 2
    bm = _tile(t, 256, SUBLANE)

    def body(z_ref, dx_ref, g_ref, dz_ref, dg_ref):
        @pl.when(pl.program_id(1) == 0)
        def _():
            dg_ref[...] = jnp.zeros_like(dg_ref)

        v, dxv = z_ref[0], dx_ref[...]
        sig = jax.nn.sigmoid(z_ref[1])
        dout = g_ref[...] * dxv
        dg_ref[...] += jnp.sum(dxv * (v * sig), axis=0, keepdims=True)
        dz_ref[0] = (dout * sig).astype(dz_ref.dtype)
        dz_ref[1] = (dout * v * (sig * (1.0 - sig))).astype(dz_ref.dtype)

    pair = pl.BlockSpec((2, None, bm, n), lambda q, i: (0, q, i, 0))
    dz, dg = pl.pallas_call(
        body, name=name,
        out_shape=[jax.ShapeDtypeStruct((2, half, t, n), MXU_DTYPE), jax.ShapeDtypeStruct((1, d), F32)],
        grid=(half, t // bm),
        in_specs=[pair, pl.BlockSpec((bm, n), lambda q, i: (i, q)), pl.BlockSpec((1, n), lambda q, i: (0, q))],
        out_specs=[pair, pl.BlockSpec((1, n), lambda q, i: (0, q))],
        compiler_params=_params(2))(z.reshape(2, half, t, n), dx, g)
    return dz.reshape(N_DEV, t, n), dg


def _swiglu_act_fwd(name, gu):
    _, t, n = gu.shape
    half = N_DEV // 2
    bm = _tile(t, 256, SUBLANE)

    def body(g_ref, u_ref, o_ref):
        gv = g_ref[...]
        o_ref[...] = (gv * jax.nn.sigmoid(gv) * u_ref[...]).astype(o_ref.dtype)

    return pl.pallas_call(
        body, name=name, out_shape=jax.ShapeDtypeStruct((half, t, n), MXU_DTYPE), grid=(half, t // bm),
        in_specs=[pl.BlockSpec((None, bm, n), lambda q, i: (q, i, 0)),
                  pl.BlockSpec((None, bm, n), lambda q, i: (q + half, i, 0))],
        out_specs=pl.BlockSpec((None, bm, n), lambda q, i: (q, i, 0)), compiler_params=_params(2))(gu, gu)


def _swiglu_act_bwd(name, gu, dact, rider=None):
    _, t, n = gu.shape
    half = N_DEV // 2
    bm = _tile(t, 256, SUBLANE)

    def body(gu_ref, da_ref, o_ref):
        gv, da = gu_ref[0], da_ref[...]
        sig = jax.nn.sigmoid(gv)
        o_ref[0] = (da * gu_ref[1] * (sig * (1.0 + gv * (1.0 - sig)))).astype(o_ref.dtype)
        o_ref[1] = (da * (gv * sig)).astype(o_ref.dtype)

    pair = pl.BlockSpec((2, None, bm, n), lambda q, i: (0, q, i, 0))
    res = _call(
        body, name=name, out_shape=jax.ShapeDtypeStruct((2, half, t, n), MXU_DTYPE), grid=(half, t // bm),
        in_specs=[pair, pl.BlockSpec((None, bm, n), lambda q, i: (q, i, 0))], out_specs=pair,
        args=(gu.reshape(2, half, t, n), dact), rider=rider)
    if rider is None:
        return res.reshape(N_DEV, t, n)
    return res[0].reshape(N_DEV, t, n), res[1]


def _ada_fwd(name, c16, w_ada, b_loc):
    nl, d, n = w_ada.shape
    bn = _tile(n, 512)

    def body(c_ref, w_ref, b_ref, o_ref):
        cv = c_ref[...]
        o_ref[...] = _dot(cv * jax.nn.sigmoid(cv), w_ref[...], NN) + b_ref[...]

    return pl.pallas_call(
        body, name=name, out_shape=jax.ShapeDtypeStruct((nl, c16.shape[0], n), F32), grid=(nl, n // bn),
        in_specs=[pl.BlockSpec(c16.shape, lambda i, j: (0, 0)), pl.BlockSpec((None, d, bn), lambda i, j: (i, 0, j)),
                  pl.BlockSpec((None, 1, bn), lambda i, j: (i, 0, j))],
        out_specs=pl.BlockSpec((None, c16.shape[0], bn), lambda i, j: (i, 0, j)),
        compiler_params=_params(2))(c16, w_ada, b_loc)


def _adam_update(g, w, m, v):
    m = ADAM_B1 * m + (1.0 - ADAM_B1) * g
    v = ADAM_B2 * v + (1.0 - ADAM_B2) * (g * g)
    m_hat = m / (1.0 - ADAM_B1 ** ADAM_STEP)
    v_hat = v / (1.0 - ADAM_B2 ** ADAM_STEP)
    delta = -ADAM_LR * (m_hat / (jnp.sqrt(v_hat) + ADAM_EPS) + ADAM_WD * w)
    return delta, m, v


def _adamw_w_ada(name, c16, dmod16, w, m, v, rider=None):
    nl, d, n = w.shape
    br = _tile(d, 256)

    def body(c_ref, dm_ref, w_ref, m_ref, v_ref, g_ref, dl_ref, mo_ref, vo_ref):
        cv = c_ref[...]
        g = _dot(cv * jax.nn.sigmoid(cv), dm_ref[...], TN)
        g_ref[...] = g
        dl_ref[...], mo_ref[...], vo_ref[...] = _adam_update(g, w_ref[...], m_ref[...], v_ref[...])

    blk = pl.BlockSpec((None, br, n), lambda i, r: (i, r, 0))
    shp = jax.ShapeDtypeStruct(w.shape, F32)
    return _call(
        body, name=name, out_shape=[shp] * 4, grid=(nl, d // br),
        in_specs=[pl.BlockSpec((c16.shape[0], br), lambda i, r: (0, r)),
                  pl.BlockSpec((None, dmod16.shape[1], n), lambda i, r: (i, 0, 0)), blk, blk, blk],
        out_specs=[blk] * 4, args=(c16, dmod16, w, m, v), rider=rider)


def _adamw_sum(name, parts, w, m, v, rider=None):
    nl = len(parts)
    p, r, c = parts[0].shape
    br = _tile(r, 128, 2 * SUBLANE)
    nb = r // br

    def body(*refs):
        p_refs, (w_ref, m_ref, v_ref, g_ref, dl_ref, mo_ref, vo_ref) = refs[:nl], refs[nl:]
        layer = pl.program_id(0)
        g = None
        for l, p_ref in enumerate(p_refs):
            gl = p_ref[0].astype(F32)
            for s in range(1, p):
                gl = gl + p_ref[s].astype(F32)
            g = gl if g is None else jnp.where(layer == l, gl, g)
        g_ref[...] = g
        dl_ref[...], mo_ref[...], vo_ref[...] = _adam_update(g, w_ref[...], m_ref[...], v_ref[...])

    blk = pl.BlockSpec((br, c), lambda l, i: (l * nb + i, 0))
    shp = jax.ShapeDtypeStruct((nl * r, c), F32)
    return _call(
        body, name=name, out_shape=[shp] * 4, grid=(nl, nb),
        in_specs=[pl.BlockSpec((p, br, c), lambda l, i: (0, i, 0))] * nl + [blk, blk, blk], out_specs=[blk] * 4,
        args=(*parts, w, m, v), rider=rider)


def _sum_parts(name, parts):
    p, r, c = parts.shape

    def body(p_ref, o_ref):
        g = p_ref[0]
        for s in range(1, p):
            g = g + p_ref[s]
        o_ref[...] = g

    return pl.pallas_call(body, name=name, out_shape=jax.ShapeDtypeStruct((r, c), F32))(parts)


def _s5_disc(name, lam_re, lam_im, log_dt, b_re, b_im):
    def body(lr_ref, li_ref, ld_ref, br_ref, bi_ref, ar_ref, ai_ref, bbr_ref, bbi_ref):
        lr, li = lr_ref[...], li_ref[...]
        dt = jnp.exp(ld_ref[...])
        mag = jnp.exp(lr * dt)
        a_re, a_im = mag * jnp.cos(li * dt), mag * jnp.sin(li * dt)
        nr, ni = a_re - 1.0, a_im
        den = lr * lr + li * li
        f_re, f_im = (nr * lr + ni * li) / den, (ni * lr - nr * li) / den
        br, bi = br_ref[...], bi_ref[...]
        ar_ref[...], ai_ref[...] = a_re, a_im
        bbr_ref[...] = f_re * br - f_im * bi
        bbi_ref[...] = f_re * bi + f_im * br

    s_a, s_b = jax.ShapeDtypeStruct(lam_re.shape, F32), jax.ShapeDtypeStruct(b_re.shape, F32)
    return pl.pallas_call(body, name=name, out_shape=[s_a, s_a, s_b, s_b])(lam_re, lam_im, log_dt, b_re, b_im)


def _s5_disc_bwd(name, lam_re, lam_im, log_dt, b_re, b_im, dab_re, dab_im, dbb_re, dbb_im):
    def body(lr_ref, li_ref, ld_ref, br_ref, bi_ref, dar_ref, dai_ref, dbbr_ref, dbbi_ref,
             dlr_ref, dli_ref, dld_ref, dbr_ref, dbi_ref):
        lr, li = lr_ref[...], li_ref[...]
        dt = jnp.exp(ld_ref[...])
        mag = jnp.exp(lr * dt)
        a_re, a_im = mag * jnp.cos(li * dt), mag * jnp.sin(li * dt)
        nr, ni = a_re - 1.0, a_im
        den = lr * lr + li * li
        f_re, f_im = (nr * lr + ni * li) / den, (ni * lr - nr * li) / den
        br, bi = br_ref[...], bi_ref[...]
        dbbr, dbbi = dbbr_ref[...], dbbi_ref[...]
        dbr_ref[...] = f_re * dbbr + f_im * dbbi
        dbi_ref[...] = f_re * dbbi - f_im * dbbr
        df_re = jnp.sum(dbbr * br + dbbi * bi, axis=1, keepdims=True)
        df_im = jnp.sum(dbbi * br - dbbr * bi, axis=1, keepdims=True)
        dnr = (df_re * lr - df_im * li) / den
        dni = (df_re * li + df_im * lr) / den
        dden = -(df_re * f_re + df_im * f_im) / den
        dlr = (df_re * nr + df_im * ni) / den + 2.0 * lr * dden
        dli = (df_re * ni - df_im * nr) / den + 2.0 * li * dden
        da_re, da_im = dar_ref[...] + dnr, dai_ref[...] + dni
        dmag_mag = da_re * a_re + da_im * a_im
        dth = da_im * a_re - da_re * a_im
        dlr_ref[...] = dlr + dmag_mag * dt
        dli_ref[...] = dli + dth * dt
        ddt = jnp.sum(dmag_mag * lr + dth * li, axis=2, keepdims=True)
        dld_ref[...] = ddt * dt

    s_a, s_b = jax.ShapeDtypeStruct(lam_re.shape, F32), jax.ShapeDtypeStruct(b_re.shape, F32)
    return pl.pallas_call(
        body, name=name, out_shape=[s_a, s_a, jax.ShapeDtypeStruct(log_dt.shape, F32), s_b, s_b],
    )(lam_re, lam_im, log_dt, b_re, b_im, dab_re, dab_im, dbb_re, dbb_im)


def _s5_time_block(t):
    return _tile(t, 128, SUBLANE)


def _s5_scan_fwd(name, u, bb_re, bb_im, ab_re, ab_im, rider=None):
    t, d = u.shape
    nsg, cs, ns = bb_re.shape
    tb = _s5_time_block(t)

    def body(u_ref, bbr_hbm, bbi_hbm, ar_ref, ai_ref, sr_ref, si_ref, srm_ref, sim_ref, bbr, bbi, cr_ref, ci_ref):
        @pl.when(pl.program_id(0) == 0)
        def _():
            pltpu.sync_copy(bbr_hbm, bbr)
            pltpu.sync_copy(bbi_hbm, bbi)
            cr_ref[...] = jnp.zeros_like(cr_ref)
            ci_ref[...] = jnp.zeros_like(ci_ref)

        for sg in range(nsg):
            us = u_ref[:, sg * cs:(sg + 1) * cs]
            sr_ref[:, sg, :] = _dot(us, bbr[sg], NN)
            si_ref[:, sg, :] = _dot(us, bbi[sg], NN)
        ar, ai = ar_ref[...], ai_ref[...]

        def step(i, carry):
            cr, ci = carry
            nr = ar * cr - ai * ci + sr_ref[i]
            ni = ar * ci + ai * cr + si_ref[i]
            sr_ref[i] = nr
            si_ref[i] = ni
            return nr, ni

        cr, ci = lax.fori_loop(0, tb, step, (cr_ref[...], ci_ref[...]), unroll=2)
        cr_ref[...], ci_ref[...] = cr, ci
        srm_ref[...] = jnp.swapaxes(sr_ref[...], 0, 1).astype(MXU_DTYPE)
        sim_ref[...] = jnp.swapaxes(si_ref[...], 0, 1).astype(MXU_DTYPE)

    scan = jax.ShapeDtypeStruct((t, nsg, ns), F32)
    mxu = jax.ShapeDtypeStruct((nsg, t, ns), MXU_DTYPE)
    hbm = pl.BlockSpec(memory_space=pltpu.HBM)
    full = pl.BlockSpec((nsg, ns), lambda i: (0, 0))
    return _call(
        body, name=name, out_shape=[scan, scan, mxu, mxu], grid=(t // tb,),
        in_specs=[_row_spec(tb, d), hbm, hbm, full, full],
        out_specs=[pl.BlockSpec((tb, nsg, ns), lambda i: (i, 0, 0))] * 2 + [pl.BlockSpec((nsg, tb, ns), lambda i: (0, i, 0))] * 2,
        scratch_shapes=[pltpu.VMEM(bb_re.shape, bb_re.dtype), pltpu.VMEM(bb_im.shape, bb_im.dtype),
                        pltpu.VMEM((nsg, ns), F32), pltpu.VMEM((nsg, ns), F32)],
        args=(u, bb_re, bb_im, ab_re, ab_im), rider=rider)


def _s5_out_fwd(name, s_re, s_im, cc_re, cc_im, u, dskip):
    nsg, t, ns = s_re.shape
    d = u.shape[1]
    cs = cc_re.shape[2]
    tb = _tile(t, 512, SUBLANE)

    def body(sr_ref, si_ref, cr_ref, ci_ref, u_ref, d_ref, yp_ref, ya_ref):
        y = _dot(sr_ref[...], cr_ref[...], NN) - _dot(si_ref[...], ci_ref[...], NN) + d_ref[...] * u_ref[...]
        yp_ref[...] = y
        ya_ref[...] = _gelu(y).astype(ya_ref.dtype)

    s_spec = pl.BlockSpec((None, tb, ns), lambda sg, i: (sg, i, 0))
    c_spec = pl.BlockSpec((None, ns, cs), lambda sg, i: (sg, 0, 0))
    col = pl.BlockSpec((tb, cs), lambda sg, i: (i, sg))
    return pl.pallas_call(
        body, name=name, out_shape=[jax.ShapeDtypeStruct((t, d), F32), jax.ShapeDtypeStruct((t, d), MXU_DTYPE)],
        grid=(nsg, t // tb), in_specs=[s_spec, s_spec, c_spec, c_spec, col, pl.BlockSpec((1, cs), lambda sg, i: (0, sg))],
        out_specs=[col, col], compiler_params=_params(2))(s_re, s_im, cc_re, cc_im, u, dskip)


def _gelu_bwd(name, dy, ypre):
    t, d = dy.shape
    bm = _tile(t, 256, SUBLANE)

    def body(dy_ref, yp_ref, o_ref):
        o_ref[...] = (dy_ref[...] * _gelu_and_grad(yp_ref[...])[1]).astype(o_ref.dtype)

    return pl.pallas_call(
        body, name=name, out_shape=jax.ShapeDtypeStruct((t, d), MXU_DTYPE), grid=(t // bm,),
        in_specs=[_row_spec(bm, d), _row_spec(bm, d)], out_specs=_row_spec(bm, d), compiler_params=_params(1))(dy, ypre)


def _s5_scan_bwd(name, dyp, cc_re, cc_im, ab_re, ab_im, s_re, s_im, rider=None):
    t, d = dyp.shape
    nsg, ns, cs = cc_re.shape
    tb = _s5_time_block(t)
    nb = t // tb

    def body(dy_ref, ccr_hbm, cci_hbm, ar_ref, ai_ref, sr_ref, si_ref, lrm_ref, lim_ref, dar_ref, dai_ref,
             ccr, cci, lr_ref, li_ref, cr_ref, ci_ref):
        @pl.when(pl.program_id(0) == 0)
        def _():
            pltpu.sync_copy(ccr_hbm, ccr)
            pltpu.sync_copy(cci_hbm, cci)
            cr_ref[...] = jnp.zeros_like(cr_ref)
            ci_ref[...] = jnp.zeros_like(ci_ref)
            dar_ref[...] = jnp.zeros_like(dar_ref)
            dai_ref[...] = jnp.zeros_like(dai_ref)

        for sg in range(nsg):
            dys = dy_ref[:, sg * cs:(sg + 1) * cs]
            lr_ref[:, sg, :] = _dot(dys, ccr[sg], NT)
            li_ref[:, sg, :] = -_dot(dys, cci[sg], NT)
        ar, ai = ar_ref[...], ai_ref[...]

        def step(i, carry):
            cr, ci, dar, dai = carry
            j = tb - 1 - i
            sr, si = sr_ref[j], si_ref[j]
            dar = dar + (cr * sr + ci * si)
            dai = dai + (ci * sr - cr * si)
            nr = lr_ref[j] + (ar * cr + ai * ci)
            ni = li_ref[j] + (ar * ci - ai * cr)
            lr_ref[j] = nr
            li_ref[j] = ni
            return nr, ni, dar, dai

        cr, ci, dar, dai = lax.fori_loop(0, tb, step, (cr_ref[...], ci_ref[...], dar_ref[...], dai_ref[...]))
        cr_ref[...], ci_ref[...] = cr, ci
        dar_ref[...], dai_ref[...] = dar, dai
        lrm_ref[...] = jnp.swapaxes(lr_ref[...], 0, 1).astype(MXU_DTYPE)
        lim_ref[...] = jnp.swapaxes(li_ref[...], 0, 1).astype(MXU_DTYPE)

    hbm = pl.BlockSpec(memory_space=pltpu.HBM)
    full = pl.BlockSpec((nsg, ns), lambda i: (0, 0))
    mxu = jax.ShapeDtypeStruct((nsg, t, ns), MXU_DTYPE)
    acc = jax.ShapeDtypeStruct((nsg, ns), F32)
    scan_spec = pl.BlockSpec((tb, nsg, ns), lambda i: (nb - 1 - i, 0, 0))
    return _call(
        body, name=name, out_shape=[mxu, mxu, acc, acc], grid=(nb,),
        in_specs=[pl.BlockSpec((tb, d), lambda i: (nb - 1 - i, 0)), hbm, hbm, full, full, scan_spec, scan_spec],
        out_specs=[pl.BlockSpec((nsg, tb, ns), lambda i: (0, nb - 1 - i, 0))] * 2 + [full, full],
        scratch_shapes=[pltpu.VMEM(cc_re.shape, cc_re.dtype), pltpu.VMEM(cc_im.shape, cc_im.dtype),
                        pltpu.VMEM((tb, nsg, ns), F32), pltpu.VMEM((tb, nsg, ns), F32),
                        pltpu.VMEM((nsg, ns), F32), pltpu.VMEM((nsg, ns), F32)],
        args=(dyp, cc_re, cc_im, ab_re, ab_im, s_re, s_im), rider=rider)


def _s5_grads(name, lam_re, lam_im, s_re, s_im, u, dyp, bb_re, bb_im, dskip, rider=None):
    nsg, t, ns = lam_re.shape
    d = u.shape[1]
    cs = bb_re.shape[1]
    tb = _tile(t, 512, SUBLANE)

    def body(lr_ref, li_ref, sr_ref, si_ref, u_ref, dy_ref, bbr_ref, bbi_ref, d_ref,
             du_ref, dbbr_ref, dbbi_ref, dccr_ref, dcci_ref, dd_ref):
        @pl.when(pl.program_id(1) == 0)
        def _():
            for r in (dbbr_ref, dbbi_ref, dccr_ref, dcci_ref, dd_ref):
                r[...] = jnp.zeros_like(r)

        lr, li, uv, dy = lr_ref[...], li_ref[...], u_ref[...], dy_ref[...]
        dyf = dy.astype(F32)
        du_ref[...] = _dot(lr, bbr_ref[...], NT) + _dot(li, bbi_ref[...], NT) + d_ref[...] * dyf
        dbbr_ref[...] += _dot(uv, lr, TN)
        dbbi_ref[...] += _dot(uv, li, TN)
        dccr_ref[...] += _dot(sr_ref[...], dy, TN)
        dcci_ref[...] -= _dot(si_ref[...], dy, TN)
        dd_ref[...] += jnp.sum(dyf * uv, axis=0, keepdims=True)

    s_spec = pl.BlockSpec((None, tb, ns), lambda sg, i: (sg, i, 0))
    col = pl.BlockSpec((tb, cs), lambda sg, i: (i, sg))
    b_spec = pl.BlockSpec((None, cs, ns), lambda sg, i: (sg, 0, 0))
    c_spec = pl.BlockSpec((None, ns, cs), lambda sg, i: (sg, 0, 0))
    vec = pl.BlockSpec((1, cs), lambda sg, i: (0, sg))
    return _call(
        body, name=name,
        out_shape=[jax.ShapeDtypeStruct((t, d), F32), jax.ShapeDtypeStruct(bb_re.shape, F32),
                   jax.ShapeDtypeStruct(bb_re.shape, F32), jax.ShapeDtypeStruct((nsg, ns, cs), F32),
                   jax.ShapeDtypeStruct((nsg, ns, cs), F32), jax.ShapeDtypeStruct((1, d), F32)],
        grid=(nsg, t // tb), in_specs=[s_spec, s_spec, s_spec, s_spec, col, col, b_spec, b_spec, vec],
        out_specs=[col, b_spec, b_spec, c_spec, c_spec, vec],
        args=(lam_re, lam_im, s_re, s_im, u, dyp, bb_re, bb_im, dskip), rider=rider)


def _shift_down(x, k, prev8):
    if k == 0:
        return x
    ext = jnp.concatenate([prev8, x], axis=0)
    return ext[SUBLANE - k:SUBLANE - k + x.shape[0]]


def _shift_up(x, k, next8):
    if k == 0:
        return x
    ext = jnp.concatenate([x, next8], axis=0)
    return ext[k:k + x.shape[0]]


def _lru_time_block(t):
    return _tile(t, 256, SUBLANE)


def _lru_gates(xp, prev8, cv_ref, wrg, wig):
    taps = cv_ref.shape[0] - 4
    row = lambda k: cv_ref[k:k + 1, :]
    xs = [_shift_down(xp, taps - 1 - k, prev8) for k in range(taps)]
    xb = row(taps)
    for k in range(taps):
        xb = xb + row(k) * xs[k]
    r = jax.nn.sigmoid(_dot(xb, wrg, NN) + row(taps + 1))
    ig = jax.nn.sigmoid(_dot(xb, wig, NN) + row(taps + 2))
    sp = jax.nn.softplus(-row(taps + 3))
    log_a = -LRU_C * r * sp
    a = jnp.exp(log_a)
    mult = jnp.sqrt(_neg_expm1(2.0 * log_a))
    return xs, xb, r, ig, sp, a, mult


def _lru_fwd(name, zz, cvec, wrg, wig, rider=None):
    _, t, w = zz.shape
    half = N_DEV // 2
    tb = _lru_time_block(t)

    def body(gb_ref, xp_ref, xprev_ref, cv_ref, wrg_ref, wig_ref, hs_ref, y_ref, a_scr, b_scr, carry):
        i = pl.program_id(1)

        @pl.when(i == 0)
        def _():
            carry[...] = jnp.zeros_like(carry)

        prev8 = jnp.where(i > 0, xprev_ref[...], 0.0)
        _, xb, _, ig, _, a, mult = _lru_gates(xp_ref[...], prev8, cv_ref, wrg_ref[...], wig_ref[...])
        a_scr[...] = a
        b_scr[...] = mult * (ig * xb)

        def step(j, h):
            h = a_scr[pl.ds(j, 1), :] * h + b_scr[pl.ds(j, 1), :]
            hs_ref[pl.ds(j, 1), :] = h
            return h

        carry[0:1, :] = lax.fori_loop(0, tb, step, carry[0:1, :], unroll=8)
        y_ref[...] = (hs_ref[...] * _gelu(gb_ref[...])).astype(y_ref.dtype)

    nrow = tb // SUBLANE
    blk = lambda off: pl.BlockSpec((None, tb, w), lambda q, i: (q + off, i, 0))
    return _call(
        body, name=name,
        out_shape=[jax.ShapeDtypeStruct((half, t, w), F32), jax.ShapeDtypeStruct((half, t, w), MXU_DTYPE)],
        grid=(half, t // tb),
        in_specs=[blk(0), blk(half),
                  pl.BlockSpec((None, SUBLANE, w), lambda q, i: (q + half, jnp.maximum(i * nrow - 1, 0), 0)),
                  pl.BlockSpec((None,) + cvec.shape[1:], lambda q, i: (q, 0, 0)),
                  pl.BlockSpec((None, w, w), lambda q, i: (q, 0, 0)), pl.BlockSpec((None, w, w), lambda q, i: (q, 0, 0))],
        out_specs=[blk(0), blk(0)],
        scratch_shapes=[pltpu.VMEM((tb, w), F32), pltpu.VMEM((tb, w), F32), pltpu.VMEM((SUBLANE, w), F32)],
        args=(zz, zz, zz, cvec, wrg, wig), rider=rider)


def _lru_bwd(name, zz, hs, dy, cvec, wrg, wig, rider=None):
    _, t, w = zz.shape
    half = N_DEV // 2
    tb = _lru_time_block(t)
    nb = t // tb
    taps = cvec.shape[1] - 4

    def body(gb_ref, xp_ref, xprev_ref, hs_ref, hprev_ref, dy_ref, cv_ref, wrg_ref, wig_ref,
             dgb_ref, dxp_ref, dcv_ref, dwrg_ref, dwig_ref, a_scr, l_scr, carry, dxb_next):
        i = pl.program_id(1)

        @pl.when(i == 0)
        def _():
            for r_ in (carry, dxb_next, dcv_ref, dwrg_ref, dwig_ref):
                r_[...] = jnp.zeros_like(r_)

        has_prev = i < nb - 1
        row = lambda k: cv_ref[k:k + 1, :]
        prev8 = jnp.where(has_prev, xprev_ref[...], 0.0)
        xs, xb, r, ig, sp, a, mult = _lru_gates(xp_ref[...], prev8, cv_ref, wrg_ref[...], wig_ref[...])
        hs_ = hs_ref[...]
        hs_m1 = _shift_down(hs_, 1, jnp.where(has_prev, hprev_ref[...], 0.0))
        gel, dgel = _gelu_and_grad(gb_ref[...])
        dy_ = dy_ref[...]
        dgb_ref[...] = (dy_ * hs_ * dgel).astype(dgb_ref.dtype)
        a_scr[...] = a
        l_scr[...] = dy_ * gel

        def step(k, c):
            j = tb - 1 - k
            lam = l_scr[pl.ds(j, 1), :] + c
            l_scr[pl.ds(j, 1), :] = lam
            return a_scr[pl.ds(j, 1), :] * lam

        carry[0:1, :] = lax.fori_loop(0, tb, step, carry[0:1, :], unroll=8)
        lam = l_scr[...]
        dmult = lam * (ig * xb)
        dig = lam * (mult * xb)
        dxb = lam * (mult * ig)
        dlog_a = (lam * hs_m1) * a - dmult * (a * a) / mult
        dr = dlog_a * (-LRU_C * sp)
        dsp = jnp.sum(dlog_a * (-LRU_C * r), axis=0, keepdims=True)
        dpr = dr * (r * (1.0 - r))
        dpi = dig * (ig * (1.0 - ig))
        dwrg_ref[...] += _dot(xb, dpr, TN)
        dwig_ref[...] += _dot(xb, dpi, TN)
        dxb = dxb + _dot(dpr, wrg_ref[...], NT) + _dot(dpi, wig_ref[...], NT)
        for k in range(taps):
            dcv_ref[k:k + 1, :] += jnp.sum(dxb * xs[k], axis=0, keepdims=True)
        dcv_ref[taps:taps + 1, :] += jnp.sum(dxb, axis=0, keepdims=True)
        dcv_ref[taps + 1:taps + 2, :] += jnp.sum(dpr, axis=0, keepdims=True)
        dcv_ref[taps + 2:taps + 3, :] += jnp.sum(dpi, axis=0, keepdims=True)
        dcv_ref[taps + 3:taps + 4, :] += dsp * (-jax.nn.sigmoid(-row(taps + 3)))
        nxt8 = dxb_next[...]
        dxp = row(taps - 1) * dxb
        for k in range(taps - 1):
            dxp = dxp + row(k) * _shift_up(dxb, taps - 1 - k, nxt8)
        dxp_ref[...] = dxp.astype(dxp_ref.dtype)
        dxb_next[...] = dxb[0:SUBLANE]

    nrow = tb // SUBLANE
    blk = lambda off: pl.BlockSpec((None, tb, w), lambda q, i: (q + off, nb - 1 - i, 0))
    halo = lambda off: pl.BlockSpec((None, SUBLANE, w), lambda q, i: (q + off, jnp.maximum((nb - 1 - i) * nrow - 1, 0), 0))
    wspec = pl.BlockSpec((None, w, w), lambda q, i: (q, 0, 0))
    cspec = pl.BlockSpec((None,) + cvec.shape[1:], lambda q, i: (q, 0, 0))
    act = jax.ShapeDtypeStruct((half, t, w), MXU_DTYPE)
    return _call(
        body, name=name,
        out_shape=[act, act, jax.ShapeDtypeStruct(cvec.shape, F32), jax.ShapeDtypeStruct(wrg.shape, F32),
                   jax.ShapeDtypeStruct(wig.shape, F32)],
        grid=(half, nb),
        in_specs=[blk(0), blk(half), halo(half), blk(0), halo(0), blk(0), cspec, wspec, wspec],
        out_specs=[blk(0), blk(0), cspec, wspec, wspec],
        scratch_shapes=[pltpu.VMEM((tb, w), F32), pltpu.VMEM((tb, w), F32), pltpu.VMEM((SUBLANE, w), F32),
                        pltpu.VMEM((SUBLANE, w), F32)],
        args=(zz, zz, zz, hs, hs, dy, cvec, wrg, wig), rider=rider)


def _band(blocks, per):
    n, a, b = blocks.shape
    eye = jnp.eye(per, dtype=blocks.dtype)
    x = blocks.reshape(n // per, per, a, b)
    return jnp.einsum('sgab,gh->sgahb', x, eye).reshape(n // per, per * a, per * b)


def _unband(bands, per):
    s, pa, pb = bands.shape
    a, b = pa // per, pb // per
    x = bands.reshape(s, per, a, per, b)
    idx = jnp.arange(per)
    return x[:, idx, :, idx, :].transpose(1, 0, 2, 3).reshape(s * per, a, b)


def _pack(arrays, rows_multiple, lanes=LANE):
    flat = jnp.concatenate([a.reshape(-1).astype(F32) for a in arrays])
    rows = -(-flat.shape[0] // (lanes * rows_multiple)) * rows_multiple
    return jnp.pad(flat, (0, rows * lanes - flat.shape[0])).reshape(rows, lanes)


def _unpack(packed, shapes):
    flat = packed.reshape(-1)
    out, off = [], 0
    for s in shapes:
        n = math.prod(s)
        out.append(flat[off:off + n].reshape(s))
        off += n
    return out


def kernel(x, c, norm_g, w_ada, b_ada, s5_w_in, s5_lam_re, s5_lam_im, s5_log_dt, s5_b_re, s5_b_im, s5_c_re, s5_c_im, s5_d, s5_w_glu, lru_w_in, lru_conv_w, lru_conv_b, lru_w_rg, lru_b_rg, lru_w_ig, lru_b_ig, lru_lam, lru_w_out, ffn_w_gu, ffn_w_down, final_g, loss_target, m_norm_g, m_w_ada, m_b_ada, m_s5_w_in, m_s5_lam_re, m_s5_lam_im, m_s5_log_dt, m_s5_b_re, m_s5_b_im, m_s5_c_re, m_s5_c_im, m_s5_d, m_s5_w_glu, m_lru_w_in, m_lru_conv_w, m_lru_conv_b, m_lru_w_rg, m_lru_b_rg, m_lru_w_ig, m_lru_b_ig, m_lru_lam, m_lru_w_out, m_ffn_w_gu, m_ffn_w_down, m_final_g, v_norm_g, v_w_ada, v_b_ada, v_s5_w_in, v_s5_lam_re, v_s5_lam_im, v_s5_log_dt, v_s5_b_re, v_s5_b_im, v_s5_c_re, v_s5_c_im, v_s5_d, v_s5_w_glu, v_lru_w_in, v_lru_conv_w, v_lru_conv_b, v_lru_w_rg, v_lru_b_rg, v_lru_w_ig, v_lru_b_ig, v_lru_lam, v_lru_w_out, v_ffn_w_gu, v_ffn_w_down, v_final_g):
    wv = dict(zip(WEIGHTS, (norm_g, w_ada, b_ada, s5_w_in, s5_lam_re, s5_lam_im, s5_log_dt, s5_b_re, s5_b_im, s5_c_re, s5_c_im, s5_d, s5_w_glu, lru_w_in, lru_conv_w, lru_conv_b, lru_w_rg, lru_b_rg, lru_w_ig, lru_b_ig, lru_lam, lru_w_out, ffn_w_gu, ffn_w_down, final_g)))
    mv = dict(zip(WEIGHTS, (m_norm_g, m_w_ada, m_b_ada, m_s5_w_in, m_s5_lam_re, m_s5_lam_im, m_s5_log_dt, m_s5_b_re, m_s5_b_im, m_s5_c_re, m_s5_c_im, m_s5_d, m_s5_w_glu, m_lru_w_in, m_lru_conv_w, m_lru_conv_b, m_lru_w_rg, m_lru_b_rg, m_lru_w_ig, m_lru_b_ig, m_lru_lam, m_lru_w_out, m_ffn_w_gu, m_ffn_w_down, m_final_g)))
    vv = dict(zip(WEIGHTS, (v_norm_g, v_w_ada, v_b_ada, v_s5_w_in, v_s5_lam_re, v_s5_lam_im, v_s5_log_dt, v_s5_b_re, v_s5_b_im, v_s5_c_re, v_s5_c_im, v_s5_d, v_s5_w_glu, v_lru_w_in, v_lru_conv_w, v_lru_conv_b, v_lru_w_rg, v_lru_b_rg, v_lru_w_ig, v_lru_b_ig, v_lru_lam, v_lru_w_out, v_ffn_w_gu, v_ffn_w_down, v_final_g)))

    me = 4 * lax.axis_index("x") + 2 * lax.axis_index("y") + lax.axis_index("c")
    x0 = x[0]
    tgt = loss_target[0]
    t, d = x0.shape
    depth = norm_g.shape[0]
    n_mod = w_ada.shape[2] * N_DEV // d
    groups, states = s5_lam_re.shape[1], s5_lam_re.shape[2]
    per_sg = S5_SUPER // S5_GROUP
    nsg = groups // per_sg
    lw = lru_lam.shape[1] * N_DEV
    lwc = lw // (N_DEV // 2)
    half = N_DEV // 2

    assert depth == 2, "the ride schedule below is written for one S5 layer followed by one RG-LRU layer"
    wire = lambda a: a.astype(WIRE_DTYPE)
    gw = {'s5_in': _all_gather("ag_s5_w_in", wire(s5_w_in[0]))}

    def riding(job, fn, *args):
        res, (got,) = fn(*args, rider=_gather_rider([wire(job[1])]))
        gw[job[0]] = got
        return res

    sh_shapes = [wv[n].shape for n in SMALL_SHARDED] + [c.shape]
    sh_all = _all_gather("ag_small", _pack([wv[n] for n in SMALL_SHARDED] + [c], SUBLANE))
    sh_parts = [jnp.stack(p) for p in zip(*[_unpack(sh_all[s], sh_shapes) for s in range(N_DEV)])]
    full = {}
    for n, p in zip(SMALL_SHARDED, sh_parts[:-1]):
        full[n] = jnp.moveaxis(p, 0, -2).reshape(p.shape[1:-1] + (-1,))
    c_all = sh_parts[-1].reshape(N_DEV, d)
    c16 = jnp.pad(c_all, ((0, 2 * SUBLANE - N_DEV), (0, 0)))

    n_loc = w_ada.shape[2]
    b_loc = lax.dynamic_slice_in_dim(b_ada, me * n_loc, n_loc, axis=1)[:, None, :]
    mod_part = _ada_fwd("ada_fwd", c16, w_ada, b_loc)[:, :N_DEV]
    mod_mine = _chunk_exchange("x_mod", [mod_part.transpose(1, 0, 2)], ALL)
    mod = mod_mine.transpose(1, 0, 2).reshape(depth, n_mod, 1, d)

    lam3 = lambda a: a[0][:, None, :]
    p_lr, p_li, p_ld = lam3(s5_lam_re), lam3(s5_lam_im), s5_log_dt[0][:, None, None]
    p_br, p_bi = s5_b_re[0].transpose(0, 2, 1), s5_b_im[0].transpose(0, 2, 1)
    ab_re3, ab_im3, bb_re3, bb_im3 = _s5_disc("s5_disc", p_lr, p_li, p_ld, p_br, p_bi)
    ab_re, ab_im = ab_re3.reshape(nsg, per_sg * states), ab_im3.reshape(nsg, per_sg * states)
    bb_re, bb_im = _band(wire(bb_re3), per_sg), _band(wire(bb_im3), per_sg)
    cc_re = _band(wire(s5_c_re[0].transpose(0, 2, 1)), per_sg)
    cc_im = _band(wire(s5_c_im[0].transpose(0, 2, 1)), per_sg)

    taps = lru_conv_w.shape[1]
    cvec = jnp.concatenate([full['lru_conv_w'].reshape(taps, lw), full['lru_conv_b'], full['lru_b_rg'],
                            full['lru_b_ig'], full['lru_lam']], axis=0)
    cvec = cvec.reshape(taps + 4, half, lwc).transpose(1, 0, 2)
    wrg = _band(wire(lru_w_rg[0]), LRU_BLOCKS_PER_CHUNK)
    wig = _band(wire(lru_w_ig[0]), LRU_BLOCKS_PER_CHUNK)

    saved = []
    xc = x0
    for i in range(depth):
        sh1, sc1, g1, sh2, sc2, g2 = [mod[i, k] for k in range(n_mod)]
        gn = full['norm_g'][i]
        h1 = _norm_mod_fwd(f"norm1_fwd{i}", xc, gn[0:1], sc1, sh1)
        if i % 2 == 0:
            u = riding(('s5_glu', s5_w_glu[0]), _mm_row, f"s5_in{i}", h1[None], gw['s5_in'].reshape(d, d))
            s_re, s_im, s_rem, s_imm = riding((('gu', i), ffn_w_gu[i]), _s5_scan_fwd, f"s5_scan{i}", u, bb_re, bb_im,
                                              ab_re, ab_im)
            ypre, yact = _s5_out_fwd(f"s5_out{i}", s_rem, s_imm, cc_re, cc_im, u, s5_d)
            z = riding((('down', i), ffn_w_down[i]), _mm_col, f"s5_glu{i}", yact, gw['s5_glu'])
            x1 = _glu_resid_fwd(f"s5_resid{i}", z, xc, g1)
            mix = (u, s_re, s_im, s_rem, s_imm, ypre, yact, z)
        else:
            zz = _mm_col(f"lru_in{i}", h1, gw['lru_in'])
            hs, ylru = riding((('gu', i), ffn_w_gu[i]), _lru_fwd, f"lru_core{i}", zz, cvec, wrg, wig)
            o = _mm_row(f"lru_out{i}", ylru, gw['lru_out'].reshape(lw, d))
            x1 = _resid(f"lru_resid{i}", xc, o, g1)
            mix = (zz, hs, ylru, o)
        h2 = _norm_mod_fwd(f"norm2_fwd{i}", x1, gn[1:2], sc2, sh2)
        if i % 2 == 0:
            gu = riding(('lru_in', lru_w_in[0]), _mm_col, f"ffn_gu{i}", h2, gw['gu', i])
            act = _swiglu_act_fwd(f"ffn_act{i}", gu)
            f = riding(('lru_out', lru_w_out[0]), _mm_row, f"ffn_down{i}", act, gw['down', i].reshape(-1, d))
        else:
            gu = riding((('down', i), ffn_w_down[i]), _mm_col, f"ffn_gu{i}", h2, gw['gu', i])
            act = _swiglu_act_fwd(f"ffn_act{i}", gu)
            f = _mm_row(f"ffn_down{i}", act, gw['down', i].reshape(-1, d))
        x2 = _resid(f"ffn_resid{i}", x1, f, g2)
        saved.append((xc, h1, mix, x1, h2, gu, act, f))
        xc = x2

    dx, loss_part, d_final_g = _loss_bwd("loss", xc, tgt, final_g[None])
    loss = lax.psum(loss_part[0, 0], ("x", "y", "c"))

    grads = {}
    parts = {}
    dmod = [None] * depth
    d_norm_g = [None] * depth

    def chip_sums(name, partial):
        return _chip_sums(name, partial.reshape(N_DEV, -1, partial.shape[-1]))

    def riding_x(key, sums, fn, *args):
        res, parts[key] = fn(*args, rider=_chunk_rider([sums], SAME_CORE))
        return res

    waiting = None
    for i in reversed(range(depth)):
        xin, h1, mix, x1, h2, gu, act, f = saved[i]
        sh1, sc1, g1, sh2, sc2, g2 = [mod[i, k] for k in range(n_mod)]
        gn = full['norm_g'][i]
        g_down = gw['down', i].reshape(-1, d)
        df, dg2 = _gate_bwd(f"ffn_gate_bwd{i}", dx, f, g2)
        dact = _mm_row_da(f"ffn_down_da{i}", df, g_down, half)
        s_down = chip_sums(f"x_ffn_w_down{i}", _mm_row_db(f"ffn_down_db{i}", act, df, WIRE_DTYPE))
        if waiting is None:
            dgu = riding_x(('ffn_w_down', i), s_down, _swiglu_act_bwd, f"ffn_act_bwd{i}", gu, dact)
            dh2 = _mm_col_da(f"ffn_gu_da{i}", dgu, gw['gu', i])
        else:
            dgu = riding_x(*waiting, _swiglu_act_bwd, f"ffn_act_bwd{i}", gu, dact)
            dh2 = riding_x(('ffn_w_down', i), s_down, _mm_col_da, f"ffn_gu_da{i}", dgu, gw['gu', i])
        s_gu = chip_sums(f"x_ffn_w_gu{i}", _mm_col_db(f"ffn_gu_db{i}", h2, dgu, WIRE_DTYPE))
        dx, dgn2, dsc2, dsh2 = _norm_mod_bwd(f"norm2_bwd{i}", x1, dh2, dx, gn[1:2], sc2)
        if i % 2 == 0:
            u, s_re, s_im, s_rem, s_imm, ypre, yact, z = mix
            dz, dg1 = _glu_resid_bwd(f"s5_resid_bwd{i}", z, dx, g1)
            dyact = _mm_col_da(f"s5_glu_da{i}", dz, gw['s5_glu'])
            s_glu = chip_sums("x_s5_w_glu", _mm_col_db(f"s5_glu_db{i}", yact, dz, WIRE_DTYPE))
            dyp = _gelu_bwd(f"s5_gelu_bwd{i}", dyact, ypre)
            l_rem, l_imm, dab_re, dab_im = riding_x(('ffn_w_gu', i), s_gu, _s5_scan_bwd, f"s5_scan_bwd{i}", dyp, cc_re,
                                                    cc_im, ab_re, ab_im, s_re, s_im)
            du, dbb_re, dbb_im, dcc_re, dcc_im, dd = riding_x(('s5_w_glu', 0), s_glu, _s5_grads, f"s5_grads{i}", l_rem,
                                                              l_imm, s_rem, s_imm, u, dyp, bb_re, bb_im, s5_d)
            dlr, dli, dld, dbr, dbi = _s5_disc_bwd(
                "s5_disc_bwd", p_lr, p_li, p_ld, p_br, p_bi, dab_re.reshape(groups, 1, states),
                dab_im.reshape(groups, 1, states), _unband(dbb_re, per_sg), _unband(dbb_im, per_sg))
            grads['s5_lam_re'], grads['s5_lam_im'], grads['s5_log_dt'] = dlr[:, 0][None], dli[:, 0][None], dld[:, 0, 0][None]
            grads['s5_b_re'], grads['s5_b_im'] = dbr.transpose(0, 2, 1)[None], dbi.transpose(0, 2, 1)[None]
            grads['s5_c_re'] = _unband(dcc_re, per_sg).transpose(0, 2, 1)[None]
            grads['s5_c_im'] = _unband(dcc_im, per_sg).transpose(0, 2, 1)[None]
            grads['s5_d'] = dd
            dub = du.astype(MXU_DTYPE)
            s_s5_in = chip_sums("x_s5_w_in", _mm_row_db(f"s5_in_db{i}", h1[None], dub, WIRE_DTYPE))
            dh1 = riding_x(('s5_w_in', 0), s_s5_in, _mm_row_da, f"s5_in_da{i}", dub, gw['s5_in'].reshape(d, d), 1)[0]
        else:
            zz, hs, ylru, o = mix
            g_lru_out = gw['lru_out'].reshape(lw, d)
            do, dg1 = _gate_bwd(f"lru_gate_bwd{i}", dx, o, g1)
            dyl = _mm_row_da(f"lru_out_da{i}", do, g_lru_out, half)
            s_lru_out = chip_sums("x_lru_w_out", _mm_row_db(f"lru_out_db{i}", ylru, do, WIRE_DTYPE))
            dgb, dxp, dcv, dwrg, dwig = riding_x(('ffn_w_gu', i), s_gu, _lru_bwd, f"lru_core_bwd{i}", zz, hs, dyl, cvec,
                                                 wrg, wig)
            dzz = jnp.concatenate([dgb, dxp], axis=0)
            dh1 = riding_x(('lru_w_out', 0), s_lru_out, _mm_col_da, f"lru_in_da{i}", dzz, gw['lru_in'])
            waiting = (('lru_w_in', 0), chip_sums("x_lru_w_in", _mm_col_db(f"lru_in_db{i}", h1, dzz, WIRE_DTYPE)))
            dcv = dcv.transpose(1, 0, 2).reshape(taps + 4, lw)
            grads['lru_conv_w'] = dcv[:taps].reshape(1, taps, 1, lw)
            grads['lru_conv_b'], grads['lru_b_rg'] = dcv[taps:taps + 1], dcv[taps + 1:taps + 2]
            grads['lru_b_ig'], grads['lru_lam'] = dcv[taps + 2:taps + 3], dcv[taps + 3:taps + 4]
            grads['lru_w_rg'] = _unband(dwrg, LRU_BLOCKS_PER_CHUNK)[None]
            grads['lru_w_ig'] = _unband(dwig, LRU_BLOCKS_PER_CHUNK)[None]
        dx, dgn1, dsc1, dsh1 = _norm_mod_bwd(f"norm1_bwd{i}", xin, dh1, dx, gn[0:1], sc1)
        dmod[i] = jnp.concatenate([dsh1, dsc1, dg1, dsh2, dsc2, dg2], axis=1)
        d_norm_g[i] = jnp.concatenate([dgn1, dgn2], axis=0)
    grad_x = dx[None]
    dmod = jnp.concatenate(dmod, axis=0)
    grads['norm_g'] = jnp.stack(d_norm_g)
    grads['b_ada'] = dmod
    grads['final_g'] = d_final_g[0]

    small_partial = _pack([grads[n] for n in SMALL], SUBLANE * N_DEV)
    rows8 = small_partial.shape[0] // N_DEV
    s_small = _chip_sums("x_small", small_partial.reshape(N_DEV, rows8, LANE))

    out = {}
    dmod_all = _all_gather("ag_dmod", dmod)
    dmod_loc = lax.dynamic_slice_in_dim(dmod_all, me * n_loc, n_loc, axis=2).transpose(1, 0, 2)
    dmod16 = jnp.pad(dmod_loc, ((0, 0), (0, 2 * SUBLANE - N_DEV), (0, 0)))
    out['w_ada'] = _adamw_w_ada("adamw_w_ada", c16, dmod16, w_ada, m_w_ada, v_w_ada)

    for name in BIG[1:]:
        w = wv[name]
        rows, cols = w.shape[-2] * w.shape[0], w.shape[-1]
        flat = lambda a: a.reshape(rows, cols)
        res = _adamw_sum("adamw_" + name, [parts[name, l] for l in range(w.shape[0])], flat(w), flat(mv[name]),
                         flat(vv[name]))
        out[name] = [r.reshape(w.shape) for r in res]

    summed = _sum_parts("sum_small", _chunk_exchange("x_small_ici", [s_small], SAME_CORE))
    small_total = _all_gather("ag_small_sum", summed).reshape(-1, LANE)
    small_grad = dict(zip(SMALL, _unpack(small_total, [grads[n].shape for n in SMALL])))
    for n in SMALL_SHARDED:
        shard = wv[n].shape[-1]
        small_grad[n] = lax.dynamic_slice_in_dim(small_grad[n], me * shard, shard, axis=small_grad[n].ndim - 1)
    small_shapes = [wv[n].shape for n in SMALL]
    pk = lambda dct: _pack([dct[n] for n in SMALL], 2 * SUBLANE, 8 * LANE)
    s_out = _adamw_sum("adamw_small", [pk(small_grad)[None]], pk(wv), pk(mv), pk(vv))
    out.update({n: r for n, *r in zip(SMALL, *[_unpack(o, small_shapes) for o in s_out])})

    return (loss, grad_x, *[out[n][0] for n in WEIGHTS], *[out[n][1] for n in WEIGHTS],
            *[out[n][2] for n in WEIGHTS], *[out[n][3] for n in WEIGHTS])
```

```python
import functools
import math

import jax
import jax.numpy as jnp
from jax import lax
from jax.experimental import pallas as pl
from jax.experimental.pallas import tpu as pltpu

F32 = jnp.float32
MXU_DTYPE = jnp.bfloat16
WIRE_DTYPE = jnp.bfloat16
N_DEV = 8
EPS = 1e-6
LRU_C = 8.0
S5_GROUP = 16
S5_STATE = 64
S5_SUPER = 256
LRU_BLOCKS_PER_CHUNK = 4
ADAM_LR, ADAM_B1, ADAM_B2, ADAM_EPS, ADAM_WD, ADAM_STEP = 0.001, 0.9, 0.999, 1e-08, 0.01, 10
VMEM_LIMIT_BYTES = 56 * 1024 * 1024
LANE = 128
SUBLANE = 8

WEIGHTS = ['norm_g', 'w_ada', 'b_ada', 's5_w_in', 's5_lam_re', 's5_lam_im', 's5_log_dt', 's5_b_re', 's5_b_im',
           's5_c_re', 's5_c_im', 's5_d', 's5_w_glu', 'lru_w_in', 'lru_conv_w', 'lru_conv_b', 'lru_w_rg', 'lru_b_rg',
           'lru_w_ig', 'lru_b_ig', 'lru_lam', 'lru_w_out', 'ffn_w_gu', 'ffn_w_down', 'final_g']
BIG = ('w_ada', 's5_w_in', 's5_w_glu', 'lru_w_in', 'lru_w_out', 'ffn_w_gu', 'ffn_w_down')
SMALL = tuple(n for n in WEIGHTS if n not in BIG)
SMALL_SHARDED = ('norm_g', 'lru_conv_w', 'lru_conv_b', 'lru_b_rg', 'lru_b_ig', 'lru_lam')

NN = (((1,), (0,)), ((), ()))
NT = (((1,), (1,)), ((), ()))
TN = (((0,), (0,)), ((), ()))


def _params(n_grid):
    return pltpu.CompilerParams(dimension_semantics=("arbitrary",) * n_grid, vmem_limit_bytes=VMEM_LIMIT_BYTES)


def _tile(dim, pref, align=LANE):
    if dim <= pref:
        return dim
    t = (pref // align) * align
    while t >= align:
        if dim % t == 0:
            return t
        t -= align
    return dim


def _dot(a, b, dims):
    return lax.dot_general(a.astype(MXU_DTYPE), b.astype(MXU_DTYPE), dims, preferred_element_type=F32)


def _gelu(x):
    k = math.sqrt(2.0 / math.pi)
    return 0.5 * x * (1.0 + jnp.tanh(k * (x + 0.044715 * (x * x * x))))


def _gelu_and_grad(x):
    k = math.sqrt(2.0 / math.pi)
    th = jnp.tanh(k * (x + 0.044715 * (x * x * x)))
    g = 0.5 * x * (1.0 + th)
    dg = 0.5 * (1.0 + th) + 0.5 * x * (1.0 - th * th) * (k * (1.0 + 3.0 * 0.044715 * (x * x)))
    return g, dg


def _neg_expm1(x):
    series = -x * (1.0 + x * (0.5 + x * (1.0 / 6.0 + x * (1.0 / 24.0 + x * (1.0 / 120.0)))))
    return jnp.where(x > -0.01, series, 1.0 - jnp.exp(x))


MESH = pl.DeviceIdType.MESH
N_CHIP = N_DEV // 2
ALL, SAME_CORE = 7, 6


def _place():
    x, y, c = lax.axis_index("x"), lax.axis_index("y"), lax.axis_index("c")
    return x, y, c


def _flip(place, k):
    x, y, c = place
    return (1 - x if (k >> 2) & 1 else x, 1 - y if (k >> 1) & 1 else y, 1 - c if k & 1 else c)


def _chunk_exchange(name, xs, group):
    return _ride_alone(name, _chunk_rider(xs, group))


class _Rider:
    def __init__(self, arrays, out_shape, scratch, start, finish, post):
        self.arrays, self.out_shape, self.scratch = list(arrays), list(out_shape), list(scratch)
        self.start, self.finish, self.post = start, finish, post


def _chunk_rider(xs, group):
    n = len(xs)
    members, r, c_ = xs[0].shape
    assert members == {ALL: N_DEV, SAME_CORE: N_CHIP}[group]
    assert all(a.shape == xs[0].shape and a.dtype == xs[0].dtype for a in xs)
    ks = [k for k in range(1, N_DEV) if not k & ~group]
    member = (lambda p: 4 * p[0] + 2 * p[1] + p[2]) if group == ALL else (lambda p: 2 * p[0] + p[1])

    def copies(ins, outs, scratch):
        out = outs[0]
        send_sems, recv_sems, local_sems = scratch
        place = _place()
        me = member(place)
        local = [pltpu.make_async_copy(ins[l].at[me], out.at[me, l], local_sems.at[l]) for l in range(n)]
        remote = []
        for l in range(n):
            for k in ks:
                pid = _flip(place, k)
                peer = member(pid)

                def copy(land_at, l=l, k=k, peer=peer, pid=pid):
                    return pltpu.make_async_remote_copy(
                        src_ref=ins[l].at[peer], dst_ref=out.at[land_at, l], send_sem=send_sems.at[l * N_DEV + k],
                        recv_sem=recv_sems.at[l * N_DEV + k], device_id=pid, device_id_type=MESH)

                remote.append((copy, me, peer))
        return local, remote

    def start(ins, outs, scratch):
        local, remote = copies(ins, outs, scratch)
        for cp in local:
            cp.start()
        for copy, me, _ in remote:
            copy(me).start()

    def finish(ins, outs, scratch):
        local, remote = copies(ins, outs, scratch)
        for copy, me, peer in remote:
            copy(me).wait_send()
            copy(peer).wait_recv()
        for cp in local:
            cp.wait()

    return _Rider(
        xs, [jax.ShapeDtypeStruct((members, n, r, c_), xs[0].dtype)],
        [pltpu.SemaphoreType.DMA((n * N_DEV,)), pltpu.SemaphoreType.DMA((n * N_DEV,)), pltpu.SemaphoreType.DMA((n,))],
        start, finish, lambda outs: outs[0].reshape(members, n * r, c_))


def _gather_rider(xs):
    n = len(xs)
    chip_flips = (2, 4, 6)
    per = 1 + 2 * len(chip_flips)

    def plan(ins, outs, scratch):
        send_sems, recv_sems, local_sems = scratch
        place = _place()
        sibling = _flip(place, 1)
        jobs = []
        for l in range(n):
            slot = lambda p, l=l: outs[l].at[2 * p[0] + p[1], p[2]]

            def copy(k, block, to, src=None, l=l, slot=slot):
                return pltpu.make_async_remote_copy(
                    src_ref=slot(block) if src is None else src, dst_ref=slot(block), send_sem=send_sems.at[l * per + k],
                    recv_sem=recv_sems.at[l * per + k], device_id=to, device_id_type=MESH)

            mine = pltpu.make_async_copy(ins[l], slot(place), local_sems.at[l])
            first = [copy(0, place, sibling, src=ins[l])]
            first += [copy(1 + j, place, _flip(place, k), src=ins[l]) for j, k in enumerate(chip_flips)]
            jobs.append((copy, mine, first))
        return place, sibling, jobs

    def start(ins, outs, scratch):
        _, _, jobs = plan(ins, outs, scratch)
        for _, mine, first in jobs:
            mine.start()
            for cp in first:
                cp.start()

    def finish(ins, outs, scratch):
        place, sibling, jobs = plan(ins, outs, scratch)
        passed = []
        for copy, _, _ in jobs:
            for j, k in enumerate(chip_flips):
                copy(1 + j, _flip(place, k), place).wait_recv()
                fwd = copy(4 + j, _flip(place, k), sibling)
                fwd.start()
                passed.append(fwd)
        for copy, mine, first in jobs:
            copy(0, sibling, place).wait_recv()
            for j, k in enumerate(chip_flips):
                copy(4 + j, _flip(sibling, k), place).wait_recv()
            for cp in first:
                cp.wait_send()
            mine.wait()
        for cp in passed:
            cp.wait_send()

    return _Rider(
        xs, [jax.ShapeDtypeStruct((N_CHIP, 2) + x.shape, x.dtype) for x in xs],
        [pltpu.SemaphoreType.DMA((n * per,)), pltpu.SemaphoreType.DMA((n * per,)), pltpu.SemaphoreType.DMA((n,))],
        start, finish, lambda outs: [o.reshape((N_DEV,) + x.shape) for o, x in zip(outs, xs)])


HBM_SPEC = pl.BlockSpec(memory_space=pltpu.HBM)


def _ride_alone(name, rider):
    n_in, n_out = len(rider.arrays), len(rider.out_shape)

    def body(*refs):
        parts = refs[:n_in], refs[n_in:n_in + n_out], refs[n_in + n_out:]
        rider.start(*parts)
        rider.finish(*parts)

    outs = pl.pallas_call(body, name=name, out_shape=rider.out_shape, in_specs=[HBM_SPEC] * n_in,
                          out_specs=[HBM_SPEC] * n_out, scratch_shapes=rider.scratch)(*rider.arrays)
    return rider.post(list(outs))


def _call(body, *, name, grid, in_specs, out_specs, out_shape, scratch_shapes=(), args, rider=None):
    single = not isinstance(out_shape, (list, tuple))
    out_shape = [out_shape] if single else list(out_shape)
    out_specs = [out_specs] if single else list(out_specs)
    scratch_shapes = list(scratch_shapes)
    unwrap = lambda outs: outs[0] if single else list(outs)
    if rider is None:
        outs = pl.pallas_call(body, name=name, grid=grid, in_specs=list(in_specs), out_specs=out_specs, out_shape=out_shape,
                              scratch_shapes=scratch_shapes, compiler_params=_params(len(grid)))(*args)
        return unwrap(outs)
    n_in, n_out, n_scr = len(in_specs), len(out_shape), len(scratch_shapes)
    r_in, r_out = len(rider.arrays), len(rider.out_shape)

    def carried(*refs):
        ins, refs = refs[:n_in], refs[n_in:]
        r_ins, refs = refs[:r_in], refs[r_in:]
        outs, refs = refs[:n_out], refs[n_out:]
        r_outs, refs = refs[:r_out], refs[r_out:]
        scr, r_scr = refs[:n_scr], refs[n_scr:]
        steps = [pl.program_id(ax) for ax in range(len(grid))]

        @pl.when(functools.reduce(jnp.logical_and, [s == 0 for s in steps]))
        def _():
            rider.start(r_ins, r_outs, r_scr)

        body(*ins, *outs, *scr)

        @pl.when(functools.reduce(jnp.logical_and, [s == g - 1 for s, g in zip(steps, grid)]))
        def _():
            rider.finish(r_ins, r_outs, r_scr)

    outs = pl.pallas_call(
        carried, name=name, grid=grid, in_specs=list(in_specs) + [HBM_SPEC] * r_in, out_specs=out_specs + [HBM_SPEC] * r_out,
        out_shape=out_shape + rider.out_shape, scratch_shapes=scratch_shapes + rider.scratch,
        compiler_params=_params(len(grid)))(*args, *rider.arrays)
    return unwrap(outs[:n_out]), rider.post(list(outs[n_out:]))


def _all_gather(name, x):
    return _ride_alone(name, _gather_rider([x]))[0]


def _sibling_exchange(name, xs):
    n = len(xs)
    _, r, c_ = xs[0].shape
    assert all(a.shape == xs[0].shape and a.dtype == xs[0].dtype for a in xs)

    def body(*refs):
        ins, out = refs[:n], refs[n]
        send_sems, recv_sems = refs[n + 1:]
        place = _place()
        c = place[2]
        sibling = _flip(place, 1)
        copies = []
        for l in range(n):
            for chip in range(N_CHIP):
                cp = pltpu.make_async_remote_copy(
                    src_ref=ins[l].at[2 * chip + (1 - c)], dst_ref=out.at[chip, l], send_sem=send_sems.at[l * N_CHIP + chip],
                    recv_sem=recv_sems.at[l * N_CHIP + chip], device_id=sibling, device_id_type=MESH)
                cp.start()
                copies.append(cp)
        for cp in copies:
            cp.wait()

    hbm = pl.BlockSpec(memory_space=pltpu.HBM)
    return pl.pallas_call(
        body, name=name, out_shape=jax.ShapeDtypeStruct((N_CHIP, n, r, c_), xs[0].dtype), in_specs=[hbm] * n, out_specs=hbm,
        scratch_shapes=[pltpu.SemaphoreType.DMA((n * N_CHIP,)), pltpu.SemaphoreType.DMA((n * N_CHIP,))],
    )(*xs)


def _pair_sum(name, x, got, l, core):
    _, r, c_ = x.shape
    br = _tile(r, 256, 2 * SUBLANE)

    def body(core_ref, x_ref, g_ref, o_ref):
        o_ref[...] = (x_ref[...].astype(F32) + g_ref[...].astype(F32)).astype(o_ref.dtype)

    return pl.pallas_call(
        body, name=name, out_shape=jax.ShapeDtypeStruct((N_CHIP, r, c_), x.dtype),
        grid_spec=pltpu.PrefetchScalarGridSpec(
            num_scalar_prefetch=1, grid=(N_CHIP, r // br),
            in_specs=[pl.BlockSpec((None, br, c_), lambda ch, i, core_ref: (2 * ch + core_ref[0], i, 0)),
                      pl.BlockSpec((None, None, br, c_), lambda ch, i, core_ref: (ch, l, i, 0))],
            out_specs=pl.BlockSpec((None, br, c_), lambda ch, i, core_ref: (ch, i, 0))),
        compiler_params=_params(2))(core, x, got)


def _chip_sums(name, x):
    got = _sibling_exchange(name + "_d2d", [x])
    core = lax.axis_index("c").astype(jnp.int32).reshape(1)
    return _pair_sum(name + "_pair", x, got, 0, core)


def _mm(name, a, b, out_shape, out_dtype, grid, a_spec, b_spec, o_spec, dims, n_red, acc_shape, rider=None):
    red = tuple(range(len(grid) - n_red, len(grid)))
    out_type = jax.ShapeDtypeStruct(out_shape, out_dtype)
    if all(grid[ax] == 1 for ax in red):
        def single(a_ref, b_ref, o_ref):
            o_ref[...] = _dot(a_ref[...], b_ref[...], dims).astype(o_ref.dtype)

        return _call(single, name=name, out_shape=out_type, grid=grid, in_specs=[a_spec, b_spec], out_specs=o_spec,
                     args=(a, b), rider=rider)

    def body(a_ref, b_ref, o_ref, acc_ref):
        first = functools.reduce(jnp.logical_and, [pl.program_id(ax) == 0 for ax in red])
        last = functools.reduce(jnp.logical_and, [pl.program_id(ax) == grid[ax] - 1 for ax in red])

        @pl.when(first)
        def _():
            acc_ref[...] = jnp.zeros_like(acc_ref)

        acc_ref[...] += _dot(a_ref[...], b_ref[...], dims)

        @pl.when(last)
        def _():
            o_ref[...] = acc_ref[...].astype(o_ref.dtype)

    return _call(body, name=name, out_shape=out_type, grid=grid, in_specs=[a_spec, b_spec], out_specs=o_spec,
                 scratch_shapes=[pltpu.VMEM(acc_shape, F32)], args=(a, b), rider=rider)


def _mm_col(name, a, b, out_dtype=F32, rider=None):
    m, k = a.shape
    j, _, n = b.shape
    bm, bk = _tile(m, 1024), _tile(k, 2048)
    return _mm(name, a, b, (j, m, n), out_dtype, (j, m // bm, k // bk),
               pl.BlockSpec((bm, bk), lambda jj, mm, kk: (mm, kk)),
               pl.BlockSpec((None, bk, n), lambda jj, mm, kk: (jj, kk, 0)),
               pl.BlockSpec((None, bm, n), lambda jj, mm, kk: (jj, mm, 0)), NN, 1, (bm, n), rider)


def _mm_col_da(name, do, b, rider=None):
    j, m, n = do.shape
    k = b.shape[1]
    bm, bk = _tile(m, 1024), _tile(k, 1024)
    return _mm(name, do, b, (m, k), F32, (m // bm, k // bk, j),
               pl.BlockSpec((None, bm, n), lambda mm, kk, jj: (jj, mm, 0)),
               pl.BlockSpec((None, bk, n), lambda mm, kk, jj: (jj, kk, 0)),
               pl.BlockSpec((bm, bk), lambda mm, kk, jj: (mm, kk)), NT, 1, (bm, bk), rider)


def _mm_col_db(name, a, do, out_dtype):
    m, k = a.shape
    j, _, n = do.shape
    bm, bk = _tile(m, 2048), _tile(k, 512)
    return _mm(name, a, do, (j, k, n), out_dtype, (j, k // bk, m // bm),
               pl.BlockSpec((bm, bk), lambda jj, kk, mm: (mm, kk)),
               pl.BlockSpec((None, bm, n), lambda jj, kk, mm: (jj, mm, 0)),
               pl.BlockSpec((None, bk, n), lambda jj, kk, mm: (jj, kk, 0)), TN, 1, (bk, n))


def _row_bk(kq):
    return kq if (kq % LANE or kq // LANE in (11,)) else _tile(kq, 2048)


def _mm_row(name, a, b, out_dtype=F32, rider=None):
    q, m, kq = a.shape
    n = b.shape[1]
    bm, bn, bk = _tile(m, 1024), _tile(n, 1024), _row_bk(kq)
    nk = kq // bk
    return _mm(name, a, b, (m, n), out_dtype, (m // bm, n // bn, q, nk),
               pl.BlockSpec((None, bm, bk), lambda mm, nn, qq, kk: (qq, mm, kk)),
               pl.BlockSpec((bk, bn), lambda mm, nn, qq, kk: (qq * nk + kk, nn)),
               pl.BlockSpec((bm, bn), lambda mm, nn, qq, kk: (mm, nn)), NN, 2, (bm, bn), rider)


def _mm_row_da(name, do, b, q, rider=None):
    m, n = do.shape
    kq = b.shape[0] // q
    bm, bn = _tile(m, 1024), _tile(n, 2048)
    return _mm(name, do, b, (q, m, kq), F32, (q, m // bm, n // bn),
               pl.BlockSpec((bm, bn), lambda qq, mm, nn: (mm, nn)),
               pl.BlockSpec((kq, bn), lambda qq, mm, nn: (qq, nn)),
               pl.BlockSpec((None, bm, kq), lambda qq, mm, nn: (qq, mm, 0)), NT, 1, (bm, kq), rider)


def _mm_row_db(name, a, do, out_dtype):
    q, m, kq = a.shape
    n = do.shape[1]
    bm, bn = _tile(m, 2048), _tile(n, 512)
    return _mm(name, a, do, (q * kq, n), out_dtype, (q, n // bn, m // bm),
               pl.BlockSpec((None, bm, kq), lambda qq, nn, mm: (qq, mm, 0)),
               pl.BlockSpec((bm, bn), lambda qq, nn, mm: (mm, nn)),
               pl.BlockSpec((kq, bn), lambda qq, nn, mm: (qq, nn)), TN, 1, (kq, bn))


def _row_spec(bm, d):
    return pl.BlockSpec((bm, d), lambda i: (i, 0))


def _vec_spec(d):
    return pl.BlockSpec((1, d), lambda i: (0, 0))


def _norm_mod_fwd(name, x, gain, sc, sh):
    t, d = x.shape
    bm = _tile(t, 256, SUBLANE)

    def body(x_ref, g_ref, sc_ref, sh_ref, h_ref):
        xv = x_ref[...]
        rstd = lax.rsqrt(jnp.mean(xv * xv, axis=-1, keepdims=True) + EPS)
        h_ref[...] = ((xv * rstd) * g_ref[...] * (1.0 + sc_ref[...]) + sh_ref[...]).astype(h_ref.dtype)

    return pl.pallas_call(
        body, name=name, out_shape=jax.ShapeDtypeStruct((t, d), MXU_DTYPE), grid=(t // bm,),
        in_specs=[_row_spec(bm, d), _vec_spec(d), _vec_spec(d), _vec_spec(d)], out_specs=_row_spec(bm, d),
        compiler_params=_params(1))(x, gain, sc, sh)


def _norm_mod_bwd(name, x, dh, dres, gain, sc):
    t, d = x.shape
    bm = _tile(t, 256, SUBLANE)

    def body(x_ref, dh_ref, dres_ref, g_ref, sc_ref, dx_ref, dg_ref, dsc_ref, dsh_ref):
        @pl.when(pl.program_id(0) == 0)
        def _():
            dg_ref[...] = jnp.zeros_like(dg_ref)
            dsc_ref[...] = jnp.zeros_like(dsc_ref)
            dsh_ref[...] = jnp.zeros_like(dsh_ref)

        xv, dh_ = x_ref[...], dh_ref[...]
        rstd = lax.rsqrt(jnp.mean(xv * xv, axis=-1, keepdims=True) + EPS)
        nrm = xv * rstd
        gain_ = g_ref[...]
        dsh_ref[...] += jnp.sum(dh_, axis=0, keepdims=True)
        dsc_ref[...] += jnp.sum(dh_ * (nrm * gain_), axis=0, keepdims=True)
        dhn = dh_ * (1.0 + sc_ref[...])
        dg_ref[...] += jnp.sum(dhn * nrm, axis=0, keepdims=True)
        dn = dhn * gain_
        dx_ref[...] = dres_ref[...] + rstd * (dn - nrm * jnp.mean(dn * nrm, axis=-1, keepdims=True))

    vec = jax.ShapeDtypeStruct((1, d), F32)
    return pl.pallas_call(
        body, name=name, out_shape=[jax.ShapeDtypeStruct((t, d), F32), vec, vec, vec], grid=(t // bm,),
        in_specs=[_row_spec(bm, d), _row_spec(bm, d), _row_spec(bm, d), _vec_spec(d), _vec_spec(d)],
        out_specs=[_row_spec(bm, d), _vec_spec(d), _vec_spec(d), _vec_spec(d)],
        compiler_params=_params(1))(x, dh, dres, gain, sc)


def _loss_bwd(name, x, target, gain):
    t, d = x.shape
    bm = _tile(t, 256, SUBLANE)

    def body(x_ref, t_ref, g_ref, dx_ref, loss_ref, dg_ref):
        @pl.when(pl.program_id(0) == 0)
        def _():
            loss_ref[...] = jnp.zeros_like(loss_ref)
            dg_ref[...] = jnp.zeros_like(dg_ref)

        xv = x_ref[...]
        rstd = lax.rsqrt(jnp.mean(xv * xv, axis=-1, keepdims=True) + EPS)
        nrm = xv * rstd
        gain_ = g_ref[...]
        err = nrm * gain_ - t_ref[...]
        per_tok = jnp.mean(err * err, axis=-1, keepdims=True)
        loss_ref[...] += 0.5 * jnp.sum(per_tok, axis=0, keepdims=True)
        dout = err * (1.0 / d)
        dg_ref[...] += jnp.sum(dout * nrm, axis=0, keepdims=True)
        dn = dout * gain_
        dx_ref[...] = rstd * (dn - nrm * jnp.mean(dn * nrm, axis=-1, keepdims=True))

    return pl.pallas_call(
        body, name=name,
        out_shape=[jax.ShapeDtypeStruct((t, d), F32), jax.ShapeDtypeStruct((1, 1), F32),
                   jax.ShapeDtypeStruct((1, d), F32)],
        grid=(t // bm,), in_specs=[_row_spec(bm, d), _row_spec(bm, d), _vec_spec(d)],
        out_specs=[_row_spec(bm, d), pl.BlockSpec((1, 1), lambda i: (0, 0)), _vec_spec(d)],
        compiler_params=_params(1))(x, target, gain)


def _resid(name, x, y, g):
    t, d = x.shape
    bm = _tile(t, 256, SUBLANE)

    def body(x_ref, y_ref, g_ref, o_ref):
        o_ref[...] = x_ref[...] + g_ref[...] * y_ref[...]

    return pl.pallas_call(
        body, name=name, out_shape=jax.ShapeDtypeStruct((t, d), F32), grid=(t // bm,),
        in_specs=[_row_spec(bm, d), _row_spec(bm, d), _vec_spec(d)], out_specs=_row_spec(bm, d),
        compiler_params=_params(1))(x, y, g)


def _gate_bwd(name, dx, y, g):
    t, d = dx.shape
    bm = _tile(t, 256, SUBLANE)

    def body(dx_ref, y_ref, g_ref, dy_ref, dg_ref):
        @pl.when(pl.program_id(0) == 0)
        def _():
            dg_ref[...] = jnp.zeros_like(dg_ref)

        dxv = dx_ref[...]
        dy_ref[...] = (g_ref[...] * dxv).astype(dy_ref.dtype)
        dg_ref[...] += jnp.sum(dxv * y_ref[...], axis=0, keepdims=True)

    return pl.pallas_call(
        body, name=name, out_shape=[jax.ShapeDtypeStruct((t, d), MXU_DTYPE), jax.ShapeDtypeStruct((1, d), F32)],
        grid=(t // bm,), in_specs=[_row_spec(bm, d), _row_spec(bm, d), _vec_spec(d)],
        out_specs=[_row_spec(bm, d), _vec_spec(d)], compiler_params=_params(1))(dx, y, g)


def _glu_resid_fwd(name, z, x, g):
    _, t, n = z.shape
    d = x.shape[1]
    half = N_DEV // 2
    bm = _tile(t, 256, SUBLANE)

    def body(v_ref, gt_ref, x_ref, g_ref, o_ref):
        o_ref[...] = x_ref[...] + g_ref[...] * (v_ref[...] * jax.nn.sigmoid(gt_ref[...]))

    return pl.pallas_call(
        body, name=name, out_shape=jax.ShapeDtypeStruct((t, d), F32), grid=(half, t // bm),
        in_specs=[pl.BlockSpec((None, bm, n), lambda q, i: (q, i, 0)),
                  pl.BlockSpec((None, bm, n), lambda q, i: (q + half, i, 0)),
                  pl.BlockSpec((bm, n), lambda q, i: (i, q)), pl.BlockSpec((1, n), lambda q, i: (0, q))],
        out_specs=pl.BlockSpec((bm, n), lambda q, i: (i, q)), compiler_params=_params(2))(z, z, x, g)


def _glu_resid_bwd(name, z, dx, g):
    _, t, n = z.shape
    d = dx.shape[1]
    half = N_DEV // 2
    bm = _tile(t, 256, SUBLANE)

    def body(z_ref, dx_ref, g_ref, dz_ref, dg_ref):
        @pl.when(pl.program_id(1) == 0)
        def _():
            dg_ref[...] = jnp.zeros_like(dg_ref)

        v, dxv = z_ref[0], dx_ref[...]
        sig = jax.nn.sigmoid(z_ref[1])
        dout = g_ref[...] * dxv
        dg_ref[...] += jnp.sum(dxv * (v * sig), axis=0, keepdims=True)
        dz_ref[0] = (dout * sig).astype(dz_ref.dtype)
        dz_ref[1] = (dout * v * (sig * (1.0 - sig))).astype(dz_ref.dtype)

    pair = pl.BlockSpec((2, None, bm, n), lambda q, i: (0, q, i, 0))
    dz, dg = pl.pallas_call(
        body, name=name,
        out_shape=[jax.ShapeDtypeStruct((2, half, t, n), MXU_DTYPE), jax.ShapeDtypeStruct((1, d), F32)],
        grid=(half, t // bm),
        in_specs=[pair, pl.BlockSpec((bm, n), lambda q, i: (i, q)), pl.BlockSpec((1, n), lambda q, i: (0, q))],
        out_specs=[pair, pl.BlockSpec((1, n), lambda q, i: (0, q))],
        compiler_params=_params(2))(z.reshape(2, half, t, n), dx, g)
    return dz.reshape(N_DEV, t, n), dg


def _swiglu_act_fwd(name, gu):
    _, t, n = gu.shape
    half = N_DEV // 2
    bm = _tile(t, 256, SUBLANE)

    def body(g_ref, u_ref, o_ref):
        gv = g_ref[...]
        o_ref[...] = (gv * jax.nn.sigmoid(gv) * u_ref[...]).astype(o_ref.dtype)

    return pl.pallas_call(
        body, name=name, out_shape=jax.ShapeDtypeStruct((half, t, n), MXU_DTYPE), grid=(half, t // bm),
        in_specs=[pl.BlockSpec((None, bm, n), lambda q, i: (q, i, 0)),
                  pl.BlockSpec((None, bm, n), lambda q, i: (q + half, i, 0))],
        out_specs=pl.BlockSpec((None, bm, n), lambda q, i: (q, i, 0)), compiler_params=_params(2))(gu, gu)


def _swiglu_act_bwd(name, gu, dact, rider=None):
    _, t, n = gu.shape
    half = N_DEV // 2
    bm = _tile(t, 256, SUBLANE)

    def body(gu_ref, da_ref, o_ref):
        gv, da = gu_ref[0], da_ref[...]
        sig = jax.nn.sigmoid(gv)
        o_ref[0] = (da * gu_ref[1] * (sig * (1.0 + gv * (1.0 - sig)))).astype(o_ref.dtype)
        o_ref[1] = (da * (gv * sig)).astype(o_ref.dtype)

    pair = pl.BlockSpec((2, None, bm, n), lambda q, i: (0, q, i, 0))
    res = _call(
        body, name=name, out_shape=jax.ShapeDtypeStruct((2, half, t, n), MXU_DTYPE), grid=(half, t // bm),
        in_specs=[pair, pl.BlockSpec((None, bm, n), lambda q, i: (q, i, 0))], out_specs=pair,
        args=(gu.reshape(2, half, t, n), dact), rider=rider)
    if rider is None:
        return res.reshape(N_DEV, t, n)
    return res[0].reshape(N_DEV, t, n), res[1]


def _ada_fwd(name, c16, w_ada, b_loc):
    nl, d, n = w_ada.shape
    bn = _tile(n, 512)

    def body(c_ref, w_ref, b_ref, o_ref):
        cv = c_ref[...]
        o_ref[...] = _dot(cv * jax.nn.sigmoid(cv), w_ref[...], NN) + b_ref[...]

    return pl.pallas_call(
        body, name=name, out_shape=jax.ShapeDtypeStruct((nl, c16.shape[0], n), F32), grid=(nl, n // bn),
        in_specs=[pl.BlockSpec(c16.shape, lambda i, j: (0, 0)), pl.BlockSpec((None, d, bn), lambda i, j: (i, 0, j)),
                  pl.BlockSpec((None, 1, bn), lambda i, j: (i, 0, j))],
        out_specs=pl.BlockSpec((None, c16.shape[0], bn), lambda i, j: (i, 0, j)),
        compiler_params=_params(2))(c16, w_ada, b_loc)


def _adam_update(g, w, m, v):
    m = ADAM_B1 * m + (1.0 - ADAM_B1) * g
    v = ADAM_B2 * v + (1.0 - ADAM_B2) * (g * g)
    m_hat = m / (1.0 - ADAM_B1 ** ADAM_STEP)
    v_hat = v / (1.0 - ADAM_B2 ** ADAM_STEP)
    delta = -ADAM_LR * (m_hat / (jnp.sqrt(v_hat) + ADAM_EPS) + ADAM_WD * w)
    return delta, m, v


def _adamw_w_ada(name, c16, dmod16, w, m, v, rider=None):
    nl, d, n = w.shape
    br = _tile(d, 256)

    def body(c_ref, dm_ref, w_ref, m_ref, v_ref, g_ref, dl_ref, mo_ref, vo_ref):
        cv = c_ref[...]
        g = _dot(cv * jax.nn.sigmoid(cv), dm_ref[...], TN)
        g_ref[...] = g
        dl_ref[...], mo_ref[...], vo_ref[...] = _adam_update(g, w_ref[...], m_ref[...], v_ref[...])

    blk = pl.BlockSpec((None, br, n), lambda i, r: (i, r, 0))
    shp = jax.ShapeDtypeStruct(w.shape, F32)
    return _call(
        body, name=name, out_shape=[shp] * 4, grid=(nl, d // br),
        in_specs=[pl.BlockSpec((c16.shape[0], br), lambda i, r: (0, r)),
                  pl.BlockSpec((None, dmod16.shape[1], n), lambda i, r: (i, 0, 0)), blk, blk, blk],
        out_specs=[blk] * 4, args=(c16, dmod16, w, m, v), rider=rider)


def _adamw_sum(name, parts, w, m, v, rider=None):
    nl = len(parts)
    p, r, c = parts[0].shape
    br = _tile(r, 128, 2 * SUBLANE)
    nb = r // br

    def body(*refs):
        p_refs, (w_ref, m_ref, v_ref, g_ref, dl_ref, mo_ref, vo_ref) = refs[:nl], refs[nl:]
        layer = pl.program_id(0)
        g = None
        for l, p_ref in enumerate(p_refs):
            gl = p_ref[0].astype(F32)
            for s in range(1, p):
                gl = gl + p_ref[s].astype(F32)
            g = gl if g is None else jnp.where(layer == l, gl, g)
        g_ref[...] = g
        dl_ref[...], mo_ref[...], vo_ref[...] = _adam_update(g, w_ref[...], m_ref[...], v_ref[...])

    blk = pl.BlockSpec((br, c), lambda l, i: (l * nb + i, 0))
    shp = jax.ShapeDtypeStruct((nl * r, c), F32)
    return _call(
        body, name=name, out_shape=[shp] * 4, grid=(nl, nb),
        in_specs=[pl.BlockSpec((p, br, c), lambda l, i: (0, i, 0))] * nl + [blk, blk, blk], out_specs=[blk] * 4,
        args=(*parts, w, m, v), rider=rider)


def _sum_parts(name, parts):
    p, r, c = parts.shape

    def body(p_ref, o_ref):
        g = p_ref[0]
        for s in range(1, p):
            g = g + p_ref[s]
        o_ref[...] = g

    return pl.pallas_call(body, name=name, out_shape=jax.ShapeDtypeStruct((r, c), F32))(parts)


def _s5_disc(name, lam_re, lam_im, log_dt, b_re, b_im):
    def body(lr_ref, li_ref, ld_ref, br_ref, bi_ref, ar_ref, ai_ref, bbr_ref, bbi_ref):
        lr, li = lr_ref[...], li_ref[...]
        dt = jnp.exp(ld_ref[...])
        mag = jnp.exp(lr * dt)
        a_re, a_im = mag * jnp.cos(li * dt), mag * jnp.sin(li * dt)
        nr, ni = a_re - 1.0, a_im
        den = lr * lr + li * li
        f_re, f_im = (nr * lr + ni * li) / den, (ni * lr - nr * li) / den
        br, bi = br_ref[...], bi_ref[...]
        ar_ref[...], ai_ref[...] = a_re, a_im
        bbr_ref[...] = f_re * br - f_im * bi
        bbi_ref[...] = f_re * bi + f_im * br

    s_a, s_b = jax.ShapeDtypeStruct(lam_re.shape, F32), jax.ShapeDtypeStruct(b_re.shape, F32)
    return pl.pallas_call(body, name=name, out_shape=[s_a, s_a, s_b, s_b])(lam_re, lam_im, log_dt, b_re, b_im)


def _s5_disc_bwd(name, lam_re, lam_im, log_dt, b_re, b_im, dab_re, dab_im, dbb_re, dbb_im):
    def body(lr_ref, li_ref, ld_ref, br_ref, bi_ref, dar_ref, dai_ref, dbbr_ref, dbbi_ref,
             dlr_ref, dli_ref, dld_ref, dbr_ref, dbi_ref):
        lr, li = lr_ref[...], li_ref[...]
        dt = jnp.exp(ld_ref[...])
        mag = jnp.exp(lr * dt)
        a_re, a_im = mag * jnp.cos(li * dt), mag * jnp.sin(li * dt)
        nr, ni = a_re - 1.0, a_im
        den = lr * lr + li * li
        f_re, f_im = (nr * lr + ni * li) / den, (ni * lr - nr * li) / den
        br, bi = br_ref[...], bi_ref[...]
        dbbr, dbbi = dbbr_ref[...], dbbi_ref[...]
        dbr_ref[...] = f_re * dbbr + f_im * dbbi
        dbi_ref[...] = f_re * dbbi - f_im * dbbr
        df_re = jnp.sum(dbbr * br + dbbi * bi, axis=1, keepdims=True)
        df_im = jnp.sum(dbbi * br - dbbr * bi, axis=1, keepdims=True)
        dnr = (df_re * lr - df_im * li) / den
        dni = (df_re * li + df_im * lr) / den
        dden = -(df_re * f_re + df_im * f_im) / den
        dlr = (df_re * nr + df_im * ni) / den + 2.0 * lr * dden
        dli = (df_re * ni - df_im * nr) / den + 2.0 * li * dden
        da_re, da_im = dar_ref[...] + dnr, dai_ref[...] + dni
        dmag_mag = da_re * a_re + da_im * a_im
        dth = da_im * a_re - da_re * a_im
        dlr_ref[...] = dlr + dmag_mag * dt
        dli_ref[...] = dli + dth * dt
        ddt = jnp.sum(dmag_mag * lr + dth * li, axis=2, keepdims=True)
        dld_ref[...] = ddt * dt

    s_a, s_b = jax.ShapeDtypeStruct(lam_re.shape, F32), jax.ShapeDtypeStruct(b_re.shape, F32)
    return pl.pallas_call(
        body, name=name, out_shape=[s_a, s_a, jax.ShapeDtypeStruct(log_dt.shape, F32), s_b, s_b],
    )(lam_re, lam_im, log_dt, b_re, b_im, dab_re, dab_im, dbb_re, dbb_im)


def _s5_time_block(t):
    return _tile(t, 128, SUBLANE)


def _s5_scan_fwd(name, u, bb_re, bb_im, ab_re, ab_im, rider=None):
    t, d = u.shape
    nsg, cs, ns = bb_re.shape
    tb = _s5_time_block(t)

    def body(u_ref, bbr_hbm, bbi_hbm, ar_ref, ai_ref, sr_ref, si_ref, srm_ref, sim_ref, bbr, bbi, cr_ref, ci_ref):
        @pl.when(pl.program_id(0) == 0)
        def _():
            pltpu.sync_copy(bbr_hbm, bbr)
            pltpu.sync_copy(bbi_hbm, bbi)
            cr_ref[...] = jnp.zeros_like(cr_ref)
            ci_ref[...] = jnp.zeros_like(ci_ref)

        for sg in range(nsg):
            us = u_ref[:, sg * cs:(sg + 1) * cs]
            sr_ref[:, sg, :] = _dot(us, bbr[sg], NN)
            si_ref[:, sg, :] = _dot(us, bbi[sg], NN)
        ar, ai = ar_ref[...], ai_ref[...]

        def step(i, carry):
            cr, ci = carry
            nr = ar * cr - ai * ci + sr_ref[i]
            ni = ar * ci + ai * cr + si_ref[i]
            sr_ref[i] = nr
            si_ref[i] = ni
            return nr, ni

        cr, ci = lax.fori_loop(0, tb, step, (cr_ref[...], ci_ref[...]), unroll=2)
        cr_ref[...], ci_ref[...] = cr, ci
        srm_ref[...] = jnp.swapaxes(sr_ref[...], 0, 1).astype(MXU_DTYPE)
        sim_ref[...] = jnp.swapaxes(si_ref[...], 0, 1).astype(MXU_DTYPE)

    scan = jax.ShapeDtypeStruct((t, nsg, ns), F32)
    mxu = jax.ShapeDtypeStruct((nsg, t, ns), MXU_DTYPE)
    hbm = pl.BlockSpec(memory_space=pltpu.HBM)
    full = pl.BlockSpec((nsg, ns), lambda i: (0, 0))
    return _call(
        body, name=name, out_shape=[scan, scan, mxu, mxu], grid=(t // tb,),
        in_specs=[_row_spec(tb, d), hbm, hbm, full, full],
        out_specs=[pl.BlockSpec((tb, nsg, ns), lambda i: (i, 0, 0))] * 2 + [pl.BlockSpec((nsg, tb, ns), lambda i: (0, i, 0))] * 2,
        scratch_shapes=[pltpu.VMEM(bb_re.shape, bb_re.dtype), pltpu.VMEM(bb_im.shape, bb_im.dtype),
                        pltpu.VMEM((nsg, ns), F32), pltpu.VMEM((nsg, ns), F32)],
        args=(u, bb_re, bb_im, ab_re, ab_im), rider=rider)


def _s5_out_fwd(name, s_re, s_im, cc_re, cc_im, u, dskip):
    nsg, t, ns = s_re.shape
    d = u.shape[1]
    cs = cc_re.shape[2]
    tb = _tile(t, 512, SUBLANE)

    def body(sr_ref, si_ref, cr_ref, ci_ref, u_ref, d_ref, yp_ref, ya_ref):
        y = _dot(sr_ref[...], cr_ref[...], NN) - _dot(si_ref[...], ci_ref[...], NN) + d_ref[...] * u_ref[...]
        yp_ref[...] = y
        ya_ref[...] = _gelu(y).astype(ya_ref.dtype)

    s_spec = pl.BlockSpec((None, tb, ns), lambda sg, i: (sg, i, 0))
    c_spec = pl.BlockSpec((None, ns, cs), lambda sg, i: (sg, 0, 0))
    col = pl.BlockSpec((tb, cs), lambda sg, i: (i, sg))
    return pl.pallas_call(
        body, name=name, out_shape=[jax.ShapeDtypeStruct((t, d), F32), jax.ShapeDtypeStruct((t, d), MXU_DTYPE)],
        grid=(nsg, t // tb), in_specs=[s_spec, s_spec, c_spec, c_spec, col, pl.BlockSpec((1, cs), lambda sg, i: (0, sg))],
        out_specs=[col, col], compiler_params=_params(2))(s_re, s_im, cc_re, cc_im, u, dskip)


def _gelu_bwd(name, dy, ypre):
    t, d = dy.shape
    bm = _tile(t, 256, SUBLANE)

    def body(dy_ref, yp_ref, o_ref):
        o_ref[...] = (dy_ref[...] * _gelu_and_grad(yp_ref[...])[1]).astype(o_ref.dtype)

    return pl.pallas_call(
        body, name=name, out_shape=jax.ShapeDtypeStruct((t, d), MXU_DTYPE), grid=(t // bm,),
        in_specs=[_row_spec(bm, d), _row_spec(bm, d)], out_specs=_row_spec(bm, d), compiler_params=_params(1))(dy, ypre)


def _s5_scan_bwd(name, dyp, cc_re, cc_im, ab_re, ab_im, s_re, s_im, rider=None):
    t, d = dyp.shape
    nsg, ns, cs = cc_re.shape
    tb = _s5_time_block(t)
    nb = t // tb

    def body(dy_ref, ccr_hbm, cci_hbm, ar_ref, ai_ref, sr_ref, si_ref, lrm_ref, lim_ref, dar_ref, dai_ref,
             ccr, cci, lr_ref, li_ref, cr_ref, ci_ref):
        @pl.when(pl.program_id(0) == 0)
        def _():
            pltpu.sync_copy(ccr_hbm, ccr)
            pltpu.sync_copy(cci_hbm, cci)
            cr_ref[...] = jnp.zeros_like(cr_ref)
            ci_ref[...] = jnp.zeros_like(ci_ref)
            dar_ref[...] = jnp.zeros_like(dar_ref)
            dai_ref[...] = jnp.zeros_like(dai_ref)

        for sg in range(nsg):
            dys = dy_ref[:, sg * cs:(sg + 1) * cs]
            lr_ref[:, sg, :] = _dot(dys, ccr[sg], NT)
            li_ref[:, sg, :] = -_dot(dys, cci[sg], NT)
        ar, ai = ar_ref[...], ai_ref[...]

        def step(i, carry):
            cr, ci, dar, dai = carry
            j = tb - 1 - i
            sr, si = sr_ref[j], si_ref[j]
            dar = dar + (cr * sr + ci * si)
            dai = dai + (ci * sr - cr * si)
            nr = lr_ref[j] + (ar * cr + ai * ci)
            ni = li_ref[j] + (ar * ci - ai * cr)
            lr_ref[j] = nr
            li_ref[j] = ni
            return nr, ni, dar, dai

        cr, ci, dar, dai = lax.fori_loop(0, tb, step, (cr_ref[...], ci_ref[...], dar_ref[...], dai_ref[...]))
        cr_ref[...], ci_ref[...] = cr, ci
        dar_ref[...], dai_ref[...] = dar, dai
        lrm_ref[...] = jnp.swapaxes(lr_ref[...], 0, 1).astype(MXU_DTYPE)
        lim_ref[...] = jnp.swapaxes(li_ref[...], 0, 1).astype(MXU_DTYPE)

    hbm = pl.BlockSpec(memory_space=pltpu.HBM)
    full = pl.BlockSpec((nsg, ns), lambda i: (0, 0))
    mxu = jax.ShapeDtypeStruct((nsg, t, ns), MXU_DTYPE)
    acc = jax.ShapeDtypeStruct((nsg, ns), F32)
    scan_spec = pl.BlockSpec((tb, nsg, ns), lambda i: (nb - 1 - i, 0, 0))
    return _call(
        body, name=name, out_shape=[mxu, mxu, acc, acc], grid=(nb,),
        in_specs=[pl.BlockSpec((tb, d), lambda i: (nb - 1 - i, 0)), hbm, hbm, full, full, scan_spec, scan_spec],
        out_specs=[pl.BlockSpec((nsg, tb, ns), lambda i: (0, nb - 1 - i, 0))] * 2 + [full, full],
        scratch_shapes=[pltpu.VMEM(cc_re.shape, cc_re.dtype), pltpu.VMEM(cc_im.shape, cc_im.dtype),
                        pltpu.VMEM((tb, nsg, ns), F32), pltpu.VMEM((tb, nsg, ns), F32),
                        pltpu.VMEM((nsg, ns), F32), pltpu.VMEM((nsg, ns), F32)],
        args=(dyp, cc_re, cc_im, ab_re, ab_im, s_re, s_im), rider=rider)


def _s5_grads(name, lam_re, lam_im, s_re, s_im, u, dyp, bb_re, bb_im, dskip, rider=None):
    nsg, t, ns = lam_re.shape
    d = u.shape[1]
    cs = bb_re.shape[1]
    tb = _tile(t, 512, SUBLANE)

    def body(lr_ref, li_ref, sr_ref, si_ref, u_ref, dy_ref, bbr_ref, bbi_ref, d_ref,
             du_ref, dbbr_ref, dbbi_ref, dccr_ref, dcci_ref, dd_ref):
        @pl.when(pl.program_id(1) == 0)
        def _():
            for r in (dbbr_ref, dbbi_ref, dccr_ref, dcci_ref, dd_ref):
                r[...] = jnp.zeros_like(r)

        lr, li, uv, dy = lr_ref[...], li_ref[...], u_ref[...], dy_ref[...]
        dyf = dy.astype(F32)
        du_ref[...] = _dot(lr, bbr_ref[...], NT) + _dot(li, bbi_ref[...], NT) + d_ref[...] * dyf
        dbbr_ref[...] += _dot(uv, lr, TN)
        dbbi_ref[...] += _dot(uv, li, TN)
        dccr_ref[...] += _dot(sr_ref[...], dy, TN)
        dcci_ref[...] -= _dot(si_ref[...], dy, TN)
        dd_ref[...] += jnp.sum(dyf * uv, axis=0, keepdims=True)

    s_spec = pl.BlockSpec((None, tb, ns), lambda sg, i: (sg, i, 0))
    col = pl.BlockSpec((tb, cs), lambda sg, i: (i, sg))
    b_spec = pl.BlockSpec((None, cs, ns), lambda sg, i: (sg, 0, 0))
    c_spec = pl.BlockSpec((None, ns, cs), lambda sg, i: (sg, 0, 0))
    vec = pl.BlockSpec((1, cs), lambda sg, i: (0, sg))
    return _call(
        body, name=name,
        out_shape=[jax.ShapeDtypeStruct((t, d), F32), jax.ShapeDtypeStruct(bb_re.shape, F32),
                   jax.ShapeDtypeStruct(bb_re.shape, F32), jax.ShapeDtypeStruct((nsg, ns, cs), F32),
                   jax.ShapeDtypeStruct((nsg, ns, cs), F32), jax.ShapeDtypeStruct((1, d), F32)],
        grid=(nsg, t // tb), in_specs=[s_spec, s_spec, s_spec, s_spec, col, col, b_spec, b_spec, vec],
        out_specs=[col, b_spec, b_spec, c_spec, c_spec, vec],
        args=(lam_re, lam_im, s_re, s_im, u, dyp, bb_re, bb_im, dskip), rider=rider)


def _shift_down(x, k, prev8):
    if k == 0:
        return x
    ext = jnp.concatenate([prev8, x], axis=0)
    return ext[SUBLANE - k:SUBLANE - k + x.shape[0]]


def _shift_up(x, k, next8):
    if k == 0:
        return x
    ext = jnp.concatenate([x, next8], axis=0)
    return ext[k:k + x.shape[0]]


def _lru_time_block(t):
    return _tile(t, 256, SUBLANE)


def _lru_gates(xp, prev8, cv_ref, wrg, wig):
    taps = cv_ref.shape[0] - 4
    row = lambda k: cv_ref[k:k + 1, :]
    xs = [_shift_down(xp, taps - 1 - k, prev8) for k in range(taps)]
    xb = row(taps)
    for k in range(taps):
        xb = xb + row(k) * xs[k]
    r = jax.nn.sigmoid(_dot(xb, wrg, NN) + row(taps + 1))
    ig = jax.nn.sigmoid(_dot(xb, wig, NN) + row(taps + 2))
    sp = jax.nn.softplus(-row(taps + 3))
    log_a = -LRU_C * r * sp
    a = jnp.exp(log_a)
    mult = jnp.sqrt(_neg_expm1(2.0 * log_a))
    return xs, xb, r, ig, sp, a, mult


def _lru_fwd(name, zz, cvec, wrg, wig, rider=None):
    _, t, w = zz.shape
    half = N_DEV // 2
    tb = _lru_time_block(t)

    def body(gb_ref, xp_ref, xprev_ref, cv_ref, wrg_ref, wig_ref, hs_ref, y_ref, a_scr, b_scr, carry):
        i = pl.program_id(1)

        @pl.when(i == 0)
        def _():
            carry[...] = jnp.zeros_like(carry)

        prev8 = jnp.where(i > 0, xprev_ref[...], 0.0)
        _, xb, _, ig, _, a, mult = _lru_gates(xp_ref[...], prev8, cv_ref, wrg_ref[...], wig_ref[...])
        a_scr[...] = a
        b_scr[...] = mult * (ig * xb)

        def step(j, h):
            h = a_scr[pl.ds(j, 1), :] * h + b_scr[pl.ds(j, 1), :]
            hs_ref[pl.ds(j, 1), :] = h
            return h

        carry[0:1, :] = lax.fori_loop(0, tb, step, carry[0:1, :], unroll=8)
        y_ref[...] = (hs_ref[...] * _gelu(gb_ref[...])).astype(y_ref.dtype)

    nrow = tb // SUBLANE
    blk = lambda off: pl.BlockSpec((None, tb, w), lambda q, i: (q + off, i, 0))
    return _call(
        body, name=name,
        out_shape=[jax.ShapeDtypeStruct((half, t, w), F32), jax.ShapeDtypeStruct((half, t, w), MXU_DTYPE)],
        grid=(half, t // tb),
        in_specs=[blk(0), blk(half),
                  pl.BlockSpec((None, SUBLANE, w), lambda q, i: (q + half, jnp.maximum(i * nrow - 1, 0), 0)),
                  pl.BlockSpec((None,) + cvec.shape[1:], lambda q, i: (q, 0, 0)),
                  pl.BlockSpec((None, w, w), lambda q, i: (q, 0, 0)), pl.BlockSpec((None, w, w), lambda q, i: (q, 0, 0))],
        out_specs=[blk(0), blk(0)],
        scratch_shapes=[pltpu.VMEM((tb, w), F32), pltpu.VMEM((tb, w), F32), pltpu.VMEM((SUBLANE, w), F32)],
        args=(zz, zz, zz, cvec, wrg, wig), rider=rider)


def _lru_bwd(name, zz, hs, dy, cvec, wrg, wig, rider=None):
    _, t, w = zz.shape
    half = N_DEV // 2
    tb = _lru_time_block(t)
    nb = t // tb
    taps = cvec.shape[1] - 4

    def body(gb_ref, xp_ref, xprev_ref, hs_ref, hprev_ref, dy_ref, cv_ref, wrg_ref, wig_ref,
             dgb_ref, dxp_ref, dcv_ref, dwrg_ref, dwig_ref, a_scr, l_scr, carry, dxb_next):
        i = pl.program_id(1)

        @pl.when(i == 0)
        def _():
            for r_ in (carry, dxb_next, dcv_ref, dwrg_ref, dwig_ref):
                r_[...] = jnp.zeros_like(r_)

        has_prev = i < nb - 1
        row = lambda k: cv_ref[k:k + 1, :]
        prev8 = jnp.where(has_prev, xprev_ref[...], 0.0)
        xs, xb, r, ig, sp, a, mult = _lru_gates(xp_ref[...], prev8, cv_ref, wrg_ref[...], wig_ref[...])
        hs_ = hs_ref[...]
        hs_m1 = _shift_down(hs_, 1, jnp.where(has_prev, hprev_ref[...], 0.0))
        gel, dgel = _gelu_and_grad(gb_ref[...])
        dy_ = dy_ref[...]
        dgb_ref[...] = (dy_ * hs_ * dgel).astype(dgb_ref.dtype)
        a_scr[...] = a
        l_scr[...] = dy_ * gel

        def step(k, c):
            j = tb - 1 - k
            lam = l_scr[pl.ds(j, 1), :] + c
            l_scr[pl.ds(j, 1), :] = lam
            return a_scr[pl.ds(j, 1), :] * lam

        carry[0:1, :] = lax.fori_loop(0, tb, step, carry[0:1, :], unroll=8)
        lam = l_scr[...]
        dmult = lam * (ig * xb)
        dig = lam * (mult * xb)
        dxb = lam * (mult * ig)
        dlog_a = (lam * hs_m1) * a - dmult * (a * a) / mult
        dr = dlog_a * (-LRU_C * sp)
        dsp = jnp.sum(dlog_a * (-LRU_C * r), axis=0, keepdims=True)
        dpr = dr * (r * (1.0 - r))
        dpi = dig * (ig * (1.0 - ig))
        dwrg_ref[...] += _dot(xb, dpr, TN)
        dwig_ref[...] += _dot(xb, dpi, TN)
        dxb = dxb + _dot(dpr, wrg_ref[...], NT) + _dot(dpi, wig_ref[...], NT)
        for k in range(taps):
            dcv_ref[k:k + 1, :] += jnp.sum(dxb * xs[k], axis=0, keepdims=True)
        dcv_ref[taps:taps + 1, :] += jnp.sum(dxb, axis=0, keepdims=True)
        dcv_ref[taps + 1:taps + 2, :] += jnp.sum(dpr, axis=0, keepdims=True)
        dcv_ref[taps + 2:taps + 3, :] += jnp.sum(dpi, axis=0, keepdims=True)
        dcv_ref[taps + 3:taps + 4, :] += dsp * (-jax.nn.sigmoid(-row(taps + 3)))
        nxt8 = dxb_next[...]
        dxp = row(taps - 1) * dxb
        for k in range(taps - 1):
            dxp = dxp + row(k) * _shift_up(dxb, taps - 1 - k, nxt8)
        dxp_ref[...] = dxp.astype(dxp_ref.dtype)
        dxb_next[...] = dxb[0:SUBLANE]

    nrow = tb // SUBLANE
    blk = lambda off: pl.BlockSpec((None, tb, w), lambda q, i: (q + off, nb - 1 - i, 0))
    halo = lambda off: pl.BlockSpec((None, SUBLANE, w), lambda q, i: (q + off, jnp.maximum((nb - 1 - i) * nrow - 1, 0), 0))
    wspec = pl.BlockSpec((None, w, w), lambda q, i: (q, 0, 0))
    cspec = pl.BlockSpec((None,) + cvec.shape[1:], lambda q, i: (q, 0, 0))
    act = jax.ShapeDtypeStruct((half, t, w), MXU_DTYPE)
    return _call(
        body, name=name,
        out_shape=[act, act, jax.ShapeDtypeStruct(cvec.shape, F32), jax.ShapeDtypeStruct(wrg.shape, F32),
                   jax.ShapeDtypeStruct(wig.shape, F32)],
        grid=(half, nb),
        in_specs=[blk(0), blk(half), halo(half), blk(0), halo(0), blk(0), cspec, wspec, wspec],
        out_specs=[blk(0), blk(0), cspec, wspec, wspec],
        scratch_shapes=[pltpu.VMEM((tb, w), F32), pltpu.VMEM((tb, w), F32), pltpu.VMEM((SUBLANE, w), F32),
                        pltpu.VMEM((SUBLANE, w), F32)],
        args=(zz, zz, zz, hs, hs, dy, cvec, wrg, wig), rider=rider)


def _band(blocks, per):
    n, a, b = blocks.shape
    eye = jnp.eye(per, dtype=blocks.dtype)
    x = blocks.reshape(n // per, per, a, b)
    return jnp.einsum('sgab,gh->sgahb', x, eye).reshape(n // per, per * a, per * b)


def _unband(bands, per):
    s, pa, pb = bands.shape
    a, b = pa // per, pb // per
    x = bands.reshape(s, per, a, per, b)
    idx = jnp.arange(per)
    return x[:, idx, :, idx, :].transpose(1, 0, 2, 3).reshape(s * per, a, b)


def _pack(arrays, rows_multiple, lanes=LANE):
    flat = jnp.concatenate([a.reshape(-1).astype(F32) for a in arrays])
    rows = -(-flat.shape[0] // (lanes * rows_multiple)) * rows_multiple
    return jnp.pad(flat, (0, rows * lanes - flat.shape[0])).reshape(rows, lanes)


def _unpack(packed, shapes):
    flat = packed.reshape(-1)
    out, off = [], 0
    for s in shapes:
        n = math.prod(s)
        out.append(flat[off:off + n].reshape(s))
        off += n
    return out


def kernel(x, c, norm_g, w_ada, b_ada, s5_w_in, s5_lam_re, s5_lam_im, s5_log_dt, s5_b_re, s5_b_im, s5_c_re, s5_c_im, s5_d, s5_w_glu, lru_w_in, lru_conv_w, lru_conv_b, lru_w_rg, lru_b_rg, lru_w_ig, lru_b_ig, lru_lam, lru_w_out, ffn_w_gu, ffn_w_down, final_g, loss_target, m_norm_g, m_w_ada, m_b_ada, m_s5_w_in, m_s5_lam_re, m_s5_lam_im, m_s5_log_dt, m_s5_b_re, m_s5_b_im, m_s5_c_re, m_s5_c_im, m_s5_d, m_s5_w_glu, m_lru_w_in, m_lru_conv_w, m_lru_conv_b, m_lru_w_rg, m_lru_b_rg, m_lru_w_ig, m_lru_b_ig, m_lru_lam, m_lru_w_out, m_ffn_w_gu, m_ffn_w_down, m_final_g, v_norm_g, v_w_ada, v_b_ada, v_s5_w_in, v_s5_lam_re, v_s5_lam_im, v_s5_log_dt, v_s5_b_re, v_s5_b_im, v_s5_c_re, v_s5_c_im, v_s5_d, v_s5_w_glu, v_lru_w_in, v_lru_conv_w, v_lru_conv_b, v_lru_w_rg, v_lru_b_rg, v_lru_w_ig, v_lru_b_ig, v_lru_lam, v_lru_w_out, v_ffn_w_gu, v_ffn_w_down, v_final_g):
    wv = dict(zip(WEIGHTS, (norm_g, w_ada, b_ada, s5_w_in, s5_lam_re, s5_lam_im, s5_log_dt, s5_b_re, s5_b_im, s5_c_re, s5_c_im, s5_d, s5_w_glu, lru_w_in, lru_conv_w, lru_conv_b, lru_w_rg, lru_b_rg, lru_w_ig, lru_b_ig, lru_lam, lru_w_out, ffn_w_gu, ffn_w_down, final_g)))
    mv = dict(zip(WEIGHTS, (m_norm_g, m_w_ada, m_b_ada, m_s5_w_in, m_s5_lam_re, m_s5_lam_im, m_s5_log_dt, m_s5_b_re, m_s5_b_im, m_s5_c_re, m_s5_c_im, m_s5_d, m_s5_w_glu, m_lru_w_in, m_lru_conv_w, m_lru_conv_b, m_lru_w_rg, m_lru_b_rg, m_lru_w_ig, m_lru_b_ig, m_lru_lam, m_lru_w_out, m_ffn_w_gu, m_ffn_w_down, m_final_g)))
    vv = dict(zip(WEIGHTS, (v_norm_g, v_w_ada, v_b_ada, v_s5_w_in, v_s5_lam_re, v_s5_lam_im, v_s5_log_dt, v_s5_b_re, v_s5_b_im, v_s5_c_re, v_s5_c_im, v_s5_d, v_s5_w_glu, v_lru_w_in, v_lru_conv_w, v_lru_conv_b, v_lru_w_rg, v_lru_b_rg, v_lru_w_ig, v_lru_b_ig, v_lru_lam, v_lru_w_out, v_ffn_w_gu, v_ffn_w_down, v_final_g)))

    me = 4 * lax.axis_index("x") + 2 * lax.axis_index("y") + lax.axis_index("c")
    x0 = x[0]
    tgt = loss_target[0]
    t, d = x0.shape
    depth = norm_g.shape[0]
    n_mod = w_ada.shape[2] * N_DEV // d
    groups, states = s5_lam_re.shape[1], s5_lam_re.shape[2]
    per_sg = S5_SUPER // S5_GROUP
    nsg = groups // per_sg
    lw = lru_lam.shape[1] * N_DEV
    lwc = lw // (N_DEV // 2)
    half = N_DEV // 2

    assert depth == 2, "the ride schedule below is written for one S5 layer followed by one RG-LRU layer"
    wire = lambda a: a.astype(WIRE_DTYPE)
    gw = {'s5_in': _all_gather("ag_s5_w_in", wire(s5_w_in[0]))}

    def riding(job, fn, *args):
        res, (got,) = fn(*args, rider=_gather_rider([wire(job[1])]))
        gw[job[0]] = got
        return res

    sh_shapes = [wv[n].shape for n in SMALL_SHARDED] + [c.shape]
    sh_all = _all_gather("ag_small", _pack([wv[n] for n in SMALL_SHARDED] + [c], SUBLANE))
    sh_parts = [jnp.stack(p) for p in zip(*[_unpack(sh_all[s], sh_shapes) for s in range(N_DEV)])]
    full = {}
    for n, p in zip(SMALL_SHARDED, sh_parts[:-1]):
        full[n] = jnp.moveaxis(p, 0, -2).reshape(p.shape[1:-1] + (-1,))
    c_all = sh_parts[-1].reshape(N_DEV, d)
    c16 = jnp.pad(c_all, ((0, 2 * SUBLANE - N_DEV), (0, 0)))

    n_loc = w_ada.shape[2]
    b_loc = lax.dynamic_slice_in_dim(b_ada, me * n_loc, n_loc, axis=1)[:, None, :]
    mod_part = _ada_fwd("ada_fwd", c16, w_ada, b_loc)[:, :N_DEV]
    mod_mine = _chunk_exchange("x_mod", [mod_part.transpose(1, 0, 2)], ALL)
    mod = mod_mine.transpose(1, 0, 2).reshape(depth, n_mod, 1, d)

    lam3 = lambda a: a[0][:, None, :]
    p_lr, p_li, p_ld = lam3(s5_lam_re), lam3(s5_lam_im), s5_log_dt[0][:, None, None]
    p_br, p_bi = s5_b_re[0].transpose(0, 2, 1), s5_b_im[0].transpose(0, 2, 1)
    ab_re3, ab_im3, bb_re3, bb_im3 = _s5_disc("s5_disc", p_lr, p_li, p_ld, p_br, p_bi)
    ab_re, ab_im = ab_re3.reshape(nsg, per_sg * states), ab_im3.reshape(nsg, per_sg * states)
    bb_re, bb_im = _band(wire(bb_re3), per_sg), _band(wire(bb_im3), per_sg)
    cc_re = _band(wire(s5_c_re[0].transpose(0, 2, 1)), per_sg)
    cc_im = _band(wire(s5_c_im[0].transpose(0, 2, 1)), per_sg)

    taps = lru_conv_w.shape[1]
    cvec = jnp.concatenate([full['lru_conv_w'].reshape(taps, lw), full['lru_conv_b'], full['lru_b_rg'],
                            full['lru_b_ig'], full['lru_lam']], axis=0)
    cvec = cvec.reshape(taps + 4, half, lwc).transpose(1, 0, 2)
    wrg = _band(wire(lru_w_rg[0]), LRU_BLOCKS_PER_CHUNK)
    wig = _band(wire(lru_w_ig[0]), LRU_BLOCKS_PER_CHUNK)

    saved = []
    xc = x0
    for i in range(depth):
        sh1, sc1, g1, sh2, sc2, g2 = [mod[i, k] for k in range(n_mod)]
        gn = full['norm_g'][i]
        h1 = _norm_mod_fwd(f"norm1_fwd{i}", xc, gn[0:1], sc1, sh1)
        if i % 2 == 0:
            u = riding(('s5_glu', s5_w_glu[0]), _mm_row, f"s5_in{i}", h1[None], gw['s5_in'].reshape(d, d))
            s_re, s_im, s_rem, s_imm = riding((('gu', i), ffn_w_gu[i]), _s5_scan_fwd, f"s5_scan{i}", u, bb_re, bb_im,
                                              ab_re, ab_im)
            ypre, yact = _s5_out_fwd(f"s5_out{i}", s_rem, s_imm, cc_re, cc_im, u, s5_d)
            z = riding((('down', i), ffn_w_down[i]), _mm_col, f"s5_glu{i}", yact, gw['s5_glu'])
            x1 = _glu_resid_fwd(f"s5_resid{i}", z, xc, g1)
            mix = (u, s_re, s_im, s_rem, s_imm, ypre, yact, z)
        else:
            zz = _mm_col(f"lru_in{i}", h1, gw['lru_in'])
            hs, ylru = riding((('gu', i), ffn_w_gu[i]), _lru_fwd, f"lru_core{i}", zz, cvec, wrg, wig)
            o = _mm_row(f"lru_out{i}", ylru, gw['lru_out'].reshape(lw, d))
            x1 = _resid(f"lru_resid{i}", xc, o, g1)
            mix = (zz, hs, ylru, o)
        h2 = _norm_mod_fwd(f"norm2_fwd{i}", x1, gn[1:2], sc2, sh2)
        if i % 2 == 0:
            gu = riding(('lru_in', lru_w_in[0]), _mm_col, f"ffn_gu{i}", h2, gw['gu', i])
            act = _swiglu_act_fwd(f"ffn_act{i}", gu)
            f = riding(('lru_out', lru_w_out[0]), _mm_row, f"ffn_down{i}", act, gw['down', i].reshape(-1, d))
        else:
            gu = riding((('down', i), ffn_w_down[i]), _mm_col, f"ffn_gu{i}", h2, gw['gu', i])
            act = _swiglu_act_fwd(f"ffn_act{i}", gu)
            f = _mm_row(f"ffn_down{i}", act, gw['down', i].reshape(-1, d))
        x2 = _resid(f"ffn_resid{i}", x1, f, g2)
        saved.append((xc, h1, mix, x1, h2, gu, act, f))
        xc = x2

    dx, loss_part, d_final_g = _loss_bwd("loss", xc, tgt, final_g[None])
    loss = lax.psum(loss_part[0, 0], ("x", "y", "c"))

    grads = {}
    parts = {}
    dmod = [None] * depth
    d_norm_g = [None] * depth

    def chip_sums(name, partial):
        return _chip_sums(name, partial.reshape(N_DEV, -1, partial.shape[-1]))

    def riding_x(key, sums, fn, *args):
        res, parts[key] = fn(*args, rider=_chunk_rider([sums], SAME_CORE))
        return res

    waiting = None
    for i in reversed(range(depth)):
        xin, h1, mix, x1, h2, gu, act, f = saved[i]
        sh1, sc1, g1, sh2, sc2, g2 = [mod[i, k] for k in range(n_mod)]
        gn = full['norm_g'][i]
        g_down = gw['down', i].reshape(-1, d)
        df, dg2 = _gate_bwd(f"ffn_gate_bwd{i}", dx, f, g2)
        dact = _mm_row_da(f"ffn_down_da{i}", df, g_down, half)
        s_down = chip_sums(f"x_ffn_w_down{i}", _mm_row_db(f"ffn_down_db{i}", act, df, WIRE_DTYPE))
        if waiting is None:
            dgu = riding_x(('ffn_w_down', i), s_down, _swiglu_act_bwd, f"ffn_act_bwd{i}", gu, dact)
            dh2 = _mm_col_da(f"ffn_gu_da{i}", dgu, gw['gu', i])
        else:
            dgu = riding_x(*waiting, _swiglu_act_bwd, f"ffn_act_bwd{i}", gu, dact)
            dh2 = riding_x(('ffn_w_down', i), s_down, _mm_col_da, f"ffn_gu_da{i}", dgu, gw['gu', i])
        s_gu = chip_sums(f"x_ffn_w_gu{i}", _mm_col_db(f"ffn_gu_db{i}", h2, dgu, WIRE_DTYPE))
        dx, dgn2, dsc2, dsh2 = _norm_mod_bwd(f"norm2_bwd{i}", x1, dh2, dx, gn[1:2], sc2)
        if i % 2 == 0:
            u, s_re, s_im, s_rem, s_imm, ypre, yact, z = mix
            dz, dg1 = _glu_resid_bwd(f"s5_resid_bwd{i}", z, dx, g1)
            dyact = _mm_col_da(f"s5_glu_da{i}", dz, gw['s5_glu'])
            s_glu = chip_sums("x_s5_w_glu", _mm_col_db(f"s5_glu_db{i}", yact, dz, WIRE_DTYPE))
            dyp = _gelu_bwd(f"s5_gelu_bwd{i}", dyact, ypre)
            l_rem, l_imm, dab_re, dab_im = riding_x(('ffn_w_gu', i), s_gu, _s5_scan_bwd, f"s5_scan_bwd{i}", dyp, cc_re,
                                                    cc_im, ab_re, ab_im, s_re, s_im)
            du, dbb_re, dbb_im, dcc_re, dcc_im, dd = riding_x(('s5_w_glu', 0), s_glu, _s5_grads, f"s5_grads{i}", l_rem,
                                                              l_imm, s_rem, s_imm, u, dyp, bb_re, bb_im, s5_d)
            dlr, dli, dld, dbr, dbi = _s5_disc_bwd(
                "s5_disc_bwd", p_lr, p_li, p_ld, p_br, p_bi, dab_re.reshape(groups, 1, states),
                dab_im.reshape(groups, 1, states), _unband(dbb_re, per_sg), _unband(dbb_im, per_sg))
            grads['s5_lam_re'], grads['s5_lam_im'], grads['s5_log_dt'] = dlr[:, 0][None], dli[:, 0][None], dld[:, 0, 0][None]
            grads['s5_b_re'], grads['s5_b_im'] = dbr.transpose(0, 2, 1)[None], dbi.transpose(0, 2, 1)[None]
            grads['s5_c_re'] = _unband(dcc_re, per_sg).transpose(0, 2, 1)[None]
            grads['s5_c_im'] = _unband(dcc_im, per_sg).transpose(0, 2, 1)[None]
            grads['s5_d'] = dd
            dub = du.astype(MXU_DTYPE)
            s_s5_in = chip_sums("x_s5_w_in", _mm_row_db(f"s5_in_db{i}", h1[None], dub, WIRE_DTYPE))
            dh1 = riding_x(('s5_w_in', 0), s_s5_in, _mm_row_da, f"s5_in_da{i}", dub, gw['s5_in'].reshape(d, d), 1)[0]
        else:
            zz, hs, ylru, o = mix
            g_lru_out = gw['lru_out'].reshape(lw, d)
            do, dg1 = _gate_bwd(f"lru_gate_bwd{i}", dx, o, g1)
            dyl = _mm_row_da(f"lru_out_da{i}", do, g_lru_out, half)
            s_lru_out = chip_sums("x_lru_w_out", _mm_row_db(f"lru_out_db{i}", ylru, do, WIRE_DTYPE))
            dgb, dxp, dcv, dwrg, dwig = riding_x(('ffn_w_gu', i), s_gu, _lru_bwd, f"lru_core_bwd{i}", zz, hs, dyl, cvec,
                                                 wrg, wig)
            dzz = jnp.concatenate([dgb, dxp], axis=0)
            dh1 = riding_x(('lru_w_out', 0), s_lru_out, _mm_col_da, f"lru_in_da{i}", dzz, gw['lru_in'])
            waiting = (('lru_w_in', 0), chip_sums("x_lru_w_in", _mm_col_db(f"lru_in_db{i}", h1, dzz, WIRE_DTYPE)))
            dcv = dcv.transpose(1, 0, 2).reshape(taps + 4, lw)
            grads['lru_conv_w'] = dcv[:taps].reshape(1, taps, 1, lw)
            grads['lru_conv_b'], grads['lru_b_rg'] = dcv[taps:taps + 1], dcv[taps + 1:taps + 2]
            grads['lru_b_ig'], grads['lru_lam'] = dcv[taps + 2:taps + 3], dcv[taps + 3:taps + 4]
            grads['lru_w_rg'] = _unband(dwrg, LRU_BLOCKS_PER_CHUNK)[None]
            grads['lru_w_ig'] = _unband(dwig, LRU_BLOCKS_PER_CHUNK)[None]
        dx, dgn1, dsc1, dsh1 = _norm_mod_bwd(f"norm1_bwd{i}", xin, dh1, dx, gn[0:1], sc1)
        dmod[i] = jnp.concatenate([dsh1, dsc1, dg1, dsh2, dsc2, dg2], axis=1)
        d_norm_g[i] = jnp.concatenate([dgn1, dgn2], axis=0)
    grad_x = dx[None]
    dmod = jnp.concatenate(dmod, axis=0)
    grads['norm_g'] = jnp.stack(d_norm_g)
    grads['b_ada'] = dmod
    grads['final_g'] = d_final_g[0]

    small_partial = _pack([grads[n] for n in SMALL], SUBLANE * N_DEV)
    rows8 = small_partial.shape[0] // N_DEV
    s_small = _chip_sums("x_small", small_partial.reshape(N_DEV, rows8, LANE))

    out = {}
    dmod_all = _all_gather("ag_dmod", dmod)
    dmod_loc = lax.dynamic_slice_in_dim(dmod_all, me * n_loc, n_loc, axis=2).transpose(1, 0, 2)
    dmod16 = jnp.pad(dmod_loc, ((0, 0), (0, 2 * SUBLANE - N_DEV), (0, 0)))
    out['w_ada'] = _adamw_w_ada("adamw_w_ada", c16, dmod16, w_ada, m_w_ada, v_w_ada)

    for name in BIG[1:]:
        w = wv[name]
        rows, cols = w.shape[-2] * w.shape[0], w.shape[-1]
        flat = lambda a: a.reshape(rows, cols)
        res = _adamw_sum("adamw_" + name, [parts[name, l] for l in range(w.shape[0])], flat(w), flat(mv[name]),
                         flat(vv[name]))
        out[name] = [r.reshape(w.shape) for r in res]

    summed = _sum_parts("sum_small", _chunk_exchange("x_small_ici", [s_small], SAME_CORE))
    small_total = _all_gather("ag_small_sum", summed).reshape(-1, LANE)
    small_grad = dict(zip(SMALL, _unpack(small_total, [grads[n].shape for n in SMALL])))
    for n in SMALL_SHARDED:
        shard = wv[n].shape[-1]
        small_grad[n] = lax.dynamic_slice_in_dim(small_grad[n], me * shard, shard, axis=small_grad[n].ndim - 1)
    small_shapes = [wv[n].shape for n in SMALL]
    pk = lambda dct: _pack([dct[n] for n in SMALL], 2 * SUBLANE, 8 * LANE)
    s_out = _adamw_sum("adamw_small", [pk(small_grad)[None]], pk(wv), pk(mv), pk(vv))
    out.update({n: r for n, *r in zip(SMALL, *[_unpack(o, small_shapes) for o in s_out])})

    return (loss, grad_x, *[out[n][0] for n in WEIGHTS], *[out[n][1] for n in WEIGHTS],
            *[out[n][2] for n in WEIGHTS], *[out[n][3] for n in WEIGHTS])
```

```python
import functools
import math

import jax
import jax.numpy as jnp
from jax import lax
from jax.experimental import pallas as pl
from jax.experimental.pallas import tpu as pltpu

F32 = jnp.float32
MXU_DTYPE = jnp.bfloat16
WIRE_DTYPE = jnp.bfloat16
N_DEV = 8
EPS = 1e-6
LRU_C = 8.0
S5_GROUP = 16
S5_STATE = 64
S5_SUPER = 256
LRU_BLOCKS_PER_CHUNK = 4
ADAM_LR, ADAM_B1, ADAM_B2, ADAM_EPS, ADAM_WD, ADAM_STEP = 0.001, 0.9, 0.999, 1e-08, 0.01, 10
VMEM_LIMIT_BYTES = 56 * 1024 * 1024
LANE = 128
SUBLANE = 8

WEIGHTS = ['norm_g', 'w_ada', 'b_ada', 's5_w_in', 's5_lam_re', 's5_lam_im', 's5_log_dt', 's5_b_re', 's5_b_im',
           's5_c_re', 's5_c_im', 's5_d', 's5_w_glu', 'lru_w_in', 'lru_conv_w', 'lru_conv_b', 'lru_w_rg', 'lru_b_rg',
           'lru_w_ig', 'lru_b_ig', 'lru_lam', 'lru_w_out', 'ffn_w_gu', 'ffn_w_down', 'final_g']
BIG = ('w_ada', 's5_w_in', 's5_w_glu', 'lru_w_in', 'lru_w_out', 'ffn_w_gu', 'ffn_w_down')
SMALL = tuple(n for n in WEIGHTS if n not in BIG)
SMALL_SHARDED = ('norm_g', 'lru_conv_w', 'lru_conv_b', 'lru_b_rg', 'lru_b_ig', 'lru_lam')

NN = (((1,), (0,)), ((), ()))
NT = (((1,), (1,)), ((), ()))
TN = (((0,), (0,)), ((), ()))


def _params(n_grid):
    return pltpu.CompilerParams(dimension_semantics=("arbitrary",) * n_grid, vmem_limit_bytes=VMEM_LIMIT_BYTES)


def _tile(dim, pref, align=LANE):
    if dim <= pref:
        return dim
    t = (pref // align) * align
    while t >= align:
        if dim % t == 0:
            return t
        t -= align
    return dim


def _dot(a, b, dims):
    return lax.dot_general(a.astype(MXU_DTYPE), b.astype(MXU_DTYPE), dims, preferred_element_type=F32)


def _gelu(x):
    k = math.sqrt(2.0 / math.pi)
    return 0.5 * x * (1.0 + jnp.tanh(k * (x + 0.044715 * (x * x * x))))


def _gelu_and_grad(x):
    k = math.sqrt(2.0 / math.pi)
    th = jnp.tanh(k * (x + 0.044715 * (x * x * x)))
    g = 0.5 * x * (1.0 + th)
    dg = 0.5 * (1.0 + th) + 0.5 * x * (1.0 - th * th) * (k * (1.0 + 3.0 * 0.044715 * (x * x)))
    return g, dg


def _neg_expm1(x):
    series = -x * (1.0 + x * (0.5 + x * (1.0 / 6.0 + x * (1.0 / 24.0 + x * (1.0 / 120.0)))))
    return jnp.where(x > -0.01, series, 1.0 - jnp.exp(x))


MESH = pl.DeviceIdType.MESH
N_CHIP = N_DEV // 2
ALL, SAME_CORE = 7, 6


def _place():
    x, y, c = lax.axis_index("x"), lax.axis_index("y"), lax.axis_index("c")
    return x, y, c


def _flip(place, k):
    x, y, c = place
    return (1 - x if (k >> 2) & 1 else x, 1 - y if (k >> 1) & 1 else y, 1 - c if k & 1 else c)


def _chunk_exchange(name, xs, group):
    return _ride_alone(name, _chunk_rider(xs, group))


class _Rider:
    def __init__(self, arrays, out_shape, scratch, start, finish, post):
        self.arrays, self.out_shape, self.scratch = list(arrays), list(out_shape), list(scratch)
        self.start, self.finish, self.post = start, finish, post


def _chunk_rider(xs, group):
    n = len(xs)
    members, r, c_ = xs[0].shape
    assert members == {ALL: N_DEV, SAME_CORE: N_CHIP}[group]
    assert all(a.shape == xs[0].shape and a.dtype == xs[0].dtype for a in xs)
    ks = [k for k in range(1, N_DEV) if not k & ~group]
    member = (lambda p: 4 * p[0] + 2 * p[1] + p[2]) if group == ALL else (lambda p: 2 * p[0] + p[1])

    def copies(ins, outs, scratch):
        out = outs[0]
        send_sems, recv_sems, local_sems = scratch
        place = _place()
        me = member(place)
        local = [pltpu.make_async_copy(ins[l].at[me], out.at[me, l], local_sems.at[l]) for l in range(n)]
        remote = []
        for l in range(n):
            for k in ks:
                pid = _flip(place, k)
                peer = member(pid)

                def copy(land_at, l=l, k=k, peer=peer, pid=pid):
                    return pltpu.make_async_remote_copy(
                        src_ref=ins[l].at[peer], dst_ref=out.at[land_at, l], send_sem=send_sems.at[l * N_DEV + k],
                        recv_sem=recv_sems.at[l * N_DEV + k], device_id=pid, device_id_type=MESH)

                remote.append((copy, me, peer))
        return local, remote

    def start(ins, outs, scratch):
        local, remote = copies(ins, outs, scratch)
        for cp in local:
            cp.start()
        for copy, me, _ in remote:
            copy(me).start()

    def finish(ins, outs, scratch):
        local, remote = copies(ins, outs, scratch)
        for copy, me, peer in remote:
            copy(me).wait_send()
            copy(peer).wait_recv()
        for cp in local:
            cp.wait()

    return _Rider(
        xs, [jax.ShapeDtypeStruct((members, n, r, c_), xs[0].dtype)],
        [pltpu.SemaphoreType.DMA((n * N_DEV,)), pltpu.SemaphoreType.DMA((n * N_DEV,)), pltpu.SemaphoreType.DMA((n,))],
        start, finish, lambda outs: outs[0].reshape(members, n * r, c_))


def _gather_rider(xs):
    n = len(xs)
    chip_flips = (2, 4, 6)
    per = 1 + 2 * len(chip_flips)

    def plan(ins, outs, scratch):
        send_sems, recv_sems, local_sems = scratch
        place = _place()
        sibling = _flip(place, 1)
        jobs = []
        for l in range(n):
            slot = lambda p, l=l: outs[l].at[2 * p[0] + p[1], p[2]]

            def copy(k, block, to, src=None, l=l, slot=slot):
                return pltpu.make_async_remote_copy(
                    src_ref=slot(block) if src is None else src, dst_ref=slot(block), send_sem=send_sems.at[l * per + k],
                    recv_sem=recv_sems.at[l * per + k], device_id=to, device_id_type=MESH)

            mine = pltpu.make_async_copy(ins[l], slot(place), local_sems.at[l])
            first = [copy(0, place, sibling, src=ins[l])]
            first += [copy(1 + j, place, _flip(place, k), src=ins[l]) for j, k in enumerate(chip_flips)]
            jobs.append((copy, mine, first))
        return place, sibling, jobs

    def start(ins, outs, scratch):
        _, _, jobs = plan(ins, outs, scratch)
        for _, mine, first in jobs:
            mine.start()
            for cp in first:
                cp.start()

    def finish(ins, outs, scratch):
        place, sibling, jobs = plan(ins, outs, scratch)
        passed = []
        for copy, _, _ in jobs:
            for j, k in enumerate(chip_flips):
                copy(1 + j, _flip(place, k), place).wait_recv()
                fwd = copy(4 + j, _flip(place, k), sibling)
                fwd.start()
                passed.append(fwd)
        for copy, mine, first in jobs:
            copy(0, sibling, place).wait_recv()
            for j, k in enumerate(chip_flips):
                copy(4 + j, _flip(sibling, k), place).wait_recv()
            for cp in first:
                cp.wait_send()
            mine.wait()
        for cp in passed:
            cp.wait_send()

    return _Rider(
        xs, [jax.ShapeDtypeStruct((N_CHIP, 2) + x.shape, x.dtype) for x in xs],
        [pltpu.SemaphoreType.DMA((n * per,)), pltpu.SemaphoreType.DMA((n * per,)), pltpu.SemaphoreType.DMA((n,))],
        start, finish, lambda outs: [o.reshape((N_DEV,) + x.shape) for o, x in zip(outs, xs)])


HBM_SPEC = pl.BlockSpec(memory_space=pltpu.HBM)


def _ride_alone(name, rider):
    n_in, n_out = len(rider.arrays), len(rider.out_shape)

    def body(*refs):
        parts = refs[:n_in], refs[n_in:n_in + n_out], refs[n_in + n_out:]
        rider.start(*parts)
        rider.finish(*parts)

    outs = pl.pallas_call(body, name=name, out_shape=rider.out_shape, in_specs=[HBM_SPEC] * n_in,
                          out_specs=[HBM_SPEC] * n_out, scratch_shapes=rider.scratch)(*rider.arrays)
    return rider.post(list(outs))


def _call(body, *, name, grid, in_specs, out_specs, out_shape, scratch_shapes=(), args, rider=None):
    single = not isinstance(out_shape, (list, tuple))
    out_shape = [out_shape] if single else list(out_shape)
    out_specs = [out_specs] if single else list(out_specs)
    scratch_shapes = list(scratch_shapes)
    unwrap = lambda outs: outs[0] if single else list(outs)
    if rider is None:
        outs = pl.pallas_call(body, name=name, grid=grid, in_specs=list(in_specs), out_specs=out_specs, out_shape=out_shape,
                              scratch_shapes=scratch_shapes, compiler_params=_params(len(grid)))(*args)
        return unwrap(outs)
    n_in, n_out, n_scr = len(in_specs), len(out_shape), len(scratch_shapes)
    r_in, r_out = len(rider.arrays), len(rider.out_shape)

    def carried(*refs):
        ins, refs = refs[:n_in], refs[n_in:]
        r_ins, refs = refs[:r_in], refs[r_in:]
        outs, refs = refs[:n_out], refs[n_out:]
        r_outs, refs = refs[:r_out], refs[r_out:]
        scr, r_scr = refs[:n_scr], refs[n_scr:]
        steps = [pl.program_id(ax) for ax in range(len(grid))]

        @pl.when(functools.reduce(jnp.logical_and, [s == 0 for s in steps]))
        def _():
            rider.start(r_ins, r_outs, r_scr)

        body(*ins, *outs, *scr)

        @pl.when(functools.reduce(jnp.logical_and, [s == g - 1 for s, g in zip(steps, grid)]))
        def _():
            rider.finish(r_ins, r_outs, r_scr)

    outs = pl.pallas_call(
        carried, name=name, grid=grid, in_specs=list(in_specs) + [HBM_SPEC] * r_in, out_specs=out_specs + [HBM_SPEC] * r_out,
        out_shape=out_shape + rider.out_shape, scratch_shapes=scratch_shapes + rider.scratch,
        compiler_params=_params(len(grid)))(*args, *rider.arrays)
    return unwrap(outs[:n_out]), rider.post(list(outs[n_out:]))


def _all_gather(name, x):
    return _ride_alone(name, _gather_rider([x]))[0]


def _sibling_rider(x):
    _, r, c_ = x.shape

    def copies(ins, outs, scratch):
        send_sems, recv_sems = scratch
        place = _place()
        return [pltpu.make_async_remote_copy(
            src_ref=ins[0].at[2 * chip + (1 - place[2])], dst_ref=outs[0].at[chip], send_sem=send_sems.at[chip],
            recv_sem=recv_sems.at[chip], device_id=_flip(place, 1), device_id_type=MESH) for chip in range(N_CHIP)]

    def start(ins, outs, scratch):
        for cp in copies(ins, outs, scratch):
            cp.start()

    def finish(ins, outs, scratch):
        for cp in copies(ins, outs, scratch):
            cp.wait()

    return _Rider([x], [jax.ShapeDtypeStruct((N_CHIP, r, c_), x.dtype)],
                  [pltpu.SemaphoreType.DMA((N_CHIP,)), pltpu.SemaphoreType.DMA((N_CHIP,))], start, finish, lambda outs: outs[0])


def _join(riders):
    def cut(seq, counts):
        out, off = [], 0
        for k in counts:
            out.append(seq[off:off + k])
            off += k
        return out

    def parts(ins, outs, scratch):
        return zip(riders, cut(ins, [len(r.arrays) for r in riders]), cut(outs, [len(r.out_shape) for r in riders]),
                   cut(scratch, [len(r.scratch) for r in riders]))

    def start(ins, outs, scratch):
        for r, i, o, s in parts(ins, outs, scratch):
            r.start(i, o, s)

    def finish(ins, outs, scratch):
        for r, i, o, s in parts(ins, outs, scratch):
            r.finish(i, o, s)

    return _Rider(
        [a for r in riders for a in r.arrays], [o for r in riders for o in r.out_shape], [s for r in riders for s in r.scratch],
        start, finish, lambda outs: [r.post(o) for r, o in zip(riders, cut(outs, [len(r.out_shape) for r in riders]))])


def _ride(fn, *args, riders):
    return fn(*args, rider=_join(riders))


def _pair_sum(name, x, got, core):
    _, r, c_ = x.shape
    br = _tile(r, 256, 2 * SUBLANE)

    def body(core_ref, x_ref, g_ref, o_ref):
        o_ref[...] = (x_ref[...].astype(F32) + g_ref[...].astype(F32)).astype(o_ref.dtype)

    return pl.pallas_call(
        body, name=name, out_shape=jax.ShapeDtypeStruct((N_CHIP, r, c_), x.dtype),
        grid_spec=pltpu.PrefetchScalarGridSpec(
            num_scalar_prefetch=1, grid=(N_CHIP, r // br),
            in_specs=[pl.BlockSpec((None, br, c_), lambda ch, i, core_ref: (2 * ch + core_ref[0], i, 0)),
                      pl.BlockSpec((None, br, c_), lambda ch, i, core_ref: (ch, i, 0))],
            out_specs=pl.BlockSpec((None, br, c_), lambda ch, i, core_ref: (ch, i, 0))),
        compiler_params=_params(2))(core, x, got)


def _mm(name, a, b, out_shape, out_dtype, grid, a_spec, b_spec, o_spec, dims, n_red, acc_shape, rider=None):
    red = tuple(range(len(grid) - n_red, len(grid)))
    out_type = jax.ShapeDtypeStruct(out_shape, out_dtype)
    if all(grid[ax] == 1 for ax in red):
        def single(a_ref, b_ref, o_ref):
            o_ref[...] = _dot(a_ref[...], b_ref[...], dims).astype(o_ref.dtype)

        return _call(single, name=name, out_shape=out_type, grid=grid, in_specs=[a_spec, b_spec], out_specs=o_spec,
                     args=(a, b), rider=rider)

    def body(a_ref, b_ref, o_ref, acc_ref):
        first = functools.reduce(jnp.logical_and, [pl.program_id(ax) == 0 for ax in red])
        last = functools.reduce(jnp.logical_and, [pl.program_id(ax) == grid[ax] - 1 for ax in red])

        @pl.when(first)
        def _():
            acc_ref[...] = jnp.zeros_like(acc_ref)

        acc_ref[...] += _dot(a_ref[...], b_ref[...], dims)

        @pl.when(last)
        def _():
            o_ref[...] = acc_ref[...].astype(o_ref.dtype)

    return _call(body, name=name, out_shape=out_type, grid=grid, in_specs=[a_spec, b_spec], out_specs=o_spec,
                 scratch_shapes=[pltpu.VMEM(acc_shape, F32)], args=(a, b), rider=rider)


def _mm_col(name, a, b, out_dtype=F32, rider=None):
    m, k = a.shape
    j, _, n = b.shape
    bm, bk = _tile(m, 1024), _tile(k, 2048)
    return _mm(name, a, b, (j, m, n), out_dtype, (j, m // bm, k // bk),
               pl.BlockSpec((bm, bk), lambda jj, mm, kk: (mm, kk)),
               pl.BlockSpec((None, bk, n), lambda jj, mm, kk: (jj, kk, 0)),
               pl.BlockSpec((None, bm, n), lambda jj, mm, kk: (jj, mm, 0)), NN, 1, (bm, n), rider)


def _mm_col_da(name, do, b, rider=None):
    j, m, n = do.shape
    k = b.shape[1]
    bm, bk = _tile(m, 1024), _tile(k, 1024)
    return _mm(name, do, b, (m, k), F32, (m // bm, k // bk, j),
               pl.BlockSpec((None, bm, n), lambda mm, kk, jj: (jj, mm, 0)),
               pl.BlockSpec((None, bk, n), lambda mm, kk, jj: (jj, kk, 0)),
               pl.BlockSpec((bm, bk), lambda mm, kk, jj: (mm, kk)), NT, 1, (bm, bk), rider)


def _mm_col_db(name, a, do, out_dtype, rider=None):
    m, k = a.shape
    j, _, n = do.shape
    bm, bk = _tile(m, 2048), _tile(k, 512)
    return _mm(name, a, do, (j, k, n), out_dtype, (j, k // bk, m // bm),
               pl.BlockSpec((bm, bk), lambda jj, kk, mm: (mm, kk)),
               pl.BlockSpec((None, bm, n), lambda jj, kk, mm: (jj, mm, 0)),
               pl.BlockSpec((None, bk, n), lambda jj, kk, mm: (jj, kk, 0)), TN, 1, (bk, n), rider)


def _row_bk(kq):
    return kq if (kq % LANE or kq // LANE in (11,)) else _tile(kq, 2048)


def _mm_row(name, a, b, out_dtype=F32, rider=None):
    q, m, kq = a.shape
    n = b.shape[1]
    bm, bn, bk = _tile(m, 1024), _tile(n, 1024), _row_bk(kq)
    nk = kq // bk
    return _mm(name, a, b, (m, n), out_dtype, (m // bm, n // bn, q, nk),
               pl.BlockSpec((None, bm, bk), lambda mm, nn, qq, kk: (qq, mm, kk)),
               pl.BlockSpec((bk, bn), lambda mm, nn, qq, kk: (qq * nk + kk, nn)),
               pl.BlockSpec((bm, bn), lambda mm, nn, qq, kk: (mm, nn)), NN, 2, (bm, bn), rider)


def _mm_row_da(name, do, b, q, rider=None):
    m, n = do.shape
    kq = b.shape[0] // q
    bm, bn = _tile(m, 1024), _tile(n, 2048)
    return _mm(name, do, b, (q, m, kq), F32, (q, m // bm, n // bn),
               pl.BlockSpec((bm, bn), lambda qq, mm, nn: (mm, nn)),
               pl.BlockSpec((kq, bn), lambda qq, mm, nn: (qq, nn)),
               pl.BlockSpec((None, bm, kq), lambda qq, mm, nn: (qq, mm, 0)), NT, 1, (bm, kq), rider)


def _mm_row_db(name, a, do, out_dtype):
    q, m, kq = a.shape
    n = do.shape[1]
    bm, bn = _tile(m, 2048), _tile(n, 512)
    return _mm(name, a, do, (q * kq, n), out_dtype, (q, n // bn, m // bm),
               pl.BlockSpec((None, bm, kq), lambda qq, nn, mm: (qq, mm, 0)),
               pl.BlockSpec((bm, bn), lambda qq, nn, mm: (mm, nn)),
               pl.BlockSpec((kq, bn), lambda qq, nn, mm: (qq, nn)), TN, 1, (kq, bn))


def _row_spec(bm, d):
    return pl.BlockSpec((bm, d), lambda i: (i, 0))


def _vec_spec(d):
    return pl.BlockSpec((1, d), lambda i: (0, 0))


def _norm_mod_fwd(name, x, gain, sc, sh):
    t, d = x.shape
    bm = _tile(t, 256, SUBLANE)

    def body(x_ref, g_ref, sc_ref, sh_ref, h_ref):
        xv = x_ref[...]
        rstd = lax.rsqrt(jnp.mean(xv * xv, axis=-1, keepdims=True) + EPS)
        h_ref[...] = ((xv * rstd) * g_ref[...] * (1.0 + sc_ref[...]) + sh_ref[...]).astype(h_ref.dtype)

    return pl.pallas_call(
        body, name=name, out_shape=jax.ShapeDtypeStruct((t, d), MXU_DTYPE), grid=(t // bm,),
        in_specs=[_row_spec(bm, d), _vec_spec(d), _vec_spec(d), _vec_spec(d)], out_specs=_row_spec(bm, d),
        compiler_params=_params(1))(x, gain, sc, sh)


def _norm_mod_bwd(name, x, dh, dres, gain, sc):
    t, d = x.shape
    bm = _tile(t, 256, SUBLANE)

    def body(x_ref, dh_ref, dres_ref, g_ref, sc_ref, dx_ref, dg_ref, dsc_ref, dsh_ref):
        @pl.when(pl.program_id(0) == 0)
        def _():
            dg_ref[...] = jnp.zeros_like(dg_ref)
            dsc_ref[...] = jnp.zeros_like(dsc_ref)
            dsh_ref[...] = jnp.zeros_like(dsh_ref)

        xv, dh_ = x_ref[...], dh_ref[...]
        rstd = lax.rsqrt(jnp.mean(xv * xv, axis=-1, keepdims=True) + EPS)
        nrm = xv * rstd
        gain_ = g_ref[...]
        dsh_ref[...] += jnp.sum(dh_, axis=0, keepdims=True)
        dsc_ref[...] += jnp.sum(dh_ * (nrm * gain_), axis=0, keepdims=True)
        dhn = dh_ * (1.0 + sc_ref[...])
        dg_ref[...] += jnp.sum(dhn * nrm, axis=0, keepdims=True)
        dn = dhn * gain_
        dx_ref[...] = dres_ref[...] + rstd * (dn - nrm * jnp.mean(dn * nrm, axis=-1, keepdims=True))

    vec = jax.ShapeDtypeStruct((1, d), F32)
    return pl.pallas_call(
        body, name=name, out_shape=[jax.ShapeDtypeStruct((t, d), F32), vec, vec, vec], grid=(t // bm,),
        in_specs=[_row_spec(bm, d), _row_spec(bm, d), _row_spec(bm, d), _vec_spec(d), _vec_spec(d)],
        out_specs=[_row_spec(bm, d), _vec_spec(d), _vec_spec(d), _vec_spec(d)],
        compiler_params=_params(1))(x, dh, dres, gain, sc)


def _loss_bwd(name, x, target, gain):
    t, d = x.shape
    bm = _tile(t, 256, SUBLANE)

    def body(x_ref, t_ref, g_ref, dx_ref, loss_ref, dg_ref):
        @pl.when(pl.program_id(0) == 0)
        def _():
            loss_ref[...] = jnp.zeros_like(loss_ref)
            dg_ref[...] = jnp.zeros_like(dg_ref)

        xv = x_ref[...]
        rstd = lax.rsqrt(jnp.mean(xv * xv, axis=-1, keepdims=True) + EPS)
        nrm = xv * rstd
        gain_ = g_ref[...]
        err = nrm * gain_ - t_ref[...]
        per_tok = jnp.mean(err * err, axis=-1, keepdims=True)
        loss_ref[...] += 0.5 * jnp.sum(per_tok, axis=0, keepdims=True)
        dout = err * (1.0 / d)
        dg_ref[...] += jnp.sum(dout * nrm, axis=0, keepdims=True)
        dn = dout * gain_
        dx_ref[...] = rstd * (dn - nrm * jnp.mean(dn * nrm, axis=-1, keepdims=True))

    return pl.pallas_call(
        body, name=name,
        out_shape=[jax.ShapeDtypeStruct((t, d), F32), jax.ShapeDtypeStruct((1, 1), F32),
                   jax.ShapeDtypeStruct((1, d), F32)],
        grid=(t // bm,), in_specs=[_row_spec(bm, d), _row_spec(bm, d), _vec_spec(d)],
        out_specs=[_row_spec(bm, d), pl.BlockSpec((1, 1), lambda i: (0, 0)), _vec_spec(d)],
        compiler_params=_params(1))(x, target, gain)


def _resid(name, x, y, g):
    t, d = x.shape
    bm = _tile(t, 256, SUBLANE)

    def body(x_ref, y_ref, g_ref, o_ref):
        o_ref[...] = x_ref[...] + g_ref[...] * y_ref[...]

    return pl.pallas_call(
        body, name=name, out_shape=jax.ShapeDtypeStruct((t, d), F32), grid=(t // bm,),
        in_specs=[_row_spec(bm, d), _row_spec(bm, d), _vec_spec(d)], out_specs=_row_spec(bm, d),
        compiler_params=_params(1))(x, y, g)


def _gate_bwd(name, dx, y, g):
    t, d = dx.shape
    bm = _tile(t, 256, SUBLANE)

    def body(dx_ref, y_ref, g_ref, dy_ref, dg_ref):
        @pl.when(pl.program_id(0) == 0)
        def _():
            dg_ref[...] = jnp.zeros_like(dg_ref)

        dxv = dx_ref[...]
        dy_ref[...] = (g_ref[...] * dxv).astype(dy_ref.dtype)
        dg_ref[...] += jnp.sum(dxv * y_ref[...], axis=0, keepdims=True)

    return pl.pallas_call(
        body, name=name, out_shape=[jax.ShapeDtypeStruct((t, d), MXU_DTYPE), jax.ShapeDtypeStruct((1, d), F32)],
        grid=(t // bm,), in_specs=[_row_spec(bm, d), _row_spec(bm, d), _vec_spec(d)],
        out_specs=[_row_spec(bm, d), _vec_spec(d)], compiler_params=_params(1))(dx, y, g)


def _glu_resid_fwd(name, z, x, g):
    _, t, n = z.shape
    d = x.shape[1]
    half = N_DEV // 2
    bm = _tile(t, 256, SUBLANE)

    def body(v_ref, gt_ref, x_ref, g_ref, o_ref):
        o_ref[...] = x_ref[...] + g_ref[...] * (v_ref[...] * jax.nn.sigmoid(gt_ref[...]))

    return pl.pallas_call(
        body, name=name, out_shape=jax.ShapeDtypeStruct((t, d), F32), grid=(half, t // bm),
        in_specs=[pl.BlockSpec((None, bm, n), lambda q, i: (q, i, 0)),
                  pl.BlockSpec((None, bm, n), lambda q, i: (q + half, i, 0)),
                  pl.BlockSpec((bm, n), lambda q, i: (i, q)), pl.BlockSpec((1, n), lambda q, i: (0, q))],
        out_specs=pl.BlockSpec((bm, n), lambda q, i: (i, q)), compiler_params=_params(2))(z, z, x, g)


def _glu_resid_bwd(name, z, dx, g):
    _, t, n = z.shape
    d = dx.shape[1]
    half = N_DEV // 2
    bm = _tile(t, 256, SUBLANE)

    def body(z_ref, dx_ref, g_ref, dz_ref, dg_ref):
        @pl.when(pl.program_id(1) == 0)
        def _():
            dg_ref[...] = jnp.zeros_like(dg_ref)

        v, dxv = z_ref[0], dx_ref[...]
        sig = jax.nn.sigmoid(z_ref[1])
        dout = g_ref[...] * dxv
        dg_ref[...] += jnp.sum(dxv * (v * sig), axis=0, keepdims=True)
        dz_ref[0] = (dout * sig).astype(dz_ref.dtype)
        dz_ref[1] = (dout * v * (sig * (1.0 - sig))).astype(dz_ref.dtype)

    pair = pl.BlockSpec((2, None, bm, n), lambda q, i: (0, q, i, 0))
    dz, dg = pl.pallas_call(
        body, name=name,
        out_shape=[jax.ShapeDtypeStruct((2, half, t, n), MXU_DTYPE), jax.ShapeDtypeStruct((1, d), F32)],
        grid=(half, t // bm),
        in_specs=[pair, pl.BlockSpec((bm, n), lambda q, i: (i, q)), pl.BlockSpec((1, n), lambda q, i: (0, q))],
        out_specs=[pair, pl.BlockSpec((1, n), lambda q, i: (0, q))],
        compiler_params=_params(2))(z.reshape(2, half, t, n), dx, g)
    return dz.reshape(N_DEV, t, n), dg


def _swiglu_act_fwd(name, gu):
    _, t, n = gu.shape
    half = N_DEV // 2
    bm = _tile(t, 256, SUBLANE)

    def body(g_ref, u_ref, o_ref):
        gv = g_ref[...].astype(F32)
        o_ref[...] = (gv * jax.nn.sigmoid(gv) * u_ref[...].astype(F32)).astype(o_ref.dtype)

    return pl.pallas_call(
        body, name=name, out_shape=jax.ShapeDtypeStruct((half, t, n), MXU_DTYPE), grid=(half, t // bm),
        in_specs=[pl.BlockSpec((None, bm, n), lambda q, i: (q, i, 0)),
                  pl.BlockSpec((None, bm, n), lambda q, i: (q + half, i, 0))],
        out_specs=pl.BlockSpec((None, bm, n), lambda q, i: (q, i, 0)), compiler_params=_params(2))(gu, gu)


def _swiglu_act_bwd(name, gu, dact, rider=None):
    _, t, n = gu.shape
    half = N_DEV // 2
    bm = _tile(t, 256, SUBLANE)

    def body(gu_ref, da_ref, o_ref):
        gv, da = gu_ref[0].astype(F32), da_ref[...]
        sig = jax.nn.sigmoid(gv)
        o_ref[0] = (da * gu_ref[1].astype(F32) * (sig * (1.0 + gv * (1.0 - sig)))).astype(o_ref.dtype)
        o_ref[1] = (da * (gv * sig)).astype(o_ref.dtype)

    pair = pl.BlockSpec((2, None, bm, n), lambda q, i: (0, q, i, 0))
    res = _call(
        body, name=name, out_shape=jax.ShapeDtypeStruct((2, half, t, n), MXU_DTYPE), grid=(half, t // bm),
        in_specs=[pair, pl.BlockSpec((None, bm, n), lambda q, i: (q, i, 0))], out_specs=pair,
        args=(gu.reshape(2, half, t, n), dact), rider=rider)
    if rider is None:
        return res.reshape(N_DEV, t, n)
    return res[0].reshape(N_DEV, t, n), res[1]


def _ada_fwd(name, c16, w_ada, b_loc):
    nl, d, n = w_ada.shape
    bn = _tile(n, 512)

    def body(c_ref, w_ref, b_ref, o_ref):
        cv = c_ref[...]
        o_ref[...] = _dot(cv * jax.nn.sigmoid(cv), w_ref[...], NN) + b_ref[...]

    return pl.pallas_call(
        body, name=name, out_shape=jax.ShapeDtypeStruct((nl, c16.shape[0], n), F32), grid=(nl, n // bn),
        in_specs=[pl.BlockSpec(c16.shape, lambda i, j: (0, 0)), pl.BlockSpec((None, d, bn), lambda i, j: (i, 0, j)),
                  pl.BlockSpec((None, 1, bn), lambda i, j: (i, 0, j))],
        out_specs=pl.BlockSpec((None, c16.shape[0], bn), lambda i, j: (i, 0, j)),
        compiler_params=_params(2))(c16, w_ada, b_loc)


def _adam_update(g, w, m, v):
    m = ADAM_B1 * m + (1.0 - ADAM_B1) * g
    v = ADAM_B2 * v + (1.0 - ADAM_B2) * (g * g)
    m_hat = m / (1.0 - ADAM_B1 ** ADAM_STEP)
    v_hat = v / (1.0 - ADAM_B2 ** ADAM_STEP)
    delta = -ADAM_LR * (m_hat / (jnp.sqrt(v_hat) + ADAM_EPS) + ADAM_WD * w)
    return delta, m, v


def _adamw_w_ada(name, c16, dmod16, w, m, v, rider=None):
    nl, d, n = w.shape
    br = _tile(d, 256)

    def body(c_ref, dm_ref, w_ref, m_ref, v_ref, g_ref, dl_ref, mo_ref, vo_ref):
        cv = c_ref[...]
        g = _dot(cv * jax.nn.sigmoid(cv), dm_ref[...], TN)
        g_ref[...] = g
        dl_ref[...], mo_ref[...], vo_ref[...] = _adam_update(g, w_ref[...], m_ref[...], v_ref[...])

    blk = pl.BlockSpec((None, br, n), lambda i, r: (i, r, 0))
    shp = jax.ShapeDtypeStruct(w.shape, F32)
    return _call(
        body, name=name, out_shape=[shp] * 4, grid=(nl, d // br),
        in_specs=[pl.BlockSpec((c16.shape[0], br), lambda i, r: (0, r)),
                  pl.BlockSpec((None, dmod16.shape[1], n), lambda i, r: (i, 0, 0)), blk, blk, blk],
        out_specs=[blk] * 4, args=(c16, dmod16, w, m, v), rider=rider)


def _adamw_sum(name, parts, w, m, v, rider=None):
    nl = len(parts)
    p, r, c = parts[0].shape
    br = _tile(r, 128, 2 * SUBLANE)
    nb = r // br

    def body(*refs):
        p_refs, (w_ref, m_ref, v_ref, g_ref, dl_ref, mo_ref, vo_ref) = refs[:nl], refs[nl:]
        layer = pl.program_id(0)
        g = None
        for l, p_ref in enumerate(p_refs):
            gl = p_ref[0].astype(F32)
            for s in range(1, p):
                gl = gl + p_ref[s].astype(F32)
            g = gl if g is None else jnp.where(layer == l, gl, g)
        g_ref[...] = g
        dl_ref[...], mo_ref[...], vo_ref[...] = _adam_update(g, w_ref[...], m_ref[...], v_ref[...])

    blk = pl.BlockSpec((br, c), lambda l, i: (l * nb + i, 0))
    shp = jax.ShapeDtypeStruct((nl * r, c), F32)
    return _call(
        body, name=name, out_shape=[shp] * 4, grid=(nl, nb),
        in_specs=[pl.BlockSpec((p, br, c), lambda l, i: (0, i, 0))] * nl + [blk, blk, blk], out_specs=[blk] * 4,
        args=(*parts, w, m, v), rider=rider)


def _sum_parts(name, parts):
    p, r, c = parts.shape

    def body(p_ref, o_ref):
        g = p_ref[0]
        for s in range(1, p):
            g = g + p_ref[s]
        o_ref[...] = g

    return pl.pallas_call(body, name=name, out_shape=jax.ShapeDtypeStruct((r, c), F32))(parts)


def _s5_disc(name, lam_re, lam_im, log_dt, b_re, b_im):
    def body(lr_ref, li_ref, ld_ref, br_ref, bi_ref, ar_ref, ai_ref, bbr_ref, bbi_ref):
        lr, li = lr_ref[...], li_ref[...]
        dt = jnp.exp(ld_ref[...])
        mag = jnp.exp(lr * dt)
        a_re, a_im = mag * jnp.cos(li * dt), mag * jnp.sin(li * dt)
        nr, ni = a_re - 1.0, a_im
        den = lr * lr + li * li
        f_re, f_im = (nr * lr + ni * li) / den, (ni * lr - nr * li) / den
        br, bi = br_ref[...], bi_ref[...]
        ar_ref[...], ai_ref[...] = a_re, a_im
        bbr_ref[...] = f_re * br - f_im * bi
        bbi_ref[...] = f_re * bi + f_im * br

    s_a, s_b = jax.ShapeDtypeStruct(lam_re.shape, F32), jax.ShapeDtypeStruct(b_re.shape, F32)
    return pl.pallas_call(body, name=name, out_shape=[s_a, s_a, s_b, s_b])(lam_re, lam_im, log_dt, b_re, b_im)


def _s5_disc_bwd(name, lam_re, lam_im, log_dt, b_re, b_im, dab_re, dab_im, dbb_re, dbb_im):
    def body(lr_ref, li_ref, ld_ref, br_ref, bi_ref, dar_ref, dai_ref, dbbr_ref, dbbi_ref,
             dlr_ref, dli_ref, dld_ref, dbr_ref, dbi_ref):
        lr, li = lr_ref[...], li_ref[...]
        dt = jnp.exp(ld_ref[...])
        mag = jnp.exp(lr * dt)
        a_re, a_im = mag * jnp.cos(li * dt), mag * jnp.sin(li * dt)
        nr, ni = a_re - 1.0, a_im
        den = lr * lr + li * li
        f_re, f_im = (nr * lr + ni * li) / den, (ni * lr - nr * li) / den
        br, bi = br_ref[...], bi_ref[...]
        dbbr, dbbi = dbbr_ref[...], dbbi_ref[...]
        dbr_ref[...] = f_re * dbbr + f_im * dbbi
        dbi_ref[...] = f_re * dbbi - f_im * dbbr
        df_re = jnp.sum(dbbr * br + dbbi * bi, axis=1, keepdims=True)
        df_im = jnp.sum(dbbi * br - dbbr * bi, axis=1, keepdims=True)
        dnr = (df_re * lr - df_im * li) / den
        dni = (df_re * li + df_im * lr) / den
        dden = -(df_re * f_re + df_im * f_im) / den
        dlr = (df_re * nr + df_im * ni) / den + 2.0 * lr * dden
        dli = (df_re * ni - df_im * nr) / den + 2.0 * li * dden
        da_re, da_im = dar_ref[...] + dnr, dai_ref[...] + dni
        dmag_mag = da_re * a_re + da_im * a_im
        dth = da_im * a_re - da_re * a_im
        dlr_ref[...] = dlr + dmag_mag * dt
        dli_ref[...] = dli + dth * dt
        ddt = jnp.sum(dmag_mag * lr + dth * li, axis=2, keepdims=True)
        dld_ref[...] = ddt * dt

    s_a, s_b = jax.ShapeDtypeStruct(lam_re.shape, F32), jax.ShapeDtypeStruct(b_re.shape, F32)
    return pl.pallas_call(
        body, name=name, out_shape=[s_a, s_a, jax.ShapeDtypeStruct(log_dt.shape, F32), s_b, s_b],
    )(lam_re, lam_im, log_dt, b_re, b_im, dab_re, dab_im, dbb_re, dbb_im)


def _s5_time_block(t):
    return _tile(t, 128, SUBLANE)


def _s5_scan_fwd(name, u, bb_re, bb_im, ab_re, ab_im, rider=None):
    t, d = u.shape
    nsg, cs, ns = bb_re.shape
    tb = _s5_time_block(t)

    def body(u_ref, bbr_hbm, bbi_hbm, ar_ref, ai_ref, sr_ref, si_ref, srm_ref, sim_ref, bbr, bbi, cr_ref, ci_ref):
        @pl.when(pl.program_id(0) == 0)
        def _():
            pltpu.sync_copy(bbr_hbm, bbr)
            pltpu.sync_copy(bbi_hbm, bbi)
            cr_ref[...] = jnp.zeros_like(cr_ref)
            ci_ref[...] = jnp.zeros_like(ci_ref)

        for sg in range(nsg):
            us = u_ref[:, sg * cs:(sg + 1) * cs]
            sr_ref[:, sg, :] = _dot(us, bbr[sg], NN)
            si_ref[:, sg, :] = _dot(us, bbi[sg], NN)
        ar, ai = ar_ref[...], ai_ref[...]

        def step(i, carry):
            cr, ci = carry
            nr = ar * cr - ai * ci + sr_ref[i]
            ni = ar * ci + ai * cr + si_ref[i]
            sr_ref[i] = nr
            si_ref[i] = ni
            return nr, ni

        cr, ci = lax.fori_loop(0, tb, step, (cr_ref[...], ci_ref[...]), unroll=2)
        cr_ref[...], ci_ref[...] = cr, ci
        srm_ref[...] = jnp.swapaxes(sr_ref[...], 0, 1).astype(MXU_DTYPE)
        sim_ref[...] = jnp.swapaxes(si_ref[...], 0, 1).astype(MXU_DTYPE)

    scan = jax.ShapeDtypeStruct((t, nsg, ns), F32)
    mxu = jax.ShapeDtypeStruct((nsg, t, ns), MXU_DTYPE)
    hbm = pl.BlockSpec(memory_space=pltpu.HBM)
    full = pl.BlockSpec((nsg, ns), lambda i: (0, 0))
    return _call(
        body, name=name, out_shape=[scan, scan, mxu, mxu], grid=(t // tb,),
        in_specs=[_row_spec(tb, d), hbm, hbm, full, full],
        out_specs=[pl.BlockSpec((tb, nsg, ns), lambda i: (i, 0, 0))] * 2 + [pl.BlockSpec((nsg, tb, ns), lambda i: (0, i, 0))] * 2,
        scratch_shapes=[pltpu.VMEM(bb_re.shape, bb_re.dtype), pltpu.VMEM(bb_im.shape, bb_im.dtype),
                        pltpu.VMEM((nsg, ns), F32), pltpu.VMEM((nsg, ns), F32)],
        args=(u, bb_re, bb_im, ab_re, ab_im), rider=rider)


def _s5_out_fwd(name, s_re, s_im, cc_re, cc_im, u, dskip):
    nsg, t, ns = s_re.shape
    d = u.shape[1]
    cs = cc_re.shape[2]
    tb = _tile(t, 512, SUBLANE)

    def body(sr_ref, si_ref, cr_ref, ci_ref, u_ref, d_ref, yp_ref, ya_ref):
        y = _dot(sr_ref[...], cr_ref[...], NN) - _dot(si_ref[...], ci_ref[...], NN) + d_ref[...] * u_ref[...]
        yp_ref[...] = y
        ya_ref[...] = _gelu(y).astype(ya_ref.dtype)

    s_spec = pl.BlockSpec((None, tb, ns), lambda sg, i: (sg, i, 0))
    c_spec = pl.BlockSpec((None, ns, cs), lambda sg, i: (sg, 0, 0))
    col = pl.BlockSpec((tb, cs), lambda sg, i: (i, sg))
    return pl.pallas_call(
        body, name=name, out_shape=[jax.ShapeDtypeStruct((t, d), F32), jax.ShapeDtypeStruct((t, d), MXU_DTYPE)],
        grid=(nsg, t // tb), in_specs=[s_spec, s_spec, c_spec, c_spec, col, pl.BlockSpec((1, cs), lambda sg, i: (0, sg))],
        out_specs=[col, col], compiler_params=_params(2))(s_re, s_im, cc_re, cc_im, u, dskip)


def _gelu_bwd(name, dy, ypre):
    t, d = dy.shape
    bm = _tile(t, 256, SUBLANE)

    def body(dy_ref, yp_ref, o_ref):
        o_ref[...] = (dy_ref[...] * _gelu_and_grad(yp_ref[...])[1]).astype(o_ref.dtype)

    return pl.pallas_call(
        body, name=name, out_shape=jax.ShapeDtypeStruct((t, d), MXU_DTYPE), grid=(t // bm,),
        in_specs=[_row_spec(bm, d), _row_spec(bm, d)], out_specs=_row_spec(bm, d), compiler_params=_params(1))(dy, ypre)


def _s5_scan_bwd(name, dyp, cc_re, cc_im, ab_re, ab_im, s_re, s_im, rider=None):
    t, d = dyp.shape
    nsg, ns, cs = cc_re.shape
    tb = _s5_time_block(t)
    nb = t // tb

    def body(dy_ref, ccr_hbm, cci_hbm, ar_ref, ai_ref, sr_ref, si_ref, lrm_ref, lim_ref, dar_ref, dai_ref,
             ccr, cci, lr_ref, li_ref, cr_ref, ci_ref):
        @pl.when(pl.program_id(0) == 0)
        def _():
            pltpu.sync_copy(ccr_hbm, ccr)
            pltpu.sync_copy(cci_hbm, cci)
            cr_ref[...] = jnp.zeros_like(cr_ref)
            ci_ref[...] = jnp.zeros_like(ci_ref)
            dar_ref[...] = jnp.zeros_like(dar_ref)
            dai_ref[...] = jnp.zeros_like(dai_ref)

        for sg in range(nsg):
            dys = dy_ref[:, sg * cs:(sg + 1) * cs]
            lr_ref[:, sg, :] = _dot(dys, ccr[sg], NT)
            li_ref[:, sg, :] = -_dot(dys, cci[sg], NT)
        ar, ai = ar_ref[...], ai_ref[...]

        def step(i, carry):
            cr, ci, dar, dai = carry
            j = tb - 1 - i
            sr, si = sr_ref[j], si_ref[j]
            dar = dar + (cr * sr + ci * si)
            dai = dai + (ci * sr - cr * si)
            nr = lr_ref[j] + (ar * cr + ai * ci)
            ni = li_ref[j] + (ar * ci - ai * cr)
            lr_ref[j] = nr
            li_ref[j] = ni
            return nr, ni, dar, dai

        cr, ci, dar, dai = lax.fori_loop(0, tb, step, (cr_ref[...], ci_ref[...], dar_ref[...], dai_ref[...]))
        cr_ref[...], ci_ref[...] = cr, ci
        dar_ref[...], dai_ref[...] = dar, dai
        lrm_ref[...] = jnp.swapaxes(lr_ref[...], 0, 1).astype(MXU_DTYPE)
        lim_ref[...] = jnp.swapaxes(li_ref[...], 0, 1).astype(MXU_DTYPE)

    hbm = pl.BlockSpec(memory_space=pltpu.HBM)
    full = pl.BlockSpec((nsg, ns), lambda i: (0, 0))
    mxu = jax.ShapeDtypeStruct((nsg, t, ns), MXU_DTYPE)
    acc = jax.ShapeDtypeStruct((nsg, ns), F32)
    scan_spec = pl.BlockSpec((tb, nsg, ns), lambda i: (nb - 1 - i, 0, 0))
    return _call(
        body, name=name, out_shape=[mxu, mxu, acc, acc], grid=(nb,),
        in_specs=[pl.BlockSpec((tb, d), lambda i: (nb - 1 - i, 0)), hbm, hbm, full, full, scan_spec, scan_spec],
        out_specs=[pl.BlockSpec((nsg, tb, ns), lambda i: (0, nb - 1 - i, 0))] * 2 + [full, full],
        scratch_shapes=[pltpu.VMEM(cc_re.shape, cc_re.dtype), pltpu.VMEM(cc_im.shape, cc_im.dtype),
                        pltpu.VMEM((tb, nsg, ns), F32), pltpu.VMEM((tb, nsg, ns), F32),
                        pltpu.VMEM((nsg, ns), F32), pltpu.VMEM((nsg, ns), F32)],
        args=(dyp, cc_re, cc_im, ab_re, ab_im, s_re, s_im), rider=rider)


def _s5_grads(name, lam_re, lam_im, s_re, s_im, u, dyp, bb_re, bb_im, dskip, rider=None):
    nsg, t, ns = lam_re.shape
    d = u.shape[1]
    cs = bb_re.shape[1]
    tb = _tile(t, 512, SUBLANE)

    def body(lr_ref, li_ref, sr_ref, si_ref, u_ref, dy_ref, bbr_ref, bbi_ref, d_ref,
             du_ref, dbbr_ref, dbbi_ref, dccr_ref, dcci_ref, dd_ref):
        @pl.when(pl.program_id(1) == 0)
        def _():
            for r in (dbbr_ref, dbbi_ref, dccr_ref, dcci_ref, dd_ref):
                r[...] = jnp.zeros_like(r)

        lr, li, uv, dy = lr_ref[...], li_ref[...], u_ref[...], dy_ref[...]
        dyf = dy.astype(F32)
        du_ref[...] = _dot(lr, bbr_ref[...], NT) + _dot(li, bbi_ref[...], NT) + d_ref[...] * dyf
        dbbr_ref[...] += _dot(uv, lr, TN)
        dbbi_ref[...] += _dot(uv, li, TN)
        dccr_ref[...] += _dot(sr_ref[...], dy, TN)
        dcci_ref[...] -= _dot(si_ref[...], dy, TN)
        dd_ref[...] += jnp.sum(dyf * uv, axis=0, keepdims=True)

    s_spec = pl.BlockSpec((None, tb, ns), lambda sg, i: (sg, i, 0))
    col = pl.BlockSpec((tb, cs), lambda sg, i: (i, sg))
    b_spec = pl.BlockSpec((None, cs, ns), lambda sg, i: (sg, 0, 0))
    c_spec = pl.BlockSpec((None, ns, cs), lambda sg, i: (sg, 0, 0))
    vec = pl.BlockSpec((1, cs), lambda sg, i: (0, sg))
    return _call(
        body, name=name,
        out_shape=[jax.ShapeDtypeStruct((t, d), F32), jax.ShapeDtypeStruct(bb_re.shape, F32),
                   jax.ShapeDtypeStruct(bb_re.shape, F32), jax.ShapeDtypeStruct((nsg, ns, cs), F32),
                   jax.ShapeDtypeStruct((nsg, ns, cs), F32), jax.ShapeDtypeStruct((1, d), F32)],
        grid=(nsg, t // tb), in_specs=[s_spec, s_spec, s_spec, s_spec, col, col, b_spec, b_spec, vec],
        out_specs=[col, b_spec, b_spec, c_spec, c_spec, vec],
        args=(lam_re, lam_im, s_re, s_im, u, dyp, bb_re, bb_im, dskip), rider=rider)


def _shift_down(x, k, prev8):
    if k == 0:
        return x
    ext = jnp.concatenate([prev8, x], axis=0)
    return ext[SUBLANE - k:SUBLANE - k + x.shape[0]]


def _shift_up(x, k, next8):
    if k == 0:
        return x
    ext = jnp.concatenate([x, next8], axis=0)
    return ext[k:k + x.shape[0]]


def _lru_time_block(t):
    return _tile(t, 256, SUBLANE)


def _lru_gates(xp, prev8, cv_ref, wrg, wig):
    taps = cv_ref.shape[0] - 4
    row = lambda k: cv_ref[k:k + 1, :]
    xs = [_shift_down(xp, taps - 1 - k, prev8) for k in range(taps)]
    xb = row(taps)
    for k in range(taps):
        xb = xb + row(k) * xs[k]
    r = jax.nn.sigmoid(_dot(xb, wrg, NN) + row(taps + 1))
    ig = jax.nn.sigmoid(_dot(xb, wig, NN) + row(taps + 2))
    sp = jax.nn.softplus(-row(taps + 3))
    log_a = -LRU_C * r * sp
    a = jnp.exp(log_a)
    mult = jnp.sqrt(_neg_expm1(2.0 * log_a))
    return xs, xb, r, ig, sp, a, mult


def _lru_fwd(name, zz, cvec, wrg, wig, rider=None):
    _, t, w = zz.shape
    half = N_DEV // 2
    tb = _lru_time_block(t)

    def body(gb_ref, xp_ref, xprev_ref, cv_ref, wrg_ref, wig_ref, hs_ref, y_ref, a_scr, b_scr, carry):
        i = pl.program_id(1)

        @pl.when(i == 0)
        def _():
            carry[...] = jnp.zeros_like(carry)

        prev8 = jnp.where(i > 0, xprev_ref[...], 0.0)
        _, xb, _, ig, _, a, mult = _lru_gates(xp_ref[...], prev8, cv_ref, wrg_ref[...], wig_ref[...])
        a_scr[...] = a
        b_scr[...] = mult * (ig * xb)

        def step(j, h):
            h = a_scr[pl.ds(j, 1), :] * h + b_scr[pl.ds(j, 1), :]
            hs_ref[pl.ds(j, 1), :] = h
            return h

        carry[0:1, :] = lax.fori_loop(0, tb, step, carry[0:1, :], unroll=8)
        y_ref[...] = (hs_ref[...] * _gelu(gb_ref[...])).astype(y_ref.dtype)

    nrow = tb // SUBLANE
    blk = lambda off: pl.BlockSpec((None, tb, w), lambda q, i: (q + off, i, 0))
    return _call(
        body, name=name,
        out_shape=[jax.ShapeDtypeStruct((half, t, w), F32), jax.ShapeDtypeStruct((half, t, w), MXU_DTYPE)],
        grid=(half, t // tb),
        in_specs=[blk(0), blk(half),
                  pl.BlockSpec((None, SUBLANE, w), lambda q, i: (q + half, jnp.maximum(i * nrow - 1, 0), 0)),
                  pl.BlockSpec((None,) + cvec.shape[1:], lambda q, i: (q, 0, 0)),
                  pl.BlockSpec((None, w, w), lambda q, i: (q, 0, 0)), pl.BlockSpec((None, w, w), lambda q, i: (q, 0, 0))],
        out_specs=[blk(0), blk(0)],
        scratch_shapes=[pltpu.VMEM((tb, w), F32), pltpu.VMEM((tb, w), F32), pltpu.VMEM((SUBLANE, w), F32)],
        args=(zz, zz, zz, cvec, wrg, wig), rider=rider)


def _lru_bwd(name, zz, hs, dy, cvec, wrg, wig, rider=None):
    _, t, w = zz.shape
    half = N_DEV // 2
    tb = _lru_time_block(t)
    nb = t // tb
    taps = cvec.shape[1] - 4

    def body(gb_ref, xp_ref, xprev_ref, hs_ref, hprev_ref, dy_ref, cv_ref, wrg_ref, wig_ref,
             dgb_ref, dxp_ref, dcv_ref, dwrg_ref, dwig_ref, a_scr, l_scr, carry, dxb_next):
        i = pl.program_id(1)

        @pl.when(i == 0)
        def _():
            for r_ in (carry, dxb_next, dcv_ref, dwrg_ref, dwig_ref):
                r_[...] = jnp.zeros_like(r_)

        has_prev = i < nb - 1
        row = lambda k: cv_ref[k:k + 1, :]
        prev8 = jnp.where(has_prev, xprev_ref[...], 0.0)
        xs, xb, r, ig, sp, a, mult = _lru_gates(xp_ref[...], prev8, cv_ref, wrg_ref[...], wig_ref[...])
        hs_ = hs_ref[...]
        hs_m1 = _shift_down(hs_, 1, jnp.where(has_prev, hprev_ref[...], 0.0))
        gel, dgel = _gelu_and_grad(gb_ref[...])
        dy_ = dy_ref[...]
        dgb_ref[...] = (dy_ * hs_ * dgel).astype(dgb_ref.dtype)
        a_scr[...] = a
        l_scr[...] = dy_ * gel

        def step(k, c):
            j = tb - 1 - k
            lam = l_scr[pl.ds(j, 1), :] + c
            l_scr[pl.ds(j, 1), :] = lam
            return a_scr[pl.ds(j, 1), :] * lam

        carry[0:1, :] = lax.fori_loop(0, tb, step, carry[0:1, :], unroll=8)
        lam = l_scr[...]
        dmult = lam * (ig * xb)
        dig = lam * (mult * xb)
        dxb = lam * (mult * ig)
        dlog_a = (lam * hs_m1) * a - dmult * (a * a) / mult
        dr = dlog_a * (-LRU_C * sp)
        dsp = jnp.sum(dlog_a * (-LRU_C * r), axis=0, keepdims=True)
        dpr = dr * (r * (1.0 - r))
        dpi = dig * (ig * (1.0 - ig))
        dwrg_ref[...] += _dot(xb, dpr, TN)
        dwig_ref[...] += _dot(xb, dpi, TN)
        dxb = dxb + _dot(dpr, wrg_ref[...], NT) + _dot(dpi, wig_ref[...], NT)
        for k in range(taps):
            dcv_ref[k:k + 1, :] += jnp.sum(dxb * xs[k], axis=0, keepdims=True)
        dcv_ref[taps:taps + 1, :] += jnp.sum(dxb, axis=0, keepdims=True)
        dcv_ref[taps + 1:taps + 2, :] += jnp.sum(dpr, axis=0, keepdims=True)
        dcv_ref[taps + 2:taps + 3, :] += jnp.sum(dpi, axis=0, keepdims=True)
        dcv_ref[taps + 3:taps + 4, :] += dsp * (-jax.nn.sigmoid(-row(taps + 3)))
        nxt8 = dxb_next[...]
        dxp = row(taps - 1) * dxb
        for k in range(taps - 1):
            dxp = dxp + row(k) * _shift_up(dxb, taps - 1 - k, nxt8)
        dxp_ref[...] = dxp.astype(dxp_ref.dtype)
        dxb_next[...] = dxb[0:SUBLANE]

    nrow = tb // SUBLANE
    blk = lambda off: pl.BlockSpec((None, tb, w), lambda q, i: (q + off, nb - 1 - i, 0))
    halo = lambda off: pl.BlockSpec((None, SUBLANE, w), lambda q, i: (q + off, jnp.maximum((nb - 1 - i) * nrow - 1, 0), 0))
    wspec = pl.BlockSpec((None, w, w), lambda q, i: (q, 0, 0))
    cspec = pl.BlockSpec((None,) + cvec.shape[1:], lambda q, i: (q, 0, 0))
    act = jax.ShapeDtypeStruct((half, t, w), MXU_DTYPE)
    return _call(
        body, name=name,
        out_shape=[act, act, jax.ShapeDtypeStruct(cvec.shape, F32), jax.ShapeDtypeStruct(wrg.shape, F32),
                   jax.ShapeDtypeStruct(wig.shape, F32)],
        grid=(half, nb),
        in_specs=[blk(0), blk(half), halo(half), blk(0), halo(0), blk(0), cspec, wspec, wspec],
        out_specs=[blk(0), blk(0), cspec, wspec, wspec],
        scratch_shapes=[pltpu.VMEM((tb, w), F32), pltpu.VMEM((tb, w), F32), pltpu.VMEM((SUBLANE, w), F32),
                        pltpu.VMEM((SUBLANE, w), F32)],
        args=(zz, zz, zz, hs, hs, dy, cvec, wrg, wig), rider=rider)


def _band(blocks, per):
    n, a, b = blocks.shape
    eye = jnp.eye(per, dtype=blocks.dtype)
    x = blocks.reshape(n // per, per, a, b)
    return jnp.einsum('sgab,gh->sgahb', x, eye).reshape(n // per, per * a, per * b)


def _unband(bands, per):
    s, pa, pb = bands.shape
    a, b = pa // per, pb // per
    x = bands.reshape(s, per, a, per, b)
    idx = jnp.arange(per)
    return x[:, idx, :, idx, :].transpose(1, 0, 2, 3).reshape(s * per, a, b)


def _pack(arrays, rows_multiple, lanes=LANE):
    flat = jnp.concatenate([a.reshape(-1).astype(F32) for a in arrays])
    rows = -(-flat.shape[0] // (lanes * rows_multiple)) * rows_multiple
    return jnp.pad(flat, (0, rows * lanes - flat.shape[0])).reshape(rows, lanes)


def _unpack(packed, shapes):
    flat = packed.reshape(-1)
    out, off = [], 0
    for s in shapes:
        n = math.prod(s)
        out.append(flat[off:off + n].reshape(s))
        off += n
    return out


def kernel(x, c, norm_g, w_ada, b_ada, s5_w_in, s5_lam_re, s5_lam_im, s5_log_dt, s5_b_re, s5_b_im, s5_c_re, s5_c_im, s5_d, s5_w_glu, lru_w_in, lru_conv_w, lru_conv_b, lru_w_rg, lru_b_rg, lru_w_ig, lru_b_ig, lru_lam, lru_w_out, ffn_w_gu, ffn_w_down, final_g, loss_target, m_norm_g, m_w_ada, m_b_ada, m_s5_w_in, m_s5_lam_re, m_s5_lam_im, m_s5_log_dt, m_s5_b_re, m_s5_b_im, m_s5_c_re, m_s5_c_im, m_s5_d, m_s5_w_glu, m_lru_w_in, m_lru_conv_w, m_lru_conv_b, m_lru_w_rg, m_lru_b_rg, m_lru_w_ig, m_lru_b_ig, m_lru_lam, m_lru_w_out, m_ffn_w_gu, m_ffn_w_down, m_final_g, v_norm_g, v_w_ada, v_b_ada, v_s5_w_in, v_s5_lam_re, v_s5_lam_im, v_s5_log_dt, v_s5_b_re, v_s5_b_im, v_s5_c_re, v_s5_c_im, v_s5_d, v_s5_w_glu, v_lru_w_in, v_lru_conv_w, v_lru_conv_b, v_lru_w_rg, v_lru_b_rg, v_lru_w_ig, v_lru_b_ig, v_lru_lam, v_lru_w_out, v_ffn_w_gu, v_ffn_w_down, v_final_g):
    wv = dict(zip(WEIGHTS, (norm_g, w_ada, b_ada, s5_w_in, s5_lam_re, s5_lam_im, s5_log_dt, s5_b_re, s5_b_im, s5_c_re, s5_c_im, s5_d, s5_w_glu, lru_w_in, lru_conv_w, lru_conv_b, lru_w_rg, lru_b_rg, lru_w_ig, lru_b_ig, lru_lam, lru_w_out, ffn_w_gu, ffn_w_down, final_g)))
    mv = dict(zip(WEIGHTS, (m_norm_g, m_w_ada, m_b_ada, m_s5_w_in, m_s5_lam_re, m_s5_lam_im, m_s5_log_dt, m_s5_b_re, m_s5_b_im, m_s5_c_re, m_s5_c_im, m_s5_d, m_s5_w_glu, m_lru_w_in, m_lru_conv_w, m_lru_conv_b, m_lru_w_rg, m_lru_b_rg, m_lru_w_ig, m_lru_b_ig, m_lru_lam, m_lru_w_out, m_ffn_w_gu, m_ffn_w_down, m_final_g)))
    vv = dict(zip(WEIGHTS, (v_norm_g, v_w_ada, v_b_ada, v_s5_w_in, v_s5_lam_re, v_s5_lam_im, v_s5_log_dt, v_s5_b_re, v_s5_b_im, v_s5_c_re, v_s5_c_im, v_s5_d, v_s5_w_glu, v_lru_w_in, v_lru_conv_w, v_lru_conv_b, v_lru_w_rg, v_lru_b_rg, v_lru_w_ig, v_lru_b_ig, v_lru_lam, v_lru_w_out, v_ffn_w_gu, v_ffn_w_down, v_final_g)))

    me = 4 * lax.axis_index("x") + 2 * lax.axis_index("y") + lax.axis_index("c")
    x0 = x[0]
    tgt = loss_target[0]
    t, d = x0.shape
    depth = norm_g.shape[0]
    n_mod = w_ada.shape[2] * N_DEV // d
    groups, states = s5_lam_re.shape[1], s5_lam_re.shape[2]
    per_sg = S5_SUPER // S5_GROUP
    nsg = groups // per_sg
    lw = lru_lam.shape[1] * N_DEV
    lwc = lw // (N_DEV // 2)
    half = N_DEV // 2

    assert depth == 2, "the ride schedule below is written for one S5 layer followed by one RG-LRU layer"
    wire = lambda a: a.astype(WIRE_DTYPE)
    gw = {'s5_in': _all_gather("ag_s5_w_in", wire(s5_w_in[0]))}

    def riding(job, fn, *args):
        res, (got,) = fn(*args, rider=_gather_rider([wire(job[1])]))
        gw[job[0]] = got
        return res

    sh_shapes = [wv[n].shape for n in SMALL_SHARDED] + [c.shape]
    sh_all = _all_gather("ag_small", _pack([wv[n] for n in SMALL_SHARDED] + [c], SUBLANE))
    sh_parts = [jnp.stack(p) for p in zip(*[_unpack(sh_all[s], sh_shapes) for s in range(N_DEV)])]
    full = {}
    for n, p in zip(SMALL_SHARDED, sh_parts[:-1]):
        full[n] = jnp.moveaxis(p, 0, -2).reshape(p.shape[1:-1] + (-1,))
    c_all = sh_parts[-1].reshape(N_DEV, d)
    c16 = jnp.pad(c_all, ((0, 2 * SUBLANE - N_DEV), (0, 0)))

    n_loc = w_ada.shape[2]
    b_loc = lax.dynamic_slice_in_dim(b_ada, me * n_loc, n_loc, axis=1)[:, None, :]
    mod_part = _ada_fwd("ada_fwd", c16, w_ada, b_loc)[:, :N_DEV]
    mod_mine = _chunk_exchange("x_mod", [mod_part.transpose(1, 0, 2)], ALL)
    mod = mod_mine.transpose(1, 0, 2).reshape(depth, n_mod, 1, d)

    lam3 = lambda a: a[0][:, None, :]
    p_lr, p_li, p_ld = lam3(s5_lam_re), lam3(s5_lam_im), s5_log_dt[0][:, None, None]
    p_br, p_bi = s5_b_re[0].transpose(0, 2, 1), s5_b_im[0].transpose(0, 2, 1)
    ab_re3, ab_im3, bb_re3, bb_im3 = _s5_disc("s5_disc", p_lr, p_li, p_ld, p_br, p_bi)
    ab_re, ab_im = ab_re3.reshape(nsg, per_sg * states), ab_im3.reshape(nsg, per_sg * states)
    bb_re, bb_im = _band(wire(bb_re3), per_sg), _band(wire(bb_im3), per_sg)
    cc_re = _band(wire(s5_c_re[0].transpose(0, 2, 1)), per_sg)
    cc_im = _band(wire(s5_c_im[0].transpose(0, 2, 1)), per_sg)

    taps = lru_conv_w.shape[1]
    cvec = jnp.concatenate([full['lru_conv_w'].reshape(taps, lw), full['lru_conv_b'], full['lru_b_rg'],
                            full['lru_b_ig'], full['lru_lam']], axis=0)
    cvec = cvec.reshape(taps + 4, half, lwc).transpose(1, 0, 2)
    wrg = _band(wire(lru_w_rg[0]), LRU_BLOCKS_PER_CHUNK)
    wig = _band(wire(lru_w_ig[0]), LRU_BLOCKS_PER_CHUNK)

    saved = []
    xc = x0
    for i in range(depth):
        sh1, sc1, g1, sh2, sc2, g2 = [mod[i, k] for k in range(n_mod)]
        gn = full['norm_g'][i]
        h1 = _norm_mod_fwd(f"norm1_fwd{i}", xc, gn[0:1], sc1, sh1)
        if i % 2 == 0:
            u = riding(('s5_glu', s5_w_glu[0]), _mm_row, f"s5_in{i}", h1[None], gw['s5_in'].reshape(d, d))
            s_re, s_im, s_rem, s_imm = riding((('gu', i), ffn_w_gu[i]), _s5_scan_fwd, f"s5_scan{i}", u, bb_re, bb_im,
                                              ab_re, ab_im)
            ypre, yact = _s5_out_fwd(f"s5_out{i}", s_rem, s_imm, cc_re, cc_im, u, s5_d)
            z = riding((('down', i), ffn_w_down[i]), _mm_col, f"s5_glu{i}", yact, gw['s5_glu'])
            x1 = _glu_resid_fwd(f"s5_resid{i}", z, xc, g1)
            mix = (u, s_re, s_im, s_rem, s_imm, ypre, yact, z)
        else:
            zz = _mm_col(f"lru_in{i}", h1, gw['lru_in'])
            hs, ylru = riding((('gu', i), ffn_w_gu[i]), _lru_fwd, f"lru_core{i}", zz, cvec, wrg, wig)
            o = _mm_row(f"lru_out{i}", ylru, gw['lru_out'].reshape(lw, d))
            x1 = _resid(f"lru_resid{i}", xc, o, g1)
            mix = (zz, hs, ylru, o)
        h2 = _norm_mod_fwd(f"norm2_fwd{i}", x1, gn[1:2], sc2, sh2)
        if i % 2 == 0:
            gu = riding(('lru_in', lru_w_in[0]), _mm_col, f"ffn_gu{i}", h2, gw['gu', i], MXU_DTYPE)
            act = _swiglu_act_fwd(f"ffn_act{i}", gu)
            f = riding(('lru_out', lru_w_out[0]), _mm_row, f"ffn_down{i}", act, gw['down', i].reshape(-1, d))
        else:
            gu = riding((('down', i), ffn_w_down[i]), _mm_col, f"ffn_gu{i}", h2, gw['gu', i], MXU_DTYPE)
            act = _swiglu_act_fwd(f"ffn_act{i}", gu)
            f = _mm_row(f"ffn_down{i}", act, gw['down', i].reshape(-1, d))
        x2 = _resid(f"ffn_resid{i}", x1, f, g2)
        saved.append((xc, h1, mix, x1, h2, gu, act, f))
        xc = x2

    dx, loss_part, d_final_g = _loss_bwd("loss", xc, tgt, final_g[None])
    loss = lax.psum(loss_part[0, 0], ("x", "y", "c"))

    grads = {}
    parts = {}
    dmod = [None] * depth
    d_norm_g = [None] * depth
    core = lax.axis_index("c").astype(jnp.int32).reshape(1)
    chunked = lambda p: p.reshape(N_DEV, -1, p.shape[-1])
    to_sibling = lambda p: _sibling_rider(chunked(p))
    pair = lambda name, p, got: _pair_sum(name, chunked(p), got, core)
    over_ici = lambda sums: _chunk_rider([sums], SAME_CORE)

    above = None
    for i in reversed(range(depth)):
        xin, h1, mix, x1, h2, gu, act, f = saved[i]
        sh1, sc1, g1, sh2, sc2, g2 = [mod[i, k] for k in range(n_mod)]
        gn = full['norm_g'][i]
        g_down = gw['down', i].reshape(-1, d)
        df, dg2 = _gate_bwd(f"ffn_gate_bwd{i}", dx, f, g2)
        if above is None:
            dact = _mm_row_da(f"ffn_down_da{i}", df, g_down, half)
        else:
            dact, (got,) = _ride(_mm_row_da, f"ffn_down_da{i}", df, g_down, half, riders=[to_sibling(above[1])])
            s_above = pair(f"x_{above[0][0]}_pair", above[1], got)
        p_down = _mm_row_db(f"ffn_down_db{i}", act, df, WIRE_DTYPE)
        dgu, (got,) = _ride(_swiglu_act_bwd, f"ffn_act_bwd{i}", gu, dact, riders=[to_sibling(p_down)])
        s_down = pair(f"x_ffn_w_down{i}_pair", p_down, got)
        if above is None:
            dh2, (parts['ffn_w_down', i],) = _ride(_mm_col_da, f"ffn_gu_da{i}", dgu, gw['gu', i], riders=[over_ici(s_down)])
            p_gu = _mm_col_db(f"ffn_gu_db{i}", h2, dgu, WIRE_DTYPE)
        else:
            dh2, (parts[above[0]],) = _ride(_mm_col_da, f"ffn_gu_da{i}", dgu, gw['gu', i], riders=[over_ici(s_above)])
            p_gu, (parts['ffn_w_down', i],) = _ride(_mm_col_db, f"ffn_gu_db{i}", h2, dgu, WIRE_DTYPE,
                                                    riders=[over_ici(s_down)])
        dx, dgn2, dsc2, dsh2 = _norm_mod_bwd(f"norm2_bwd{i}", x1, dh2, dx, gn[1:2], sc2)
        if i % 2 == 0:
            u, s_re, s_im, s_rem, s_imm, ypre, yact, z = mix
            dz, dg1 = _glu_resid_bwd(f"s5_resid_bwd{i}", z, dx, g1)
            dyact, (got,) = _ride(_mm_col_da, f"s5_glu_da{i}", dz, gw['s5_glu'], riders=[to_sibling(p_gu)])
            s_gu = pair(f"x_ffn_w_gu{i}_pair", p_gu, got)
            p_glu = _mm_col_db(f"s5_glu_db{i}", yact, dz, WIRE_DTYPE)
            dyp = _gelu_bwd(f"s5_gelu_bwd{i}", dyact, ypre)
            (l_rem, l_imm, dab_re, dab_im), (parts['ffn_w_gu', i], got) = _ride(
                _s5_scan_bwd, f"s5_scan_bwd{i}", dyp, cc_re, cc_im, ab_re, ab_im, s_re, s_im,
                riders=[over_ici(s_gu), to_sibling(p_glu)])
            s_glu = pair("x_s5_w_glu_pair", p_glu, got)
            (du, dbb_re, dbb_im, dcc_re, dcc_im, dd), (parts['s5_w_glu', 0],) = _ride(
                _s5_grads, f"s5_grads{i}", l_rem, l_imm, s_rem, s_imm, u, dyp, bb_re, bb_im, s5_d, riders=[over_ici(s_glu)])
            dlr, dli, dld, dbr, dbi = _s5_disc_bwd(
                "s5_disc_bwd", p_lr, p_li, p_ld, p_br, p_bi, dab_re.reshape(groups, 1, states),
                dab_im.reshape(groups, 1, states), _unband(dbb_re, per_sg), _unband(dbb_im, per_sg))
            grads['s5_lam_re'], grads['s5_lam_im'], grads['s5_log_dt'] = dlr[:, 0][None], dli[:, 0][None], dld[:, 0, 0][None]
            grads['s5_b_re'], grads['s5_b_im'] = dbr.transpose(0, 2, 1)[None], dbi.transpose(0, 2, 1)[None]
            grads['s5_c_re'] = _unband(dcc_re, per_sg).transpose(0, 2, 1)[None]
            grads['s5_c_im'] = _unband(dcc_im, per_sg).transpose(0, 2, 1)[None]
            grads['s5_d'] = dd
            dub = du.astype(MXU_DTYPE)
            p_s5_in = _mm_row_db(f"s5_in_db{i}", h1[None], dub, WIRE_DTYPE)
            dh1, (got,) = _ride(_mm_row_da, f"s5_in_da{i}", dub, gw['s5_in'].reshape(d, d), 1, riders=[to_sibling(p_s5_in)])
            dh1 = dh1[0]
            s_s5_in = pair("x_s5_w_in_pair", p_s5_in, got)
        else:
            zz, hs, ylru, o = mix
            g_lru_out = gw['lru_out'].reshape(lw, d)
            do, dg1 = _gate_bwd(f"lru_gate_bwd{i}", dx, o, g1)
            dyl, (got,) = _ride(_mm_row_da, f"lru_out_da{i}", do, g_lru_out, half, riders=[to_sibling(p_gu)])
            s_gu = pair(f"x_ffn_w_gu{i}_pair", p_gu, got)
            p_lru_out = _mm_row_db(f"lru_out_db{i}", ylru, do, WIRE_DTYPE)
            (dgb, dxp, dcv, dwrg, dwig), (parts['ffn_w_gu', i], got) = _ride(
                _lru_bwd, f"lru_core_bwd{i}", zz, hs, dyl, cvec, wrg, wig, riders=[over_ici(s_gu), to_sibling(p_lru_out)])
            s_lru_out = pair("x_lru_w_out_pair", p_lru_out, got)
            dzz = jnp.concatenate([dgb, dxp], axis=0)
            dh1, (parts['lru_w_out', 0],) = _ride(_mm_col_da, f"lru_in_da{i}", dzz, gw['lru_in'],
                                                  riders=[over_ici(s_lru_out)])
            above = (('lru_w_in', 0), _mm_col_db(f"lru_in_db{i}", h1, dzz, WIRE_DTYPE))
            dcv = dcv.transpose(1, 0, 2).reshape(taps + 4, lw)
            grads['lru_conv_w'] = dcv[:taps].reshape(1, taps, 1, lw)
            grads['lru_conv_b'], grads['lru_b_rg'] = dcv[taps:taps + 1], dcv[taps + 1:taps + 2]
            grads['lru_b_ig'], grads['lru_lam'] = dcv[taps + 2:taps + 3], dcv[taps + 3:taps + 4]
            grads['lru_w_rg'] = _unband(dwrg, LRU_BLOCKS_PER_CHUNK)[None]
            grads['lru_w_ig'] = _unband(dwig, LRU_BLOCKS_PER_CHUNK)[None]
        dx, dgn1, dsc1, dsh1 = _norm_mod_bwd(f"norm1_bwd{i}", xin, dh1, dx, gn[0:1], sc1)
        dmod[i] = jnp.concatenate([dsh1, dsc1, dg1, dsh2, dsc2, dg2], axis=1)
        d_norm_g[i] = jnp.concatenate([dgn1, dgn2], axis=0)
    grad_x = dx[None]
    dmod = jnp.concatenate(dmod, axis=0)
    grads['norm_g'] = jnp.stack(d_norm_g)
    grads['b_ada'] = dmod
    grads['final_g'] = d_final_g[0]

    small_partial = _pack([grads[n] for n in SMALL], SUBLANE * N_DEV)
    rows8 = small_partial.shape[0] // N_DEV
    small_partial = small_partial.reshape(N_DEV, rows8, LANE)
    s_small = pair("x_small_pair", small_partial, _ride_alone("x_small_d2d", _sibling_rider(small_partial)))
    parts['s5_w_in', 0], small_parts = _ride_alone("x_tail_ici", _join([over_ici(s_s5_in), over_ici(s_small)]))

    out = {}
    dmod_all = _all_gather("ag_dmod", dmod)
    dmod_loc = lax.dynamic_slice_in_dim(dmod_all, me * n_loc, n_loc, axis=2).transpose(1, 0, 2)
    dmod16 = jnp.pad(dmod_loc, ((0, 0), (0, 2 * SUBLANE - N_DEV), (0, 0)))
    out['w_ada'] = _adamw_w_ada("adamw_w_ada", c16, dmod16, w_ada, m_w_ada, v_w_ada)

    for name in BIG[1:]:
        w = wv[name]
        rows, cols = w.shape[-2] * w.shape[0], w.shape[-1]
        flat = lambda a: a.reshape(rows, cols)
        res = _adamw_sum("adamw_" + name, [parts[name, l] for l in range(w.shape[0])], flat(w), flat(mv[name]),
                         flat(vv[name]))
        out[name] = [r.reshape(w.shape) for r in res]

    summed = _sum_parts("sum_small", small_parts)
    small_total = _all_gather("ag_small_sum", summed).reshape(-1, LANE)
    small_grad = dict(zip(SMALL, _unpack(small_total, [grads[n].shape for n in SMALL])))
    for n in SMALL_SHARDED:
        shard = wv[n].shape[-1]
        small_grad[n] = lax.dynamic_slice_in_dim(small_grad[n], me * shard, shard, axis=small_grad[n].ndim - 1)
    small_shapes = [wv[n].shape for n in SMALL]
    pk = lambda dct: _pack([dct[n] for n in SMALL], 2 * SUBLANE, 8 * LANE)
    s_out = _adamw_sum("adamw_small", [pk(small_grad)[None]], pk(wv), pk(mv), pk(vv))
    out.update({n: r for n, *r in zip(SMALL, *[_unpack(o, small_shapes) for o in s_out])})

    return (loss, grad_x, *[out[n][0] for n in WEIGHTS], *[out[n][1] for n in WEIGHTS],
            *[out[n][2] for n in WEIGHTS], *[out[n][3] for n in WEIGHTS])
```

```python
import functools
import math

import jax
import jax.numpy as jnp
from jax import lax
from jax.experimental import pallas as pl
from jax.experimental.pallas import tpu as pltpu

F32 = jnp.float32
MXU_DTYPE = jnp.bfloat16
WIRE_DTYPE = jnp.bfloat16
N_DEV = 8
EPS = 1e-6
LRU_C = 8.0
S5_GROUP = 16
S5_STATE = 64
S5_SUPER = 256
LRU_BLOCKS_PER_CHUNK = 4
ADAM_LR, ADAM_B1, ADAM_B2, ADAM_EPS, ADAM_WD, ADAM_STEP = 0.001, 0.9, 0.999, 1e-08, 0.01, 10
VMEM_LIMIT_BYTES = 56 * 1024 * 1024
LANE = 128
SUBLANE = 8

WEIGHTS = ['norm_g', 'w_ada', 'b_ada', 's5_w_in', 's5_lam_re', 's5_lam_im', 's5_log_dt', 's5_b_re', 's5_b_im',
           's5_c_re', 's5_c_im', 's5_d', 's5_w_glu', 'lru_w_in', 'lru_conv_w', 'lru_conv_b', 'lru_w_rg', 'lru_b_rg',
           'lru_w_ig', 'lru_b_ig', 'lru_lam', 'lru_w_out', 'ffn_w_gu', 'ffn_w_down', 'final_g']
BIG = ('w_ada', 's5_w_in', 's5_w_glu', 'lru_w_in', 'lru_w_out', 'ffn_w_gu', 'ffn_w_down')
SMALL = tuple(n for n in WEIGHTS if n not in BIG)
SMALL_SHARDED = ('norm_g', 'lru_conv_w', 'lru_conv_b', 'lru_b_rg', 'lru_b_ig', 'lru_lam')

NN = (((1,), (0,)), ((), ()))
NT = (((1,), (1,)), ((), ()))
TN = (((0,), (0,)), ((), ()))


def _params(n_grid):
    return pltpu.CompilerParams(dimension_semantics=("arbitrary",) * n_grid, vmem_limit_bytes=VMEM_LIMIT_BYTES)


def _tile(dim, pref, align=LANE):
    if dim <= pref:
        return dim
    t = (pref // align) * align
    while t >= align:
        if dim % t == 0:
            return t
        t -= align
    return dim


def _dot(a, b, dims):
    return lax.dot_general(a.astype(MXU_DTYPE), b.astype(MXU_DTYPE), dims, preferred_element_type=F32)


def _gelu(x):
    k = math.sqrt(2.0 / math.pi)
    return 0.5 * x * (1.0 + jnp.tanh(k * (x + 0.044715 * (x * x * x))))


def _gelu_and_grad(x):
    k = math.sqrt(2.0 / math.pi)
    th = jnp.tanh(k * (x + 0.044715 * (x * x * x)))
    g = 0.5 * x * (1.0 + th)
    dg = 0.5 * (1.0 + th) + 0.5 * x * (1.0 - th * th) * (k * (1.0 + 3.0 * 0.044715 * (x * x)))
    return g, dg


def _neg_expm1(x):
    series = -x * (1.0 + x * (0.5 + x * (1.0 / 6.0 + x * (1.0 / 24.0 + x * (1.0 / 120.0)))))
    return jnp.where(x > -0.01, series, 1.0 - jnp.exp(x))


MESH = pl.DeviceIdType.MESH
N_CHIP = N_DEV // 2
ALL, SAME_CORE = 7, 6


def _place():
    x, y, c = lax.axis_index("x"), lax.axis_index("y"), lax.axis_index("c")
    return x, y, c


def _flip(place, k):
    x, y, c = place
    return (1 - x if (k >> 2) & 1 else x, 1 - y if (k >> 1) & 1 else y, 1 - c if k & 1 else c)


def _chunk_exchange(name, xs, group):
    return _ride_alone(name, _chunk_rider(xs, group))


class _Rider:
    def __init__(self, arrays, out_shape, scratch, start, finish, post):
        self.arrays, self.out_shape, self.scratch = list(arrays), list(out_shape), list(scratch)
        self.start, self.finish, self.post = start, finish, post


def _chunk_rider(xs, group):
    n = len(xs)
    members, r, c_ = xs[0].shape
    assert members == {ALL: N_DEV, SAME_CORE: N_CHIP}[group]
    assert all(a.shape == xs[0].shape and a.dtype == xs[0].dtype for a in xs)
    ks = [k for k in range(1, N_DEV) if not k & ~group]
    member = (lambda p: 4 * p[0] + 2 * p[1] + p[2]) if group == ALL else (lambda p: 2 * p[0] + p[1])

    def copies(ins, outs, scratch):
        out = outs[0]
        send_sems, recv_sems, local_sems = scratch
        place = _place()
        me = member(place)
        local = [pltpu.make_async_copy(ins[l].at[me], out.at[me, l], local_sems.at[l]) for l in range(n)]
        remote = []
        for l in range(n):
            for k in ks:
                pid = _flip(place, k)
                peer = member(pid)

                def copy(land_at, l=l, k=k, peer=peer, pid=pid):
                    return pltpu.make_async_remote_copy(
                        src_ref=ins[l].at[peer], dst_ref=out.at[land_at, l], send_sem=send_sems.at[l * N_DEV + k],
                        recv_sem=recv_sems.at[l * N_DEV + k], device_id=pid, device_id_type=MESH)

                remote.append((copy, me, peer))
        return local, remote

    def start(ins, outs, scratch):
        local, remote = copies(ins, outs, scratch)
        for cp in local:
            cp.start()
        for copy, me, _ in remote:
            copy(me).start()

    def finish(ins, outs, scratch):
        local, remote = copies(ins, outs, scratch)
        for copy, me, peer in remote:
            copy(me).wait_send()
            copy(peer).wait_recv()
        for cp in local:
            cp.wait()

    return _Rider(
        xs, [jax.ShapeDtypeStruct((members, n, r, c_), xs[0].dtype)],
        [pltpu.SemaphoreType.DMA((n * N_DEV,)), pltpu.SemaphoreType.DMA((n * N_DEV,)), pltpu.SemaphoreType.DMA((n,))],
        start, finish, lambda outs: outs[0].reshape(members, n * r, c_))


def _gather_rider(xs):
    n = len(xs)
    chip_flips = (2, 4, 6)
    per = 1 + 2 * len(chip_flips)

    def plan(ins, outs, scratch):
        send_sems, recv_sems, local_sems = scratch
        place = _place()
        sibling = _flip(place, 1)
        jobs = []
        for l in range(n):
            slot = lambda p, l=l: outs[l].at[2 * p[0] + p[1], p[2]]

            def copy(k, block, to, src=None, l=l, slot=slot):
                return pltpu.make_async_remote_copy(
                    src_ref=slot(block) if src is None else src, dst_ref=slot(block), send_sem=send_sems.at[l * per + k],
                    recv_sem=recv_sems.at[l * per + k], device_id=to, device_id_type=MESH)

            mine = pltpu.make_async_copy(ins[l], slot(place), local_sems.at[l])
            first = [copy(0, place, sibling, src=ins[l])]
            first += [copy(1 + j, place, _flip(place, k), src=ins[l]) for j, k in enumerate(chip_flips)]
            jobs.append((copy, mine, first))
        return place, sibling, jobs

    def start(ins, outs, scratch):
        _, _, jobs = plan(ins, outs, scratch)
        for _, mine, first in jobs:
            mine.start()
            for cp in first:
                cp.start()

    def finish(ins, outs, scratch):
        place, sibling, jobs = plan(ins, outs, scratch)
        passed = []
        for copy, _, _ in jobs:
            for j, k in enumerate(chip_flips):
                copy(1 + j, _flip(place, k), place).wait_recv()
                fwd = copy(4 + j, _flip(place, k), sibling)
                fwd.start()
                passed.append(fwd)
        for copy, mine, first in jobs:
            copy(0, sibling, place).wait_recv()
            for j, k in enumerate(chip_flips):
                copy(4 + j, _flip(sibling, k), place).wait_recv()
            for cp in first:
                cp.wait_send()
            mine.wait()
        for cp in passed:
            cp.wait_send()

    return _Rider(
        xs, [jax.ShapeDtypeStruct((N_CHIP, 2) + x.shape, x.dtype) for x in xs],
        [pltpu.SemaphoreType.DMA((n * per,)), pltpu.SemaphoreType.DMA((n * per,)), pltpu.SemaphoreType.DMA((n,))],
        start, finish, lambda outs: [o.reshape((N_DEV,) + x.shape) for o, x in zip(outs, xs)])


HBM_SPEC = pl.BlockSpec(memory_space=pltpu.HBM)


def _ride_alone(name, rider):
    n_in, n_out = len(rider.arrays), len(rider.out_shape)

    def body(*refs):
        parts = refs[:n_in], refs[n_in:n_in + n_out], refs[n_in + n_out:]
        rider.start(*parts)
        rider.finish(*parts)

    outs = pl.pallas_call(body, name=name, out_shape=rider.out_shape, in_specs=[HBM_SPEC] * n_in,
                          out_specs=[HBM_SPEC] * n_out, scratch_shapes=rider.scratch)(*rider.arrays)
    return rider.post(list(outs))


def _call(body, *, name, grid, in_specs, out_specs, out_shape, scratch_shapes=(), args, rider=None):
    single = not isinstance(out_shape, (list, tuple))
    out_shape = [out_shape] if single else list(out_shape)
    out_specs = [out_specs] if single else list(out_specs)
    scratch_shapes = list(scratch_shapes)
    unwrap = lambda outs: outs[0] if single else list(outs)
    if rider is None:
        outs = pl.pallas_call(body, name=name, grid=grid, in_specs=list(in_specs), out_specs=out_specs, out_shape=out_shape,
                              scratch_shapes=scratch_shapes, compiler_params=_params(len(grid)))(*args)
        return unwrap(outs)
    n_in, n_out, n_scr = len(in_specs), len(out_shape), len(scratch_shapes)
    r_in, r_out = len(rider.arrays), len(rider.out_shape)

    def carried(*refs):
        ins, refs = refs[:n_in], refs[n_in:]
        r_ins, refs = refs[:r_in], refs[r_in:]
        outs, refs = refs[:n_out], refs[n_out:]
        r_outs, refs = refs[:r_out], refs[r_out:]
        scr, r_scr = refs[:n_scr], refs[n_scr:]
        steps = [pl.program_id(ax) for ax in range(len(grid))]

        @pl.when(functools.reduce(jnp.logical_and, [s == 0 for s in steps]))
        def _():
            rider.start(r_ins, r_outs, r_scr)

        body(*ins, *outs, *scr)

        @pl.when(functools.reduce(jnp.logical_and, [s == g - 1 for s, g in zip(steps, grid)]))
        def _():
            rider.finish(r_ins, r_outs, r_scr)

    outs = pl.pallas_call(
        carried, name=name, grid=grid, in_specs=list(in_specs) + [HBM_SPEC] * r_in, out_specs=out_specs + [HBM_SPEC] * r_out,
        out_shape=out_shape + rider.out_shape, scratch_shapes=scratch_shapes + rider.scratch,
        compiler_params=_params(len(grid)))(*args, *rider.arrays)
    return unwrap(outs[:n_out]), rider.post(list(outs[n_out:]))


def _all_gather(name, x):
    return _ride_alone(name, _gather_rider([x]))[0]


def _sibling_rider(x):
    _, r, c_ = x.shape

    def copies(ins, outs, scratch):
        send_sems, recv_sems = scratch
        place = _place()
        return [pltpu.make_async_remote_copy(
            src_ref=ins[0].at[2 * chip + (1 - place[2])], dst_ref=outs[0].at[chip], send_sem=send_sems.at[chip],
            recv_sem=recv_sems.at[chip], device_id=_flip(place, 1), device_id_type=MESH) for chip in range(N_CHIP)]

    def start(ins, outs, scratch):
        for cp in copies(ins, outs, scratch):
            cp.start()

    def finish(ins, outs, scratch):
        for cp in copies(ins, outs, scratch):
            cp.wait()

    return _Rider([x], [jax.ShapeDtypeStruct((N_CHIP, r, c_), x.dtype)],
                  [pltpu.SemaphoreType.DMA((N_CHIP,)), pltpu.SemaphoreType.DMA((N_CHIP,))], start, finish, lambda outs: outs[0])


def _join(riders):
    def cut(seq, counts):
        out, off = [], 0
        for k in counts:
            out.append(seq[off:off + k])
            off += k
        return out

    def parts(ins, outs, scratch):
        return zip(riders, cut(ins, [len(r.arrays) for r in riders]), cut(outs, [len(r.out_shape) for r in riders]),
                   cut(scratch, [len(r.scratch) for r in riders]))

    def start(ins, outs, scratch):
        for r, i, o, s in parts(ins, outs, scratch):
            r.start(i, o, s)

    def finish(ins, outs, scratch):
        for r, i, o, s in parts(ins, outs, scratch):
            r.finish(i, o, s)

    return _Rider(
        [a for r in riders for a in r.arrays], [o for r in riders for o in r.out_shape], [s for r in riders for s in r.scratch],
        start, finish, lambda outs: [r.post(o) for r, o in zip(riders, cut(outs, [len(r.out_shape) for r in riders]))])


def _ride(fn, *args, riders):
    return fn(*args, rider=_join(riders))


def _pair_sum(name, x, got, core):
    _, r, c_ = x.shape
    br = _tile(r, 256, 2 * SUBLANE)

    def body(core_ref, x_ref, g_ref, o_ref):
        o_ref[...] = (x_ref[...].astype(F32) + g_ref[...].astype(F32)).astype(o_ref.dtype)

    return pl.pallas_call(
        body, name=name, out_shape=jax.ShapeDtypeStruct((N_CHIP, r, c_), x.dtype),
        grid_spec=pltpu.PrefetchScalarGridSpec(
            num_scalar_prefetch=1, grid=(N_CHIP, r // br),
            in_specs=[pl.BlockSpec((None, br, c_), lambda ch, i, core_ref: (2 * ch + core_ref[0], i, 0)),
                      pl.BlockSpec((None, br, c_), lambda ch, i, core_ref: (ch, i, 0))],
            out_specs=pl.BlockSpec((None, br, c_), lambda ch, i, core_ref: (ch, i, 0))),
        compiler_params=_params(2))(core, x, got)


def _mm(name, a, b, out_shape, out_dtype, grid, a_spec, b_spec, o_spec, dims, n_red, acc_shape, rider=None):
    red = tuple(range(len(grid) - n_red, len(grid)))
    out_type = jax.ShapeDtypeStruct(out_shape, out_dtype)
    if all(grid[ax] == 1 for ax in red):
        def single(a_ref, b_ref, o_ref):
            o_ref[...] = _dot(a_ref[...], b_ref[...], dims).astype(o_ref.dtype)

        return _call(single, name=name, out_shape=out_type, grid=grid, in_specs=[a_spec, b_spec], out_specs=o_spec,
                     args=(a, b), rider=rider)

    def body(a_ref, b_ref, o_ref, acc_ref):
        first = functools.reduce(jnp.logical_and, [pl.program_id(ax) == 0 for ax in red])
        last = functools.reduce(jnp.logical_and, [pl.program_id(ax) == grid[ax] - 1 for ax in red])

        @pl.when(first)
        def _():
            acc_ref[...] = jnp.zeros_like(acc_ref)

        acc_ref[...] += _dot(a_ref[...], b_ref[...], dims)

        @pl.when(last)
        def _():
            o_ref[...] = acc_ref[...].astype(o_ref.dtype)

    return _call(body, name=name, out_shape=out_type, grid=grid, in_specs=[a_spec, b_spec], out_specs=o_spec,
                 scratch_shapes=[pltpu.VMEM(acc_shape, F32)], args=(a, b), rider=rider)


def _mm_col(name, a, b, out_dtype=F32, rider=None):
    m, k = a.shape
    j, _, n = b.shape
    bm, bk = _tile(m, 1024), _tile(k, 2048)
    return _mm(name, a, b, (j, m, n), out_dtype, (j, m // bm, k // bk),
               pl.BlockSpec((bm, bk), lambda jj, mm, kk: (mm, kk)),
               pl.BlockSpec((None, bk, n), lambda jj, mm, kk: (jj, kk, 0)),
               pl.BlockSpec((None, bm, n), lambda jj, mm, kk: (jj, mm, 0)), NN, 1, (bm, n), rider)


def _mm_col_da(name, do, b, rider=None):
    j, m, n = do.shape
    k = b.shape[1]
    bm, bk = _tile(m, 1024), _tile(k, 1024)
    return _mm(name, do, b, (m, k), F32, (m // bm, k // bk, j),
               pl.BlockSpec((None, bm, n), lambda mm, kk, jj: (jj, mm, 0)),
               pl.BlockSpec((None, bk, n), lambda mm, kk, jj: (jj, kk, 0)),
               pl.BlockSpec((bm, bk), lambda mm, kk, jj: (mm, kk)), NT, 1, (bm, bk), rider)


def _mm_col_db(name, a, do, out_dtype, rider=None):
    m, k = a.shape
    j, _, n = do.shape
    bm, bk = _tile(m, 2048), _tile(k, 512)
    return _mm(name, a, do, (j, k, n), out_dtype, (j, k // bk, m // bm),
               pl.BlockSpec((bm, bk), lambda jj, kk, mm: (mm, kk)),
               pl.BlockSpec((None, bm, n), lambda jj, kk, mm: (jj, mm, 0)),
               pl.BlockSpec((None, bk, n), lambda jj, kk, mm: (jj, kk, 0)), TN, 1, (bk, n), rider)


def _row_bk(kq):
    return kq if (kq % LANE or kq // LANE in (11,)) else _tile(kq, 2048)


def _mm_row(name, a, b, out_dtype=F32, rider=None):
    q, m, kq = a.shape
    n = b.shape[1]
    bm, bn, bk = _tile(m, 1024), _tile(n, 1024), _row_bk(kq)
    nk = kq // bk
    return _mm(name, a, b, (m, n), out_dtype, (m // bm, n // bn, q, nk),
               pl.BlockSpec((None, bm, bk), lambda mm, nn, qq, kk: (qq, mm, kk)),
               pl.BlockSpec((bk, bn), lambda mm, nn, qq, kk: (qq * nk + kk, nn)),
               pl.BlockSpec((bm, bn), lambda mm, nn, qq, kk: (mm, nn)), NN, 2, (bm, bn), rider)


def _mm_row_da(name, do, b, q, rider=None):
    m, n = do.shape
    kq = b.shape[0] // q
    bm, bn = _tile(m, 1024), _tile(n, 2048)
    return _mm(name, do, b, (q, m, kq), F32, (q, m // bm, n // bn),
               pl.BlockSpec((bm, bn), lambda qq, mm, nn: (mm, nn)),
               pl.BlockSpec((kq, bn), lambda qq, mm, nn: (qq, nn)),
               pl.BlockSpec((None, bm, kq), lambda qq, mm, nn: (qq, mm, 0)), NT, 1, (bm, kq), rider)


def _mm_row_db(name, a, do, out_dtype):
    q, m, kq = a.shape
    n = do.shape[1]
    bm, bn = _tile(m, 2048), _tile(n, 512)
    return _mm(name, a, do, (q * kq, n), out_dtype, (q, n // bn, m // bm),
               pl.BlockSpec((None, bm, kq), lambda qq, nn, mm: (qq, mm, 0)),
               pl.BlockSpec((bm, bn), lambda qq, nn, mm: (mm, nn)),
               pl.BlockSpec((kq, bn), lambda qq, nn, mm: (qq, nn)), TN, 1, (kq, bn))


def _row_spec(bm, d):
    return pl.BlockSpec((bm, d), lambda i: (i, 0))


def _vec_spec(d):
    return pl.BlockSpec((1, d), lambda i: (0, 0))


def _norm_mod_fwd(name, x, gain, sc, sh):
    t, d = x.shape
    bm = _tile(t, 256, SUBLANE)

    def body(x_ref, g_ref, sc_ref, sh_ref, h_ref):
        xv = x_ref[...]
        rstd = lax.rsqrt(jnp.mean(xv * xv, axis=-1, keepdims=True) + EPS)
        h_ref[...] = ((xv * rstd) * g_ref[...] * (1.0 + sc_ref[...]) + sh_ref[...]).astype(h_ref.dtype)

    return pl.pallas_call(
        body, name=name, out_shape=jax.ShapeDtypeStruct((t, d), MXU_DTYPE), grid=(t // bm,),
        in_specs=[_row_spec(bm, d), _vec_spec(d), _vec_spec(d), _vec_spec(d)], out_specs=_row_spec(bm, d),
        compiler_params=_params(1))(x, gain, sc, sh)


def _norm_mod_bwd(name, x, dh, dres, gain, sc):
    t, d = x.shape
    bm = _tile(t, 256, SUBLANE)

    def body(x_ref, dh_ref, dres_ref, g_ref, sc_ref, dx_ref, dg_ref, dsc_ref, dsh_ref):
        @pl.when(pl.program_id(0) == 0)
        def _():
            dg_ref[...] = jnp.zeros_like(dg_ref)
            dsc_ref[...] = jnp.zeros_like(dsc_ref)
            dsh_ref[...] = jnp.zeros_like(dsh_ref)

        xv, dh_ = x_ref[...], dh_ref[...]
        rstd = lax.rsqrt(jnp.mean(xv * xv, axis=-1, keepdims=True) + EPS)
        nrm = xv * rstd
        gain_ = g_ref[...]
        dsh_ref[...] += jnp.sum(dh_, axis=0, keepdims=True)
        dsc_ref[...] += jnp.sum(dh_ * (nrm * gain_), axis=0, keepdims=True)
        dhn = dh_ * (1.0 + sc_ref[...])
        dg_ref[...] += jnp.sum(dhn * nrm, axis=0, keepdims=True)
        dn = dhn * gain_
        dx_ref[...] = dres_ref[...] + rstd * (dn - nrm * jnp.mean(dn * nrm, axis=-1, keepdims=True))

    vec = jax.ShapeDtypeStruct((1, d), F32)
    return pl.pallas_call(
        body, name=name, out_shape=[jax.ShapeDtypeStruct((t, d), F32), vec, vec, vec], grid=(t // bm,),
        in_specs=[_row_spec(bm, d), _row_spec(bm, d), _row_spec(bm, d), _vec_spec(d), _vec_spec(d)],
        out_specs=[_row_spec(bm, d), _vec_spec(d), _vec_spec(d), _vec_spec(d)],
        compiler_params=_params(1))(x, dh, dres, gain, sc)


def _loss_bwd(name, x, target, gain):
    t, d = x.shape
    bm = _tile(t, 256, SUBLANE)

    def body(x_ref, t_ref, g_ref, dx_ref, loss_ref, dg_ref):
        @pl.when(pl.program_id(0) == 0)
        def _():
            loss_ref[...] = jnp.zeros_like(loss_ref)
            dg_ref[...] = jnp.zeros_like(dg_ref)

        xv = x_ref[...]
        rstd = lax.rsqrt(jnp.mean(xv * xv, axis=-1, keepdims=True) + EPS)
        nrm = xv * rstd
        gain_ = g_ref[...]
        err = nrm * gain_ - t_ref[...]
        per_tok = jnp.mean(err * err, axis=-1, keepdims=True)
        loss_ref[...] += 0.5 * jnp.sum(per_tok, axis=0, keepdims=True)
        dout = err * (1.0 / d)
        dg_ref[...] += jnp.sum(dout * nrm, axis=0, keepdims=True)
        dn = dout * gain_
        dx_ref[...] = rstd * (dn - nrm * jnp.mean(dn * nrm, axis=-1, keepdims=True))

    return pl.pallas_call(
        body, name=name,
        out_shape=[jax.ShapeDtypeStruct((t, d), F32), jax.ShapeDtypeStruct((1, 1), F32),
                   jax.ShapeDtypeStruct((1, d), F32)],
        grid=(t // bm,), in_specs=[_row_spec(bm, d), _row_spec(bm, d), _vec_spec(d)],
        out_specs=[_row_spec(bm, d), pl.BlockSpec((1, 1), lambda i: (0, 0)), _vec_spec(d)],
        compiler_params=_params(1))(x, target, gain)


def _resid(name, x, y, g):
    t, d = x.shape
    bm = _tile(t, 256, SUBLANE)

    def body(x_ref, y_ref, g_ref, o_ref):
        o_ref[...] = x_ref[...] + g_ref[...] * y_ref[...]

    return pl.pallas_call(
        body, name=name, out_shape=jax.ShapeDtypeStruct((t, d), F32), grid=(t // bm,),
        in_specs=[_row_spec(bm, d), _row_spec(bm, d), _vec_spec(d)], out_specs=_row_spec(bm, d),
        compiler_params=_params(1))(x, y, g)


def _gate_bwd(name, dx, y, g):
    t, d = dx.shape
    bm = _tile(t, 256, SUBLANE)

    def body(dx_ref, y_ref, g_ref, dy_ref, dg_ref):
        @pl.when(pl.program_id(0) == 0)
        def _():
            dg_ref[...] = jnp.zeros_like(dg_ref)

        dxv = dx_ref[...]
        dy_ref[...] = (g_ref[...] * dxv).astype(dy_ref.dtype)
        dg_ref[...] += jnp.sum(dxv * y_ref[...], axis=0, keepdims=True)

    return pl.pallas_call(
        body, name=name, out_shape=[jax.ShapeDtypeStruct((t, d), MXU_DTYPE), jax.ShapeDtypeStruct((1, d), F32)],
        grid=(t // bm,), in_specs=[_row_spec(bm, d), _row_spec(bm, d), _vec_spec(d)],
        out_specs=[_row_spec(bm, d), _vec_spec(d)], compiler_params=_params(1))(dx, y, g)


def _glu_resid_fwd(name, z, x, g):
    _, t, n = z.shape
    d = x.shape[1]
    half = N_DEV // 2
    bm = _tile(t, 256, SUBLANE)

    def body(v_ref, gt_ref, x_ref, g_ref, o_ref):
        o_ref[...] = x_ref[...] + g_ref[...] * (v_ref[...] * jax.nn.sigmoid(gt_ref[...]))

    return pl.pallas_call(
        body, name=name, out_shape=jax.ShapeDtypeStruct((t, d), F32), grid=(half, t // bm),
        in_specs=[pl.BlockSpec((None, bm, n), lambda q, i: (q, i, 0)),
                  pl.BlockSpec((None, bm, n), lambda q, i: (q + half, i, 0)),
                  pl.BlockSpec((bm, n), lambda q, i: (i, q)), pl.BlockSpec((1, n), lambda q, i: (0, q))],
        out_specs=pl.BlockSpec((bm, n), lambda q, i: (i, q)), compiler_params=_params(2))(z, z, x, g)


def _glu_resid_bwd(name, z, dx, g):
    _, t, n = z.shape
    d = dx.shape[1]
    half = N_DEV // 2
    bm = _tile(t, 256, SUBLANE)

    def body(z_ref, dx_ref, g_ref, dz_ref, dg_ref):
        @pl.when(pl.program_id(1) == 0)
        def _():
            dg_ref[...] = jnp.zeros_like(dg_ref)

        v, dxv = z_ref[0], dx_ref[...]
        sig = jax.nn.sigmoid(z_ref[1])
        dout = g_ref[...] * dxv
        dg_ref[...] += jnp.sum(dxv * (v * sig), axis=0, keepdims=True)
        dz_ref[0] = (dout * sig).astype(dz_ref.dtype)
        dz_ref[1] = (dout * v * (sig * (1.0 - sig))).astype(dz_ref.dtype)

    pair = pl.BlockSpec((2, None, bm, n), lambda q, i: (0, q, i, 0))
    dz, dg = pl.pallas_call(
        body, name=name,
        out_shape=[jax.ShapeDtypeStruct((2, half, t, n), MXU_DTYPE), jax.ShapeDtypeStruct((1, d), F32)],
        grid=(half, t // bm),
        in_specs=[pair, pl.BlockSpec((bm, n), lambda q, i: (i, q)), pl.BlockSpec((1, n), lambda q, i: (0, q))],
        out_specs=[pair, pl.BlockSpec((1, n), lambda q, i: (0, q))],
        compiler_params=_params(2))(z.reshape(2, half, t, n), dx, g)
    return dz.reshape(N_DEV, t, n), dg


def _swiglu_act_fwd(name, gu):
    _, t, n = gu.shape
    half = N_DEV // 2
    bm = _tile(t, 256, SUBLANE)

    def body(g_ref, u_ref, o_ref):
        gv = g_ref[...].astype(F32)
        o_ref[...] = (gv * jax.nn.sigmoid(gv) * u_ref[...].astype(F32)).astype(o_ref.dtype)

    return pl.pallas_call(
        body, name=name, out_shape=jax.ShapeDtypeStruct((half, t, n), MXU_DTYPE), grid=(half, t // bm),
        in_specs=[pl.BlockSpec((None, bm, n), lambda q, i: (q, i, 0)),
                  pl.BlockSpec((None, bm, n), lambda q, i: (q + half, i, 0))],
        out_specs=pl.BlockSpec((None, bm, n), lambda q, i: (q, i, 0)), compiler_params=_params(2))(gu, gu)


def _swiglu_act_bwd(name, gu, dact, rider=None):
    _, t, n = gu.shape
    half = N_DEV // 2
    bm = _tile(t, 256, SUBLANE)

    def body(gu_ref, da_ref, o_ref):
        gv, da = gu_ref[0].astype(F32), da_ref[...]
        sig = jax.nn.sigmoid(gv)
        o_ref[0] = (da * gu_ref[1].astype(F32) * (sig * (1.0 + gv * (1.0 - sig)))).astype(o_ref.dtype)
        o_ref[1] = (da * (gv * sig)).astype(o_ref.dtype)

    pair = pl.BlockSpec((2, None, bm, n), lambda q, i: (0, q, i, 0))
    res = _call(
        body, name=name, out_shape=jax.ShapeDtypeStruct((2, half, t, n), MXU_DTYPE), grid=(half, t // bm),
        in_specs=[pair, pl.BlockSpec((None, bm, n), lambda q, i: (q, i, 0))], out_specs=pair,
        args=(gu.reshape(2, half, t, n), dact), rider=rider)
    if rider is None:
        return res.reshape(N_DEV, t, n)
    return res[0].reshape(N_DEV, t, n), res[1]


def _ada_fwd(name, c16, w_ada, b_loc):
    nl, d, n = w_ada.shape
    bn = _tile(n, 512)

    def body(c_ref, w_ref, b_ref, o_ref):
        cv = c_ref[...]
        o_ref[...] = _dot(cv * jax.nn.sigmoid(cv), w_ref[...], NN) + b_ref[...]

    return pl.pallas_call(
        body, name=name, out_shape=jax.ShapeDtypeStruct((nl, c16.shape[0], n), F32), grid=(nl, n // bn),
        in_specs=[pl.BlockSpec(c16.shape, lambda i, j: (0, 0)), pl.BlockSpec((None, d, bn), lambda i, j: (i, 0, j)),
                  pl.BlockSpec((None, 1, bn), lambda i, j: (i, 0, j))],
        out_specs=pl.BlockSpec((None, c16.shape[0], bn), lambda i, j: (i, 0, j)),
        compiler_params=_params(2))(c16, w_ada, b_loc)


def _adam_update(g, w, m, v):
    m = ADAM_B1 * m + (1.0 - ADAM_B1) * g
    v = ADAM_B2 * v + (1.0 - ADAM_B2) * (g * g)
    m_hat = m / (1.0 - ADAM_B1 ** ADAM_STEP)
    v_hat = v / (1.0 - ADAM_B2 ** ADAM_STEP)
    delta = -ADAM_LR * (m_hat / (jnp.sqrt(v_hat) + ADAM_EPS) + ADAM_WD * w)
    return delta, m, v


def _adamw_w_ada(name, c16, dmod16, w, m, v, rider=None):
    nl, d, n = w.shape
    br = _tile(d, 256)

    def body(c_ref, dm_ref, w_ref, m_ref, v_ref, g_ref, dl_ref, mo_ref, vo_ref):
        cv = c_ref[...]
        g = _dot(cv * jax.nn.sigmoid(cv), dm_ref[...], TN)
        g_ref[...] = g
        dl_ref[...], mo_ref[...], vo_ref[...] = _adam_update(g, w_ref[...], m_ref[...], v_ref[...])

    blk = pl.BlockSpec((None, br, n), lambda i, r: (i, r, 0))
    shp = jax.ShapeDtypeStruct(w.shape, F32)
    return _call(
        body, name=name, out_shape=[shp] * 4, grid=(nl, d // br),
        in_specs=[pl.BlockSpec((c16.shape[0], br), lambda i, r: (0, r)),
                  pl.BlockSpec((None, dmod16.shape[1], n), lambda i, r: (i, 0, 0)), blk, blk, blk],
        out_specs=[blk] * 4, args=(c16, dmod16, w, m, v), rider=rider)


def _adamw_sum(name, parts, w, m, v, rider=None):
    nl = len(parts)
    p, r, c = parts[0].shape
    br = _tile(r, max(128, (1 << 17) // max(c, LANE)), 2 * SUBLANE)
    nb = r // br

    def body(*refs):
        p_refs, (w_ref, m_ref, v_ref, g_ref, dl_ref, mo_ref, vo_ref) = refs[:nl], refs[nl:]
        layer = pl.program_id(0)
        g = None
        for l, p_ref in enumerate(p_refs):
            gl = p_ref[0].astype(F32)
            for s in range(1, p):
                gl = gl + p_ref[s].astype(F32)
            g = gl if g is None else jnp.where(layer == l, gl, g)
        g_ref[...] = g
        dl_ref[...], mo_ref[...], vo_ref[...] = _adam_update(g, w_ref[...], m_ref[...], v_ref[...])

    blk = pl.BlockSpec((br, c), lambda l, i: (l * nb + i, 0))
    shp = jax.ShapeDtypeStruct((nl * r, c), F32)
    return _call(
        body, name=name, out_shape=[shp] * 4, grid=(nl, nb),
        in_specs=[pl.BlockSpec((p, br, c), lambda l, i: (0, i, 0))] * nl + [blk, blk, blk], out_specs=[blk] * 4,
        args=(*parts, w, m, v), rider=rider)


def _sum_parts(name, parts):
    p, r, c = parts.shape

    def body(p_ref, o_ref):
        g = p_ref[0]
        for s in range(1, p):
            g = g + p_ref[s]
        o_ref[...] = g

    return pl.pallas_call(body, name=name, out_shape=jax.ShapeDtypeStruct((r, c), F32))(parts)


def _s5_disc(name, lam_re, lam_im, log_dt, b_re, b_im):
    def body(lr_ref, li_ref, ld_ref, br_ref, bi_ref, ar_ref, ai_ref, bbr_ref, bbi_ref):
        lr, li = lr_ref[...], li_ref[...]
        dt = jnp.exp(ld_ref[...])
        mag = jnp.exp(lr * dt)
        a_re, a_im = mag * jnp.cos(li * dt), mag * jnp.sin(li * dt)
        nr, ni = a_re - 1.0, a_im
        den = lr * lr + li * li
        f_re, f_im = (nr * lr + ni * li) / den, (ni * lr - nr * li) / den
        br, bi = br_ref[...], bi_ref[...]
        ar_ref[...], ai_ref[...] = a_re, a_im
        bbr_ref[...] = f_re * br - f_im * bi
        bbi_ref[...] = f_re * bi + f_im * br

    s_a, s_b = jax.ShapeDtypeStruct(lam_re.shape, F32), jax.ShapeDtypeStruct(b_re.shape, F32)
    return pl.pallas_call(body, name=name, out_shape=[s_a, s_a, s_b, s_b])(lam_re, lam_im, log_dt, b_re, b_im)


def _s5_disc_bwd(name, lam_re, lam_im, log_dt, b_re, b_im, dab_re, dab_im, dbb_re, dbb_im):
    def body(lr_ref, li_ref, ld_ref, br_ref, bi_ref, dar_ref, dai_ref, dbbr_ref, dbbi_ref,
             dlr_ref, dli_ref, dld_ref, dbr_ref, dbi_ref):
        lr, li = lr_ref[...], li_ref[...]
        dt = jnp.exp(ld_ref[...])
        mag = jnp.exp(lr * dt)
        a_re, a_im = mag * jnp.cos(li * dt), mag * jnp.sin(li * dt)
        nr, ni = a_re - 1.0, a_im
        den = lr * lr + li * li
        f_re, f_im = (nr * lr + ni * li) / den, (ni * lr - nr * li) / den
        br, bi = br_ref[...], bi_ref[...]
        dbbr, dbbi = dbbr_ref[...], dbbi_ref[...]
        dbr_ref[...] = f_re * dbbr + f_im * dbbi
        dbi_ref[...] = f_re * dbbi - f_im * dbbr
        df_re = jnp.sum(dbbr * br + dbbi * bi, axis=1, keepdims=True)
        df_im = jnp.sum(dbbi * br - dbbr * bi, axis=1, keepdims=True)
        dnr = (df_re * lr - df_im * li) / den
        dni = (df_re * li + df_im * lr) / den
        dden = -(df_re * f_re + df_im * f_im) / den
        dlr = (df_re * nr + df_im * ni) / den + 2.0 * lr * dden
        dli = (df_re * ni - df_im * nr) / den + 2.0 * li * dden
        da_re, da_im = dar_ref[...] + dnr, dai_ref[...] + dni
        dmag_mag = da_re * a_re + da_im * a_im
        dth = da_im * a_re - da_re * a_im
        dlr_ref[...] = dlr + dmag_mag * dt
        dli_ref[...] = dli + dth * dt
        ddt = jnp.sum(dmag_mag * lr + dth * li, axis=2, keepdims=True)
        dld_ref[...] = ddt * dt

    s_a, s_b = jax.ShapeDtypeStruct(lam_re.shape, F32), jax.ShapeDtypeStruct(b_re.shape, F32)
    return pl.pallas_call(
        body, name=name, out_shape=[s_a, s_a, jax.ShapeDtypeStruct(log_dt.shape, F32), s_b, s_b],
    )(lam_re, lam_im, log_dt, b_re, b_im, dab_re, dab_im, dbb_re, dbb_im)


def _s5_time_block(t):
    return _tile(t, 128, SUBLANE)


def _s5_scan_fwd(name, u, bb_re, bb_im, ab_re, ab_im, rider=None):
    t, d = u.shape
    nsg, cs, ns = bb_re.shape
    tb = _s5_time_block(t)

    def body(u_ref, bbr_hbm, bbi_hbm, ar_ref, ai_ref, sr_ref, si_ref, srm_ref, sim_ref, bbr, bbi, cr_ref, ci_ref):
        @pl.when(pl.program_id(0) == 0)
        def _():
            pltpu.sync_copy(bbr_hbm, bbr)
            pltpu.sync_copy(bbi_hbm, bbi)
            cr_ref[...] = jnp.zeros_like(cr_ref)
            ci_ref[...] = jnp.zeros_like(ci_ref)

        for sg in range(nsg):
            us = u_ref[:, sg * cs:(sg + 1) * cs]
            sr_ref[:, sg, :] = _dot(us, bbr[sg], NN)
            si_ref[:, sg, :] = _dot(us, bbi[sg], NN)
        ar, ai = ar_ref[...], ai_ref[...]

        def step(i, carry):
            cr, ci = carry
            nr = ar * cr - ai * ci + sr_ref[i]
            ni = ar * ci + ai * cr + si_ref[i]
            sr_ref[i] = nr
            si_ref[i] = ni
            return nr, ni

        cr, ci = lax.fori_loop(0, tb, step, (cr_ref[...], ci_ref[...]), unroll=2)
        cr_ref[...], ci_ref[...] = cr, ci
        srm_ref[...] = jnp.swapaxes(sr_ref[...], 0, 1).astype(MXU_DTYPE)
        sim_ref[...] = jnp.swapaxes(si_ref[...], 0, 1).astype(MXU_DTYPE)

    scan = jax.ShapeDtypeStruct((t, nsg, ns), F32)
    mxu = jax.ShapeDtypeStruct((nsg, t, ns), MXU_DTYPE)
    hbm = pl.BlockSpec(memory_space=pltpu.HBM)
    full = pl.BlockSpec((nsg, ns), lambda i: (0, 0))
    return _call(
        body, name=name, out_shape=[scan, scan, mxu, mxu], grid=(t // tb,),
        in_specs=[_row_spec(tb, d), hbm, hbm, full, full],
        out_specs=[pl.BlockSpec((tb, nsg, ns), lambda i: (i, 0, 0))] * 2 + [pl.BlockSpec((nsg, tb, ns), lambda i: (0, i, 0))] * 2,
        scratch_shapes=[pltpu.VMEM(bb_re.shape, bb_re.dtype), pltpu.VMEM(bb_im.shape, bb_im.dtype),
                        pltpu.VMEM((nsg, ns), F32), pltpu.VMEM((nsg, ns), F32)],
        args=(u, bb_re, bb_im, ab_re, ab_im), rider=rider)


def _s5_out_fwd(name, s_re, s_im, cc_re, cc_im, u, dskip):
    nsg, t, ns = s_re.shape
    d = u.shape[1]
    cs = cc_re.shape[2]
    tb = _tile(t, 512, SUBLANE)

    def body(sr_ref, si_ref, cr_ref, ci_ref, u_ref, d_ref, yp_ref, ya_ref):
        y = _dot(sr_ref[...], cr_ref[...], NN) - _dot(si_ref[...], ci_ref[...], NN) + d_ref[...] * u_ref[...]
        yp_ref[...] = y
        ya_ref[...] = _gelu(y).astype(ya_ref.dtype)

    s_spec = pl.BlockSpec((None, tb, ns), lambda sg, i: (sg, i, 0))
    c_spec = pl.BlockSpec((None, ns, cs), lambda sg, i: (sg, 0, 0))
    col = pl.BlockSpec((tb, cs), lambda sg, i: (i, sg))
    return pl.pallas_call(
        body, name=name, out_shape=[jax.ShapeDtypeStruct((t, d), F32), jax.ShapeDtypeStruct((t, d), MXU_DTYPE)],
        grid=(nsg, t // tb), in_specs=[s_spec, s_spec, c_spec, c_spec, col, pl.BlockSpec((1, cs), lambda sg, i: (0, sg))],
        out_specs=[col, col], compiler_params=_params(2))(s_re, s_im, cc_re, cc_im, u, dskip)


def _gelu_bwd(name, dy, ypre):
    t, d = dy.shape
    bm = _tile(t, 256, SUBLANE)

    def body(dy_ref, yp_ref, o_ref):
        o_ref[...] = (dy_ref[...] * _gelu_and_grad(yp_ref[...])[1]).astype(o_ref.dtype)

    return pl.pallas_call(
        body, name=name, out_shape=jax.ShapeDtypeStruct((t, d), MXU_DTYPE), grid=(t // bm,),
        in_specs=[_row_spec(bm, d), _row_spec(bm, d)], out_specs=_row_spec(bm, d), compiler_params=_params(1))(dy, ypre)


def _s5_scan_bwd(name, dyp, cc_re, cc_im, ab_re, ab_im, s_re, s_im, rider=None):
    t, d = dyp.shape
    nsg, ns, cs = cc_re.shape
    tb = _s5_time_block(t)
    nb = t // tb

    def body(dy_ref, ccr_hbm, cci_hbm, ar_ref, ai_ref, sr_ref, si_ref, lrm_ref, lim_ref, dar_ref, dai_ref,
             ccr, cci, lr_ref, li_ref, cr_ref, ci_ref):
        @pl.when(pl.program_id(0) == 0)
        def _():
            pltpu.sync_copy(ccr_hbm, ccr)
            pltpu.sync_copy(cci_hbm, cci)
            cr_ref[...] = jnp.zeros_like(cr_ref)
            ci_ref[...] = jnp.zeros_like(ci_ref)
            dar_ref[...] = jnp.zeros_like(dar_ref)
            dai_ref[...] = jnp.zeros_like(dai_ref)

        for sg in range(nsg):
            dys = dy_ref[:, sg * cs:(sg + 1) * cs]
            lr_ref[:, sg, :] = _dot(dys, ccr[sg], NT)
            li_ref[:, sg, :] = -_dot(dys, cci[sg], NT)
        ar, ai = ar_ref[...], ai_ref[...]

        def step(i, carry):
            cr, ci, dar, dai = carry
            j = tb - 1 - i
            sr, si = sr_ref[j], si_ref[j]
            dar = dar + (cr * sr + ci * si)
            dai = dai + (ci * sr - cr * si)
            nr = lr_ref[j] + (ar * cr + ai * ci)
            ni = li_ref[j] + (ar * ci - ai * cr)
            lr_ref[j] = nr
            li_ref[j] = ni
            return nr, ni, dar, dai

        cr, ci, dar, dai = lax.fori_loop(0, tb, step, (cr_ref[...], ci_ref[...], dar_ref[...], dai_ref[...]))
        cr_ref[...], ci_ref[...] = cr, ci
        dar_ref[...], dai_ref[...] = dar, dai
        lrm_ref[...] = jnp.swapaxes(lr_ref[...], 0, 1).astype(MXU_DTYPE)
        lim_ref[...] = jnp.swapaxes(li_ref[...], 0, 1).astype(MXU_DTYPE)

    hbm = pl.BlockSpec(memory_space=pltpu.HBM)
    full = pl.BlockSpec((nsg, ns), lambda i: (0, 0))
    mxu = jax.ShapeDtypeStruct((nsg, t, ns), MXU_DTYPE)
    acc = jax.ShapeDtypeStruct((nsg, ns), F32)
    scan_spec = pl.BlockSpec((tb, nsg, ns), lambda i: (nb - 1 - i, 0, 0))
    return _call(
        body, name=name, out_shape=[mxu, mxu, acc, acc], grid=(nb,),
        in_specs=[pl.BlockSpec((tb, d), lambda i: (nb - 1 - i, 0)), hbm, hbm, full, full, scan_spec, scan_spec],
        out_specs=[pl.BlockSpec((nsg, tb, ns), lambda i: (0, nb - 1 - i, 0))] * 2 + [full, full],
        scratch_shapes=[pltpu.VMEM(cc_re.shape, cc_re.dtype), pltpu.VMEM(cc_im.shape, cc_im.dtype),
                        pltpu.VMEM((tb, nsg, ns), F32), pltpu.VMEM((tb, nsg, ns), F32),
                        pltpu.VMEM((nsg, ns), F32), pltpu.VMEM((nsg, ns), F32)],
        args=(dyp, cc_re, cc_im, ab_re, ab_im, s_re, s_im), rider=rider)


def _s5_grads(name, lam_re, lam_im, s_re, s_im, u, dyp, bb_re, bb_im, dskip, rider=None):
    nsg, t, ns = lam_re.shape
    d = u.shape[1]
    cs = bb_re.shape[1]
    tb = _tile(t, 512, SUBLANE)

    def body(lr_ref, li_ref, sr_ref, si_ref, u_ref, dy_ref, bbr_ref, bbi_ref, d_ref,
             du_ref, dbbr_ref, dbbi_ref, dccr_ref, dcci_ref, dd_ref):
        @pl.when(pl.program_id(1) == 0)
        def _():
            for r in (dbbr_ref, dbbi_ref, dccr_ref, dcci_ref, dd_ref):
                r[...] = jnp.zeros_like(r)

        lr, li, uv, dy = lr_ref[...], li_ref[...], u_ref[...], dy_ref[...]
        dyf = dy.astype(F32)
        du_ref[...] = _dot(lr, bbr_ref[...], NT) + _dot(li, bbi_ref[...], NT) + d_ref[...] * dyf
        dbbr_ref[...] += _dot(uv, lr, TN)
        dbbi_ref[...] += _dot(uv, li, TN)
        dccr_ref[...] += _dot(sr_ref[...], dy, TN)
        dcci_ref[...] -= _dot(si_ref[...], dy, TN)
        dd_ref[...] += jnp.sum(dyf * uv, axis=0, keepdims=True)

    s_spec = pl.BlockSpec((None, tb, ns), lambda sg, i: (sg, i, 0))
    col = pl.BlockSpec((tb, cs), lambda sg, i: (i, sg))
    b_spec = pl.BlockSpec((None, cs, ns), lambda sg, i: (sg, 0, 0))
    c_spec = pl.BlockSpec((None, ns, cs), lambda sg, i: (sg, 0, 0))
    vec = pl.BlockSpec((1, cs), lambda sg, i: (0, sg))
    return _call(
        body, name=name,
        out_shape=[jax.ShapeDtypeStruct((t, d), F32), jax.ShapeDtypeStruct(bb_re.shape, F32),
                   jax.ShapeDtypeStruct(bb_re.shape, F32), jax.ShapeDtypeStruct((nsg, ns, cs), F32),
                   jax.ShapeDtypeStruct((nsg, ns, cs), F32), jax.ShapeDtypeStruct((1, d), F32)],
        grid=(nsg, t // tb), in_specs=[s_spec, s_spec, s_spec, s_spec, col, col, b_spec, b_spec, vec],
        out_specs=[col, b_spec, b_spec, c_spec, c_spec, vec],
        args=(lam_re, lam_im, s_re, s_im, u, dyp, bb_re, bb_im, dskip), rider=rider)


def _shift_down(x, k, prev8):
    if k == 0:
        return x
    ext = jnp.concatenate([prev8, x], axis=0)
    return ext[SUBLANE - k:SUBLANE - k + x.shape[0]]


def _shift_up(x, k, next8):
    if k == 0:
        return x
    ext = jnp.concatenate([x, next8], axis=0)
    return ext[k:k + x.shape[0]]


def _lru_time_block(t):
    return _tile(t, 256, SUBLANE)


def _lru_gates(xp, prev8, cv_ref, wrg, wig):
    taps = cv_ref.shape[0] - 4
    row = lambda k: cv_ref[k:k + 1, :]
    xs = [_shift_down(xp, taps - 1 - k, prev8) for k in range(taps)]
    xb = row(taps)
    for k in range(taps):
        xb = xb + row(k) * xs[k]
    r = jax.nn.sigmoid(_dot(xb, wrg, NN) + row(taps + 1))
    ig = jax.nn.sigmoid(_dot(xb, wig, NN) + row(taps + 2))
    sp = jax.nn.softplus(-row(taps + 3))
    log_a = -LRU_C * r * sp
    a = jnp.exp(log_a)
    mult = jnp.sqrt(_neg_expm1(2.0 * log_a))
    return xs, xb, r, ig, sp, a, mult


def _lru_fwd(name, zz, cvec, wrg, wig, rider=None):
    _, t, w = zz.shape
    half = N_DEV // 2
    tb = _lru_time_block(t)

    def body(gb_ref, xp_ref, xprev_ref, cv_ref, wrg_ref, wig_ref, hs_ref, y_ref, a_scr, b_scr, carry):
        i = pl.program_id(1)

        @pl.when(i == 0)
        def _():
            carry[...] = jnp.zeros_like(carry)

        prev8 = jnp.where(i > 0, xprev_ref[...], 0.0)
        _, xb, _, ig, _, a, mult = _lru_gates(xp_ref[...], prev8, cv_ref, wrg_ref[...], wig_ref[...])
        a_scr[...] = a
        b_scr[...] = mult * (ig * xb)

        def step(j, h):
            h = a_scr[pl.ds(j, 1), :] * h + b_scr[pl.ds(j, 1), :]
            hs_ref[pl.ds(j, 1), :] = h
            return h

        carry[0:1, :] = lax.fori_loop(0, tb, step, carry[0:1, :], unroll=8)
        y_ref[...] = (hs_ref[...] * _gelu(gb_ref[...])).astype(y_ref.dtype)

    nrow = tb // SUBLANE
    blk = lambda off: pl.BlockSpec((None, tb, w), lambda q, i: (q + off, i, 0))
    return _call(
        body, name=name,
        out_shape=[jax.ShapeDtypeStruct((half, t, w), F32), jax.ShapeDtypeStruct((half, t, w), MXU_DTYPE)],
        grid=(half, t // tb),
        in_specs=[blk(0), blk(half),
                  pl.BlockSpec((None, SUBLANE, w), lambda q, i: (q + half, jnp.maximum(i * nrow - 1, 0), 0)),
                  pl.BlockSpec((None,) + cvec.shape[1:], lambda q, i: (q, 0, 0)),
                  pl.BlockSpec((None, w, w), lambda q, i: (q, 0, 0)), pl.BlockSpec((None, w, w), lambda q, i: (q, 0, 0))],
        out_specs=[blk(0), blk(0)],
        scratch_shapes=[pltpu.VMEM((tb, w), F32), pltpu.VMEM((tb, w), F32), pltpu.VMEM((SUBLANE, w), F32)],
        args=(zz, zz, zz, cvec, wrg, wig), rider=rider)


def _lru_bwd(name, zz, hs, dy, cvec, wrg, wig, rider=None):
    _, t, w = zz.shape
    half = N_DEV // 2
    tb = _lru_time_block(t)
    nb = t // tb
    taps = cvec.shape[1] - 4

    def body(gb_ref, xp_ref, xprev_ref, hs_ref, hprev_ref, dy_ref, cv_ref, wrg_ref, wig_ref,
             dgb_ref, dxp_ref, dcv_ref, dwrg_ref, dwig_ref, a_scr, l_scr, carry, dxb_next):
        i = pl.program_id(1)

        @pl.when(i == 0)
        def _():
            for r_ in (carry, dxb_next, dcv_ref, dwrg_ref, dwig_ref):
                r_[...] = jnp.zeros_like(r_)

        has_prev = i < nb - 1
        row = lambda k: cv_ref[k:k + 1, :]
        prev8 = jnp.where(has_prev, xprev_ref[...], 0.0)
        xs, xb, r, ig, sp, a, mult = _lru_gates(xp_ref[...], prev8, cv_ref, wrg_ref[...], wig_ref[...])
        hs_ = hs_ref[...]
        hs_m1 = _shift_down(hs_, 1, jnp.where(has_prev, hprev_ref[...], 0.0))
        gel, dgel = _gelu_and_grad(gb_ref[...])
        dy_ = dy_ref[...]
        dgb_ref[...] = (dy_ * hs_ * dgel).astype(dgb_ref.dtype)
        a_scr[...] = a
        l_scr[...] = dy_ * gel

        def step(k, c):
            j = tb - 1 - k
            lam = l_scr[pl.ds(j, 1), :] + c
            l_scr[pl.ds(j, 1), :] = lam
            return a_scr[pl.ds(j, 1), :] * lam

        carry[0:1, :] = lax.fori_loop(0, tb, step, carry[0:1, :], unroll=8)
        lam = l_scr[...]
        dmult = lam * (ig * xb)
        dig = lam * (mult * xb)
        dxb = lam * (mult * ig)
        dlog_a = (lam * hs_m1) * a - dmult * (a * a) / mult
        dr = dlog_a * (-LRU_C * sp)
        dsp = jnp.sum(dlog_a * (-LRU_C * r), axis=0, keepdims=True)
        dpr = dr * (r * (1.0 - r))
        dpi = dig * (ig * (1.0 - ig))
        dwrg_ref[...] += _dot(xb, dpr, TN)
        dwig_ref[...] += _dot(xb, dpi, TN)
        dxb = dxb + _dot(dpr, wrg_ref[...], NT) + _dot(dpi, wig_ref[...], NT)
        for k in range(taps):
            dcv_ref[k:k + 1, :] += jnp.sum(dxb * xs[k], axis=0, keepdims=True)
        dcv_ref[taps:taps + 1, :] += jnp.sum(dxb, axis=0, keepdims=True)
        dcv_ref[taps + 1:taps + 2, :] += jnp.sum(dpr, axis=0, keepdims=True)
        dcv_ref[taps + 2:taps + 3, :] += jnp.sum(dpi, axis=0, keepdims=True)
        dcv_ref[taps + 3:taps + 4, :] += dsp * (-jax.nn.sigmoid(-row(taps + 3)))
        nxt8 = dxb_next[...]
        dxp = row(taps - 1) * dxb
        for k in range(taps - 1):
            dxp = dxp + row(k) * _shift_up(dxb, taps - 1 - k, nxt8)
        dxp_ref[...] = dxp.astype(dxp_ref.dtype)
        dxb_next[...] = dxb[0:SUBLANE]

    nrow = tb // SUBLANE
    blk = lambda off: pl.BlockSpec((None, tb, w), lambda q, i: (q + off, nb - 1 - i, 0))
    halo = lambda off: pl.BlockSpec((None, SUBLANE, w), lambda q, i: (q + off, jnp.maximum((nb - 1 - i) * nrow - 1, 0), 0))
    wspec = pl.BlockSpec((None, w, w), lambda q, i: (q, 0, 0))
    cspec = pl.BlockSpec((None,) + cvec.shape[1:], lambda q, i: (q, 0, 0))
    act = jax.ShapeDtypeStruct((half, t, w), MXU_DTYPE)
    return _call(
        body, name=name,
        out_shape=[act, act, jax.ShapeDtypeStruct(cvec.shape, F32), jax.ShapeDtypeStruct(wrg.shape, F32),
                   jax.ShapeDtypeStruct(wig.shape, F32)],
        grid=(half, nb),
        in_specs=[blk(0), blk(half), halo(half), blk(0), halo(0), blk(0), cspec, wspec, wspec],
        out_specs=[blk(0), blk(0), cspec, wspec, wspec],
        scratch_shapes=[pltpu.VMEM((tb, w), F32), pltpu.VMEM((tb, w), F32), pltpu.VMEM((SUBLANE, w), F32),
                        pltpu.VMEM((SUBLANE, w), F32)],
        args=(zz, zz, zz, hs, hs, dy, cvec, wrg, wig), rider=rider)


def _band(name, blocks, per, dtype):
    n, a, b = blocks.shape

    def body(x_ref, o_ref):
        o_ref[...] = jnp.zeros_like(o_ref)
        for g in range(per):
            o_ref[g * a:(g + 1) * a, g * b:(g + 1) * b] = x_ref[g].astype(o_ref.dtype)

    return pl.pallas_call(
        body, name=name, out_shape=jax.ShapeDtypeStruct((n // per, per * a, per * b), dtype), grid=(n // per,),
        in_specs=[pl.BlockSpec((per, a, b), lambda s: (s, 0, 0))],
        out_specs=pl.BlockSpec((None, per * a, per * b), lambda s: (s, 0, 0)), compiler_params=_params(1))(blocks)


def _unband(name, bands, per):
    s, pa, pb = bands.shape
    a, b = pa // per, pb // per

    def body(x_ref, o_ref):
        for g in range(per):
            o_ref[g] = x_ref[g * a:(g + 1) * a, g * b:(g + 1) * b]

    return pl.pallas_call(
        body, name=name, out_shape=jax.ShapeDtypeStruct((s * per, a, b), bands.dtype), grid=(s,),
        in_specs=[pl.BlockSpec((None, pa, pb), lambda i: (i, 0, 0))],
        out_specs=pl.BlockSpec((per, a, b), lambda i: (i, 0, 0)), compiler_params=_params(1))(bands)


def _pack(arrays, rows_multiple, lanes=LANE):
    flat = [a.reshape(-1).astype(F32) for a in arrays]
    size = sum(a.shape[0] for a in flat)
    rows = -(-size // (lanes * rows_multiple)) * rows_multiple
    if rows * lanes > size:
        flat.append(jnp.zeros((rows * lanes - size,), F32))
    return jnp.concatenate(flat).reshape(rows, lanes)


def _unpack(packed, shapes):
    flat = packed.reshape(-1)
    out, off = [], 0
    for s in shapes:
        n = math.prod(s)
        out.append(flat[off:off + n].reshape(s))
        off += n
    return out


def kernel(x, c, norm_g, w_ada, b_ada, s5_w_in, s5_lam_re, s5_lam_im, s5_log_dt, s5_b_re, s5_b_im, s5_c_re, s5_c_im, s5_d, s5_w_glu, lru_w_in, lru_conv_w, lru_conv_b, lru_w_rg, lru_b_rg, lru_w_ig, lru_b_ig, lru_lam, lru_w_out, ffn_w_gu, ffn_w_down, final_g, loss_target, m_norm_g, m_w_ada, m_b_ada, m_s5_w_in, m_s5_lam_re, m_s5_lam_im, m_s5_log_dt, m_s5_b_re, m_s5_b_im, m_s5_c_re, m_s5_c_im, m_s5_d, m_s5_w_glu, m_lru_w_in, m_lru_conv_w, m_lru_conv_b, m_lru_w_rg, m_lru_b_rg, m_lru_w_ig, m_lru_b_ig, m_lru_lam, m_lru_w_out, m_ffn_w_gu, m_ffn_w_down, m_final_g, v_norm_g, v_w_ada, v_b_ada, v_s5_w_in, v_s5_lam_re, v_s5_lam_im, v_s5_log_dt, v_s5_b_re, v_s5_b_im, v_s5_c_re, v_s5_c_im, v_s5_d, v_s5_w_glu, v_lru_w_in, v_lru_conv_w, v_lru_conv_b, v_lru_w_rg, v_lru_b_rg, v_lru_w_ig, v_lru_b_ig, v_lru_lam, v_lru_w_out, v_ffn_w_gu, v_ffn_w_down, v_final_g):
    wv = dict(zip(WEIGHTS, (norm_g, w_ada, b_ada, s5_w_in, s5_lam_re, s5_lam_im, s5_log_dt, s5_b_re, s5_b_im, s5_c_re, s5_c_im, s5_d, s5_w_glu, lru_w_in, lru_conv_w, lru_conv_b, lru_w_rg, lru_b_rg, lru_w_ig, lru_b_ig, lru_lam, lru_w_out, ffn_w_gu, ffn_w_down, final_g)))
    mv = dict(zip(WEIGHTS, (m_norm_g, m_w_ada, m_b_ada, m_s5_w_in, m_s5_lam_re, m_s5_lam_im, m_s5_log_dt, m_s5_b_re, m_s5_b_im, m_s5_c_re, m_s5_c_im, m_s5_d, m_s5_w_glu, m_lru_w_in, m_lru_conv_w, m_lru_conv_b, m_lru_w_rg, m_lru_b_rg, m_lru_w_ig, m_lru_b_ig, m_lru_lam, m_lru_w_out, m_ffn_w_gu, m_ffn_w_down, m_final_g)))
    vv = dict(zip(WEIGHTS, (v_norm_g, v_w_ada, v_b_ada, v_s5_w_in, v_s5_lam_re, v_s5_lam_im, v_s5_log_dt, v_s5_b_re, v_s5_b_im, v_s5_c_re, v_s5_c_im, v_s5_d, v_s5_w_glu, v_lru_w_in, v_lru_conv_w, v_lru_conv_b, v_lru_w_rg, v_lru_b_rg, v_lru_w_ig, v_lru_b_ig, v_lru_lam, v_lru_w_out, v_ffn_w_gu, v_ffn_w_down, v_final_g)))

    me = 4 * lax.axis_index("x") + 2 * lax.axis_index("y") + lax.axis_index("c")
    x0 = x[0]
    tgt = loss_target[0]
    t, d = x0.shape
    depth = norm_g.shape[0]
    n_mod = w_ada.shape[2] * N_DEV // d
    groups, states = s5_lam_re.shape[1], s5_lam_re.shape[2]
    per_sg = S5_SUPER // S5_GROUP
    nsg = groups // per_sg
    lw = lru_lam.shape[1] * N_DEV
    lwc = lw // (N_DEV // 2)
    half = N_DEV // 2

    assert depth == 2, "the ride schedule below is written for one S5 layer followed by one RG-LRU layer"
    wire = lambda a: a.astype(WIRE_DTYPE)
    gw = {'s5_in': _all_gather("ag_s5_w_in", wire(s5_w_in[0]))}

    def riding(job, fn, *args):
        res, (got,) = fn(*args, rider=_gather_rider([wire(job[1])]))
        gw[job[0]] = got
        return res

    sh_shapes = [wv[n].shape for n in SMALL_SHARDED] + [c.shape]
    sh_all = _all_gather("ag_small", _pack([wv[n] for n in SMALL_SHARDED] + [c], SUBLANE))
    sh_parts = [jnp.stack(p) for p in zip(*[_unpack(sh_all[s], sh_shapes) for s in range(N_DEV)])]
    full = {}
    for n, p in zip(SMALL_SHARDED, sh_parts[:-1]):
        full[n] = jnp.moveaxis(p, 0, -2).reshape(p.shape[1:-1] + (-1,))
    c_all = sh_parts[-1].reshape(N_DEV, d)
    c16 = jnp.pad(c_all, ((0, 2 * SUBLANE - N_DEV), (0, 0)))

    n_loc = w_ada.shape[2]
    b_loc = lax.dynamic_slice_in_dim(b_ada, me * n_loc, n_loc, axis=1)[:, None, :]
    mod_part = _ada_fwd("ada_fwd", c16, w_ada, b_loc)[:, :N_DEV]
    mod_mine = _chunk_exchange("x_mod", [mod_part.transpose(1, 0, 2)], ALL)
    mod = mod_mine.transpose(1, 0, 2).reshape(depth, n_mod, 1, d)

    lam3 = lambda a: a[0][:, None, :]
    p_lr, p_li, p_ld = lam3(s5_lam_re), lam3(s5_lam_im), s5_log_dt[0][:, None, None]
    p_br, p_bi = s5_b_re[0].transpose(0, 2, 1), s5_b_im[0].transpose(0, 2, 1)
    ab_re3, ab_im3, bb_re3, bb_im3 = _s5_disc("s5_disc", p_lr, p_li, p_ld, p_br, p_bi)
    ab_re, ab_im = ab_re3.reshape(nsg, per_sg * states), ab_im3.reshape(nsg, per_sg * states)
    bb_re = _band("band_bb_re", bb_re3, per_sg, MXU_DTYPE)
    bb_im = _band("band_bb_im", bb_im3, per_sg, MXU_DTYPE)
    cc_re = _band("band_cc_re", s5_c_re[0].transpose(0, 2, 1), per_sg, MXU_DTYPE)
    cc_im = _band("band_cc_im", s5_c_im[0].transpose(0, 2, 1), per_sg, MXU_DTYPE)

    taps = lru_conv_w.shape[1]
    cvec = jnp.concatenate([full['lru_conv_w'].reshape(taps, lw), full['lru_conv_b'], full['lru_b_rg'],
                            full['lru_b_ig'], full['lru_lam']], axis=0)
    cvec = cvec.reshape(taps + 4, half, lwc).transpose(1, 0, 2)
    wrg = _band("band_w_rg", lru_w_rg[0], LRU_BLOCKS_PER_CHUNK, MXU_DTYPE)
    wig = _band("band_w_ig", lru_w_ig[0], LRU_BLOCKS_PER_CHUNK, MXU_DTYPE)

    saved = []
    xc = x0
    for i in range(depth):
        sh1, sc1, g1, sh2, sc2, g2 = [mod[i, k] for k in range(n_mod)]
        gn = full['norm_g'][i]
        h1 = _norm_mod_fwd(f"norm1_fwd{i}", xc, gn[0:1], sc1, sh1)
        if i % 2 == 0:
            u = riding(('s5_glu', s5_w_glu[0]), _mm_row, f"s5_in{i}", h1[None], gw['s5_in'].reshape(d, d))
            s_re, s_im, s_rem, s_imm = riding((('gu', i), ffn_w_gu[i]), _s5_scan_fwd, f"s5_scan{i}", u, bb_re, bb_im,
                                              ab_re, ab_im)
            ypre, yact = _s5_out_fwd(f"s5_out{i}", s_rem, s_imm, cc_re, cc_im, u, s5_d)
            z = riding((('down', i), ffn_w_down[i]), _mm_col, f"s5_glu{i}", yact, gw['s5_glu'])
            x1 = _glu_resid_fwd(f"s5_resid{i}", z, xc, g1)
            mix = (u, s_re, s_im, s_rem, s_imm, ypre, yact, z)
        else:
            zz = _mm_col(f"lru_in{i}", h1, gw['lru_in'])
            hs, ylru = riding((('gu', i), ffn_w_gu[i]), _lru_fwd, f"lru_core{i}", zz, cvec, wrg, wig)
            o = _mm_row(f"lru_out{i}", ylru, gw['lru_out'].reshape(lw, d))
            x1 = _resid(f"lru_resid{i}", xc, o, g1)
            mix = (zz, hs, ylru, o)
        h2 = _norm_mod_fwd(f"norm2_fwd{i}", x1, gn[1:2], sc2, sh2)
        if i % 2 == 0:
            gu = riding(('lru_in', lru_w_in[0]), _mm_col, f"ffn_gu{i}", h2, gw['gu', i], MXU_DTYPE)
            act = _swiglu_act_fwd(f"ffn_act{i}", gu)
            f = riding(('lru_out', lru_w_out[0]), _mm_row, f"ffn_down{i}", act, gw['down', i].reshape(-1, d))
        else:
            gu = riding((('down', i), ffn_w_down[i]), _mm_col, f"ffn_gu{i}", h2, gw['gu', i], MXU_DTYPE)
            act = _swiglu_act_fwd(f"ffn_act{i}", gu)
            f = _mm_row(f"ffn_down{i}", act, gw['down', i].reshape(-1, d))
        x2 = _resid(f"ffn_resid{i}", x1, f, g2)
        saved.append((xc, h1, mix, x1, h2, gu, act, f))
        xc = x2

    dx, loss_part, d_final_g = _loss_bwd("loss", xc, tgt, final_g[None])
    loss = lax.psum(loss_part[0, 0], ("x", "y", "c"))

    grads = {}
    parts = {}
    dmod = [None] * depth
    d_norm_g = [None] * depth
    core = lax.axis_index("c").astype(jnp.int32).reshape(1)
    chunked = lambda p: p.reshape(N_DEV, -1, p.shape[-1])
    to_sibling = lambda p: _sibling_rider(chunked(p))
    pair = lambda name, p, got: _pair_sum(name, chunked(p), got, core)
    over_ici = lambda sums: _chunk_rider([sums], SAME_CORE)

    above = None
    for i in reversed(range(depth)):
        xin, h1, mix, x1, h2, gu, act, f = saved[i]
        sh1, sc1, g1, sh2, sc2, g2 = [mod[i, k] for k in range(n_mod)]
        gn = full['norm_g'][i]
        g_down = gw['down', i].reshape(-1, d)
        df, dg2 = _gate_bwd(f"ffn_gate_bwd{i}", dx, f, g2)
        if above is None:
            dact = _mm_row_da(f"ffn_down_da{i}", df, g_down, half)
        else:
            dact, (got,) = _ride(_mm_row_da, f"ffn_down_da{i}", df, g_down, half, riders=[to_sibling(above[1])])
            s_above = pair(f"x_{above[0][0]}_pair", above[1], got)
        p_down = _mm_row_db(f"ffn_down_db{i}", act, df, WIRE_DTYPE)
        dgu, (got,) = _ride(_swiglu_act_bwd, f"ffn_act_bwd{i}", gu, dact, riders=[to_sibling(p_down)])
        s_down = pair(f"x_ffn_w_down{i}_pair", p_down, got)
        if above is None:
            dh2, (parts['ffn_w_down', i],) = _ride(_mm_col_da, f"ffn_gu_da{i}", dgu, gw['gu', i], riders=[over_ici(s_down)])
            p_gu = _mm_col_db(f"ffn_gu_db{i}", h2, dgu, WIRE_DTYPE)
        else:
            dh2, (parts[above[0]],) = _ride(_mm_col_da, f"ffn_gu_da{i}", dgu, gw['gu', i], riders=[over_ici(s_above)])
            p_gu, (parts['ffn_w_down', i],) = _ride(_mm_col_db, f"ffn_gu_db{i}", h2, dgu, WIRE_DTYPE,
                                                    riders=[over_ici(s_down)])
        dx, dgn2, dsc2, dsh2 = _norm_mod_bwd(f"norm2_bwd{i}", x1, dh2, dx, gn[1:2], sc2)
        if i % 2 == 0:
            u, s_re, s_im, s_rem, s_imm, ypre, yact, z = mix
            dz, dg1 = _glu_resid_bwd(f"s5_resid_bwd{i}", z, dx, g1)
            dyact, (got,) = _ride(_mm_col_da, f"s5_glu_da{i}", dz, gw['s5_glu'], riders=[to_sibling(p_gu)])
            s_gu = pair(f"x_ffn_w_gu{i}_pair", p_gu, got)
            p_glu = _mm_col_db(f"s5_glu_db{i}", yact, dz, WIRE_DTYPE)
            dyp = _gelu_bwd(f"s5_gelu_bwd{i}", dyact, ypre)
            (l_rem, l_imm, dab_re, dab_im), (parts['ffn_w_gu', i], got) = _ride(
                _s5_scan_bwd, f"s5_scan_bwd{i}", dyp, cc_re, cc_im, ab_re, ab_im, s_re, s_im,
                riders=[over_ici(s_gu), to_sibling(p_glu)])
            s_glu = pair("x_s5_w_glu_pair", p_glu, got)
            (du, dbb_re, dbb_im, dcc_re, dcc_im, dd), (parts['s5_w_glu', 0],) = _ride(
                _s5_grads, f"s5_grads{i}", l_rem, l_imm, s_rem, s_imm, u, dyp, bb_re, bb_im, s5_d, riders=[over_ici(s_glu)])
            dlr, dli, dld, dbr, dbi = _s5_disc_bwd(
                "s5_disc_bwd", p_lr, p_li, p_ld, p_br, p_bi, dab_re.reshape(groups, 1, states),
                dab_im.reshape(groups, 1, states), _unband("unband_bb_re", dbb_re, per_sg),
                _unband("unband_bb_im", dbb_im, per_sg))
            grads['s5_lam_re'], grads['s5_lam_im'], grads['s5_log_dt'] = dlr[:, 0][None], dli[:, 0][None], dld[:, 0, 0][None]
            grads['s5_b_re'], grads['s5_b_im'] = dbr.transpose(0, 2, 1)[None], dbi.transpose(0, 2, 1)[None]
            grads['s5_c_re'] = _unband("unband_cc_re", dcc_re, per_sg).transpose(0, 2, 1)[None]
            grads['s5_c_im'] = _unband("unband_cc_im", dcc_im, per_sg).transpose(0, 2, 1)[None]
            grads['s5_d'] = dd
            dub = du.astype(MXU_DTYPE)
            p_s5_in = _mm_row_db(f"s5_in_db{i}", h1[None], dub, WIRE_DTYPE)
            dh1, (got,) = _ride(_mm_row_da, f"s5_in_da{i}", dub, gw['s5_in'].reshape(d, d), 1, riders=[to_sibling(p_s5_in)])
            dh1 = dh1[0]
            s_s5_in = pair("x_s5_w_in_pair", p_s5_in, got)
        else:
            zz, hs, ylru, o = mix
            g_lru_out = gw['lru_out'].reshape(lw, d)
            do, dg1 = _gate_bwd(f"lru_gate_bwd{i}", dx, o, g1)
            dyl, (got,) = _ride(_mm_row_da, f"lru_out_da{i}", do, g_lru_out, half, riders=[to_sibling(p_gu)])
            s_gu = pair(f"x_ffn_w_gu{i}_pair", p_gu, got)
            p_lru_out = _mm_row_db(f"lru_out_db{i}", ylru, do, WIRE_DTYPE)
            (dgb, dxp, dcv, dwrg, dwig), (parts['ffn_w_gu', i], got) = _ride(
                _lru_bwd, f"lru_core_bwd{i}", zz, hs, dyl, cvec, wrg, wig, riders=[over_ici(s_gu), to_sibling(p_lru_out)])
            s_lru_out = pair("x_lru_w_out_pair", p_lru_out, got)
            dzz = jnp.concatenate([dgb, dxp], axis=0)
            dh1, (parts['lru_w_out', 0],) = _ride(_mm_col_da, f"lru_in_da{i}", dzz, gw['lru_in'],
                                                  riders=[over_ici(s_lru_out)])
            above = (('lru_w_in', 0), _mm_col_db(f"lru_in_db{i}", h1, dzz, WIRE_DTYPE))
            dcv = dcv.transpose(1, 0, 2).reshape(taps + 4, lw)
            grads['lru_conv_w'] = dcv[:taps].reshape(1, taps, 1, lw)
            grads['lru_conv_b'], grads['lru_b_rg'] = dcv[taps:taps + 1], dcv[taps + 1:taps + 2]
            grads['lru_b_ig'], grads['lru_lam'] = dcv[taps + 2:taps + 3], dcv[taps + 3:taps + 4]
            grads['lru_w_rg'] = _unband("unband_w_rg", dwrg, LRU_BLOCKS_PER_CHUNK)[None]
            grads['lru_w_ig'] = _unband("unband_w_ig", dwig, LRU_BLOCKS_PER_CHUNK)[None]
        dx, dgn1, dsc1, dsh1 = _norm_mod_bwd(f"norm1_bwd{i}", xin, dh1, dx, gn[0:1], sc1)
        dmod[i] = jnp.concatenate([dsh1, dsc1, dg1, dsh2, dsc2, dg2], axis=1)
        d_norm_g[i] = jnp.concatenate([dgn1, dgn2], axis=0)
    grad_x = dx[None]
    dmod = jnp.concatenate(dmod, axis=0)
    grads['norm_g'] = jnp.stack(d_norm_g)
    grads['b_ada'] = dmod
    grads['final_g'] = d_final_g[0]

    small_partial = _pack([grads[n] for n in SMALL], SUBLANE * N_DEV)
    rows8 = small_partial.shape[0] // N_DEV
    small_partial = small_partial.reshape(N_DEV, rows8, LANE)
    s_small = pair("x_small_pair", small_partial, _ride_alone("x_small_d2d", _sibling_rider(small_partial)))
    parts['s5_w_in', 0], small_parts = _ride_alone("x_tail_ici", _join([over_ici(s_s5_in), over_ici(s_small)]))

    out = {}
    dmod_all = _all_gather("ag_dmod", dmod)
    dmod_loc = lax.dynamic_slice_in_dim(dmod_all, me * n_loc, n_loc, axis=2).transpose(1, 0, 2)
    dmod16 = jnp.pad(dmod_loc, ((0, 0), (0, 2 * SUBLANE - N_DEV), (0, 0)))
    out['w_ada'] = _adamw_w_ada("adamw_w_ada", c16, dmod16, w_ada, m_w_ada, v_w_ada)

    for name in BIG[1:]:
        w = wv[name]
        rows, cols = w.shape[-2] * w.shape[0], w.shape[-1]
        flat = lambda a: a.reshape(rows, cols)
        res = _adamw_sum("adamw_" + name, [parts[name, l] for l in range(w.shape[0])], flat(w), flat(mv[name]),
                         flat(vv[name]))
        out[name] = [r.reshape(w.shape) for r in res]

    summed = _sum_parts("sum_small", small_parts)
    small_total = _all_gather("ag_small_sum", summed).reshape(-1, LANE)
    small_grad = dict(zip(SMALL, _unpack(small_total, [grads[n].shape for n in SMALL])))
    for n in SMALL_SHARDED:
        shard = wv[n].shape[-1]
        small_grad[n] = lax.dynamic_slice_in_dim(small_grad[n], me * shard, shard, axis=small_grad[n].ndim - 1)
    for n in SMALL:
        w = wv[n]
        flat = lambda a: a.reshape(-1, w.shape[-1])
        res = _adamw_sum("adamw_" + n, [flat(small_grad[n])[None]], flat(w), flat(mv[n]), flat(vv[n]))
        out[n] = [r.reshape(w.shape) for r in res]

    return (loss, grad_x, *[out[n][0] for n in WEIGHTS], *[out[n][1] for n in WEIGHTS],
            *[out[n][2] for n in WEIGHTS], *[out[n][3] for n in WEIGHTS])
```

```python
import functools
import math

import jax
import jax.numpy as jnp
from jax import lax
from jax.experimental import pallas as pl
from jax.experimental.pallas import tpu as pltpu

F32 = jnp.float32
MXU_DTYPE = jnp.bfloat16
WIRE_DTYPE = jnp.bfloat16
N_DEV = 8
EPS = 1e-6
LRU_C = 8.0
S5_GROUP = 16
S5_STATE = 64
S5_SUPER = 256
LRU_BLOCKS_PER_CHUNK = 4
ADAM_LR, ADAM_B1, ADAM_B2, ADAM_EPS, ADAM_WD, ADAM_STEP = 0.001, 0.9, 0.999, 1e-08, 0.01, 10
VMEM_LIMIT_BYTES = 56 * 1024 * 1024
LANE = 128
SUBLANE = 8

WEIGHTS = ['norm_g', 'w_ada', 'b_ada', 's5_w_in', 's5_lam_re', 's5_lam_im', 's5_log_dt', 's5_b_re', 's5_b_im',
           's5_c_re', 's5_c_im', 's5_d', 's5_w_glu', 'lru_w_in', 'lru_conv_w', 'lru_conv_b', 'lru_w_rg', 'lru_b_rg',
           'lru_w_ig', 'lru_b_ig', 'lru_lam', 'lru_w_out', 'ffn_w_gu', 'ffn_w_down', 'final_g']
BIG = ('w_ada', 's5_w_in', 's5_w_glu', 'lru_w_in', 'lru_w_out', 'ffn_w_gu', 'ffn_w_down')
SMALL = tuple(n for n in WEIGHTS if n not in BIG)
SMALL_SHARDED = ('norm_g', 'lru_conv_w', 'lru_conv_b', 'lru_b_rg', 'lru_b_ig', 'lru_lam')

NN = (((1,), (0,)), ((), ()))
NT = (((1,), (1,)), ((), ()))
TN = (((0,), (0,)), ((), ()))


def _params(n_grid):
    return pltpu.CompilerParams(dimension_semantics=("arbitrary",) * n_grid, vmem_limit_bytes=VMEM_LIMIT_BYTES)


def _tile(dim, pref, align=LANE):
    if dim <= pref:
        return dim
    t = (pref // align) * align
    while t >= align:
        if dim % t == 0:
            return t
        t -= align
    return dim


def _dot(a, b, dims):
    return lax.dot_general(a.astype(MXU_DTYPE), b.astype(MXU_DTYPE), dims, preferred_element_type=F32)


def _gelu(x):
    k = math.sqrt(2.0 / math.pi)
    return 0.5 * x * (1.0 + jnp.tanh(k * (x + 0.044715 * (x * x * x))))


def _gelu_and_grad(x):
    k = math.sqrt(2.0 / math.pi)
    th = jnp.tanh(k * (x + 0.044715 * (x * x * x)))
    g = 0.5 * x * (1.0 + th)
    dg = 0.5 * (1.0 + th) + 0.5 * x * (1.0 - th * th) * (k * (1.0 + 3.0 * 0.044715 * (x * x)))
    return g, dg


def _neg_expm1(x):
    series = -x * (1.0 + x * (0.5 + x * (1.0 / 6.0 + x * (1.0 / 24.0 + x * (1.0 / 120.0)))))
    return jnp.where(x > -0.01, series, 1.0 - jnp.exp(x))


MESH = pl.DeviceIdType.MESH
N_CHIP = N_DEV // 2
ALL, SAME_CORE = 7, 6


def _place():
    x, y, c = lax.axis_index("x"), lax.axis_index("y"), lax.axis_index("c")
    return x, y, c


def _flip(place, k):
    x, y, c = place
    return (1 - x if (k >> 2) & 1 else x, 1 - y if (k >> 1) & 1 else y, 1 - c if k & 1 else c)


def _chunk_exchange(name, xs, group):
    return _ride_alone(name, _chunk_rider(xs, group))


class _Rider:
    def __init__(self, arrays, out_shape, scratch, start, finish, post, mid=None):
        self.arrays, self.out_shape, self.scratch = list(arrays), list(out_shape), list(scratch)
        self.start, self.finish, self.post, self.mid = start, finish, post, mid


def _chunk_rider(xs, group):
    n = len(xs)
    members, r, c_ = xs[0].shape
    assert members == {ALL: N_DEV, SAME_CORE: N_CHIP}[group]
    assert all(a.shape == xs[0].shape and a.dtype == xs[0].dtype for a in xs)
    ks = [k for k in range(1, N_DEV) if not k & ~group]
    member = (lambda p: 4 * p[0] + 2 * p[1] + p[2]) if group == ALL else (lambda p: 2 * p[0] + p[1])

    def copies(ins, outs, scratch):
        out = outs[0]
        send_sems, recv_sems, local_sems = scratch
        place = _place()
        me = member(place)
        local = [pltpu.make_async_copy(ins[l].at[me], out.at[me, l], local_sems.at[l]) for l in range(n)]
        remote = []
        for l in range(n):
            for k in ks:
                pid = _flip(place, k)
                peer = member(pid)

                def copy(land_at, l=l, k=k, peer=peer, pid=pid):
                    return pltpu.make_async_remote_copy(
                        src_ref=ins[l].at[peer], dst_ref=out.at[land_at, l], send_sem=send_sems.at[l * N_DEV + k],
                        recv_sem=recv_sems.at[l * N_DEV + k], device_id=pid, device_id_type=MESH)

                remote.append((copy, me, peer))
        return local, remote

    def start(ins, outs, scratch):
        local, remote = copies(ins, outs, scratch)
        for cp in local:
            cp.start()
        for copy, me, _ in remote:
            copy(me).start()

    def finish(ins, outs, scratch):
        local, remote = copies(ins, outs, scratch)
        for copy, me, peer in remote:
            copy(me).wait_send()
            copy(peer).wait_recv()
        for cp in local:
            cp.wait()

    return _Rider(
        xs, [jax.ShapeDtypeStruct((members, n, r, c_), xs[0].dtype)],
        [pltpu.SemaphoreType.DMA((n * N_DEV,)), pltpu.SemaphoreType.DMA((n * N_DEV,)), pltpu.SemaphoreType.DMA((n,))],
        start, finish, lambda outs: outs[0].reshape(members, n * r, c_))


def _gather_rider(xs):
    n = len(xs)
    per = 7

    def plan(ins, outs, scratch):
        send_sems, recv_sems, local_sems = scratch
        x, y, c = place = _place()
        sibling, x_nb, y_nb, diag = (x, y, 1 - c), (1 - x, y, c), (x, 1 - y, c), (1 - x, 1 - y, c)
        relayed = (x + c * (1 - 2 * x), y + (1 - c) * (1 - 2 * y), c)
        onward = (x + (1 - c) * (1 - 2 * x), y + c * (1 - 2 * y), c)
        jobs = []
        for l in range(n):
            slot = lambda p, l=l: outs[l].at[2 * p[0] + p[1], p[2]]

            def copy(k, block, to, src=None, l=l, slot=slot):
                return pltpu.make_async_remote_copy(
                    src_ref=slot(block) if src is None else src, dst_ref=slot(block), send_sem=send_sems.at[l * per + k],
                    recv_sem=recv_sems.at[l * per + k], device_id=to, device_id_type=MESH)

            jobs.append(dict(
                mine=pltpu.make_async_copy(ins[l], slot(place), local_sems.at[l]),
                first=[copy(0, place, sibling, src=ins[l]), copy(1, place, x_nb, src=ins[l]), copy(2, place, y_nb, src=ins[l])],
                landed=[copy(1, x_nb, place), copy(2, y_nb, place)],
                second=[copy(3, relayed, onward), copy(4, x_nb, sibling), copy(5, y_nb, sibling)],
                relay_landed=copy(3, diag, place), last=copy(6, diag, sibling),
                from_sibling=[copy(0, sibling, place)] + [copy(4 + j, (p[0], p[1], 1 - c), place)
                                                          for j, p in enumerate((x_nb, y_nb, diag))]))
        return jobs

    def start(ins, outs, scratch):
        for job in plan(ins, outs, scratch):
            job['mine'].start()
            for cp in job['first']:
                cp.start()

    def mid(ins, outs, scratch):
        for job in plan(ins, outs, scratch):
            for cp in job['landed']:
                cp.wait_recv()
            for cp in job['second']:
                cp.start()

    def finish(ins, outs, scratch):
        jobs = plan(ins, outs, scratch)
        for job in jobs:
            job['relay_landed'].wait_recv()
            job['last'].start()
        for job in jobs:
            for cp in job['from_sibling']:
                cp.wait_recv()
            for cp in job['first'] + job['second'] + [job['last']]:
                cp.wait_send()
            job['mine'].wait()

    return _Rider(
        xs, [jax.ShapeDtypeStruct((N_CHIP, 2) + x.shape, x.dtype) for x in xs],
        [pltpu.SemaphoreType.DMA((n * per,)), pltpu.SemaphoreType.DMA((n * per,)), pltpu.SemaphoreType.DMA((n,))],
        start, finish, lambda outs: [o.reshape((N_DEV,) + x.shape) for o, x in zip(outs, xs)], mid=mid)


HBM_SPEC = pl.BlockSpec(memory_space=pltpu.HBM)


def _ride_alone(name, rider):
    n_in, n_out = len(rider.arrays), len(rider.out_shape)

    def body(*refs):
        parts = refs[:n_in], refs[n_in:n_in + n_out], refs[n_in + n_out:]
        rider.start(*parts)
        if rider.mid is not None:
            rider.mid(*parts)
        rider.finish(*parts)

    outs = pl.pallas_call(body, name=name, out_shape=rider.out_shape, in_specs=[HBM_SPEC] * n_in,
                          out_specs=[HBM_SPEC] * n_out, scratch_shapes=rider.scratch)(*rider.arrays)
    return rider.post(list(outs))


def _call(body, *, name, grid, in_specs, out_specs, out_shape, scratch_shapes=(), args, rider=None):
    single = not isinstance(out_shape, (list, tuple))
    out_shape = [out_shape] if single else list(out_shape)
    out_specs = [out_specs] if single else list(out_specs)
    scratch_shapes = list(scratch_shapes)
    unwrap = lambda outs: outs[0] if single else list(outs)
    if rider is None:
        outs = pl.pallas_call(body, name=name, grid=grid, in_specs=list(in_specs), out_specs=out_specs, out_shape=out_shape,
                              scratch_shapes=scratch_shapes, compiler_params=_params(len(grid)))(*args)
        return unwrap(outs)
    n_in, n_out, n_scr = len(in_specs), len(out_shape), len(scratch_shapes)
    r_in, r_out = len(rider.arrays), len(rider.out_shape)

    def carried(*refs):
        ins, refs = refs[:n_in], refs[n_in:]
        r_ins, refs = refs[:r_in], refs[r_in:]
        outs, refs = refs[:n_out], refs[n_out:]
        r_outs, refs = refs[:r_out], refs[r_out:]
        scr, r_scr = refs[:n_scr], refs[n_scr:]
        step = 0
        for ax, g in enumerate(grid):
            step = step * g + pl.program_id(ax)

        @pl.when(step == 0)
        def _():
            rider.start(r_ins, r_outs, r_scr)

        body(*ins, *outs, *scr)

        if rider.mid is not None and total > 1:
            @pl.when(step == (5 * (total - 1)) // 8)
            def _():
                rider.mid(r_ins, r_outs, r_scr)

        @pl.when(step == total - 1)
        def _():
            if rider.mid is not None and total == 1:
                rider.mid(r_ins, r_outs, r_scr)
            rider.finish(r_ins, r_outs, r_scr)

    total = math.prod(grid)

    outs = pl.pallas_call(
        carried, name=name, grid=grid, in_specs=list(in_specs) + [HBM_SPEC] * r_in, out_specs=out_specs + [HBM_SPEC] * r_out,
        out_shape=out_shape + rider.out_shape, scratch_shapes=scratch_shapes + rider.scratch,
        compiler_params=_params(len(grid)))(*args, *rider.arrays)
    return unwrap(outs[:n_out]), rider.post(list(outs[n_out:]))


def _all_gather(name, x):
    return _ride_alone(name, _gather_rider([x]))[0]


def _sibling_rider(x):
    _, r, c_ = x.shape

    def copies(ins, outs, scratch):
        send_sems, recv_sems = scratch
        place = _place()
        return [pltpu.make_async_remote_copy(
            src_ref=ins[0].at[2 * chip + (1 - place[2])], dst_ref=outs[0].at[chip], send_sem=send_sems.at[chip],
            recv_sem=recv_sems.at[chip], device_id=_flip(place, 1), device_id_type=MESH) for chip in range(N_CHIP)]

    def start(ins, outs, scratch):
        for cp in copies(ins, outs, scratch):
            cp.start()

    def finish(ins, outs, scratch):
        for cp in copies(ins, outs, scratch):
            cp.wait()

    return _Rider([x], [jax.ShapeDtypeStruct((N_CHIP, r, c_), x.dtype)],
                  [pltpu.SemaphoreType.DMA((N_CHIP,)), pltpu.SemaphoreType.DMA((N_CHIP,))], start, finish, lambda outs: outs[0])


def _join(riders):
    def cut(seq, counts):
        out, off = [], 0
        for k in counts:
            out.append(seq[off:off + k])
            off += k
        return out

    def parts(ins, outs, scratch):
        return zip(riders, cut(ins, [len(r.arrays) for r in riders]), cut(outs, [len(r.out_shape) for r in riders]),
                   cut(scratch, [len(r.scratch) for r in riders]))

    def start(ins, outs, scratch):
        for r, i, o, s in parts(ins, outs, scratch):
            r.start(i, o, s)

    def finish(ins, outs, scratch):
        for r, i, o, s in parts(ins, outs, scratch):
            r.finish(i, o, s)

    def mid(ins, outs, scratch):
        for r, i, o, s in parts(ins, outs, scratch):
            if r.mid is not None:
                r.mid(i, o, s)

    return _Rider(
        [a for r in riders for a in r.arrays], [o for r in riders for o in r.out_shape], [s for r in riders for s in r.scratch],
        start, finish, lambda outs: [r.post(o) for r, o in zip(riders, cut(outs, [len(r.out_shape) for r in riders]))],
        mid=mid if any(r.mid is not None for r in riders) else None)


def _ride(fn, *args, riders):
    return fn(*args, rider=_join(riders))


def _pair_sum(name, x, got, core):
    _, r, c_ = x.shape
    br = _tile(r, 256, 2 * SUBLANE)

    def body(core_ref, x_ref, g_ref, o_ref):
        o_ref[...] = (x_ref[...].astype(F32) + g_ref[...].astype(F32)).astype(o_ref.dtype)

    return pl.pallas_call(
        body, name=name, out_shape=jax.ShapeDtypeStruct((N_CHIP, r, c_), x.dtype),
        grid_spec=pltpu.PrefetchScalarGridSpec(
            num_scalar_prefetch=1, grid=(N_CHIP, r // br),
            in_specs=[pl.BlockSpec((None, br, c_), lambda ch, i, core_ref: (2 * ch + core_ref[0], i, 0)),
                      pl.BlockSpec((None, br, c_), lambda ch, i, core_ref: (ch, i, 0))],
            out_specs=pl.BlockSpec((None, br, c_), lambda ch, i, core_ref: (ch, i, 0))),
        compiler_params=_params(2))(core, x, got)


def _mm(name, a, b, out_shape, out_dtype, grid, a_spec, b_spec, o_spec, dims, n_red, acc_shape, rider=None):
    red = tuple(range(len(grid) - n_red, len(grid)))
    out_type = jax.ShapeDtypeStruct(out_shape, out_dtype)
    if all(grid[ax] == 1 for ax in red):
        def single(a_ref, b_ref, o_ref):
            o_ref[...] = _dot(a_ref[...], b_ref[...], dims).astype(o_ref.dtype)

        return _call(single, name=name, out_shape=out_type, grid=grid, in_specs=[a_spec, b_spec], out_specs=o_spec,
                     args=(a, b), rider=rider)

    def body(a_ref, b_ref, o_ref, acc_ref):
        first = functools.reduce(jnp.logical_and, [pl.program_id(ax) == 0 for ax in red])
        last = functools.reduce(jnp.logical_and, [pl.program_id(ax) == grid[ax] - 1 for ax in red])

        @pl.when(first)
        def _():
            acc_ref[...] = jnp.zeros_like(acc_ref)

        acc_ref[...] += _dot(a_ref[...], b_ref[...], dims)

        @pl.when(last)
        def _():
            o_ref[...] = acc_ref[...].astype(o_ref.dtype)

    return _call(body, name=name, out_shape=out_type, grid=grid, in_specs=[a_spec, b_spec], out_specs=o_spec,
                 scratch_shapes=[pltpu.VMEM(acc_shape, F32)], args=(a, b), rider=rider)


def _mm_col(name, a, b, out_dtype=F32, rider=None):
    m, k = a.shape
    j, _, n = b.shape
    bm, bk = _tile(m, 1024), _tile(k, 2048)
    return _mm(name, a, b, (j, m, n), out_dtype, (j, m // bm, k // bk),
               pl.BlockSpec((bm, bk), lambda jj, mm, kk: (mm, kk)),
               pl.BlockSpec((None, bk, n), lambda jj, mm, kk: (jj, kk, 0)),
               pl.BlockSpec((None, bm, n), lambda jj, mm, kk: (jj, mm, 0)), NN, 1, (bm, n), rider)


def _mm_col_da(name, do, b, rider=None):
    j, m, n = do.shape
    k = b.shape[1]
    bm, bk = _tile(m, 1024), _tile(k, 1024)
    return _mm(name, do, b, (m, k), F32, (m // bm, k // bk, j),
               pl.BlockSpec((None, bm, n), lambda mm, kk, jj: (jj, mm, 0)),
               pl.BlockSpec((None, bk, n), lambda mm, kk, jj: (jj, kk, 0)),
               pl.BlockSpec((bm, bk), lambda mm, kk, jj: (mm, kk)), NT, 1, (bm, bk), rider)


def _mm_col_db(name, a, do, out_dtype, rider=None):
    m, k = a.shape
    j, _, n = do.shape
    bm, bk = _tile(m, 2048), _tile(k, 512)
    return _mm(name, a, do, (j, k, n), out_dtype, (j, k // bk, m // bm),
               pl.BlockSpec((bm, bk), lambda jj, kk, mm: (mm, kk)),
               pl.BlockSpec((None, bm, n), lambda jj, kk, mm: (jj, mm, 0)),
               pl.BlockSpec((None, bk, n), lambda jj, kk, mm: (jj, kk, 0)), TN, 1, (bk, n), rider)


def _row_bk(kq):
    return kq if (kq % LANE or kq // LANE in (11,)) else _tile(kq, 2048)


def _mm_row(name, a, b, out_dtype=F32, rider=None):
    q, m, kq = a.shape
    n = b.shape[1]
    bm, bn, bk = _tile(m, 1024), _tile(n, 1024), _row_bk(kq)
    nk = kq // bk
    return _mm(name, a, b, (m, n), out_dtype, (m // bm, n // bn, q, nk),
               pl.BlockSpec((None, bm, bk), lambda mm, nn, qq, kk: (qq, mm, kk)),
               pl.BlockSpec((bk, bn), lambda mm, nn, qq, kk: (qq * nk + kk, nn)),
               pl.BlockSpec((bm, bn), lambda mm, nn, qq, kk: (mm, nn)), NN, 2, (bm, bn), rider)


def _mm_row_da(name, do, b, q, rider=None):
    m, n = do.shape
    kq = b.shape[0] // q
    bm, bn = _tile(m, 1024), _tile(n, 2048)
    return _mm(name, do, b, (q, m, kq), F32, (q, m // bm, n // bn),
               pl.BlockSpec((bm, bn), lambda qq, mm, nn: (mm, nn)),
               pl.BlockSpec((kq, bn), lambda qq, mm, nn: (qq, nn)),
               pl.BlockSpec((None, bm, kq), lambda qq, mm, nn: (qq, mm, 0)), NT, 1, (bm, kq), rider)


def _mm_row_db(name, a, do, out_dtype):
    q, m, kq = a.shape
    n = do.shape[1]
    bm, bn = _tile(m, 2048), _tile(n, 512)
    return _mm(name, a, do, (q * kq, n), out_dtype, (q, n // bn, m // bm),
               pl.BlockSpec((None, bm, kq), lambda qq, nn, mm: (qq, mm, 0)),
               pl.BlockSpec((bm, bn), lambda qq, nn, mm: (mm, nn)),
               pl.BlockSpec((kq, bn), lambda qq, nn, mm: (qq, nn)), TN, 1, (kq, bn))


def _row_spec(bm, d):
    return pl.BlockSpec((bm, d), lambda i: (i, 0))


def _vec_spec(d):
    return pl.BlockSpec((1, d), lambda i: (0, 0))


def _norm_mod_fwd(name, x, gain, sc, sh):
    t, d = x.shape
    bm = _tile(t, 256, SUBLANE)

    def body(x_ref, g_ref, sc_ref, sh_ref, h_ref):
        xv = x_ref[...]
        rstd = lax.rsqrt(jnp.mean(xv * xv, axis=-1, keepdims=True) + EPS)
        h_ref[...] = ((xv * rstd) * g_ref[...] * (1.0 + sc_ref[...]) + sh_ref[...]).astype(h_ref.dtype)

    return pl.pallas_call(
        body, name=name, out_shape=jax.ShapeDtypeStruct((t, d), MXU_DTYPE), grid=(t // bm,),
        in_specs=[_row_spec(bm, d), _vec_spec(d), _vec_spec(d), _vec_spec(d)], out_specs=_row_spec(bm, d),
        compiler_params=_params(1))(x, gain, sc, sh)


def _norm_mod_bwd(name, x, dh, dres, gain, sc):
    t, d = x.shape
    bm = _tile(t, 256, SUBLANE)

    def body(x_ref, dh_ref, dres_ref, g_ref, sc_ref, dx_ref, dg_ref, dsc_ref, dsh_ref):
        @pl.when(pl.program_id(0) == 0)
        def _():
            dg_ref[...] = jnp.zeros_like(dg_ref)
            dsc_ref[...] = jnp.zeros_like(dsc_ref)
            dsh_ref[...] = jnp.zeros_like(dsh_ref)

        xv, dh_ = x_ref[...], dh_ref[...]
        rstd = lax.rsqrt(jnp.mean(xv * xv, axis=-1, keepdims=True) + EPS)
        nrm = xv * rstd
        gain_ = g_ref[...]
        dsh_ref[...] += jnp.sum(dh_, axis=0, keepdims=True)
        dsc_ref[...] += jnp.sum(dh_ * (nrm * gain_), axis=0, keepdims=True)
        dhn = dh_ * (1.0 + sc_ref[...])
        dg_ref[...] += jnp.sum(dhn * nrm, axis=0, keepdims=True)
        dn = dhn * gain_
        dx_ref[...] = dres_ref[...] + rstd * (dn - nrm * jnp.mean(dn * nrm, axis=-1, keepdims=True))

    vec = jax.ShapeDtypeStruct((1, d), F32)
    return pl.pallas_call(
        body, name=name, out_shape=[jax.ShapeDtypeStruct((t, d), F32), vec, vec, vec], grid=(t // bm,),
        in_specs=[_row_spec(bm, d), _row_spec(bm, d), _row_spec(bm, d), _vec_spec(d), _vec_spec(d)],
        out_specs=[_row_spec(bm, d), _vec_spec(d), _vec_spec(d), _vec_spec(d)],
        compiler_params=_params(1))(x, dh, dres, gain, sc)


def _loss_bwd(name, x, target, gain):
    t, d = x.shape
    bm = _tile(t, 256, SUBLANE)

    def body(x_ref, t_ref, g_ref, dx_ref, loss_ref, dg_ref):
        @pl.when(pl.program_id(0) == 0)
        def _():
            loss_ref[...] = jnp.zeros_like(loss_ref)
            dg_ref[...] = jnp.zeros_like(dg_ref)

        xv = x_ref[...]
        rstd = lax.rsqrt(jnp.mean(xv * xv, axis=-1, keepdims=True) + EPS)
        nrm = xv * rstd
        gain_ = g_ref[...]
        err = nrm * gain_ - t_ref[...]
        per_tok = jnp.mean(err * err, axis=-1, keepdims=True)
        loss_ref[...] += 0.5 * jnp.sum(per_tok, axis=0, keepdims=True)
        dout = err * (1.0 / d)
        dg_ref[...] += jnp.sum(dout * nrm, axis=0, keepdims=True)
        dn = dout * gain_
        dx_ref[...] = rstd * (dn - nrm * jnp.mean(dn * nrm, axis=-1, keepdims=True))

    return pl.pallas_call(
        body, name=name,
        out_shape=[jax.ShapeDtypeStruct((t, d), F32), jax.ShapeDtypeStruct((1, 1), F32),
                   jax.ShapeDtypeStruct((1, d), F32)],
        grid=(t // bm,), in_specs=[_row_spec(bm, d), _row_spec(bm, d), _vec_spec(d)],
        out_specs=[_row_spec(bm, d), pl.BlockSpec((1, 1), lambda i: (0, 0)), _vec_spec(d)],
        compiler_params=_params(1))(x, target, gain)


def _resid(name, x, y, g):
    t, d = x.shape
    bm = _tile(t, 256, SUBLANE)

    def body(x_ref, y_ref, g_ref, o_ref):
        o_ref[...] = x_ref[...] + g_ref[...] * y_ref[...]

    return pl.pallas_call(
        body, name=name, out_shape=jax.ShapeDtypeStruct((t, d), F32), grid=(t // bm,),
        in_specs=[_row_spec(bm, d), _row_spec(bm, d), _vec_spec(d)], out_specs=_row_spec(bm, d),
        compiler_params=_params(1))(x, y, g)


def _gate_bwd(name, dx, y, g):
    t, d = dx.shape
    bm = _tile(t, 256, SUBLANE)

    def body(dx_ref, y_ref, g_ref, dy_ref, dg_ref):
        @pl.when(pl.program_id(0) == 0)
        def _():
            dg_ref[...] = jnp.zeros_like(dg_ref)

        dxv = dx_ref[...]
        dy_ref[...] = (g_ref[...] * dxv).astype(dy_ref.dtype)
        dg_ref[...] += jnp.sum(dxv * y_ref[...], axis=0, keepdims=True)

    return pl.pallas_call(
        body, name=name, out_shape=[jax.ShapeDtypeStruct((t, d), MXU_DTYPE), jax.ShapeDtypeStruct((1, d), F32)],
        grid=(t // bm,), in_specs=[_row_spec(bm, d), _row_spec(bm, d), _vec_spec(d)],
        out_specs=[_row_spec(bm, d), _vec_spec(d)], compiler_params=_params(1))(dx, y, g)


def _glu_resid_fwd(name, z, x, g):
    _, t, n = z.shape
    d = x.shape[1]
    half = N_DEV // 2
    bm = _tile(t, 256, SUBLANE)

    def body(v_ref, gt_ref, x_ref, g_ref, o_ref):
        o_ref[...] = x_ref[...] + g_ref[...] * (v_ref[...] * jax.nn.sigmoid(gt_ref[...]))

    return pl.pallas_call(
        body, name=name, out_shape=jax.ShapeDtypeStruct((t, d), F32), grid=(half, t // bm),
        in_specs=[pl.BlockSpec((None, bm, n), lambda q, i: (q, i, 0)),
                  pl.BlockSpec((None, bm, n), lambda q, i: (q + half, i, 0)),
                  pl.BlockSpec((bm, n), lambda q, i: (i, q)), pl.BlockSpec((1, n), lambda q, i: (0, q))],
        out_specs=pl.BlockSpec((bm, n), lambda q, i: (i, q)), compiler_params=_params(2))(z, z, x, g)


def _glu_resid_bwd(name, z, dx, g):
    _, t, n = z.shape
    d = dx.shape[1]
    half = N_DEV // 2
    bm = _tile(t, 256, SUBLANE)

    def body(z_ref, dx_ref, g_ref, dz_ref, dg_ref):
        @pl.when(pl.program_id(1) == 0)
        def _():
            dg_ref[...] = jnp.zeros_like(dg_ref)

        v, dxv = z_ref[0], dx_ref[...]
        sig = jax.nn.sigmoid(z_ref[1])
        dout = g_ref[...] * dxv
        dg_ref[...] += jnp.sum(dxv * (v * sig), axis=0, keepdims=True)
        dz_ref[0] = (dout * sig).astype(dz_ref.dtype)
        dz_ref[1] = (dout * v * (sig * (1.0 - sig))).astype(dz_ref.dtype)

    pair = pl.BlockSpec((2, None, bm, n), lambda q, i: (0, q, i, 0))
    dz, dg = pl.pallas_call(
        body, name=name,
        out_shape=[jax.ShapeDtypeStruct((2, half, t, n), MXU_DTYPE), jax.ShapeDtypeStruct((1, d), F32)],
        grid=(half, t // bm),
        in_specs=[pair, pl.BlockSpec((bm, n), lambda q, i: (i, q)), pl.BlockSpec((1, n), lambda q, i: (0, q))],
        out_specs=[pair, pl.BlockSpec((1, n), lambda q, i: (0, q))],
        compiler_params=_params(2))(z.reshape(2, half, t, n), dx, g)
    return dz.reshape(N_DEV, t, n), dg


def _swiglu_act_fwd(name, gu):
    _, t, n = gu.shape
    half = N_DEV // 2
    bm = _tile(t, 256, SUBLANE)

    def body(g_ref, u_ref, o_ref):
        gv = g_ref[...].astype(F32)
        o_ref[...] = (gv * jax.nn.sigmoid(gv) * u_ref[...].astype(F32)).astype(o_ref.dtype)

    return pl.pallas_call(
        body, name=name, out_shape=jax.ShapeDtypeStruct((half, t, n), MXU_DTYPE), grid=(half, t // bm),
        in_specs=[pl.BlockSpec((None, bm, n), lambda q, i: (q, i, 0)),
                  pl.BlockSpec((None, bm, n), lambda q, i: (q + half, i, 0))],
        out_specs=pl.BlockSpec((None, bm, n), lambda q, i: (q, i, 0)), compiler_params=_params(2))(gu, gu)


def _swiglu_act_bwd(name, gu, dact, rider=None):
    _, t, n = gu.shape
    half = N_DEV // 2
    bm = _tile(t, 256, SUBLANE)

    def body(gu_ref, da_ref, o_ref):
        gv, da = gu_ref[0].astype(F32), da_ref[...]
        sig = jax.nn.sigmoid(gv)
        o_ref[0] = (da * gu_ref[1].astype(F32) * (sig * (1.0 + gv * (1.0 - sig)))).astype(o_ref.dtype)
        o_ref[1] = (da * (gv * sig)).astype(o_ref.dtype)

    pair = pl.BlockSpec((2, None, bm, n), lambda q, i: (0, q, i, 0))
    res = _call(
        body, name=name, out_shape=jax.ShapeDtypeStruct((2, half, t, n), MXU_DTYPE), grid=(half, t // bm),
        in_specs=[pair, pl.BlockSpec((None, bm, n), lambda q, i: (q, i, 0))], out_specs=pair,
        args=(gu.reshape(2, half, t, n), dact), rider=rider)
    if rider is None:
        return res.reshape(N_DEV, t, n)
    return res[0].reshape(N_DEV, t, n), res[1]


def _ada_fwd(name, c16, w_ada, b_loc):
    nl, d, n = w_ada.shape
    bn = _tile(n, 512)

    def body(c_ref, w_ref, b_ref, o_ref):
        cv = c_ref[...]
        o_ref[...] = _dot(cv * jax.nn.sigmoid(cv), w_ref[...], NN) + b_ref[...]

    return pl.pallas_call(
        body, name=name, out_shape=jax.ShapeDtypeStruct((nl, c16.shape[0], n), F32), grid=(nl, n // bn),
        in_specs=[pl.BlockSpec(c16.shape, lambda i, j: (0, 0)), pl.BlockSpec((None, d, bn), lambda i, j: (i, 0, j)),
                  pl.BlockSpec((None, 1, bn), lambda i, j: (i, 0, j))],
        out_specs=pl.BlockSpec((None, c16.shape[0], bn), lambda i, j: (i, 0, j)),
        compiler_params=_params(2))(c16, w_ada, b_loc)


def _adam_update(g, w, m, v):
    m = ADAM_B1 * m + (1.0 - ADAM_B1) * g
    v = ADAM_B2 * v + (1.0 - ADAM_B2) * (g * g)
    m_hat = m / (1.0 - ADAM_B1 ** ADAM_STEP)
    v_hat = v / (1.0 - ADAM_B2 ** ADAM_STEP)
    delta = -ADAM_LR * (m_hat / (jnp.sqrt(v_hat) + ADAM_EPS) + ADAM_WD * w)
    return delta, m, v


def _adamw_w_ada(name, c16, dmod16, w, m, v, rider=None):
    nl, d, n = w.shape
    br = _tile(d, 256)

    def body(c_ref, dm_ref, w_ref, m_ref, v_ref, g_ref, dl_ref, mo_ref, vo_ref):
        cv = c_ref[...]
        g = _dot(cv * jax.nn.sigmoid(cv), dm_ref[...], TN)
        g_ref[...] = g
        dl_ref[...], mo_ref[...], vo_ref[...] = _adam_update(g, w_ref[...], m_ref[...], v_ref[...])

    blk = pl.BlockSpec((None, br, n), lambda i, r: (i, r, 0))
    shp = jax.ShapeDtypeStruct(w.shape, F32)
    return _call(
        body, name=name, out_shape=[shp] * 4, grid=(nl, d // br),
        in_specs=[pl.BlockSpec((c16.shape[0], br), lambda i, r: (0, r)),
                  pl.BlockSpec((None, dmod16.shape[1], n), lambda i, r: (i, 0, 0)), blk, blk, blk],
        out_specs=[blk] * 4, args=(c16, dmod16, w, m, v), rider=rider)


def _adamw_sum(name, parts, w, m, v, rider=None):
    nl = len(parts)
    p, r, c = parts[0].shape
    br = _tile(r, max(128, (1 << 17) // max(c, LANE)), 2 * SUBLANE)
    nb = r // br

    def body(*refs):
        p_refs, (w_ref, m_ref, v_ref, g_ref, dl_ref, mo_ref, vo_ref) = refs[:nl], refs[nl:]
        layer = pl.program_id(0)
        g = None
        for l, p_ref in enumerate(p_refs):
            gl = p_ref[0].astype(F32)
            for s in range(1, p):
                gl = gl + p_ref[s].astype(F32)
            g = gl if g is None else jnp.where(layer == l, gl, g)
        g_ref[...] = g
        dl_ref[...], mo_ref[...], vo_ref[...] = _adam_update(g, w_ref[...], m_ref[...], v_ref[...])

    blk = pl.BlockSpec((br, c), lambda l, i: (l * nb + i, 0))
    shp = jax.ShapeDtypeStruct((nl * r, c), F32)
    return _call(
        body, name=name, out_shape=[shp] * 4, grid=(nl, nb),
        in_specs=[pl.BlockSpec((p, br, c), lambda l, i: (0, i, 0))] * nl + [blk, blk, blk], out_specs=[blk] * 4,
        args=(*parts, w, m, v), rider=rider)


def _sum_parts(name, parts):
    p, r, c = parts.shape

    def body(p_ref, o_ref):
        g = p_ref[0]
        for s in range(1, p):
            g = g + p_ref[s]
        o_ref[...] = g

    return pl.pallas_call(body, name=name, out_shape=jax.ShapeDtypeStruct((r, c), F32))(parts)


def _s5_disc(name, lam_re, lam_im, log_dt, b_re, b_im):
    def body(lr_ref, li_ref, ld_ref, br_ref, bi_ref, ar_ref, ai_ref, bbr_ref, bbi_ref):
        lr, li = lr_ref[...], li_ref[...]
        dt = jnp.exp(ld_ref[...])
        mag = jnp.exp(lr * dt)
        a_re, a_im = mag * jnp.cos(li * dt), mag * jnp.sin(li * dt)
        nr, ni = a_re - 1.0, a_im
        den = lr * lr + li * li
        f_re, f_im = (nr * lr + ni * li) / den, (ni * lr - nr * li) / den
        br, bi = br_ref[...], bi_ref[...]
        ar_ref[...], ai_ref[...] = a_re, a_im
        bbr_ref[...] = f_re * br - f_im * bi
        bbi_ref[...] = f_re * bi + f_im * br

    s_a, s_b = jax.ShapeDtypeStruct(lam_re.shape, F32), jax.ShapeDtypeStruct(b_re.shape, F32)
    return pl.pallas_call(body, name=name, out_shape=[s_a, s_a, s_b, s_b])(lam_re, lam_im, log_dt, b_re, b_im)


def _s5_disc_bwd(name, lam_re, lam_im, log_dt, b_re, b_im, dab_re, dab_im, dbb_re, dbb_im):
    def body(lr_ref, li_ref, ld_ref, br_ref, bi_ref, dar_ref, dai_ref, dbbr_ref, dbbi_ref,
             dlr_ref, dli_ref, dld_ref, dbr_ref, dbi_ref):
        lr, li = lr_ref[...], li_ref[...]
        dt = jnp.exp(ld_ref[...])
        mag = jnp.exp(lr * dt)
        a_re, a_im = mag * jnp.cos(li * dt), mag * jnp.sin(li * dt)
        nr, ni = a_re - 1.0, a_im
        den = lr * lr + li * li
        f_re, f_im = (nr * lr + ni * li) / den, (ni * lr - nr * li) / den
        br, bi = br_ref[...], bi_ref[...]
        dbbr, dbbi = dbbr_ref[...], dbbi_ref[...]
        dbr_ref[...] = f_re * dbbr + f_im * dbbi
        dbi_ref[...] = f_re * dbbi - f_im * dbbr
        df_re = jnp.sum(dbbr * br + dbbi * bi, axis=1, keepdims=True)
        df_im = jnp.sum(dbbi * br - dbbr * bi, axis=1, keepdims=True)
        dnr = (df_re * lr - df_im * li) / den
        dni = (df_re * li + df_im * lr) / den
        dden = -(df_re * f_re + df_im * f_im) / den
        dlr = (df_re * nr + df_im * ni) / den + 2.0 * lr * dden
        dli = (df_re * ni - df_im * nr) / den + 2.0 * li * dden
        da_re, da_im = dar_ref[...] + dnr, dai_ref[...] + dni
        dmag_mag = da_re * a_re + da_im * a_im
        dth = da_im * a_re - da_re * a_im
        dlr_ref[...] = dlr + dmag_mag * dt
        dli_ref[...] = dli + dth * dt
        ddt = jnp.sum(dmag_mag * lr + dth * li, axis=2, keepdims=True)
        dld_ref[...] = ddt * dt

    s_a, s_b = jax.ShapeDtypeStruct(lam_re.shape, F32), jax.ShapeDtypeStruct(b_re.shape, F32)
    return pl.pallas_call(
        body, name=name, out_shape=[s_a, s_a, jax.ShapeDtypeStruct(log_dt.shape, F32), s_b, s_b],
    )(lam_re, lam_im, log_dt, b_re, b_im, dab_re, dab_im, dbb_re, dbb_im)


def _s5_time_block(t):
    return _tile(t, 128, SUBLANE)


def _s5_scan_fwd(name, u, bb_re, bb_im, ab_re, ab_im, rider=None):
    t, d = u.shape
    nsg, cs, ns = bb_re.shape
    tb = _s5_time_block(t)

    def body(u_ref, bbr_hbm, bbi_hbm, ar_ref, ai_ref, sr_ref, si_ref, srm_ref, sim_ref, bbr, bbi, cr_ref, ci_ref):
        @pl.when(pl.program_id(0) == 0)
        def _():
            pltpu.sync_copy(bbr_hbm, bbr)
            pltpu.sync_copy(bbi_hbm, bbi)
            cr_ref[...] = jnp.zeros_like(cr_ref)
            ci_ref[...] = jnp.zeros_like(ci_ref)

        for sg in range(nsg):
            us = u_ref[:, sg * cs:(sg + 1) * cs]
            sr_ref[:, sg, :] = _dot(us, bbr[sg], NN)
            si_ref[:, sg, :] = _dot(us, bbi[sg], NN)
        ar, ai = ar_ref[...], ai_ref[...]

        def step(i, carry):
            cr, ci = carry
            nr = ar * cr - ai * ci + sr_ref[i]
            ni = ar * ci + ai * cr + si_ref[i]
            sr_ref[i] = nr
            si_ref[i] = ni
            return nr, ni

        cr, ci = lax.fori_loop(0, tb, step, (cr_ref[...], ci_ref[...]), unroll=2)
        cr_ref[...], ci_ref[...] = cr, ci
        srm_ref[...] = jnp.swapaxes(sr_ref[...], 0, 1).astype(MXU_DTYPE)
        sim_ref[...] = jnp.swapaxes(si_ref[...], 0, 1).astype(MXU_DTYPE)

    scan = jax.ShapeDtypeStruct((t, nsg, ns), F32)
    mxu = jax.ShapeDtypeStruct((nsg, t, ns), MXU_DTYPE)
    hbm = pl.BlockSpec(memory_space=pltpu.HBM)
    full = pl.BlockSpec((nsg, ns), lambda i: (0, 0))
    return _call(
        body, name=name, out_shape=[scan, scan, mxu, mxu], grid=(t // tb,),
        in_specs=[_row_spec(tb, d), hbm, hbm, full, full],
        out_specs=[pl.BlockSpec((tb, nsg, ns), lambda i: (i, 0, 0))] * 2 + [pl.BlockSpec((nsg, tb, ns), lambda i: (0, i, 0))] * 2,
        scratch_shapes=[pltpu.VMEM(bb_re.shape, bb_re.dtype), pltpu.VMEM(bb_im.shape, bb_im.dtype),
                        pltpu.VMEM((nsg, ns), F32), pltpu.VMEM((nsg, ns), F32)],
        args=(u, bb_re, bb_im, ab_re, ab_im), rider=rider)


def _s5_out_fwd(name, s_re, s_im, cc_re, cc_im, u, dskip):
    nsg, t, ns = s_re.shape
    d = u.shape[1]
    cs = cc_re.shape[2]
    tb = _tile(t, 512, SUBLANE)

    def body(sr_ref, si_ref, cr_ref, ci_ref, u_ref, d_ref, yp_ref, ya_ref):
        y = _dot(sr_ref[...], cr_ref[...], NN) - _dot(si_ref[...], ci_ref[...], NN) + d_ref[...] * u_ref[...]
        yp_ref[...] = y
        ya_ref[...] = _gelu(y).astype(ya_ref.dtype)

    s_spec = pl.BlockSpec((None, tb, ns), lambda sg, i: (sg, i, 0))
    c_spec = pl.BlockSpec((None, ns, cs), lambda sg, i: (sg, 0, 0))
    col = pl.BlockSpec((tb, cs), lambda sg, i: (i, sg))
    return pl.pallas_call(
        body, name=name, out_shape=[jax.ShapeDtypeStruct((t, d), F32), jax.ShapeDtypeStruct((t, d), MXU_DTYPE)],
        grid=(nsg, t // tb), in_specs=[s_spec, s_spec, c_spec, c_spec, col, pl.BlockSpec((1, cs), lambda sg, i: (0, sg))],
        out_specs=[col, col], compiler_params=_params(2))(s_re, s_im, cc_re, cc_im, u, dskip)


def _gelu_bwd(name, dy, ypre):
    t, d = dy.shape
    bm = _tile(t, 256, SUBLANE)

    def body(dy_ref, yp_ref, o_ref):
        o_ref[...] = (dy_ref[...] * _gelu_and_grad(yp_ref[...])[1]).astype(o_ref.dtype)

    return pl.pallas_call(
        body, name=name, out_shape=jax.ShapeDtypeStruct((t, d), MXU_DTYPE), grid=(t // bm,),
        in_specs=[_row_spec(bm, d), _row_spec(bm, d)], out_specs=_row_spec(bm, d), compiler_params=_params(1))(dy, ypre)


def _s5_scan_bwd(name, dyp, cc_re, cc_im, ab_re, ab_im, s_re, s_im, rider=None):
    t, d = dyp.shape
    nsg, ns, cs = cc_re.shape
    tb = _s5_time_block(t)
    nb = t // tb

    def body(dy_ref, ccr_hbm, cci_hbm, ar_ref, ai_ref, sr_ref, si_ref, lrm_ref, lim_ref, dar_ref, dai_ref,
             ccr, cci, lr_ref, li_ref, cr_ref, ci_ref):
        @pl.when(pl.program_id(0) == 0)
        def _():
            pltpu.sync_copy(ccr_hbm, ccr)
            pltpu.sync_copy(cci_hbm, cci)
            cr_ref[...] = jnp.zeros_like(cr_ref)
            ci_ref[...] = jnp.zeros_like(ci_ref)
            dar_ref[...] = jnp.zeros_like(dar_ref)
            dai_ref[...] = jnp.zeros_like(dai_ref)

        for sg in range(nsg):
            dys = dy_ref[:, sg * cs:(sg + 1) * cs]
            lr_ref[:, sg, :] = _dot(dys, ccr[sg], NT)
            li_ref[:, sg, :] = -_dot(dys, cci[sg], NT)
        ar, ai = ar_ref[...], ai_ref[...]

        def step(i, carry):
            cr, ci, dar, dai = carry
            j = tb - 1 - i
            sr, si = sr_ref[j], si_ref[j]
            dar = dar + (cr * sr + ci * si)
            dai = dai + (ci * sr - cr * si)
            nr = lr_ref[j] + (ar * cr + ai * ci)
            ni = li_ref[j] + (ar * ci - ai * cr)
            lr_ref[j] = nr
            li_ref[j] = ni
            return nr, ni, dar, dai

        cr, ci, dar, dai = lax.fori_loop(0, tb, step, (cr_ref[...], ci_ref[...], dar_ref[...], dai_ref[...]))
        cr_ref[...], ci_ref[...] = cr, ci
        dar_ref[...], dai_ref[...] = dar, dai
        lrm_ref[...] = jnp.swapaxes(lr_ref[...], 0, 1).astype(MXU_DTYPE)
        lim_ref[...] = jnp.swapaxes(li_ref[...], 0, 1).astype(MXU_DTYPE)

    hbm = pl.BlockSpec(memory_space=pltpu.HBM)
    full = pl.BlockSpec((nsg, ns), lambda i: (0, 0))
    mxu = jax.ShapeDtypeStruct((nsg, t, ns), MXU_DTYPE)
    acc = jax.ShapeDtypeStruct((nsg, ns), F32)
    scan_spec = pl.BlockSpec((tb, nsg, ns), lambda i: (nb - 1 - i, 0, 0))
    return _call(
        body, name=name, out_shape=[mxu, mxu, acc, acc], grid=(nb,),
        in_specs=[pl.BlockSpec((tb, d), lambda i: (nb - 1 - i, 0)), hbm, hbm, full, full, scan_spec, scan_spec],
        out_specs=[pl.BlockSpec((nsg, tb, ns), lambda i: (0, nb - 1 - i, 0))] * 2 + [full, full],
        scratch_shapes=[pltpu.VMEM(cc_re.shape, cc_re.dtype), pltpu.VMEM(cc_im.shape, cc_im.dtype),
                        pltpu.VMEM((tb, nsg, ns), F32), pltpu.VMEM((tb, nsg, ns), F32),
                        pltpu.VMEM((nsg, ns), F32), pltpu.VMEM((nsg, ns), F32)],
        args=(dyp, cc_re, cc_im, ab_re, ab_im, s_re, s_im), rider=rider)


def _s5_grads(name, lam_re, lam_im, s_re, s_im, u, dyp, bb_re, bb_im, dskip, rider=None):
    nsg, t, ns = lam_re.shape
    d = u.shape[1]
    cs = bb_re.shape[1]
    tb = _tile(t, 512, SUBLANE)

    def body(lr_ref, li_ref, sr_ref, si_ref, u_ref, dy_ref, bbr_ref, bbi_ref, d_ref,
             du_ref, dbbr_ref, dbbi_ref, dccr_ref, dcci_ref, dd_ref):
        @pl.when(pl.program_id(1) == 0)
        def _():
            for r in (dbbr_ref, dbbi_ref, dccr_ref, dcci_ref, dd_ref):
                r[...] = jnp.zeros_like(r)

        lr, li, uv, dy = lr_ref[...], li_ref[...], u_ref[...], dy_ref[...]
        dyf = dy.astype(F32)
        du_ref[...] = _dot(lr, bbr_ref[...], NT) + _dot(li, bbi_ref[...], NT) + d_ref[...] * dyf
        dbbr_ref[...] += _dot(uv, lr, TN)
        dbbi_ref[...] += _dot(uv, li, TN)
        dccr_ref[...] += _dot(sr_ref[...], dy, TN)
        dcci_ref[...] -= _dot(si_ref[...], dy, TN)
        dd_ref[...] += jnp.sum(dyf * uv, axis=0, keepdims=True)

    s_spec = pl.BlockSpec((None, tb, ns), lambda sg, i: (sg, i, 0))
    col = pl.BlockSpec((tb, cs), lambda sg, i: (i, sg))
    b_spec = pl.BlockSpec((None, cs, ns), lambda sg, i: (sg, 0, 0))
    c_spec = pl.BlockSpec((None, ns, cs), lambda sg, i: (sg, 0, 0))
    vec = pl.BlockSpec((1, cs), lambda sg, i: (0, sg))
    return _call(
        body, name=name,
        out_shape=[jax.ShapeDtypeStruct((t, d), F32), jax.ShapeDtypeStruct(bb_re.shape, F32),
                   jax.ShapeDtypeStruct(bb_re.shape, F32), jax.ShapeDtypeStruct((nsg, ns, cs), F32),
                   jax.ShapeDtypeStruct((nsg, ns, cs), F32), jax.ShapeDtypeStruct((1, d), F32)],
        grid=(nsg, t // tb), in_specs=[s_spec, s_spec, s_spec, s_spec, col, col, b_spec, b_spec, vec],
        out_specs=[col, b_spec, b_spec, c_spec, c_spec, vec],
        args=(lam_re, lam_im, s_re, s_im, u, dyp, bb_re, bb_im, dskip), rider=rider)


def _shift_down(x, k, prev8):
    if k == 0:
        return x
    ext = jnp.concatenate([prev8, x], axis=0)
    return ext[SUBLANE - k:SUBLANE - k + x.shape[0]]


def _shift_up(x, k, next8):
    if k == 0:
        return x
    ext = jnp.concatenate([x, next8], axis=0)
    return ext[k:k + x.shape[0]]


def _lru_time_block(t):
    return _tile(t, 256, SUBLANE)


def _lru_gates(xp, prev8, cv_ref, wrg, wig):
    taps = cv_ref.shape[0] - 4
    row = lambda k: cv_ref[k:k + 1, :]
    xs = [_shift_down(xp, taps - 1 - k, prev8) for k in range(taps)]
    xb = row(taps)
    for k in range(taps):
        xb = xb + row(k) * xs[k]
    r = jax.nn.sigmoid(_dot(xb, wrg, NN) + row(taps + 1))
    ig = jax.nn.sigmoid(_dot(xb, wig, NN) + row(taps + 2))
    sp = jax.nn.softplus(-row(taps + 3))
    log_a = -LRU_C * r * sp
    a = jnp.exp(log_a)
    mult = jnp.sqrt(_neg_expm1(2.0 * log_a))
    return xs, xb, r, ig, sp, a, mult


def _lru_fwd(name, zz, cvec, wrg, wig, rider=None):
    _, t, w = zz.shape
    half = N_DEV // 2
    tb = _lru_time_block(t)

    def body(gb_ref, xp_ref, xprev_ref, cv_ref, wrg_ref, wig_ref, hs_ref, y_ref, a_scr, b_scr, carry):
        i = pl.program_id(1)

        @pl.when(i == 0)
        def _():
            carry[...] = jnp.zeros_like(carry)

        prev8 = jnp.where(i > 0, xprev_ref[...], 0.0)
        _, xb, _, ig, _, a, mult = _lru_gates(xp_ref[...], prev8, cv_ref, wrg_ref[...], wig_ref[...])
        a_scr[...] = a
        b_scr[...] = mult * (ig * xb)

        def step(j, h):
            h = a_scr[pl.ds(j, 1), :] * h + b_scr[pl.ds(j, 1), :]
            hs_ref[pl.ds(j, 1), :] = h
            return h

        carry[0:1, :] = lax.fori_loop(0, tb, step, carry[0:1, :], unroll=8)
        y_ref[...] = (hs_ref[...] * _gelu(gb_ref[...])).astype(y_ref.dtype)

    nrow = tb // SUBLANE
    blk = lambda off: pl.BlockSpec((None, tb, w), lambda q, i: (q + off, i, 0))
    return _call(
        body, name=name,
        out_shape=[jax.ShapeDtypeStruct((half, t, w), F32), jax.ShapeDtypeStruct((half, t, w), MXU_DTYPE)],
        grid=(half, t // tb),
        in_specs=[blk(0), blk(half),
                  pl.BlockSpec((None, SUBLANE, w), lambda q, i: (q + half, jnp.maximum(i * nrow - 1, 0), 0)),
                  pl.BlockSpec((None,) + cvec.shape[1:], lambda q, i: (q, 0, 0)),
                  pl.BlockSpec((None, w, w), lambda q, i: (q, 0, 0)), pl.BlockSpec((None, w, w), lambda q, i: (q, 0, 0))],
        out_specs=[blk(0), blk(0)],
        scratch_shapes=[pltpu.VMEM((tb, w), F32), pltpu.VMEM((tb, w), F32), pltpu.VMEM((SUBLANE, w), F32)],
        args=(zz, zz, zz, cvec, wrg, wig), rider=rider)


def _lru_bwd(name, zz, hs, dy, cvec, wrg, wig, rider=None):
    _, t, w = zz.shape
    half = N_DEV // 2
    tb = _lru_time_block(t)
    nb = t // tb
    taps = cvec.shape[1] - 4

    def body(gb_ref, xp_ref, xprev_ref, hs_ref, hprev_ref, dy_ref, cv_ref, wrg_ref, wig_ref,
             dgb_ref, dxp_ref, dcv_ref, dwrg_ref, dwig_ref, a_scr, l_scr, carry, dxb_next):
        i = pl.program_id(1)

        @pl.when(i == 0)
        def _():
            for r_ in (carry, dxb_next, dcv_ref, dwrg_ref, dwig_ref):
                r_[...] = jnp.zeros_like(r_)

        has_prev = i < nb - 1
        row = lambda k: cv_ref[k:k + 1, :]
        prev8 = jnp.where(has_prev, xprev_ref[...], 0.0)
        xs, xb, r, ig, sp, a, mult = _lru_gates(xp_ref[...], prev8, cv_ref, wrg_ref[...], wig_ref[...])
        hs_ = hs_ref[...]
        hs_m1 = _shift_down(hs_, 1, jnp.where(has_prev, hprev_ref[...], 0.0))
        gel, dgel = _gelu_and_grad(gb_ref[...])
        dy_ = dy_ref[...]
        dgb_ref[...] = (dy_ * hs_ * dgel).astype(dgb_ref.dtype)
        a_scr[...] = a
        l_scr[...] = dy_ * gel

        def step(k, c):
            j = tb - 1 - k
            lam = l_scr[pl.ds(j, 1), :] + c
            l_scr[pl.ds(j, 1), :] = lam
            return a_scr[pl.ds(j, 1), :] * lam

        carry[0:1, :] = lax.fori_loop(0, tb, step, carry[0:1, :], unroll=8)
        lam = l_scr[...]
        dmult = lam * (ig * xb)
        dig = lam * (mult * xb)
        dxb = lam * (mult * ig)
        dlog_a = (lam * hs_m1) * a - dmult * (a * a) / mult
        dr = dlog_a * (-LRU_C * sp)
        dsp = jnp.sum(dlog_a * (-LRU_C * r), axis=0, keepdims=True)
        dpr = dr * (r * (1.0 - r))
        dpi = dig * (ig * (1.0 - ig))
        dwrg_ref[...] += _dot(xb, dpr, TN)
        dwig_ref[...] += _dot(xb, dpi, TN)
        dxb = dxb + _dot(dpr, wrg_ref[...], NT) + _dot(dpi, wig_ref[...], NT)
        for k in range(taps):
            dcv_ref[k:k + 1, :] += jnp.sum(dxb * xs[k], axis=0, keepdims=True)
        dcv_ref[taps:taps + 1, :] += jnp.sum(dxb, axis=0, keepdims=True)
        dcv_ref[taps + 1:taps + 2, :] += jnp.sum(dpr, axis=0, keepdims=True)
        dcv_ref[taps + 2:taps + 3, :] += jnp.sum(dpi, axis=0, keepdims=True)
        dcv_ref[taps + 3:taps + 4, :] += dsp * (-jax.nn.sigmoid(-row(taps + 3)))
        nxt8 = dxb_next[...]
        dxp = row(taps - 1) * dxb
        for k in range(taps - 1):
            dxp = dxp + row(k) * _shift_up(dxb, taps - 1 - k, nxt8)
        dxp_ref[...] = dxp.astype(dxp_ref.dtype)
        dxb_next[...] = dxb[0:SUBLANE]

    nrow = tb // SUBLANE
    blk = lambda off: pl.BlockSpec((None, tb, w), lambda q, i: (q + off, nb - 1 - i, 0))
    halo = lambda off: pl.BlockSpec((None, SUBLANE, w), lambda q, i: (q + off, jnp.maximum((nb - 1 - i) * nrow - 1, 0), 0))
    wspec = pl.BlockSpec((None, w, w), lambda q, i: (q, 0, 0))
    cspec = pl.BlockSpec((None,) + cvec.shape[1:], lambda q, i: (q, 0, 0))
    act = jax.ShapeDtypeStruct((half, t, w), MXU_DTYPE)
    return _call(
        body, name=name,
        out_shape=[act, act, jax.ShapeDtypeStruct(cvec.shape, F32), jax.ShapeDtypeStruct(wrg.shape, F32),
                   jax.ShapeDtypeStruct(wig.shape, F32)],
        grid=(half, nb),
        in_specs=[blk(0), blk(half), halo(half), blk(0), halo(0), blk(0), cspec, wspec, wspec],
        out_specs=[blk(0), blk(0), cspec, wspec, wspec],
        scratch_shapes=[pltpu.VMEM((tb, w), F32), pltpu.VMEM((tb, w), F32), pltpu.VMEM((SUBLANE, w), F32),
                        pltpu.VMEM((SUBLANE, w), F32)],
        args=(zz, zz, zz, hs, hs, dy, cvec, wrg, wig), rider=rider)


def _band(name, blocks, per, dtype):
    n, a, b = blocks.shape

    def body(x_ref, o_ref):
        o_ref[...] = jnp.zeros_like(o_ref)
        for g in range(per):
            o_ref[g * a:(g + 1) * a, g * b:(g + 1) * b] = x_ref[g].astype(o_ref.dtype)

    return pl.pallas_call(
        body, name=name, out_shape=jax.ShapeDtypeStruct((n // per, per * a, per * b), dtype), grid=(n // per,),
        in_specs=[pl.BlockSpec((per, a, b), lambda s: (s, 0, 0))],
        out_specs=pl.BlockSpec((None, per * a, per * b), lambda s: (s, 0, 0)), compiler_params=_params(1))(blocks)


def _unband(name, bands, per):
    s, pa, pb = bands.shape
    a, b = pa // per, pb // per

    def body(x_ref, o_ref):
        for g in range(per):
            o_ref[g] = x_ref[g * a:(g + 1) * a, g * b:(g + 1) * b]

    return pl.pallas_call(
        body, name=name, out_shape=jax.ShapeDtypeStruct((s * per, a, b), bands.dtype), grid=(s,),
        in_specs=[pl.BlockSpec((None, pa, pb), lambda i: (i, 0, 0))],
        out_specs=pl.BlockSpec((per, a, b), lambda i: (i, 0, 0)), compiler_params=_params(1))(bands)


def _pack(arrays, rows_multiple, lanes=LANE):
    flat = [a.reshape(-1).astype(F32) for a in arrays]
    size = sum(a.shape[0] for a in flat)
    rows = -(-size // (lanes * rows_multiple)) * rows_multiple
    if rows * lanes > size:
        flat.append(jnp.zeros((rows * lanes - size,), F32))
    return jnp.concatenate(flat).reshape(rows, lanes)


def _unpack(packed, shapes):
    flat = packed.reshape(-1)
    out, off = [], 0
    for s in shapes:
        n = math.prod(s)
        out.append(flat[off:off + n].reshape(s))
        off += n
    return out


def kernel(x, c, norm_g, w_ada, b_ada, s5_w_in, s5_lam_re, s5_lam_im, s5_log_dt, s5_b_re, s5_b_im, s5_c_re, s5_c_im, s5_d, s5_w_glu, lru_w_in, lru_conv_w, lru_conv_b, lru_w_rg, lru_b_rg, lru_w_ig, lru_b_ig, lru_lam, lru_w_out, ffn_w_gu, ffn_w_down, final_g, loss_target, m_norm_g, m_w_ada, m_b_ada, m_s5_w_in, m_s5_lam_re, m_s5_lam_im, m_s5_log_dt, m_s5_b_re, m_s5_b_im, m_s5_c_re, m_s5_c_im, m_s5_d, m_s5_w_glu, m_lru_w_in, m_lru_conv_w, m_lru_conv_b, m_lru_w_rg, m_lru_b_rg, m_lru_w_ig, m_lru_b_ig, m_lru_lam, m_lru_w_out, m_ffn_w_gu, m_ffn_w_down, m_final_g, v_norm_g, v_w_ada, v_b_ada, v_s5_w_in, v_s5_lam_re, v_s5_lam_im, v_s5_log_dt, v_s5_b_re, v_s5_b_im, v_s5_c_re, v_s5_c_im, v_s5_d, v_s5_w_glu, v_lru_w_in, v_lru_conv_w, v_lru_conv_b, v_lru_w_rg, v_lru_b_rg, v_lru_w_ig, v_lru_b_ig, v_lru_lam, v_lru_w_out, v_ffn_w_gu, v_ffn_w_down, v_final_g):
    wv = dict(zip(WEIGHTS, (norm_g, w_ada, b_ada, s5_w_in, s5_lam_re, s5_lam_im, s5_log_dt, s5_b_re, s5_b_im, s5_c_re, s5_c_im, s5_d, s5_w_glu, lru_w_in, lru_conv_w, lru_conv_b, lru_w_rg, lru_b_rg, lru_w_ig, lru_b_ig, lru_lam, lru_w_out, ffn_w_gu, ffn_w_down, final_g)))
    mv = dict(zip(WEIGHTS, (m_norm_g, m_w_ada, m_b_ada, m_s5_w_in, m_s5_lam_re, m_s5_lam_im, m_s5_log_dt, m_s5_b_re, m_s5_b_im, m_s5_c_re, m_s5_c_im, m_s5_d, m_s5_w_glu, m_lru_w_in, m_lru_conv_w, m_lru_conv_b, m_lru_w_rg, m_lru_b_rg, m_lru_w_ig, m_lru_b_ig, m_lru_lam, m_lru_w_out, m_ffn_w_gu, m_ffn_w_down, m_final_g)))
    vv = dict(zip(WEIGHTS, (v_norm_g, v_w_ada, v_b_ada, v_s5_w_in, v_s5_lam_re, v_s5_lam_im, v_s5_log_dt, v_s5_b_re, v_s5_b_im, v_s5_c_re, v_s5_c_im, v_s5_d, v_s5_w_glu, v_lru_w_in, v_lru_conv_w, v_lru_conv_b, v_lru_w_rg, v_lru_b_rg, v_lru_w_ig, v_lru_b_ig, v_lru_lam, v_lru_w_out, v_ffn_w_gu, v_ffn_w_down, v_final_g)))

    me = 4 * lax.axis_index("x") + 2 * lax.axis_index("y") + lax.axis_index("c")
    x0 = x[0]
    tgt = loss_target[0]
    t, d = x0.shape
    depth = norm_g.shape[0]
    n_mod = w_ada.shape[2] * N_DEV // d
    groups, states = s5_lam_re.shape[1], s5_lam_re.shape[2]
    per_sg = S5_SUPER // S5_GROUP
    nsg = groups // per_sg
    lw = lru_lam.shape[1] * N_DEV
    lwc = lw // (N_DEV // 2)
    half = N_DEV // 2

    assert depth == 2, "the ride schedule below is written for one S5 layer followed by one RG-LRU layer"
    wire = lambda a: a.astype(WIRE_DTYPE)
    gw = {'s5_in': _all_gather("ag_s5_w_in", wire(s5_w_in[0]))}

    def riding(job, fn, *args):
        res, (got,) = fn(*args, rider=_gather_rider([wire(job[1])]))
        gw[job[0]] = got
        return res

    sh_shapes = [wv[n].shape for n in SMALL_SHARDED] + [c.shape]
    sh_all = _all_gather("ag_small", _pack([wv[n] for n in SMALL_SHARDED] + [c], SUBLANE))
    sh_parts = [jnp.stack(p) for p in zip(*[_unpack(sh_all[s], sh_shapes) for s in range(N_DEV)])]
    full = {}
    for n, p in zip(SMALL_SHARDED, sh_parts[:-1]):
        full[n] = jnp.moveaxis(p, 0, -2).reshape(p.shape[1:-1] + (-1,))
    c_all = sh_parts[-1].reshape(N_DEV, d)
    c16 = jnp.pad(c_all, ((0, 2 * SUBLANE - N_DEV), (0, 0)))

    n_loc = w_ada.shape[2]
    b_loc = lax.dynamic_slice_in_dim(b_ada, me * n_loc, n_loc, axis=1)[:, None, :]
    mod_part = _ada_fwd("ada_fwd", c16, w_ada, b_loc)[:, :N_DEV]
    mod_mine = _chunk_exchange("x_mod", [mod_part.transpose(1, 0, 2)], ALL)
    mod = mod_mine.transpose(1, 0, 2).reshape(depth, n_mod, 1, d)

    lam3 = lambda a: a[0][:, None, :]
    p_lr, p_li, p_ld = lam3(s5_lam_re), lam3(s5_lam_im), s5_log_dt[0][:, None, None]
    p_br, p_bi = s5_b_re[0].transpose(0, 2, 1), s5_b_im[0].transpose(0, 2, 1)
    ab_re3, ab_im3, bb_re3, bb_im3 = _s5_disc("s5_disc", p_lr, p_li, p_ld, p_br, p_bi)
    ab_re, ab_im = ab_re3.reshape(nsg, per_sg * states), ab_im3.reshape(nsg, per_sg * states)
    bb_re = _band("band_bb_re", bb_re3, per_sg, MXU_DTYPE)
    bb_im = _band("band_bb_im", bb_im3, per_sg, MXU_DTYPE)
    cc_re = _band("band_cc_re", s5_c_re[0].transpose(0, 2, 1), per_sg, MXU_DTYPE)
    cc_im = _band("band_cc_im", s5_c_im[0].transpose(0, 2, 1), per_sg, MXU_DTYPE)

    taps = lru_conv_w.shape[1]
    cvec = jnp.concatenate([full['lru_conv_w'].reshape(taps, lw), full['lru_conv_b'], full['lru_b_rg'],
                            full['lru_b_ig'], full['lru_lam']], axis=0)
    cvec = cvec.reshape(taps + 4, half, lwc).transpose(1, 0, 2)
    wrg = _band("band_w_rg", lru_w_rg[0], LRU_BLOCKS_PER_CHUNK, MXU_DTYPE)
    wig = _band("band_w_ig", lru_w_ig[0], LRU_BLOCKS_PER_CHUNK, MXU_DTYPE)

    saved = []
    xc = x0
    for i in range(depth):
        sh1, sc1, g1, sh2, sc2, g2 = [mod[i, k] for k in range(n_mod)]
        gn = full['norm_g'][i]
        h1 = _norm_mod_fwd(f"norm1_fwd{i}", xc, gn[0:1], sc1, sh1)
        if i % 2 == 0:
            u = riding(('s5_glu', s5_w_glu[0]), _mm_row, f"s5_in{i}", h1[None], gw['s5_in'].reshape(d, d))
            s_re, s_im, s_rem, s_imm = riding((('gu', i), ffn_w_gu[i]), _s5_scan_fwd, f"s5_scan{i}", u, bb_re, bb_im,
                                              ab_re, ab_im)
            ypre, yact = _s5_out_fwd(f"s5_out{i}", s_rem, s_imm, cc_re, cc_im, u, s5_d)
            z = riding((('down', i), ffn_w_down[i]), _mm_col, f"s5_glu{i}", yact, gw['s5_glu'])
            x1 = _glu_resid_fwd(f"s5_resid{i}", z, xc, g1)
            mix = (u, s_re, s_im, s_rem, s_imm, ypre, yact, z)
        else:
            zz = _mm_col(f"lru_in{i}", h1, gw['lru_in'])
            hs, ylru = riding((('gu', i), ffn_w_gu[i]), _lru_fwd, f"lru_core{i}", zz, cvec, wrg, wig)
            o = _mm_row(f"lru_out{i}", ylru, gw['lru_out'].reshape(lw, d))
            x1 = _resid(f"lru_resid{i}", xc, o, g1)
            mix = (zz, hs, ylru, o)
        h2 = _norm_mod_fwd(f"norm2_fwd{i}", x1, gn[1:2], sc2, sh2)
        if i % 2 == 0:
            gu = riding(('lru_in', lru_w_in[0]), _mm_col, f"ffn_gu{i}", h2, gw['gu', i], MXU_DTYPE)
            act = _swiglu_act_fwd(f"ffn_act{i}", gu)
            f = riding(('lru_out', lru_w_out[0]), _mm_row, f"ffn_down{i}", act, gw['down', i].reshape(-1, d))
        else:
            gu = riding((('down', i), ffn_w_down[i]), _mm_col, f"ffn_gu{i}", h2, gw['gu', i], MXU_DTYPE)
            act = _swiglu_act_fwd(f"ffn_act{i}", gu)
            f = _mm_row(f"ffn_down{i}", act, gw['down', i].reshape(-1, d))
        x2 = _resid(f"ffn_resid{i}", x1, f, g2)
        saved.append((xc, h1, mix, x1, h2, gu, act, f))
        xc = x2

    dx, loss_part, d_final_g = _loss_bwd("loss", xc, tgt, final_g[None])
    loss = lax.psum(loss_part[0, 0], ("x", "y", "c"))

    grads = {}
    parts = {}
    dmod = [None] * depth
    d_norm_g = [None] * depth
    core = lax.axis_index("c").astype(jnp.int32).reshape(1)
    chunked = lambda p: p.reshape(N_DEV, -1, p.shape[-1])
    to_sibling = lambda p: _sibling_rider(chunked(p))
    pair = lambda name, p, got: _pair_sum(name, chunked(p), got, core)
    over_ici = lambda sums: _chunk_rider([sums], SAME_CORE)

    above = None
    for i in reversed(range(depth)):
        xin, h1, mix, x1, h2, gu, act, f = saved[i]
        sh1, sc1, g1, sh2, sc2, g2 = [mod[i, k] for k in range(n_mod)]
        gn = full['norm_g'][i]
        g_down = gw['down', i].reshape(-1, d)
        df, dg2 = _gate_bwd(f"ffn_gate_bwd{i}", dx, f, g2)
        if above is None:
            dact = _mm_row_da(f"ffn_down_da{i}", df, g_down, half)
        else:
            dact, (got,) = _ride(_mm_row_da, f"ffn_down_da{i}", df, g_down, half, riders=[to_sibling(above[1])])
            s_above = pair(f"x_{above[0][0]}_pair", above[1], got)
        p_down = _mm_row_db(f"ffn_down_db{i}", act, df, WIRE_DTYPE)
        dgu, (got,) = _ride(_swiglu_act_bwd, f"ffn_act_bwd{i}", gu, dact, riders=[to_sibling(p_down)])
        s_down = pair(f"x_ffn_w_down{i}_pair", p_down, got)
        if above is None:
            dh2, (parts['ffn_w_down', i],) = _ride(_mm_col_da, f"ffn_gu_da{i}", dgu, gw['gu', i], riders=[over_ici(s_down)])
            p_gu = _mm_col_db(f"ffn_gu_db{i}", h2, dgu, WIRE_DTYPE)
        else:
            dh2, (parts[above[0]],) = _ride(_mm_col_da, f"ffn_gu_da{i}", dgu, gw['gu', i], riders=[over_ici(s_above)])
            p_gu, (parts['ffn_w_down', i],) = _ride(_mm_col_db, f"ffn_gu_db{i}", h2, dgu, WIRE_DTYPE,
                                                    riders=[over_ici(s_down)])
        dx, dgn2, dsc2, dsh2 = _norm_mod_bwd(f"norm2_bwd{i}", x1, dh2, dx, gn[1:2], sc2)
        if i % 2 == 0:
            u, s_re, s_im, s_rem, s_imm, ypre, yact, z = mix
            dz, dg1 = _glu_resid_bwd(f"s5_resid_bwd{i}", z, dx, g1)
            dyact, (got,) = _ride(_mm_col_da, f"s5_glu_da{i}", dz, gw['s5_glu'], riders=[to_sibling(p_gu)])
            s_gu = pair(f"x_ffn_w_gu{i}_pair", p_gu, got)
            p_glu = _mm_col_db(f"s5_glu_db{i}", yact, dz, WIRE_DTYPE)
            dyp = _gelu_bwd(f"s5_gelu_bwd{i}", dyact, ypre)
            (l_rem, l_imm, dab_re, dab_im), (parts['ffn_w_gu', i], got) = _ride(
                _s5_scan_bwd, f"s5_scan_bwd{i}", dyp, cc_re, cc_im, ab_re, ab_im, s_re, s_im,
                riders=[over_ici(s_gu), to_sibling(p_glu)])
            s_glu = pair("x_s5_w_glu_pair", p_glu, got)
            (du, dbb_re, dbb_im, dcc_re, dcc_im, dd), (parts['s5_w_glu', 0],) = _ride(
                _s5_grads, f"s5_grads{i}", l_rem, l_imm, s_rem, s_imm, u, dyp, bb_re, bb_im, s5_d, riders=[over_ici(s_glu)])
            dlr, dli, dld, dbr, dbi = _s5_disc_bwd(
                "s5_disc_bwd", p_lr, p_li, p_ld, p_br, p_bi, dab_re.reshape(groups, 1, states),
                dab_im.reshape(groups, 1, states), _unband("unband_bb_re", dbb_re, per_sg),
                _unband("unband_bb_im", dbb_im, per_sg))
            grads['s5_lam_re'], grads['s5_lam_im'], grads['s5_log_dt'] = dlr[:, 0][None], dli[:, 0][None], dld[:, 0, 0][None]
            grads['s5_b_re'], grads['s5_b_im'] = dbr.transpose(0, 2, 1)[None], dbi.transpose(0, 2, 1)[None]
            grads['s5_c_re'] = _unband("unband_cc_re", dcc_re, per_sg).transpose(0, 2, 1)[None]
            grads['s5_c_im'] = _unband("unband_cc_im", dcc_im, per_sg).transpose(0, 2, 1)[None]
            grads['s5_d'] = dd
            dub = du.astype(MXU_DTYPE)
            p_s5_in = _mm_row_db(f"s5_in_db{i}", h1[None], dub, WIRE_DTYPE)
            dh1, (got,) = _ride(_mm_row_da, f"s5_in_da{i}", dub, gw['s5_in'].reshape(d, d), 1, riders=[to_sibling(p_s5_in)])
            dh1 = dh1[0]
            s_s5_in = pair("x_s5_w_in_pair", p_s5_in, got)
        else:
            zz, hs, ylru, o = mix
            g_lru_out = gw['lru_out'].reshape(lw, d)
            do, dg1 = _gate_bwd(f"lru_gate_bwd{i}", dx, o, g1)
            dyl, (got,) = _ride(_mm_row_da, f"lru_out_da{i}", do, g_lru_out, half, riders=[to_sibling(p_gu)])
            s_gu = pair(f"x_ffn_w_gu{i}_pair", p_gu, got)
            p_lru_out = _mm_row_db(f"lru_out_db{i}", ylru, do, WIRE_DTYPE)
            (dgb, dxp, dcv, dwrg, dwig), (parts['ffn_w_gu', i], got) = _ride(
                _lru_bwd, f"lru_core_bwd{i}", zz, hs, dyl, cvec, wrg, wig, riders=[over_ici(s_gu), to_sibling(p_lru_out)])
            s_lru_out = pair("x_lru_w_out_pair", p_lru_out, got)
            dzz = jnp.concatenate([dgb, dxp], axis=0)
            dh1, (parts['lru_w_out', 0],) = _ride(_mm_col_da, f"lru_in_da{i}", dzz, gw['lru_in'],
                                                  riders=[over_ici(s_lru_out)])
            above = (('lru_w_in', 0), _mm_col_db(f"lru_in_db{i}", h1, dzz, WIRE_DTYPE))
            dcv = dcv.transpose(1, 0, 2).reshape(taps + 4, lw)
            grads['lru_conv_w'] = dcv[:taps].reshape(1, taps, 1, lw)
            grads['lru_conv_b'], grads['lru_b_rg'] = dcv[taps:taps + 1], dcv[taps + 1:taps + 2]
            grads['lru_b_ig'], grads['lru_lam'] = dcv[taps + 2:taps + 3], dcv[taps + 3:taps + 4]
            grads['lru_w_rg'] = _unband("unband_w_rg", dwrg, LRU_BLOCKS_PER_CHUNK)[None]
            grads['lru_w_ig'] = _unband("unband_w_ig", dwig, LRU_BLOCKS_PER_CHUNK)[None]
        dx, dgn1, dsc1, dsh1 = _norm_mod_bwd(f"norm1_bwd{i}", xin, dh1, dx, gn[0:1], sc1)
        dmod[i] = jnp.concatenate([dsh1, dsc1, dg1, dsh2, dsc2, dg2], axis=1)
        d_norm_g[i] = jnp.concatenate([dgn1, dgn2], axis=0)
    grad_x = dx[None]
    dmod = jnp.concatenate(dmod, axis=0)
    grads['norm_g'] = jnp.stack(d_norm_g)
    grads['b_ada'] = dmod
    grads['final_g'] = d_final_g[0]

    small_partial = _pack([grads[n] for n in SMALL], SUBLANE * N_DEV)
    rows8 = small_partial.shape[0] // N_DEV
    small_partial = small_partial.reshape(N_DEV, rows8, LANE)
    s_small = pair("x_small_pair", small_partial, _ride_alone("x_small_d2d", _sibling_rider(small_partial)))
    parts['s5_w_in', 0], small_parts = _ride_alone("x_tail_ici", _join([over_ici(s_s5_in), over_ici(s_small)]))

    out = {}
    dmod_all = _all_gather("ag_dmod", dmod)
    dmod_loc = lax.dynamic_slice_in_dim(dmod_all, me * n_loc, n_loc, axis=2).transpose(1, 0, 2)
    dmod16 = jnp.pad(dmod_loc, ((0, 0), (0, 2 * SUBLANE - N_DEV), (0, 0)))
    out['w_ada'] = _adamw_w_ada("adamw_w_ada", c16, dmod16, w_ada, m_w_ada, v_w_ada)

    for name in BIG[1:]:
        w = wv[name]
        rows, cols = w.shape[-2] * w.shape[0], w.shape[-1]
        flat = lambda a: a.reshape(rows, cols)
        res = _adamw_sum("adamw_" + name, [parts[name, l] for l in range(w.shape[0])], flat(w), flat(mv[name]),
                         flat(vv[name]))
        out[name] = [r.reshape(w.shape) for r in res]

    summed = _sum_parts("sum_small", small_parts)
    small_total = _all_gather("ag_small_sum", summed).reshape(-1, LANE)
    small_grad = dict(zip(SMALL, _unpack(small_total, [grads[n].shape for n in SMALL])))
    for n in SMALL_SHARDED:
        shard = wv[n].shape[-1]
        small_grad[n] = lax.dynamic_slice_in_dim(small_grad[n], me * shard, shard, axis=small_grad[n].ndim - 1)
    for n in SMALL:
        w = wv[n]
        flat = lambda a: a.reshape(-1, w.shape[-1])
        res = _adamw_sum("adamw_" + n, [flat(small_grad[n])[None]], flat(w), flat(mv[n]), flat(vv[n]))
        out[n] = [r.reshape(w.shape) for r in res]

    return (loss, grad_x, *[out[n][0] for n in WEIGHTS], *[out[n][1] for n in WEIGHTS],
            *[out[n][2] for n in WEIGHTS], *[out[n][3] for n in WEIGHTS])
```

```python
import functools
import math

import jax
import jax.numpy as jnp
from jax import lax
from jax.experimental import pallas as pl
from jax.experimental.pallas import tpu as pltpu

F32 = jnp.float32
MXU_DTYPE = jnp.bfloat16
WIRE_DTYPE = jnp.bfloat16
N_DEV = 8
EPS = 1e-6
LRU_C = 8.0
S5_GROUP = 16
S5_STATE = 64
S5_SUPER = 256
LRU_BLOCKS_PER_CHUNK = 4
ADAM_LR, ADAM_B1, ADAM_B2, ADAM_EPS, ADAM_WD, ADAM_STEP = 0.001, 0.9, 0.999, 1e-08, 0.01, 10
VMEM_LIMIT_BYTES = 56 * 1024 * 1024
LANE = 128
SUBLANE = 8

WEIGHTS = ['norm_g', 'w_ada', 'b_ada', 's5_w_in', 's5_lam_re', 's5_lam_im', 's5_log_dt', 's5_b_re', 's5_b_im',
           's5_c_re', 's5_c_im', 's5_d', 's5_w_glu', 'lru_w_in', 'lru_conv_w', 'lru_conv_b', 'lru_w_rg', 'lru_b_rg',
           'lru_w_ig', 'lru_b_ig', 'lru_lam', 'lru_w_out', 'ffn_w_gu', 'ffn_w_down', 'final_g']
BIG = ('w_ada', 's5_w_in', 's5_w_glu', 'lru_w_in', 'lru_w_out', 'ffn_w_gu', 'ffn_w_down')
SMALL = tuple(n for n in WEIGHTS if n not in BIG)
SMALL_SHARDED = ('norm_g', 'lru_conv_w', 'lru_conv_b', 'lru_b_rg', 'lru_b_ig', 'lru_lam')

NN = (((1,), (0,)), ((), ()))
NT = (((1,), (1,)), ((), ()))
TN = (((0,), (0,)), ((), ()))


def _params(n_grid):
    return pltpu.CompilerParams(dimension_semantics=("arbitrary",) * n_grid, vmem_limit_bytes=VMEM_LIMIT_BYTES)


def _tile(dim, pref, align=LANE):
    if dim <= pref:
        return dim
    t = (pref // align) * align
    while t >= align:
        if dim % t == 0:
            return t
        t -= align
    return dim


def _dot(a, b, dims):
    return lax.dot_general(a.astype(MXU_DTYPE), b.astype(MXU_DTYPE), dims, preferred_element_type=F32)


def _gelu(x):
    k = math.sqrt(2.0 / math.pi)
    return 0.5 * x * (1.0 + jnp.tanh(k * (x + 0.044715 * (x * x * x))))


def _gelu_and_grad(x):
    k = math.sqrt(2.0 / math.pi)
    th = jnp.tanh(k * (x + 0.044715 * (x * x * x)))
    g = 0.5 * x * (1.0 + th)
    dg = 0.5 * (1.0 + th) + 0.5 * x * (1.0 - th * th) * (k * (1.0 + 3.0 * 0.044715 * (x * x)))
    return g, dg


def _neg_expm1(x):
    series = -x * (1.0 + x * (0.5 + x * (1.0 / 6.0 + x * (1.0 / 24.0 + x * (1.0 / 120.0)))))
    return jnp.where(x > -0.01, series, 1.0 - jnp.exp(x))


MESH = pl.DeviceIdType.MESH
N_CHIP = N_DEV // 2
ALL, SAME_CORE = 7, 6


def _place():
    x, y, c = lax.axis_index("x"), lax.axis_index("y"), lax.axis_index("c")
    return x, y, c


def _flip(place, k):
    x, y, c = place
    return (1 - x if (k >> 2) & 1 else x, 1 - y if (k >> 1) & 1 else y, 1 - c if k & 1 else c)


def _chunk_exchange(name, xs, group):
    return _ride_alone(name, _chunk_rider(xs, group))


class _Rider:
    def __init__(self, arrays, out_shape, scratch, start, finish, post, mid=None):
        self.arrays, self.out_shape, self.scratch = list(arrays), list(out_shape), list(scratch)
        self.start, self.finish, self.post, self.mid = start, finish, post, mid


def _chunk_rider(xs, group):
    n = len(xs)
    members, r, c_ = xs[0].shape
    assert members == {ALL: N_DEV, SAME_CORE: N_CHIP}[group]
    assert all(a.shape == xs[0].shape and a.dtype == xs[0].dtype for a in xs)
    ks = [k for k in range(1, N_DEV) if not k & ~group]
    member = (lambda p: 4 * p[0] + 2 * p[1] + p[2]) if group == ALL else (lambda p: 2 * p[0] + p[1])

    def copies(ins, outs, scratch):
        out = outs[0]
        send_sems, recv_sems, local_sems = scratch
        place = _place()
        me = member(place)
        local = [pltpu.make_async_copy(ins[l].at[me], out.at[me, l], local_sems.at[l]) for l in range(n)]
        remote = []
        for l in range(n):
            for k in ks:
                pid = _flip(place, k)
                peer = member(pid)

                def copy(land_at, l=l, k=k, peer=peer, pid=pid):
                    return pltpu.make_async_remote_copy(
                        src_ref=ins[l].at[peer], dst_ref=out.at[land_at, l], send_sem=send_sems.at[l * N_DEV + k],
                        recv_sem=recv_sems.at[l * N_DEV + k], device_id=pid, device_id_type=MESH)

                remote.append((copy, me, peer))
        return local, remote

    def start(ins, outs, scratch):
        local, remote = copies(ins, outs, scratch)
        for cp in local:
            cp.start()
        for copy, me, _ in remote:
            copy(me).start()

    def finish(ins, outs, scratch):
        local, remote = copies(ins, outs, scratch)
        for copy, me, peer in remote:
            copy(me).wait_send()
            copy(peer).wait_recv()
        for cp in local:
            cp.wait()

    return _Rider(
        xs, [jax.ShapeDtypeStruct((members, n, r, c_), xs[0].dtype)],
        [pltpu.SemaphoreType.DMA((n * N_DEV,)), pltpu.SemaphoreType.DMA((n * N_DEV,)), pltpu.SemaphoreType.DMA((n,))],
        start, finish, lambda outs: outs[0].reshape(members, n * r, c_))


def _gather_rider(xs):
    n = len(xs)
    per = 7

    def plan(ins, outs, scratch):
        send_sems, recv_sems, local_sems = scratch
        x, y, c = place = _place()
        sibling, x_nb, y_nb, diag = (x, y, 1 - c), (1 - x, y, c), (x, 1 - y, c), (1 - x, 1 - y, c)
        relayed = (x + c * (1 - 2 * x), y + (1 - c) * (1 - 2 * y), c)
        onward = (x + (1 - c) * (1 - 2 * x), y + c * (1 - 2 * y), c)
        jobs = []
        for l in range(n):
            slot = lambda p, l=l: outs[l].at[2 * p[0] + p[1], p[2]]

            def copy(k, block, to, src=None, l=l, slot=slot):
                return pltpu.make_async_remote_copy(
                    src_ref=slot(block) if src is None else src, dst_ref=slot(block), send_sem=send_sems.at[l * per + k],
                    recv_sem=recv_sems.at[l * per + k], device_id=to, device_id_type=MESH)

            jobs.append(dict(
                mine=pltpu.make_async_copy(ins[l], slot(place), local_sems.at[l]),
                first=[copy(0, place, sibling, src=ins[l]), copy(1, place, x_nb, src=ins[l]), copy(2, place, y_nb, src=ins[l])],
                landed=[copy(1, x_nb, place), copy(2, y_nb, place)],
                second=[copy(3, relayed, onward), copy(4, x_nb, sibling), copy(5, y_nb, sibling)],
                relay_landed=copy(3, diag, place), last=copy(6, diag, sibling),
                from_sibling=[copy(0, sibling, place)] + [copy(4 + j, (p[0], p[1], 1 - c), place)
                                                          for j, p in enumerate((x_nb, y_nb, diag))]))
        return jobs

    def start(ins, outs, scratch):
        for job in plan(ins, outs, scratch):
            job['mine'].start()
            for cp in job['first']:
                cp.start()

    def mid(ins, outs, scratch):
        for job in plan(ins, outs, scratch):
            for cp in job['landed']:
                cp.wait_recv()
            for cp in job['second']:
                cp.start()

    def finish(ins, outs, scratch):
        jobs = plan(ins, outs, scratch)
        for job in jobs:
            job['relay_landed'].wait_recv()
            job['last'].start()
        for job in jobs:
            for cp in job['from_sibling']:
                cp.wait_recv()
            for cp in job['first'] + job['second'] + [job['last']]:
                cp.wait_send()
            job['mine'].wait()

    return _Rider(
        xs, [jax.ShapeDtypeStruct((N_CHIP, 2) + x.shape, x.dtype) for x in xs],
        [pltpu.SemaphoreType.DMA((n * per,)), pltpu.SemaphoreType.DMA((n * per,)), pltpu.SemaphoreType.DMA((n,))],
        start, finish, lambda outs: [o.reshape((N_DEV,) + x.shape) for o, x in zip(outs, xs)], mid=mid)


HBM_SPEC = pl.BlockSpec(memory_space=pltpu.HBM)


def _ride_alone(name, rider):
    n_in, n_out = len(rider.arrays), len(rider.out_shape)

    def body(*refs):
        parts = refs[:n_in], refs[n_in:n_in + n_out], refs[n_in + n_out:]
        rider.start(*parts)
        if rider.mid is not None:
            rider.mid(*parts)
        rider.finish(*parts)

    outs = pl.pallas_call(body, name=name, out_shape=rider.out_shape, in_specs=[HBM_SPEC] * n_in,
                          out_specs=[HBM_SPEC] * n_out, scratch_shapes=rider.scratch)(*rider.arrays)
    return rider.post(list(outs))


def _call(body, *, name, grid, in_specs, out_specs, out_shape, scratch_shapes=(), args, rider=None):
    single = not isinstance(out_shape, (list, tuple))
    out_shape = [out_shape] if single else list(out_shape)
    out_specs = [out_specs] if single else list(out_specs)
    scratch_shapes = list(scratch_shapes)
    unwrap = lambda outs: outs[0] if single else list(outs)
    if rider is None:
        outs = pl.pallas_call(body, name=name, grid=grid, in_specs=list(in_specs), out_specs=out_specs, out_shape=out_shape,
                              scratch_shapes=scratch_shapes, compiler_params=_params(len(grid)))(*args)
        return unwrap(outs)
    n_in, n_out, n_scr = len(in_specs), len(out_shape), len(scratch_shapes)
    r_in, r_out = len(rider.arrays), len(rider.out_shape)

    def carried(*refs):
        ins, refs = refs[:n_in], refs[n_in:]
        r_ins, refs = refs[:r_in], refs[r_in:]
        outs, refs = refs[:n_out], refs[n_out:]
        r_outs, refs = refs[:r_out], refs[r_out:]
        scr, r_scr = refs[:n_scr], refs[n_scr:]
        step = 0
        for ax, g in enumerate(grid):
            step = step * g + pl.program_id(ax)

        @pl.when(step == 0)
        def _():
            rider.start(r_ins, r_outs, r_scr)

        body(*ins, *outs, *scr)

        if rider.mid is not None and total > 1:
            @pl.when(step == (5 * (total - 1)) // 8)
            def _():
                rider.mid(r_ins, r_outs, r_scr)

        @pl.when(step == total - 1)
        def _():
            if rider.mid is not None and total == 1:
                rider.mid(r_ins, r_outs, r_scr)
            rider.finish(r_ins, r_outs, r_scr)

    total = math.prod(grid)

    outs = pl.pallas_call(
        carried, name=name, grid=grid, in_specs=list(in_specs) + [HBM_SPEC] * r_in, out_specs=out_specs + [HBM_SPEC] * r_out,
        out_shape=out_shape + rider.out_shape, scratch_shapes=scratch_shapes + rider.scratch,
        compiler_params=_params(len(grid)))(*args, *rider.arrays)
    return unwrap(outs[:n_out]), rider.post(list(outs[n_out:]))


def _all_gather(name, x):
    return _ride_alone(name, _gather_rider([x]))[0]


def _sibling_rider(x):
    _, r, c_ = x.shape

    def copies(ins, outs, scratch):
        send_sems, recv_sems = scratch
        place = _place()
        return [pltpu.make_async_remote_copy(
            src_ref=ins[0].at[2 * chip + (1 - place[2])], dst_ref=outs[0].at[chip], send_sem=send_sems.at[chip],
            recv_sem=recv_sems.at[chip], device_id=_flip(place, 1), device_id_type=MESH) for chip in range(N_CHIP)]

    def start(ins, outs, scratch):
        for cp in copies(ins, outs, scratch):
            cp.start()

    def finish(ins, outs, scratch):
        for cp in copies(ins, outs, scratch):
            cp.wait()

    return _Rider([x], [jax.ShapeDtypeStruct((N_CHIP, r, c_), x.dtype)],
                  [pltpu.SemaphoreType.DMA((N_CHIP,)), pltpu.SemaphoreType.DMA((N_CHIP,))], start, finish, lambda outs: outs[0])


def _join(riders):
    def cut(seq, counts):
        out, off = [], 0
        for k in counts:
            out.append(seq[off:off + k])
            off += k
        return out

    def parts(ins, outs, scratch):
        return zip(riders, cut(ins, [len(r.arrays) for r in riders]), cut(outs, [len(r.out_shape) for r in riders]),
                   cut(scratch, [len(r.scratch) for r in riders]))

    def start(ins, outs, scratch):
        for r, i, o, s in parts(ins, outs, scratch):
            r.start(i, o, s)

    def finish(ins, outs, scratch):
        for r, i, o, s in parts(ins, outs, scratch):
            r.finish(i, o, s)

    def mid(ins, outs, scratch):
        for r, i, o, s in parts(ins, outs, scratch):
            if r.mid is not None:
                r.mid(i, o, s)

    return _Rider(
        [a for r in riders for a in r.arrays], [o for r in riders for o in r.out_shape], [s for r in riders for s in r.scratch],
        start, finish, lambda outs: [r.post(o) for r, o in zip(riders, cut(outs, [len(r.out_shape) for r in riders]))],
        mid=mid if any(r.mid is not None for r in riders) else None)


def _ride(fn, *args, riders):
    return fn(*args, rider=_join(riders))


def _pair_sum(name, x, got, core):
    _, r, c_ = x.shape
    br = _tile(r, max(256, (1 << 21) // c_), 2 * SUBLANE)

    def body(core_ref, x_ref, g_ref, o_ref):
        o_ref[...] = (x_ref[...].astype(F32) + g_ref[...].astype(F32)).astype(o_ref.dtype)

    return pl.pallas_call(
        body, name=name, out_shape=jax.ShapeDtypeStruct((N_CHIP, r, c_), x.dtype),
        grid_spec=pltpu.PrefetchScalarGridSpec(
            num_scalar_prefetch=1, grid=(N_CHIP, r // br),
            in_specs=[pl.BlockSpec((None, br, c_), lambda ch, i, core_ref: (2 * ch + core_ref[0], i, 0)),
                      pl.BlockSpec((None, br, c_), lambda ch, i, core_ref: (ch, i, 0))],
            out_specs=pl.BlockSpec((None, br, c_), lambda ch, i, core_ref: (ch, i, 0))),
        compiler_params=_params(2))(core, x, got)


def _mm(name, a, b, out_shape, out_dtype, grid, a_spec, b_spec, o_spec, dims, n_red, acc_shape, rider=None):
    red = tuple(range(len(grid) - n_red, len(grid)))
    out_type = jax.ShapeDtypeStruct(out_shape, out_dtype)
    if all(grid[ax] == 1 for ax in red):
        def single(a_ref, b_ref, o_ref):
            o_ref[...] = _dot(a_ref[...], b_ref[...], dims).astype(o_ref.dtype)

        return _call(single, name=name, out_shape=out_type, grid=grid, in_specs=[a_spec, b_spec], out_specs=o_spec,
                     args=(a, b), rider=rider)

    def body(a_ref, b_ref, o_ref, acc_ref):
        first = functools.reduce(jnp.logical_and, [pl.program_id(ax) == 0 for ax in red])
        last = functools.reduce(jnp.logical_and, [pl.program_id(ax) == grid[ax] - 1 for ax in red])

        @pl.when(first)
        def _():
            acc_ref[...] = jnp.zeros_like(acc_ref)

        acc_ref[...] += _dot(a_ref[...], b_ref[...], dims)

        @pl.when(last)
        def _():
            o_ref[...] = acc_ref[...].astype(o_ref.dtype)

    return _call(body, name=name, out_shape=out_type, grid=grid, in_specs=[a_spec, b_spec], out_specs=o_spec,
                 scratch_shapes=[pltpu.VMEM(acc_shape, F32)], args=(a, b), rider=rider)


def _mm_col(name, a, b, out_dtype=F32, rider=None):
    m, k = a.shape
    j, _, n = b.shape
    bm, bk = _tile(m, 1024), _tile(k, 2048)
    return _mm(name, a, b, (j, m, n), out_dtype, (j, m // bm, k // bk),
               pl.BlockSpec((bm, bk), lambda jj, mm, kk: (mm, kk)),
               pl.BlockSpec((None, bk, n), lambda jj, mm, kk: (jj, kk, 0)),
               pl.BlockSpec((None, bm, n), lambda jj, mm, kk: (jj, mm, 0)), NN, 1, (bm, n), rider)


def _mm_col_da(name, do, b, rider=None):
    j, m, n = do.shape
    k = b.shape[1]
    bm, bk = _tile(m, 1024), _tile(k, 1024)
    return _mm(name, do, b, (m, k), F32, (m // bm, k // bk, j),
               pl.BlockSpec((None, bm, n), lambda mm, kk, jj: (jj, mm, 0)),
               pl.BlockSpec((None, bk, n), lambda mm, kk, jj: (jj, kk, 0)),
               pl.BlockSpec((bm, bk), lambda mm, kk, jj: (mm, kk)), NT, 1, (bm, bk), rider)


def _mm_col_db(name, a, do, out_dtype, rider=None):
    m, k = a.shape
    j, _, n = do.shape
    bm, bk = _tile(m, 2048), _tile(k, 512)
    return _mm(name, a, do, (j, k, n), out_dtype, (j, k // bk, m // bm),
               pl.BlockSpec((bm, bk), lambda jj, kk, mm: (mm, kk)),
               pl.BlockSpec((None, bm, n), lambda jj, kk, mm: (jj, mm, 0)),
               pl.BlockSpec((None, bk, n), lambda jj, kk, mm: (jj, kk, 0)), TN, 1, (bk, n), rider)


def _row_bk(kq):
    return kq if (kq % LANE or kq // LANE in (11,)) else _tile(kq, 2048)


def _mm_row(name, a, b, out_dtype=F32, rider=None):
    q, m, kq = a.shape
    n = b.shape[1]
    bm, bn, bk = _tile(m, 1024), _tile(n, 1024), _row_bk(kq)
    nk = kq // bk
    return _mm(name, a, b, (m, n), out_dtype, (m // bm, n // bn, q, nk),
               pl.BlockSpec((None, bm, bk), lambda mm, nn, qq, kk: (qq, mm, kk)),
               pl.BlockSpec((bk, bn), lambda mm, nn, qq, kk: (qq * nk + kk, nn)),
               pl.BlockSpec((bm, bn), lambda mm, nn, qq, kk: (mm, nn)), NN, 2, (bm, bn), rider)


def _mm_row_da(name, do, b, q, rider=None):
    m, n = do.shape
    kq = b.shape[0] // q
    bm, bn = _tile(m, 1024), _tile(n, 2048)
    return _mm(name, do, b, (q, m, kq), F32, (q, m // bm, n // bn),
               pl.BlockSpec((bm, bn), lambda qq, mm, nn: (mm, nn)),
               pl.BlockSpec((kq, bn), lambda qq, mm, nn: (qq, nn)),
               pl.BlockSpec((None, bm, kq), lambda qq, mm, nn: (qq, mm, 0)), NT, 1, (bm, kq), rider)


def _mm_row_db(name, a, do, out_dtype):
    q, m, kq = a.shape
    n = do.shape[1]
    bm, bn = _tile(m, 2048), _tile(n, 512)
    return _mm(name, a, do, (q * kq, n), out_dtype, (q, n // bn, m // bm),
               pl.BlockSpec((None, bm, kq), lambda qq, nn, mm: (qq, mm, 0)),
               pl.BlockSpec((bm, bn), lambda qq, nn, mm: (mm, nn)),
               pl.BlockSpec((kq, bn), lambda qq, nn, mm: (qq, nn)), TN, 1, (kq, bn))


def _row_spec(bm, d):
    return pl.BlockSpec((bm, d), lambda i: (i, 0))


def _vec_spec(d):
    return pl.BlockSpec((1, d), lambda i: (0, 0))


def _norm_mod_fwd(name, x, gain, sc, sh):
    t, d = x.shape
    bm = _tile(t, 256, SUBLANE)

    def body(x_ref, g_ref, sc_ref, sh_ref, h_ref):
        xv = x_ref[...]
        rstd = lax.rsqrt(jnp.mean(xv * xv, axis=-1, keepdims=True) + EPS)
        h_ref[...] = ((xv * rstd) * g_ref[...] * (1.0 + sc_ref[...]) + sh_ref[...]).astype(h_ref.dtype)

    return pl.pallas_call(
        body, name=name, out_shape=jax.ShapeDtypeStruct((t, d), MXU_DTYPE), grid=(t // bm,),
        in_specs=[_row_spec(bm, d), _vec_spec(d), _vec_spec(d), _vec_spec(d)], out_specs=_row_spec(bm, d),
        compiler_params=_params(1))(x, gain, sc, sh)


def _norm_mod_bwd(name, x, dh, dres, gain, sc):
    t, d = x.shape
    bm = _tile(t, 256, SUBLANE)

    def body(x_ref, dh_ref, dres_ref, g_ref, sc_ref, dx_ref, dg_ref, dsc_ref, dsh_ref):
        @pl.when(pl.program_id(0) == 0)
        def _():
            dg_ref[...] = jnp.zeros_like(dg_ref)
            dsc_ref[...] = jnp.zeros_like(dsc_ref)
            dsh_ref[...] = jnp.zeros_like(dsh_ref)

        xv, dh_ = x_ref[...], dh_ref[...]
        rstd = lax.rsqrt(jnp.mean(xv * xv, axis=-1, keepdims=True) + EPS)
        nrm = xv * rstd
        gain_ = g_ref[...]
        dsh_ref[...] += jnp.sum(dh_, axis=0, keepdims=True)
        dsc_ref[...] += jnp.sum(dh_ * (nrm * gain_), axis=0, keepdims=True)
        dhn = dh_ * (1.0 + sc_ref[...])
        dg_ref[...] += jnp.sum(dhn * nrm, axis=0, keepdims=True)
        dn = dhn * gain_
        dx_ref[...] = dres_ref[...] + rstd * (dn - nrm * jnp.mean(dn * nrm, axis=-1, keepdims=True))

    vec = jax.ShapeDtypeStruct((1, d), F32)
    return pl.pallas_call(
        body, name=name, out_shape=[jax.ShapeDtypeStruct((t, d), F32), vec, vec, vec], grid=(t // bm,),
        in_specs=[_row_spec(bm, d), _row_spec(bm, d), _row_spec(bm, d), _vec_spec(d), _vec_spec(d)],
        out_specs=[_row_spec(bm, d), _vec_spec(d), _vec_spec(d), _vec_spec(d)],
        compiler_params=_params(1))(x, dh, dres, gain, sc)


def _loss_bwd(name, x, target, gain):
    t, d = x.shape
    bm = _tile(t, 256, SUBLANE)

    def body(x_ref, t_ref, g_ref, dx_ref, loss_ref, dg_ref):
        @pl.when(pl.program_id(0) == 0)
        def _():
            loss_ref[...] = jnp.zeros_like(loss_ref)
            dg_ref[...] = jnp.zeros_like(dg_ref)

        xv = x_ref[...]
        rstd = lax.rsqrt(jnp.mean(xv * xv, axis=-1, keepdims=True) + EPS)
        nrm = xv * rstd
        gain_ = g_ref[...]
        err = nrm * gain_ - t_ref[...]
        per_tok = jnp.mean(err * err, axis=-1, keepdims=True)
        loss_ref[...] += 0.5 * jnp.sum(per_tok, axis=0, keepdims=True)
        dout = err * (1.0 / d)
        dg_ref[...] += jnp.sum(dout * nrm, axis=0, keepdims=True)
        dn = dout * gain_
        dx_ref[...] = rstd * (dn - nrm * jnp.mean(dn * nrm, axis=-1, keepdims=True))

    return pl.pallas_call(
        body, name=name,
        out_shape=[jax.ShapeDtypeStruct((t, d), F32), jax.ShapeDtypeStruct((1, 1), F32),
                   jax.ShapeDtypeStruct((1, d), F32)],
        grid=(t // bm,), in_specs=[_row_spec(bm, d), _row_spec(bm, d), _vec_spec(d)],
        out_specs=[_row_spec(bm, d), pl.BlockSpec((1, 1), lambda i: (0, 0)), _vec_spec(d)],
        compiler_params=_params(1))(x, target, gain)


def _resid(name, x, y, g):
    t, d = x.shape
    bm = _tile(t, 256, SUBLANE)

    def body(x_ref, y_ref, g_ref, o_ref):
        o_ref[...] = x_ref[...] + g_ref[...] * y_ref[...]

    return pl.pallas_call(
        body, name=name, out_shape=jax.ShapeDtypeStruct((t, d), F32), grid=(t // bm,),
        in_specs=[_row_spec(bm, d), _row_spec(bm, d), _vec_spec(d)], out_specs=_row_spec(bm, d),
        compiler_params=_params(1))(x, y, g)


def _gate_bwd(name, dx, y, g):
    t, d = dx.shape
    bm = _tile(t, 256, SUBLANE)

    def body(dx_ref, y_ref, g_ref, dy_ref, dg_ref):
        @pl.when(pl.program_id(0) == 0)
        def _():
            dg_ref[...] = jnp.zeros_like(dg_ref)

        dxv = dx_ref[...]
        dy_ref[...] = (g_ref[...] * dxv).astype(dy_ref.dtype)
        dg_ref[...] += jnp.sum(dxv * y_ref[...], axis=0, keepdims=True)

    return pl.pallas_call(
        body, name=name, out_shape=[jax.ShapeDtypeStruct((t, d), MXU_DTYPE), jax.ShapeDtypeStruct((1, d), F32)],
        grid=(t // bm,), in_specs=[_row_spec(bm, d), _row_spec(bm, d), _vec_spec(d)],
        out_specs=[_row_spec(bm, d), _vec_spec(d)], compiler_params=_params(1))(dx, y, g)


def _glu_resid_fwd(name, z, x, g):
    _, t, n = z.shape
    d = x.shape[1]
    half = N_DEV // 2
    bm = _tile(t, 256, SUBLANE)

    def body(v_ref, gt_ref, x_ref, g_ref, o_ref):
        o_ref[...] = x_ref[...] + g_ref[...] * (v_ref[...] * jax.nn.sigmoid(gt_ref[...]))

    return pl.pallas_call(
        body, name=name, out_shape=jax.ShapeDtypeStruct((t, d), F32), grid=(half, t // bm),
        in_specs=[pl.BlockSpec((None, bm, n), lambda q, i: (q, i, 0)),
                  pl.BlockSpec((None, bm, n), lambda q, i: (q + half, i, 0)),
                  pl.BlockSpec((bm, n), lambda q, i: (i, q)), pl.BlockSpec((1, n), lambda q, i: (0, q))],
        out_specs=pl.BlockSpec((bm, n), lambda q, i: (i, q)), compiler_params=_params(2))(z, z, x, g)


def _glu_resid_bwd(name, z, dx, g):
    _, t, n = z.shape
    d = dx.shape[1]
    half = N_DEV // 2
    bm = _tile(t, 256, SUBLANE)

    def body(z_ref, dx_ref, g_ref, dz_ref, dg_ref):
        @pl.when(pl.program_id(1) == 0)
        def _():
            dg_ref[...] = jnp.zeros_like(dg_ref)

        v, dxv = z_ref[0], dx_ref[...]
        sig = jax.nn.sigmoid(z_ref[1])
        dout = g_ref[...] * dxv
        dg_ref[...] += jnp.sum(dxv * (v * sig), axis=0, keepdims=True)
        dz_ref[0] = (dout * sig).astype(dz_ref.dtype)
        dz_ref[1] = (dout * v * (sig * (1.0 - sig))).astype(dz_ref.dtype)

    pair = pl.BlockSpec((2, None, bm, n), lambda q, i: (0, q, i, 0))
    dz, dg = pl.pallas_call(
        body, name=name,
        out_shape=[jax.ShapeDtypeStruct((2, half, t, n), MXU_DTYPE), jax.ShapeDtypeStruct((1, d), F32)],
        grid=(half, t // bm),
        in_specs=[pair, pl.BlockSpec((bm, n), lambda q, i: (i, q)), pl.BlockSpec((1, n), lambda q, i: (0, q))],
        out_specs=[pair, pl.BlockSpec((1, n), lambda q, i: (0, q))],
        compiler_params=_params(2))(z.reshape(2, half, t, n), dx, g)
    return dz.reshape(N_DEV, t, n), dg


def _swiglu_act_fwd(name, gu):
    _, t, n = gu.shape
    half = N_DEV // 2
    bm = _tile(t, 256, SUBLANE)

    def body(g_ref, u_ref, o_ref):
        gv = g_ref[...].astype(F32)
        o_ref[...] = (gv * jax.nn.sigmoid(gv) * u_ref[...].astype(F32)).astype(o_ref.dtype)

    return pl.pallas_call(
        body, name=name, out_shape=jax.ShapeDtypeStruct((half, t, n), MXU_DTYPE), grid=(half, t // bm),
        in_specs=[pl.BlockSpec((None, bm, n), lambda q, i: (q, i, 0)),
                  pl.BlockSpec((None, bm, n), lambda q, i: (q + half, i, 0))],
        out_specs=pl.BlockSpec((None, bm, n), lambda q, i: (q, i, 0)), compiler_params=_params(2))(gu, gu)


def _swiglu_act_bwd(name, gu, dact, rider=None):
    _, t, n = gu.shape
    half = N_DEV // 2
    bm = _tile(t, 256, SUBLANE)

    def body(gu_ref, da_ref, o_ref):
        gv, da = gu_ref[0].astype(F32), da_ref[...]
        sig = jax.nn.sigmoid(gv)
        o_ref[0] = (da * gu_ref[1].astype(F32) * (sig * (1.0 + gv * (1.0 - sig)))).astype(o_ref.dtype)
        o_ref[1] = (da * (gv * sig)).astype(o_ref.dtype)

    pair = pl.BlockSpec((2, None, bm, n), lambda q, i: (0, q, i, 0))
    res = _call(
        body, name=name, out_shape=jax.ShapeDtypeStruct((2, half, t, n), MXU_DTYPE), grid=(half, t // bm),
        in_specs=[pair, pl.BlockSpec((None, bm, n), lambda q, i: (q, i, 0))], out_specs=pair,
        args=(gu.reshape(2, half, t, n), dact), rider=rider)
    if rider is None:
        return res.reshape(N_DEV, t, n)
    return res[0].reshape(N_DEV, t, n), res[1]


def _ada_fwd(name, c16, w_ada, b_loc):
    nl, d, n = w_ada.shape
    bn = _tile(n, 512)

    def body(c_ref, w_ref, b_ref, o_ref):
        cv = c_ref[...]
        o_ref[...] = _dot(cv * jax.nn.sigmoid(cv), w_ref[...], NN) + b_ref[...]

    return pl.pallas_call(
        body, name=name, out_shape=jax.ShapeDtypeStruct((nl, c16.shape[0], n), F32), grid=(nl, n // bn),
        in_specs=[pl.BlockSpec(c16.shape, lambda i, j: (0, 0)), pl.BlockSpec((None, d, bn), lambda i, j: (i, 0, j)),
                  pl.BlockSpec((None, 1, bn), lambda i, j: (i, 0, j))],
        out_specs=pl.BlockSpec((None, c16.shape[0], bn), lambda i, j: (i, 0, j)),
        compiler_params=_params(2))(c16, w_ada, b_loc)


def _adam_update(g, w, m, v):
    m = ADAM_B1 * m + (1.0 - ADAM_B1) * g
    v = ADAM_B2 * v + (1.0 - ADAM_B2) * (g * g)
    m_hat = m / (1.0 - ADAM_B1 ** ADAM_STEP)
    v_hat = v / (1.0 - ADAM_B2 ** ADAM_STEP)
    delta = -ADAM_LR * (m_hat / (jnp.sqrt(v_hat) + ADAM_EPS) + ADAM_WD * w)
    return delta, m, v


def _adamw_w_ada(name, c16, dmod16, w, m, v, rider=None):
    nl, d, n = w.shape
    br = _tile(d, 256)

    def body(c_ref, dm_ref, w_ref, m_ref, v_ref, g_ref, dl_ref, mo_ref, vo_ref):
        cv = c_ref[...]
        g = _dot(cv * jax.nn.sigmoid(cv), dm_ref[...], TN)
        g_ref[...] = g
        dl_ref[...], mo_ref[...], vo_ref[...] = _adam_update(g, w_ref[...], m_ref[...], v_ref[...])

    blk = pl.BlockSpec((None, br, n), lambda i, r: (i, r, 0))
    shp = jax.ShapeDtypeStruct(w.shape, F32)
    return _call(
        body, name=name, out_shape=[shp] * 4, grid=(nl, d // br),
        in_specs=[pl.BlockSpec((c16.shape[0], br), lambda i, r: (0, r)),
                  pl.BlockSpec((None, dmod16.shape[1], n), lambda i, r: (i, 0, 0)), blk, blk, blk],
        out_specs=[blk] * 4, args=(c16, dmod16, w, m, v), rider=rider)


def _adamw_sum(name, parts, w, m, v, rider=None):
    nl = len(parts)
    p, r, c = parts[0].shape
    br = _tile(r, max(128, (1 << 17) // max(c, LANE)), 2 * SUBLANE)
    nb = r // br

    def body(*refs):
        p_refs, (w_ref, m_ref, v_ref, g_ref, dl_ref, mo_ref, vo_ref) = refs[:nl], refs[nl:]
        layer = pl.program_id(0)
        g = None
        for l, p_ref in enumerate(p_refs):
            gl = p_ref[0].astype(F32)
            for s in range(1, p):
                gl = gl + p_ref[s].astype(F32)
            g = gl if g is None else jnp.where(layer == l, gl, g)
        g_ref[...] = g
        dl_ref[...], mo_ref[...], vo_ref[...] = _adam_update(g, w_ref[...], m_ref[...], v_ref[...])

    blk = pl.BlockSpec((br, c), lambda l, i: (l * nb + i, 0))
    shp = jax.ShapeDtypeStruct((nl * r, c), F32)
    return _call(
        body, name=name, out_shape=[shp] * 4, grid=(nl, nb),
        in_specs=[pl.BlockSpec((p, br, c), lambda l, i: (0, i, 0))] * nl + [blk, blk, blk], out_specs=[blk] * 4,
        args=(*parts, w, m, v), rider=rider)


def _sum_parts(name, parts):
    p, r, c = parts.shape

    def body(p_ref, o_ref):
        g = p_ref[0]
        for s in range(1, p):
            g = g + p_ref[s]
        o_ref[...] = g

    return pl.pallas_call(body, name=name, out_shape=jax.ShapeDtypeStruct((r, c), F32))(parts)


def _s5_disc(name, lam_re, lam_im, log_dt, b_re, b_im):
    def body(lr_ref, li_ref, ld_ref, br_ref, bi_ref, ar_ref, ai_ref, bbr_ref, bbi_ref):
        lr, li = lr_ref[...], li_ref[...]
        dt = jnp.exp(ld_ref[...])
        mag = jnp.exp(lr * dt)
        a_re, a_im = mag * jnp.cos(li * dt), mag * jnp.sin(li * dt)
        nr, ni = a_re - 1.0, a_im
        den = lr * lr + li * li
        f_re, f_im = (nr * lr + ni * li) / den, (ni * lr - nr * li) / den
        br, bi = br_ref[...], bi_ref[...]
        ar_ref[...], ai_ref[...] = a_re, a_im
        bbr_ref[...] = f_re * br - f_im * bi
        bbi_ref[...] = f_re * bi + f_im * br

    s_a, s_b = jax.ShapeDtypeStruct(lam_re.shape, F32), jax.ShapeDtypeStruct(b_re.shape, F32)
    return pl.pallas_call(body, name=name, out_shape=[s_a, s_a, s_b, s_b])(lam_re, lam_im, log_dt, b_re, b_im)


def _s5_disc_bwd(name, lam_re, lam_im, log_dt, b_re, b_im, dab_re, dab_im, dbb_re, dbb_im):
    def body(lr_ref, li_ref, ld_ref, br_ref, bi_ref, dar_ref, dai_ref, dbbr_ref, dbbi_ref,
             dlr_ref, dli_ref, dld_ref, dbr_ref, dbi_ref):
        lr, li = lr_ref[...], li_ref[...]
        dt = jnp.exp(ld_ref[...])
        mag = jnp.exp(lr * dt)
        a_re, a_im = mag * jnp.cos(li * dt), mag * jnp.sin(li * dt)
        nr, ni = a_re - 1.0, a_im
        den = lr * lr + li * li
        f_re, f_im = (nr * lr + ni * li) / den, (ni * lr - nr * li) / den
        br, bi = br_ref[...], bi_ref[...]
        dbbr, dbbi = dbbr_ref[...], dbbi_ref[...]
        dbr_ref[...] = f_re * dbbr + f_im * dbbi
        dbi_ref[...] = f_re * dbbi - f_im * dbbr
        df_re = jnp.sum(dbbr * br + dbbi * bi, axis=1, keepdims=True)
        df_im = jnp.sum(dbbi * br - dbbr * bi, axis=1, keepdims=True)
        dnr = (df_re * lr - df_im * li) / den
        dni = (df_re * li + df_im * lr) / den
        dden = -(df_re * f_re + df_im * f_im) / den
        dlr = (df_re * nr + df_im * ni) / den + 2.0 * lr * dden
        dli = (df_re * ni - df_im * nr) / den + 2.0 * li * dden
        da_re, da_im = dar_ref[...] + dnr, dai_ref[...] + dni
        dmag_mag = da_re * a_re + da_im * a_im
        dth = da_im * a_re - da_re * a_im
        dlr_ref[...] = dlr + dmag_mag * dt
        dli_ref[...] = dli + dth * dt
        ddt = jnp.sum(dmag_mag * lr + dth * li, axis=2, keepdims=True)
        dld_ref[...] = ddt * dt

    s_a, s_b = jax.ShapeDtypeStruct(lam_re.shape, F32), jax.ShapeDtypeStruct(b_re.shape, F32)
    return pl.pallas_call(
        body, name=name, out_shape=[s_a, s_a, jax.ShapeDtypeStruct(log_dt.shape, F32), s_b, s_b],
    )(lam_re, lam_im, log_dt, b_re, b_im, dab_re, dab_im, dbb_re, dbb_im)


def _s5_time_block(t):
    return _tile(t, 128, SUBLANE)


def _s5_scan_fwd(name, u, bb_re, bb_im, ab_re, ab_im, cc_re, cc_im, dskip, rider=None):
    t, d = u.shape
    nsg, cs, ns = bb_re.shape
    tb = _s5_time_block(t)

    def body(u_ref, bbr_hbm, bbi_hbm, ar_ref, ai_ref, ccr_hbm, cci_hbm, d_ref, sr_ref, si_ref, srm_ref, sim_ref, yp_ref,
             ya_ref, bbr, bbi, ccr, cci, cr_ref, ci_ref):
        @pl.when(pl.program_id(0) == 0)
        def _():
            pltpu.sync_copy(bbr_hbm, bbr)
            pltpu.sync_copy(bbi_hbm, bbi)
            pltpu.sync_copy(ccr_hbm, ccr)
            pltpu.sync_copy(cci_hbm, cci)
            cr_ref[...] = jnp.zeros_like(cr_ref)
            ci_ref[...] = jnp.zeros_like(ci_ref)

        for sg in range(nsg):
            us = u_ref[:, sg * cs:(sg + 1) * cs]
            sr_ref[:, sg, :] = _dot(us, bbr[sg], NN)
            si_ref[:, sg, :] = _dot(us, bbi[sg], NN)
        ar, ai = ar_ref[...], ai_ref[...]

        def step(i, carry):
            cr, ci = carry
            nr = ar * cr - ai * ci + sr_ref[i]
            ni = ar * ci + ai * cr + si_ref[i]
            sr_ref[i] = nr
            si_ref[i] = ni
            return nr, ni

        cr, ci = lax.fori_loop(0, tb, step, (cr_ref[...], ci_ref[...]), unroll=2)
        cr_ref[...], ci_ref[...] = cr, ci
        srm_ref[...] = jnp.swapaxes(sr_ref[...], 0, 1).astype(MXU_DTYPE)
        sim_ref[...] = jnp.swapaxes(si_ref[...], 0, 1).astype(MXU_DTYPE)
        for sg in range(nsg):
            cols = slice(sg * cs, (sg + 1) * cs)
            y = _dot(srm_ref[sg], ccr[sg], NN) - _dot(sim_ref[sg], cci[sg], NN) + d_ref[:, cols] * u_ref[:, cols]
            yp_ref[:, cols] = y
            ya_ref[:, cols] = _gelu(y).astype(ya_ref.dtype)

    scan = jax.ShapeDtypeStruct((t, nsg, ns), F32)
    mxu = jax.ShapeDtypeStruct((nsg, t, ns), MXU_DTYPE)
    hbm = pl.BlockSpec(memory_space=pltpu.HBM)
    full = pl.BlockSpec((nsg, ns), lambda i: (0, 0))
    return _call(
        body, name=name,
        out_shape=[scan, scan, mxu, mxu, jax.ShapeDtypeStruct((t, d), F32), jax.ShapeDtypeStruct((t, d), MXU_DTYPE)],
        grid=(t // tb,), in_specs=[_row_spec(tb, d), hbm, hbm, full, full, hbm, hbm, _vec_spec(d)],
        out_specs=[pl.BlockSpec((tb, nsg, ns), lambda i: (i, 0, 0))] * 2 + [pl.BlockSpec((nsg, tb, ns), lambda i: (0, i, 0))] * 2
        + [_row_spec(tb, d)] * 2,
        scratch_shapes=[pltpu.VMEM(bb_re.shape, bb_re.dtype), pltpu.VMEM(bb_im.shape, bb_im.dtype),
                        pltpu.VMEM(cc_re.shape, cc_re.dtype), pltpu.VMEM(cc_im.shape, cc_im.dtype),
                        pltpu.VMEM((nsg, ns), F32), pltpu.VMEM((nsg, ns), F32)],
        args=(u, bb_re, bb_im, ab_re, ab_im, cc_re, cc_im, dskip), rider=rider)


def _s5_scan_bwd(name, dyact, ypre, cc_re, cc_im, ab_re, ab_im, s_re, s_im, rider=None):
    t, d = dyact.shape
    nsg, ns, cs = cc_re.shape
    tb = _s5_time_block(t)
    nb = t // tb

    def body(dya_ref, yp_ref, ccr_hbm, cci_hbm, ar_ref, ai_ref, sr_ref, si_ref, dy_ref, lrm_ref, lim_ref, dar_ref, dai_ref,
             ccr, cci, lr_ref, li_ref, cr_ref, ci_ref):
        dy_ref[...] = (dya_ref[...] * _gelu_and_grad(yp_ref[...])[1]).astype(dy_ref.dtype)

        @pl.when(pl.program_id(0) == 0)
        def _():
            pltpu.sync_copy(ccr_hbm, ccr)
            pltpu.sync_copy(cci_hbm, cci)
            cr_ref[...] = jnp.zeros_like(cr_ref)
            ci_ref[...] = jnp.zeros_like(ci_ref)
            dar_ref[...] = jnp.zeros_like(dar_ref)
            dai_ref[...] = jnp.zeros_like(dai_ref)

        for sg in range(nsg):
            dys = dy_ref[:, sg * cs:(sg + 1) * cs]
            lr_ref[:, sg, :] = _dot(dys, ccr[sg], NT)
            li_ref[:, sg, :] = -_dot(dys, cci[sg], NT)
        ar, ai = ar_ref[...], ai_ref[...]

        def step(i, carry):
            cr, ci, dar, dai = carry
            j = tb - 1 - i
            sr, si = sr_ref[j], si_ref[j]
            dar = dar + (cr * sr + ci * si)
            dai = dai + (ci * sr - cr * si)
            nr = lr_ref[j] + (ar * cr + ai * ci)
            ni = li_ref[j] + (ar * ci - ai * cr)
            lr_ref[j] = nr
            li_ref[j] = ni
            return nr, ni, dar, dai

        cr, ci, dar, dai = lax.fori_loop(0, tb, step, (cr_ref[...], ci_ref[...], dar_ref[...], dai_ref[...]))
        cr_ref[...], ci_ref[...] = cr, ci
        dar_ref[...], dai_ref[...] = dar, dai
        lrm_ref[...] = jnp.swapaxes(lr_ref[...], 0, 1).astype(MXU_DTYPE)
        lim_ref[...] = jnp.swapaxes(li_ref[...], 0, 1).astype(MXU_DTYPE)

    hbm = pl.BlockSpec(memory_space=pltpu.HBM)
    full = pl.BlockSpec((nsg, ns), lambda i: (0, 0))
    mxu = jax.ShapeDtypeStruct((nsg, t, ns), MXU_DTYPE)
    acc = jax.ShapeDtypeStruct((nsg, ns), F32)
    scan_spec = pl.BlockSpec((tb, nsg, ns), lambda i: (nb - 1 - i, 0, 0))
    rows = pl.BlockSpec((tb, d), lambda i: (nb - 1 - i, 0))
    return _call(
        body, name=name, out_shape=[jax.ShapeDtypeStruct((t, d), MXU_DTYPE), mxu, mxu, acc, acc], grid=(nb,),
        in_specs=[rows, rows, hbm, hbm, full, full, scan_spec, scan_spec],
        out_specs=[rows] + [pl.BlockSpec((nsg, tb, ns), lambda i: (0, nb - 1 - i, 0))] * 2 + [full, full],
        scratch_shapes=[pltpu.VMEM(cc_re.shape, cc_re.dtype), pltpu.VMEM(cc_im.shape, cc_im.dtype),
                        pltpu.VMEM((tb, nsg, ns), F32), pltpu.VMEM((tb, nsg, ns), F32),
                        pltpu.VMEM((nsg, ns), F32), pltpu.VMEM((nsg, ns), F32)],
        args=(dyact, ypre, cc_re, cc_im, ab_re, ab_im, s_re, s_im), rider=rider)


def _s5_grads(name, lam_re, lam_im, s_re, s_im, u, dyp, bb_re, bb_im, dskip, rider=None):
    nsg, t, ns = lam_re.shape
    d = u.shape[1]
    cs = bb_re.shape[1]
    tb = _tile(t, 512, SUBLANE)

    def body(lr_ref, li_ref, sr_ref, si_ref, u_ref, dy_ref, bbr_ref, bbi_ref, d_ref,
             du_ref, dbbr_ref, dbbi_ref, dccr_ref, dcci_ref, dd_ref):
        @pl.when(pl.program_id(1) == 0)
        def _():
            for r in (dbbr_ref, dbbi_ref, dccr_ref, dcci_ref, dd_ref):
                r[...] = jnp.zeros_like(r)

        lr, li, uv, dy = lr_ref[...], li_ref[...], u_ref[...], dy_ref[...]
        dyf = dy.astype(F32)
        du_ref[...] = _dot(lr, bbr_ref[...], NT) + _dot(li, bbi_ref[...], NT) + d_ref[...] * dyf
        dbbr_ref[...] += _dot(uv, lr, TN)
        dbbi_ref[...] += _dot(uv, li, TN)
        dccr_ref[...] += _dot(sr_ref[...], dy, TN)
        dcci_ref[...] -= _dot(si_ref[...], dy, TN)
        dd_ref[...] += jnp.sum(dyf * uv, axis=0, keepdims=True)

    s_spec = pl.BlockSpec((None, tb, ns), lambda sg, i: (sg, i, 0))
    col = pl.BlockSpec((tb, cs), lambda sg, i: (i, sg))
    b_spec = pl.BlockSpec((None, cs, ns), lambda sg, i: (sg, 0, 0))
    c_spec = pl.BlockSpec((None, ns, cs), lambda sg, i: (sg, 0, 0))
    vec = pl.BlockSpec((1, cs), lambda sg, i: (0, sg))
    return _call(
        body, name=name,
        out_shape=[jax.ShapeDtypeStruct((t, d), F32), jax.ShapeDtypeStruct(bb_re.shape, F32),
                   jax.ShapeDtypeStruct(bb_re.shape, F32), jax.ShapeDtypeStruct((nsg, ns, cs), F32),
                   jax.ShapeDtypeStruct((nsg, ns, cs), F32), jax.ShapeDtypeStruct((1, d), F32)],
        grid=(nsg, t // tb), in_specs=[s_spec, s_spec, s_spec, s_spec, col, col, b_spec, b_spec, vec],
        out_specs=[col, b_spec, b_spec, c_spec, c_spec, vec],
        args=(lam_re, lam_im, s_re, s_im, u, dyp, bb_re, bb_im, dskip), rider=rider)


def _shift_down(x, k, prev8):
    if k == 0:
        return x
    ext = jnp.concatenate([prev8, x], axis=0)
    return ext[SUBLANE - k:SUBLANE - k + x.shape[0]]


def _shift_up(x, k, next8):
    if k == 0:
        return x
    ext = jnp.concatenate([x, next8], axis=0)
    return ext[k:k + x.shape[0]]


def _lru_time_block(t):
    return _tile(t, 256, SUBLANE)


def _lru_gates(xp, prev8, cv_ref, wrg, wig):
    taps = cv_ref.shape[0] - 4
    row = lambda k: cv_ref[k:k + 1, :]
    xs = [_shift_down(xp, taps - 1 - k, prev8) for k in range(taps)]
    xb = row(taps)
    for k in range(taps):
        xb = xb + row(k) * xs[k]
    r = jax.nn.sigmoid(_dot(xb, wrg, NN) + row(taps + 1))
    ig = jax.nn.sigmoid(_dot(xb, wig, NN) + row(taps + 2))
    sp = jax.nn.softplus(-row(taps + 3))
    log_a = -LRU_C * r * sp
    a = jnp.exp(log_a)
    mult = jnp.sqrt(_neg_expm1(2.0 * log_a))
    return xs, xb, r, ig, sp, a, mult


def _lru_fwd(name, zz, cvec, wrg, wig, rider=None):
    _, t, w = zz.shape
    half = N_DEV // 2
    tb = _lru_time_block(t)

    def body(gb_ref, xp_ref, xprev_ref, cv_ref, wrg_ref, wig_ref, hs_ref, y_ref, a_scr, b_scr, carry):
        i = pl.program_id(1)

        @pl.when(i == 0)
        def _():
            carry[...] = jnp.zeros_like(carry)

        prev8 = jnp.where(i > 0, xprev_ref[...], 0.0)
        _, xb, _, ig, _, a, mult = _lru_gates(xp_ref[...], prev8, cv_ref, wrg_ref[...], wig_ref[...])
        a_scr[...] = a
        b_scr[...] = mult * (ig * xb)

        def step(j, h):
            h = a_scr[pl.ds(j, 1), :] * h + b_scr[pl.ds(j, 1), :]
            hs_ref[pl.ds(j, 1), :] = h
            return h

        carry[0:1, :] = lax.fori_loop(0, tb, step, carry[0:1, :], unroll=8)
        y_ref[...] = (hs_ref[...] * _gelu(gb_ref[...])).astype(y_ref.dtype)

    nrow = tb // SUBLANE
    blk = lambda off: pl.BlockSpec((None, tb, w), lambda q, i: (q + off, i, 0))
    return _call(
        body, name=name,
        out_shape=[jax.ShapeDtypeStruct((half, t, w), F32), jax.ShapeDtypeStruct((half, t, w), MXU_DTYPE)],
        grid=(half, t // tb),
        in_specs=[blk(0), blk(half),
                  pl.BlockSpec((None, SUBLANE, w), lambda q, i: (q + half, jnp.maximum(i * nrow - 1, 0), 0)),
                  pl.BlockSpec((None,) + cvec.shape[1:], lambda q, i: (q, 0, 0)),
                  pl.BlockSpec((None, w, w), lambda q, i: (q, 0, 0)), pl.BlockSpec((None, w, w), lambda q, i: (q, 0, 0))],
        out_specs=[blk(0), blk(0)],
        scratch_shapes=[pltpu.VMEM((tb, w), F32), pltpu.VMEM((tb, w), F32), pltpu.VMEM((SUBLANE, w), F32)],
        args=(zz, zz, zz, cvec, wrg, wig), rider=rider)


def _lru_bwd(name, zz, hs, dy, cvec, wrg, wig, rider=None):
    _, t, w = zz.shape
    half = N_DEV // 2
    tb = _lru_time_block(t)
    nb = t // tb
    taps = cvec.shape[1] - 4

    def body(gb_ref, xp_ref, xprev_ref, hs_ref, hprev_ref, dy_ref, cv_ref, wrg_ref, wig_ref,
             dgb_ref, dxp_ref, dcv_ref, dwrg_ref, dwig_ref, a_scr, l_scr, carry, dxb_next):
        i = pl.program_id(1)

        @pl.when(i == 0)
        def _():
            for r_ in (carry, dxb_next, dcv_ref, dwrg_ref, dwig_ref):
                r_[...] = jnp.zeros_like(r_)

        has_prev = i < nb - 1
        row = lambda k: cv_ref[k:k + 1, :]
        prev8 = jnp.where(has_prev, xprev_ref[...], 0.0)
        xs, xb, r, ig, sp, a, mult = _lru_gates(xp_ref[...], prev8, cv_ref, wrg_ref[...], wig_ref[...])
        hs_ = hs_ref[...]
        hs_m1 = _shift_down(hs_, 1, jnp.where(has_prev, hprev_ref[...], 0.0))
        gel, dgel = _gelu_and_grad(gb_ref[...])
        dy_ = dy_ref[...]
        dgb_ref[...] = (dy_ * hs_ * dgel).astype(dgb_ref.dtype)
        a_scr[...] = a
        l_scr[...] = dy_ * gel

        def step(k, c):
            j = tb - 1 - k
            lam = l_scr[pl.ds(j, 1), :] + c
            l_scr[pl.ds(j, 1), :] = lam
            return a_scr[pl.ds(j, 1), :] * lam

        carry[0:1, :] = lax.fori_loop(0, tb, step, carry[0:1, :], unroll=8)
        lam = l_scr[...]
        dmult = lam * (ig * xb)
        dig = lam * (mult * xb)
        dxb = lam * (mult * ig)
        dlog_a = (lam * hs_m1) * a - dmult * (a * a) / mult
        dr = dlog_a * (-LRU_C * sp)
        dsp = jnp.sum(dlog_a * (-LRU_C * r), axis=0, keepdims=True)
        dpr = dr * (r * (1.0 - r))
        dpi = dig * (ig * (1.0 - ig))
        dwrg_ref[...] += _dot(xb, dpr, TN)
        dwig_ref[...] += _dot(xb, dpi, TN)
        dxb = dxb + _dot(dpr, wrg_ref[...], NT) + _dot(dpi, wig_ref[...], NT)
        for k in range(taps):
            dcv_ref[k:k + 1, :] += jnp.sum(dxb * xs[k], axis=0, keepdims=True)
        dcv_ref[taps:taps + 1, :] += jnp.sum(dxb, axis=0, keepdims=True)
        dcv_ref[taps + 1:taps + 2, :] += jnp.sum(dpr, axis=0, keepdims=True)
        dcv_ref[taps + 2:taps + 3, :] += jnp.sum(dpi, axis=0, keepdims=True)
        dcv_ref[taps + 3:taps + 4, :] += dsp * (-jax.nn.sigmoid(-row(taps + 3)))
        nxt8 = dxb_next[...]
        dxp = row(taps - 1) * dxb
        for k in range(taps - 1):
            dxp = dxp + row(k) * _shift_up(dxb, taps - 1 - k, nxt8)
        dxp_ref[...] = dxp.astype(dxp_ref.dtype)
        dxb_next[...] = dxb[0:SUBLANE]

    nrow = tb // SUBLANE
    blk = lambda off: pl.BlockSpec((None, tb, w), lambda q, i: (q + off, nb - 1 - i, 0))
    halo = lambda off: pl.BlockSpec((None, SUBLANE, w), lambda q, i: (q + off, jnp.maximum((nb - 1 - i) * nrow - 1, 0), 0))
    wspec = pl.BlockSpec((None, w, w), lambda q, i: (q, 0, 0))
    cspec = pl.BlockSpec((None,) + cvec.shape[1:], lambda q, i: (q, 0, 0))
    act = jax.ShapeDtypeStruct((half, t, w), MXU_DTYPE)
    return _call(
        body, name=name,
        out_shape=[act, act, jax.ShapeDtypeStruct(cvec.shape, F32), jax.ShapeDtypeStruct(wrg.shape, F32),
                   jax.ShapeDtypeStruct(wig.shape, F32)],
        grid=(half, nb),
        in_specs=[blk(0), blk(half), halo(half), blk(0), halo(0), blk(0), cspec, wspec, wspec],
        out_specs=[blk(0), blk(0), cspec, wspec, wspec],
        scratch_shapes=[pltpu.VMEM((tb, w), F32), pltpu.VMEM((tb, w), F32), pltpu.VMEM((SUBLANE, w), F32),
                        pltpu.VMEM((SUBLANE, w), F32)],
        args=(zz, zz, zz, hs, hs, dy, cvec, wrg, wig), rider=rider)


def _band(name, blocks, per, dtype):
    n, a, b = blocks.shape

    def body(x_ref, o_ref):
        o_ref[...] = jnp.zeros_like(o_ref)
        for g in range(per):
            o_ref[g * a:(g + 1) * a, g * b:(g + 1) * b] = x_ref[g].astype(o_ref.dtype)

    return pl.pallas_call(
        body, name=name, out_shape=jax.ShapeDtypeStruct((n // per, per * a, per * b), dtype), grid=(n // per,),
        in_specs=[pl.BlockSpec((per, a, b), lambda s: (s, 0, 0))],
        out_specs=pl.BlockSpec((None, per * a, per * b), lambda s: (s, 0, 0)), compiler_params=_params(1))(blocks)


def _unband(name, bands, per):
    s, pa, pb = bands.shape
    a, b = pa // per, pb // per

    def body(x_ref, o_ref):
        for g in range(per):
            o_ref[g] = x_ref[g * a:(g + 1) * a, g * b:(g + 1) * b]

    return pl.pallas_call(
        body, name=name, out_shape=jax.ShapeDtypeStruct((s * per, a, b), bands.dtype), grid=(s,),
        in_specs=[pl.BlockSpec((None, pa, pb), lambda i: (i, 0, 0))],
        out_specs=pl.BlockSpec((per, a, b), lambda i: (i, 0, 0)), compiler_params=_params(1))(bands)


def _pack(arrays, rows_multiple, lanes=LANE):
    flat = [a.reshape(-1).astype(F32) for a in arrays]
    size = sum(a.shape[0] for a in flat)
    rows = -(-size // (lanes * rows_multiple)) * rows_multiple
    if rows * lanes > size:
        flat.append(jnp.zeros((rows * lanes - size,), F32))
    return jnp.concatenate(flat).reshape(rows, lanes)


def _unpack(packed, shapes):
    flat = packed.reshape(-1)
    out, off = [], 0
    for s in shapes:
        n = math.prod(s)
        out.append(flat[off:off + n].reshape(s))
        off += n
    return out


def kernel(x, c, norm_g, w_ada, b_ada, s5_w_in, s5_lam_re, s5_lam_im, s5_log_dt, s5_b_re, s5_b_im, s5_c_re, s5_c_im, s5_d, s5_w_glu, lru_w_in, lru_conv_w, lru_conv_b, lru_w_rg, lru_b_rg, lru_w_ig, lru_b_ig, lru_lam, lru_w_out, ffn_w_gu, ffn_w_down, final_g, loss_target, m_norm_g, m_w_ada, m_b_ada, m_s5_w_in, m_s5_lam_re, m_s5_lam_im, m_s5_log_dt, m_s5_b_re, m_s5_b_im, m_s5_c_re, m_s5_c_im, m_s5_d, m_s5_w_glu, m_lru_w_in, m_lru_conv_w, m_lru_conv_b, m_lru_w_rg, m_lru_b_rg, m_lru_w_ig, m_lru_b_ig, m_lru_lam, m_lru_w_out, m_ffn_w_gu, m_ffn_w_down, m_final_g, v_norm_g, v_w_ada, v_b_ada, v_s5_w_in, v_s5_lam_re, v_s5_lam_im, v_s5_log_dt, v_s5_b_re, v_s5_b_im, v_s5_c_re, v_s5_c_im, v_s5_d, v_s5_w_glu, v_lru_w_in, v_lru_conv_w, v_lru_conv_b, v_lru_w_rg, v_lru_b_rg, v_lru_w_ig, v_lru_b_ig, v_lru_lam, v_lru_w_out, v_ffn_w_gu, v_ffn_w_down, v_final_g):
    wv = dict(zip(WEIGHTS, (norm_g, w_ada, b_ada, s5_w_in, s5_lam_re, s5_lam_im, s5_log_dt, s5_b_re, s5_b_im, s5_c_re, s5_c_im, s5_d, s5_w_glu, lru_w_in, lru_conv_w, lru_conv_b, lru_w_rg, lru_b_rg, lru_w_ig, lru_b_ig, lru_lam, lru_w_out, ffn_w_gu, ffn_w_down, final_g)))
    mv = dict(zip(WEIGHTS, (m_norm_g, m_w_ada, m_b_ada, m_s5_w_in, m_s5_lam_re, m_s5_lam_im, m_s5_log_dt, m_s5_b_re, m_s5_b_im, m_s5_c_re, m_s5_c_im, m_s5_d, m_s5_w_glu, m_lru_w_in, m_lru_conv_w, m_lru_conv_b, m_lru_w_rg, m_lru_b_rg, m_lru_w_ig, m_lru_b_ig, m_lru_lam, m_lru_w_out, m_ffn_w_gu, m_ffn_w_down, m_final_g)))
    vv = dict(zip(WEIGHTS, (v_norm_g, v_w_ada, v_b_ada, v_s5_w_in, v_s5_lam_re, v_s5_lam_im, v_s5_log_dt, v_s5_b_re, v_s5_b_im, v_s5_c_re, v_s5_c_im, v_s5_d, v_s5_w_glu, v_lru_w_in, v_lru_conv_w, v_lru_conv_b, v_lru_w_rg, v_lru_b_rg, v_lru_w_ig, v_lru_b_ig, v_lru_lam, v_lru_w_out, v_ffn_w_gu, v_ffn_w_down, v_final_g)))

    me = 4 * lax.axis_index("x") + 2 * lax.axis_index("y") + lax.axis_index("c")
    x0 = x[0]
    tgt = loss_target[0]
    t, d = x0.shape
    depth = norm_g.shape[0]
    n_mod = w_ada.shape[2] * N_DEV // d
    groups, states = s5_lam_re.shape[1], s5_lam_re.shape[2]
    per_sg = S5_SUPER // S5_GROUP
    nsg = groups // per_sg
    lw = lru_lam.shape[1] * N_DEV
    lwc = lw // (N_DEV // 2)
    half = N_DEV // 2

    assert depth == 2, "the ride schedule below is written for one S5 layer followed by one RG-LRU layer"
    wire = lambda a: a.astype(WIRE_DTYPE)
    gw = {'s5_in': _all_gather("ag_s5_w_in", wire(s5_w_in[0]))}

    def riding(job, fn, *args):
        res, (got,) = fn(*args, rider=_gather_rider([wire(job[1])]))
        gw[job[0]] = got
        return res

    sh_shapes = [wv[n].shape for n in SMALL_SHARDED] + [c.shape]
    sh_all = _all_gather("ag_small", _pack([wv[n] for n in SMALL_SHARDED] + [c], SUBLANE))
    sh_parts = [jnp.stack(p) for p in zip(*[_unpack(sh_all[s], sh_shapes) for s in range(N_DEV)])]
    full = {}
    for n, p in zip(SMALL_SHARDED, sh_parts[:-1]):
        full[n] = jnp.moveaxis(p, 0, -2).reshape(p.shape[1:-1] + (-1,))
    c_all = sh_parts[-1].reshape(N_DEV, d)
    c16 = jnp.pad(c_all, ((0, 2 * SUBLANE - N_DEV), (0, 0)))

    n_loc = w_ada.shape[2]
    b_loc = lax.dynamic_slice_in_dim(b_ada, me * n_loc, n_loc, axis=1)[:, None, :]
    mod_part = _ada_fwd("ada_fwd", c16, w_ada, b_loc)[:, :N_DEV]
    mod_mine = _chunk_exchange("x_mod", [mod_part.transpose(1, 0, 2)], ALL)
    mod = mod_mine.transpose(1, 0, 2).reshape(depth, n_mod, 1, d)

    lam3 = lambda a: a[0][:, None, :]
    p_lr, p_li, p_ld = lam3(s5_lam_re), lam3(s5_lam_im), s5_log_dt[0][:, None, None]
    p_br, p_bi = s5_b_re[0].transpose(0, 2, 1), s5_b_im[0].transpose(0, 2, 1)
    ab_re3, ab_im3, bb_re3, bb_im3 = _s5_disc("s5_disc", p_lr, p_li, p_ld, p_br, p_bi)
    ab_re, ab_im = ab_re3.reshape(nsg, per_sg * states), ab_im3.reshape(nsg, per_sg * states)
    bb_re = _band("band_bb_re", bb_re3, per_sg, MXU_DTYPE)
    bb_im = _band("band_bb_im", bb_im3, per_sg, MXU_DTYPE)
    cc_re = _band("band_cc_re", s5_c_re[0].transpose(0, 2, 1), per_sg, MXU_DTYPE)
    cc_im = _band("band_cc_im", s5_c_im[0].transpose(0, 2, 1), per_sg, MXU_DTYPE)

    taps = lru_conv_w.shape[1]
    cvec = jnp.concatenate([full['lru_conv_w'].reshape(taps, lw), full['lru_conv_b'], full['lru_b_rg'],
                            full['lru_b_ig'], full['lru_lam']], axis=0)
    cvec = cvec.reshape(taps + 4, half, lwc).transpose(1, 0, 2)
    wrg = _band("band_w_rg", lru_w_rg[0], LRU_BLOCKS_PER_CHUNK, MXU_DTYPE)
    wig = _band("band_w_ig", lru_w_ig[0], LRU_BLOCKS_PER_CHUNK, MXU_DTYPE)

    saved = []
    xc = x0
    for i in range(depth):
        sh1, sc1, g1, sh2, sc2, g2 = [mod[i, k] for k in range(n_mod)]
        gn = full['norm_g'][i]
        h1 = _norm_mod_fwd(f"norm1_fwd{i}", xc, gn[0:1], sc1, sh1)
        if i % 2 == 0:
            u = riding(('s5_glu', s5_w_glu[0]), _mm_row, f"s5_in{i}", h1[None], gw['s5_in'].reshape(d, d))
            s_re, s_im, s_rem, s_imm, ypre, yact = riding((('gu', i), ffn_w_gu[i]), _s5_scan_fwd, f"s5_scan{i}", u, bb_re,
                                                          bb_im, ab_re, ab_im, cc_re, cc_im, s5_d)
            z = riding((('down', i), ffn_w_down[i]), _mm_col, f"s5_glu{i}", yact, gw['s5_glu'])
            x1 = _glu_resid_fwd(f"s5_resid{i}", z, xc, g1)
            mix = (u, s_re, s_im, s_rem, s_imm, ypre, yact, z)
        else:
            zz = _mm_col(f"lru_in{i}", h1, gw['lru_in'])
            hs, ylru = riding((('gu', i), ffn_w_gu[i]), _lru_fwd, f"lru_core{i}", zz, cvec, wrg, wig)
            o = _mm_row(f"lru_out{i}", ylru, gw['lru_out'].reshape(lw, d))
            x1 = _resid(f"lru_resid{i}", xc, o, g1)
            mix = (zz, hs, ylru, o)
        h2 = _norm_mod_fwd(f"norm2_fwd{i}", x1, gn[1:2], sc2, sh2)
        if i % 2 == 0:
            gu = riding(('lru_in', lru_w_in[0]), _mm_col, f"ffn_gu{i}", h2, gw['gu', i], MXU_DTYPE)
            act = _swiglu_act_fwd(f"ffn_act{i}", gu)
            f = riding(('lru_out', lru_w_out[0]), _mm_row, f"ffn_down{i}", act, gw['down', i].reshape(-1, d))
        else:
            gu = riding((('down', i), ffn_w_down[i]), _mm_col, f"ffn_gu{i}", h2, gw['gu', i], MXU_DTYPE)
            act = _swiglu_act_fwd(f"ffn_act{i}", gu)
            f = _mm_row(f"ffn_down{i}", act, gw['down', i].reshape(-1, d))
        x2 = _resid(f"ffn_resid{i}", x1, f, g2)
        saved.append((xc, h1, mix, x1, h2, gu, act, f))
        xc = x2

    dx, loss_part, d_final_g = _loss_bwd("loss", xc, tgt, final_g[None])
    loss = lax.psum(loss_part[0, 0], ("x", "y", "c"))

    grads = {}
    parts = {}
    dmod = [None] * depth
    d_norm_g = [None] * depth
    core = lax.axis_index("c").astype(jnp.int32).reshape(1)
    chunked = lambda p: p.reshape(N_DEV, -1, p.shape[-1])
    to_sibling = lambda p: _sibling_rider(chunked(p))
    pair = lambda name, p, got: _pair_sum(name, chunked(p), got, core)
    over_ici = lambda sums: _chunk_rider([sums], SAME_CORE)

    above = None
    for i in reversed(range(depth)):
        xin, h1, mix, x1, h2, gu, act, f = saved[i]
        sh1, sc1, g1, sh2, sc2, g2 = [mod[i, k] for k in range(n_mod)]
        gn = full['norm_g'][i]
        g_down = gw['down', i].reshape(-1, d)
        df, dg2 = _gate_bwd(f"ffn_gate_bwd{i}", dx, f, g2)
        if above is None:
            dact = _mm_row_da(f"ffn_down_da{i}", df, g_down, half)
        else:
            dact, (got,) = _ride(_mm_row_da, f"ffn_down_da{i}", df, g_down, half, riders=[to_sibling(above[1])])
            s_above = pair(f"x_{above[0][0]}_pair", above[1], got)
        p_down = _mm_row_db(f"ffn_down_db{i}", act, df, WIRE_DTYPE)
        dgu, (got,) = _ride(_swiglu_act_bwd, f"ffn_act_bwd{i}", gu, dact, riders=[to_sibling(p_down)])
        s_down = pair(f"x_ffn_w_down{i}_pair", p_down, got)
        if above is None:
            dh2, (parts['ffn_w_down', i],) = _ride(_mm_col_da, f"ffn_gu_da{i}", dgu, gw['gu', i], riders=[over_ici(s_down)])
            p_gu = _mm_col_db(f"ffn_gu_db{i}", h2, dgu, WIRE_DTYPE)
        else:
            dh2, (parts[above[0]],) = _ride(_mm_col_da, f"ffn_gu_da{i}", dgu, gw['gu', i], riders=[over_ici(s_above)])
            p_gu, (parts['ffn_w_down', i],) = _ride(_mm_col_db, f"ffn_gu_db{i}", h2, dgu, WIRE_DTYPE,
                                                    riders=[over_ici(s_down)])
        dx, dgn2, dsc2, dsh2 = _norm_mod_bwd(f"norm2_bwd{i}", x1, dh2, dx, gn[1:2], sc2)
        if i % 2 == 0:
            u, s_re, s_im, s_rem, s_imm, ypre, yact, z = mix
            dz, dg1 = _glu_resid_bwd(f"s5_resid_bwd{i}", z, dx, g1)
            dyact, (got,) = _ride(_mm_col_da, f"s5_glu_da{i}", dz, gw['s5_glu'], riders=[to_sibling(p_gu)])
            s_gu = pair(f"x_ffn_w_gu{i}_pair", p_gu, got)
            p_glu = _mm_col_db(f"s5_glu_db{i}", yact, dz, WIRE_DTYPE)
            (dyp, l_rem, l_imm, dab_re, dab_im), (parts['ffn_w_gu', i], got) = _ride(
                _s5_scan_bwd, f"s5_scan_bwd{i}", dyact, ypre, cc_re, cc_im, ab_re, ab_im, s_re, s_im,
                riders=[over_ici(s_gu), to_sibling(p_glu)])
            s_glu = pair("x_s5_w_glu_pair", p_glu, got)
            (du, dbb_re, dbb_im, dcc_re, dcc_im, dd), (parts['s5_w_glu', 0],) = _ride(
                _s5_grads, f"s5_grads{i}", l_rem, l_imm, s_rem, s_imm, u, dyp, bb_re, bb_im, s5_d, riders=[over_ici(s_glu)])
            dlr, dli, dld, dbr, dbi = _s5_disc_bwd(
                "s5_disc_bwd", p_lr, p_li, p_ld, p_br, p_bi, dab_re.reshape(groups, 1, states),
                dab_im.reshape(groups, 1, states), _unband("unband_bb_re", dbb_re, per_sg),
                _unband("unband_bb_im", dbb_im, per_sg))
            grads['s5_lam_re'], grads['s5_lam_im'], grads['s5_log_dt'] = dlr[:, 0][None], dli[:, 0][None], dld[:, 0, 0][None]
            grads['s5_b_re'], grads['s5_b_im'] = dbr.transpose(0, 2, 1)[None], dbi.transpose(0, 2, 1)[None]
            grads['s5_c_re'] = _unband("unband_cc_re", dcc_re, per_sg).transpose(0, 2, 1)[None]
            grads['s5_c_im'] = _unband("unband_cc_im", dcc_im, per_sg).transpose(0, 2, 1)[None]
            grads['s5_d'] = dd
            dub = du.astype(MXU_DTYPE)
            p_s5_in = _mm_row_db(f"s5_in_db{i}", h1[None], dub, WIRE_DTYPE)
            dh1, (got,) = _ride(_mm_row_da, f"s5_in_da{i}", dub, gw['s5_in'].reshape(d, d), 1, riders=[to_sibling(p_s5_in)])
            dh1 = dh1[0]
            s_s5_in = pair("x_s5_w_in_pair", p_s5_in, got)
        else:
            zz, hs, ylru, o = mix
            g_lru_out = gw['lru_out'].reshape(lw, d)
            do, dg1 = _gate_bwd(f"lru_gate_bwd{i}", dx, o, g1)
            dyl, (got,) = _ride(_mm_row_da, f"lru_out_da{i}", do, g_lru_out, half, riders=[to_sibling(p_gu)])
            s_gu = pair(f"x_ffn_w_gu{i}_pair", p_gu, got)
            p_lru_out = _mm_row_db(f"lru_out_db{i}", ylru, do, WIRE_DTYPE)
            (dgb, dxp, dcv, dwrg, dwig), (parts['ffn_w_gu', i], got) = _ride(
                _lru_bwd, f"lru_core_bwd{i}", zz, hs, dyl, cvec, wrg, wig, riders=[over_ici(s_gu), to_sibling(p_lru_out)])
            s_lru_out = pair("x_lru_w_out_pair", p_lru_out, got)
            dzz = jnp.concatenate([dgb, dxp], axis=0)
            dh1, (parts['lru_w_out', 0],) = _ride(_mm_col_da, f"lru_in_da{i}", dzz, gw['lru_in'],
                                                  riders=[over_ici(s_lru_out)])
            above = (('lru_w_in', 0), _mm_col_db(f"lru_in_db{i}", h1, dzz, WIRE_DTYPE))
            dcv = dcv.transpose(1, 0, 2).reshape(taps + 4, lw)
            grads['lru_conv_w'] = dcv[:taps].reshape(1, taps, 1, lw)
            grads['lru_conv_b'], grads['lru_b_rg'] = dcv[taps:taps + 1], dcv[taps + 1:taps + 2]
            grads['lru_b_ig'], grads['lru_lam'] = dcv[taps + 2:taps + 3], dcv[taps + 3:taps + 4]
            grads['lru_w_rg'] = _unband("unband_w_rg", dwrg, LRU_BLOCKS_PER_CHUNK)[None]
            grads['lru_w_ig'] = _unband("unband_w_ig", dwig, LRU_BLOCKS_PER_CHUNK)[None]
        dx, dgn1, dsc1, dsh1 = _norm_mod_bwd(f"norm1_bwd{i}", xin, dh1, dx, gn[0:1], sc1)
        dmod[i] = jnp.concatenate([dsh1, dsc1, dg1, dsh2, dsc2, dg2], axis=1)
        d_norm_g[i] = jnp.concatenate([dgn1, dgn2], axis=0)
    grad_x = dx[None]
    dmod = jnp.concatenate(dmod, axis=0)
    grads['norm_g'] = jnp.stack(d_norm_g)
    grads['b_ada'] = dmod
    grads['final_g'] = d_final_g[0]

    small_partial = _pack([grads[n] for n in SMALL], SUBLANE * N_DEV)
    rows8 = small_partial.shape[0] // N_DEV
    small_partial = small_partial.reshape(N_DEV, rows8, LANE)
    s_small = pair("x_small_pair", small_partial, _ride_alone("x_small_d2d", _sibling_rider(small_partial)))
    parts['s5_w_in', 0], small_parts = _ride_alone("x_tail_ici", _join([over_ici(s_s5_in), over_ici(s_small)]))

    out = {}
    dmod_all = _all_gather("ag_dmod", dmod)
    dmod_loc = lax.dynamic_slice_in_dim(dmod_all, me * n_loc, n_loc, axis=2).transpose(1, 0, 2)
    dmod16 = jnp.pad(dmod_loc, ((0, 0), (0, 2 * SUBLANE - N_DEV), (0, 0)))
    out['w_ada'] = _adamw_w_ada("adamw_w_ada", c16, dmod16, w_ada, m_w_ada, v_w_ada)

    for name in BIG[1:]:
        w = wv[name]
        rows, cols = w.shape[-2] * w.shape[0], w.shape[-1]
        flat = lambda a: a.reshape(rows, cols)
        res = _adamw_sum("adamw_" + name, [parts[name, l] for l in range(w.shape[0])], flat(w), flat(mv[name]),
                         flat(vv[name]))
        out[name] = [r.reshape(w.shape) for r in res]

    summed = _sum_parts("sum_small", small_parts)
    small_total = _all_gather("ag_small_sum", summed).reshape(-1, LANE)
    small_grad = dict(zip(SMALL, _unpack(small_total, [grads[n].shape for n in SMALL])))
    for n in SMALL_SHARDED:
        shard = wv[n].shape[-1]
        small_grad[n] = lax.dynamic_slice_in_dim(small_grad[n], me * shard, shard, axis=small_grad[n].ndim - 1)
    for n in SMALL:
        w = wv[n]
        flat = lambda a: a.reshape(-1, w.shape[-1])
        res = _adamw_sum("adamw_" + n, [flat(small_grad[n])[None]], flat(w), flat(mv[n]), flat(vv[n]))
        out[n] = [r.reshape(w.shape) for r in res]

    return (loss, grad_x, *[out[n][0] for n in WEIGHTS], *[out[n][1] for n in WEIGHTS],
            *[out[n][2] for n in WEIGHTS], *[out[n][3] for n in WEIGHTS])
```

```python
import functools
import math

import jax
import jax.numpy as jnp
from jax import lax
from jax.experimental import pallas as pl
from jax.experimental.pallas import tpu as pltpu

F32 = jnp.float32
MXU_DTYPE = jnp.bfloat16
WIRE_DTYPE = jnp.bfloat16
N_DEV = 8
EPS = 1e-6
LRU_C = 8.0
S5_GROUP = 16
S5_STATE = 64
S5_SUPER = 256
LRU_BLOCKS_PER_CHUNK = 4
ADAM_LR, ADAM_B1, ADAM_B2, ADAM_EPS, ADAM_WD, ADAM_STEP = 0.001, 0.9, 0.999, 1e-08, 0.01, 10
VMEM_LIMIT_BYTES = 56 * 1024 * 1024
LANE = 128
SUBLANE = 8

WEIGHTS = ['norm_g', 'w_ada', 'b_ada', 's5_w_in', 's5_lam_re', 's5_lam_im', 's5_log_dt', 's5_b_re', 's5_b_im',
           's5_c_re', 's5_c_im', 's5_d', 's5_w_glu', 'lru_w_in', 'lru_conv_w', 'lru_conv_b', 'lru_w_rg', 'lru_b_rg',
           'lru_w_ig', 'lru_b_ig', 'lru_lam', 'lru_w_out', 'ffn_w_gu', 'ffn_w_down', 'final_g']
BIG = ('w_ada', 's5_w_in', 's5_w_glu', 'lru_w_in', 'lru_w_out', 'ffn_w_gu', 'ffn_w_down')
SMALL = tuple(n for n in WEIGHTS if n not in BIG)
SMALL_SHARDED = ('norm_g', 'lru_conv_w', 'lru_conv_b', 'lru_b_rg', 'lru_b_ig', 'lru_lam')

NN = (((1,), (0,)), ((), ()))
NT = (((1,), (1,)), ((), ()))
TN = (((0,), (0,)), ((), ()))


def _params(n_grid):
    return pltpu.CompilerParams(dimension_semantics=("arbitrary",) * n_grid, vmem_limit_bytes=VMEM_LIMIT_BYTES)


def _tile(dim, pref, align=LANE):
    if dim <= pref:
        return dim
    t = (pref // align) * align
    while t >= align:
        if dim % t == 0:
            return t
        t -= align
    return dim


def _dot(a, b, dims):
    return lax.dot_general(a.astype(MXU_DTYPE), b.astype(MXU_DTYPE), dims, preferred_element_type=F32)


def _gelu(x):
    k = math.sqrt(2.0 / math.pi)
    return 0.5 * x * (1.0 + jnp.tanh(k * (x + 0.044715 * (x * x * x))))


def _gelu_and_grad(x):
    k = math.sqrt(2.0 / math.pi)
    th = jnp.tanh(k * (x + 0.044715 * (x * x * x)))
    g = 0.5 * x * (1.0 + th)
    dg = 0.5 * (1.0 + th) + 0.5 * x * (1.0 - th * th) * (k * (1.0 + 3.0 * 0.044715 * (x * x)))
    return g, dg


def _neg_expm1(x):
    series = -x * (1.0 + x * (0.5 + x * (1.0 / 6.0 + x * (1.0 / 24.0 + x * (1.0 / 120.0)))))
    return jnp.where(x > -0.01, series, 1.0 - jnp.exp(x))


MESH = pl.DeviceIdType.MESH
N_CHIP = N_DEV // 2
ALL, SAME_CORE = 7, 6


def _place():
    x, y, c = lax.axis_index("x"), lax.axis_index("y"), lax.axis_index("c")
    return x, y, c


def _flip(place, k):
    x, y, c = place
    return (1 - x if (k >> 2) & 1 else x, 1 - y if (k >> 1) & 1 else y, 1 - c if k & 1 else c)


def _chunk_exchange(name, xs, group):
    return _ride_alone(name, _chunk_rider(xs, group))


class _Rider:
    def __init__(self, arrays, out_shape, scratch, start, finish, post, mid=None):
        self.arrays, self.out_shape, self.scratch = list(arrays), list(out_shape), list(scratch)
        self.start, self.finish, self.post, self.mid = start, finish, post, mid


def _chunk_rider(xs, group, rows=None):
    n = len(xs)
    members, r_all, c_ = xs[0].shape
    r0, r = (0, r_all) if rows is None else rows
    assert members == {ALL: N_DEV, SAME_CORE: N_CHIP}[group]
    assert all(a.shape == xs[0].shape and a.dtype == xs[0].dtype for a in xs)
    ks = [k for k in range(1, N_DEV) if not k & ~group]
    member = (lambda p: 4 * p[0] + 2 * p[1] + p[2]) if group == ALL else (lambda p: 2 * p[0] + p[1])

    def copies(ins, outs, scratch):
        out = outs[0]
        send_sems, recv_sems, local_sems = scratch
        place = _place()
        me = member(place)
        src = lambda l, who: ins[l].at[who] if rows is None else ins[l].at[who, pl.ds(r0, r)]
        local = [pltpu.make_async_copy(src(l, me), out.at[me, l], local_sems.at[l]) for l in range(n)]
        remote = []
        for l in range(n):
            for k in ks:
                pid = _flip(place, k)
                peer = member(pid)

                def copy(land_at, l=l, k=k, peer=peer, pid=pid):
                    return pltpu.make_async_remote_copy(
                        src_ref=src(l, peer), dst_ref=out.at[land_at, l], send_sem=send_sems.at[l * N_DEV + k],
                        recv_sem=recv_sems.at[l * N_DEV + k], device_id=pid, device_id_type=MESH)

                remote.append((copy, me, peer))
        return local, remote

    def start(ins, outs, scratch):
        local, remote = copies(ins, outs, scratch)
        for cp in local:
            cp.start()
        for copy, me, _ in remote:
            copy(me).start()

    def finish(ins, outs, scratch):
        local, remote = copies(ins, outs, scratch)
        for copy, me, peer in remote:
            copy(me).wait_send()
            copy(peer).wait_recv()
        for cp in local:
            cp.wait()

    return _Rider(
        xs, [jax.ShapeDtypeStruct((members, n, r, c_), xs[0].dtype)],
        [pltpu.SemaphoreType.DMA((n * N_DEV,)), pltpu.SemaphoreType.DMA((n * N_DEV,)), pltpu.SemaphoreType.DMA((n,))],
        start, finish, lambda outs: outs[0].reshape(members, n * r, c_))


def _gather_rider(xs):
    n = len(xs)
    per = 7

    def plan(ins, outs, scratch):
        send_sems, recv_sems, local_sems = scratch
        x, y, c = place = _place()
        sibling, x_nb, y_nb, diag = (x, y, 1 - c), (1 - x, y, c), (x, 1 - y, c), (1 - x, 1 - y, c)
        relayed = (x + c * (1 - 2 * x), y + (1 - c) * (1 - 2 * y), c)
        onward = (x + (1 - c) * (1 - 2 * x), y + c * (1 - 2 * y), c)
        jobs = []
        for l in range(n):
            slot = lambda p, l=l: outs[l].at[2 * p[0] + p[1], p[2]]

            def copy(k, block, to, src=None, l=l, slot=slot):
                return pltpu.make_async_remote_copy(
                    src_ref=slot(block) if src is None else src, dst_ref=slot(block), send_sem=send_sems.at[l * per + k],
                    recv_sem=recv_sems.at[l * per + k], device_id=to, device_id_type=MESH)

            jobs.append(dict(
                mine=pltpu.make_async_copy(ins[l], slot(place), local_sems.at[l]),
                first=[copy(0, place, sibling, src=ins[l]), copy(1, place, x_nb, src=ins[l]), copy(2, place, y_nb, src=ins[l])],
                landed=[copy(1, x_nb, place), copy(2, y_nb, place)],
                second=[copy(3, relayed, onward), copy(4, x_nb, sibling), copy(5, y_nb, sibling)],
                relay_landed=copy(3, diag, place), last=copy(6, diag, sibling),
                from_sibling=[copy(0, sibling, place)] + [copy(4 + j, (p[0], p[1], 1 - c), place)
                                                          for j, p in enumerate((x_nb, y_nb, diag))]))
        return jobs

    def start(ins, outs, scratch):
        for job in plan(ins, outs, scratch):
            job['mine'].start()
            for cp in job['first']:
                cp.start()

    def mid(ins, outs, scratch):
        for job in plan(ins, outs, scratch):
            for cp in job['landed']:
                cp.wait_recv()
            for cp in job['second']:
                cp.start()

    def finish(ins, outs, scratch):
        jobs = plan(ins, outs, scratch)
        for job in jobs:
            job['relay_landed'].wait_recv()
            job['last'].start()
        for job in jobs:
            for cp in job['from_sibling']:
                cp.wait_recv()
            for cp in job['first'] + job['second'] + [job['last']]:
                cp.wait_send()
            job['mine'].wait()

    return _Rider(
        xs, [jax.ShapeDtypeStruct((N_CHIP, 2) + x.shape, x.dtype) for x in xs],
        [pltpu.SemaphoreType.DMA((n * per,)), pltpu.SemaphoreType.DMA((n * per,)), pltpu.SemaphoreType.DMA((n,))],
        start, finish, lambda outs: [o.reshape((N_DEV,) + x.shape) for o, x in zip(outs, xs)], mid=mid)


HBM_SPEC = pl.BlockSpec(memory_space=pltpu.HBM)


def _ride_alone(name, rider):
    n_in, n_out = len(rider.arrays), len(rider.out_shape)

    def body(*refs):
        parts = refs[:n_in], refs[n_in:n_in + n_out], refs[n_in + n_out:]
        rider.start(*parts)
        if rider.mid is not None:
            rider.mid(*parts)
        rider.finish(*parts)

    outs = pl.pallas_call(body, name=name, out_shape=rider.out_shape, in_specs=[HBM_SPEC] * n_in,
                          out_specs=[HBM_SPEC] * n_out, scratch_shapes=rider.scratch)(*rider.arrays)
    return rider.post(list(outs))


def _call(body, *, name, grid, in_specs, out_specs, out_shape, scratch_shapes=(), args, rider=None):
    single = not isinstance(out_shape, (list, tuple))
    out_shape = [out_shape] if single else list(out_shape)
    out_specs = [out_specs] if single else list(out_specs)
    scratch_shapes = list(scratch_shapes)
    unwrap = lambda outs: outs[0] if single else list(outs)
    if rider is None:
        outs = pl.pallas_call(body, name=name, grid=grid, in_specs=list(in_specs), out_specs=out_specs, out_shape=out_shape,
                              scratch_shapes=scratch_shapes, compiler_params=_params(len(grid)))(*args)
        return unwrap(outs)
    n_in, n_out, n_scr = len(in_specs), len(out_shape), len(scratch_shapes)
    r_in, r_out = len(rider.arrays), len(rider.out_shape)

    def carried(*refs):
        ins, refs = refs[:n_in], refs[n_in:]
        r_ins, refs = refs[:r_in], refs[r_in:]
        outs, refs = refs[:n_out], refs[n_out:]
        r_outs, refs = refs[:r_out], refs[r_out:]
        scr, r_scr = refs[:n_scr], refs[n_scr:]
        step = 0
        for ax, g in enumerate(grid):
            step = step * g + pl.program_id(ax)

        @pl.when(step == 0)
        def _():
            rider.start(r_ins, r_outs, r_scr)

        body(*ins, *outs, *scr)

        if rider.mid is not None and total > 1:
            @pl.when(step == (5 * (total - 1)) // 8)
            def _():
                rider.mid(r_ins, r_outs, r_scr)

        @pl.when(step == total - 1)
        def _():
            if rider.mid is not None and total == 1:
                rider.mid(r_ins, r_outs, r_scr)
            rider.finish(r_ins, r_outs, r_scr)

    total = math.prod(grid)

    outs = pl.pallas_call(
        carried, name=name, grid=grid, in_specs=list(in_specs) + [HBM_SPEC] * r_in, out_specs=out_specs + [HBM_SPEC] * r_out,
        out_shape=out_shape + rider.out_shape, scratch_shapes=scratch_shapes + rider.scratch,
        compiler_params=_params(len(grid)))(*args, *rider.arrays)
    return unwrap(outs[:n_out]), rider.post(list(outs[n_out:]))


def _all_gather(name, x):
    return _ride_alone(name, _gather_rider([x]))[0]


def _sibling_rider(x):
    _, r, c_ = x.shape

    def copies(ins, outs, scratch):
        send_sems, recv_sems = scratch
        place = _place()
        return [pltpu.make_async_remote_copy(
            src_ref=ins[0].at[2 * chip + (1 - place[2])], dst_ref=outs[0].at[chip], send_sem=send_sems.at[chip],
            recv_sem=recv_sems.at[chip], device_id=_flip(place, 1), device_id_type=MESH) for chip in range(N_CHIP)]

    def start(ins, outs, scratch):
        for cp in copies(ins, outs, scratch):
            cp.start()

    def finish(ins, outs, scratch):
        for cp in copies(ins, outs, scratch):
            cp.wait()

    return _Rider([x], [jax.ShapeDtypeStruct((N_CHIP, r, c_), x.dtype)],
                  [pltpu.SemaphoreType.DMA((N_CHIP,)), pltpu.SemaphoreType.DMA((N_CHIP,))], start, finish, lambda outs: outs[0])


def _join(riders):
    def cut(seq, counts):
        out, off = [], 0
        for k in counts:
            out.append(seq[off:off + k])
            off += k
        return out

    def parts(ins, outs, scratch):
        return zip(riders, cut(ins, [len(r.arrays) for r in riders]), cut(outs, [len(r.out_shape) for r in riders]),
                   cut(scratch, [len(r.scratch) for r in riders]))

    def start(ins, outs, scratch):
        for r, i, o, s in parts(ins, outs, scratch):
            r.start(i, o, s)

    def finish(ins, outs, scratch):
        for r, i, o, s in parts(ins, outs, scratch):
            r.finish(i, o, s)

    def mid(ins, outs, scratch):
        for r, i, o, s in parts(ins, outs, scratch):
            if r.mid is not None:
                r.mid(i, o, s)

    return _Rider(
        [a for r in riders for a in r.arrays], [o for r in riders for o in r.out_shape], [s for r in riders for s in r.scratch],
        start, finish, lambda outs: [r.post(o) for r, o in zip(riders, cut(outs, [len(r.out_shape) for r in riders]))],
        mid=mid if any(r.mid is not None for r in riders) else None)


def _ride(fn, *args, riders):
    return fn(*args, rider=_join(riders))


def _pair_sum(name, x, got, core):
    _, r, c_ = x.shape
    br = _tile(r, max(256, (1 << 21) // c_), 2 * SUBLANE)

    def body(core_ref, x_ref, g_ref, o_ref):
        o_ref[...] = (x_ref[...].astype(F32) + g_ref[...].astype(F32)).astype(o_ref.dtype)

    return pl.pallas_call(
        body, name=name, out_shape=jax.ShapeDtypeStruct((N_CHIP, r, c_), x.dtype),
        grid_spec=pltpu.PrefetchScalarGridSpec(
            num_scalar_prefetch=1, grid=(N_CHIP, r // br),
            in_specs=[pl.BlockSpec((None, br, c_), lambda ch, i, core_ref: (2 * ch + core_ref[0], i, 0)),
                      pl.BlockSpec((None, br, c_), lambda ch, i, core_ref: (ch, i, 0))],
            out_specs=pl.BlockSpec((None, br, c_), lambda ch, i, core_ref: (ch, i, 0))),
        compiler_params=_params(2))(core, x, got)


def _mm(name, a, b, out_shape, out_dtype, grid, a_spec, b_spec, o_spec, dims, n_red, acc_shape, rider=None):
    red = tuple(range(len(grid) - n_red, len(grid)))
    out_type = jax.ShapeDtypeStruct(out_shape, out_dtype)
    if all(grid[ax] == 1 for ax in red):
        def single(a_ref, b_ref, o_ref):
            o_ref[...] = _dot(a_ref[...], b_ref[...], dims).astype(o_ref.dtype)

        return _call(single, name=name, out_shape=out_type, grid=grid, in_specs=[a_spec, b_spec], out_specs=o_spec,
                     args=(a, b), rider=rider)

    def body(a_ref, b_ref, o_ref, acc_ref):
        first = functools.reduce(jnp.logical_and, [pl.program_id(ax) == 0 for ax in red])
        last = functools.reduce(jnp.logical_and, [pl.program_id(ax) == grid[ax] - 1 for ax in red])

        @pl.when(first)
        def _():
            acc_ref[...] = jnp.zeros_like(acc_ref)

        acc_ref[...] += _dot(a_ref[...], b_ref[...], dims)

        @pl.when(last)
        def _():
            o_ref[...] = acc_ref[...].astype(o_ref.dtype)

    return _call(body, name=name, out_shape=out_type, grid=grid, in_specs=[a_spec, b_spec], out_specs=o_spec,
                 scratch_shapes=[pltpu.VMEM(acc_shape, F32)], args=(a, b), rider=rider)


def _mm_col(name, a, b, out_dtype=F32, rider=None):
    m, k = a.shape
    j, _, n = b.shape
    bm, bk = _tile(m, 1024), _tile(k, 2048)
    return _mm(name, a, b, (j, m, n), out_dtype, (j, m // bm, k // bk),
               pl.BlockSpec((bm, bk), lambda jj, mm, kk: (mm, kk)),
               pl.BlockSpec((None, bk, n), lambda jj, mm, kk: (jj, kk, 0)),
               pl.BlockSpec((None, bm, n), lambda jj, mm, kk: (jj, mm, 0)), NN, 1, (bm, n), rider)


def _mm_col_da(name, do, b, rider=None):
    j, m, n = do.shape
    k = b.shape[1]
    bm, bk = _tile(m, 1024), _tile(k, 1024)
    return _mm(name, do, b, (m, k), F32, (m // bm, k // bk, j),
               pl.BlockSpec((None, bm, n), lambda mm, kk, jj: (jj, mm, 0)),
               pl.BlockSpec((None, bk, n), lambda mm, kk, jj: (jj, kk, 0)),
               pl.BlockSpec((bm, bk), lambda mm, kk, jj: (mm, kk)), NT, 1, (bm, bk), rider)


def _mm_col_db(name, a, do, out_dtype, rider=None):
    m, k = a.shape
    j, _, n = do.shape
    bm, bk = _tile(m, 2048), _tile(k, 512)
    return _mm(name, a, do, (j, k, n), out_dtype, (j, k // bk, m // bm),
               pl.BlockSpec((bm, bk), lambda jj, kk, mm: (mm, kk)),
               pl.BlockSpec((None, bm, n), lambda jj, kk, mm: (jj, mm, 0)),
               pl.BlockSpec((None, bk, n), lambda jj, kk, mm: (jj, kk, 0)), TN, 1, (bk, n), rider)


def _row_bk(kq):
    return kq if (kq % LANE or kq // LANE in (11,)) else _tile(kq, 2048)


def _mm_row(name, a, b, out_dtype=F32, rider=None):
    q, m, kq = a.shape
    n = b.shape[1]
    bm, bn, bk = _tile(m, 1024), _tile(n, 1024), _row_bk(kq)
    nk = kq // bk
    return _mm(name, a, b, (m, n), out_dtype, (m // bm, n // bn, q, nk),
               pl.BlockSpec((None, bm, bk), lambda mm, nn, qq, kk: (qq, mm, kk)),
               pl.BlockSpec((bk, bn), lambda mm, nn, qq, kk: (qq * nk + kk, nn)),
               pl.BlockSpec((bm, bn), lambda mm, nn, qq, kk: (mm, nn)), NN, 2, (bm, bn), rider)


def _mm_row_da(name, do, b, q, rider=None):
    m, n = do.shape
    kq = b.shape[0] // q
    bm, bn = _tile(m, 1024), _tile(n, 2048)
    return _mm(name, do, b, (q, m, kq), F32, (q, m // bm, n // bn),
               pl.BlockSpec((bm, bn), lambda qq, mm, nn: (mm, nn)),
               pl.BlockSpec((kq, bn), lambda qq, mm, nn: (qq, nn)),
               pl.BlockSpec((None, bm, kq), lambda qq, mm, nn: (qq, mm, 0)), NT, 1, (bm, kq), rider)


def _mm_row_db(name, a, do, out_dtype):
    q, m, kq = a.shape
    n = do.shape[1]
    bm, bn = _tile(m, 2048), _tile(n, 512)
    return _mm(name, a, do, (q * kq, n), out_dtype, (q, n // bn, m // bm),
               pl.BlockSpec((None, bm, kq), lambda qq, nn, mm: (qq, mm, 0)),
               pl.BlockSpec((bm, bn), lambda qq, nn, mm: (mm, nn)),
               pl.BlockSpec((kq, bn), lambda qq, nn, mm: (qq, nn)), TN, 1, (kq, bn))


def _row_spec(bm, d):
    return pl.BlockSpec((bm, d), lambda i: (i, 0))


def _vec_spec(d):
    return pl.BlockSpec((1, d), lambda i: (0, 0))


def _norm_mod_fwd(name, x, gain, sc, sh):
    t, d = x.shape
    bm = _tile(t, 256, SUBLANE)

    def body(x_ref, g_ref, sc_ref, sh_ref, h_ref):
        xv = x_ref[...]
        rstd = lax.rsqrt(jnp.mean(xv * xv, axis=-1, keepdims=True) + EPS)
        h_ref[...] = ((xv * rstd) * g_ref[...] * (1.0 + sc_ref[...]) + sh_ref[...]).astype(h_ref.dtype)

    return pl.pallas_call(
        body, name=name, out_shape=jax.ShapeDtypeStruct((t, d), MXU_DTYPE), grid=(t // bm,),
        in_specs=[_row_spec(bm, d), _vec_spec(d), _vec_spec(d), _vec_spec(d)], out_specs=_row_spec(bm, d),
        compiler_params=_params(1))(x, gain, sc, sh)


def _norm_mod_bwd(name, x, dh, dres, gain, sc):
    t, d = x.shape
    bm = _tile(t, 256, SUBLANE)

    def body(x_ref, dh_ref, dres_ref, g_ref, sc_ref, dx_ref, dg_ref, dsc_ref, dsh_ref):
        @pl.when(pl.program_id(0) == 0)
        def _():
            dg_ref[...] = jnp.zeros_like(dg_ref)
            dsc_ref[...] = jnp.zeros_like(dsc_ref)
            dsh_ref[...] = jnp.zeros_like(dsh_ref)

        xv, dh_ = x_ref[...], dh_ref[...]
        rstd = lax.rsqrt(jnp.mean(xv * xv, axis=-1, keepdims=True) + EPS)
        nrm = xv * rstd
        gain_ = g_ref[...]
        dsh_ref[...] += jnp.sum(dh_, axis=0, keepdims=True)
        dsc_ref[...] += jnp.sum(dh_ * (nrm * gain_), axis=0, keepdims=True)
        dhn = dh_ * (1.0 + sc_ref[...])
        dg_ref[...] += jnp.sum(dhn * nrm, axis=0, keepdims=True)
        dn = dhn * gain_
        dx_ref[...] = dres_ref[...] + rstd * (dn - nrm * jnp.mean(dn * nrm, axis=-1, keepdims=True))

    vec = jax.ShapeDtypeStruct((1, d), F32)
    return pl.pallas_call(
        body, name=name, out_shape=[jax.ShapeDtypeStruct((t, d), F32), vec, vec, vec], grid=(t // bm,),
        in_specs=[_row_spec(bm, d), _row_spec(bm, d), _row_spec(bm, d), _vec_spec(d), _vec_spec(d)],
        out_specs=[_row_spec(bm, d), _vec_spec(d), _vec_spec(d), _vec_spec(d)],
        compiler_params=_params(1))(x, dh, dres, gain, sc)


def _loss_bwd(name, x, target, gain):
    t, d = x.shape
    bm = _tile(t, 256, SUBLANE)

    def body(x_ref, t_ref, g_ref, dx_ref, loss_ref, dg_ref):
        @pl.when(pl.program_id(0) == 0)
        def _():
            loss_ref[...] = jnp.zeros_like(loss_ref)
            dg_ref[...] = jnp.zeros_like(dg_ref)

        xv = x_ref[...]
        rstd = lax.rsqrt(jnp.mean(xv * xv, axis=-1, keepdims=True) + EPS)
        nrm = xv * rstd
        gain_ = g_ref[...]
        err = nrm * gain_ - t_ref[...]
        per_tok = jnp.mean(err * err, axis=-1, keepdims=True)
        loss_ref[...] += 0.5 * jnp.sum(per_tok, axis=0, keepdims=True)
        dout = err * (1.0 / d)
        dg_ref[...] += jnp.sum(dout * nrm, axis=0, keepdims=True)
        dn = dout * gain_
        dx_ref[...] = rstd * (dn - nrm * jnp.mean(dn * nrm, axis=-1, keepdims=True))

    return pl.pallas_call(
        body, name=name,
        out_shape=[jax.ShapeDtypeStruct((t, d), F32), jax.ShapeDtypeStruct((1, 1), F32),
                   jax.ShapeDtypeStruct((1, d), F32)],
        grid=(t // bm,), in_specs=[_row_spec(bm, d), _row_spec(bm, d), _vec_spec(d)],
        out_specs=[_row_spec(bm, d), pl.BlockSpec((1, 1), lambda i: (0, 0)), _vec_spec(d)],
        compiler_params=_params(1))(x, target, gain)


def _resid(name, x, y, g):
    t, d = x.shape
    bm = _tile(t, 256, SUBLANE)

    def body(x_ref, y_ref, g_ref, o_ref):
        o_ref[...] = x_ref[...] + g_ref[...] * y_ref[...]

    return pl.pallas_call(
        body, name=name, out_shape=jax.ShapeDtypeStruct((t, d), F32), grid=(t // bm,),
        in_specs=[_row_spec(bm, d), _row_spec(bm, d), _vec_spec(d)], out_specs=_row_spec(bm, d),
        compiler_params=_params(1))(x, y, g)


def _gate_bwd(name, dx, y, g):
    t, d = dx.shape
    bm = _tile(t, 256, SUBLANE)

    def body(dx_ref, y_ref, g_ref, dy_ref, dg_ref):
        @pl.when(pl.program_id(0) == 0)
        def _():
            dg_ref[...] = jnp.zeros_like(dg_ref)

        dxv = dx_ref[...]
        dy_ref[...] = (g_ref[...] * dxv).astype(dy_ref.dtype)
        dg_ref[...] += jnp.sum(dxv * y_ref[...], axis=0, keepdims=True)

    return pl.pallas_call(
        body, name=name, out_shape=[jax.ShapeDtypeStruct((t, d), MXU_DTYPE), jax.ShapeDtypeStruct((1, d), F32)],
        grid=(t // bm,), in_specs=[_row_spec(bm, d), _row_spec(bm, d), _vec_spec(d)],
        out_specs=[_row_spec(bm, d), _vec_spec(d)], compiler_params=_params(1))(dx, y, g)


def _glu_resid_fwd(name, z, x, g):
    _, t, n = z.shape
    d = x.shape[1]
    half = N_DEV // 2
    bm = _tile(t, 256, SUBLANE)

    def body(v_ref, gt_ref, x_ref, g_ref, o_ref):
        o_ref[...] = x_ref[...] + g_ref[...] * (v_ref[...] * jax.nn.sigmoid(gt_ref[...]))

    return pl.pallas_call(
        body, name=name, out_shape=jax.ShapeDtypeStruct((t, d), F32), grid=(half, t // bm),
        in_specs=[pl.BlockSpec((None, bm, n), lambda q, i: (q, i, 0)),
                  pl.BlockSpec((None, bm, n), lambda q, i: (q + half, i, 0)),
                  pl.BlockSpec((bm, n), lambda q, i: (i, q)), pl.BlockSpec((1, n), lambda q, i: (0, q))],
        out_specs=pl.BlockSpec((bm, n), lambda q, i: (i, q)), compiler_params=_params(2))(z, z, x, g)


def _glu_resid_bwd(name, z, dx, g):
    _, t, n = z.shape
    d = dx.shape[1]
    half = N_DEV // 2
    bm = _tile(t, 256, SUBLANE)

    def body(z_ref, dx_ref, g_ref, dz_ref, dg_ref):
        @pl.when(pl.program_id(1) == 0)
        def _():
            dg_ref[...] = jnp.zeros_like(dg_ref)

        v, dxv = z_ref[0], dx_ref[...]
        sig = jax.nn.sigmoid(z_ref[1])
        dout = g_ref[...] * dxv
        dg_ref[...] += jnp.sum(dxv * (v * sig), axis=0, keepdims=True)
        dz_ref[0] = (dout * sig).astype(dz_ref.dtype)
        dz_ref[1] = (dout * v * (sig * (1.0 - sig))).astype(dz_ref.dtype)

    pair = pl.BlockSpec((2, None, bm, n), lambda q, i: (0, q, i, 0))
    dz, dg = pl.pallas_call(
        body, name=name,
        out_shape=[jax.ShapeDtypeStruct((2, half, t, n), MXU_DTYPE), jax.ShapeDtypeStruct((1, d), F32)],
        grid=(half, t // bm),
        in_specs=[pair, pl.BlockSpec((bm, n), lambda q, i: (i, q)), pl.BlockSpec((1, n), lambda q, i: (0, q))],
        out_specs=[pair, pl.BlockSpec((1, n), lambda q, i: (0, q))],
        compiler_params=_params(2))(z.reshape(2, half, t, n), dx, g)
    return dz.reshape(N_DEV, t, n), dg


def _swiglu_act_fwd(name, gu):
    _, t, n = gu.shape
    half = N_DEV // 2
    bm = _tile(t, 256, SUBLANE)

    def body(g_ref, u_ref, o_ref):
        gv = g_ref[...].astype(F32)
        o_ref[...] = (gv * jax.nn.sigmoid(gv) * u_ref[...].astype(F32)).astype(o_ref.dtype)

    return pl.pallas_call(
        body, name=name, out_shape=jax.ShapeDtypeStruct((half, t, n), MXU_DTYPE), grid=(half, t // bm),
        in_specs=[pl.BlockSpec((None, bm, n), lambda q, i: (q, i, 0)),
                  pl.BlockSpec((None, bm, n), lambda q, i: (q + half, i, 0))],
        out_specs=pl.BlockSpec((None, bm, n), lambda q, i: (q, i, 0)), compiler_params=_params(2))(gu, gu)


def _swiglu_act_bwd(name, gu, dact, rider=None):
    _, t, n = gu.shape
    half = N_DEV // 2
    bm = _tile(t, 256, SUBLANE)

    def body(gu_ref, da_ref, o_ref):
        gv, da = gu_ref[0].astype(F32), da_ref[...]
        sig = jax.nn.sigmoid(gv)
        o_ref[0] = (da * gu_ref[1].astype(F32) * (sig * (1.0 + gv * (1.0 - sig)))).astype(o_ref.dtype)
        o_ref[1] = (da * (gv * sig)).astype(o_ref.dtype)

    pair = pl.BlockSpec((2, None, bm, n), lambda q, i: (0, q, i, 0))
    res = _call(
        body, name=name, out_shape=jax.ShapeDtypeStruct((2, half, t, n), MXU_DTYPE), grid=(half, t // bm),
        in_specs=[pair, pl.BlockSpec((None, bm, n), lambda q, i: (q, i, 0))], out_specs=pair,
        args=(gu.reshape(2, half, t, n), dact), rider=rider)
    if rider is None:
        return res.reshape(N_DEV, t, n)
    return res[0].reshape(N_DEV, t, n), res[1]


def _ada_fwd(name, c16, w_ada, b_loc):
    nl, d, n = w_ada.shape
    bn = _tile(n, 512)

    def body(c_ref, w_ref, b_ref, o_ref):
        cv = c_ref[...]
        o_ref[...] = _dot(cv * jax.nn.sigmoid(cv), w_ref[...], NN) + b_ref[...]

    return pl.pallas_call(
        body, name=name, out_shape=jax.ShapeDtypeStruct((nl, c16.shape[0], n), F32), grid=(nl, n // bn),
        in_specs=[pl.BlockSpec(c16.shape, lambda i, j: (0, 0)), pl.BlockSpec((None, d, bn), lambda i, j: (i, 0, j)),
                  pl.BlockSpec((None, 1, bn), lambda i, j: (i, 0, j))],
        out_specs=pl.BlockSpec((None, c16.shape[0], bn), lambda i, j: (i, 0, j)),
        compiler_params=_params(2))(c16, w_ada, b_loc)


def _adam_update(g, w, m, v):
    m = ADAM_B1 * m + (1.0 - ADAM_B1) * g
    v = ADAM_B2 * v + (1.0 - ADAM_B2) * (g * g)
    m_hat = m / (1.0 - ADAM_B1 ** ADAM_STEP)
    v_hat = v / (1.0 - ADAM_B2 ** ADAM_STEP)
    delta = -ADAM_LR * (m_hat / (jnp.sqrt(v_hat) + ADAM_EPS) + ADAM_WD * w)
    return delta, m, v


def _adamw_w_ada(name, c16, dmod16, w, m, v, rider=None):
    nl, d, n = w.shape
    br = _tile(d, 256)

    def body(c_ref, dm_ref, w_ref, m_ref, v_ref, g_ref, dl_ref, mo_ref, vo_ref):
        cv = c_ref[...]
        g = _dot(cv * jax.nn.sigmoid(cv), dm_ref[...], TN)
        g_ref[...] = g
        dl_ref[...], mo_ref[...], vo_ref[...] = _adam_update(g, w_ref[...], m_ref[...], v_ref[...])

    blk = pl.BlockSpec((None, br, n), lambda i, r: (i, r, 0))
    shp = jax.ShapeDtypeStruct(w.shape, F32)
    return _call(
        body, name=name, out_shape=[shp] * 4, grid=(nl, d // br),
        in_specs=[pl.BlockSpec((c16.shape[0], br), lambda i, r: (0, r)),
                  pl.BlockSpec((None, dmod16.shape[1], n), lambda i, r: (i, 0, 0)), blk, blk, blk],
        out_specs=[blk] * 4, args=(c16, dmod16, w, m, v), rider=rider)


def _adamw_sum(name, parts, w, m, v, rider=None):
    nl = len(parts)
    p, r, c = parts[0].shape
    br = _tile(r, max(128, (1 << 17) // max(c, LANE)), 2 * SUBLANE)
    nb = r // br

    def body(*refs):
        p_refs, (w_ref, m_ref, v_ref, g_ref, dl_ref, mo_ref, vo_ref) = refs[:nl], refs[nl:]
        layer = pl.program_id(0)
        g = None
        for l, p_ref in enumerate(p_refs):
            gl = p_ref[0].astype(F32)
            for s in range(1, p):
                gl = gl + p_ref[s].astype(F32)
            g = gl if g is None else jnp.where(layer == l, gl, g)
        g_ref[...] = g
        dl_ref[...], mo_ref[...], vo_ref[...] = _adam_update(g, w_ref[...], m_ref[...], v_ref[...])

    blk = pl.BlockSpec((br, c), lambda l, i: (l * nb + i, 0))
    shp = jax.ShapeDtypeStruct((nl * r, c), F32)
    return _call(
        body, name=name, out_shape=[shp] * 4, grid=(nl, nb),
        in_specs=[pl.BlockSpec((p, br, c), lambda l, i, k=k: (0, jnp.where(l == k, i, 0), 0)) for k in range(nl)]
        + [blk, blk, blk], out_specs=[blk] * 4,
        args=(*parts, w, m, v), rider=rider)


def _sum_parts(name, parts):
    p, r, c = parts.shape

    def body(p_ref, o_ref):
        g = p_ref[0]
        for s in range(1, p):
            g = g + p_ref[s]
        o_ref[...] = g

    return pl.pallas_call(body, name=name, out_shape=jax.ShapeDtypeStruct((r, c), F32))(parts)


def _s5_disc(name, lam_re, lam_im, log_dt, b_re, b_im):
    def body(lr_ref, li_ref, ld_ref, br_ref, bi_ref, ar_ref, ai_ref, bbr_ref, bbi_ref):
        lr, li = lr_ref[...], li_ref[...]
        dt = jnp.exp(ld_ref[...])
        mag = jnp.exp(lr * dt)
        a_re, a_im = mag * jnp.cos(li * dt), mag * jnp.sin(li * dt)
        nr, ni = a_re - 1.0, a_im
        den = lr * lr + li * li
        f_re, f_im = (nr * lr + ni * li) / den, (ni * lr - nr * li) / den
        br, bi = br_ref[...], bi_ref[...]
        ar_ref[...], ai_ref[...] = a_re, a_im
        bbr_ref[...] = f_re * br - f_im * bi
        bbi_ref[...] = f_re * bi + f_im * br

    s_a, s_b = jax.ShapeDtypeStruct(lam_re.shape, F32), jax.ShapeDtypeStruct(b_re.shape, F32)
    return pl.pallas_call(body, name=name, out_shape=[s_a, s_a, s_b, s_b])(lam_re, lam_im, log_dt, b_re, b_im)


def _s5_disc_bwd(name, lam_re, lam_im, log_dt, b_re, b_im, dab_re, dab_im, dbb_re, dbb_im):
    def body(lr_ref, li_ref, ld_ref, br_ref, bi_ref, dar_ref, dai_ref, dbbr_ref, dbbi_ref,
             dlr_ref, dli_ref, dld_ref, dbr_ref, dbi_ref):
        lr, li = lr_ref[...], li_ref[...]
        dt = jnp.exp(ld_ref[...])
        mag = jnp.exp(lr * dt)
        a_re, a_im = mag * jnp.cos(li * dt), mag * jnp.sin(li * dt)
        nr, ni = a_re - 1.0, a_im
        den = lr * lr + li * li
        f_re, f_im = (nr * lr + ni * li) / den, (ni * lr - nr * li) / den
        br, bi = br_ref[...], bi_ref[...]
        dbbr, dbbi = dbbr_ref[...], dbbi_ref[...]
        dbr_ref[...] = f_re * dbbr + f_im * dbbi
        dbi_ref[...] = f_re * dbbi - f_im * dbbr
        df_re = jnp.sum(dbbr * br + dbbi * bi, axis=1, keepdims=True)
        df_im = jnp.sum(dbbi * br - dbbr * bi, axis=1, keepdims=True)
        dnr = (df_re * lr - df_im * li) / den
        dni = (df_re * li + df_im * lr) / den
        dden = -(df_re * f_re + df_im * f_im) / den
        dlr = (df_re * nr + df_im * ni) / den + 2.0 * lr * dden
        dli = (df_re * ni - df_im * nr) / den + 2.0 * li * dden
        da_re, da_im = dar_ref[...] + dnr, dai_ref[...] + dni
        dmag_mag = da_re * a_re + da_im * a_im
        dth = da_im * a_re - da_re * a_im
        dlr_ref[...] = dlr + dmag_mag * dt
        dli_ref[...] = dli + dth * dt
        ddt = jnp.sum(dmag_mag * lr + dth * li, axis=2, keepdims=True)
        dld_ref[...] = ddt * dt

    s_a, s_b = jax.ShapeDtypeStruct(lam_re.shape, F32), jax.ShapeDtypeStruct(b_re.shape, F32)
    return pl.pallas_call(
        body, name=name, out_shape=[s_a, s_a, jax.ShapeDtypeStruct(log_dt.shape, F32), s_b, s_b],
    )(lam_re, lam_im, log_dt, b_re, b_im, dab_re, dab_im, dbb_re, dbb_im)


def _s5_time_block(t):
    return _tile(t, 128, SUBLANE)


def _s5_scan_fwd(name, u, bb_re, bb_im, ab_re, ab_im, cc_re, cc_im, dskip, rider=None):
    t, d = u.shape
    nsg, cs, ns = bb_re.shape
    tb = _s5_time_block(t)

    def body(u_ref, bbr_hbm, bbi_hbm, ar_ref, ai_ref, ccr_hbm, cci_hbm, d_ref, sr_ref, si_ref, srm_ref, sim_ref, yp_ref,
             ya_ref, bbr, bbi, ccr, cci, cr_ref, ci_ref):
        @pl.when(pl.program_id(0) == 0)
        def _():
            pltpu.sync_copy(bbr_hbm, bbr)
            pltpu.sync_copy(bbi_hbm, bbi)
            pltpu.sync_copy(ccr_hbm, ccr)
            pltpu.sync_copy(cci_hbm, cci)
            cr_ref[...] = jnp.zeros_like(cr_ref)
            ci_ref[...] = jnp.zeros_like(ci_ref)

        for sg in range(nsg):
            us = u_ref[:, sg * cs:(sg + 1) * cs]
            sr_ref[:, sg, :] = _dot(us, bbr[sg], NN)
            si_ref[:, sg, :] = _dot(us, bbi[sg], NN)
        ar, ai = ar_ref[...], ai_ref[...]

        def step(i, carry):
            cr, ci = carry
            nr = ar * cr - ai * ci + sr_ref[i]
            ni = ar * ci + ai * cr + si_ref[i]
            sr_ref[i] = nr
            si_ref[i] = ni
            return nr, ni

        cr, ci = lax.fori_loop(0, tb, step, (cr_ref[...], ci_ref[...]), unroll=2)
        cr_ref[...], ci_ref[...] = cr, ci
        srm_ref[...] = jnp.swapaxes(sr_ref[...], 0, 1).astype(MXU_DTYPE)
        sim_ref[...] = jnp.swapaxes(si_ref[...], 0, 1).astype(MXU_DTYPE)
        for sg in range(nsg):
            cols = slice(sg * cs, (sg + 1) * cs)
            y = _dot(srm_ref[sg], ccr[sg], NN) - _dot(sim_ref[sg], cci[sg], NN) + d_ref[:, cols] * u_ref[:, cols]
            yp_ref[:, cols] = y
            ya_ref[:, cols] = _gelu(y).astype(ya_ref.dtype)

    scan = jax.ShapeDtypeStruct((t, nsg, ns), F32)
    mxu = jax.ShapeDtypeStruct((nsg, t, ns), MXU_DTYPE)
    hbm = pl.BlockSpec(memory_space=pltpu.HBM)
    full = pl.BlockSpec((nsg, ns), lambda i: (0, 0))
    return _call(
        body, name=name,
        out_shape=[scan, scan, mxu, mxu, jax.ShapeDtypeStruct((t, d), F32), jax.ShapeDtypeStruct((t, d), MXU_DTYPE)],
        grid=(t // tb,), in_specs=[_row_spec(tb, d), hbm, hbm, full, full, hbm, hbm, _vec_spec(d)],
        out_specs=[pl.BlockSpec((tb, nsg, ns), lambda i: (i, 0, 0))] * 2 + [pl.BlockSpec((nsg, tb, ns), lambda i: (0, i, 0))] * 2
        + [_row_spec(tb, d)] * 2,
        scratch_shapes=[pltpu.VMEM(bb_re.shape, bb_re.dtype), pltpu.VMEM(bb_im.shape, bb_im.dtype),
                        pltpu.VMEM(cc_re.shape, cc_re.dtype), pltpu.VMEM(cc_im.shape, cc_im.dtype),
                        pltpu.VMEM((nsg, ns), F32), pltpu.VMEM((nsg, ns), F32)],
        args=(u, bb_re, bb_im, ab_re, ab_im, cc_re, cc_im, dskip), rider=rider)


def _s5_scan_bwd(name, dyact, ypre, cc_re, cc_im, ab_re, ab_im, s_re, s_im, rider=None):
    t, d = dyact.shape
    nsg, ns, cs = cc_re.shape
    tb = _s5_time_block(t)
    nb = t // tb

    def body(dya_ref, yp_ref, ccr_hbm, cci_hbm, ar_ref, ai_ref, sr_ref, si_ref, dy_ref, lrm_ref, lim_ref, dar_ref, dai_ref,
             ccr, cci, lr_ref, li_ref, cr_ref, ci_ref):
        dy_ref[...] = (dya_ref[...] * _gelu_and_grad(yp_ref[...])[1]).astype(dy_ref.dtype)

        @pl.when(pl.program_id(0) == 0)
        def _():
            pltpu.sync_copy(ccr_hbm, ccr)
            pltpu.sync_copy(cci_hbm, cci)
            cr_ref[...] = jnp.zeros_like(cr_ref)
            ci_ref[...] = jnp.zeros_like(ci_ref)
            dar_ref[...] = jnp.zeros_like(dar_ref)
            dai_ref[...] = jnp.zeros_like(dai_ref)

        for sg in range(nsg):
            dys = dy_ref[:, sg * cs:(sg + 1) * cs]
            lr_ref[:, sg, :] = _dot(dys, ccr[sg], NT)
            li_ref[:, sg, :] = -_dot(dys, cci[sg], NT)
        ar, ai = ar_ref[...], ai_ref[...]

        def step(i, carry):
            cr, ci, dar, dai = carry
            j = tb - 1 - i
            sr, si = sr_ref[j], si_ref[j]
            dar = dar + (cr * sr + ci * si)
            dai = dai + (ci * sr - cr * si)
            nr = lr_ref[j] + (ar * cr + ai * ci)
            ni = li_ref[j] + (ar * ci - ai * cr)
            lr_ref[j] = nr
            li_ref[j] = ni
            return nr, ni, dar, dai

        cr, ci, dar, dai = lax.fori_loop(0, tb, step, (cr_ref[...], ci_ref[...], dar_ref[...], dai_ref[...]))
        cr_ref[...], ci_ref[...] = cr, ci
        dar_ref[...], dai_ref[...] = dar, dai
        lrm_ref[...] = jnp.swapaxes(lr_ref[...], 0, 1).astype(MXU_DTYPE)
        lim_ref[...] = jnp.swapaxes(li_ref[...], 0, 1).astype(MXU_DTYPE)

    hbm = pl.BlockSpec(memory_space=pltpu.HBM)
    full = pl.BlockSpec((nsg, ns), lambda i: (0, 0))
    mxu = jax.ShapeDtypeStruct((nsg, t, ns), MXU_DTYPE)
    acc = jax.ShapeDtypeStruct((nsg, ns), F32)
    scan_spec = pl.BlockSpec((tb, nsg, ns), lambda i: (nb - 1 - i, 0, 0))
    rows = pl.BlockSpec((tb, d), lambda i: (nb - 1 - i, 0))
    return _call(
        body, name=name, out_shape=[jax.ShapeDtypeStruct((t, d), MXU_DTYPE), mxu, mxu, acc, acc], grid=(nb,),
        in_specs=[rows, rows, hbm, hbm, full, full, scan_spec, scan_spec],
        out_specs=[rows] + [pl.BlockSpec((nsg, tb, ns), lambda i: (0, nb - 1 - i, 0))] * 2 + [full, full],
        scratch_shapes=[pltpu.VMEM(cc_re.shape, cc_re.dtype), pltpu.VMEM(cc_im.shape, cc_im.dtype),
                        pltpu.VMEM((tb, nsg, ns), F32), pltpu.VMEM((tb, nsg, ns), F32),
                        pltpu.VMEM((nsg, ns), F32), pltpu.VMEM((nsg, ns), F32)],
        args=(dyact, ypre, cc_re, cc_im, ab_re, ab_im, s_re, s_im), rider=rider)


def _s5_grads(name, lam_re, lam_im, s_re, s_im, u, dyp, bb_re, bb_im, dskip, rider=None):
    nsg, t, ns = lam_re.shape
    d = u.shape[1]
    cs = bb_re.shape[1]
    tb = _tile(t, 512, SUBLANE)

    def body(lr_ref, li_ref, sr_ref, si_ref, u_ref, dy_ref, bbr_ref, bbi_ref, d_ref,
             du_ref, dbbr_ref, dbbi_ref, dccr_ref, dcci_ref, dd_ref):
        @pl.when(pl.program_id(1) == 0)
        def _():
            for r in (dbbr_ref, dbbi_ref, dccr_ref, dcci_ref, dd_ref):
                r[...] = jnp.zeros_like(r)

        lr, li, uv, dy = lr_ref[...], li_ref[...], u_ref[...], dy_ref[...]
        dyf = dy.astype(F32)
        du_ref[...] = _dot(lr, bbr_ref[...], NT) + _dot(li, bbi_ref[...], NT) + d_ref[...] * dyf
        dbbr_ref[...] += _dot(uv, lr, TN)
        dbbi_ref[...] += _dot(uv, li, TN)
        dccr_ref[...] += _dot(sr_ref[...], dy, TN)
        dcci_ref[...] -= _dot(si_ref[...], dy, TN)
        dd_ref[...] += jnp.sum(dyf * uv, axis=0, keepdims=True)

    s_spec = pl.BlockSpec((None, tb, ns), lambda sg, i: (sg, i, 0))
    col = pl.BlockSpec((tb, cs), lambda sg, i: (i, sg))
    b_spec = pl.BlockSpec((None, cs, ns), lambda sg, i: (sg, 0, 0))
    c_spec = pl.BlockSpec((None, ns, cs), lambda sg, i: (sg, 0, 0))
    vec = pl.BlockSpec((1, cs), lambda sg, i: (0, sg))
    return _call(
        body, name=name,
        out_shape=[jax.ShapeDtypeStruct((t, d), F32), jax.ShapeDtypeStruct(bb_re.shape, F32),
                   jax.ShapeDtypeStruct(bb_re.shape, F32), jax.ShapeDtypeStruct((nsg, ns, cs), F32),
                   jax.ShapeDtypeStruct((nsg, ns, cs), F32), jax.ShapeDtypeStruct((1, d), F32)],
        grid=(nsg, t // tb), in_specs=[s_spec, s_spec, s_spec, s_spec, col, col, b_spec, b_spec, vec],
        out_specs=[col, b_spec, b_spec, c_spec, c_spec, vec],
        args=(lam_re, lam_im, s_re, s_im, u, dyp, bb_re, bb_im, dskip), rider=rider)


def _shift_down(x, k, prev8):
    if k == 0:
        return x
    ext = jnp.concatenate([prev8, x], axis=0)
    return ext[SUBLANE - k:SUBLANE - k + x.shape[0]]


def _shift_up(x, k, next8):
    if k == 0:
        return x
    ext = jnp.concatenate([x, next8], axis=0)
    return ext[k:k + x.shape[0]]


def _lru_time_block(t):
    return _tile(t, 256, SUBLANE)


def _lru_gates(xp, prev8, cv_ref, wrg, wig):
    taps = cv_ref.shape[0] - 4
    row = lambda k: cv_ref[k:k + 1, :]
    xs = [_shift_down(xp, taps - 1 - k, prev8) for k in range(taps)]
    xb = row(taps)
    for k in range(taps):
        xb = xb + row(k) * xs[k]
    r = jax.nn.sigmoid(_dot(xb, wrg, NN) + row(taps + 1))
    ig = jax.nn.sigmoid(_dot(xb, wig, NN) + row(taps + 2))
    sp = jax.nn.softplus(-row(taps + 3))
    log_a = -LRU_C * r * sp
    a = jnp.exp(log_a)
    mult = jnp.sqrt(_neg_expm1(2.0 * log_a))
    return xs, xb, r, ig, sp, a, mult


def _lru_fwd(name, zz, cvec, wrg, wig, rider=None):
    _, t, w = zz.shape
    half = N_DEV // 2
    tb = _lru_time_block(t)

    def body(gb_ref, xp_ref, xprev_ref, cv_ref, wrg_ref, wig_ref, hs_ref, y_ref, a_scr, b_scr, carry):
        i = pl.program_id(1)

        @pl.when(i == 0)
        def _():
            carry[...] = jnp.zeros_like(carry)

        prev8 = jnp.where(i > 0, xprev_ref[...], 0.0)
        _, xb, _, ig, _, a, mult = _lru_gates(xp_ref[...], prev8, cv_ref, wrg_ref[...], wig_ref[...])
        a_scr[...] = a
        b_scr[...] = mult * (ig * xb)

        def step(j, h):
            h = a_scr[pl.ds(j, 1), :] * h + b_scr[pl.ds(j, 1), :]
            hs_ref[pl.ds(j, 1), :] = h
            return h

        carry[0:1, :] = lax.fori_loop(0, tb, step, carry[0:1, :], unroll=8)
        y_ref[...] = (hs_ref[...] * _gelu(gb_ref[...])).astype(y_ref.dtype)

    nrow = tb // SUBLANE
    blk = lambda off: pl.BlockSpec((None, tb, w), lambda q, i: (q + off, i, 0))
    return _call(
        body, name=name,
        out_shape=[jax.ShapeDtypeStruct((half, t, w), F32), jax.ShapeDtypeStruct((half, t, w), MXU_DTYPE)],
        grid=(half, t // tb),
        in_specs=[blk(0), blk(half),
                  pl.BlockSpec((None, SUBLANE, w), lambda q, i: (q + half, jnp.maximum(i * nrow - 1, 0), 0)),
                  pl.BlockSpec((None,) + cvec.shape[1:], lambda q, i: (q, 0, 0)),
                  pl.BlockSpec((None, w, w), lambda q, i: (q, 0, 0)), pl.BlockSpec((None, w, w), lambda q, i: (q, 0, 0))],
        out_specs=[blk(0), blk(0)],
        scratch_shapes=[pltpu.VMEM((tb, w), F32), pltpu.VMEM((tb, w), F32), pltpu.VMEM((SUBLANE, w), F32)],
        args=(zz, zz, zz, cvec, wrg, wig), rider=rider)


def _lru_bwd(name, zz, hs, dy, cvec, wrg, wig, rider=None):
    _, t, w = zz.shape
    half = N_DEV // 2
    tb = _lru_time_block(t)
    nb = t // tb
    taps = cvec.shape[1] - 4

    def body(gb_ref, xp_ref, xprev_ref, hs_ref, hprev_ref, dy_ref, cv_ref, wrg_ref, wig_ref,
             dgb_ref, dxp_ref, dcv_ref, dwrg_ref, dwig_ref, a_scr, l_scr, carry, dxb_next):
        i = pl.program_id(1)

        @pl.when(i == 0)
        def _():
            for r_ in (carry, dxb_next, dcv_ref, dwrg_ref, dwig_ref):
                r_[...] = jnp.zeros_like(r_)

        has_prev = i < nb - 1
        row = lambda k: cv_ref[k:k + 1, :]
        prev8 = jnp.where(has_prev, xprev_ref[...], 0.0)
        xs, xb, r, ig, sp, a, mult = _lru_gates(xp_ref[...], prev8, cv_ref, wrg_ref[...], wig_ref[...])
        hs_ = hs_ref[...]
        hs_m1 = _shift_down(hs_, 1, jnp.where(has_prev, hprev_ref[...], 0.0))
        gel, dgel = _gelu_and_grad(gb_ref[...])
        dy_ = dy_ref[...]
        dgb_ref[...] = (dy_ * hs_ * dgel).astype(dgb_ref.dtype)
        a_scr[...] = a
        l_scr[...] = dy_ * gel

        def step(k, c):
            j = tb - 1 - k
            lam = l_scr[pl.ds(j, 1), :] + c
            l_scr[pl.ds(j, 1), :] = lam
            return a_scr[pl.ds(j, 1), :] * lam

        carry[0:1, :] = lax.fori_loop(0, tb, step, carry[0:1, :], unroll=8)
        lam = l_scr[...]
        dmult = lam * (ig * xb)
        dig = lam * (mult * xb)
        dxb = lam * (mult * ig)
        dlog_a = (lam * hs_m1) * a - dmult * (a * a) / mult
        dr = dlog_a * (-LRU_C * sp)
        dsp = jnp.sum(dlog_a * (-LRU_C * r), axis=0, keepdims=True)
        dpr = dr * (r * (1.0 - r))
        dpi = dig * (ig * (1.0 - ig))
        dwrg_ref[...] += _dot(xb, dpr, TN)
        dwig_ref[...] += _dot(xb, dpi, TN)
        dxb = dxb + _dot(dpr, wrg_ref[...], NT) + _dot(dpi, wig_ref[...], NT)
        for k in range(taps):
            dcv_ref[k:k + 1, :] += jnp.sum(dxb * xs[k], axis=0, keepdims=True)
        dcv_ref[taps:taps + 1, :] += jnp.sum(dxb, axis=0, keepdims=True)
        dcv_ref[taps + 1:taps + 2, :] += jnp.sum(dpr, axis=0, keepdims=True)
        dcv_ref[taps + 2:taps + 3, :] += jnp.sum(dpi, axis=0, keepdims=True)
        dcv_ref[taps + 3:taps + 4, :] += dsp * (-jax.nn.sigmoid(-row(taps + 3)))
        nxt8 = dxb_next[...]
        dxp = row(taps - 1) * dxb
        for k in range(taps - 1):
            dxp = dxp + row(k) * _shift_up(dxb, taps - 1 - k, nxt8)
        dxp_ref[...] = dxp.astype(dxp_ref.dtype)
        dxb_next[...] = dxb[0:SUBLANE]

    nrow = tb // SUBLANE
    blk = lambda off: pl.BlockSpec((None, tb, w), lambda q, i: (q + off, nb - 1 - i, 0))
    halo = lambda off: pl.BlockSpec((None, SUBLANE, w), lambda q, i: (q + off, jnp.maximum((nb - 1 - i) * nrow - 1, 0), 0))
    wspec = pl.BlockSpec((None, w, w), lambda q, i: (q, 0, 0))
    cspec = pl.BlockSpec((None,) + cvec.shape[1:], lambda q, i: (q, 0, 0))
    act = jax.ShapeDtypeStruct((half, t, w), MXU_DTYPE)
    return _call(
        body, name=name,
        out_shape=[act, act, jax.ShapeDtypeStruct(cvec.shape, F32), jax.ShapeDtypeStruct(wrg.shape, F32),
                   jax.ShapeDtypeStruct(wig.shape, F32)],
        grid=(half, nb),
        in_specs=[blk(0), blk(half), halo(half), blk(0), halo(0), blk(0), cspec, wspec, wspec],
        out_specs=[blk(0), blk(0), cspec, wspec, wspec],
        scratch_shapes=[pltpu.VMEM((tb, w), F32), pltpu.VMEM((tb, w), F32), pltpu.VMEM((SUBLANE, w), F32),
                        pltpu.VMEM((SUBLANE, w), F32)],
        args=(zz, zz, zz, hs, hs, dy, cvec, wrg, wig), rider=rider)


def _band(name, blocks, per, dtype):
    n, a, b = blocks.shape

    def body(x_ref, o_ref):
        o_ref[...] = jnp.zeros_like(o_ref)
        for g in range(per):
            o_ref[g * a:(g + 1) * a, g * b:(g + 1) * b] = x_ref[g].astype(o_ref.dtype)

    return pl.pallas_call(
        body, name=name, out_shape=jax.ShapeDtypeStruct((n // per, per * a, per * b), dtype), grid=(n // per,),
        in_specs=[pl.BlockSpec((per, a, b), lambda s: (s, 0, 0))],
        out_specs=pl.BlockSpec((None, per * a, per * b), lambda s: (s, 0, 0)), compiler_params=_params(1))(blocks)


def _unband(name, bands, per):
    s, pa, pb = bands.shape
    a, b = pa // per, pb // per

    def body(x_ref, o_ref):
        for g in range(per):
            o_ref[g] = x_ref[g * a:(g + 1) * a, g * b:(g + 1) * b]

    return pl.pallas_call(
        body, name=name, out_shape=jax.ShapeDtypeStruct((s * per, a, b), bands.dtype), grid=(s,),
        in_specs=[pl.BlockSpec((None, pa, pb), lambda i: (i, 0, 0))],
        out_specs=pl.BlockSpec((per, a, b), lambda i: (i, 0, 0)), compiler_params=_params(1))(bands)


def _pack(arrays, rows_multiple, lanes=LANE):
    flat = [a.reshape(-1).astype(F32) for a in arrays]
    size = sum(a.shape[0] for a in flat)
    rows = -(-size // (lanes * rows_multiple)) * rows_multiple
    if rows * lanes > size:
        flat.append(jnp.zeros((rows * lanes - size,), F32))
    return jnp.concatenate(flat).reshape(rows, lanes)


def _unpack(packed, shapes):
    flat = packed.reshape(-1)
    out, off = [], 0
    for s in shapes:
        n = math.prod(s)
        out.append(flat[off:off + n].reshape(s))
        off += n
    return out


def kernel(x, c, norm_g, w_ada, b_ada, s5_w_in, s5_lam_re, s5_lam_im, s5_log_dt, s5_b_re, s5_b_im, s5_c_re, s5_c_im, s5_d, s5_w_glu, lru_w_in, lru_conv_w, lru_conv_b, lru_w_rg, lru_b_rg, lru_w_ig, lru_b_ig, lru_lam, lru_w_out, ffn_w_gu, ffn_w_down, final_g, loss_target, m_norm_g, m_w_ada, m_b_ada, m_s5_w_in, m_s5_lam_re, m_s5_lam_im, m_s5_log_dt, m_s5_b_re, m_s5_b_im, m_s5_c_re, m_s5_c_im, m_s5_d, m_s5_w_glu, m_lru_w_in, m_lru_conv_w, m_lru_conv_b, m_lru_w_rg, m_lru_b_rg, m_lru_w_ig, m_lru_b_ig, m_lru_lam, m_lru_w_out, m_ffn_w_gu, m_ffn_w_down, m_final_g, v_norm_g, v_w_ada, v_b_ada, v_s5_w_in, v_s5_lam_re, v_s5_lam_im, v_s5_log_dt, v_s5_b_re, v_s5_b_im, v_s5_c_re, v_s5_c_im, v_s5_d, v_s5_w_glu, v_lru_w_in, v_lru_conv_w, v_lru_conv_b, v_lru_w_rg, v_lru_b_rg, v_lru_w_ig, v_lru_b_ig, v_lru_lam, v_lru_w_out, v_ffn_w_gu, v_ffn_w_down, v_final_g):
    wv = dict(zip(WEIGHTS, (norm_g, w_ada, b_ada, s5_w_in, s5_lam_re, s5_lam_im, s5_log_dt, s5_b_re, s5_b_im, s5_c_re, s5_c_im, s5_d, s5_w_glu, lru_w_in, lru_conv_w, lru_conv_b, lru_w_rg, lru_b_rg, lru_w_ig, lru_b_ig, lru_lam, lru_w_out, ffn_w_gu, ffn_w_down, final_g)))
    mv = dict(zip(WEIGHTS, (m_norm_g, m_w_ada, m_b_ada, m_s5_w_in, m_s5_lam_re, m_s5_lam_im, m_s5_log_dt, m_s5_b_re, m_s5_b_im, m_s5_c_re, m_s5_c_im, m_s5_d, m_s5_w_glu, m_lru_w_in, m_lru_conv_w, m_lru_conv_b, m_lru_w_rg, m_lru_b_rg, m_lru_w_ig, m_lru_b_ig, m_lru_lam, m_lru_w_out, m_ffn_w_gu, m_ffn_w_down, m_final_g)))
    vv = dict(zip(WEIGHTS, (v_norm_g, v_w_ada, v_b_ada, v_s5_w_in, v_s5_lam_re, v_s5_lam_im, v_s5_log_dt, v_s5_b_re, v_s5_b_im, v_s5_c_re, v_s5_c_im, v_s5_d, v_s5_w_glu, v_lru_w_in, v_lru_conv_w, v_lru_conv_b, v_lru_w_rg, v_lru_b_rg, v_lru_w_ig, v_lru_b_ig, v_lru_lam, v_lru_w_out, v_ffn_w_gu, v_ffn_w_down, v_final_g)))

    me = 4 * lax.axis_index("x") + 2 * lax.axis_index("y") + lax.axis_index("c")
    x0 = x[0]
    tgt = loss_target[0]
    t, d = x0.shape
    depth = norm_g.shape[0]
    n_mod = w_ada.shape[2] * N_DEV // d
    groups, states = s5_lam_re.shape[1], s5_lam_re.shape[2]
    per_sg = S5_SUPER // S5_GROUP
    nsg = groups // per_sg
    lw = lru_lam.shape[1] * N_DEV
    lwc = lw // (N_DEV // 2)
    half = N_DEV // 2

    assert depth == 2, "the ride schedule below is written for one S5 layer followed by one RG-LRU layer"
    wire = lambda a: a.astype(WIRE_DTYPE)
    gw = {'s5_in': _all_gather("ag_s5_w_in", wire(s5_w_in[0]))}

    def riding(job, fn, *args):
        res, (got,) = fn(*args, rider=_gather_rider([wire(job[1])]))
        gw[job[0]] = got
        return res

    sh_shapes = [wv[n].shape for n in SMALL_SHARDED] + [c.shape]
    sh_all = _all_gather("ag_small", _pack([wv[n] for n in SMALL_SHARDED] + [c], SUBLANE))
    sh_parts = [jnp.stack(p) for p in zip(*[_unpack(sh_all[s], sh_shapes) for s in range(N_DEV)])]
    full = {}
    for n, p in zip(SMALL_SHARDED, sh_parts[:-1]):
        full[n] = jnp.moveaxis(p, 0, -2).reshape(p.shape[1:-1] + (-1,))
    c_all = sh_parts[-1].reshape(N_DEV, d)
    c16 = jnp.pad(c_all, ((0, 2 * SUBLANE - N_DEV), (0, 0)))

    n_loc = w_ada.shape[2]
    b_loc = lax.dynamic_slice_in_dim(b_ada, me * n_loc, n_loc, axis=1)[:, None, :]
    mod_part = _ada_fwd("ada_fwd", c16, w_ada, b_loc)[:, :N_DEV]
    mod_mine = _chunk_exchange("x_mod", [mod_part.transpose(1, 0, 2)], ALL)
    mod = mod_mine.transpose(1, 0, 2).reshape(depth, n_mod, 1, d)

    lam3 = lambda a: a[0][:, None, :]
    p_lr, p_li, p_ld = lam3(s5_lam_re), lam3(s5_lam_im), s5_log_dt[0][:, None, None]
    p_br, p_bi = s5_b_re[0].transpose(0, 2, 1), s5_b_im[0].transpose(0, 2, 1)
    ab_re3, ab_im3, bb_re3, bb_im3 = _s5_disc("s5_disc", p_lr, p_li, p_ld, p_br, p_bi)
    ab_re, ab_im = ab_re3.reshape(nsg, per_sg * states), ab_im3.reshape(nsg, per_sg * states)
    bb_re = _band("band_bb_re", bb_re3, per_sg, MXU_DTYPE)
    bb_im = _band("band_bb_im", bb_im3, per_sg, MXU_DTYPE)
    cc_re = _band("band_cc_re", s5_c_re[0].transpose(0, 2, 1), per_sg, MXU_DTYPE)
    cc_im = _band("band_cc_im", s5_c_im[0].transpose(0, 2, 1), per_sg, MXU_DTYPE)

    taps = lru_conv_w.shape[1]
    cvec = jnp.concatenate([full['lru_conv_w'].reshape(taps, lw), full['lru_conv_b'], full['lru_b_rg'],
                            full['lru_b_ig'], full['lru_lam']], axis=0)
    cvec = cvec.reshape(taps + 4, half, lwc).transpose(1, 0, 2)
    wrg = _band("band_w_rg", lru_w_rg[0], LRU_BLOCKS_PER_CHUNK, MXU_DTYPE)
    wig = _band("band_w_ig", lru_w_ig[0], LRU_BLOCKS_PER_CHUNK, MXU_DTYPE)

    saved = []
    xc = x0
    for i in range(depth):
        sh1, sc1, g1, sh2, sc2, g2 = [mod[i, k] for k in range(n_mod)]
        gn = full['norm_g'][i]
        h1 = _norm_mod_fwd(f"norm1_fwd{i}", xc, gn[0:1], sc1, sh1)
        if i % 2 == 0:
            u = riding(('s5_glu', s5_w_glu[0]), _mm_row, f"s5_in{i}", h1[None], gw['s5_in'].reshape(d, d))
            s_re, s_im, s_rem, s_imm, ypre, yact = riding((('gu', i), ffn_w_gu[i]), _s5_scan_fwd, f"s5_scan{i}", u, bb_re,
                                                          bb_im, ab_re, ab_im, cc_re, cc_im, s5_d)
            z = riding((('down', i), ffn_w_down[i]), _mm_col, f"s5_glu{i}", yact, gw['s5_glu'])
            x1 = _glu_resid_fwd(f"s5_resid{i}", z, xc, g1)
            mix = (u, s_re, s_im, s_rem, s_imm, ypre, yact, z)
        else:
            zz = _mm_col(f"lru_in{i}", h1, gw['lru_in'])
            hs, ylru = riding((('gu', i), ffn_w_gu[i]), _lru_fwd, f"lru_core{i}", zz, cvec, wrg, wig)
            o = _mm_row(f"lru_out{i}", ylru, gw['lru_out'].reshape(lw, d))
            x1 = _resid(f"lru_resid{i}", xc, o, g1)
            mix = (zz, hs, ylru, o)
        h2 = _norm_mod_fwd(f"norm2_fwd{i}", x1, gn[1:2], sc2, sh2)
        if i % 2 == 0:
            gu = riding(('lru_in', lru_w_in[0]), _mm_col, f"ffn_gu{i}", h2, gw['gu', i], MXU_DTYPE)
            act = _swiglu_act_fwd(f"ffn_act{i}", gu)
            f = riding(('lru_out', lru_w_out[0]), _mm_row, f"ffn_down{i}", act, gw['down', i].reshape(-1, d))
        else:
            gu = riding((('down', i), ffn_w_down[i]), _mm_col, f"ffn_gu{i}", h2, gw['gu', i], MXU_DTYPE)
            act = _swiglu_act_fwd(f"ffn_act{i}", gu)
            f = _mm_row(f"ffn_down{i}", act, gw['down', i].reshape(-1, d))
        x2 = _resid(f"ffn_resid{i}", x1, f, g2)
        saved.append((xc, h1, mix, x1, h2, gu, act, f))
        xc = x2

    dx, loss_part, d_final_g = _loss_bwd("loss", xc, tgt, final_g[None])
    loss = lax.psum(loss_part[0, 0], ("x", "y", "c"))

    grads = {}
    parts = {}
    dmod = [None] * depth
    d_norm_g = [None] * depth
    core = lax.axis_index("c").astype(jnp.int32).reshape(1)
    chunked = lambda p: p.reshape(N_DEV, -1, p.shape[-1])
    to_sibling = lambda p: _sibling_rider(chunked(p))
    pair = lambda name, p, got: _pair_sum(name, chunked(p), got, core)
    over_ici = lambda sums: _chunk_rider([sums], SAME_CORE)
    half_over_ici = lambda sums, h: _chunk_rider([sums], SAME_CORE, rows=(h * (sums.shape[1] // 2), sums.shape[1] // 2))

    above = None
    for i in reversed(range(depth)):
        xin, h1, mix, x1, h2, gu, act, f = saved[i]
        sh1, sc1, g1, sh2, sc2, g2 = [mod[i, k] for k in range(n_mod)]
        gn = full['norm_g'][i]
        g_down = gw['down', i].reshape(-1, d)
        df, dg2 = _gate_bwd(f"ffn_gate_bwd{i}", dx, f, g2)
        if above is None:
            dact = _mm_row_da(f"ffn_down_da{i}", df, g_down, half)
        else:
            dact, (got,) = _ride(_mm_row_da, f"ffn_down_da{i}", df, g_down, half, riders=[to_sibling(above[1])])
            s_above = pair(f"x_{above[0][0]}_pair", above[1], got)
        p_down = _mm_row_db(f"ffn_down_db{i}", act, df, WIRE_DTYPE)
        dgu, (got,) = _ride(_swiglu_act_bwd, f"ffn_act_bwd{i}", gu, dact, riders=[to_sibling(p_down)])
        s_down = pair(f"x_ffn_w_down{i}_pair", p_down, got)
        if above is None:
            dh2, (parts['ffn_w_down', i],) = _ride(_mm_col_da, f"ffn_gu_da{i}", dgu, gw['gu', i], riders=[over_ici(s_down)])
            p_gu = _mm_col_db(f"ffn_gu_db{i}", h2, dgu, WIRE_DTYPE)
        else:
            dh2, (parts[above[0]],) = _ride(_mm_col_da, f"ffn_gu_da{i}", dgu, gw['gu', i], riders=[over_ici(s_above)])
            p_gu, (parts['ffn_w_down', i],) = _ride(_mm_col_db, f"ffn_gu_db{i}", h2, dgu, WIRE_DTYPE,
                                                    riders=[over_ici(s_down)])
        dx, dgn2, dsc2, dsh2 = _norm_mod_bwd(f"norm2_bwd{i}", x1, dh2, dx, gn[1:2], sc2)
        if i % 2 == 0:
            u, s_re, s_im, s_rem, s_imm, ypre, yact, z = mix
            dz, dg1 = _glu_resid_bwd(f"s5_resid_bwd{i}", z, dx, g1)
            dyact, (got,) = _ride(_mm_col_da, f"s5_glu_da{i}", dz, gw['s5_glu'], riders=[to_sibling(p_gu)])
            s_gu = pair(f"x_ffn_w_gu{i}_pair", p_gu, got)
            p_glu, (gu_lo,) = _ride(_mm_col_db, f"s5_glu_db{i}", yact, dz, WIRE_DTYPE, riders=[half_over_ici(s_gu, 0)])
            (dyp, l_rem, l_imm, dab_re, dab_im), (gu_hi, got) = _ride(
                _s5_scan_bwd, f"s5_scan_bwd{i}", dyact, ypre, cc_re, cc_im, ab_re, ab_im, s_re, s_im,
                riders=[half_over_ici(s_gu, 1), to_sibling(p_glu)])
            parts['ffn_w_gu', i] = [gu_lo, gu_hi]
            s_glu = pair("x_s5_w_glu_pair", p_glu, got)
            (du, dbb_re, dbb_im, dcc_re, dcc_im, dd), (parts['s5_w_glu', 0],) = _ride(
                _s5_grads, f"s5_grads{i}", l_rem, l_imm, s_rem, s_imm, u, dyp, bb_re, bb_im, s5_d, riders=[over_ici(s_glu)])
            dlr, dli, dld, dbr, dbi = _s5_disc_bwd(
                "s5_disc_bwd", p_lr, p_li, p_ld, p_br, p_bi, dab_re.reshape(groups, 1, states),
                dab_im.reshape(groups, 1, states), _unband("unband_bb_re", dbb_re, per_sg),
                _unband("unband_bb_im", dbb_im, per_sg))
            grads['s5_lam_re'], grads['s5_lam_im'], grads['s5_log_dt'] = dlr[:, 0][None], dli[:, 0][None], dld[:, 0, 0][None]
            grads['s5_b_re'], grads['s5_b_im'] = dbr.transpose(0, 2, 1)[None], dbi.transpose(0, 2, 1)[None]
            grads['s5_c_re'] = _unband("unband_cc_re", dcc_re, per_sg).transpose(0, 2, 1)[None]
            grads['s5_c_im'] = _unband("unband_cc_im", dcc_im, per_sg).transpose(0, 2, 1)[None]
            grads['s5_d'] = dd
            dub = du.astype(MXU_DTYPE)
            p_s5_in = _mm_row_db(f"s5_in_db{i}", h1[None], dub, WIRE_DTYPE)
            dh1, (got,) = _ride(_mm_row_da, f"s5_in_da{i}", dub, gw['s5_in'].reshape(d, d), 1, riders=[to_sibling(p_s5_in)])
            dh1 = dh1[0]
            s_s5_in = pair("x_s5_w_in_pair", p_s5_in, got)
        else:
            zz, hs, ylru, o = mix
            g_lru_out = gw['lru_out'].reshape(lw, d)
            do, dg1 = _gate_bwd(f"lru_gate_bwd{i}", dx, o, g1)
            dyl, (got,) = _ride(_mm_row_da, f"lru_out_da{i}", do, g_lru_out, half, riders=[to_sibling(p_gu)])
            s_gu = pair(f"x_ffn_w_gu{i}_pair", p_gu, got)
            p_lru_out = _mm_row_db(f"lru_out_db{i}", ylru, do, WIRE_DTYPE)
            (dgb, dxp, dcv, dwrg, dwig), (gu_lo, gu_hi, got) = _ride(
                _lru_bwd, f"lru_core_bwd{i}", zz, hs, dyl, cvec, wrg, wig,
                riders=[half_over_ici(s_gu, 0), half_over_ici(s_gu, 1), to_sibling(p_lru_out)])
            parts['ffn_w_gu', i] = [gu_lo, gu_hi]
            s_lru_out = pair("x_lru_w_out_pair", p_lru_out, got)
            dzz = jnp.concatenate([dgb, dxp], axis=0)
            dh1, (parts['lru_w_out', 0],) = _ride(_mm_col_da, f"lru_in_da{i}", dzz, gw['lru_in'],
                                                  riders=[over_ici(s_lru_out)])
            above = (('lru_w_in', 0), _mm_col_db(f"lru_in_db{i}", h1, dzz, WIRE_DTYPE))
            dcv = dcv.transpose(1, 0, 2).reshape(taps + 4, lw)
            grads['lru_conv_w'] = dcv[:taps].reshape(1, taps, 1, lw)
            grads['lru_conv_b'], grads['lru_b_rg'] = dcv[taps:taps + 1], dcv[taps + 1:taps + 2]
            grads['lru_b_ig'], grads['lru_lam'] = dcv[taps + 2:taps + 3], dcv[taps + 3:taps + 4]
            grads['lru_w_rg'] = _unband("unband_w_rg", dwrg, LRU_BLOCKS_PER_CHUNK)[None]
            grads['lru_w_ig'] = _unband("unband_w_ig", dwig, LRU_BLOCKS_PER_CHUNK)[None]
        dx, dgn1, dsc1, dsh1 = _norm_mod_bwd(f"norm1_bwd{i}", xin, dh1, dx, gn[0:1], sc1)
        dmod[i] = jnp.concatenate([dsh1, dsc1, dg1, dsh2, dsc2, dg2], axis=1)
        d_norm_g[i] = jnp.concatenate([dgn1, dgn2], axis=0)
    grad_x = dx[None]
    dmod = jnp.concatenate(dmod, axis=0)
    grads['norm_g'] = jnp.stack(d_norm_g)
    grads['b_ada'] = dmod
    grads['final_g'] = d_final_g[0]

    small_partial = _pack([grads[n] for n in SMALL], SUBLANE * N_DEV)
    rows8 = small_partial.shape[0] // N_DEV
    small_partial = small_partial.reshape(N_DEV, rows8, LANE)
    s_small = pair("x_small_pair", small_partial, _ride_alone("x_small_d2d", _sibling_rider(small_partial)))
    parts['s5_w_in', 0], small_parts = _ride_alone("x_tail_ici", _join([over_ici(s_s5_in), over_ici(s_small)]))

    out = {}
    dmod_all = _all_gather("ag_dmod", dmod)
    dmod_loc = lax.dynamic_slice_in_dim(dmod_all, me * n_loc, n_loc, axis=2).transpose(1, 0, 2)
    dmod16 = jnp.pad(dmod_loc, ((0, 0), (0, 2 * SUBLANE - N_DEV), (0, 0)))
    out['w_ada'] = _adamw_w_ada("adamw_w_ada", c16, dmod16, w_ada, m_w_ada, v_w_ada)

    for name in BIG[1:]:
        w = wv[name]
        rows, cols = w.shape[-2] * w.shape[0], w.shape[-1]
        flat = lambda a: a.reshape(rows, cols)
        pieces = []
        for l in range(w.shape[0]):
            pieces += parts[name, l] if isinstance(parts[name, l], list) else [parts[name, l]]
        res = _adamw_sum("adamw_" + name, pieces, flat(w), flat(mv[name]), flat(vv[name]))
        out[name] = [r.reshape(w.shape) for r in res]

    summed = _sum_parts("sum_small", small_parts)
    small_total = _all_gather("ag_small_sum", summed).reshape(-1, LANE)
    small_grad = dict(zip(SMALL, _unpack(small_total, [grads[n].shape for n in SMALL])))
    for n in SMALL_SHARDED:
        shard = wv[n].shape[-1]
        small_grad[n] = lax.dynamic_slice_in_dim(small_grad[n], me * shard, shard, axis=small_grad[n].ndim - 1)
    for n in SMALL:
        w = wv[n]
        flat = lambda a: a.reshape(-1, w.shape[-1])
        res = _adamw_sum("adamw_" + n, [flat(small_grad[n])[None]], flat(w), flat(mv[n]), flat(vv[n]))
        out[n] = [r.reshape(w.shape) for r in res]

    return (loss, grad_x, *[out[n][0] for n in WEIGHTS], *[out[n][1] for n in WEIGHTS],
            *[out[n][2] for n in WEIGHTS], *[out[n][3] for n in WEIGHTS])
```

```python
import functools
import math

import jax
import jax.numpy as jnp
from jax import lax
from jax.experimental import pallas as pl
from jax.experimental.pallas import tpu as pltpu

F32 = jnp.float32
MXU_DTYPE = jnp.bfloat16
WIRE_DTYPE = jnp.bfloat16
N_DEV = 8
EPS = 1e-6
LRU_C = 8.0
S5_GROUP = 16
S5_STATE = 64
S5_SUPER = 256
LRU_BLOCKS_PER_CHUNK = 4
ADAM_LR, ADAM_B1, ADAM_B2, ADAM_EPS, ADAM_WD, ADAM_STEP = 0.001, 0.9, 0.999, 1e-08, 0.01, 10
VMEM_LIMIT_BYTES = 56 * 1024 * 1024
LANE = 128
SUBLANE = 8

WEIGHTS = ['norm_g', 'w_ada', 'b_ada', 's5_w_in', 's5_lam_re', 's5_lam_im', 's5_log_dt', 's5_b_re', 's5_b_im',
           's5_c_re', 's5_c_im', 's5_d', 's5_w_glu', 'lru_w_in', 'lru_conv_w', 'lru_conv_b', 'lru_w_rg', 'lru_b_rg',
           'lru_w_ig', 'lru_b_ig', 'lru_lam', 'lru_w_out', 'ffn_w_gu', 'ffn_w_down', 'final_g']
BIG = ('w_ada', 's5_w_in', 's5_w_glu', 'lru_w_in', 'lru_w_out', 'ffn_w_gu', 'ffn_w_down')
SMALL = tuple(n for n in WEIGHTS if n not in BIG)
SMALL_SHARDED = ('norm_g', 'lru_conv_w', 'lru_conv_b', 'lru_b_rg', 'lru_b_ig', 'lru_lam')

NN = (((1,), (0,)), ((), ()))
NT = (((1,), (1,)), ((), ()))
TN = (((0,), (0,)), ((), ()))


def _params(n_grid):
    return pltpu.CompilerParams(dimension_semantics=("arbitrary",) * n_grid, vmem_limit_bytes=VMEM_LIMIT_BYTES)


def _tile(dim, pref, align=LANE):
    if dim <= pref:
        return dim
    t = (pref // align) * align
    while t >= align:
        if dim % t == 0:
            return t
        t -= align
    return dim


def _dot(a, b, dims):
    return lax.dot_general(a.astype(MXU_DTYPE), b.astype(MXU_DTYPE), dims, preferred_element_type=F32)


def _gelu(x):
    k = math.sqrt(2.0 / math.pi)
    return 0.5 * x * (1.0 + jnp.tanh(k * (x + 0.044715 * (x * x * x))))


def _gelu_and_grad(x):
    k = math.sqrt(2.0 / math.pi)
    th = jnp.tanh(k * (x + 0.044715 * (x * x * x)))
    g = 0.5 * x * (1.0 + th)
    dg = 0.5 * (1.0 + th) + 0.5 * x * (1.0 - th * th) * (k * (1.0 + 3.0 * 0.044715 * (x * x)))
    return g, dg


def _neg_expm1(x):
    series = -x * (1.0 + x * (0.5 + x * (1.0 / 6.0 + x * (1.0 / 24.0 + x * (1.0 / 120.0)))))
    return jnp.where(x > -0.01, series, 1.0 - jnp.exp(x))


MESH = pl.DeviceIdType.MESH
N_CHIP = N_DEV // 2
ALL, SAME_CORE = 7, 6


def _place():
    x, y, c = lax.axis_index("x"), lax.axis_index("y"), lax.axis_index("c")
    return x, y, c


def _flip(place, k):
    x, y, c = place
    return (1 - x if (k >> 2) & 1 else x, 1 - y if (k >> 1) & 1 else y, 1 - c if k & 1 else c)


def _chunk_exchange(name, xs, group):
    return _ride_alone(name, _chunk_rider(xs, group))


class _Rider:
    def __init__(self, arrays, out_shape, scratch, start, finish, post, mid=None):
        self.arrays, self.out_shape, self.scratch = list(arrays), list(out_shape), list(scratch)
        self.start, self.finish, self.post, self.mid = start, finish, post, mid


def _chunk_rider(xs, group, rows=None):
    n = len(xs)
    members, r_all, c_ = xs[0].shape
    r0, r = (0, r_all) if rows is None else rows
    assert members == {ALL: N_DEV, SAME_CORE: N_CHIP}[group]
    assert all(a.shape == xs[0].shape and a.dtype == xs[0].dtype for a in xs)
    ks = [k for k in range(1, N_DEV) if not k & ~group]
    member = (lambda p: 4 * p[0] + 2 * p[1] + p[2]) if group == ALL else (lambda p: 2 * p[0] + p[1])

    def copies(ins, outs, scratch):
        out = outs[0]
        send_sems, recv_sems, local_sems = scratch
        place = _place()
        me = member(place)
        src = lambda l, who: ins[l].at[who] if rows is None else ins[l].at[who, pl.ds(r0, r)]
        local = [pltpu.make_async_copy(src(l, me), out.at[me, l], local_sems.at[l]) for l in range(n)]
        remote = []
        for l in range(n):
            for k in ks:
                pid = _flip(place, k)
                peer = member(pid)

                def copy(land_at, l=l, k=k, peer=peer, pid=pid):
                    return pltpu.make_async_remote_copy(
                        src_ref=src(l, peer), dst_ref=out.at[land_at, l], send_sem=send_sems.at[l * N_DEV + k],
                        recv_sem=recv_sems.at[l * N_DEV + k], device_id=pid, device_id_type=MESH)

                remote.append((copy, me, peer))
        return local, remote

    def start(ins, outs, scratch):
        local, remote = copies(ins, outs, scratch)
        for cp in local:
            cp.start()
        for copy, me, _ in remote:
            copy(me).start()

    def finish(ins, outs, scratch):
        local, remote = copies(ins, outs, scratch)
        for copy, me, peer in remote:
            copy(me).wait_send()
            copy(peer).wait_recv()
        for cp in local:
            cp.wait()

    return _Rider(
        xs, [jax.ShapeDtypeStruct((members, n, r, c_), xs[0].dtype)],
        [pltpu.SemaphoreType.DMA((n * N_DEV,)), pltpu.SemaphoreType.DMA((n * N_DEV,)), pltpu.SemaphoreType.DMA((n,))],
        start, finish, lambda outs: outs[0].reshape(members, n * r, c_))


def _gather_rider(xs):
    n = len(xs)
    per = 7

    def plan(ins, outs, scratch):
        send_sems, recv_sems, local_sems = scratch
        x, y, c = place = _place()
        sibling, x_nb, y_nb, diag = (x, y, 1 - c), (1 - x, y, c), (x, 1 - y, c), (1 - x, 1 - y, c)
        relayed = (x + c * (1 - 2 * x), y + (1 - c) * (1 - 2 * y), c)
        onward = (x + (1 - c) * (1 - 2 * x), y + c * (1 - 2 * y), c)
        jobs = []
        for l in range(n):
            slot = lambda p, l=l: outs[l].at[2 * p[0] + p[1], p[2]]

            def copy(k, block, to, src=None, l=l, slot=slot):
                return pltpu.make_async_remote_copy(
                    src_ref=slot(block) if src is None else src, dst_ref=slot(block), send_sem=send_sems.at[l * per + k],
                    recv_sem=recv_sems.at[l * per + k], device_id=to, device_id_type=MESH)

            jobs.append(dict(
                mine=pltpu.make_async_copy(ins[l], slot(place), local_sems.at[l]),
                first=[copy(0, place, sibling, src=ins[l]), copy(1, place, x_nb, src=ins[l]), copy(2, place, y_nb, src=ins[l])],
                landed=[copy(1, x_nb, place), copy(2, y_nb, place)],
                second=[copy(3, relayed, onward), copy(4, x_nb, sibling), copy(5, y_nb, sibling)],
                relay_landed=copy(3, diag, place), last=copy(6, diag, sibling),
                from_sibling=[copy(0, sibling, place)] + [copy(4 + j, (p[0], p[1], 1 - c), place)
                                                          for j, p in enumerate((x_nb, y_nb, diag))]))
        return jobs

    def start(ins, outs, scratch):
        for job in plan(ins, outs, scratch):
            job['mine'].start()
            for cp in job['first']:
                cp.start()

    def mid(ins, outs, scratch):
        for job in plan(ins, outs, scratch):
            for cp in job['landed']:
                cp.wait_recv()
            for cp in job['second']:
                cp.start()

    def finish(ins, outs, scratch):
        jobs = plan(ins, outs, scratch)
        for job in jobs:
            job['relay_landed'].wait_recv()
            job['last'].start()
        for job in jobs:
            for cp in job['from_sibling']:
                cp.wait_recv()
            for cp in job['first'] + job['second'] + [job['last']]:
                cp.wait_send()
            job['mine'].wait()

    return _Rider(
        xs, [jax.ShapeDtypeStruct((N_CHIP, 2) + x.shape, x.dtype) for x in xs],
        [pltpu.SemaphoreType.DMA((n * per,)), pltpu.SemaphoreType.DMA((n * per,)), pltpu.SemaphoreType.DMA((n,))],
        start, finish, lambda outs: [o.reshape((N_DEV,) + x.shape) for o, x in zip(outs, xs)], mid=mid)


HBM_SPEC = pl.BlockSpec(memory_space=pltpu.HBM)


def _ride_alone(name, rider):
    n_in, n_out = len(rider.arrays), len(rider.out_shape)

    def body(*refs):
        parts = refs[:n_in], refs[n_in:n_in + n_out], refs[n_in + n_out:]
        rider.start(*parts)
        if rider.mid is not None:
            rider.mid(*parts)
        rider.finish(*parts)

    outs = pl.pallas_call(body, name=name, out_shape=rider.out_shape, in_specs=[HBM_SPEC] * n_in,
                          out_specs=[HBM_SPEC] * n_out, scratch_shapes=rider.scratch)(*rider.arrays)
    return rider.post(list(outs))


def _call(body, *, name, grid, in_specs, out_specs, out_shape, scratch_shapes=(), args, rider=None):
    single = not isinstance(out_shape, (list, tuple))
    out_shape = [out_shape] if single else list(out_shape)
    out_specs = [out_specs] if single else list(out_specs)
    scratch_shapes = list(scratch_shapes)
    unwrap = lambda outs: outs[0] if single else list(outs)
    if rider is None:
        outs = pl.pallas_call(body, name=name, grid=grid, in_specs=list(in_specs), out_specs=out_specs, out_shape=out_shape,
                              scratch_shapes=scratch_shapes, compiler_params=_params(len(grid)))(*args)
        return unwrap(outs)
    n_in, n_out, n_scr = len(in_specs), len(out_shape), len(scratch_shapes)
    r_in, r_out = len(rider.arrays), len(rider.out_shape)

    def carried(*refs):
        ins, refs = refs[:n_in], refs[n_in:]
        r_ins, refs = refs[:r_in], refs[r_in:]
        outs, refs = refs[:n_out], refs[n_out:]
        r_outs, refs = refs[:r_out], refs[r_out:]
        scr, r_scr = refs[:n_scr], refs[n_scr:]
        step = 0
        for ax, g in enumerate(grid):
            step = step * g + pl.program_id(ax)

        @pl.when(step == 0)
        def _():
            rider.start(r_ins, r_outs, r_scr)

        body(*ins, *outs, *scr)

        if rider.mid is not None and total > 1:
            @pl.when(step == (5 * (total - 1)) // 8)
            def _():
                rider.mid(r_ins, r_outs, r_scr)

        @pl.when(step == total - 1)
        def _():
            if rider.mid is not None and total == 1:
                rider.mid(r_ins, r_outs, r_scr)
            rider.finish(r_ins, r_outs, r_scr)

    total = math.prod(grid)

    outs = pl.pallas_call(
        carried, name=name, grid=grid, in_specs=list(in_specs) + [HBM_SPEC] * r_in, out_specs=out_specs + [HBM_SPEC] * r_out,
        out_shape=out_shape + rider.out_shape, scratch_shapes=scratch_shapes + rider.scratch,
        compiler_params=_params(len(grid)))(*args, *rider.arrays)
    return unwrap(outs[:n_out]), rider.post(list(outs[n_out:]))


def _all_gather(name, x):
    return _ride_alone(name, _gather_rider([x]))[0]


def _sibling_rider(x):
    _, r, c_ = x.shape

    def copies(ins, outs, scratch):
        send_sems, recv_sems = scratch
        place = _place()
        return [pltpu.make_async_remote_copy(
            src_ref=ins[0].at[2 * chip + (1 - place[2])], dst_ref=outs[0].at[chip], send_sem=send_sems.at[chip],
            recv_sem=recv_sems.at[chip], device_id=_flip(place, 1), device_id_type=MESH) for chip in range(N_CHIP)]

    def start(ins, outs, scratch):
        for cp in copies(ins, outs, scratch):
            cp.start()

    def finish(ins, outs, scratch):
        for cp in copies(ins, outs, scratch):
            cp.wait()

    return _Rider([x], [jax.ShapeDtypeStruct((N_CHIP, r, c_), x.dtype)],
                  [pltpu.SemaphoreType.DMA((N_CHIP,)), pltpu.SemaphoreType.DMA((N_CHIP,))], start, finish, lambda outs: outs[0])


def _join(riders):
    def cut(seq, counts):
        out, off = [], 0
        for k in counts:
            out.append(seq[off:off + k])
            off += k
        return out

    def parts(ins, outs, scratch):
        return zip(riders, cut(ins, [len(r.arrays) for r in riders]), cut(outs, [len(r.out_shape) for r in riders]),
                   cut(scratch, [len(r.scratch) for r in riders]))

    def start(ins, outs, scratch):
        for r, i, o, s in parts(ins, outs, scratch):
            r.start(i, o, s)

    def finish(ins, outs, scratch):
        for r, i, o, s in parts(ins, outs, scratch):
            r.finish(i, o, s)

    def mid(ins, outs, scratch):
        for r, i, o, s in parts(ins, outs, scratch):
            if r.mid is not None:
                r.mid(i, o, s)

    return _Rider(
        [a for r in riders for a in r.arrays], [o for r in riders for o in r.out_shape], [s for r in riders for s in r.scratch],
        start, finish, lambda outs: [r.post(o) for r, o in zip(riders, cut(outs, [len(r.out_shape) for r in riders]))],
        mid=mid if any(r.mid is not None for r in riders) else None)


def _ride(fn, *args, riders):
    return fn(*args, rider=_join(riders))


def _pair_sum(name, x, got, core):
    _, r, c_ = x.shape
    br = _tile(r, max(256, (1 << 21) // c_), 2 * SUBLANE)

    def body(core_ref, x_ref, g_ref, o_ref):
        o_ref[...] = (x_ref[...].astype(F32) + g_ref[...].astype(F32)).astype(o_ref.dtype)

    return pl.pallas_call(
        body, name=name, out_shape=jax.ShapeDtypeStruct((N_CHIP, r, c_), x.dtype),
        grid_spec=pltpu.PrefetchScalarGridSpec(
            num_scalar_prefetch=1, grid=(N_CHIP, r // br),
            in_specs=[pl.BlockSpec((None, br, c_), lambda ch, i, core_ref: (2 * ch + core_ref[0], i, 0)),
                      pl.BlockSpec((None, br, c_), lambda ch, i, core_ref: (ch, i, 0))],
            out_specs=pl.BlockSpec((None, br, c_), lambda ch, i, core_ref: (ch, i, 0))),
        compiler_params=_params(2))(core, x, got)


def _mm(name, a, b, out_shape, out_dtype, grid, a_spec, b_spec, o_spec, dims, n_red, acc_shape, rider=None):
    red = tuple(range(len(grid) - n_red, len(grid)))
    out_type = jax.ShapeDtypeStruct(out_shape, out_dtype)
    if all(grid[ax] == 1 for ax in red):
        def single(a_ref, b_ref, o_ref):
            o_ref[...] = _dot(a_ref[...], b_ref[...], dims).astype(o_ref.dtype)

        return _call(single, name=name, out_shape=out_type, grid=grid, in_specs=[a_spec, b_spec], out_specs=o_spec,
                     args=(a, b), rider=rider)

    def body(a_ref, b_ref, o_ref, acc_ref):
        first = functools.reduce(jnp.logical_and, [pl.program_id(ax) == 0 for ax in red])
        last = functools.reduce(jnp.logical_and, [pl.program_id(ax) == grid[ax] - 1 for ax in red])

        @pl.when(first)
        def _():
            acc_ref[...] = jnp.zeros_like(acc_ref)

        acc_ref[...] += _dot(a_ref[...], b_ref[...], dims)

        @pl.when(last)
        def _():
            o_ref[...] = acc_ref[...].astype(o_ref.dtype)

    return _call(body, name=name, out_shape=out_type, grid=grid, in_specs=[a_spec, b_spec], out_specs=o_spec,
                 scratch_shapes=[pltpu.VMEM(acc_shape, F32)], args=(a, b), rider=rider)


def _mm_col(name, a, b, out_dtype=F32, rider=None):
    m, k = a.shape
    j, _, n = b.shape
    bm, bk = _tile(m, 1024), _tile(k, 2048)
    return _mm(name, a, b, (j, m, n), out_dtype, (j, m // bm, k // bk),
               pl.BlockSpec((bm, bk), lambda jj, mm, kk: (mm, kk)),
               pl.BlockSpec((None, bk, n), lambda jj, mm, kk: (jj, kk, 0)),
               pl.BlockSpec((None, bm, n), lambda jj, mm, kk: (jj, mm, 0)), NN, 1, (bm, n), rider)


def _mm_col_da(name, do, b, rider=None):
    j, m, n = do.shape
    k = b.shape[1]
    bm, bk = _tile(m, 1024), _tile(k, 1024)
    return _mm(name, do, b, (m, k), F32, (m // bm, k // bk, j),
               pl.BlockSpec((None, bm, n), lambda mm, kk, jj: (jj, mm, 0)),
               pl.BlockSpec((None, bk, n), lambda mm, kk, jj: (jj, kk, 0)),
               pl.BlockSpec((bm, bk), lambda mm, kk, jj: (mm, kk)), NT, 1, (bm, bk), rider)


def _mm_col_db(name, a, do, out_dtype, rider=None):
    m, k = a.shape
    j, _, n = do.shape
    bm, bk = _tile(m, 2048), _tile(k, 512)
    return _mm(name, a, do, (j, k, n), out_dtype, (j, k // bk, m // bm),
               pl.BlockSpec((bm, bk), lambda jj, kk, mm: (mm, kk)),
               pl.BlockSpec((None, bm, n), lambda jj, kk, mm: (jj, mm, 0)),
               pl.BlockSpec((None, bk, n), lambda jj, kk, mm: (jj, kk, 0)), TN, 1, (bk, n), rider)


def _row_bk(kq):
    return kq if (kq % LANE or kq // LANE in (11,)) else _tile(kq, 2048)


def _mm_row(name, a, b, out_dtype=F32, rider=None):
    q, m, kq = a.shape
    n = b.shape[1]
    bm, bn, bk = _tile(m, 1024), _tile(n, 1024), _row_bk(kq)
    nk = kq // bk
    return _mm(name, a, b, (m, n), out_dtype, (m // bm, n // bn, q, nk),
               pl.BlockSpec((None, bm, bk), lambda mm, nn, qq, kk: (qq, mm, kk)),
               pl.BlockSpec((bk, bn), lambda mm, nn, qq, kk: (qq * nk + kk, nn)),
               pl.BlockSpec((bm, bn), lambda mm, nn, qq, kk: (mm, nn)), NN, 2, (bm, bn), rider)


def _mm_row_da(name, do, b, q, rider=None):
    m, n = do.shape
    kq = b.shape[0] // q
    bm, bn = _tile(m, 1024), _tile(n, 2048)
    return _mm(name, do, b, (q, m, kq), F32, (q, m // bm, n // bn),
               pl.BlockSpec((bm, bn), lambda qq, mm, nn: (mm, nn)),
               pl.BlockSpec((kq, bn), lambda qq, mm, nn: (qq, nn)),
               pl.BlockSpec((None, bm, kq), lambda qq, mm, nn: (qq, mm, 0)), NT, 1, (bm, kq), rider)


def _mm_row_db(name, a, do, out_dtype):
    q, m, kq = a.shape
    n = do.shape[1]
    bm, bn = _tile(m, 2048), _tile(n, 512)
    return _mm(name, a, do, (q * kq, n), out_dtype, (q, n // bn, m // bm),
               pl.BlockSpec((None, bm, kq), lambda qq, nn, mm: (qq, mm, 0)),
               pl.BlockSpec((bm, bn), lambda qq, nn, mm: (mm, nn)),
               pl.BlockSpec((kq, bn), lambda qq, nn, mm: (qq, nn)), TN, 1, (kq, bn))


def _row_spec(bm, d):
    return pl.BlockSpec((bm, d), lambda i: (i, 0))


def _vec_spec(d):
    return pl.BlockSpec((1, d), lambda i: (0, 0))


def _norm_mod_fwd(name, x, gain, sc, sh):
    t, d = x.shape
    bm = _tile(t, 256, SUBLANE)

    def body(x_ref, g_ref, sc_ref, sh_ref, h_ref):
        xv = x_ref[...]
        rstd = lax.rsqrt(jnp.mean(xv * xv, axis=-1, keepdims=True) + EPS)
        h_ref[...] = ((xv * rstd) * g_ref[...] * (1.0 + sc_ref[...]) + sh_ref[...]).astype(h_ref.dtype)

    return pl.pallas_call(
        body, name=name, out_shape=jax.ShapeDtypeStruct((t, d), MXU_DTYPE), grid=(t // bm,),
        in_specs=[_row_spec(bm, d), _vec_spec(d), _vec_spec(d), _vec_spec(d)], out_specs=_row_spec(bm, d),
        compiler_params=_params(1))(x, gain, sc, sh)


def _norm_mod_bwd(name, x, dh, dres, gain, sc, rider=None):
    t, d = x.shape
    bm = _tile(t, 256, SUBLANE)

    def body(x_ref, dh_ref, dres_ref, g_ref, sc_ref, dx_ref, dg_ref, dsc_ref, dsh_ref):
        @pl.when(pl.program_id(0) == 0)
        def _():
            dg_ref[...] = jnp.zeros_like(dg_ref)
            dsc_ref[...] = jnp.zeros_like(dsc_ref)
            dsh_ref[...] = jnp.zeros_like(dsh_ref)

        xv, dh_ = x_ref[...], dh_ref[...]
        rstd = lax.rsqrt(jnp.mean(xv * xv, axis=-1, keepdims=True) + EPS)
        nrm = xv * rstd
        gain_ = g_ref[...]
        dsh_ref[...] += jnp.sum(dh_, axis=0, keepdims=True)
        dsc_ref[...] += jnp.sum(dh_ * (nrm * gain_), axis=0, keepdims=True)
        dhn = dh_ * (1.0 + sc_ref[...])
        dg_ref[...] += jnp.sum(dhn * nrm, axis=0, keepdims=True)
        dn = dhn * gain_
        dx_ref[...] = dres_ref[...] + rstd * (dn - nrm * jnp.mean(dn * nrm, axis=-1, keepdims=True))

    vec = jax.ShapeDtypeStruct((1, d), F32)
    return _call(
        body, name=name, out_shape=[jax.ShapeDtypeStruct((t, d), F32), vec, vec, vec], grid=(t // bm,),
        in_specs=[_row_spec(bm, d), _row_spec(bm, d), _row_spec(bm, d), _vec_spec(d), _vec_spec(d)],
        out_specs=[_row_spec(bm, d), _vec_spec(d), _vec_spec(d), _vec_spec(d)],
        args=(x, dh, dres, gain, sc), rider=rider)


def _loss_bwd(name, x, target, gain):
    t, d = x.shape
    bm = _tile(t, 256, SUBLANE)

    def body(x_ref, t_ref, g_ref, dx_ref, loss_ref, dg_ref):
        @pl.when(pl.program_id(0) == 0)
        def _():
            loss_ref[...] = jnp.zeros_like(loss_ref)
            dg_ref[...] = jnp.zeros_like(dg_ref)

        xv = x_ref[...]
        rstd = lax.rsqrt(jnp.mean(xv * xv, axis=-1, keepdims=True) + EPS)
        nrm = xv * rstd
        gain_ = g_ref[...]
        err = nrm * gain_ - t_ref[...]
        per_tok = jnp.mean(err * err, axis=-1, keepdims=True)
        loss_ref[...] += 0.5 * jnp.sum(per_tok, axis=0, keepdims=True)
        dout = err * (1.0 / d)
        dg_ref[...] += jnp.sum(dout * nrm, axis=0, keepdims=True)
        dn = dout * gain_
        dx_ref[...] = rstd * (dn - nrm * jnp.mean(dn * nrm, axis=-1, keepdims=True))

    return pl.pallas_call(
        body, name=name,
        out_shape=[jax.ShapeDtypeStruct((t, d), F32), jax.ShapeDtypeStruct((1, 1), F32),
                   jax.ShapeDtypeStruct((1, d), F32)],
        grid=(t // bm,), in_specs=[_row_spec(bm, d), _row_spec(bm, d), _vec_spec(d)],
        out_specs=[_row_spec(bm, d), pl.BlockSpec((1, 1), lambda i: (0, 0)), _vec_spec(d)],
        compiler_params=_params(1))(x, target, gain)


def _resid(name, x, y, g):
    t, d = x.shape
    bm = _tile(t, 256, SUBLANE)

    def body(x_ref, y_ref, g_ref, o_ref):
        o_ref[...] = x_ref[...] + g_ref[...] * y_ref[...]

    return pl.pallas_call(
        body, name=name, out_shape=jax.ShapeDtypeStruct((t, d), F32), grid=(t // bm,),
        in_specs=[_row_spec(bm, d), _row_spec(bm, d), _vec_spec(d)], out_specs=_row_spec(bm, d),
        compiler_params=_params(1))(x, y, g)


def _gate_bwd(name, dx, y, g):
    t, d = dx.shape
    bm = _tile(t, 256, SUBLANE)

    def body(dx_ref, y_ref, g_ref, dy_ref, dg_ref):
        @pl.when(pl.program_id(0) == 0)
        def _():
            dg_ref[...] = jnp.zeros_like(dg_ref)

        dxv = dx_ref[...]
        dy_ref[...] = (g_ref[...] * dxv).astype(dy_ref.dtype)
        dg_ref[...] += jnp.sum(dxv * y_ref[...], axis=0, keepdims=True)

    return pl.pallas_call(
        body, name=name, out_shape=[jax.ShapeDtypeStruct((t, d), MXU_DTYPE), jax.ShapeDtypeStruct((1, d), F32)],
        grid=(t // bm,), in_specs=[_row_spec(bm, d), _row_spec(bm, d), _vec_spec(d)],
        out_specs=[_row_spec(bm, d), _vec_spec(d)], compiler_params=_params(1))(dx, y, g)


def _glu_resid_fwd(name, z, x, g):
    _, t, n = z.shape
    d = x.shape[1]
    half = N_DEV // 2
    bm = _tile(t, 256, SUBLANE)

    def body(v_ref, gt_ref, x_ref, g_ref, o_ref):
        o_ref[...] = x_ref[...] + g_ref[...] * (v_ref[...] * jax.nn.sigmoid(gt_ref[...]))

    return pl.pallas_call(
        body, name=name, out_shape=jax.ShapeDtypeStruct((t, d), F32), grid=(half, t // bm),
        in_specs=[pl.BlockSpec((None, bm, n), lambda q, i: (q, i, 0)),
                  pl.BlockSpec((None, bm, n), lambda q, i: (q + half, i, 0)),
                  pl.BlockSpec((bm, n), lambda q, i: (i, q)), pl.BlockSpec((1, n), lambda q, i: (0, q))],
        out_specs=pl.BlockSpec((bm, n), lambda q, i: (i, q)), compiler_params=_params(2))(z, z, x, g)


def _glu_resid_bwd(name, z, dx, g):
    _, t, n = z.shape
    d = dx.shape[1]
    half = N_DEV // 2
    bm = _tile(t, 256, SUBLANE)

    def body(z_ref, dx_ref, g_ref, dz_ref, dg_ref):
        @pl.when(pl.program_id(1) == 0)
        def _():
            dg_ref[...] = jnp.zeros_like(dg_ref)

        v, dxv = z_ref[0], dx_ref[...]
        sig = jax.nn.sigmoid(z_ref[1])
        dout = g_ref[...] * dxv
        dg_ref[...] += jnp.sum(dxv * (v * sig), axis=0, keepdims=True)
        dz_ref[0] = (dout * sig).astype(dz_ref.dtype)
        dz_ref[1] = (dout * v * (sig * (1.0 - sig))).astype(dz_ref.dtype)

    pair = pl.BlockSpec((2, None, bm, n), lambda q, i: (0, q, i, 0))
    dz, dg = pl.pallas_call(
        body, name=name,
        out_shape=[jax.ShapeDtypeStruct((2, half, t, n), MXU_DTYPE), jax.ShapeDtypeStruct((1, d), F32)],
        grid=(half, t // bm),
        in_specs=[pair, pl.BlockSpec((bm, n), lambda q, i: (i, q)), pl.BlockSpec((1, n), lambda q, i: (0, q))],
        out_specs=[pair, pl.BlockSpec((1, n), lambda q, i: (0, q))],
        compiler_params=_params(2))(z.reshape(2, half, t, n), dx, g)
    return dz.reshape(N_DEV, t, n), dg


def _swiglu_act_fwd(name, gu):
    _, t, n = gu.shape
    half = N_DEV // 2
    bm = _tile(t, 256, SUBLANE)

    def body(g_ref, u_ref, o_ref):
        gv = g_ref[...].astype(F32)
        o_ref[...] = (gv * jax.nn.sigmoid(gv) * u_ref[...].astype(F32)).astype(o_ref.dtype)

    return pl.pallas_call(
        body, name=name, out_shape=jax.ShapeDtypeStruct((half, t, n), MXU_DTYPE), grid=(half, t // bm),
        in_specs=[pl.BlockSpec((None, bm, n), lambda q, i: (q, i, 0)),
                  pl.BlockSpec((None, bm, n), lambda q, i: (q + half, i, 0))],
        out_specs=pl.BlockSpec((None, bm, n), lambda q, i: (q, i, 0)), compiler_params=_params(2))(gu, gu)


def _swiglu_act_bwd(name, gu, dact, rider=None):
    _, t, n = gu.shape
    half = N_DEV // 2
    bm = _tile(t, 256, SUBLANE)

    def body(gu_ref, da_ref, o_ref):
        gv, da = gu_ref[0].astype(F32), da_ref[...]
        sig = jax.nn.sigmoid(gv)
        o_ref[0] = (da * gu_ref[1].astype(F32) * (sig * (1.0 + gv * (1.0 - sig)))).astype(o_ref.dtype)
        o_ref[1] = (da * (gv * sig)).astype(o_ref.dtype)

    pair = pl.BlockSpec((2, None, bm, n), lambda q, i: (0, q, i, 0))
    res = _call(
        body, name=name, out_shape=jax.ShapeDtypeStruct((2, half, t, n), MXU_DTYPE), grid=(half, t // bm),
        in_specs=[pair, pl.BlockSpec((None, bm, n), lambda q, i: (q, i, 0))], out_specs=pair,
        args=(gu.reshape(2, half, t, n), dact), rider=rider)
    if rider is None:
        return res.reshape(N_DEV, t, n)
    return res[0].reshape(N_DEV, t, n), res[1]


def _ada_fwd(name, c16, w_ada, b_loc):
    nl, d, n = w_ada.shape
    bn = _tile(n, 512)

    def body(c_ref, w_ref, b_ref, o_ref):
        cv = c_ref[...]
        o_ref[...] = _dot(cv * jax.nn.sigmoid(cv), w_ref[...], NN) + b_ref[...]

    return pl.pallas_call(
        body, name=name, out_shape=jax.ShapeDtypeStruct((nl, c16.shape[0], n), F32), grid=(nl, n // bn),
        in_specs=[pl.BlockSpec(c16.shape, lambda i, j: (0, 0)), pl.BlockSpec((None, d, bn), lambda i, j: (i, 0, j)),
                  pl.BlockSpec((None, 1, bn), lambda i, j: (i, 0, j))],
        out_specs=pl.BlockSpec((None, c16.shape[0], bn), lambda i, j: (i, 0, j)),
        compiler_params=_params(2))(c16, w_ada, b_loc)


def _adam_update(g, w, m, v):
    m = ADAM_B1 * m + (1.0 - ADAM_B1) * g
    v = ADAM_B2 * v + (1.0 - ADAM_B2) * (g * g)
    m_hat = m / (1.0 - ADAM_B1 ** ADAM_STEP)
    v_hat = v / (1.0 - ADAM_B2 ** ADAM_STEP)
    delta = -ADAM_LR * (m_hat / (jnp.sqrt(v_hat) + ADAM_EPS) + ADAM_WD * w)
    return delta, m, v


def _adamw_w_ada(name, c16, dmod16, w, m, v, rider=None):
    nl, d, n = w.shape
    br = _tile(d, 256)

    def body(c_ref, dm_ref, w_ref, m_ref, v_ref, g_ref, dl_ref, mo_ref, vo_ref):
        cv = c_ref[...]
        g = _dot(cv * jax.nn.sigmoid(cv), dm_ref[...], TN)
        g_ref[...] = g
        dl_ref[...], mo_ref[...], vo_ref[...] = _adam_update(g, w_ref[...], m_ref[...], v_ref[...])

    blk = pl.BlockSpec((None, br, n), lambda i, r: (i, r, 0))
    shp = jax.ShapeDtypeStruct(w.shape, F32)
    return _call(
        body, name=name, out_shape=[shp] * 4, grid=(nl, d // br),
        in_specs=[pl.BlockSpec((c16.shape[0], br), lambda i, r: (0, r)),
                  pl.BlockSpec((None, dmod16.shape[1], n), lambda i, r: (i, 0, 0)), blk, blk, blk],
        out_specs=[blk] * 4, args=(c16, dmod16, w, m, v), rider=rider)


def _adamw_sum(name, parts, w, m, v, rider=None):
    nl = len(parts)
    p, r, c = parts[0].shape
    br = _tile(r, max(128, (1 << 17) // max(c, LANE)), 2 * SUBLANE)
    nb = r // br

    def body(*refs):
        p_refs, (w_ref, m_ref, v_ref, g_ref, dl_ref, mo_ref, vo_ref) = refs[:nl], refs[nl:]
        layer = pl.program_id(0)
        g = None
        for l, p_ref in enumerate(p_refs):
            gl = p_ref[0].astype(F32)
            for s in range(1, p):
                gl = gl + p_ref[s].astype(F32)
            g = gl if g is None else jnp.where(layer == l, gl, g)
        g_ref[...] = g
        dl_ref[...], mo_ref[...], vo_ref[...] = _adam_update(g, w_ref[...], m_ref[...], v_ref[...])

    blk = pl.BlockSpec((br, c), lambda l, i: (l * nb + i, 0))
    shp = jax.ShapeDtypeStruct((nl * r, c), F32)
    return _call(
        body, name=name, out_shape=[shp] * 4, grid=(nl, nb),
        in_specs=[pl.BlockSpec((p, br, c), lambda l, i, k=k: (0, jnp.where(l == k, i, 0), 0)) for k in range(nl)]
        + [blk, blk, blk], out_specs=[blk] * 4,
        args=(*parts, w, m, v), rider=rider)


def _sum_parts(name, parts):
    p, r, c = parts.shape

    def body(p_ref, o_ref):
        g = p_ref[0]
        for s in range(1, p):
            g = g + p_ref[s]
        o_ref[...] = g

    return pl.pallas_call(body, name=name, out_shape=jax.ShapeDtypeStruct((r, c), F32))(parts)


def _s5_disc(name, lam_re, lam_im, log_dt, b_re, b_im):
    def body(lr_ref, li_ref, ld_ref, br_ref, bi_ref, ar_ref, ai_ref, bbr_ref, bbi_ref):
        lr, li = lr_ref[...], li_ref[...]
        dt = jnp.exp(ld_ref[...])
        mag = jnp.exp(lr * dt)
        a_re, a_im = mag * jnp.cos(li * dt), mag * jnp.sin(li * dt)
        nr, ni = a_re - 1.0, a_im
        den = lr * lr + li * li
        f_re, f_im = (nr * lr + ni * li) / den, (ni * lr - nr * li) / den
        br, bi = br_ref[...], bi_ref[...]
        ar_ref[...], ai_ref[...] = a_re, a_im
        bbr_ref[...] = f_re * br - f_im * bi
        bbi_ref[...] = f_re * bi + f_im * br

    s_a, s_b = jax.ShapeDtypeStruct(lam_re.shape, F32), jax.ShapeDtypeStruct(b_re.shape, F32)
    return pl.pallas_call(body, name=name, out_shape=[s_a, s_a, s_b, s_b])(lam_re, lam_im, log_dt, b_re, b_im)


def _s5_disc_bwd(name, lam_re, lam_im, log_dt, b_re, b_im, dab_re, dab_im, dbb_re, dbb_im):
    def body(lr_ref, li_ref, ld_ref, br_ref, bi_ref, dar_ref, dai_ref, dbbr_ref, dbbi_ref,
             dlr_ref, dli_ref, dld_ref, dbr_ref, dbi_ref):
        lr, li = lr_ref[...], li_ref[...]
        dt = jnp.exp(ld_ref[...])
        mag = jnp.exp(lr * dt)
        a_re, a_im = mag * jnp.cos(li * dt), mag * jnp.sin(li * dt)
        nr, ni = a_re - 1.0, a_im
        den = lr * lr + li * li
        f_re, f_im = (nr * lr + ni * li) / den, (ni * lr - nr * li) / den
        br, bi = br_ref[...], bi_ref[...]
        dbbr, dbbi = dbbr_ref[...], dbbi_ref[...]
        dbr_ref[...] = f_re * dbbr + f_im * dbbi
        dbi_ref[...] = f_re * dbbi - f_im * dbbr
        df_re = jnp.sum(dbbr * br + dbbi * bi, axis=1, keepdims=True)
        df_im = jnp.sum(dbbi * br - dbbr * bi, axis=1, keepdims=True)
        dnr = (df_re * lr - df_im * li) / den
        dni = (df_re * li + df_im * lr) / den
        dden = -(df_re * f_re + df_im * f_im) / den
        dlr = (df_re * nr + df_im * ni) / den + 2.0 * lr * dden
        dli = (df_re * ni - df_im * nr) / den + 2.0 * li * dden
        da_re, da_im = dar_ref[...] + dnr, dai_ref[...] + dni
        dmag_mag = da_re * a_re + da_im * a_im
        dth = da_im * a_re - da_re * a_im
        dlr_ref[...] = dlr + dmag_mag * dt
        dli_ref[...] = dli + dth * dt
        ddt = jnp.sum(dmag_mag * lr + dth * li, axis=2, keepdims=True)
        dld_ref[...] = ddt * dt

    s_a, s_b = jax.ShapeDtypeStruct(lam_re.shape, F32), jax.ShapeDtypeStruct(b_re.shape, F32)
    return pl.pallas_call(
        body, name=name, out_shape=[s_a, s_a, jax.ShapeDtypeStruct(log_dt.shape, F32), s_b, s_b],
    )(lam_re, lam_im, log_dt, b_re, b_im, dab_re, dab_im, dbb_re, dbb_im)


def _s5_time_block(t):
    return _tile(t, 128, SUBLANE)


def _s5_scan_fwd(name, u, bb_re, bb_im, ab_re, ab_im, cc_re, cc_im, dskip, rider=None):
    t, d = u.shape
    nsg, cs, ns = bb_re.shape
    tb = _s5_time_block(t)

    def body(u_ref, bbr_hbm, bbi_hbm, ar_ref, ai_ref, ccr_hbm, cci_hbm, d_ref, sr_ref, si_ref, srm_ref, sim_ref, yp_ref,
             ya_ref, bbr, bbi, ccr, cci, cr_ref, ci_ref):
        @pl.when(pl.program_id(0) == 0)
        def _():
            pltpu.sync_copy(bbr_hbm, bbr)
            pltpu.sync_copy(bbi_hbm, bbi)
            pltpu.sync_copy(ccr_hbm, ccr)
            pltpu.sync_copy(cci_hbm, cci)
            cr_ref[...] = jnp.zeros_like(cr_ref)
            ci_ref[...] = jnp.zeros_like(ci_ref)

        for sg in range(nsg):
            us = u_ref[:, sg * cs:(sg + 1) * cs]
            sr_ref[:, sg, :] = _dot(us, bbr[sg], NN)
            si_ref[:, sg, :] = _dot(us, bbi[sg], NN)
        ar, ai = ar_ref[...], ai_ref[...]

        def step(i, carry):
            cr, ci = carry
            nr = ar * cr - ai * ci + sr_ref[i]
            ni = ar * ci + ai * cr + si_ref[i]
            sr_ref[i] = nr
            si_ref[i] = ni
            return nr, ni

        cr, ci = lax.fori_loop(0, tb, step, (cr_ref[...], ci_ref[...]), unroll=2)
        cr_ref[...], ci_ref[...] = cr, ci
        srm_ref[...] = jnp.swapaxes(sr_ref[...], 0, 1).astype(MXU_DTYPE)
        sim_ref[...] = jnp.swapaxes(si_ref[...], 0, 1).astype(MXU_DTYPE)
        for sg in range(nsg):
            cols = slice(sg * cs, (sg + 1) * cs)
            y = _dot(srm_ref[sg], ccr[sg], NN) - _dot(sim_ref[sg], cci[sg], NN) + d_ref[:, cols] * u_ref[:, cols]
            yp_ref[:, cols] = y
            ya_ref[:, cols] = _gelu(y).astype(ya_ref.dtype)

    scan = jax.ShapeDtypeStruct((t, nsg, ns), F32)
    mxu = jax.ShapeDtypeStruct((nsg, t, ns), MXU_DTYPE)
    hbm = pl.BlockSpec(memory_space=pltpu.HBM)
    full = pl.BlockSpec((nsg, ns), lambda i: (0, 0))
    return _call(
        body, name=name,
        out_shape=[scan, scan, mxu, mxu, jax.ShapeDtypeStruct((t, d), F32), jax.ShapeDtypeStruct((t, d), MXU_DTYPE)],
        grid=(t // tb,), in_specs=[_row_spec(tb, d), hbm, hbm, full, full, hbm, hbm, _vec_spec(d)],
        out_specs=[pl.BlockSpec((tb, nsg, ns), lambda i: (i, 0, 0))] * 2 + [pl.BlockSpec((nsg, tb, ns), lambda i: (0, i, 0))] * 2
        + [_row_spec(tb, d)] * 2,
        scratch_shapes=[pltpu.VMEM(bb_re.shape, bb_re.dtype), pltpu.VMEM(bb_im.shape, bb_im.dtype),
                        pltpu.VMEM(cc_re.shape, cc_re.dtype), pltpu.VMEM(cc_im.shape, cc_im.dtype),
                        pltpu.VMEM((nsg, ns), F32), pltpu.VMEM((nsg, ns), F32)],
        args=(u, bb_re, bb_im, ab_re, ab_im, cc_re, cc_im, dskip), rider=rider)


def _s5_scan_bwd(name, dyact, ypre, cc_re, cc_im, ab_re, ab_im, s_re, s_im, rider=None):
    t, d = dyact.shape
    nsg, ns, cs = cc_re.shape
    tb = _s5_time_block(t)
    nb = t // tb

    def body(dya_ref, yp_ref, ccr_hbm, cci_hbm, ar_ref, ai_ref, sr_ref, si_ref, dy_ref, lrm_ref, lim_ref, dar_ref, dai_ref,
             ccr, cci, lr_ref, li_ref, cr_ref, ci_ref):
        dy_ref[...] = (dya_ref[...] * _gelu_and_grad(yp_ref[...])[1]).astype(dy_ref.dtype)

        @pl.when(pl.program_id(0) == 0)
        def _():
            pltpu.sync_copy(ccr_hbm, ccr)
            pltpu.sync_copy(cci_hbm, cci)
            cr_ref[...] = jnp.zeros_like(cr_ref)
            ci_ref[...] = jnp.zeros_like(ci_ref)
            dar_ref[...] = jnp.zeros_like(dar_ref)
            dai_ref[...] = jnp.zeros_like(dai_ref)

        for sg in range(nsg):
            dys = dy_ref[:, sg * cs:(sg + 1) * cs]
            lr_ref[:, sg, :] = _dot(dys, ccr[sg], NT)
            li_ref[:, sg, :] = -_dot(dys, cci[sg], NT)
        ar, ai = ar_ref[...], ai_ref[...]

        def step(i, carry):
            cr, ci, dar, dai = carry
            j = tb - 1 - i
            sr, si = sr_ref[j], si_ref[j]
            dar = dar + (cr * sr + ci * si)
            dai = dai + (ci * sr - cr * si)
            nr = lr_ref[j] + (ar * cr + ai * ci)
            ni = li_ref[j] + (ar * ci - ai * cr)
            lr_ref[j] = nr
            li_ref[j] = ni
            return nr, ni, dar, dai

        cr, ci, dar, dai = lax.fori_loop(0, tb, step, (cr_ref[...], ci_ref[...], dar_ref[...], dai_ref[...]))
        cr_ref[...], ci_ref[...] = cr, ci
        dar_ref[...], dai_ref[...] = dar, dai
        lrm_ref[...] = jnp.swapaxes(lr_ref[...], 0, 1).astype(MXU_DTYPE)
        lim_ref[...] = jnp.swapaxes(li_ref[...], 0, 1).astype(MXU_DTYPE)

    hbm = pl.BlockSpec(memory_space=pltpu.HBM)
    full = pl.BlockSpec((nsg, ns), lambda i: (0, 0))
    mxu = jax.ShapeDtypeStruct((nsg, t, ns), MXU_DTYPE)
    acc = jax.ShapeDtypeStruct((nsg, ns), F32)
    scan_spec = pl.BlockSpec((tb, nsg, ns), lambda i: (nb - 1 - i, 0, 0))
    rows = pl.BlockSpec((tb, d), lambda i: (nb - 1 - i, 0))
    return _call(
        body, name=name, out_shape=[jax.ShapeDtypeStruct((t, d), MXU_DTYPE), mxu, mxu, acc, acc], grid=(nb,),
        in_specs=[rows, rows, hbm, hbm, full, full, scan_spec, scan_spec],
        out_specs=[rows] + [pl.BlockSpec((nsg, tb, ns), lambda i: (0, nb - 1 - i, 0))] * 2 + [full, full],
        scratch_shapes=[pltpu.VMEM(cc_re.shape, cc_re.dtype), pltpu.VMEM(cc_im.shape, cc_im.dtype),
                        pltpu.VMEM((tb, nsg, ns), F32), pltpu.VMEM((tb, nsg, ns), F32),
                        pltpu.VMEM((nsg, ns), F32), pltpu.VMEM((nsg, ns), F32)],
        args=(dyact, ypre, cc_re, cc_im, ab_re, ab_im, s_re, s_im), rider=rider)


def _s5_grads(name, lam_re, lam_im, s_re, s_im, u, dyp, bb_re, bb_im, dskip, rider=None):
    nsg, t, ns = lam_re.shape
    d = u.shape[1]
    cs = bb_re.shape[1]
    tb = _tile(t, 512, SUBLANE)

    def body(lr_ref, li_ref, sr_ref, si_ref, u_ref, dy_ref, bbr_ref, bbi_ref, d_ref,
             du_ref, dbbr_ref, dbbi_ref, dccr_ref, dcci_ref, dd_ref):
        @pl.when(pl.program_id(1) == 0)
        def _():
            for r in (dbbr_ref, dbbi_ref, dccr_ref, dcci_ref, dd_ref):
                r[...] = jnp.zeros_like(r)

        lr, li, uv, dy = lr_ref[...], li_ref[...], u_ref[...], dy_ref[...]
        dyf = dy.astype(F32)
        du_ref[...] = _dot(lr, bbr_ref[...], NT) + _dot(li, bbi_ref[...], NT) + d_ref[...] * dyf
        dbbr_ref[...] += _dot(uv, lr, TN)
        dbbi_ref[...] += _dot(uv, li, TN)
        dccr_ref[...] += _dot(sr_ref[...], dy, TN)
        dcci_ref[...] -= _dot(si_ref[...], dy, TN)
        dd_ref[...] += jnp.sum(dyf * uv, axis=0, keepdims=True)

    s_spec = pl.BlockSpec((None, tb, ns), lambda sg, i: (sg, i, 0))
    col = pl.BlockSpec((tb, cs), lambda sg, i: (i, sg))
    b_spec = pl.BlockSpec((None, cs, ns), lambda sg, i: (sg, 0, 0))
    c_spec = pl.BlockSpec((None, ns, cs), lambda sg, i: (sg, 0, 0))
    vec = pl.BlockSpec((1, cs), lambda sg, i: (0, sg))
    return _call(
        body, name=name,
        out_shape=[jax.ShapeDtypeStruct((t, d), F32), jax.ShapeDtypeStruct(bb_re.shape, F32),
                   jax.ShapeDtypeStruct(bb_re.shape, F32), jax.ShapeDtypeStruct((nsg, ns, cs), F32),
                   jax.ShapeDtypeStruct((nsg, ns, cs), F32), jax.ShapeDtypeStruct((1, d), F32)],
        grid=(nsg, t // tb), in_specs=[s_spec, s_spec, s_spec, s_spec, col, col, b_spec, b_spec, vec],
        out_specs=[col, b_spec, b_spec, c_spec, c_spec, vec],
        args=(lam_re, lam_im, s_re, s_im, u, dyp, bb_re, bb_im, dskip), rider=rider)


def _shift_down(x, k, prev8):
    if k == 0:
        return x
    ext = jnp.concatenate([prev8, x], axis=0)
    return ext[SUBLANE - k:SUBLANE - k + x.shape[0]]


def _shift_up(x, k, next8):
    if k == 0:
        return x
    ext = jnp.concatenate([x, next8], axis=0)
    return ext[k:k + x.shape[0]]


def _lru_time_block(t):
    return _tile(t, 256, SUBLANE)


def _lru_gates(xp, prev8, cv_ref, wrg, wig):
    taps = cv_ref.shape[0] - 4
    row = lambda k: cv_ref[k:k + 1, :]
    xs = [_shift_down(xp, taps - 1 - k, prev8) for k in range(taps)]
    xb = row(taps)
    for k in range(taps):
        xb = xb + row(k) * xs[k]
    r = jax.nn.sigmoid(_dot(xb, wrg, NN) + row(taps + 1))
    ig = jax.nn.sigmoid(_dot(xb, wig, NN) + row(taps + 2))
    sp = jax.nn.softplus(-row(taps + 3))
    log_a = -LRU_C * r * sp
    a = jnp.exp(log_a)
    mult = jnp.sqrt(_neg_expm1(2.0 * log_a))
    return xs, xb, r, ig, sp, a, mult


def _lru_fwd(name, zz, cvec, wrg, wig, rider=None):
    _, t, w = zz.shape
    half = N_DEV // 2
    tb = _lru_time_block(t)

    def body(gb_ref, xp_ref, xprev_ref, cv_ref, wrg_ref, wig_ref, hs_ref, y_ref, a_scr, b_scr, carry):
        i = pl.program_id(1)

        @pl.when(i == 0)
        def _():
            carry[...] = jnp.zeros_like(carry)

        prev8 = jnp.where(i > 0, xprev_ref[...], 0.0)
        _, xb, _, ig, _, a, mult = _lru_gates(xp_ref[...], prev8, cv_ref, wrg_ref[...], wig_ref[...])
        a_scr[...] = a
        b_scr[...] = mult * (ig * xb)

        def step(j, h):
            h = a_scr[pl.ds(j, 1), :] * h + b_scr[pl.ds(j, 1), :]
            hs_ref[pl.ds(j, 1), :] = h
            return h

        carry[0:1, :] = lax.fori_loop(0, tb, step, carry[0:1, :], unroll=8)
        y_ref[...] = (hs_ref[...] * _gelu(gb_ref[...])).astype(y_ref.dtype)

    nrow = tb // SUBLANE
    blk = lambda off: pl.BlockSpec((None, tb, w), lambda q, i: (q + off, i, 0))
    return _call(
        body, name=name,
        out_shape=[jax.ShapeDtypeStruct((half, t, w), F32), jax.ShapeDtypeStruct((half, t, w), MXU_DTYPE)],
        grid=(half, t // tb),
        in_specs=[blk(0), blk(half),
                  pl.BlockSpec((None, SUBLANE, w), lambda q, i: (q + half, jnp.maximum(i * nrow - 1, 0), 0)),
                  pl.BlockSpec((None,) + cvec.shape[1:], lambda q, i: (q, 0, 0)),
                  pl.BlockSpec((None, w, w), lambda q, i: (q, 0, 0)), pl.BlockSpec((None, w, w), lambda q, i: (q, 0, 0))],
        out_specs=[blk(0), blk(0)],
        scratch_shapes=[pltpu.VMEM((tb, w), F32), pltpu.VMEM((tb, w), F32), pltpu.VMEM((SUBLANE, w), F32)],
        args=(zz, zz, zz, cvec, wrg, wig), rider=rider)


def _lru_bwd(name, zz, hs, dy, cvec, wrg, wig, rider=None):
    _, t, w = zz.shape
    half = N_DEV // 2
    tb = _lru_time_block(t)
    nb = t // tb
    taps = cvec.shape[1] - 4

    def body(gb_ref, xp_ref, xprev_ref, hs_ref, hprev_ref, dy_ref, cv_ref, wrg_ref, wig_ref,
             dgb_ref, dxp_ref, dcv_ref, dwrg_ref, dwig_ref, a_scr, l_scr, carry, dxb_next):
        i = pl.program_id(1)

        @pl.when(i == 0)
        def _():
            for r_ in (carry, dxb_next, dcv_ref, dwrg_ref, dwig_ref):
                r_[...] = jnp.zeros_like(r_)

        has_prev = i < nb - 1
        row = lambda k: cv_ref[k:k + 1, :]
        prev8 = jnp.where(has_prev, xprev_ref[...], 0.0)
        xs, xb, r, ig, sp, a, mult = _lru_gates(xp_ref[...], prev8, cv_ref, wrg_ref[...], wig_ref[...])
        hs_ = hs_ref[...]
        hs_m1 = _shift_down(hs_, 1, jnp.where(has_prev, hprev_ref[...], 0.0))
        gel, dgel = _gelu_and_grad(gb_ref[...])
        dy_ = dy_ref[...]
        dgb_ref[...] = (dy_ * hs_ * dgel).astype(dgb_ref.dtype)
        a_scr[...] = a
        l_scr[...] = dy_ * gel

        def step(k, c):
            j = tb - 1 - k
            lam = l_scr[pl.ds(j, 1), :] + c
            l_scr[pl.ds(j, 1), :] = lam
            return a_scr[pl.ds(j, 1), :] * lam

        carry[0:1, :] = lax.fori_loop(0, tb, step, carry[0:1, :], unroll=8)
        lam = l_scr[...]
        dmult = lam * (ig * xb)
        dig = lam * (mult * xb)
        dxb = lam * (mult * ig)
        dlog_a = (lam * hs_m1) * a - dmult * (a * a) / mult
        dr = dlog_a * (-LRU_C * sp)
        dsp = jnp.sum(dlog_a * (-LRU_C * r), axis=0, keepdims=True)
        dpr = dr * (r * (1.0 - r))
        dpi = dig * (ig * (1.0 - ig))
        dwrg_ref[...] += _dot(xb, dpr, TN)
        dwig_ref[...] += _dot(xb, dpi, TN)
        dxb = dxb + _dot(dpr, wrg_ref[...], NT) + _dot(dpi, wig_ref[...], NT)
        for k in range(taps):
            dcv_ref[k:k + 1, :] += jnp.sum(dxb * xs[k], axis=0, keepdims=True)
        dcv_ref[taps:taps + 1, :] += jnp.sum(dxb, axis=0, keepdims=True)
        dcv_ref[taps + 1:taps + 2, :] += jnp.sum(dpr, axis=0, keepdims=True)
        dcv_ref[taps + 2:taps + 3, :] += jnp.sum(dpi, axis=0, keepdims=True)
        dcv_ref[taps + 3:taps + 4, :] += dsp * (-jax.nn.sigmoid(-row(taps + 3)))
        nxt8 = dxb_next[...]
        dxp = row(taps - 1) * dxb
        for k in range(taps - 1):
            dxp = dxp + row(k) * _shift_up(dxb, taps - 1 - k, nxt8)
        dxp_ref[...] = dxp.astype(dxp_ref.dtype)
        dxb_next[...] = dxb[0:SUBLANE]

    nrow = tb // SUBLANE
    blk = lambda off: pl.BlockSpec((None, tb, w), lambda q, i: (q + off, nb - 1 - i, 0))
    halo = lambda off: pl.BlockSpec((None, SUBLANE, w), lambda q, i: (q + off, jnp.maximum((nb - 1 - i) * nrow - 1, 0), 0))
    wspec = pl.BlockSpec((None, w, w), lambda q, i: (q, 0, 0))
    cspec = pl.BlockSpec((None,) + cvec.shape[1:], lambda q, i: (q, 0, 0))
    act = jax.ShapeDtypeStruct((half, t, w), MXU_DTYPE)
    return _call(
        body, name=name,
        out_shape=[act, act, jax.ShapeDtypeStruct(cvec.shape, F32), jax.ShapeDtypeStruct(wrg.shape, F32),
                   jax.ShapeDtypeStruct(wig.shape, F32)],
        grid=(half, nb),
        in_specs=[blk(0), blk(half), halo(half), blk(0), halo(0), blk(0), cspec, wspec, wspec],
        out_specs=[blk(0), blk(0), cspec, wspec, wspec],
        scratch_shapes=[pltpu.VMEM((tb, w), F32), pltpu.VMEM((tb, w), F32), pltpu.VMEM((SUBLANE, w), F32),
                        pltpu.VMEM((SUBLANE, w), F32)],
        args=(zz, zz, zz, hs, hs, dy, cvec, wrg, wig), rider=rider)


def _band(name, blocks, per, dtype):
    n, a, b = blocks.shape

    def body(x_ref, o_ref):
        o_ref[...] = jnp.zeros_like(o_ref)
        for g in range(per):
            o_ref[g * a:(g + 1) * a, g * b:(g + 1) * b] = x_ref[g].astype(o_ref.dtype)

    return pl.pallas_call(
        body, name=name, out_shape=jax.ShapeDtypeStruct((n // per, per * a, per * b), dtype), grid=(n // per,),
        in_specs=[pl.BlockSpec((per, a, b), lambda s: (s, 0, 0))],
        out_specs=pl.BlockSpec((None, per * a, per * b), lambda s: (s, 0, 0)), compiler_params=_params(1))(blocks)


def _unband(name, bands, per):
    s, pa, pb = bands.shape
    a, b = pa // per, pb // per

    def body(x_ref, o_ref):
        for g in range(per):
            o_ref[g] = x_ref[g * a:(g + 1) * a, g * b:(g + 1) * b]

    return pl.pallas_call(
        body, name=name, out_shape=jax.ShapeDtypeStruct((s * per, a, b), bands.dtype), grid=(s,),
        in_specs=[pl.BlockSpec((None, pa, pb), lambda i: (i, 0, 0))],
        out_specs=pl.BlockSpec((per, a, b), lambda i: (i, 0, 0)), compiler_params=_params(1))(bands)


def _pack(arrays, rows_multiple, lanes=LANE):
    flat = [a.reshape(-1).astype(F32) for a in arrays]
    size = sum(a.shape[0] for a in flat)
    rows = -(-size // (lanes * rows_multiple)) * rows_multiple
    if rows * lanes > size:
        flat.append(jnp.zeros((rows * lanes - size,), F32))
    return jnp.concatenate(flat).reshape(rows, lanes)


def _unpack(packed, shapes):
    flat = packed.reshape(-1)
    out, off = [], 0
    for s in shapes:
        n = math.prod(s)
        out.append(flat[off:off + n].reshape(s))
        off += n
    return out


def kernel(x, c, norm_g, w_ada, b_ada, s5_w_in, s5_lam_re, s5_lam_im, s5_log_dt, s5_b_re, s5_b_im, s5_c_re, s5_c_im, s5_d, s5_w_glu, lru_w_in, lru_conv_w, lru_conv_b, lru_w_rg, lru_b_rg, lru_w_ig, lru_b_ig, lru_lam, lru_w_out, ffn_w_gu, ffn_w_down, final_g, loss_target, m_norm_g, m_w_ada, m_b_ada, m_s5_w_in, m_s5_lam_re, m_s5_lam_im, m_s5_log_dt, m_s5_b_re, m_s5_b_im, m_s5_c_re, m_s5_c_im, m_s5_d, m_s5_w_glu, m_lru_w_in, m_lru_conv_w, m_lru_conv_b, m_lru_w_rg, m_lru_b_rg, m_lru_w_ig, m_lru_b_ig, m_lru_lam, m_lru_w_out, m_ffn_w_gu, m_ffn_w_down, m_final_g, v_norm_g, v_w_ada, v_b_ada, v_s5_w_in, v_s5_lam_re, v_s5_lam_im, v_s5_log_dt, v_s5_b_re, v_s5_b_im, v_s5_c_re, v_s5_c_im, v_s5_d, v_s5_w_glu, v_lru_w_in, v_lru_conv_w, v_lru_conv_b, v_lru_w_rg, v_lru_b_rg, v_lru_w_ig, v_lru_b_ig, v_lru_lam, v_lru_w_out, v_ffn_w_gu, v_ffn_w_down, v_final_g):
    wv = dict(zip(WEIGHTS, (norm_g, w_ada, b_ada, s5_w_in, s5_lam_re, s5_lam_im, s5_log_dt, s5_b_re, s5_b_im, s5_c_re, s5_c_im, s5_d, s5_w_glu, lru_w_in, lru_conv_w, lru_conv_b, lru_w_rg, lru_b_rg, lru_w_ig, lru_b_ig, lru_lam, lru_w_out, ffn_w_gu, ffn_w_down, final_g)))
    mv = dict(zip(WEIGHTS, (m_norm_g, m_w_ada, m_b_ada, m_s5_w_in, m_s5_lam_re, m_s5_lam_im, m_s5_log_dt, m_s5_b_re, m_s5_b_im, m_s5_c_re, m_s5_c_im, m_s5_d, m_s5_w_glu, m_lru_w_in, m_lru_conv_w, m_lru_conv_b, m_lru_w_rg, m_lru_b_rg, m_lru_w_ig, m_lru_b_ig, m_lru_lam, m_lru_w_out, m_ffn_w_gu, m_ffn_w_down, m_final_g)))
    vv = dict(zip(WEIGHTS, (v_norm_g, v_w_ada, v_b_ada, v_s5_w_in, v_s5_lam_re, v_s5_lam_im, v_s5_log_dt, v_s5_b_re, v_s5_b_im, v_s5_c_re, v_s5_c_im, v_s5_d, v_s5_w_glu, v_lru_w_in, v_lru_conv_w, v_lru_conv_b, v_lru_w_rg, v_lru_b_rg, v_lru_w_ig, v_lru_b_ig, v_lru_lam, v_lru_w_out, v_ffn_w_gu, v_ffn_w_down, v_final_g)))

    me = 4 * lax.axis_index("x") + 2 * lax.axis_index("y") + lax.axis_index("c")
    x0 = x[0]
    tgt = loss_target[0]
    t, d = x0.shape
    depth = norm_g.shape[0]
    n_mod = w_ada.shape[2] * N_DEV // d
    groups, states = s5_lam_re.shape[1], s5_lam_re.shape[2]
    per_sg = S5_SUPER // S5_GROUP
    nsg = groups // per_sg
    lw = lru_lam.shape[1] * N_DEV
    lwc = lw // (N_DEV // 2)
    half = N_DEV // 2

    assert depth == 2, "the ride schedule below is written for one S5 layer followed by one RG-LRU layer"
    wire = lambda a: a.astype(WIRE_DTYPE)
    gw = {'s5_in': _all_gather("ag_s5_w_in", wire(s5_w_in[0]))}

    def riding(job, fn, *args):
        res, (got,) = fn(*args, rider=_gather_rider([wire(job[1])]))
        gw[job[0]] = got
        return res

    sh_shapes = [wv[n].shape for n in SMALL_SHARDED] + [c.shape]
    sh_all = _all_gather("ag_small", _pack([wv[n] for n in SMALL_SHARDED] + [c], SUBLANE))
    sh_parts = [jnp.stack(p) for p in zip(*[_unpack(sh_all[s], sh_shapes) for s in range(N_DEV)])]
    full = {}
    for n, p in zip(SMALL_SHARDED, sh_parts[:-1]):
        full[n] = jnp.moveaxis(p, 0, -2).reshape(p.shape[1:-1] + (-1,))
    c_all = sh_parts[-1].reshape(N_DEV, d)
    c16 = jnp.pad(c_all, ((0, 2 * SUBLANE - N_DEV), (0, 0)))

    n_loc = w_ada.shape[2]
    b_loc = lax.dynamic_slice_in_dim(b_ada, me * n_loc, n_loc, axis=1)[:, None, :]
    mod_part = _ada_fwd("ada_fwd", c16, w_ada, b_loc)[:, :N_DEV]
    mod_mine = _chunk_exchange("x_mod", [mod_part.transpose(1, 0, 2)], ALL)
    mod = mod_mine.transpose(1, 0, 2).reshape(depth, n_mod, 1, d)

    lam3 = lambda a: a[0][:, None, :]
    p_lr, p_li, p_ld = lam3(s5_lam_re), lam3(s5_lam_im), s5_log_dt[0][:, None, None]
    p_br, p_bi = s5_b_re[0].transpose(0, 2, 1), s5_b_im[0].transpose(0, 2, 1)
    ab_re3, ab_im3, bb_re3, bb_im3 = _s5_disc("s5_disc", p_lr, p_li, p_ld, p_br, p_bi)
    ab_re, ab_im = ab_re3.reshape(nsg, per_sg * states), ab_im3.reshape(nsg, per_sg * states)
    bb_re = _band("band_bb_re", bb_re3, per_sg, MXU_DTYPE)
    bb_im = _band("band_bb_im", bb_im3, per_sg, MXU_DTYPE)
    cc_re = _band("band_cc_re", s5_c_re[0].transpose(0, 2, 1), per_sg, MXU_DTYPE)
    cc_im = _band("band_cc_im", s5_c_im[0].transpose(0, 2, 1), per_sg, MXU_DTYPE)

    taps = lru_conv_w.shape[1]
    cvec = jnp.concatenate([full['lru_conv_w'].reshape(taps, lw), full['lru_conv_b'], full['lru_b_rg'],
                            full['lru_b_ig'], full['lru_lam']], axis=0)
    cvec = cvec.reshape(taps + 4, half, lwc).transpose(1, 0, 2)
    wrg = _band("band_w_rg", lru_w_rg[0], LRU_BLOCKS_PER_CHUNK, MXU_DTYPE)
    wig = _band("band_w_ig", lru_w_ig[0], LRU_BLOCKS_PER_CHUNK, MXU_DTYPE)

    saved = []
    xc = x0
    for i in range(depth):
        sh1, sc1, g1, sh2, sc2, g2 = [mod[i, k] for k in range(n_mod)]
        gn = full['norm_g'][i]
        h1 = _norm_mod_fwd(f"norm1_fwd{i}", xc, gn[0:1], sc1, sh1)
        if i % 2 == 0:
            u = riding(('s5_glu', s5_w_glu[0]), _mm_row, f"s5_in{i}", h1[None], gw['s5_in'].reshape(d, d))
            s_re, s_im, s_rem, s_imm, ypre, yact = riding((('gu', i), ffn_w_gu[i]), _s5_scan_fwd, f"s5_scan{i}", u, bb_re,
                                                          bb_im, ab_re, ab_im, cc_re, cc_im, s5_d)
            z = riding((('down', i), ffn_w_down[i]), _mm_col, f"s5_glu{i}", yact, gw['s5_glu'])
            x1 = _glu_resid_fwd(f"s5_resid{i}", z, xc, g1)
            mix = (u, s_re, s_im, s_rem, s_imm, ypre, yact, z)
        else:
            zz = _mm_col(f"lru_in{i}", h1, gw['lru_in'])
            hs, ylru = riding((('gu', i), ffn_w_gu[i]), _lru_fwd, f"lru_core{i}", zz, cvec, wrg, wig)
            o = _mm_row(f"lru_out{i}", ylru, gw['lru_out'].reshape(lw, d))
            x1 = _resid(f"lru_resid{i}", xc, o, g1)
            mix = (zz, hs, ylru, o)
        h2 = _norm_mod_fwd(f"norm2_fwd{i}", x1, gn[1:2], sc2, sh2)
        if i % 2 == 0:
            gu = riding(('lru_in', lru_w_in[0]), _mm_col, f"ffn_gu{i}", h2, gw['gu', i], MXU_DTYPE)
            act = _swiglu_act_fwd(f"ffn_act{i}", gu)
            f = riding(('lru_out', lru_w_out[0]), _mm_row, f"ffn_down{i}", act, gw['down', i].reshape(-1, d))
        else:
            gu = riding((('down', i), ffn_w_down[i]), _mm_col, f"ffn_gu{i}", h2, gw['gu', i], MXU_DTYPE)
            act = _swiglu_act_fwd(f"ffn_act{i}", gu)
            f = _mm_row(f"ffn_down{i}", act, gw['down', i].reshape(-1, d))
        x2 = _resid(f"ffn_resid{i}", x1, f, g2)
        saved.append((xc, h1, mix, x1, h2, gu, act, f))
        xc = x2

    dx, loss_part, d_final_g = _loss_bwd("loss", xc, tgt, final_g[None])
    loss = lax.psum(loss_part[0, 0], ("x", "y", "c"))

    grads = {}
    parts = {}
    dmod = [None] * depth
    d_norm_g = [None] * depth
    core = lax.axis_index("c").astype(jnp.int32).reshape(1)
    chunked = lambda p: p.reshape(N_DEV, -1, p.shape[-1])
    to_sibling = lambda p: _sibling_rider(chunked(p))
    pair = lambda name, p, got: _pair_sum(name, chunked(p), got, core)
    over_ici = lambda sums: _chunk_rider([sums], SAME_CORE)
    quarter_over_ici = lambda sums, q: _chunk_rider([sums], SAME_CORE, rows=(q * (sums.shape[1] // 4), sums.shape[1] // 4))

    above = None
    for i in reversed(range(depth)):
        xin, h1, mix, x1, h2, gu, act, f = saved[i]
        sh1, sc1, g1, sh2, sc2, g2 = [mod[i, k] for k in range(n_mod)]
        gn = full['norm_g'][i]
        g_down = gw['down', i].reshape(-1, d)
        df, dg2 = _gate_bwd(f"ffn_gate_bwd{i}", dx, f, g2)
        if above is None:
            dact = _mm_row_da(f"ffn_down_da{i}", df, g_down, half)
        else:
            dact, (got,) = _ride(_mm_row_da, f"ffn_down_da{i}", df, g_down, half, riders=[to_sibling(above[1])])
            s_above = pair(f"x_{above[0][0]}_pair", above[1], got)
        p_down = _mm_row_db(f"ffn_down_db{i}", act, df, WIRE_DTYPE)
        dgu, (got,) = _ride(_swiglu_act_bwd, f"ffn_act_bwd{i}", gu, dact, riders=[to_sibling(p_down)])
        s_down = pair(f"x_ffn_w_down{i}_pair", p_down, got)
        if above is None:
            dh2, (parts['ffn_w_down', i],) = _ride(_mm_col_da, f"ffn_gu_da{i}", dgu, gw['gu', i], riders=[over_ici(s_down)])
            p_gu = _mm_col_db(f"ffn_gu_db{i}", h2, dgu, WIRE_DTYPE)
        else:
            dh2, (parts[above[0]],) = _ride(_mm_col_da, f"ffn_gu_da{i}", dgu, gw['gu', i], riders=[over_ici(s_above)])
            p_gu, (parts['ffn_w_down', i],) = _ride(_mm_col_db, f"ffn_gu_db{i}", h2, dgu, WIRE_DTYPE,
                                                    riders=[over_ici(s_down)])
        (dx, dgn2, dsc2, dsh2), (got,) = _ride(_norm_mod_bwd, f"norm2_bwd{i}", x1, dh2, dx, gn[1:2], sc2,
                                               riders=[to_sibling(p_gu)])
        s_gu = pair(f"x_ffn_w_gu{i}_pair", p_gu, got)
        if i % 2 == 0:
            u, s_re, s_im, s_rem, s_imm, ypre, yact, z = mix
            dz, dg1 = _glu_resid_bwd(f"s5_resid_bwd{i}", z, dx, g1)
            dyact, (gu_0,) = _ride(_mm_col_da, f"s5_glu_da{i}", dz, gw['s5_glu'], riders=[quarter_over_ici(s_gu, 0)])
            p_glu, (gu_1,) = _ride(_mm_col_db, f"s5_glu_db{i}", yact, dz, WIRE_DTYPE, riders=[quarter_over_ici(s_gu, 1)])
            (dyp, l_rem, l_imm, dab_re, dab_im), (gu_2, gu_3, got) = _ride(
                _s5_scan_bwd, f"s5_scan_bwd{i}", dyact, ypre, cc_re, cc_im, ab_re, ab_im, s_re, s_im,
                riders=[quarter_over_ici(s_gu, 2), quarter_over_ici(s_gu, 3), to_sibling(p_glu)])
            parts['ffn_w_gu', i] = [gu_0, gu_1, gu_2, gu_3]
            s_glu = pair("x_s5_w_glu_pair", p_glu, got)
            (du, dbb_re, dbb_im, dcc_re, dcc_im, dd), (parts['s5_w_glu', 0],) = _ride(
                _s5_grads, f"s5_grads{i}", l_rem, l_imm, s_rem, s_imm, u, dyp, bb_re, bb_im, s5_d, riders=[over_ici(s_glu)])
            dlr, dli, dld, dbr, dbi = _s5_disc_bwd(
                "s5_disc_bwd", p_lr, p_li, p_ld, p_br, p_bi, dab_re.reshape(groups, 1, states),
                dab_im.reshape(groups, 1, states), _unband("unband_bb_re", dbb_re, per_sg),
                _unband("unband_bb_im", dbb_im, per_sg))
            grads['s5_lam_re'], grads['s5_lam_im'], grads['s5_log_dt'] = dlr[:, 0][None], dli[:, 0][None], dld[:, 0, 0][None]
            grads['s5_b_re'], grads['s5_b_im'] = dbr.transpose(0, 2, 1)[None], dbi.transpose(0, 2, 1)[None]
            grads['s5_c_re'] = _unband("unband_cc_re", dcc_re, per_sg).transpose(0, 2, 1)[None]
            grads['s5_c_im'] = _unband("unband_cc_im", dcc_im, per_sg).transpose(0, 2, 1)[None]
            grads['s5_d'] = dd
            dub = du.astype(MXU_DTYPE)
            p_s5_in = _mm_row_db(f"s5_in_db{i}", h1[None], dub, WIRE_DTYPE)
            dh1, (got,) = _ride(_mm_row_da, f"s5_in_da{i}", dub, gw['s5_in'].reshape(d, d), 1, riders=[to_sibling(p_s5_in)])
            dh1 = dh1[0]
            s_s5_in = pair("x_s5_w_in_pair", p_s5_in, got)
        else:
            zz, hs, ylru, o = mix
            g_lru_out = gw['lru_out'].reshape(lw, d)
            do, dg1 = _gate_bwd(f"lru_gate_bwd{i}", dx, o, g1)
            dyl = _mm_row_da(f"lru_out_da{i}", do, g_lru_out, half)
            p_lru_out = _mm_row_db(f"lru_out_db{i}", ylru, do, WIRE_DTYPE)
            (dgb, dxp, dcv, dwrg, dwig), (*parts['ffn_w_gu', i], got) = _ride(
                _lru_bwd, f"lru_core_bwd{i}", zz, hs, dyl, cvec, wrg, wig,
                riders=[quarter_over_ici(s_gu, q) for q in range(4)] + [to_sibling(p_lru_out)])
            s_lru_out = pair("x_lru_w_out_pair", p_lru_out, got)
            dzz = jnp.concatenate([dgb, dxp], axis=0)
            dh1, (parts['lru_w_out', 0],) = _ride(_mm_col_da, f"lru_in_da{i}", dzz, gw['lru_in'],
                                                  riders=[over_ici(s_lru_out)])
            above = (('lru_w_in', 0), _mm_col_db(f"lru_in_db{i}", h1, dzz, WIRE_DTYPE))
            dcv = dcv.transpose(1, 0, 2).reshape(taps + 4, lw)
            grads['lru_conv_w'] = dcv[:taps].reshape(1, taps, 1, lw)
            grads['lru_conv_b'], grads['lru_b_rg'] = dcv[taps:taps + 1], dcv[taps + 1:taps + 2]
            grads['lru_b_ig'], grads['lru_lam'] = dcv[taps + 2:taps + 3], dcv[taps + 3:taps + 4]
            grads['lru_w_rg'] = _unband("unband_w_rg", dwrg, LRU_BLOCKS_PER_CHUNK)[None]
            grads['lru_w_ig'] = _unband("unband_w_ig", dwig, LRU_BLOCKS_PER_CHUNK)[None]
        dx, dgn1, dsc1, dsh1 = _norm_mod_bwd(f"norm1_bwd{i}", xin, dh1, dx, gn[0:1], sc1)
        dmod[i] = jnp.concatenate([dsh1, dsc1, dg1, dsh2, dsc2, dg2], axis=1)
        d_norm_g[i] = jnp.concatenate([dgn1, dgn2], axis=0)
    grad_x = dx[None]
    dmod = jnp.concatenate(dmod, axis=0)
    grads['norm_g'] = jnp.stack(d_norm_g)
    grads['b_ada'] = dmod
    grads['final_g'] = d_final_g[0]

    small_partial = _pack([grads[n] for n in SMALL], SUBLANE * N_DEV)
    rows8 = small_partial.shape[0] // N_DEV
    small_partial = small_partial.reshape(N_DEV, rows8, LANE)
    s_small = pair("x_small_pair", small_partial, _ride_alone("x_small_d2d", _sibling_rider(small_partial)))
    parts['s5_w_in', 0], small_parts = _ride_alone("x_tail_ici", _join([over_ici(s_s5_in), over_ici(s_small)]))

    out = {}
    dmod_all = _all_gather("ag_dmod", dmod)
    dmod_loc = lax.dynamic_slice_in_dim(dmod_all, me * n_loc, n_loc, axis=2).transpose(1, 0, 2)
    dmod16 = jnp.pad(dmod_loc, ((0, 0), (0, 2 * SUBLANE - N_DEV), (0, 0)))
    out['w_ada'] = _adamw_w_ada("adamw_w_ada", c16, dmod16, w_ada, m_w_ada, v_w_ada)

    for name in BIG[1:]:
        w = wv[name]
        rows, cols = w.shape[-2] * w.shape[0], w.shape[-1]
        flat = lambda a: a.reshape(rows, cols)
        pieces = []
        for l in range(w.shape[0]):
            pieces += parts[name, l] if isinstance(parts[name, l], list) else [parts[name, l]]
        res = _adamw_sum("adamw_" + name, pieces, flat(w), flat(mv[name]), flat(vv[name]))
        out[name] = [r.reshape(w.shape) for r in res]

    summed = _sum_parts("sum_small", small_parts)
    small_total = _all_gather("ag_small_sum", summed).reshape(-1, LANE)
    small_grad = dict(zip(SMALL, _unpack(small_total, [grads[n].shape for n in SMALL])))
    for n in SMALL_SHARDED:
        shard = wv[n].shape[-1]
        small_grad[n] = lax.dynamic_slice_in_dim(small_grad[n], me * shard, shard, axis=small_grad[n].ndim - 1)
    for n in SMALL:
        w = wv[n]
        flat = lambda a: a.reshape(-1, w.shape[-1])
        res = _adamw_sum("adamw_" + n, [flat(small_grad[n])[None]], flat(w), flat(mv[n]), flat(vv[n]))
        out[n] = [r.reshape(w.shape) for r in res]

    return (loss, grad_x, *[out[n][0] for n in WEIGHTS], *[out[n][1] for n in WEIGHTS],
            *[out[n][2] for n in WEIGHTS], *[out[n][3] for n in WEIGHTS])
```

```python
import functools
import math

import jax
import jax.numpy as jnp
from jax import lax
from jax.experimental import pallas as pl
from jax.experimental.pallas import tpu as pltpu

F32 = jnp.float32
MXU_DTYPE = jnp.bfloat16
WIRE_DTYPE = jnp.bfloat16
N_DEV = 8
EPS = 1e-6
LRU_C = 8.0
S5_GROUP = 16
S5_STATE = 64
S5_SUPER = 256
LRU_BLOCKS_PER_CHUNK = 4
ADAM_LR, ADAM_B1, ADAM_B2, ADAM_EPS, ADAM_WD, ADAM_STEP = 0.001, 0.9, 0.999, 1e-08, 0.01, 10
VMEM_LIMIT_BYTES = 56 * 1024 * 1024
LANE = 128
SUBLANE = 8

WEIGHTS = ['norm_g', 'w_ada', 'b_ada', 's5_w_in', 's5_lam_re', 's5_lam_im', 's5_log_dt', 's5_b_re', 's5_b_im',
           's5_c_re', 's5_c_im', 's5_d', 's5_w_glu', 'lru_w_in', 'lru_conv_w', 'lru_conv_b', 'lru_w_rg', 'lru_b_rg',
           'lru_w_ig', 'lru_b_ig', 'lru_lam', 'lru_w_out', 'ffn_w_gu', 'ffn_w_down', 'final_g']
BIG = ('w_ada', 's5_w_in', 's5_w_glu', 'lru_w_in', 'lru_w_out', 'ffn_w_gu', 'ffn_w_down')
SMALL = tuple(n for n in WEIGHTS if n not in BIG)
SMALL_SHARDED = ('norm_g', 'lru_conv_w', 'lru_conv_b', 'lru_b_rg', 'lru_b_ig', 'lru_lam')

NN = (((1,), (0,)), ((), ()))
NT = (((1,), (1,)), ((), ()))
TN = (((0,), (0,)), ((), ()))


def _params(n_grid):
    return pltpu.CompilerParams(dimension_semantics=("arbitrary",) * n_grid, vmem_limit_bytes=VMEM_LIMIT_BYTES)


def _tile(dim, pref, align=LANE):
    if dim <= pref:
        return dim
    t = (pref // align) * align
    while t >= align:
        if dim % t == 0:
            return t
        t -= align
    return dim


def _dot(a, b, dims):
    return lax.dot_general(a.astype(MXU_DTYPE), b.astype(MXU_DTYPE), dims, preferred_element_type=F32)


def _gelu(x):
    k = math.sqrt(2.0 / math.pi)
    return 0.5 * x * (1.0 + jnp.tanh(k * (x + 0.044715 * (x * x * x))))


def _gelu_and_grad(x):
    k = math.sqrt(2.0 / math.pi)
    th = jnp.tanh(k * (x + 0.044715 * (x * x * x)))
    g = 0.5 * x * (1.0 + th)
    dg = 0.5 * (1.0 + th) + 0.5 * x * (1.0 - th * th) * (k * (1.0 + 3.0 * 0.044715 * (x * x)))
    return g, dg


def _neg_expm1(x):
    series = -x * (1.0 + x * (0.5 + x * (1.0 / 6.0 + x * (1.0 / 24.0 + x * (1.0 / 120.0)))))
    return jnp.where(x > -0.01, series, 1.0 - jnp.exp(x))


MESH = pl.DeviceIdType.MESH
N_CHIP = N_DEV // 2
ALL, SAME_CORE = 7, 6


def _place():
    x, y, c = lax.axis_index("x"), lax.axis_index("y"), lax.axis_index("c")
    return x, y, c


def _flip(place, k):
    x, y, c = place
    return (1 - x if (k >> 2) & 1 else x, 1 - y if (k >> 1) & 1 else y, 1 - c if k & 1 else c)


def _chunk_exchange(name, xs, group):
    return _ride_alone(name, _chunk_rider(xs, group))


class _Rider:
    def __init__(self, arrays, out_shape, scratch, start, finish, post, mid=None):
        self.arrays, self.out_shape, self.scratch = list(arrays), list(out_shape), list(scratch)
        self.start, self.finish, self.post, self.mid = start, finish, post, mid


def _chunk_rider(xs, group, rows=None):
    n = len(xs)
    members, r_all, c_ = xs[0].shape
    r0, r = (0, r_all) if rows is None else rows
    assert members == {ALL: N_DEV, SAME_CORE: N_CHIP}[group]
    assert all(a.shape == xs[0].shape and a.dtype == xs[0].dtype for a in xs)
    ks = [k for k in range(1, N_DEV) if not k & ~group]
    member = (lambda p: 4 * p[0] + 2 * p[1] + p[2]) if group == ALL else (lambda p: 2 * p[0] + p[1])

    def copies(ins, outs, scratch):
        out = outs[0]
        send_sems, recv_sems, local_sems = scratch
        place = _place()
        me = member(place)
        src = lambda l, who: ins[l].at[who] if rows is None else ins[l].at[who, pl.ds(r0, r)]
        local = [pltpu.make_async_copy(src(l, me), out.at[me, l], local_sems.at[l]) for l in range(n)]
        remote = []
        for l in range(n):
            for k in ks:
                pid = _flip(place, k)
                peer = member(pid)

                def copy(land_at, l=l, k=k, peer=peer, pid=pid):
                    return pltpu.make_async_remote_copy(
                        src_ref=src(l, peer), dst_ref=out.at[land_at, l], send_sem=send_sems.at[l * N_DEV + k],
                        recv_sem=recv_sems.at[l * N_DEV + k], device_id=pid, device_id_type=MESH)

                remote.append((copy, me, peer))
        return local, remote

    def start(ins, outs, scratch):
        local, remote = copies(ins, outs, scratch)
        for cp in local:
            cp.start()
        for copy, me, _ in remote:
            copy(me).start()

    def finish(ins, outs, scratch):
        local, remote = copies(ins, outs, scratch)
        for copy, me, peer in remote:
            copy(me).wait_send()
            copy(peer).wait_recv()
        for cp in local:
            cp.wait()

    return _Rider(
        xs, [jax.ShapeDtypeStruct((members, n, r, c_), xs[0].dtype)],
        [pltpu.SemaphoreType.DMA((n * N_DEV,)), pltpu.SemaphoreType.DMA((n * N_DEV,)), pltpu.SemaphoreType.DMA((n,))],
        start, finish, lambda outs: outs[0].reshape(members, n * r, c_))


def _gather_rider(xs):
    n = len(xs)
    per = 7

    def plan(ins, outs, scratch):
        send_sems, recv_sems, local_sems = scratch
        x, y, c = place = _place()
        sibling, x_nb, y_nb, diag = (x, y, 1 - c), (1 - x, y, c), (x, 1 - y, c), (1 - x, 1 - y, c)
        relayed = (x + c * (1 - 2 * x), y + (1 - c) * (1 - 2 * y), c)
        onward = (x + (1 - c) * (1 - 2 * x), y + c * (1 - 2 * y), c)
        jobs = []
        for l in range(n):
            slot = lambda p, l=l: outs[l].at[2 * p[0] + p[1], p[2]]

            def copy(k, block, to, src=None, l=l, slot=slot):
                return pltpu.make_async_remote_copy(
                    src_ref=slot(block) if src is None else src, dst_ref=slot(block), send_sem=send_sems.at[l * per + k],
                    recv_sem=recv_sems.at[l * per + k], device_id=to, device_id_type=MESH)

            jobs.append(dict(
                mine=pltpu.make_async_copy(ins[l], slot(place), local_sems.at[l]),
                first=[copy(0, place, sibling, src=ins[l]), copy(1, place, x_nb, src=ins[l]), copy(2, place, y_nb, src=ins[l])],
                landed=[copy(1, x_nb, place), copy(2, y_nb, place)],
                second=[copy(3, relayed, onward), copy(4, x_nb, sibling), copy(5, y_nb, sibling)],
                relay_landed=copy(3, diag, place), last=copy(6, diag, sibling),
                from_sibling=[copy(0, sibling, place)] + [copy(4 + j, (p[0], p[1], 1 - c), place)
                                                          for j, p in enumerate((x_nb, y_nb, diag))]))
        return jobs

    def start(ins, outs, scratch):
        for job in plan(ins, outs, scratch):
            job['mine'].start()
            for cp in job['first']:
                cp.start()

    def mid(ins, outs, scratch):
        for job in plan(ins, outs, scratch):
            for cp in job['landed']:
                cp.wait_recv()
            for cp in job['second']:
                cp.start()

    def finish(ins, outs, scratch):
        jobs = plan(ins, outs, scratch)
        for job in jobs:
            job['relay_landed'].wait_recv()
            job['last'].start()
        for job in jobs:
            for cp in job['from_sibling']:
                cp.wait_recv()
            for cp in job['first'] + job['second'] + [job['last']]:
                cp.wait_send()
            job['mine'].wait()

    return _Rider(
        xs, [jax.ShapeDtypeStruct((N_CHIP, 2) + x.shape, x.dtype) for x in xs],
        [pltpu.SemaphoreType.DMA((n * per,)), pltpu.SemaphoreType.DMA((n * per,)), pltpu.SemaphoreType.DMA((n,))],
        start, finish, lambda outs: [o.reshape((N_DEV,) + x.shape) for o, x in zip(outs, xs)], mid=mid)


HBM_SPEC = pl.BlockSpec(memory_space=pltpu.HBM)


def _ride_alone(name, rider):
    n_in, n_out = len(rider.arrays), len(rider.out_shape)

    def body(*refs):
        parts = refs[:n_in], refs[n_in:n_in + n_out], refs[n_in + n_out:]
        rider.start(*parts)
        if rider.mid is not None:
            rider.mid(*parts)
        rider.finish(*parts)

    outs = pl.pallas_call(body, name=name, out_shape=rider.out_shape, in_specs=[HBM_SPEC] * n_in,
                          out_specs=[HBM_SPEC] * n_out, scratch_shapes=rider.scratch)(*rider.arrays)
    return rider.post(list(outs))


def _call(body, *, name, grid, in_specs, out_specs, out_shape, scratch_shapes=(), args, rider=None):
    single = not isinstance(out_shape, (list, tuple))
    out_shape = [out_shape] if single else list(out_shape)
    out_specs = [out_specs] if single else list(out_specs)
    scratch_shapes = list(scratch_shapes)
    unwrap = lambda outs: outs[0] if single else list(outs)
    if rider is None:
        outs = pl.pallas_call(body, name=name, grid=grid, in_specs=list(in_specs), out_specs=out_specs, out_shape=out_shape,
                              scratch_shapes=scratch_shapes, compiler_params=_params(len(grid)))(*args)
        return unwrap(outs)
    n_in, n_out, n_scr = len(in_specs), len(out_shape), len(scratch_shapes)
    r_in, r_out = len(rider.arrays), len(rider.out_shape)

    def carried(*refs):
        ins, refs = refs[:n_in], refs[n_in:]
        r_ins, refs = refs[:r_in], refs[r_in:]
        outs, refs = refs[:n_out], refs[n_out:]
        r_outs, refs = refs[:r_out], refs[r_out:]
        scr, r_scr = refs[:n_scr], refs[n_scr:]
        step = 0
        for ax, g in enumerate(grid):
            step = step * g + pl.program_id(ax)

        @pl.when(step == 0)
        def _():
            rider.start(r_ins, r_outs, r_scr)

        body(*ins, *outs, *scr)

        if rider.mid is not None and total > 1:
            @pl.when(step == (5 * (total - 1)) // 8)
            def _():
                rider.mid(r_ins, r_outs, r_scr)

        @pl.when(step == total - 1)
        def _():
            if rider.mid is not None and total == 1:
                rider.mid(r_ins, r_outs, r_scr)
            rider.finish(r_ins, r_outs, r_scr)

    total = math.prod(grid)

    outs = pl.pallas_call(
        carried, name=name, grid=grid, in_specs=list(in_specs) + [HBM_SPEC] * r_in, out_specs=out_specs + [HBM_SPEC] * r_out,
        out_shape=out_shape + rider.out_shape, scratch_shapes=scratch_shapes + rider.scratch,
        compiler_params=_params(len(grid)))(*args, *rider.arrays)
    return unwrap(outs[:n_out]), rider.post(list(outs[n_out:]))


def _all_gather(name, x):
    return _ride_alone(name, _gather_rider([x]))[0]


def _sibling_rider(x):
    _, r, c_ = x.shape

    def copies(ins, outs, scratch):
        send_sems, recv_sems = scratch
        place = _place()
        return [pltpu.make_async_remote_copy(
            src_ref=ins[0].at[2 * chip + (1 - place[2])], dst_ref=outs[0].at[chip], send_sem=send_sems.at[chip],
            recv_sem=recv_sems.at[chip], device_id=_flip(place, 1), device_id_type=MESH) for chip in range(N_CHIP)]

    def start(ins, outs, scratch):
        for cp in copies(ins, outs, scratch):
            cp.start()

    def finish(ins, outs, scratch):
        for cp in copies(ins, outs, scratch):
            cp.wait()

    return _Rider([x], [jax.ShapeDtypeStruct((N_CHIP, r, c_), x.dtype)],
                  [pltpu.SemaphoreType.DMA((N_CHIP,)), pltpu.SemaphoreType.DMA((N_CHIP,))], start, finish, lambda outs: outs[0])


def _join(riders):
    def cut(seq, counts):
        out, off = [], 0
        for k in counts:
            out.append(seq[off:off + k])
            off += k
        return out

    def parts(ins, outs, scratch):
        return zip(riders, cut(ins, [len(r.arrays) for r in riders]), cut(outs, [len(r.out_shape) for r in riders]),
                   cut(scratch, [len(r.scratch) for r in riders]))

    def start(ins, outs, scratch):
        for r, i, o, s in parts(ins, outs, scratch):
            r.start(i, o, s)

    def finish(ins, outs, scratch):
        for r, i, o, s in parts(ins, outs, scratch):
            r.finish(i, o, s)

    def mid(ins, outs, scratch):
        for r, i, o, s in parts(ins, outs, scratch):
            if r.mid is not None:
                r.mid(i, o, s)

    return _Rider(
        [a for r in riders for a in r.arrays], [o for r in riders for o in r.out_shape], [s for r in riders for s in r.scratch],
        start, finish, lambda outs: [r.post(o) for r, o in zip(riders, cut(outs, [len(r.out_shape) for r in riders]))],
        mid=mid if any(r.mid is not None for r in riders) else None)


def _ride(fn, *args, riders):
    return fn(*args, rider=_join(riders))


def _pair_sum(name, x, got, core):
    _, r, c_ = x.shape
    br = _tile(r, max(256, (1 << 21) // c_), 2 * SUBLANE)

    def body(core_ref, x_ref, g_ref, o_ref):
        o_ref[...] = (x_ref[...].astype(F32) + g_ref[...].astype(F32)).astype(o_ref.dtype)

    return pl.pallas_call(
        body, name=name, out_shape=jax.ShapeDtypeStruct((N_CHIP, r, c_), x.dtype),
        grid_spec=pltpu.PrefetchScalarGridSpec(
            num_scalar_prefetch=1, grid=(N_CHIP, r // br),
            in_specs=[pl.BlockSpec((None, br, c_), lambda ch, i, core_ref: (2 * ch + core_ref[0], i, 0)),
                      pl.BlockSpec((None, br, c_), lambda ch, i, core_ref: (ch, i, 0))],
            out_specs=pl.BlockSpec((None, br, c_), lambda ch, i, core_ref: (ch, i, 0))),
        compiler_params=_params(2))(core, x, got)


def _mm(name, a, b, out_shape, out_dtype, grid, a_spec, b_spec, o_spec, dims, n_red, acc_shape, rider=None):
    red = tuple(range(len(grid) - n_red, len(grid)))
    out_type = jax.ShapeDtypeStruct(out_shape, out_dtype)
    if all(grid[ax] == 1 for ax in red):
        def single(a_ref, b_ref, o_ref):
            o_ref[...] = _dot(a_ref[...], b_ref[...], dims).astype(o_ref.dtype)

        return _call(single, name=name, out_shape=out_type, grid=grid, in_specs=[a_spec, b_spec], out_specs=o_spec,
                     args=(a, b), rider=rider)

    def body(a_ref, b_ref, o_ref, acc_ref):
        first = functools.reduce(jnp.logical_and, [pl.program_id(ax) == 0 for ax in red])
        last = functools.reduce(jnp.logical_and, [pl.program_id(ax) == grid[ax] - 1 for ax in red])

        @pl.when(first)
        def _():
            acc_ref[...] = jnp.zeros_like(acc_ref)

        acc_ref[...] += _dot(a_ref[...], b_ref[...], dims)

        @pl.when(last)
        def _():
            o_ref[...] = acc_ref[...].astype(o_ref.dtype)

    return _call(body, name=name, out_shape=out_type, grid=grid, in_specs=[a_spec, b_spec], out_specs=o_spec,
                 scratch_shapes=[pltpu.VMEM(acc_shape, F32)], args=(a, b), rider=rider)


def _mm_col(name, a, b, out_dtype=F32, rider=None):
    m, k = a.shape
    j, _, n = b.shape
    bm, bk = _tile(m, 1024), _tile(k, 2048)
    return _mm(name, a, b, (j, m, n), out_dtype, (j, m // bm, k // bk),
               pl.BlockSpec((bm, bk), lambda jj, mm, kk: (mm, kk)),
               pl.BlockSpec((None, bk, n), lambda jj, mm, kk: (jj, kk, 0)),
               pl.BlockSpec((None, bm, n), lambda jj, mm, kk: (jj, mm, 0)), NN, 1, (bm, n), rider)


def _mm_col_da(name, do, b, rider=None):
    j, m, n = do.shape
    k = b.shape[1]
    bm, bk = _tile(m, 1024), _tile(k, 1024)
    return _mm(name, do, b, (m, k), F32, (m // bm, k // bk, j),
               pl.BlockSpec((None, bm, n), lambda mm, kk, jj: (jj, mm, 0)),
               pl.BlockSpec((None, bk, n), lambda mm, kk, jj: (jj, kk, 0)),
               pl.BlockSpec((bm, bk), lambda mm, kk, jj: (mm, kk)), NT, 1, (bm, bk), rider)


def _mm_col_db(name, a, do, out_dtype, rider=None):
    m, k = a.shape
    j, _, n = do.shape
    bm, bk = _tile(m, 2048), _tile(k, 512)
    return _mm(name, a, do, (j, k, n), out_dtype, (j, k // bk, m // bm),
               pl.BlockSpec((bm, bk), lambda jj, kk, mm: (mm, kk)),
               pl.BlockSpec((None, bm, n), lambda jj, kk, mm: (jj, mm, 0)),
               pl.BlockSpec((None, bk, n), lambda jj, kk, mm: (jj, kk, 0)), TN, 1, (bk, n), rider)


def _row_bk(kq):
    return kq if (kq % LANE or kq // LANE in (11,)) else _tile(kq, 2048)


def _mm_row(name, a, b, out_dtype=F32, rider=None):
    q, m, kq = a.shape
    n = b.shape[1]
    bm, bn, bk = _tile(m, 1024), _tile(n, 1024), _row_bk(kq)
    nk = kq // bk
    return _mm(name, a, b, (m, n), out_dtype, (m // bm, n // bn, q, nk),
               pl.BlockSpec((None, bm, bk), lambda mm, nn, qq, kk: (qq, mm, kk)),
               pl.BlockSpec((bk, bn), lambda mm, nn, qq, kk: (qq * nk + kk, nn)),
               pl.BlockSpec((bm, bn), lambda mm, nn, qq, kk: (mm, nn)), NN, 2, (bm, bn), rider)


def _mm_row_da(name, do, b, q, rider=None):
    m, n = do.shape
    kq = b.shape[0] // q
    bm, bn = _tile(m, 1024), _tile(n, 2048)
    return _mm(name, do, b, (q, m, kq), F32, (q, m // bm, n // bn),
               pl.BlockSpec((bm, bn), lambda qq, mm, nn: (mm, nn)),
               pl.BlockSpec((kq, bn), lambda qq, mm, nn: (qq, nn)),
               pl.BlockSpec((None, bm, kq), lambda qq, mm, nn: (qq, mm, 0)), NT, 1, (bm, kq), rider)


def _mm_row_db(name, a, do, out_dtype):
    q, m, kq = a.shape
    n = do.shape[1]
    bm, bn = _tile(m, 2048), _tile(n, 512)
    return _mm(name, a, do, (q * kq, n), out_dtype, (q, n // bn, m // bm),
               pl.BlockSpec((None, bm, kq), lambda qq, nn, mm: (qq, mm, 0)),
               pl.BlockSpec((bm, bn), lambda qq, nn, mm: (mm, nn)),
               pl.BlockSpec((kq, bn), lambda qq, nn, mm: (qq, nn)), TN, 1, (kq, bn))


def _row_spec(bm, d):
    return pl.BlockSpec((bm, d), lambda i: (i, 0))


def _vec_spec(d):
    return pl.BlockSpec((1, d), lambda i: (0, 0))


def _norm_mod_fwd(name, x, gain, sc, sh):
    t, d = x.shape
    bm = _tile(t, 256, SUBLANE)

    def body(x_ref, g_ref, sc_ref, sh_ref, h_ref):
        xv = x_ref[...]
        rstd = lax.rsqrt(jnp.mean(xv * xv, axis=-1, keepdims=True) + EPS)
        h_ref[...] = ((xv * rstd) * g_ref[...] * (1.0 + sc_ref[...]) + sh_ref[...]).astype(h_ref.dtype)

    return pl.pallas_call(
        body, name=name, out_shape=jax.ShapeDtypeStruct((t, d), MXU_DTYPE), grid=(t // bm,),
        in_specs=[_row_spec(bm, d), _vec_spec(d), _vec_spec(d), _vec_spec(d)], out_specs=_row_spec(bm, d),
        compiler_params=_params(1))(x, gain, sc, sh)


def _norm_mod_bwd(name, x, dh, dres, gain, sc, rider=None):
    t, d = x.shape
    bm = _tile(t, 256, SUBLANE)

    def body(x_ref, dh_ref, dres_ref, g_ref, sc_ref, dx_ref, dg_ref, dsc_ref, dsh_ref):
        @pl.when(pl.program_id(0) == 0)
        def _():
            dg_ref[...] = jnp.zeros_like(dg_ref)
            dsc_ref[...] = jnp.zeros_like(dsc_ref)
            dsh_ref[...] = jnp.zeros_like(dsh_ref)

        xv, dh_ = x_ref[...], dh_ref[...]
        rstd = lax.rsqrt(jnp.mean(xv * xv, axis=-1, keepdims=True) + EPS)
        nrm = xv * rstd
        gain_ = g_ref[...]
        dsh_ref[...] += jnp.sum(dh_, axis=0, keepdims=True)
        dsc_ref[...] += jnp.sum(dh_ * (nrm * gain_), axis=0, keepdims=True)
        dhn = dh_ * (1.0 + sc_ref[...])
        dg_ref[...] += jnp.sum(dhn * nrm, axis=0, keepdims=True)
        dn = dhn * gain_
        dx_ref[...] = dres_ref[...] + rstd * (dn - nrm * jnp.mean(dn * nrm, axis=-1, keepdims=True))

    vec = jax.ShapeDtypeStruct((1, d), F32)
    return _call(
        body, name=name, out_shape=[jax.ShapeDtypeStruct((t, d), F32), vec, vec, vec], grid=(t // bm,),
        in_specs=[_row_spec(bm, d), _row_spec(bm, d), _row_spec(bm, d), _vec_spec(d), _vec_spec(d)],
        out_specs=[_row_spec(bm, d), _vec_spec(d), _vec_spec(d), _vec_spec(d)],
        args=(x, dh, dres, gain, sc), rider=rider)


def _loss_bwd(name, x, target, gain):
    t, d = x.shape
    bm = _tile(t, 256, SUBLANE)

    def body(x_ref, t_ref, g_ref, dx_ref, loss_ref, dg_ref):
        @pl.when(pl.program_id(0) == 0)
        def _():
            loss_ref[...] = jnp.zeros_like(loss_ref)
            dg_ref[...] = jnp.zeros_like(dg_ref)

        xv = x_ref[...]
        rstd = lax.rsqrt(jnp.mean(xv * xv, axis=-1, keepdims=True) + EPS)
        nrm = xv * rstd
        gain_ = g_ref[...]
        err = nrm * gain_ - t_ref[...]
        per_tok = jnp.mean(err * err, axis=-1, keepdims=True)
        loss_ref[...] += 0.5 * jnp.sum(per_tok, axis=0, keepdims=True)
        dout = err * (1.0 / d)
        dg_ref[...] += jnp.sum(dout * nrm, axis=0, keepdims=True)
        dn = dout * gain_
        dx_ref[...] = rstd * (dn - nrm * jnp.mean(dn * nrm, axis=-1, keepdims=True))

    return pl.pallas_call(
        body, name=name,
        out_shape=[jax.ShapeDtypeStruct((t, d), F32), jax.ShapeDtypeStruct((1, 1), F32),
                   jax.ShapeDtypeStruct((1, d), F32)],
        grid=(t // bm,), in_specs=[_row_spec(bm, d), _row_spec(bm, d), _vec_spec(d)],
        out_specs=[_row_spec(bm, d), pl.BlockSpec((1, 1), lambda i: (0, 0)), _vec_spec(d)],
        compiler_params=_params(1))(x, target, gain)


def _resid(name, x, y, g):
    t, d = x.shape
    bm = _tile(t, 256, SUBLANE)

    def body(x_ref, y_ref, g_ref, o_ref):
        o_ref[...] = x_ref[...] + g_ref[...] * y_ref[...]

    return pl.pallas_call(
        body, name=name, out_shape=jax.ShapeDtypeStruct((t, d), F32), grid=(t // bm,),
        in_specs=[_row_spec(bm, d), _row_spec(bm, d), _vec_spec(d)], out_specs=_row_spec(bm, d),
        compiler_params=_params(1))(x, y, g)


def _gate_bwd(name, dx, y, g):
    t, d = dx.shape
    bm = _tile(t, 256, SUBLANE)

    def body(dx_ref, y_ref, g_ref, dy_ref, dg_ref):
        @pl.when(pl.program_id(0) == 0)
        def _():
            dg_ref[...] = jnp.zeros_like(dg_ref)

        dxv = dx_ref[...]
        dy_ref[...] = (g_ref[...] * dxv).astype(dy_ref.dtype)
        dg_ref[...] += jnp.sum(dxv * y_ref[...], axis=0, keepdims=True)

    return pl.pallas_call(
        body, name=name, out_shape=[jax.ShapeDtypeStruct((t, d), MXU_DTYPE), jax.ShapeDtypeStruct((1, d), F32)],
        grid=(t // bm,), in_specs=[_row_spec(bm, d), _row_spec(bm, d), _vec_spec(d)],
        out_specs=[_row_spec(bm, d), _vec_spec(d)], compiler_params=_params(1))(dx, y, g)


def _glu_resid_fwd(name, z, x, g):
    _, t, n = z.shape
    d = x.shape[1]
    half = N_DEV // 2
    bm = _tile(t, 256, SUBLANE)

    def body(v_ref, gt_ref, x_ref, g_ref, o_ref):
        o_ref[...] = x_ref[...] + g_ref[...] * (v_ref[...] * jax.nn.sigmoid(gt_ref[...]))

    return pl.pallas_call(
        body, name=name, out_shape=jax.ShapeDtypeStruct((t, d), F32), grid=(half, t // bm),
        in_specs=[pl.BlockSpec((None, bm, n), lambda q, i: (q, i, 0)),
                  pl.BlockSpec((None, bm, n), lambda q, i: (q + half, i, 0)),
                  pl.BlockSpec((bm, n), lambda q, i: (i, q)), pl.BlockSpec((1, n), lambda q, i: (0, q))],
        out_specs=pl.BlockSpec((bm, n), lambda q, i: (i, q)), compiler_params=_params(2))(z, z, x, g)


def _glu_resid_bwd(name, z, dx, g, rider=None):
    _, t, n = z.shape
    d = dx.shape[1]
    half = N_DEV // 2
    bm = _tile(t, 256, SUBLANE)

    def body(z_ref, dx_ref, g_ref, dz_ref, dg_ref):
        @pl.when(pl.program_id(1) == 0)
        def _():
            dg_ref[...] = jnp.zeros_like(dg_ref)

        v, dxv = z_ref[0], dx_ref[...]
        sig = jax.nn.sigmoid(z_ref[1])
        dout = g_ref[...] * dxv
        dg_ref[...] += jnp.sum(dxv * (v * sig), axis=0, keepdims=True)
        dz_ref[0] = (dout * sig).astype(dz_ref.dtype)
        dz_ref[1] = (dout * v * (sig * (1.0 - sig))).astype(dz_ref.dtype)

    pair = pl.BlockSpec((2, None, bm, n), lambda q, i: (0, q, i, 0))
    res = _call(
        body, name=name,
        out_shape=[jax.ShapeDtypeStruct((2, half, t, n), MXU_DTYPE), jax.ShapeDtypeStruct((1, d), F32)],
        grid=(half, t // bm),
        in_specs=[pair, pl.BlockSpec((bm, n), lambda q, i: (i, q)), pl.BlockSpec((1, n), lambda q, i: (0, q))],
        out_specs=[pair, pl.BlockSpec((1, n), lambda q, i: (0, q))],
        args=(z.reshape(2, half, t, n), dx, g), rider=rider)
    (dz, dg), got = res if rider is not None else (res, None)
    dz = dz.reshape(N_DEV, t, n)
    return ((dz, dg), got) if rider is not None else (dz, dg)


def _swiglu_act_fwd(name, gu):
    _, t, n = gu.shape
    half = N_DEV // 2
    bm = _tile(t, 256, SUBLANE)

    def body(g_ref, u_ref, o_ref):
        gv = g_ref[...].astype(F32)
        o_ref[...] = (gv * jax.nn.sigmoid(gv) * u_ref[...].astype(F32)).astype(o_ref.dtype)

    return pl.pallas_call(
        body, name=name, out_shape=jax.ShapeDtypeStruct((half, t, n), MXU_DTYPE), grid=(half, t // bm),
        in_specs=[pl.BlockSpec((None, bm, n), lambda q, i: (q, i, 0)),
                  pl.BlockSpec((None, bm, n), lambda q, i: (q + half, i, 0))],
        out_specs=pl.BlockSpec((None, bm, n), lambda q, i: (q, i, 0)), compiler_params=_params(2))(gu, gu)


def _swiglu_act_bwd(name, gu, dact, rider=None):
    _, t, n = gu.shape
    half = N_DEV // 2
    bm = _tile(t, 256, SUBLANE)

    def body(gu_ref, da_ref, o_ref):
        gv, da = gu_ref[0].astype(F32), da_ref[...]
        sig = jax.nn.sigmoid(gv)
        o_ref[0] = (da * gu_ref[1].astype(F32) * (sig * (1.0 + gv * (1.0 - sig)))).astype(o_ref.dtype)
        o_ref[1] = (da * (gv * sig)).astype(o_ref.dtype)

    pair = pl.BlockSpec((2, None, bm, n), lambda q, i: (0, q, i, 0))
    res = _call(
        body, name=name, out_shape=jax.ShapeDtypeStruct((2, half, t, n), MXU_DTYPE), grid=(half, t // bm),
        in_specs=[pair, pl.BlockSpec((None, bm, n), lambda q, i: (q, i, 0))], out_specs=pair,
        args=(gu.reshape(2, half, t, n), dact), rider=rider)
    if rider is None:
        return res.reshape(N_DEV, t, n)
    return res[0].reshape(N_DEV, t, n), res[1]


def _ada_fwd(name, c16, w_ada, b_loc, rider=None):
    nl, d, n = w_ada.shape
    bn = _tile(n, 512)

    def body(c_ref, w_ref, b_ref, o_ref):
        cv = c_ref[...]
        o_ref[...] = _dot(cv * jax.nn.sigmoid(cv), w_ref[...], NN) + b_ref[...]

    return _call(
        body, name=name, out_shape=jax.ShapeDtypeStruct((nl, c16.shape[0], n), F32), grid=(nl, n // bn),
        in_specs=[pl.BlockSpec(c16.shape, lambda i, j: (0, 0)), pl.BlockSpec((None, d, bn), lambda i, j: (i, 0, j)),
                  pl.BlockSpec((None, 1, bn), lambda i, j: (i, 0, j))],
        out_specs=pl.BlockSpec((None, c16.shape[0], bn), lambda i, j: (i, 0, j)),
        args=(c16, w_ada, b_loc), rider=rider)


def _adam_update(g, w, m, v):
    m = ADAM_B1 * m + (1.0 - ADAM_B1) * g
    v = ADAM_B2 * v + (1.0 - ADAM_B2) * (g * g)
    m_hat = m / (1.0 - ADAM_B1 ** ADAM_STEP)
    v_hat = v / (1.0 - ADAM_B2 ** ADAM_STEP)
    delta = -ADAM_LR * (m_hat / (jnp.sqrt(v_hat) + ADAM_EPS) + ADAM_WD * w)
    return delta, m, v


def _adamw_w_ada(name, c16, dmod16, w, m, v, rider=None):
    nl, d, n = w.shape
    br = _tile(d, 256)

    def body(c_ref, dm_ref, w_ref, m_ref, v_ref, g_ref, dl_ref, mo_ref, vo_ref):
        cv = c_ref[...]
        g = _dot(cv * jax.nn.sigmoid(cv), dm_ref[...], TN)
        g_ref[...] = g
        dl_ref[...], mo_ref[...], vo_ref[...] = _adam_update(g, w_ref[...], m_ref[...], v_ref[...])

    blk = pl.BlockSpec((None, br, n), lambda i, r: (i, r, 0))
    shp = jax.ShapeDtypeStruct(w.shape, F32)
    return _call(
        body, name=name, out_shape=[shp] * 4, grid=(nl, d // br),
        in_specs=[pl.BlockSpec((c16.shape[0], br), lambda i, r: (0, r)),
                  pl.BlockSpec((None, dmod16.shape[1], n), lambda i, r: (i, 0, 0)), blk, blk, blk],
        out_specs=[blk] * 4, args=(c16, dmod16, w, m, v), rider=rider)


def _adamw_sum(name, parts, w, m, v, rider=None):
    nl = len(parts)
    p, r, c = parts[0].shape
    br = _tile(r, max(128, (1 << 17) // max(c, LANE)), 2 * SUBLANE)
    nb = r // br

    def body(*refs):
        p_refs, (w_ref, m_ref, v_ref, g_ref, dl_ref, mo_ref, vo_ref) = refs[:nl], refs[nl:]
        for l, p_ref in enumerate(p_refs):
            @pl.when(pl.program_id(0) == l)
            def _(p_ref=p_ref):
                gl = p_ref[0].astype(F32)
                for s in range(1, p):
                    gl = gl + p_ref[s].astype(F32)
                g_ref[...] = gl

        g = g_ref[...]
        dl_ref[...], mo_ref[...], vo_ref[...] = _adam_update(g, w_ref[...], m_ref[...], v_ref[...])

    blk = pl.BlockSpec((br, c), lambda l, i: (l * nb + i, 0))
    shp = jax.ShapeDtypeStruct((nl * r, c), F32)
    return _call(
        body, name=name, out_shape=[shp] * 4, grid=(nl, nb),
        in_specs=[pl.BlockSpec((p, br, c), lambda l, i, k=k: (0, jnp.where(l == k, i, 0), 0)) for k in range(nl)]
        + [blk, blk, blk], out_specs=[blk] * 4,
        args=(*parts, w, m, v), rider=rider)


def _sum_parts(name, parts):
    p, r, c = parts.shape

    def body(p_ref, o_ref):
        g = p_ref[0]
        for s in range(1, p):
            g = g + p_ref[s]
        o_ref[...] = g

    return pl.pallas_call(body, name=name, out_shape=jax.ShapeDtypeStruct((r, c), F32))(parts)


def _s5_disc(name, lam_re, lam_im, log_dt, b_re, b_im):
    def body(lr_ref, li_ref, ld_ref, br_ref, bi_ref, ar_ref, ai_ref, bbr_ref, bbi_ref):
        lr, li = lr_ref[...], li_ref[...]
        dt = jnp.exp(ld_ref[...])
        mag = jnp.exp(lr * dt)
        a_re, a_im = mag * jnp.cos(li * dt), mag * jnp.sin(li * dt)
        nr, ni = a_re - 1.0, a_im
        den = lr * lr + li * li
        f_re, f_im = (nr * lr + ni * li) / den, (ni * lr - nr * li) / den
        br, bi = br_ref[...], bi_ref[...]
        ar_ref[...], ai_ref[...] = a_re, a_im
        bbr_ref[...] = f_re * br - f_im * bi
        bbi_ref[...] = f_re * bi + f_im * br

    s_a, s_b = jax.ShapeDtypeStruct(lam_re.shape, F32), jax.ShapeDtypeStruct(b_re.shape, F32)
    return pl.pallas_call(body, name=name, out_shape=[s_a, s_a, s_b, s_b])(lam_re, lam_im, log_dt, b_re, b_im)


def _s5_disc_bwd(name, lam_re, lam_im, log_dt, b_re, b_im, dab_re, dab_im, dbb_re, dbb_im):
    def body(lr_ref, li_ref, ld_ref, br_ref, bi_ref, dar_ref, dai_ref, dbbr_ref, dbbi_ref,
             dlr_ref, dli_ref, dld_ref, dbr_ref, dbi_ref):
        lr, li = lr_ref[...], li_ref[...]
        dt = jnp.exp(ld_ref[...])
        mag = jnp.exp(lr * dt)
        a_re, a_im = mag * jnp.cos(li * dt), mag * jnp.sin(li * dt)
        nr, ni = a_re - 1.0, a_im
        den = lr * lr + li * li
        f_re, f_im = (nr * lr + ni * li) / den, (ni * lr - nr * li) / den
        br, bi = br_ref[...], bi_ref[...]
        dbbr, dbbi = dbbr_ref[...], dbbi_ref[...]
        dbr_ref[...] = f_re * dbbr + f_im * dbbi
        dbi_ref[...] = f_re * dbbi - f_im * dbbr
        df_re = jnp.sum(dbbr * br + dbbi * bi, axis=1, keepdims=True)
        df_im = jnp.sum(dbbi * br - dbbr * bi, axis=1, keepdims=True)
        dnr = (df_re * lr - df_im * li) / den
        dni = (df_re * li + df_im * lr) / den
        dden = -(df_re * f_re + df_im * f_im) / den
        dlr = (df_re * nr + df_im * ni) / den + 2.0 * lr * dden
        dli = (df_re * ni - df_im * nr) / den + 2.0 * li * dden
        da_re, da_im = dar_ref[...] + dnr, dai_ref[...] + dni
        dmag_mag = da_re * a_re + da_im * a_im
        dth = da_im * a_re - da_re * a_im
        dlr_ref[...] = dlr + dmag_mag * dt
        dli_ref[...] = dli + dth * dt
        ddt = jnp.sum(dmag_mag * lr + dth * li, axis=2, keepdims=True)
        dld_ref[...] = ddt * dt

    s_a, s_b = jax.ShapeDtypeStruct(lam_re.shape, F32), jax.ShapeDtypeStruct(b_re.shape, F32)
    return pl.pallas_call(
        body, name=name, out_shape=[s_a, s_a, jax.ShapeDtypeStruct(log_dt.shape, F32), s_b, s_b],
    )(lam_re, lam_im, log_dt, b_re, b_im, dab_re, dab_im, dbb_re, dbb_im)


def _s5_time_block(t):
    return _tile(t, 128, SUBLANE)


def _s5_scan_fwd(name, u, bb_re, bb_im, ab_re, ab_im, cc_re, cc_im, dskip, rider=None):
    t, d = u.shape
    nsg, cs, ns = bb_re.shape
    tb = _s5_time_block(t)

    def body(u_ref, bbr_hbm, bbi_hbm, ar_ref, ai_ref, ccr_hbm, cci_hbm, d_ref, sr_ref, si_ref, srm_ref, sim_ref, yp_ref,
             ya_ref, bbr, bbi, ccr, cci, cr_ref, ci_ref):
        @pl.when(pl.program_id(0) == 0)
        def _():
            pltpu.sync_copy(bbr_hbm, bbr)
            pltpu.sync_copy(bbi_hbm, bbi)
            pltpu.sync_copy(ccr_hbm, ccr)
            pltpu.sync_copy(cci_hbm, cci)
            cr_ref[...] = jnp.zeros_like(cr_ref)
            ci_ref[...] = jnp.zeros_like(ci_ref)

        for sg in range(nsg):
            us = u_ref[:, sg * cs:(sg + 1) * cs]
            sr_ref[:, sg, :] = _dot(us, bbr[sg], NN)
            si_ref[:, sg, :] = _dot(us, bbi[sg], NN)
        ar, ai = ar_ref[...], ai_ref[...]

        def step(i, carry):
            cr, ci = carry
            nr = ar * cr - ai * ci + sr_ref[i]
            ni = ar * ci + ai * cr + si_ref[i]
            sr_ref[i] = nr
            si_ref[i] = ni
            return nr, ni

        cr, ci = lax.fori_loop(0, tb, step, (cr_ref[...], ci_ref[...]), unroll=2)
        cr_ref[...], ci_ref[...] = cr, ci
        srm_ref[...] = jnp.swapaxes(sr_ref[...], 0, 1).astype(MXU_DTYPE)
        sim_ref[...] = jnp.swapaxes(si_ref[...], 0, 1).astype(MXU_DTYPE)
        for sg in range(nsg):
            cols = slice(sg * cs, (sg + 1) * cs)
            y = _dot(srm_ref[sg], ccr[sg], NN) - _dot(sim_ref[sg], cci[sg], NN) + d_ref[:, cols] * u_ref[:, cols]
            yp_ref[:, cols] = y
            ya_ref[:, cols] = _gelu(y).astype(ya_ref.dtype)

    scan = jax.ShapeDtypeStruct((t, nsg, ns), F32)
    mxu = jax.ShapeDtypeStruct((nsg, t, ns), MXU_DTYPE)
    hbm = pl.BlockSpec(memory_space=pltpu.HBM)
    full = pl.BlockSpec((nsg, ns), lambda i: (0, 0))
    return _call(
        body, name=name,
        out_shape=[scan, scan, mxu, mxu, jax.ShapeDtypeStruct((t, d), F32), jax.ShapeDtypeStruct((t, d), MXU_DTYPE)],
        grid=(t // tb,), in_specs=[_row_spec(tb, d), hbm, hbm, full, full, hbm, hbm, _vec_spec(d)],
        out_specs=[pl.BlockSpec((tb, nsg, ns), lambda i: (i, 0, 0))] * 2 + [pl.BlockSpec((nsg, tb, ns), lambda i: (0, i, 0))] * 2
        + [_row_spec(tb, d)] * 2,
        scratch_shapes=[pltpu.VMEM(bb_re.shape, bb_re.dtype), pltpu.VMEM(bb_im.shape, bb_im.dtype),
                        pltpu.VMEM(cc_re.shape, cc_re.dtype), pltpu.VMEM(cc_im.shape, cc_im.dtype),
                        pltpu.VMEM((nsg, ns), F32), pltpu.VMEM((nsg, ns), F32)],
        args=(u, bb_re, bb_im, ab_re, ab_im, cc_re, cc_im, dskip), rider=rider)


def _s5_scan_bwd(name, dyact, ypre, cc_re, cc_im, ab_re, ab_im, s_re, s_im, rider=None):
    t, d = dyact.shape
    nsg, ns, cs = cc_re.shape
    tb = _s5_time_block(t)
    nb = t // tb

    def body(dya_ref, yp_ref, ccr_hbm, cci_hbm, ar_ref, ai_ref, sr_ref, si_ref, dy_ref, lrm_ref, lim_ref, dar_ref, dai_ref,
             ccr, cci, lr_ref, li_ref, cr_ref, ci_ref):
        dy_ref[...] = (dya_ref[...] * _gelu_and_grad(yp_ref[...])[1]).astype(dy_ref.dtype)

        @pl.when(pl.program_id(0) == 0)
        def _():
            pltpu.sync_copy(ccr_hbm, ccr)
            pltpu.sync_copy(cci_hbm, cci)
            cr_ref[...] = jnp.zeros_like(cr_ref)
            ci_ref[...] = jnp.zeros_like(ci_ref)
            dar_ref[...] = jnp.zeros_like(dar_ref)
            dai_ref[...] = jnp.zeros_like(dai_ref)

        for sg in range(nsg):
            dys = dy_ref[:, sg * cs:(sg + 1) * cs]
            lr_ref[:, sg, :] = _dot(dys, ccr[sg], NT)
            li_ref[:, sg, :] = -_dot(dys, cci[sg], NT)
        ar, ai = ar_ref[...], ai_ref[...]

        def step(i, carry):
            cr, ci, dar, dai = carry
            j = tb - 1 - i
            sr, si = sr_ref[j], si_ref[j]
            dar = dar + (cr * sr + ci * si)
            dai = dai + (ci * sr - cr * si)
            nr = lr_ref[j] + (ar * cr + ai * ci)
            ni = li_ref[j] + (ar * ci - ai * cr)
            lr_ref[j] = nr
            li_ref[j] = ni
            return nr, ni, dar, dai

        cr, ci, dar, dai = lax.fori_loop(0, tb, step, (cr_ref[...], ci_ref[...], dar_ref[...], dai_ref[...]))
        cr_ref[...], ci_ref[...] = cr, ci
        dar_ref[...], dai_ref[...] = dar, dai
        lrm_ref[...] = jnp.swapaxes(lr_ref[...], 0, 1).astype(MXU_DTYPE)
        lim_ref[...] = jnp.swapaxes(li_ref[...], 0, 1).astype(MXU_DTYPE)

    hbm = pl.BlockSpec(memory_space=pltpu.HBM)
    full = pl.BlockSpec((nsg, ns), lambda i: (0, 0))
    mxu = jax.ShapeDtypeStruct((nsg, t, ns), MXU_DTYPE)
    acc = jax.ShapeDtypeStruct((nsg, ns), F32)
    scan_spec = pl.BlockSpec((tb, nsg, ns), lambda i: (nb - 1 - i, 0, 0))
    rows = pl.BlockSpec((tb, d), lambda i: (nb - 1 - i, 0))
    return _call(
        body, name=name, out_shape=[jax.ShapeDtypeStruct((t, d), MXU_DTYPE), mxu, mxu, acc, acc], grid=(nb,),
        in_specs=[rows, rows, hbm, hbm, full, full, scan_spec, scan_spec],
        out_specs=[rows] + [pl.BlockSpec((nsg, tb, ns), lambda i: (0, nb - 1 - i, 0))] * 2 + [full, full],
        scratch_shapes=[pltpu.VMEM(cc_re.shape, cc_re.dtype), pltpu.VMEM(cc_im.shape, cc_im.dtype),
                        pltpu.VMEM((tb, nsg, ns), F32), pltpu.VMEM((tb, nsg, ns), F32),
                        pltpu.VMEM((nsg, ns), F32), pltpu.VMEM((nsg, ns), F32)],
        args=(dyact, ypre, cc_re, cc_im, ab_re, ab_im, s_re, s_im), rider=rider)


def _s5_grads(name, lam_re, lam_im, s_re, s_im, u, dyp, bb_re, bb_im, dskip, rider=None):
    nsg, t, ns = lam_re.shape
    d = u.shape[1]
    cs = bb_re.shape[1]
    tb = _tile(t, 512, SUBLANE)

    def body(lr_ref, li_ref, sr_ref, si_ref, u_ref, dy_ref, bbr_ref, bbi_ref, d_ref,
             du_ref, dbbr_ref, dbbi_ref, dccr_ref, dcci_ref, dd_ref):
        @pl.when(pl.program_id(1) == 0)
        def _():
            for r in (dbbr_ref, dbbi_ref, dccr_ref, dcci_ref, dd_ref):
                r[...] = jnp.zeros_like(r)

        lr, li, uv, dy = lr_ref[...], li_ref[...], u_ref[...], dy_ref[...]
        dyf = dy.astype(F32)
        du_ref[...] = _dot(lr, bbr_ref[...], NT) + _dot(li, bbi_ref[...], NT) + d_ref[...] * dyf
        dbbr_ref[...] += _dot(uv, lr, TN)
        dbbi_ref[...] += _dot(uv, li, TN)
        dccr_ref[...] += _dot(sr_ref[...], dy, TN)
        dcci_ref[...] -= _dot(si_ref[...], dy, TN)
        dd_ref[...] += jnp.sum(dyf * uv, axis=0, keepdims=True)

    s_spec = pl.BlockSpec((None, tb, ns), lambda sg, i: (sg, i, 0))
    col = pl.BlockSpec((tb, cs), lambda sg, i: (i, sg))
    b_spec = pl.BlockSpec((None, cs, ns), lambda sg, i: (sg, 0, 0))
    c_spec = pl.BlockSpec((None, ns, cs), lambda sg, i: (sg, 0, 0))
    vec = pl.BlockSpec((1, cs), lambda sg, i: (0, sg))
    return _call(
        body, name=name,
        out_shape=[jax.ShapeDtypeStruct((t, d), F32), jax.ShapeDtypeStruct(bb_re.shape, F32),
                   jax.ShapeDtypeStruct(bb_re.shape, F32), jax.ShapeDtypeStruct((nsg, ns, cs), F32),
                   jax.ShapeDtypeStruct((nsg, ns, cs), F32), jax.ShapeDtypeStruct((1, d), F32)],
        grid=(nsg, t // tb), in_specs=[s_spec, s_spec, s_spec, s_spec, col, col, b_spec, b_spec, vec],
        out_specs=[col, b_spec, b_spec, c_spec, c_spec, vec],
        args=(lam_re, lam_im, s_re, s_im, u, dyp, bb_re, bb_im, dskip), rider=rider)


def _shift_down(x, k, prev8):
    if k == 0:
        return x
    ext = jnp.concatenate([prev8, x], axis=0)
    return ext[SUBLANE - k:SUBLANE - k + x.shape[0]]


def _shift_up(x, k, next8):
    if k == 0:
        return x
    ext = jnp.concatenate([x, next8], axis=0)
    return ext[k:k + x.shape[0]]


def _lru_time_block(t):
    return _tile(t, 256, SUBLANE)


def _lru_gates(xp, prev8, cv_ref, wrg, wig):
    taps = cv_ref.shape[0] - 4
    row = lambda k: cv_ref[k:k + 1, :]
    xs = [_shift_down(xp, taps - 1 - k, prev8) for k in range(taps)]
    xb = row(taps)
    for k in range(taps):
        xb = xb + row(k) * xs[k]
    r = jax.nn.sigmoid(_dot(xb, wrg, NN) + row(taps + 1))
    ig = jax.nn.sigmoid(_dot(xb, wig, NN) + row(taps + 2))
    sp = jax.nn.softplus(-row(taps + 3))
    log_a = -LRU_C * r * sp
    a = jnp.exp(log_a)
    mult = jnp.sqrt(_neg_expm1(2.0 * log_a))
    return xs, xb, r, ig, sp, a, mult


def _lru_fwd(name, zz, cvec, wrg, wig, rider=None):
    _, t, w = zz.shape
    half = N_DEV // 2
    tb = _lru_time_block(t)

    def body(gb_ref, xp_ref, xprev_ref, cv_ref, wrg_ref, wig_ref, hs_ref, y_ref, a_scr, b_scr, carry):
        i = pl.program_id(1)

        @pl.when(i == 0)
        def _():
            carry[...] = jnp.zeros_like(carry)

        prev8 = jnp.where(i > 0, xprev_ref[...], 0.0)
        _, xb, _, ig, _, a, mult = _lru_gates(xp_ref[...], prev8, cv_ref, wrg_ref[...], wig_ref[...])
        a_scr[...] = a
        b_scr[...] = mult * (ig * xb)

        def step(j, h):
            h = a_scr[pl.ds(j, 1), :] * h + b_scr[pl.ds(j, 1), :]
            hs_ref[pl.ds(j, 1), :] = h
            return h

        carry[0:1, :] = lax.fori_loop(0, tb, step, carry[0:1, :], unroll=8)
        y_ref[...] = (hs_ref[...] * _gelu(gb_ref[...])).astype(y_ref.dtype)

    nrow = tb // SUBLANE
    blk = lambda off: pl.BlockSpec((None, tb, w), lambda q, i: (q + off, i, 0))
    return _call(
        body, name=name,
        out_shape=[jax.ShapeDtypeStruct((half, t, w), F32), jax.ShapeDtypeStruct((half, t, w), MXU_DTYPE)],
        grid=(half, t // tb),
        in_specs=[blk(0), blk(half),
                  pl.BlockSpec((None, SUBLANE, w), lambda q, i: (q + half, jnp.maximum(i * nrow - 1, 0), 0)),
                  pl.BlockSpec((None,) + cvec.shape[1:], lambda q, i: (q, 0, 0)),
                  pl.BlockSpec((None, w, w), lambda q, i: (q, 0, 0)), pl.BlockSpec((None, w, w), lambda q, i: (q, 0, 0))],
        out_specs=[blk(0), blk(0)],
        scratch_shapes=[pltpu.VMEM((tb, w), F32), pltpu.VMEM((tb, w), F32), pltpu.VMEM((SUBLANE, w), F32)],
        args=(zz, zz, zz, cvec, wrg, wig), rider=rider)


def _lru_bwd(name, zz, hs, dy, cvec, wrg, wig, rider=None):
    _, t, w = zz.shape
    half = N_DEV // 2
    tb = _lru_time_block(t)
    nb = t // tb
    taps = cvec.shape[1] - 4

    def body(gb_ref, xp_ref, xprev_ref, hs_ref, hprev_ref, dy_ref, cv_ref, wrg_ref, wig_ref,
             dgb_ref, dxp_ref, dcv_ref, dwrg_ref, dwig_ref, a_scr, l_scr, carry, dxb_next):
        i = pl.program_id(1)

        @pl.when(i == 0)
        def _():
            for r_ in (carry, dxb_next, dcv_ref, dwrg_ref, dwig_ref):
                r_[...] = jnp.zeros_like(r_)

        has_prev = i < nb - 1
        row = lambda k: cv_ref[k:k + 1, :]
        prev8 = jnp.where(has_prev, xprev_ref[...], 0.0)
        xs, xb, r, ig, sp, a, mult = _lru_gates(xp_ref[...], prev8, cv_ref, wrg_ref[...], wig_ref[...])
        hs_ = hs_ref[...]
        hs_m1 = _shift_down(hs_, 1, jnp.where(has_prev, hprev_ref[...], 0.0))
        gel, dgel = _gelu_and_grad(gb_ref[...])
        dy_ = dy_ref[...]
        dgb_ref[...] = (dy_ * hs_ * dgel).astype(dgb_ref.dtype)
        a_scr[...] = a
        l_scr[...] = dy_ * gel

        def step(k, c):
            j = tb - 1 - k
            lam = l_scr[pl.ds(j, 1), :] + c
            l_scr[pl.ds(j, 1), :] = lam
            return a_scr[pl.ds(j, 1), :] * lam

        carry[0:1, :] = lax.fori_loop(0, tb, step, carry[0:1, :], unroll=8)
        lam = l_scr[...]
        dmult = lam * (ig * xb)
        dig = lam * (mult * xb)
        dxb = lam * (mult * ig)
        dlog_a = (lam * hs_m1) * a - dmult * (a * a) / mult
        dr = dlog_a * (-LRU_C * sp)
        dsp = jnp.sum(dlog_a * (-LRU_C * r), axis=0, keepdims=True)
        dpr = dr * (r * (1.0 - r))
        dpi = dig * (ig * (1.0 - ig))
        dwrg_ref[...] += _dot(xb, dpr, TN)
        dwig_ref[...] += _dot(xb, dpi, TN)
        dxb = dxb + _dot(dpr, wrg_ref[...], NT) + _dot(dpi, wig_ref[...], NT)
        for k in range(taps):
            dcv_ref[k:k + 1, :] += jnp.sum(dxb * xs[k], axis=0, keepdims=True)
        dcv_ref[taps:taps + 1, :] += jnp.sum(dxb, axis=0, keepdims=True)
        dcv_ref[taps + 1:taps + 2, :] += jnp.sum(dpr, axis=0, keepdims=True)
        dcv_ref[taps + 2:taps + 3, :] += jnp.sum(dpi, axis=0, keepdims=True)
        dcv_ref[taps + 3:taps + 4, :] += dsp * (-jax.nn.sigmoid(-row(taps + 3)))
        nxt8 = dxb_next[...]
        dxp = row(taps - 1) * dxb
        for k in range(taps - 1):
            dxp = dxp + row(k) * _shift_up(dxb, taps - 1 - k, nxt8)
        dxp_ref[...] = dxp.astype(dxp_ref.dtype)
        dxb_next[...] = dxb[0:SUBLANE]

    nrow = tb // SUBLANE
    blk = lambda off: pl.BlockSpec((None, tb, w), lambda q, i: (q + off, nb - 1 - i, 0))
    halo = lambda off: pl.BlockSpec((None, SUBLANE, w), lambda q, i: (q + off, jnp.maximum((nb - 1 - i) * nrow - 1, 0), 0))
    wspec = pl.BlockSpec((None, w, w), lambda q, i: (q, 0, 0))
    cspec = pl.BlockSpec((None,) + cvec.shape[1:], lambda q, i: (q, 0, 0))
    act = jax.ShapeDtypeStruct((half, t, w), MXU_DTYPE)
    return _call(
        body, name=name,
        out_shape=[act, act, jax.ShapeDtypeStruct(cvec.shape, F32), jax.ShapeDtypeStruct(wrg.shape, F32),
                   jax.ShapeDtypeStruct(wig.shape, F32)],
        grid=(half, nb),
        in_specs=[blk(0), blk(half), halo(half), blk(0), halo(0), blk(0), cspec, wspec, wspec],
        out_specs=[blk(0), blk(0), cspec, wspec, wspec],
        scratch_shapes=[pltpu.VMEM((tb, w), F32), pltpu.VMEM((tb, w), F32), pltpu.VMEM((SUBLANE, w), F32),
                        pltpu.VMEM((SUBLANE, w), F32)],
        args=(zz, zz, zz, hs, hs, dy, cvec, wrg, wig), rider=rider)


def _band(name, blocks, per, dtype):
    n, a, b = blocks.shape

    def body(x_ref, o_ref):
        o_ref[...] = jnp.zeros_like(o_ref)
        for g in range(per):
            o_ref[g * a:(g + 1) * a, g * b:(g + 1) * b] = x_ref[g].astype(o_ref.dtype)

    return pl.pallas_call(
        body, name=name, out_shape=jax.ShapeDtypeStruct((n // per, per * a, per * b), dtype), grid=(n // per,),
        in_specs=[pl.BlockSpec((per, a, b), lambda s: (s, 0, 0))],
        out_specs=pl.BlockSpec((None, per * a, per * b), lambda s: (s, 0, 0)), compiler_params=_params(1))(blocks)


def _unband(name, bands, per):
    s, pa, pb = bands.shape
    a, b = pa // per, pb // per

    def body(x_ref, o_ref):
        for g in range(per):
            o_ref[g] = x_ref[g * a:(g + 1) * a, g * b:(g + 1) * b]

    return pl.pallas_call(
        body, name=name, out_shape=jax.ShapeDtypeStruct((s * per, a, b), bands.dtype), grid=(s,),
        in_specs=[pl.BlockSpec((None, pa, pb), lambda i: (i, 0, 0))],
        out_specs=pl.BlockSpec((per, a, b), lambda i: (i, 0, 0)), compiler_params=_params(1))(bands)


def _pack(arrays, rows_multiple, lanes=LANE):
    flat = [a.reshape(-1).astype(F32) for a in arrays]
    size = sum(a.shape[0] for a in flat)
    rows = -(-size // (lanes * rows_multiple)) * rows_multiple
    if rows * lanes > size:
        flat.append(jnp.zeros((rows * lanes - size,), F32))
    return jnp.concatenate(flat).reshape(rows, lanes)


def _unpack(packed, shapes):
    flat = packed.reshape(-1)
    out, off = [], 0
    for s in shapes:
        n = math.prod(s)
        out.append(flat[off:off + n].reshape(s))
        off += n
    return out


def kernel(x, c, norm_g, w_ada, b_ada, s5_w_in, s5_lam_re, s5_lam_im, s5_log_dt, s5_b_re, s5_b_im, s5_c_re, s5_c_im, s5_d, s5_w_glu, lru_w_in, lru_conv_w, lru_conv_b, lru_w_rg, lru_b_rg, lru_w_ig, lru_b_ig, lru_lam, lru_w_out, ffn_w_gu, ffn_w_down, final_g, loss_target, m_norm_g, m_w_ada, m_b_ada, m_s5_w_in, m_s5_lam_re, m_s5_lam_im, m_s5_log_dt, m_s5_b_re, m_s5_b_im, m_s5_c_re, m_s5_c_im, m_s5_d, m_s5_w_glu, m_lru_w_in, m_lru_conv_w, m_lru_conv_b, m_lru_w_rg, m_lru_b_rg, m_lru_w_ig, m_lru_b_ig, m_lru_lam, m_lru_w_out, m_ffn_w_gu, m_ffn_w_down, m_final_g, v_norm_g, v_w_ada, v_b_ada, v_s5_w_in, v_s5_lam_re, v_s5_lam_im, v_s5_log_dt, v_s5_b_re, v_s5_b_im, v_s5_c_re, v_s5_c_im, v_s5_d, v_s5_w_glu, v_lru_w_in, v_lru_conv_w, v_lru_conv_b, v_lru_w_rg, v_lru_b_rg, v_lru_w_ig, v_lru_b_ig, v_lru_lam, v_lru_w_out, v_ffn_w_gu, v_ffn_w_down, v_final_g):
    wv = dict(zip(WEIGHTS, (norm_g, w_ada, b_ada, s5_w_in, s5_lam_re, s5_lam_im, s5_log_dt, s5_b_re, s5_b_im, s5_c_re, s5_c_im, s5_d, s5_w_glu, lru_w_in, lru_conv_w, lru_conv_b, lru_w_rg, lru_b_rg, lru_w_ig, lru_b_ig, lru_lam, lru_w_out, ffn_w_gu, ffn_w_down, final_g)))
    mv = dict(zip(WEIGHTS, (m_norm_g, m_w_ada, m_b_ada, m_s5_w_in, m_s5_lam_re, m_s5_lam_im, m_s5_log_dt, m_s5_b_re, m_s5_b_im, m_s5_c_re, m_s5_c_im, m_s5_d, m_s5_w_glu, m_lru_w_in, m_lru_conv_w, m_lru_conv_b, m_lru_w_rg, m_lru_b_rg, m_lru_w_ig, m_lru_b_ig, m_lru_lam, m_lru_w_out, m_ffn_w_gu, m_ffn_w_down, m_final_g)))
    vv = dict(zip(WEIGHTS, (v_norm_g, v_w_ada, v_b_ada, v_s5_w_in, v_s5_lam_re, v_s5_lam_im, v_s5_log_dt, v_s5_b_re, v_s5_b_im, v_s5_c_re, v_s5_c_im, v_s5_d, v_s5_w_glu, v_lru_w_in, v_lru_conv_w, v_lru_conv_b, v_lru_w_rg, v_lru_b_rg, v_lru_w_ig, v_lru_b_ig, v_lru_lam, v_lru_w_out, v_ffn_w_gu, v_ffn_w_down, v_final_g)))

    me = 4 * lax.axis_index("x") + 2 * lax.axis_index("y") + lax.axis_index("c")
    x0 = x[0]
    tgt = loss_target[0]
    t, d = x0.shape
    depth = norm_g.shape[0]
    n_mod = w_ada.shape[2] * N_DEV // d
    groups, states = s5_lam_re.shape[1], s5_lam_re.shape[2]
    per_sg = S5_SUPER // S5_GROUP
    nsg = groups // per_sg
    lw = lru_lam.shape[1] * N_DEV
    lwc = lw // (N_DEV // 2)
    half = N_DEV // 2

    assert depth == 2, "the ride schedule below is written for one S5 layer followed by one RG-LRU layer"
    wire = lambda a: a.astype(WIRE_DTYPE)
    gw = {}

    def riding(job, fn, *args):
        res, (got,) = fn(*args, rider=_gather_rider([wire(job[1])]))
        gw[job[0]] = got
        return res

    sh_shapes = [wv[n].shape for n in SMALL_SHARDED] + [c.shape]
    sh_all = _all_gather("ag_small", _pack([wv[n] for n in SMALL_SHARDED] + [c], SUBLANE))
    sh_parts = [jnp.stack(p) for p in zip(*[_unpack(sh_all[s], sh_shapes) for s in range(N_DEV)])]
    full = {}
    for n, p in zip(SMALL_SHARDED, sh_parts[:-1]):
        full[n] = jnp.moveaxis(p, 0, -2).reshape(p.shape[1:-1] + (-1,))
    c_all = sh_parts[-1].reshape(N_DEV, d)
    c16 = jnp.pad(c_all, ((0, 2 * SUBLANE - N_DEV), (0, 0)))

    n_loc = w_ada.shape[2]
    b_loc = lax.dynamic_slice_in_dim(b_ada, me * n_loc, n_loc, axis=1)[:, None, :]
    mod_part = riding(('s5_in', s5_w_in[0]), _ada_fwd, "ada_fwd", c16, w_ada, b_loc)[:, :N_DEV]
    mod_mine = _chunk_exchange("x_mod", [mod_part.transpose(1, 0, 2)], ALL)
    mod = mod_mine.transpose(1, 0, 2).reshape(depth, n_mod, 1, d)

    lam3 = lambda a: a[0][:, None, :]
    p_lr, p_li, p_ld = lam3(s5_lam_re), lam3(s5_lam_im), s5_log_dt[0][:, None, None]
    p_br, p_bi = s5_b_re[0].transpose(0, 2, 1), s5_b_im[0].transpose(0, 2, 1)
    ab_re3, ab_im3, bb_re3, bb_im3 = _s5_disc("s5_disc", p_lr, p_li, p_ld, p_br, p_bi)
    ab_re, ab_im = ab_re3.reshape(nsg, per_sg * states), ab_im3.reshape(nsg, per_sg * states)
    bb_re = _band("band_bb_re", bb_re3, per_sg, MXU_DTYPE)
    bb_im = _band("band_bb_im", bb_im3, per_sg, MXU_DTYPE)
    cc_re = _band("band_cc_re", s5_c_re[0].transpose(0, 2, 1), per_sg, MXU_DTYPE)
    cc_im = _band("band_cc_im", s5_c_im[0].transpose(0, 2, 1), per_sg, MXU_DTYPE)

    taps = lru_conv_w.shape[1]
    cvec = jnp.concatenate([full['lru_conv_w'].reshape(taps, lw), full['lru_conv_b'], full['lru_b_rg'],
                            full['lru_b_ig'], full['lru_lam']], axis=0)
    cvec = cvec.reshape(taps + 4, half, lwc).transpose(1, 0, 2)
    wrg = _band("band_w_rg", lru_w_rg[0], LRU_BLOCKS_PER_CHUNK, MXU_DTYPE)
    wig = _band("band_w_ig", lru_w_ig[0], LRU_BLOCKS_PER_CHUNK, MXU_DTYPE)

    saved = []
    xc = x0
    for i in range(depth):
        sh1, sc1, g1, sh2, sc2, g2 = [mod[i, k] for k in range(n_mod)]
        gn = full['norm_g'][i]
        h1 = _norm_mod_fwd(f"norm1_fwd{i}", xc, gn[0:1], sc1, sh1)
        if i % 2 == 0:
            u = riding(('s5_glu', s5_w_glu[0]), _mm_row, f"s5_in{i}", h1[None], gw['s5_in'].reshape(d, d))
            s_re, s_im, s_rem, s_imm, ypre, yact = riding((('gu', i), ffn_w_gu[i]), _s5_scan_fwd, f"s5_scan{i}", u, bb_re,
                                                          bb_im, ab_re, ab_im, cc_re, cc_im, s5_d)
            z = riding((('down', i), ffn_w_down[i]), _mm_col, f"s5_glu{i}", yact, gw['s5_glu'])
            x1 = _glu_resid_fwd(f"s5_resid{i}", z, xc, g1)
            mix = (u, s_re, s_im, s_rem, s_imm, ypre, yact, z)
        else:
            zz = _mm_col(f"lru_in{i}", h1, gw['lru_in'])
            hs, ylru = riding((('gu', i), ffn_w_gu[i]), _lru_fwd, f"lru_core{i}", zz, cvec, wrg, wig)
            o = _mm_row(f"lru_out{i}", ylru, gw['lru_out'].reshape(lw, d))
            x1 = _resid(f"lru_resid{i}", xc, o, g1)
            mix = (zz, hs, ylru, o)
        h2 = _norm_mod_fwd(f"norm2_fwd{i}", x1, gn[1:2], sc2, sh2)
        if i % 2 == 0:
            gu = riding(('lru_in', lru_w_in[0]), _mm_col, f"ffn_gu{i}", h2, gw['gu', i], MXU_DTYPE)
            act = _swiglu_act_fwd(f"ffn_act{i}", gu)
            f = riding(('lru_out', lru_w_out[0]), _mm_row, f"ffn_down{i}", act, gw['down', i].reshape(-1, d))
        else:
            gu = riding((('down', i), ffn_w_down[i]), _mm_col, f"ffn_gu{i}", h2, gw['gu', i], MXU_DTYPE)
            act = _swiglu_act_fwd(f"ffn_act{i}", gu)
            f = _mm_row(f"ffn_down{i}", act, gw['down', i].reshape(-1, d))
        x2 = _resid(f"ffn_resid{i}", x1, f, g2)
        saved.append((xc, h1, mix, x1, h2, gu, act, f))
        xc = x2

    dx, loss_part, d_final_g = _loss_bwd("loss", xc, tgt, final_g[None])
    loss = lax.psum(loss_part[0, 0], ("x", "y", "c"))

    grads = {}
    parts = {}
    dmod = [None] * depth
    d_norm_g = [None] * depth
    core = lax.axis_index("c").astype(jnp.int32).reshape(1)
    chunked = lambda p: p.reshape(N_DEV, -1, p.shape[-1])
    to_sibling = lambda p: _sibling_rider(chunked(p))
    pair = lambda name, p, got: _pair_sum(name, chunked(p), got, core)
    over_ici = lambda sums: _chunk_rider([sums], SAME_CORE)
    quarter_over_ici = lambda sums, q: _chunk_rider([sums], SAME_CORE, rows=(q * (sums.shape[1] // 4), sums.shape[1] // 4))

    above = None
    for i in reversed(range(depth)):
        xin, h1, mix, x1, h2, gu, act, f = saved[i]
        sh1, sc1, g1, sh2, sc2, g2 = [mod[i, k] for k in range(n_mod)]
        gn = full['norm_g'][i]
        g_down = gw['down', i].reshape(-1, d)
        df, dg2 = _gate_bwd(f"ffn_gate_bwd{i}", dx, f, g2)
        if above is None:
            dact = _mm_row_da(f"ffn_down_da{i}", df, g_down, half)
        else:
            dact, (got,) = _ride(_mm_row_da, f"ffn_down_da{i}", df, g_down, half, riders=[to_sibling(above[1])])
            s_above = pair(f"x_{above[0][0]}_pair", above[1], got)
        p_down = _mm_row_db(f"ffn_down_db{i}", act, df, WIRE_DTYPE)
        dgu, (got,) = _ride(_swiglu_act_bwd, f"ffn_act_bwd{i}", gu, dact, riders=[to_sibling(p_down)])
        s_down = pair(f"x_ffn_w_down{i}_pair", p_down, got)
        if above is None:
            dh2, (parts['ffn_w_down', i],) = _ride(_mm_col_da, f"ffn_gu_da{i}", dgu, gw['gu', i], riders=[over_ici(s_down)])
            p_gu = _mm_col_db(f"ffn_gu_db{i}", h2, dgu, WIRE_DTYPE)
        else:
            dh2, (parts[above[0]],) = _ride(_mm_col_da, f"ffn_gu_da{i}", dgu, gw['gu', i], riders=[over_ici(s_above)])
            p_gu, (parts['ffn_w_down', i],) = _ride(_mm_col_db, f"ffn_gu_db{i}", h2, dgu, WIRE_DTYPE,
                                                    riders=[over_ici(s_down)])
        dx, dgn2, dsc2, dsh2 = _norm_mod_bwd(f"norm2_bwd{i}", x1, dh2, dx, gn[1:2], sc2)
        if i % 2 == 0:
            u, s_re, s_im, s_rem, s_imm, ypre, yact, z = mix
            (dz, dg1), (got,) = _ride(_glu_resid_bwd, f"s5_resid_bwd{i}", z, dx, g1, riders=[to_sibling(p_gu)])
            s_gu = pair(f"x_ffn_w_gu{i}_pair", p_gu, got)
            dyact, (gu_0,) = _ride(_mm_col_da, f"s5_glu_da{i}", dz, gw['s5_glu'], riders=[quarter_over_ici(s_gu, 0)])
            p_glu, (gu_1,) = _ride(_mm_col_db, f"s5_glu_db{i}", yact, dz, WIRE_DTYPE, riders=[quarter_over_ici(s_gu, 1)])
            (dyp, l_rem, l_imm, dab_re, dab_im), (gu_2, gu_3, got) = _ride(
                _s5_scan_bwd, f"s5_scan_bwd{i}", dyact, ypre, cc_re, cc_im, ab_re, ab_im, s_re, s_im,
                riders=[quarter_over_ici(s_gu, 2), quarter_over_ici(s_gu, 3), to_sibling(p_glu)])
            parts['ffn_w_gu', i] = [gu_0, gu_1, gu_2, gu_3]
            s_glu = pair("x_s5_w_glu_pair", p_glu, got)
            (du, dbb_re, dbb_im, dcc_re, dcc_im, dd), (parts['s5_w_glu', 0],) = _ride(
                _s5_grads, f"s5_grads{i}", l_rem, l_imm, s_rem, s_imm, u, dyp, bb_re, bb_im, s5_d, riders=[over_ici(s_glu)])
            dlr, dli, dld, dbr, dbi = _s5_disc_bwd(
                "s5_disc_bwd", p_lr, p_li, p_ld, p_br, p_bi, dab_re.reshape(groups, 1, states),
                dab_im.reshape(groups, 1, states), _unband("unband_bb_re", dbb_re, per_sg),
                _unband("unband_bb_im", dbb_im, per_sg))
            grads['s5_lam_re'], grads['s5_lam_im'], grads['s5_log_dt'] = dlr[:, 0][None], dli[:, 0][None], dld[:, 0, 0][None]
            grads['s5_b_re'], grads['s5_b_im'] = dbr.transpose(0, 2, 1)[None], dbi.transpose(0, 2, 1)[None]
            grads['s5_c_re'] = _unband("unband_cc_re", dcc_re, per_sg).transpose(0, 2, 1)[None]
            grads['s5_c_im'] = _unband("unband_cc_im", dcc_im, per_sg).transpose(0, 2, 1)[None]
            grads['s5_d'] = dd
            dub = du.astype(MXU_DTYPE)
            p_s5_in = _mm_row_db(f"s5_in_db{i}", h1[None], dub, WIRE_DTYPE)
            dh1, (got,) = _ride(_mm_row_da, f"s5_in_da{i}", dub, gw['s5_in'].reshape(d, d), 1, riders=[to_sibling(p_s5_in)])
            dh1 = dh1[0]
            s_s5_in = pair("x_s5_w_in_pair", p_s5_in, got)
        else:
            zz, hs, ylru, o = mix
            g_lru_out = gw['lru_out'].reshape(lw, d)
            do, dg1 = _gate_bwd(f"lru_gate_bwd{i}", dx, o, g1)
            dyl, (got,) = _ride(_mm_row_da, f"lru_out_da{i}", do, g_lru_out, half, riders=[to_sibling(p_gu)])
            s_gu = pair(f"x_ffn_w_gu{i}_pair", p_gu, got)
            p_lru_out = _mm_row_db(f"lru_out_db{i}", ylru, do, WIRE_DTYPE)
            (dgb, dxp, dcv, dwrg, dwig), (*parts['ffn_w_gu', i], got) = _ride(
                _lru_bwd, f"lru_core_bwd{i}", zz, hs, dyl, cvec, wrg, wig,
                riders=[quarter_over_ici(s_gu, q) for q in range(4)] + [to_sibling(p_lru_out)])
            s_lru_out = pair("x_lru_w_out_pair", p_lru_out, got)
            dzz = jnp.concatenate([dgb, dxp], axis=0)
            dh1, (parts['lru_w_out', 0],) = _ride(_mm_col_da, f"lru_in_da{i}", dzz, gw['lru_in'],
                                                  riders=[over_ici(s_lru_out)])
            above = (('lru_w_in', 0), _mm_col_db(f"lru_in_db{i}", h1, dzz, WIRE_DTYPE))
            dcv = dcv.transpose(1, 0, 2).reshape(taps + 4, lw)
            grads['lru_conv_w'] = dcv[:taps].reshape(1, taps, 1, lw)
            grads['lru_conv_b'], grads['lru_b_rg'] = dcv[taps:taps + 1], dcv[taps + 1:taps + 2]
            grads['lru_b_ig'], grads['lru_lam'] = dcv[taps + 2:taps + 3], dcv[taps + 3:taps + 4]
            grads['lru_w_rg'] = _unband("unband_w_rg", dwrg, LRU_BLOCKS_PER_CHUNK)[None]
            grads['lru_w_ig'] = _unband("unband_w_ig", dwig, LRU_BLOCKS_PER_CHUNK)[None]
        dx, dgn1, dsc1, dsh1 = _norm_mod_bwd(f"norm1_bwd{i}", xin, dh1, dx, gn[0:1], sc1)
        dmod[i] = jnp.concatenate([dsh1, dsc1, dg1, dsh2, dsc2, dg2], axis=1)
        d_norm_g[i] = jnp.concatenate([dgn1, dgn2], axis=0)
    grad_x = dx[None]
    dmod = jnp.concatenate(dmod, axis=0)
    grads['norm_g'] = jnp.stack(d_norm_g)
    grads['b_ada'] = dmod
    grads['final_g'] = d_final_g[0]

    small_partial = _pack([grads[n] for n in SMALL], SUBLANE * N_DEV)
    rows8 = small_partial.shape[0] // N_DEV
    small_partial = small_partial.reshape(N_DEV, rows8, LANE)
    s_small = pair("x_small_pair", small_partial, _ride_alone("x_small_d2d", _sibling_rider(small_partial)))
    parts['s5_w_in', 0], small_parts = _ride_alone("x_tail_ici", _join([over_ici(s_s5_in), over_ici(s_small)]))

    out = {}
    dmod_all = _all_gather("ag_dmod", dmod)
    dmod_loc = lax.dynamic_slice_in_dim(dmod_all, me * n_loc, n_loc, axis=2).transpose(1, 0, 2)
    dmod16 = jnp.pad(dmod_loc, ((0, 0), (0, 2 * SUBLANE - N_DEV), (0, 0)))
    out['w_ada'] = _adamw_w_ada("adamw_w_ada", c16, dmod16, w_ada, m_w_ada, v_w_ada)

    for name in BIG[1:]:
        w = wv[name]
        rows, cols = w.shape[-2] * w.shape[0], w.shape[-1]
        flat = lambda a: a.reshape(rows, cols)
        pieces = []
        for l in range(w.shape[0]):
            pieces += parts[name, l] if isinstance(parts[name, l], list) else [parts[name, l]]
        res = _adamw_sum("adamw_" + name, pieces, flat(w), flat(mv[name]), flat(vv[name]))
        out[name] = [r.reshape(w.shape) for r in res]

    summed = _sum_parts("sum_small", small_parts)
    small_total = _all_gather("ag_small_sum", summed).reshape(-1, LANE)
    small_grad = dict(zip(SMALL, _unpack(small_total, [grads[n].shape for n in SMALL])))
    for n in SMALL_SHARDED:
        shard = wv[n].shape[-1]
        small_grad[n] = lax.dynamic_slice_in_dim(small_grad[n], me * shard, shard, axis=small_grad[n].ndim - 1)
    for n in SMALL:
        w = wv[n]
        flat = lambda a: a.reshape(-1, w.shape[-1])
        res = _adamw_sum("adamw_" + n, [flat(small_grad[n])[None]], flat(w), flat(mv[n]), flat(vv[n]))
        out[n] = [r.reshape(w.shape) for r in res]

    return (loss, grad_x, *[out[n][0] for n in WEIGHTS], *[out[n][1] for n in WEIGHTS],
            *[out[n][2] for n in WEIGHTS], *[out[n][3] for n in WEIGHTS])
```

```python
import functools
import math

import jax
import jax.numpy as jnp
from jax import lax
from jax.experimental import pallas as pl
from jax.experimental.pallas import tpu as pltpu

F32 = jnp.float32
MXU_DTYPE = jnp.bfloat16
WIRE_DTYPE = jnp.bfloat16
N_DEV = 8
EPS = 1e-6
LRU_C = 8.0
S5_GROUP = 16
S5_STATE = 64
S5_SUPER = 256
LRU_BLOCKS_PER_CHUNK = 4
ADAM_LR, ADAM_B1, ADAM_B2, ADAM_EPS, ADAM_WD, ADAM_STEP = 0.001, 0.9, 0.999, 1e-08, 0.01, 10
VMEM_LIMIT_BYTES = 56 * 1024 * 1024
LANE = 128
SUBLANE = 8

WEIGHTS = ['norm_g', 'w_ada', 'b_ada', 's5_w_in', 's5_lam_re', 's5_lam_im', 's5_log_dt', 's5_b_re', 's5_b_im',
           's5_c_re', 's5_c_im', 's5_d', 's5_w_glu', 'lru_w_in', 'lru_conv_w', 'lru_conv_b', 'lru_w_rg', 'lru_b_rg',
           'lru_w_ig', 'lru_b_ig', 'lru_lam', 'lru_w_out', 'ffn_w_gu', 'ffn_w_down', 'final_g']
BIG = ('w_ada', 's5_w_in', 's5_w_glu', 'lru_w_in', 'lru_w_out', 'ffn_w_gu', 'ffn_w_down')
SMALL = tuple(n for n in WEIGHTS if n not in BIG)
SMALL_SHARDED = ('norm_g', 'lru_conv_w', 'lru_conv_b', 'lru_b_rg', 'lru_b_ig', 'lru_lam')
SMALL_EARLY = tuple(n for n in SMALL if n.startswith('lru_'))
SMALL_LATE = tuple(n for n in SMALL if n not in SMALL_EARLY)

NN = (((1,), (0,)), ((), ()))
NT = (((1,), (1,)), ((), ()))
TN = (((0,), (0,)), ((), ()))


def _params(n_grid):
    return pltpu.CompilerParams(dimension_semantics=("arbitrary",) * n_grid, vmem_limit_bytes=VMEM_LIMIT_BYTES)


def _tile(dim, pref, align=LANE):
    if dim <= pref:
        return dim
    t = (pref // align) * align
    while t >= align:
        if dim % t == 0:
            return t
        t -= align
    return dim


def _dot(a, b, dims):
    return lax.dot_general(a.astype(MXU_DTYPE), b.astype(MXU_DTYPE), dims, preferred_element_type=F32)


def _gelu(x):
    k = math.sqrt(2.0 / math.pi)
    return 0.5 * x * (1.0 + jnp.tanh(k * (x + 0.044715 * (x * x * x))))


def _gelu_and_grad(x):
    k = math.sqrt(2.0 / math.pi)
    th = jnp.tanh(k * (x + 0.044715 * (x * x * x)))
    g = 0.5 * x * (1.0 + th)
    dg = 0.5 * (1.0 + th) + 0.5 * x * (1.0 - th * th) * (k * (1.0 + 3.0 * 0.044715 * (x * x)))
    return g, dg


def _neg_expm1(x):
    series = -x * (1.0 + x * (0.5 + x * (1.0 / 6.0 + x * (1.0 / 24.0 + x * (1.0 / 120.0)))))
    return jnp.where(x > -0.01, series, 1.0 - jnp.exp(x))


MESH = pl.DeviceIdType.MESH
N_CHIP = N_DEV // 2
ALL, SAME_CORE = 7, 6


def _place():
    x, y, c = lax.axis_index("x"), lax.axis_index("y"), lax.axis_index("c")
    return x, y, c


def _flip(place, k):
    x, y, c = place
    return (1 - x if (k >> 2) & 1 else x, 1 - y if (k >> 1) & 1 else y, 1 - c if k & 1 else c)


def _chunk_exchange(name, xs, group):
    return _ride_alone(name, _chunk_rider(xs, group))


class _Rider:
    def __init__(self, arrays, out_shape, scratch, start, finish, post, mid=None):
        self.arrays, self.out_shape, self.scratch = list(arrays), list(out_shape), list(scratch)
        self.start, self.finish, self.post, self.mid = start, finish, post, mid


def _chunk_rider(xs, group, rows=None):
    n = len(xs)
    members, r_all, c_ = xs[0].shape
    r0, r = (0, r_all) if rows is None else rows
    assert members == {ALL: N_DEV, SAME_CORE: N_CHIP}[group]
    assert all(a.shape == xs[0].shape and a.dtype == xs[0].dtype for a in xs)
    ks = [k for k in range(1, N_DEV) if not k & ~group]
    member = (lambda p: 4 * p[0] + 2 * p[1] + p[2]) if group == ALL else (lambda p: 2 * p[0] + p[1])

    def copies(ins, outs, scratch):
        out = outs[0]
        send_sems, recv_sems, local_sems = scratch
        place = _place()
        me = member(place)
        src = lambda l, who: ins[l].at[who] if rows is None else ins[l].at[who, pl.ds(r0, r)]
        local = [pltpu.make_async_copy(src(l, me), out.at[me, l], local_sems.at[l]) for l in range(n)]
        remote = []
        for l in range(n):
            for k in ks:
                pid = _flip(place, k)
                peer = member(pid)

                def copy(land_at, l=l, k=k, peer=peer, pid=pid):
                    return pltpu.make_async_remote_copy(
                        src_ref=src(l, peer), dst_ref=out.at[land_at, l], send_sem=send_sems.at[l * N_DEV + k],
                        recv_sem=recv_sems.at[l * N_DEV + k], device_id=pid, device_id_type=MESH)

                remote.append((copy, me, peer))
        return local, remote

    def start(ins, outs, scratch):
        local, remote = copies(ins, outs, scratch)
        for cp in local:
            cp.start()
        for copy, me, _ in remote:
            copy(me).start()

    def finish(ins, outs, scratch):
        local, remote = copies(ins, outs, scratch)
        for copy, me, peer in remote:
            copy(me).wait_send()
            copy(peer).wait_recv()
        for cp in local:
            cp.wait()

    return _Rider(
        xs, [jax.ShapeDtypeStruct((members, n, r, c_), xs[0].dtype)],
        [pltpu.SemaphoreType.DMA((n * N_DEV,)), pltpu.SemaphoreType.DMA((n * N_DEV,)), pltpu.SemaphoreType.DMA((n,))],
        start, finish, lambda outs: outs[0].reshape(members, n * r, c_))


def _gather_rider(xs):
    n = len(xs)
    per = 7

    def plan(ins, outs, scratch):
        send_sems, recv_sems, local_sems = scratch
        x, y, c = place = _place()
        sibling, x_nb, y_nb, diag = (x, y, 1 - c), (1 - x, y, c), (x, 1 - y, c), (1 - x, 1 - y, c)
        relayed = (x + c * (1 - 2 * x), y + (1 - c) * (1 - 2 * y), c)
        onward = (x + (1 - c) * (1 - 2 * x), y + c * (1 - 2 * y), c)
        jobs = []
        for l in range(n):
            slot = lambda p, l=l: outs[l].at[2 * p[0] + p[1], p[2]]

            def copy(k, block, to, src=None, l=l, slot=slot):
                return pltpu.make_async_remote_copy(
                    src_ref=slot(block) if src is None else src, dst_ref=slot(block), send_sem=send_sems.at[l * per + k],
                    recv_sem=recv_sems.at[l * per + k], device_id=to, device_id_type=MESH)

            jobs.append(dict(
                mine=pltpu.make_async_copy(ins[l], slot(place), local_sems.at[l]),
                first=[copy(0, place, sibling, src=ins[l]), copy(1, place, x_nb, src=ins[l]), copy(2, place, y_nb, src=ins[l])],
                landed=[copy(1, x_nb, place), copy(2, y_nb, place)],
                second=[copy(3, relayed, onward), copy(4, x_nb, sibling), copy(5, y_nb, sibling)],
                relay_landed=copy(3, diag, place), last=copy(6, diag, sibling),
                from_sibling=[copy(0, sibling, place)] + [copy(4 + j, (p[0], p[1], 1 - c), place)
                                                          for j, p in enumerate((x_nb, y_nb, diag))]))
        return jobs

    def start(ins, outs, scratch):
        for job in plan(ins, outs, scratch):
            job['mine'].start()
            for cp in job['first']:
                cp.start()

    def mid(ins, outs, scratch):
        for job in plan(ins, outs, scratch):
            for cp in job['landed']:
                cp.wait_recv()
            for cp in job['second']:
                cp.start()

    def finish(ins, outs, scratch):
        jobs = plan(ins, outs, scratch)
        for job in jobs:
            job['relay_landed'].wait_recv()
            job['last'].start()
        for job in jobs:
            for cp in job['from_sibling']:
                cp.wait_recv()
            for cp in job['first'] + job['second'] + [job['last']]:
                cp.wait_send()
            job['mine'].wait()

    return _Rider(
        xs, [jax.ShapeDtypeStruct((N_CHIP, 2) + x.shape, x.dtype) for x in xs],
        [pltpu.SemaphoreType.DMA((n * per,)), pltpu.SemaphoreType.DMA((n * per,)), pltpu.SemaphoreType.DMA((n,))],
        start, finish, lambda outs: [o.reshape((N_DEV,) + x.shape) for o, x in zip(outs, xs)], mid=mid)


HBM_SPEC = pl.BlockSpec(memory_space=pltpu.HBM)


def _ride_alone(name, rider):
    n_in, n_out = len(rider.arrays), len(rider.out_shape)

    def body(*refs):
        parts = refs[:n_in], refs[n_in:n_in + n_out], refs[n_in + n_out:]
        rider.start(*parts)
        if rider.mid is not None:
            rider.mid(*parts)
        rider.finish(*parts)

    outs = pl.pallas_call(body, name=name, out_shape=rider.out_shape, in_specs=[HBM_SPEC] * n_in,
                          out_specs=[HBM_SPEC] * n_out, scratch_shapes=rider.scratch)(*rider.arrays)
    return rider.post(list(outs))


def _call(body, *, name, grid, in_specs, out_specs, out_shape, scratch_shapes=(), args, rider=None):
    single = not isinstance(out_shape, (list, tuple))
    out_shape = [out_shape] if single else list(out_shape)
    out_specs = [out_specs] if single else list(out_specs)
    scratch_shapes = list(scratch_shapes)
    unwrap = lambda outs: outs[0] if single else list(outs)
    if rider is None:
        outs = pl.pallas_call(body, name=name, grid=grid, in_specs=list(in_specs), out_specs=out_specs, out_shape=out_shape,
                              scratch_shapes=scratch_shapes, compiler_params=_params(len(grid)))(*args)
        return unwrap(outs)
    n_in, n_out, n_scr = len(in_specs), len(out_shape), len(scratch_shapes)
    r_in, r_out = len(rider.arrays), len(rider.out_shape)

    def carried(*refs):
        ins, refs = refs[:n_in], refs[n_in:]
        r_ins, refs = refs[:r_in], refs[r_in:]
        outs, refs = refs[:n_out], refs[n_out:]
        r_outs, refs = refs[:r_out], refs[r_out:]
        scr, r_scr = refs[:n_scr], refs[n_scr:]
        step = 0
        for ax, g in enumerate(grid):
            step = step * g + pl.program_id(ax)

        @pl.when(step == 0)
        def _():
            rider.start(r_ins, r_outs, r_scr)

        body(*ins, *outs, *scr)

        if rider.mid is not None and total > 1:
            @pl.when(step == (5 * (total - 1)) // 8)
            def _():
                rider.mid(r_ins, r_outs, r_scr)

        @pl.when(step == total - 1)
        def _():
            if rider.mid is not None and total == 1:
                rider.mid(r_ins, r_outs, r_scr)
            rider.finish(r_ins, r_outs, r_scr)

    total = math.prod(grid)

    outs = pl.pallas_call(
        carried, name=name, grid=grid, in_specs=list(in_specs) + [HBM_SPEC] * r_in, out_specs=out_specs + [HBM_SPEC] * r_out,
        out_shape=out_shape + rider.out_shape, scratch_shapes=scratch_shapes + rider.scratch,
        compiler_params=_params(len(grid)))(*args, *rider.arrays)
    return unwrap(outs[:n_out]), rider.post(list(outs[n_out:]))


def _all_gather(name, x):
    return _ride_alone(name, _gather_rider([x]))[0]


def _sibling_rider(x):
    _, r, c_ = x.shape

    def copies(ins, outs, scratch):
        send_sems, recv_sems = scratch
        place = _place()
        return [pltpu.make_async_remote_copy(
            src_ref=ins[0].at[2 * chip + (1 - place[2])], dst_ref=outs[0].at[chip], send_sem=send_sems.at[chip],
            recv_sem=recv_sems.at[chip], device_id=_flip(place, 1), device_id_type=MESH) for chip in range(N_CHIP)]

    def start(ins, outs, scratch):
        for cp in copies(ins, outs, scratch):
            cp.start()

    def finish(ins, outs, scratch):
        for cp in copies(ins, outs, scratch):
            cp.wait()

    return _Rider([x], [jax.ShapeDtypeStruct((N_CHIP, r, c_), x.dtype)],
                  [pltpu.SemaphoreType.DMA((N_CHIP,)), pltpu.SemaphoreType.DMA((N_CHIP,))], start, finish, lambda outs: outs[0])


def _join(riders):
    def cut(seq, counts):
        out, off = [], 0
        for k in counts:
            out.append(seq[off:off + k])
            off += k
        return out

    def parts(ins, outs, scratch):
        return zip(riders, cut(ins, [len(r.arrays) for r in riders]), cut(outs, [len(r.out_shape) for r in riders]),
                   cut(scratch, [len(r.scratch) for r in riders]))

    def start(ins, outs, scratch):
        for r, i, o, s in parts(ins, outs, scratch):
            r.start(i, o, s)

    def finish(ins, outs, scratch):
        for r, i, o, s in parts(ins, outs, scratch):
            r.finish(i, o, s)

    def mid(ins, outs, scratch):
        for r, i, o, s in parts(ins, outs, scratch):
            if r.mid is not None:
                r.mid(i, o, s)

    return _Rider(
        [a for r in riders for a in r.arrays], [o for r in riders for o in r.out_shape], [s for r in riders for s in r.scratch],
        start, finish, lambda outs: [r.post(o) for r, o in zip(riders, cut(outs, [len(r.out_shape) for r in riders]))],
        mid=mid if any(r.mid is not None for r in riders) else None)


def _ride(fn, *args, riders):
    return fn(*args, rider=_join(riders))


def _pair_sum(name, x, got, core):
    _, r, c_ = x.shape
    br = _tile(r, max(256, (1 << 21) // c_), 2 * SUBLANE)

    def body(core_ref, x_ref, g_ref, o_ref):
        o_ref[...] = (x_ref[...].astype(F32) + g_ref[...].astype(F32)).astype(o_ref.dtype)

    return pl.pallas_call(
        body, name=name, out_shape=jax.ShapeDtypeStruct((N_CHIP, r, c_), x.dtype),
        grid_spec=pltpu.PrefetchScalarGridSpec(
            num_scalar_prefetch=1, grid=(N_CHIP, r // br),
            in_specs=[pl.BlockSpec((None, br, c_), lambda ch, i, core_ref: (2 * ch + core_ref[0], i, 0)),
                      pl.BlockSpec((None, br, c_), lambda ch, i, core_ref: (ch, i, 0))],
            out_specs=pl.BlockSpec((None, br, c_), lambda ch, i, core_ref: (ch, i, 0))),
        compiler_params=_params(2))(core, x, got)


def _mm(name, a, b, out_shape, out_dtype, grid, a_spec, b_spec, o_spec, dims, n_red, acc_shape, rider=None):
    red = tuple(range(len(grid) - n_red, len(grid)))
    out_type = jax.ShapeDtypeStruct(out_shape, out_dtype)
    if all(grid[ax] == 1 for ax in red):
        def single(a_ref, b_ref, o_ref):
            o_ref[...] = _dot(a_ref[...], b_ref[...], dims).astype(o_ref.dtype)

        return _call(single, name=name, out_shape=out_type, grid=grid, in_specs=[a_spec, b_spec], out_specs=o_spec,
                     args=(a, b), rider=rider)

    def body(a_ref, b_ref, o_ref, acc_ref):
        first = functools.reduce(jnp.logical_and, [pl.program_id(ax) == 0 for ax in red])
        last = functools.reduce(jnp.logical_and, [pl.program_id(ax) == grid[ax] - 1 for ax in red])

        @pl.when(first)
        def _():
            acc_ref[...] = jnp.zeros_like(acc_ref)

        acc_ref[...] += _dot(a_ref[...], b_ref[...], dims)

        @pl.when(last)
        def _():
            o_ref[...] = acc_ref[...].astype(o_ref.dtype)

    return _call(body, name=name, out_shape=out_type, grid=grid, in_specs=[a_spec, b_spec], out_specs=o_spec,
                 scratch_shapes=[pltpu.VMEM(acc_shape, F32)], args=(a, b), rider=rider)


def _mm_col(name, a, b, out_dtype=F32, rider=None):
    m, k = a.shape
    j, _, n = b.shape
    bm, bk = _tile(m, 1024), _tile(k, 2048)
    return _mm(name, a, b, (j, m, n), out_dtype, (j, m // bm, k // bk),
               pl.BlockSpec((bm, bk), lambda jj, mm, kk: (mm, kk)),
               pl.BlockSpec((None, bk, n), lambda jj, mm, kk: (jj, kk, 0)),
               pl.BlockSpec((None, bm, n), lambda jj, mm, kk: (jj, mm, 0)), NN, 1, (bm, n), rider)


def _mm_col_da(name, do, b, rider=None):
    j, m, n = do.shape
    k = b.shape[1]
    bm, bk = _tile(m, 1024), _tile(k, 1024)
    return _mm(name, do, b, (m, k), F32, (m // bm, k // bk, j),
               pl.BlockSpec((None, bm, n), lambda mm, kk, jj: (jj, mm, 0)),
               pl.BlockSpec((None, bk, n), lambda mm, kk, jj: (jj, kk, 0)),
               pl.BlockSpec((bm, bk), lambda mm, kk, jj: (mm, kk)), NT, 1, (bm, bk), rider)


def _mm_col_db(name, a, do, out_dtype, rider=None):
    m, k = a.shape
    j, _, n = do.shape
    bm, bk = _tile(m, 2048), _tile(k, 512)
    return _mm(name, a, do, (j, k, n), out_dtype, (j, k // bk, m // bm),
               pl.BlockSpec((bm, bk), lambda jj, kk, mm: (mm, kk)),
               pl.BlockSpec((None, bm, n), lambda jj, kk, mm: (jj, mm, 0)),
               pl.BlockSpec((None, bk, n), lambda jj, kk, mm: (jj, kk, 0)), TN, 1, (bk, n), rider)


def _row_bk(kq):
    return kq if (kq % LANE or kq // LANE in (11,)) else _tile(kq, 2048)


def _mm_row(name, a, b, out_dtype=F32, rider=None):
    q, m, kq = a.shape
    n = b.shape[1]
    bm, bn, bk = _tile(m, 1024), _tile(n, 1024), _row_bk(kq)
    nk = kq // bk
    return _mm(name, a, b, (m, n), out_dtype, (m // bm, n // bn, q, nk),
               pl.BlockSpec((None, bm, bk), lambda mm, nn, qq, kk: (qq, mm, kk)),
               pl.BlockSpec((bk, bn), lambda mm, nn, qq, kk: (qq * nk + kk, nn)),
               pl.BlockSpec((bm, bn), lambda mm, nn, qq, kk: (mm, nn)), NN, 2, (bm, bn), rider)


def _mm_row_da(name, do, b, q, rider=None):
    m, n = do.shape
    kq = b.shape[0] // q
    bm, bn = _tile(m, 1024), _tile(n, 2048)
    return _mm(name, do, b, (q, m, kq), F32, (q, m // bm, n // bn),
               pl.BlockSpec((bm, bn), lambda qq, mm, nn: (mm, nn)),
               pl.BlockSpec((kq, bn), lambda qq, mm, nn: (qq, nn)),
               pl.BlockSpec((None, bm, kq), lambda qq, mm, nn: (qq, mm, 0)), NT, 1, (bm, kq), rider)


def _mm_row_db(name, a, do, out_dtype):
    q, m, kq = a.shape
    n = do.shape[1]
    bm, bn = _tile(m, 2048), _tile(n, 512)
    return _mm(name, a, do, (q * kq, n), out_dtype, (q, n // bn, m // bm),
               pl.BlockSpec((None, bm, kq), lambda qq, nn, mm: (qq, mm, 0)),
               pl.BlockSpec((bm, bn), lambda qq, nn, mm: (mm, nn)),
               pl.BlockSpec((kq, bn), lambda qq, nn, mm: (qq, nn)), TN, 1, (kq, bn))


def _row_spec(bm, d):
    return pl.BlockSpec((bm, d), lambda i: (i, 0))


def _vec_spec(d):
    return pl.BlockSpec((1, d), lambda i: (0, 0))


def _norm_mod_fwd(name, x, gain, sc, sh):
    t, d = x.shape
    bm = _tile(t, 256, SUBLANE)

    def body(x_ref, g_ref, sc_ref, sh_ref, h_ref):
        xv = x_ref[...]
        rstd = lax.rsqrt(jnp.mean(xv * xv, axis=-1, keepdims=True) + EPS)
        h_ref[...] = ((xv * rstd) * g_ref[...] * (1.0 + sc_ref[...]) + sh_ref[...]).astype(h_ref.dtype)

    return pl.pallas_call(
        body, name=name, out_shape=jax.ShapeDtypeStruct((t, d), MXU_DTYPE), grid=(t // bm,),
        in_specs=[_row_spec(bm, d), _vec_spec(d), _vec_spec(d), _vec_spec(d)], out_specs=_row_spec(bm, d),
        compiler_params=_params(1))(x, gain, sc, sh)


def _norm_mod_bwd(name, x, dh, dres, gain, sc, rider=None):
    t, d = x.shape
    bm = _tile(t, 256, SUBLANE)

    def body(x_ref, dh_ref, dres_ref, g_ref, sc_ref, dx_ref, dg_ref, dsc_ref, dsh_ref):
        @pl.when(pl.program_id(0) == 0)
        def _():
            dg_ref[...] = jnp.zeros_like(dg_ref)
            dsc_ref[...] = jnp.zeros_like(dsc_ref)
            dsh_ref[...] = jnp.zeros_like(dsh_ref)

        xv, dh_ = x_ref[...], dh_ref[...]
        rstd = lax.rsqrt(jnp.mean(xv * xv, axis=-1, keepdims=True) + EPS)
        nrm = xv * rstd
        gain_ = g_ref[...]
        dsh_ref[...] += jnp.sum(dh_, axis=0, keepdims=True)
        dsc_ref[...] += jnp.sum(dh_ * (nrm * gain_), axis=0, keepdims=True)
        dhn = dh_ * (1.0 + sc_ref[...])
        dg_ref[...] += jnp.sum(dhn * nrm, axis=0, keepdims=True)
        dn = dhn * gain_
        dx_ref[...] = dres_ref[...] + rstd * (dn - nrm * jnp.mean(dn * nrm, axis=-1, keepdims=True))

    vec = jax.ShapeDtypeStruct((1, d), F32)
    return _call(
        body, name=name, out_shape=[jax.ShapeDtypeStruct((t, d), F32), vec, vec, vec], grid=(t // bm,),
        in_specs=[_row_spec(bm, d), _row_spec(bm, d), _row_spec(bm, d), _vec_spec(d), _vec_spec(d)],
        out_specs=[_row_spec(bm, d), _vec_spec(d), _vec_spec(d), _vec_spec(d)],
        args=(x, dh, dres, gain, sc), rider=rider)


def _loss_bwd(name, x, target, gain):
    t, d = x.shape
    bm = _tile(t, 256, SUBLANE)

    def body(x_ref, t_ref, g_ref, dx_ref, loss_ref, dg_ref):
        @pl.when(pl.program_id(0) == 0)
        def _():
            loss_ref[...] = jnp.zeros_like(loss_ref)
            dg_ref[...] = jnp.zeros_like(dg_ref)

        xv = x_ref[...]
        rstd = lax.rsqrt(jnp.mean(xv * xv, axis=-1, keepdims=True) + EPS)
        nrm = xv * rstd
        gain_ = g_ref[...]
        err = nrm * gain_ - t_ref[...]
        per_tok = jnp.mean(err * err, axis=-1, keepdims=True)
        loss_ref[...] += 0.5 * jnp.sum(per_tok, axis=0, keepdims=True)
        dout = err * (1.0 / d)
        dg_ref[...] += jnp.sum(dout * nrm, axis=0, keepdims=True)
        dn = dout * gain_
        dx_ref[...] = rstd * (dn - nrm * jnp.mean(dn * nrm, axis=-1, keepdims=True))

    return pl.pallas_call(
        body, name=name,
        out_shape=[jax.ShapeDtypeStruct((t, d), F32), jax.ShapeDtypeStruct((1, 1), F32),
                   jax.ShapeDtypeStruct((1, d), F32)],
        grid=(t // bm,), in_specs=[_row_spec(bm, d), _row_spec(bm, d), _vec_spec(d)],
        out_specs=[_row_spec(bm, d), pl.BlockSpec((1, 1), lambda i: (0, 0)), _vec_spec(d)],
        compiler_params=_params(1))(x, target, gain)


def _resid(name, x, y, g):
    t, d = x.shape
    bm = _tile(t, 256, SUBLANE)

    def body(x_ref, y_ref, g_ref, o_ref):
        o_ref[...] = x_ref[...] + g_ref[...] * y_ref[...]

    return pl.pallas_call(
        body, name=name, out_shape=jax.ShapeDtypeStruct((t, d), F32), grid=(t // bm,),
        in_specs=[_row_spec(bm, d), _row_spec(bm, d), _vec_spec(d)], out_specs=_row_spec(bm, d),
        compiler_params=_params(1))(x, y, g)


def _gate_bwd(name, dx, y, g):
    t, d = dx.shape
    bm = _tile(t, 256, SUBLANE)

    def body(dx_ref, y_ref, g_ref, dy_ref, dg_ref):
        @pl.when(pl.program_id(0) == 0)
        def _():
            dg_ref[...] = jnp.zeros_like(dg_ref)

        dxv = dx_ref[...]
        dy_ref[...] = (g_ref[...] * dxv).astype(dy_ref.dtype)
        dg_ref[...] += jnp.sum(dxv * y_ref[...], axis=0, keepdims=True)

    return pl.pallas_call(
        body, name=name, out_shape=[jax.ShapeDtypeStruct((t, d), MXU_DTYPE), jax.ShapeDtypeStruct((1, d), F32)],
        grid=(t // bm,), in_specs=[_row_spec(bm, d), _row_spec(bm, d), _vec_spec(d)],
        out_specs=[_row_spec(bm, d), _vec_spec(d)], compiler_params=_params(1))(dx, y, g)


def _glu_resid_fwd(name, z, x, g):
    _, t, n = z.shape
    d = x.shape[1]
    half = N_DEV // 2
    bm = _tile(t, 256, SUBLANE)

    def body(v_ref, gt_ref, x_ref, g_ref, o_ref):
        o_ref[...] = x_ref[...] + g_ref[...] * (v_ref[...] * jax.nn.sigmoid(gt_ref[...]))

    return pl.pallas_call(
        body, name=name, out_shape=jax.ShapeDtypeStruct((t, d), F32), grid=(half, t // bm),
        in_specs=[pl.BlockSpec((None, bm, n), lambda q, i: (q, i, 0)),
                  pl.BlockSpec((None, bm, n), lambda q, i: (q + half, i, 0)),
                  pl.BlockSpec((bm, n), lambda q, i: (i, q)), pl.BlockSpec((1, n), lambda q, i: (0, q))],
        out_specs=pl.BlockSpec((bm, n), lambda q, i: (i, q)), compiler_params=_params(2))(z, z, x, g)


def _glu_resid_bwd(name, z, dx, g, rider=None):
    _, t, n = z.shape
    d = dx.shape[1]
    half = N_DEV // 2
    bm = _tile(t, 256, SUBLANE)

    def body(z_ref, dx_ref, g_ref, dz_ref, dg_ref):
        @pl.when(pl.program_id(1) == 0)
        def _():
            dg_ref[...] = jnp.zeros_like(dg_ref)

        v, dxv = z_ref[0], dx_ref[...]
        sig = jax.nn.sigmoid(z_ref[1])
        dout = g_ref[...] * dxv
        dg_ref[...] += jnp.sum(dxv * (v * sig), axis=0, keepdims=True)
        dz_ref[0] = (dout * sig).astype(dz_ref.dtype)
        dz_ref[1] = (dout * v * (sig * (1.0 - sig))).astype(dz_ref.dtype)

    pair = pl.BlockSpec((2, None, bm, n), lambda q, i: (0, q, i, 0))
    res = _call(
        body, name=name,
        out_shape=[jax.ShapeDtypeStruct((2, half, t, n), MXU_DTYPE), jax.ShapeDtypeStruct((1, d), F32)],
        grid=(half, t // bm),
        in_specs=[pair, pl.BlockSpec((bm, n), lambda q, i: (i, q)), pl.BlockSpec((1, n), lambda q, i: (0, q))],
        out_specs=[pair, pl.BlockSpec((1, n), lambda q, i: (0, q))],
        args=(z.reshape(2, half, t, n), dx, g), rider=rider)
    (dz, dg), got = res if rider is not None else (res, None)
    dz = dz.reshape(N_DEV, t, n)
    return ((dz, dg), got) if rider is not None else (dz, dg)


def _swiglu_act_fwd(name, gu):
    _, t, n = gu.shape
    half = N_DEV // 2
    bm = _tile(t, 256, SUBLANE)

    def body(g_ref, u_ref, o_ref):
        gv = g_ref[...].astype(F32)
        o_ref[...] = (gv * jax.nn.sigmoid(gv) * u_ref[...].astype(F32)).astype(o_ref.dtype)

    return pl.pallas_call(
        body, name=name, out_shape=jax.ShapeDtypeStruct((half, t, n), MXU_DTYPE), grid=(half, t // bm),
        in_specs=[pl.BlockSpec((None, bm, n), lambda q, i: (q, i, 0)),
                  pl.BlockSpec((None, bm, n), lambda q, i: (q + half, i, 0))],
        out_specs=pl.BlockSpec((None, bm, n), lambda q, i: (q, i, 0)), compiler_params=_params(2))(gu, gu)


def _swiglu_act_bwd(name, gu, dact, rider=None):
    _, t, n = gu.shape
    half = N_DEV // 2
    bm = _tile(t, 256, SUBLANE)

    def body(gu_ref, da_ref, o_ref):
        gv, da = gu_ref[0].astype(F32), da_ref[...]
        sig = jax.nn.sigmoid(gv)
        o_ref[0] = (da * gu_ref[1].astype(F32) * (sig * (1.0 + gv * (1.0 - sig)))).astype(o_ref.dtype)
        o_ref[1] = (da * (gv * sig)).astype(o_ref.dtype)

    pair = pl.BlockSpec((2, None, bm, n), lambda q, i: (0, q, i, 0))
    res = _call(
        body, name=name, out_shape=jax.ShapeDtypeStruct((2, half, t, n), MXU_DTYPE), grid=(half, t // bm),
        in_specs=[pair, pl.BlockSpec((None, bm, n), lambda q, i: (q, i, 0))], out_specs=pair,
        args=(gu.reshape(2, half, t, n), dact), rider=rider)
    if rider is None:
        return res.reshape(N_DEV, t, n)
    return res[0].reshape(N_DEV, t, n), res[1]


def _ada_fwd(name, c16, w_ada, b_loc, rider=None):
    nl, d, n = w_ada.shape
    bn = _tile(n, 512)

    def body(c_ref, w_ref, b_ref, o_ref):
        cv = c_ref[...]
        o_ref[...] = _dot(cv * jax.nn.sigmoid(cv), w_ref[...], NN) + b_ref[...]

    return _call(
        body, name=name, out_shape=jax.ShapeDtypeStruct((nl, c16.shape[0], n), F32), grid=(nl, n // bn),
        in_specs=[pl.BlockSpec(c16.shape, lambda i, j: (0, 0)), pl.BlockSpec((None, d, bn), lambda i, j: (i, 0, j)),
                  pl.BlockSpec((None, 1, bn), lambda i, j: (i, 0, j))],
        out_specs=pl.BlockSpec((None, c16.shape[0], bn), lambda i, j: (i, 0, j)),
        args=(c16, w_ada, b_loc), rider=rider)


def _adam_update(g, w, m, v):
    m = ADAM_B1 * m + (1.0 - ADAM_B1) * g
    v = ADAM_B2 * v + (1.0 - ADAM_B2) * (g * g)
    m_hat = m / (1.0 - ADAM_B1 ** ADAM_STEP)
    v_hat = v / (1.0 - ADAM_B2 ** ADAM_STEP)
    delta = -ADAM_LR * (m_hat / (jnp.sqrt(v_hat) + ADAM_EPS) + ADAM_WD * w)
    return delta, m, v


def _adamw_w_ada(name, c16, dmod16, w, m, v, rider=None):
    nl, d, n = w.shape
    br = _tile(d, 256)

    def body(c_ref, dm_ref, w_ref, m_ref, v_ref, g_ref, dl_ref, mo_ref, vo_ref):
        cv = c_ref[...]
        g = _dot(cv * jax.nn.sigmoid(cv), dm_ref[...], TN)
        g_ref[...] = g
        dl_ref[...], mo_ref[...], vo_ref[...] = _adam_update(g, w_ref[...], m_ref[...], v_ref[...])

    blk = pl.BlockSpec((None, br, n), lambda i, r: (i, r, 0))
    shp = jax.ShapeDtypeStruct(w.shape, F32)
    return _call(
        body, name=name, out_shape=[shp] * 4, grid=(nl, d // br),
        in_specs=[pl.BlockSpec((c16.shape[0], br), lambda i, r: (0, r)),
                  pl.BlockSpec((None, dmod16.shape[1], n), lambda i, r: (i, 0, 0)), blk, blk, blk],
        out_specs=[blk] * 4, args=(c16, dmod16, w, m, v), rider=rider)


def _adamw_sum(name, parts, w, m, v, rider=None):
    nl = len(parts)
    p, r, c = parts[0].shape
    br = _tile(r, max(128, (1 << 17) // max(c, LANE)), 2 * SUBLANE)
    nb = r // br

    def body(*refs):
        p_refs, (w_ref, m_ref, v_ref, g_ref, dl_ref, mo_ref, vo_ref) = refs[:nl], refs[nl:]
        for l, p_ref in enumerate(p_refs):
            @pl.when(pl.program_id(0) == l)
            def _(p_ref=p_ref):
                gl = p_ref[0].astype(F32)
                for s in range(1, p):
                    gl = gl + p_ref[s].astype(F32)
                g_ref[...] = gl

        g = g_ref[...]
        dl_ref[...], mo_ref[...], vo_ref[...] = _adam_update(g, w_ref[...], m_ref[...], v_ref[...])

    blk = pl.BlockSpec((br, c), lambda l, i: (l * nb + i, 0))
    shp = jax.ShapeDtypeStruct((nl * r, c), F32)
    return _call(
        body, name=name, out_shape=[shp] * 4, grid=(nl, nb),
        in_specs=[pl.BlockSpec((p, br, c), lambda l, i, k=k: (0, jnp.where(l == k, i, 0), 0)) for k in range(nl)]
        + [blk, blk, blk], out_specs=[blk] * 4,
        args=(*parts, w, m, v), rider=rider)


def _sum_parts(name, parts):
    p, r, c = parts.shape

    def body(p_ref, o_ref):
        g = p_ref[0]
        for s in range(1, p):
            g = g + p_ref[s]
        o_ref[...] = g

    return pl.pallas_call(body, name=name, out_shape=jax.ShapeDtypeStruct((r, c), F32))(parts)


def _s5_disc(name, lam_re, lam_im, log_dt, b_re, b_im):
    def body(lr_ref, li_ref, ld_ref, br_ref, bi_ref, ar_ref, ai_ref, bbr_ref, bbi_ref):
        lr, li = lr_ref[...], li_ref[...]
        dt = jnp.exp(ld_ref[...])
        mag = jnp.exp(lr * dt)
        a_re, a_im = mag * jnp.cos(li * dt), mag * jnp.sin(li * dt)
        nr, ni = a_re - 1.0, a_im
        den = lr * lr + li * li
        f_re, f_im = (nr * lr + ni * li) / den, (ni * lr - nr * li) / den
        br, bi = br_ref[...], bi_ref[...]
        ar_ref[...], ai_ref[...] = a_re, a_im
        bbr_ref[...] = f_re * br - f_im * bi
        bbi_ref[...] = f_re * bi + f_im * br

    s_a, s_b = jax.ShapeDtypeStruct(lam_re.shape, F32), jax.ShapeDtypeStruct(b_re.shape, F32)
    return pl.pallas_call(body, name=name, out_shape=[s_a, s_a, s_b, s_b])(lam_re, lam_im, log_dt, b_re, b_im)


def _s5_disc_bwd(name, lam_re, lam_im, log_dt, b_re, b_im, dab_re, dab_im, dbb_re, dbb_im):
    def body(lr_ref, li_ref, ld_ref, br_ref, bi_ref, dar_ref, dai_ref, dbbr_ref, dbbi_ref,
             dlr_ref, dli_ref, dld_ref, dbr_ref, dbi_ref):
        lr, li = lr_ref[...], li_ref[...]
        dt = jnp.exp(ld_ref[...])
        mag = jnp.exp(lr * dt)
        a_re, a_im = mag * jnp.cos(li * dt), mag * jnp.sin(li * dt)
        nr, ni = a_re - 1.0, a_im
        den = lr * lr + li * li
        f_re, f_im = (nr * lr + ni * li) / den, (ni * lr - nr * li) / den
        br, bi = br_ref[...], bi_ref[...]
        dbbr, dbbi = dbbr_ref[...], dbbi_ref[...]
        dbr_ref[...] = f_re * dbbr + f_im * dbbi
        dbi_ref[...] = f_re * dbbi - f_im * dbbr
        df_re = jnp.sum(dbbr * br + dbbi * bi, axis=1, keepdims=True)
        df_im = jnp.sum(dbbi * br - dbbr * bi, axis=1, keepdims=True)
        dnr = (df_re * lr - df_im * li) / den
        dni = (df_re * li + df_im * lr) / den
        dden = -(df_re * f_re + df_im * f_im) / den
        dlr = (df_re * nr + df_im * ni) / den + 2.0 * lr * dden
        dli = (df_re * ni - df_im * nr) / den + 2.0 * li * dden
        da_re, da_im = dar_ref[...] + dnr, dai_ref[...] + dni
        dmag_mag = da_re * a_re + da_im * a_im
        dth = da_im * a_re - da_re * a_im
        dlr_ref[...] = dlr + dmag_mag * dt
        dli_ref[...] = dli + dth * dt
        ddt = jnp.sum(dmag_mag * lr + dth * li, axis=2, keepdims=True)
        dld_ref[...] = ddt * dt

    s_a, s_b = jax.ShapeDtypeStruct(lam_re.shape, F32), jax.ShapeDtypeStruct(b_re.shape, F32)
    return pl.pallas_call(
        body, name=name, out_shape=[s_a, s_a, jax.ShapeDtypeStruct(log_dt.shape, F32), s_b, s_b],
    )(lam_re, lam_im, log_dt, b_re, b_im, dab_re, dab_im, dbb_re, dbb_im)


def _s5_time_block(t):
    return _tile(t, 128, SUBLANE)


def _s5_scan_fwd(name, u, bb_re, bb_im, ab_re, ab_im, cc_re, cc_im, dskip, rider=None):
    t, d = u.shape
    nsg, cs, ns = bb_re.shape
    tb = _s5_time_block(t)

    def body(u_ref, bbr_hbm, bbi_hbm, ar_ref, ai_ref, ccr_hbm, cci_hbm, d_ref, sr_ref, si_ref, srm_ref, sim_ref, yp_ref,
             ya_ref, bbr, bbi, ccr, cci, cr_ref, ci_ref):
        @pl.when(pl.program_id(0) == 0)
        def _():
            pltpu.sync_copy(bbr_hbm, bbr)
            pltpu.sync_copy(bbi_hbm, bbi)
            pltpu.sync_copy(ccr_hbm, ccr)
            pltpu.sync_copy(cci_hbm, cci)
            cr_ref[...] = jnp.zeros_like(cr_ref)
            ci_ref[...] = jnp.zeros_like(ci_ref)

        for sg in range(nsg):
            us = u_ref[:, sg * cs:(sg + 1) * cs]
            sr_ref[:, sg, :] = _dot(us, bbr[sg], NN)
            si_ref[:, sg, :] = _dot(us, bbi[sg], NN)
        ar, ai = ar_ref[...], ai_ref[...]

        def step(i, carry):
            cr, ci = carry
            nr = ar * cr - ai * ci + sr_ref[i]
            ni = ar * ci + ai * cr + si_ref[i]
            sr_ref[i] = nr
            si_ref[i] = ni
            return nr, ni

        cr, ci = lax.fori_loop(0, tb, step, (cr_ref[...], ci_ref[...]), unroll=2)
        cr_ref[...], ci_ref[...] = cr, ci
        srm_ref[...] = jnp.swapaxes(sr_ref[...], 0, 1).astype(MXU_DTYPE)
        sim_ref[...] = jnp.swapaxes(si_ref[...], 0, 1).astype(MXU_DTYPE)
        for sg in range(nsg):
            cols = slice(sg * cs, (sg + 1) * cs)
            y = _dot(srm_ref[sg], ccr[sg], NN) - _dot(sim_ref[sg], cci[sg], NN) + d_ref[:, cols] * u_ref[:, cols]
            yp_ref[:, cols] = y
            ya_ref[:, cols] = _gelu(y).astype(ya_ref.dtype)

    scan = jax.ShapeDtypeStruct((t, nsg, ns), F32)
    mxu = jax.ShapeDtypeStruct((nsg, t, ns), MXU_DTYPE)
    hbm = pl.BlockSpec(memory_space=pltpu.HBM)
    full = pl.BlockSpec((nsg, ns), lambda i: (0, 0))
    return _call(
        body, name=name,
        out_shape=[scan, scan, mxu, mxu, jax.ShapeDtypeStruct((t, d), F32), jax.ShapeDtypeStruct((t, d), MXU_DTYPE)],
        grid=(t // tb,), in_specs=[_row_spec(tb, d), hbm, hbm, full, full, hbm, hbm, _vec_spec(d)],
        out_specs=[pl.BlockSpec((tb, nsg, ns), lambda i: (i, 0, 0))] * 2 + [pl.BlockSpec((nsg, tb, ns), lambda i: (0, i, 0))] * 2
        + [_row_spec(tb, d)] * 2,
        scratch_shapes=[pltpu.VMEM(bb_re.shape, bb_re.dtype), pltpu.VMEM(bb_im.shape, bb_im.dtype),
                        pltpu.VMEM(cc_re.shape, cc_re.dtype), pltpu.VMEM(cc_im.shape, cc_im.dtype),
                        pltpu.VMEM((nsg, ns), F32), pltpu.VMEM((nsg, ns), F32)],
        args=(u, bb_re, bb_im, ab_re, ab_im, cc_re, cc_im, dskip), rider=rider)


def _s5_scan_bwd(name, dyact, ypre, cc_re, cc_im, ab_re, ab_im, s_re, s_im, rider=None):
    t, d = dyact.shape
    nsg, ns, cs = cc_re.shape
    tb = _s5_time_block(t)
    nb = t // tb

    def body(dya_ref, yp_ref, ccr_hbm, cci_hbm, ar_ref, ai_ref, sr_ref, si_ref, dy_ref, lrm_ref, lim_ref, dar_ref, dai_ref,
             ccr, cci, lr_ref, li_ref, cr_ref, ci_ref):
        dy_ref[...] = (dya_ref[...] * _gelu_and_grad(yp_ref[...])[1]).astype(dy_ref.dtype)

        @pl.when(pl.program_id(0) == 0)
        def _():
            pltpu.sync_copy(ccr_hbm, ccr)
            pltpu.sync_copy(cci_hbm, cci)
            cr_ref[...] = jnp.zeros_like(cr_ref)
            ci_ref[...] = jnp.zeros_like(ci_ref)
            dar_ref[...] = jnp.zeros_like(dar_ref)
            dai_ref[...] = jnp.zeros_like(dai_ref)

        for sg in range(nsg):
            dys = dy_ref[:, sg * cs:(sg + 1) * cs]
            lr_ref[:, sg, :] = _dot(dys, ccr[sg], NT)
            li_ref[:, sg, :] = -_dot(dys, cci[sg], NT)
        ar, ai = ar_ref[...], ai_ref[...]

        def step(i, carry):
            cr, ci, dar, dai = carry
            j = tb - 1 - i
            sr, si = sr_ref[j], si_ref[j]
            dar = dar + (cr * sr + ci * si)
            dai = dai + (ci * sr - cr * si)
            nr = lr_ref[j] + (ar * cr + ai * ci)
            ni = li_ref[j] + (ar * ci - ai * cr)
            lr_ref[j] = nr
            li_ref[j] = ni
            return nr, ni, dar, dai

        cr, ci, dar, dai = lax.fori_loop(0, tb, step, (cr_ref[...], ci_ref[...], dar_ref[...], dai_ref[...]))
        cr_ref[...], ci_ref[...] = cr, ci
        dar_ref[...], dai_ref[...] = dar, dai
        lrm_ref[...] = jnp.swapaxes(lr_ref[...], 0, 1).astype(MXU_DTYPE)
        lim_ref[...] = jnp.swapaxes(li_ref[...], 0, 1).astype(MXU_DTYPE)

    hbm = pl.BlockSpec(memory_space=pltpu.HBM)
    full = pl.BlockSpec((nsg, ns), lambda i: (0, 0))
    mxu = jax.ShapeDtypeStruct((nsg, t, ns), MXU_DTYPE)
    acc = jax.ShapeDtypeStruct((nsg, ns), F32)
    scan_spec = pl.BlockSpec((tb, nsg, ns), lambda i: (nb - 1 - i, 0, 0))
    rows = pl.BlockSpec((tb, d), lambda i: (nb - 1 - i, 0))
    return _call(
        body, name=name, out_shape=[jax.ShapeDtypeStruct((t, d), MXU_DTYPE), mxu, mxu, acc, acc], grid=(nb,),
        in_specs=[rows, rows, hbm, hbm, full, full, scan_spec, scan_spec],
        out_specs=[rows] + [pl.BlockSpec((nsg, tb, ns), lambda i: (0, nb - 1 - i, 0))] * 2 + [full, full],
        scratch_shapes=[pltpu.VMEM(cc_re.shape, cc_re.dtype), pltpu.VMEM(cc_im.shape, cc_im.dtype),
                        pltpu.VMEM((tb, nsg, ns), F32), pltpu.VMEM((tb, nsg, ns), F32),
                        pltpu.VMEM((nsg, ns), F32), pltpu.VMEM((nsg, ns), F32)],
        args=(dyact, ypre, cc_re, cc_im, ab_re, ab_im, s_re, s_im), rider=rider)


def _s5_grads(name, lam_re, lam_im, s_re, s_im, u, dyp, bb_re, bb_im, dskip, rider=None):
    nsg, t, ns = lam_re.shape
    d = u.shape[1]
    cs = bb_re.shape[1]
    tb = _tile(t, 512, SUBLANE)

    def body(lr_ref, li_ref, sr_ref, si_ref, u_ref, dy_ref, bbr_ref, bbi_ref, d_ref,
             du_ref, dbbr_ref, dbbi_ref, dccr_ref, dcci_ref, dd_ref):
        @pl.when(pl.program_id(1) == 0)
        def _():
            for r in (dbbr_ref, dbbi_ref, dccr_ref, dcci_ref, dd_ref):
                r[...] = jnp.zeros_like(r)

        lr, li, uv, dy = lr_ref[...], li_ref[...], u_ref[...], dy_ref[...]
        dyf = dy.astype(F32)
        du_ref[...] = _dot(lr, bbr_ref[...], NT) + _dot(li, bbi_ref[...], NT) + d_ref[...] * dyf
        dbbr_ref[...] += _dot(uv, lr, TN)
        dbbi_ref[...] += _dot(uv, li, TN)
        dccr_ref[...] += _dot(sr_ref[...], dy, TN)
        dcci_ref[...] -= _dot(si_ref[...], dy, TN)
        dd_ref[...] += jnp.sum(dyf * uv, axis=0, keepdims=True)

    s_spec = pl.BlockSpec((None, tb, ns), lambda sg, i: (sg, i, 0))
    col = pl.BlockSpec((tb, cs), lambda sg, i: (i, sg))
    b_spec = pl.BlockSpec((None, cs, ns), lambda sg, i: (sg, 0, 0))
    c_spec = pl.BlockSpec((None, ns, cs), lambda sg, i: (sg, 0, 0))
    vec = pl.BlockSpec((1, cs), lambda sg, i: (0, sg))
    return _call(
        body, name=name,
        out_shape=[jax.ShapeDtypeStruct((t, d), F32), jax.ShapeDtypeStruct(bb_re.shape, F32),
                   jax.ShapeDtypeStruct(bb_re.shape, F32), jax.ShapeDtypeStruct((nsg, ns, cs), F32),
                   jax.ShapeDtypeStruct((nsg, ns, cs), F32), jax.ShapeDtypeStruct((1, d), F32)],
        grid=(nsg, t // tb), in_specs=[s_spec, s_spec, s_spec, s_spec, col, col, b_spec, b_spec, vec],
        out_specs=[col, b_spec, b_spec, c_spec, c_spec, vec],
        args=(lam_re, lam_im, s_re, s_im, u, dyp, bb_re, bb_im, dskip), rider=rider)


def _shift_down(x, k, prev8):
    if k == 0:
        return x
    ext = jnp.concatenate([prev8, x], axis=0)
    return ext[SUBLANE - k:SUBLANE - k + x.shape[0]]


def _shift_up(x, k, next8):
    if k == 0:
        return x
    ext = jnp.concatenate([x, next8], axis=0)
    return ext[k:k + x.shape[0]]


def _lru_time_block(t):
    return _tile(t, 256, SUBLANE)


def _lru_gates(xp, prev8, cv_ref, wrg, wig):
    taps = cv_ref.shape[0] - 4
    row = lambda k: cv_ref[k:k + 1, :]
    xs = [_shift_down(xp, taps - 1 - k, prev8) for k in range(taps)]
    xb = row(taps)
    for k in range(taps):
        xb = xb + row(k) * xs[k]
    r = jax.nn.sigmoid(_dot(xb, wrg, NN) + row(taps + 1))
    ig = jax.nn.sigmoid(_dot(xb, wig, NN) + row(taps + 2))
    sp = jax.nn.softplus(-row(taps + 3))
    log_a = -LRU_C * r * sp
    a = jnp.exp(log_a)
    mult = jnp.sqrt(_neg_expm1(2.0 * log_a))
    return xs, xb, r, ig, sp, a, mult


def _lru_fwd(name, zz, cvec, wrg, wig, rider=None):
    _, t, w = zz.shape
    half = N_DEV // 2
    tb = _lru_time_block(t)

    def body(gb_ref, xp_ref, xprev_ref, cv_ref, wrg_ref, wig_ref, hs_ref, y_ref, a_scr, b_scr, carry):
        i = pl.program_id(1)

        @pl.when(i == 0)
        def _():
            carry[...] = jnp.zeros_like(carry)

        prev8 = jnp.where(i > 0, xprev_ref[...], 0.0)
        _, xb, _, ig, _, a, mult = _lru_gates(xp_ref[...], prev8, cv_ref, wrg_ref[...], wig_ref[...])
        a_scr[...] = a
        b_scr[...] = mult * (ig * xb)

        def step(j, h):
            h = a_scr[pl.ds(j, 1), :] * h + b_scr[pl.ds(j, 1), :]
            hs_ref[pl.ds(j, 1), :] = h
            return h

        carry[0:1, :] = lax.fori_loop(0, tb, step, carry[0:1, :], unroll=8)
        y_ref[...] = (hs_ref[...] * _gelu(gb_ref[...])).astype(y_ref.dtype)

    nrow = tb // SUBLANE
    blk = lambda off: pl.BlockSpec((None, tb, w), lambda q, i: (q + off, i, 0))
    return _call(
        body, name=name,
        out_shape=[jax.ShapeDtypeStruct((half, t, w), F32), jax.ShapeDtypeStruct((half, t, w), MXU_DTYPE)],
        grid=(half, t // tb),
        in_specs=[blk(0), blk(half),
                  pl.BlockSpec((None, SUBLANE, w), lambda q, i: (q + half, jnp.maximum(i * nrow - 1, 0), 0)),
                  pl.BlockSpec((None,) + cvec.shape[1:], lambda q, i: (q, 0, 0)),
                  pl.BlockSpec((None, w, w), lambda q, i: (q, 0, 0)), pl.BlockSpec((None, w, w), lambda q, i: (q, 0, 0))],
        out_specs=[blk(0), blk(0)],
        scratch_shapes=[pltpu.VMEM((tb, w), F32), pltpu.VMEM((tb, w), F32), pltpu.VMEM((SUBLANE, w), F32)],
        args=(zz, zz, zz, cvec, wrg, wig), rider=rider)


def _lru_bwd(name, zz, hs, dy, cvec, wrg, wig, rider=None):
    _, t, w = zz.shape
    half = N_DEV // 2
    tb = _lru_time_block(t)
    nb = t // tb
    taps = cvec.shape[1] - 4

    def body(gb_ref, xp_ref, xprev_ref, hs_ref, hprev_ref, dy_ref, cv_ref, wrg_ref, wig_ref,
             dgb_ref, dxp_ref, dcv_ref, dwrg_ref, dwig_ref, a_scr, l_scr, carry, dxb_next):
        i = pl.program_id(1)

        @pl.when(i == 0)
        def _():
            for r_ in (carry, dxb_next, dcv_ref, dwrg_ref, dwig_ref):
                r_[...] = jnp.zeros_like(r_)

        has_prev = i < nb - 1
        row = lambda k: cv_ref[k:k + 1, :]
        prev8 = jnp.where(has_prev, xprev_ref[...], 0.0)
        xs, xb, r, ig, sp, a, mult = _lru_gates(xp_ref[...], prev8, cv_ref, wrg_ref[...], wig_ref[...])
        hs_ = hs_ref[...]
        hs_m1 = _shift_down(hs_, 1, jnp.where(has_prev, hprev_ref[...], 0.0))
        gel, dgel = _gelu_and_grad(gb_ref[...])
        dy_ = dy_ref[...]
        dgb_ref[...] = (dy_ * hs_ * dgel).astype(dgb_ref.dtype)
        a_scr[...] = a
        l_scr[...] = dy_ * gel

        def step(k, c):
            j = tb - 1 - k
            lam = l_scr[pl.ds(j, 1), :] + c
            l_scr[pl.ds(j, 1), :] = lam
            return a_scr[pl.ds(j, 1), :] * lam

        carry[0:1, :] = lax.fori_loop(0, tb, step, carry[0:1, :], unroll=8)
        lam = l_scr[...]
        dmult = lam * (ig * xb)
        dig = lam * (mult * xb)
        dxb = lam * (mult * ig)
        dlog_a = (lam * hs_m1) * a - dmult * (a * a) / mult
        dr = dlog_a * (-LRU_C * sp)
        dsp = jnp.sum(dlog_a * (-LRU_C * r), axis=0, keepdims=True)
        dpr = dr * (r * (1.0 - r))
        dpi = dig * (ig * (1.0 - ig))
        dwrg_ref[...] += _dot(xb, dpr, TN)
        dwig_ref[...] += _dot(xb, dpi, TN)
        dxb = dxb + _dot(dpr, wrg_ref[...], NT) + _dot(dpi, wig_ref[...], NT)
        for k in range(taps):
            dcv_ref[k:k + 1, :] += jnp.sum(dxb * xs[k], axis=0, keepdims=True)
        dcv_ref[taps:taps + 1, :] += jnp.sum(dxb, axis=0, keepdims=True)
        dcv_ref[taps + 1:taps + 2, :] += jnp.sum(dpr, axis=0, keepdims=True)
        dcv_ref[taps + 2:taps + 3, :] += jnp.sum(dpi, axis=0, keepdims=True)
        dcv_ref[taps + 3:taps + 4, :] += dsp * (-jax.nn.sigmoid(-row(taps + 3)))
        nxt8 = dxb_next[...]
        dxp = row(taps - 1) * dxb
        for k in range(taps - 1):
            dxp = dxp + row(k) * _shift_up(dxb, taps - 1 - k, nxt8)
        dxp_ref[...] = dxp.astype(dxp_ref.dtype)
        dxb_next[...] = dxb[0:SUBLANE]

    nrow = tb // SUBLANE
    blk = lambda off: pl.BlockSpec((None, tb, w), lambda q, i: (q + off, nb - 1 - i, 0))
    halo = lambda off: pl.BlockSpec((None, SUBLANE, w), lambda q, i: (q + off, jnp.maximum((nb - 1 - i) * nrow - 1, 0), 0))
    wspec = pl.BlockSpec((None, w, w), lambda q, i: (q, 0, 0))
    cspec = pl.BlockSpec((None,) + cvec.shape[1:], lambda q, i: (q, 0, 0))
    act = jax.ShapeDtypeStruct((half, t, w), MXU_DTYPE)
    return _call(
        body, name=name,
        out_shape=[act, act, jax.ShapeDtypeStruct(cvec.shape, F32), jax.ShapeDtypeStruct(wrg.shape, F32),
                   jax.ShapeDtypeStruct(wig.shape, F32)],
        grid=(half, nb),
        in_specs=[blk(0), blk(half), halo(half), blk(0), halo(0), blk(0), cspec, wspec, wspec],
        out_specs=[blk(0), blk(0), cspec, wspec, wspec],
        scratch_shapes=[pltpu.VMEM((tb, w), F32), pltpu.VMEM((tb, w), F32), pltpu.VMEM((SUBLANE, w), F32),
                        pltpu.VMEM((SUBLANE, w), F32)],
        args=(zz, zz, zz, hs, hs, dy, cvec, wrg, wig), rider=rider)


def _band(name, blocks, per, dtype):
    n, a, b = blocks.shape

    def body(x_ref, o_ref):
        o_ref[...] = jnp.zeros_like(o_ref)
        for g in range(per):
            o_ref[g * a:(g + 1) * a, g * b:(g + 1) * b] = x_ref[g].astype(o_ref.dtype)

    return pl.pallas_call(
        body, name=name, out_shape=jax.ShapeDtypeStruct((n // per, per * a, per * b), dtype), grid=(n // per,),
        in_specs=[pl.BlockSpec((per, a, b), lambda s: (s, 0, 0))],
        out_specs=pl.BlockSpec((None, per * a, per * b), lambda s: (s, 0, 0)), compiler_params=_params(1))(blocks)


def _unband(name, bands, per):
    s, pa, pb = bands.shape
    a, b = pa // per, pb // per

    def body(x_ref, o_ref):
        for g in range(per):
            o_ref[g] = x_ref[g * a:(g + 1) * a, g * b:(g + 1) * b]

    return pl.pallas_call(
        body, name=name, out_shape=jax.ShapeDtypeStruct((s * per, a, b), bands.dtype), grid=(s,),
        in_specs=[pl.BlockSpec((None, pa, pb), lambda i: (i, 0, 0))],
        out_specs=pl.BlockSpec((per, a, b), lambda i: (i, 0, 0)), compiler_params=_params(1))(bands)


def _pack(arrays, rows_multiple, lanes=LANE):
    flat = [a.reshape(-1).astype(F32) for a in arrays]
    size = sum(a.shape[0] for a in flat)
    rows = -(-size // (lanes * rows_multiple)) * rows_multiple
    if rows * lanes > size:
        flat.append(jnp.zeros((rows * lanes - size,), F32))
    return jnp.concatenate(flat).reshape(rows, lanes)


def _unpack(packed, shapes):
    flat = packed.reshape(-1)
    out, off = [], 0
    for s in shapes:
        n = math.prod(s)
        out.append(flat[off:off + n].reshape(s))
        off += n
    return out


def kernel(x, c, norm_g, w_ada, b_ada, s5_w_in, s5_lam_re, s5_lam_im, s5_log_dt, s5_b_re, s5_b_im, s5_c_re, s5_c_im, s5_d, s5_w_glu, lru_w_in, lru_conv_w, lru_conv_b, lru_w_rg, lru_b_rg, lru_w_ig, lru_b_ig, lru_lam, lru_w_out, ffn_w_gu, ffn_w_down, final_g, loss_target, m_norm_g, m_w_ada, m_b_ada, m_s5_w_in, m_s5_lam_re, m_s5_lam_im, m_s5_log_dt, m_s5_b_re, m_s5_b_im, m_s5_c_re, m_s5_c_im, m_s5_d, m_s5_w_glu, m_lru_w_in, m_lru_conv_w, m_lru_conv_b, m_lru_w_rg, m_lru_b_rg, m_lru_w_ig, m_lru_b_ig, m_lru_lam, m_lru_w_out, m_ffn_w_gu, m_ffn_w_down, m_final_g, v_norm_g, v_w_ada, v_b_ada, v_s5_w_in, v_s5_lam_re, v_s5_lam_im, v_s5_log_dt, v_s5_b_re, v_s5_b_im, v_s5_c_re, v_s5_c_im, v_s5_d, v_s5_w_glu, v_lru_w_in, v_lru_conv_w, v_lru_conv_b, v_lru_w_rg, v_lru_b_rg, v_lru_w_ig, v_lru_b_ig, v_lru_lam, v_lru_w_out, v_ffn_w_gu, v_ffn_w_down, v_final_g):
    wv = dict(zip(WEIGHTS, (norm_g, w_ada, b_ada, s5_w_in, s5_lam_re, s5_lam_im, s5_log_dt, s5_b_re, s5_b_im, s5_c_re, s5_c_im, s5_d, s5_w_glu, lru_w_in, lru_conv_w, lru_conv_b, lru_w_rg, lru_b_rg, lru_w_ig, lru_b_ig, lru_lam, lru_w_out, ffn_w_gu, ffn_w_down, final_g)))
    mv = dict(zip(WEIGHTS, (m_norm_g, m_w_ada, m_b_ada, m_s5_w_in, m_s5_lam_re, m_s5_lam_im, m_s5_log_dt, m_s5_b_re, m_s5_b_im, m_s5_c_re, m_s5_c_im, m_s5_d, m_s5_w_glu, m_lru_w_in, m_lru_conv_w, m_lru_conv_b, m_lru_w_rg, m_lru_b_rg, m_lru_w_ig, m_lru_b_ig, m_lru_lam, m_lru_w_out, m_ffn_w_gu, m_ffn_w_down, m_final_g)))
    vv = dict(zip(WEIGHTS, (v_norm_g, v_w_ada, v_b_ada, v_s5_w_in, v_s5_lam_re, v_s5_lam_im, v_s5_log_dt, v_s5_b_re, v_s5_b_im, v_s5_c_re, v_s5_c_im, v_s5_d, v_s5_w_glu, v_lru_w_in, v_lru_conv_w, v_lru_conv_b, v_lru_w_rg, v_lru_b_rg, v_lru_w_ig, v_lru_b_ig, v_lru_lam, v_lru_w_out, v_ffn_w_gu, v_ffn_w_down, v_final_g)))

    me = 4 * lax.axis_index("x") + 2 * lax.axis_index("y") + lax.axis_index("c")
    x0 = x[0]
    tgt = loss_target[0]
    t, d = x0.shape
    depth = norm_g.shape[0]
    n_mod = w_ada.shape[2] * N_DEV // d
    groups, states = s5_lam_re.shape[1], s5_lam_re.shape[2]
    per_sg = S5_SUPER // S5_GROUP
    nsg = groups // per_sg
    lw = lru_lam.shape[1] * N_DEV
    lwc = lw // (N_DEV // 2)
    half = N_DEV // 2

    assert depth == 2, "the ride schedule below is written for one S5 layer followed by one RG-LRU layer"
    wire = lambda a: a.astype(WIRE_DTYPE)
    gw = {}

    def riding(job, fn, *args):
        res, (got,) = fn(*args, rider=_gather_rider([wire(job[1])]))
        gw[job[0]] = got
        return res

    sh_shapes = [wv[n].shape for n in SMALL_SHARDED] + [c.shape]
    sh_all = _all_gather("ag_small", _pack([wv[n] for n in SMALL_SHARDED] + [c], SUBLANE))
    sh_parts = [jnp.stack(p) for p in zip(*[_unpack(sh_all[s], sh_shapes) for s in range(N_DEV)])]
    full = {}
    for n, p in zip(SMALL_SHARDED, sh_parts[:-1]):
        full[n] = jnp.moveaxis(p, 0, -2).reshape(p.shape[1:-1] + (-1,))
    c_all = sh_parts[-1].reshape(N_DEV, d)
    c16 = jnp.pad(c_all, ((0, 2 * SUBLANE - N_DEV), (0, 0)))

    n_loc = w_ada.shape[2]
    b_loc = lax.dynamic_slice_in_dim(b_ada, me * n_loc, n_loc, axis=1)[:, None, :]
    mod_part = riding(('s5_in', s5_w_in[0]), _ada_fwd, "ada_fwd", c16, w_ada, b_loc)[:, :N_DEV]
    mod_mine = _chunk_exchange("x_mod", [mod_part.transpose(1, 0, 2)], ALL)
    mod = mod_mine.transpose(1, 0, 2).reshape(depth, n_mod, 1, d)

    lam3 = lambda a: a[0][:, None, :]
    p_lr, p_li, p_ld = lam3(s5_lam_re), lam3(s5_lam_im), s5_log_dt[0][:, None, None]
    p_br, p_bi = s5_b_re[0].transpose(0, 2, 1), s5_b_im[0].transpose(0, 2, 1)
    ab_re3, ab_im3, bb_re3, bb_im3 = _s5_disc("s5_disc", p_lr, p_li, p_ld, p_br, p_bi)
    ab_re, ab_im = ab_re3.reshape(nsg, per_sg * states), ab_im3.reshape(nsg, per_sg * states)
    bb_re = _band("band_bb_re", bb_re3, per_sg, MXU_DTYPE)
    bb_im = _band("band_bb_im", bb_im3, per_sg, MXU_DTYPE)
    cc_re = _band("band_cc_re", s5_c_re[0].transpose(0, 2, 1), per_sg, MXU_DTYPE)
    cc_im = _band("band_cc_im", s5_c_im[0].transpose(0, 2, 1), per_sg, MXU_DTYPE)

    taps = lru_conv_w.shape[1]
    cvec = jnp.concatenate([full['lru_conv_w'].reshape(taps, lw), full['lru_conv_b'], full['lru_b_rg'],
                            full['lru_b_ig'], full['lru_lam']], axis=0)
    cvec = cvec.reshape(taps + 4, half, lwc).transpose(1, 0, 2)
    wrg = _band("band_w_rg", lru_w_rg[0], LRU_BLOCKS_PER_CHUNK, MXU_DTYPE)
    wig = _band("band_w_ig", lru_w_ig[0], LRU_BLOCKS_PER_CHUNK, MXU_DTYPE)

    saved = []
    xc = x0
    for i in range(depth):
        sh1, sc1, g1, sh2, sc2, g2 = [mod[i, k] for k in range(n_mod)]
        gn = full['norm_g'][i]
        h1 = _norm_mod_fwd(f"norm1_fwd{i}", xc, gn[0:1], sc1, sh1)
        if i % 2 == 0:
            u = riding(('s5_glu', s5_w_glu[0]), _mm_row, f"s5_in{i}", h1[None], gw['s5_in'].reshape(d, d))
            s_re, s_im, s_rem, s_imm, ypre, yact = riding((('gu', i), ffn_w_gu[i]), _s5_scan_fwd, f"s5_scan{i}", u, bb_re,
                                                          bb_im, ab_re, ab_im, cc_re, cc_im, s5_d)
            z = riding((('down', i), ffn_w_down[i]), _mm_col, f"s5_glu{i}", yact, gw['s5_glu'])
            x1 = _glu_resid_fwd(f"s5_resid{i}", z, xc, g1)
            mix = (u, s_re, s_im, s_rem, s_imm, ypre, yact, z)
        else:
            zz = _mm_col(f"lru_in{i}", h1, gw['lru_in'])
            hs, ylru = riding((('gu', i), ffn_w_gu[i]), _lru_fwd, f"lru_core{i}", zz, cvec, wrg, wig)
            o = _mm_row(f"lru_out{i}", ylru, gw['lru_out'].reshape(lw, d))
            x1 = _resid(f"lru_resid{i}", xc, o, g1)
            mix = (zz, hs, ylru, o)
        h2 = _norm_mod_fwd(f"norm2_fwd{i}", x1, gn[1:2], sc2, sh2)
        if i % 2 == 0:
            gu = riding(('lru_in', lru_w_in[0]), _mm_col, f"ffn_gu{i}", h2, gw['gu', i], MXU_DTYPE)
            act = _swiglu_act_fwd(f"ffn_act{i}", gu)
            f = riding(('lru_out', lru_w_out[0]), _mm_row, f"ffn_down{i}", act, gw['down', i].reshape(-1, d))
        else:
            gu = riding((('down', i), ffn_w_down[i]), _mm_col, f"ffn_gu{i}", h2, gw['gu', i], MXU_DTYPE)
            act = _swiglu_act_fwd(f"ffn_act{i}", gu)
            f = _mm_row(f"ffn_down{i}", act, gw['down', i].reshape(-1, d))
        x2 = _resid(f"ffn_resid{i}", x1, f, g2)
        saved.append((xc, h1, mix, x1, h2, gu, act, f))
        xc = x2

    dx, loss_part, d_final_g = _loss_bwd("loss", xc, tgt, final_g[None])
    loss = lax.psum(loss_part[0, 0], ("x", "y", "c"))

    grads = {}
    parts = {}
    dmod = [None] * depth
    d_norm_g = [None] * depth
    core = lax.axis_index("c").astype(jnp.int32).reshape(1)
    chunked = lambda p: p.reshape(N_DEV, -1, p.shape[-1])
    to_sibling = lambda p: _sibling_rider(chunked(p))
    pair = lambda name, p, got: _pair_sum(name, chunked(p), got, core)
    over_ici = lambda sums: _chunk_rider([sums], SAME_CORE)
    quarter_over_ici = lambda sums, q: _chunk_rider([sums], SAME_CORE, rows=(q * (sums.shape[1] // 4), sums.shape[1] // 4))

    above = None
    for i in reversed(range(depth)):
        xin, h1, mix, x1, h2, gu, act, f = saved[i]
        sh1, sc1, g1, sh2, sc2, g2 = [mod[i, k] for k in range(n_mod)]
        gn = full['norm_g'][i]
        g_down = gw['down', i].reshape(-1, d)
        df, dg2 = _gate_bwd(f"ffn_gate_bwd{i}", dx, f, g2)
        if above is None:
            dact = _mm_row_da(f"ffn_down_da{i}", df, g_down, half)
        else:
            dact, (got, early_parts) = _ride(_mm_row_da, f"ffn_down_da{i}", df, g_down, half,
                                             riders=[to_sibling(above[1]), over_ici(s_early)])
            s_above = pair(f"x_{above[0][0]}_pair", above[1], got)
        p_down = _mm_row_db(f"ffn_down_db{i}", act, df, WIRE_DTYPE)
        if above is None:
            dgu, (got,) = _ride(_swiglu_act_bwd, f"ffn_act_bwd{i}", gu, dact, riders=[to_sibling(p_down)])
        else:
            dgu, (got, (early_total,)) = _ride(
                _swiglu_act_bwd, f"ffn_act_bwd{i}", gu, dact,
                riders=[to_sibling(p_down), _gather_rider([_sum_parts("sum_small_early", early_parts)])])
        s_down = pair(f"x_ffn_w_down{i}_pair", p_down, got)
        if above is None:
            dh2, (parts['ffn_w_down', i],) = _ride(_mm_col_da, f"ffn_gu_da{i}", dgu, gw['gu', i], riders=[over_ici(s_down)])
            p_gu = _mm_col_db(f"ffn_gu_db{i}", h2, dgu, WIRE_DTYPE)
        else:
            dh2, (parts[above[0]],) = _ride(_mm_col_da, f"ffn_gu_da{i}", dgu, gw['gu', i], riders=[over_ici(s_above)])
            p_gu, (parts['ffn_w_down', i],) = _ride(_mm_col_db, f"ffn_gu_db{i}", h2, dgu, WIRE_DTYPE,
                                                    riders=[over_ici(s_down)])
        dx, dgn2, dsc2, dsh2 = _norm_mod_bwd(f"norm2_bwd{i}", x1, dh2, dx, gn[1:2], sc2)
        if i % 2 == 0:
            u, s_re, s_im, s_rem, s_imm, ypre, yact, z = mix
            (dz, dg1), (got,) = _ride(_glu_resid_bwd, f"s5_resid_bwd{i}", z, dx, g1, riders=[to_sibling(p_gu)])
            s_gu = pair(f"x_ffn_w_gu{i}_pair", p_gu, got)
            dyact, (gu_0,) = _ride(_mm_col_da, f"s5_glu_da{i}", dz, gw['s5_glu'], riders=[quarter_over_ici(s_gu, 0)])
            p_glu, (gu_1,) = _ride(_mm_col_db, f"s5_glu_db{i}", yact, dz, WIRE_DTYPE, riders=[quarter_over_ici(s_gu, 1)])
            (dyp, l_rem, l_imm, dab_re, dab_im), (gu_2, gu_3, got) = _ride(
                _s5_scan_bwd, f"s5_scan_bwd{i}", dyact, ypre, cc_re, cc_im, ab_re, ab_im, s_re, s_im,
                riders=[quarter_over_ici(s_gu, 2), quarter_over_ici(s_gu, 3), to_sibling(p_glu)])
            parts['ffn_w_gu', i] = [gu_0, gu_1, gu_2, gu_3]
            s_glu = pair("x_s5_w_glu_pair", p_glu, got)
            (du, dbb_re, dbb_im, dcc_re, dcc_im, dd), (parts['s5_w_glu', 0],) = _ride(
                _s5_grads, f"s5_grads{i}", l_rem, l_imm, s_rem, s_imm, u, dyp, bb_re, bb_im, s5_d, riders=[over_ici(s_glu)])
            dlr, dli, dld, dbr, dbi = _s5_disc_bwd(
                "s5_disc_bwd", p_lr, p_li, p_ld, p_br, p_bi, dab_re.reshape(groups, 1, states),
                dab_im.reshape(groups, 1, states), _unband("unband_bb_re", dbb_re, per_sg),
                _unband("unband_bb_im", dbb_im, per_sg))
            grads['s5_lam_re'], grads['s5_lam_im'], grads['s5_log_dt'] = dlr[:, 0][None], dli[:, 0][None], dld[:, 0, 0][None]
            grads['s5_b_re'], grads['s5_b_im'] = dbr.transpose(0, 2, 1)[None], dbi.transpose(0, 2, 1)[None]
            grads['s5_c_re'] = _unband("unband_cc_re", dcc_re, per_sg).transpose(0, 2, 1)[None]
            grads['s5_c_im'] = _unband("unband_cc_im", dcc_im, per_sg).transpose(0, 2, 1)[None]
            grads['s5_d'] = dd
            dub = du.astype(MXU_DTYPE)
            p_s5_in = _mm_row_db(f"s5_in_db{i}", h1[None], dub, WIRE_DTYPE)
            dh1, (got,) = _ride(_mm_row_da, f"s5_in_da{i}", dub, gw['s5_in'].reshape(d, d), 1, riders=[to_sibling(p_s5_in)])
            dh1 = dh1[0]
            s_s5_in = pair("x_s5_w_in_pair", p_s5_in, got)
            dx, dgn1, dsc1, dsh1 = _norm_mod_bwd(f"norm1_bwd{i}", xin, dh1, dx, gn[0:1], sc1)
        else:
            zz, hs, ylru, o = mix
            g_lru_out = gw['lru_out'].reshape(lw, d)
            do, dg1 = _gate_bwd(f"lru_gate_bwd{i}", dx, o, g1)
            dyl, (got,) = _ride(_mm_row_da, f"lru_out_da{i}", do, g_lru_out, half, riders=[to_sibling(p_gu)])
            s_gu = pair(f"x_ffn_w_gu{i}_pair", p_gu, got)
            p_lru_out = _mm_row_db(f"lru_out_db{i}", ylru, do, WIRE_DTYPE)
            (dgb, dxp, dcv, dwrg, dwig), (*parts['ffn_w_gu', i], got) = _ride(
                _lru_bwd, f"lru_core_bwd{i}", zz, hs, dyl, cvec, wrg, wig,
                riders=[quarter_over_ici(s_gu, q) for q in range(4)] + [to_sibling(p_lru_out)])
            s_lru_out = pair("x_lru_w_out_pair", p_lru_out, got)
            dzz = jnp.concatenate([dgb, dxp], axis=0)
            dh1, (parts['lru_w_out', 0],) = _ride(_mm_col_da, f"lru_in_da{i}", dzz, gw['lru_in'],
                                                  riders=[over_ici(s_lru_out)])
            above = (('lru_w_in', 0), _mm_col_db(f"lru_in_db{i}", h1, dzz, WIRE_DTYPE))
            dcv = dcv.transpose(1, 0, 2).reshape(taps + 4, lw)
            grads['lru_conv_w'] = dcv[:taps].reshape(1, taps, 1, lw)
            grads['lru_conv_b'], grads['lru_b_rg'] = dcv[taps:taps + 1], dcv[taps + 1:taps + 2]
            grads['lru_b_ig'], grads['lru_lam'] = dcv[taps + 2:taps + 3], dcv[taps + 3:taps + 4]
            grads['lru_w_rg'] = _unband("unband_w_rg", dwrg, LRU_BLOCKS_PER_CHUNK)[None]
            grads['lru_w_ig'] = _unband("unband_w_ig", dwig, LRU_BLOCKS_PER_CHUNK)[None]
            early = _pack([grads[n] for n in SMALL_EARLY], SUBLANE * N_DEV).reshape(N_DEV, -1, LANE)
            (dx, dgn1, dsc1, dsh1), (got,) = _ride(_norm_mod_bwd, f"norm1_bwd{i}", xin, dh1, dx, gn[0:1], sc1,
                                                   riders=[_sibling_rider(early)])
            s_early = pair("x_small_early_pair", early, got)
        dmod[i] = jnp.concatenate([dsh1, dsc1, dg1, dsh2, dsc2, dg2], axis=1)
        d_norm_g[i] = jnp.concatenate([dgn1, dgn2], axis=0)
    grad_x = dx[None]
    dmod = jnp.concatenate(dmod, axis=0)
    grads['norm_g'] = jnp.stack(d_norm_g)
    grads['b_ada'] = dmod
    grads['final_g'] = d_final_g[0]

    small_partial = _pack([grads[n] for n in SMALL_LATE], SUBLANE * N_DEV)
    rows8 = small_partial.shape[0] // N_DEV
    small_partial = small_partial.reshape(N_DEV, rows8, LANE)
    s_small = pair("x_small_pair", small_partial, _ride_alone("x_small_d2d", _sibling_rider(small_partial)))
    parts['s5_w_in', 0], small_parts = _ride_alone("x_tail_ici", _join([over_ici(s_s5_in), over_ici(s_small)]))

    out = {}
    dmod_all = _all_gather("ag_dmod", dmod)
    dmod_loc = lax.dynamic_slice_in_dim(dmod_all, me * n_loc, n_loc, axis=2).transpose(1, 0, 2)
    dmod16 = jnp.pad(dmod_loc, ((0, 0), (0, 2 * SUBLANE - N_DEV), (0, 0)))
    out['w_ada'] = _adamw_w_ada("adamw_w_ada", c16, dmod16, w_ada, m_w_ada, v_w_ada)

    for name in BIG[1:]:
        w = wv[name]
        rows, cols = w.shape[-2] * w.shape[0], w.shape[-1]
        flat = lambda a: a.reshape(rows, cols)
        pieces = []
        for l in range(w.shape[0]):
            pieces += parts[name, l] if isinstance(parts[name, l], list) else [parts[name, l]]
        res = _adamw_sum("adamw_" + name, pieces, flat(w), flat(mv[name]), flat(vv[name]))
        out[name] = [r.reshape(w.shape) for r in res]

    summed = _sum_parts("sum_small", small_parts)
    small_total = _all_gather("ag_small_sum", summed).reshape(-1, LANE)
    small_grad = dict(zip(SMALL_LATE, _unpack(small_total, [grads[n].shape for n in SMALL_LATE])))
    small_grad.update(zip(SMALL_EARLY, _unpack(early_total.reshape(-1, LANE), [grads[n].shape for n in SMALL_EARLY])))
    for n in SMALL_SHARDED:
        shard = wv[n].shape[-1]
        small_grad[n] = lax.dynamic_slice_in_dim(small_grad[n], me * shard, shard, axis=small_grad[n].ndim - 1)
    for n in SMALL:
        w = wv[n]
        flat = lambda a: a.reshape(-1, w.shape[-1])
        res = _adamw_sum("adamw_" + n, [flat(small_grad[n])[None]], flat(w), flat(mv[n]), flat(vv[n]))
        out[n] = [r.reshape(w.shape) for r in res]

    return (loss, grad_x, *[out[n][0] for n in WEIGHTS], *[out[n][1] for n in WEIGHTS],
            *[out[n][2] for n in WEIGHTS], *[out[n][3] for n in WEIGHTS])
```

```python
import functools
import math

import jax
import jax.numpy as jnp
from jax import lax
from jax.experimental import pallas as pl
from jax.experimental.pallas import tpu as pltpu

F32 = jnp.float32
MXU_DTYPE = jnp.bfloat16
WIRE_DTYPE = jnp.bfloat16
N_DEV = 8
EPS = 1e-6
LRU_C = 8.0
S5_GROUP = 16
S5_STATE = 64
S5_SUPER = 256
LRU_BLOCKS_PER_CHUNK = 4
ADAM_LR, ADAM_B1, ADAM_B2, ADAM_EPS, ADAM_WD, ADAM_STEP = 0.001, 0.9, 0.999, 1e-08, 0.01, 10
VMEM_LIMIT_BYTES = 56 * 1024 * 1024
LANE = 128
SUBLANE = 8

WEIGHTS = ['norm_g', 'w_ada', 'b_ada', 's5_w_in', 's5_lam_re', 's5_lam_im', 's5_log_dt', 's5_b_re', 's5_b_im',
           's5_c_re', 's5_c_im', 's5_d', 's5_w_glu', 'lru_w_in', 'lru_conv_w', 'lru_conv_b', 'lru_w_rg', 'lru_b_rg',
           'lru_w_ig', 'lru_b_ig', 'lru_lam', 'lru_w_out', 'ffn_w_gu', 'ffn_w_down', 'final_g']
BIG = ('w_ada', 's5_w_in', 's5_w_glu', 'lru_w_in', 'lru_w_out', 'ffn_w_gu', 'ffn_w_down')
SMALL = tuple(n for n in WEIGHTS if n not in BIG)
SMALL_SHARDED = ('norm_g', 'lru_conv_w', 'lru_conv_b', 'lru_b_rg', 'lru_b_ig', 'lru_lam')
SMALL_EARLY = tuple(n for n in SMALL if n.startswith('lru_'))
SMALL_LATE = tuple(n for n in SMALL if n not in SMALL_EARLY)

NN = (((1,), (0,)), ((), ()))
NT = (((1,), (1,)), ((), ()))
TN = (((0,), (0,)), ((), ()))


def _params(n_grid):
    return pltpu.CompilerParams(dimension_semantics=("arbitrary",) * n_grid, vmem_limit_bytes=VMEM_LIMIT_BYTES)


def _tile(dim, pref, align=LANE):
    if dim <= pref:
        return dim
    t = (pref // align) * align
    while t >= align:
        if dim % t == 0:
            return t
        t -= align
    return dim


def _dot(a, b, dims):
    return lax.dot_general(a.astype(MXU_DTYPE), b.astype(MXU_DTYPE), dims, preferred_element_type=F32)


def _gelu(x):
    k = math.sqrt(2.0 / math.pi)
    return 0.5 * x * (1.0 + jnp.tanh(k * (x + 0.044715 * (x * x * x))))


def _gelu_and_grad(x):
    k = math.sqrt(2.0 / math.pi)
    th = jnp.tanh(k * (x + 0.044715 * (x * x * x)))
    g = 0.5 * x * (1.0 + th)
    dg = 0.5 * (1.0 + th) + 0.5 * x * (1.0 - th * th) * (k * (1.0 + 3.0 * 0.044715 * (x * x)))
    return g, dg


def _neg_expm1(x):
    series = -x * (1.0 + x * (0.5 + x * (1.0 / 6.0 + x * (1.0 / 24.0 + x * (1.0 / 120.0)))))
    return jnp.where(x > -0.01, series, 1.0 - jnp.exp(x))


MESH = pl.DeviceIdType.MESH
N_CHIP = N_DEV // 2
ALL, SAME_CORE = 7, 6


def _place():
    x, y, c = lax.axis_index("x"), lax.axis_index("y"), lax.axis_index("c")
    return x, y, c


def _flip(place, k):
    x, y, c = place
    return (1 - x if (k >> 2) & 1 else x, 1 - y if (k >> 1) & 1 else y, 1 - c if k & 1 else c)


def _chunk_exchange(name, xs, group):
    return _ride_alone(name, _chunk_rider(xs, group))


class _Rider:
    def __init__(self, arrays, out_shape, scratch, start, finish, post, mid=None):
        self.arrays, self.out_shape, self.scratch = list(arrays), list(out_shape), list(scratch)
        self.start, self.finish, self.post, self.mid = start, finish, post, mid


def _chunk_rider(xs, group, rows=None):
    n = len(xs)
    members, r_all, c_ = xs[0].shape
    r0, r = (0, r_all) if rows is None else rows
    assert members == {ALL: N_DEV, SAME_CORE: N_CHIP}[group]
    assert all(a.shape == xs[0].shape and a.dtype == xs[0].dtype for a in xs)
    ks = [k for k in range(1, N_DEV) if not k & ~group]
    member = (lambda p: 4 * p[0] + 2 * p[1] + p[2]) if group == ALL else (lambda p: 2 * p[0] + p[1])

    def copies(ins, outs, scratch):
        out = outs[0]
        send_sems, recv_sems, local_sems = scratch
        place = _place()
        me = member(place)
        src = lambda l, who: ins[l].at[who] if rows is None else ins[l].at[who, pl.ds(r0, r)]
        local = [pltpu.make_async_copy(src(l, me), out.at[me, l], local_sems.at[l]) for l in range(n)]
        remote = []
        for l in range(n):
            for k in ks:
                pid = _flip(place, k)
                peer = member(pid)

                def copy(land_at, l=l, k=k, peer=peer, pid=pid):
                    return pltpu.make_async_remote_copy(
                        src_ref=src(l, peer), dst_ref=out.at[land_at, l], send_sem=send_sems.at[l * N_DEV + k],
                        recv_sem=recv_sems.at[l * N_DEV + k], device_id=pid, device_id_type=MESH)

                remote.append((copy, me, peer))
        return local, remote

    def start(ins, outs, scratch):
        local, remote = copies(ins, outs, scratch)
        for cp in local:
            cp.start()
        for copy, me, _ in remote:
            copy(me).start()

    def finish(ins, outs, scratch):
        local, remote = copies(ins, outs, scratch)
        for copy, me, peer in remote:
            copy(me).wait_send()
            copy(peer).wait_recv()
        for cp in local:
            cp.wait()

    return _Rider(
        xs, [jax.ShapeDtypeStruct((members, n, r, c_), xs[0].dtype)],
        [pltpu.SemaphoreType.DMA((n * N_DEV,)), pltpu.SemaphoreType.DMA((n * N_DEV,)), pltpu.SemaphoreType.DMA((n,))],
        start, finish, lambda outs: outs[0].reshape(members, n * r, c_))


def _gather_rider(xs):
    n = len(xs)
    per = 7

    def plan(ins, outs, scratch):
        send_sems, recv_sems, local_sems = scratch
        x, y, c = place = _place()
        sibling, x_nb, y_nb, diag = (x, y, 1 - c), (1 - x, y, c), (x, 1 - y, c), (1 - x, 1 - y, c)
        relayed = (x + c * (1 - 2 * x), y + (1 - c) * (1 - 2 * y), c)
        onward = (x + (1 - c) * (1 - 2 * x), y + c * (1 - 2 * y), c)
        jobs = []
        for l in range(n):
            slot = lambda p, l=l: outs[l].at[2 * p[0] + p[1], p[2]]

            def copy(k, block, to, src=None, l=l, slot=slot):
                return pltpu.make_async_remote_copy(
                    src_ref=slot(block) if src is None else src, dst_ref=slot(block), send_sem=send_sems.at[l * per + k],
                    recv_sem=recv_sems.at[l * per + k], device_id=to, device_id_type=MESH)

            jobs.append(dict(
                mine=pltpu.make_async_copy(ins[l], slot(place), local_sems.at[l]),
                first=[copy(0, place, sibling, src=ins[l]), copy(1, place, x_nb, src=ins[l]), copy(2, place, y_nb, src=ins[l])],
                landed=[copy(1, x_nb, place), copy(2, y_nb, place)],
                second=[copy(3, relayed, onward), copy(4, x_nb, sibling), copy(5, y_nb, sibling)],
                relay_landed=copy(3, diag, place), last=copy(6, diag, sibling),
                from_sibling=[copy(0, sibling, place)] + [copy(4 + j, (p[0], p[1], 1 - c), place)
                                                          for j, p in enumerate((x_nb, y_nb, diag))]))
        return jobs

    def start(ins, outs, scratch):
        for job in plan(ins, outs, scratch):
            job['mine'].start()
            for cp in job['first']:
                cp.start()

    def mid(ins, outs, scratch):
        for job in plan(ins, outs, scratch):
            for cp in job['landed']:
                cp.wait_recv()
            for cp in job['second']:
                cp.start()

    def finish(ins, outs, scratch):
        jobs = plan(ins, outs, scratch)
        for job in jobs:
            job['relay_landed'].wait_recv()
            job['last'].start()
        for job in jobs:
            for cp in job['from_sibling']:
                cp.wait_recv()
            for cp in job['first'] + job['second'] + [job['last']]:
                cp.wait_send()
            job['mine'].wait()

    return _Rider(
        xs, [jax.ShapeDtypeStruct((N_CHIP, 2) + x.shape, x.dtype) for x in xs],
        [pltpu.SemaphoreType.DMA((n * per,)), pltpu.SemaphoreType.DMA((n * per,)), pltpu.SemaphoreType.DMA((n,))],
        start, finish, lambda outs: [o.reshape((N_DEV,) + x.shape) for o, x in zip(outs, xs)], mid=mid)


HBM_SPEC = pl.BlockSpec(memory_space=pltpu.HBM)


def _ride_alone(name, rider):
    n_in, n_out = len(rider.arrays), len(rider.out_shape)

    def body(*refs):
        parts = refs[:n_in], refs[n_in:n_in + n_out], refs[n_in + n_out:]
        rider.start(*parts)
        if rider.mid is not None:
            rider.mid(*parts)
        rider.finish(*parts)

    outs = pl.pallas_call(body, name=name, out_shape=rider.out_shape, in_specs=[HBM_SPEC] * n_in,
                          out_specs=[HBM_SPEC] * n_out, scratch_shapes=rider.scratch)(*rider.arrays)
    return rider.post(list(outs))


def _call(body, *, name, grid, in_specs, out_specs, out_shape, scratch_shapes=(), args, rider=None):
    single = not isinstance(out_shape, (list, tuple))
    out_shape = [out_shape] if single else list(out_shape)
    out_specs = [out_specs] if single else list(out_specs)
    scratch_shapes = list(scratch_shapes)
    unwrap = lambda outs: outs[0] if single else list(outs)
    if rider is None:
        outs = pl.pallas_call(body, name=name, grid=grid, in_specs=list(in_specs), out_specs=out_specs, out_shape=out_shape,
                              scratch_shapes=scratch_shapes, compiler_params=_params(len(grid)))(*args)
        return unwrap(outs)
    n_in, n_out, n_scr = len(in_specs), len(out_shape), len(scratch_shapes)
    r_in, r_out = len(rider.arrays), len(rider.out_shape)

    def carried(*refs):
        ins, refs = refs[:n_in], refs[n_in:]
        r_ins, refs = refs[:r_in], refs[r_in:]
        outs, refs = refs[:n_out], refs[n_out:]
        r_outs, refs = refs[:r_out], refs[r_out:]
        scr, r_scr = refs[:n_scr], refs[n_scr:]
        step = 0
        for ax, g in enumerate(grid):
            step = step * g + pl.program_id(ax)

        @pl.when(step == 0)
        def _():
            rider.start(r_ins, r_outs, r_scr)

        body(*ins, *outs, *scr)

        if rider.mid is not None and total > 1:
            @pl.when(step == (5 * (total - 1)) // 8)
            def _():
                rider.mid(r_ins, r_outs, r_scr)

        @pl.when(step == total - 1)
        def _():
            if rider.mid is not None and total == 1:
                rider.mid(r_ins, r_outs, r_scr)
            rider.finish(r_ins, r_outs, r_scr)

    total = math.prod(grid)

    outs = pl.pallas_call(
        carried, name=name, grid=grid, in_specs=list(in_specs) + [HBM_SPEC] * r_in, out_specs=out_specs + [HBM_SPEC] * r_out,
        out_shape=out_shape + rider.out_shape, scratch_shapes=scratch_shapes + rider.scratch,
        compiler_params=_params(len(grid)))(*args, *rider.arrays)
    return unwrap(outs[:n_out]), rider.post(list(outs[n_out:]))


def _all_gather(name, x):
    return _ride_alone(name, _gather_rider([x]))[0]


def _sibling_rider(x):
    _, r, c_ = x.shape

    def copies(ins, outs, scratch):
        send_sems, recv_sems = scratch
        place = _place()
        return [pltpu.make_async_remote_copy(
            src_ref=ins[0].at[2 * chip + (1 - place[2])], dst_ref=outs[0].at[chip], send_sem=send_sems.at[chip],
            recv_sem=recv_sems.at[chip], device_id=_flip(place, 1), device_id_type=MESH) for chip in range(N_CHIP)]

    def start(ins, outs, scratch):
        for cp in copies(ins, outs, scratch):
            cp.start()

    def finish(ins, outs, scratch):
        for cp in copies(ins, outs, scratch):
            cp.wait()

    return _Rider([x], [jax.ShapeDtypeStruct((N_CHIP, r, c_), x.dtype)],
                  [pltpu.SemaphoreType.DMA((N_CHIP,)), pltpu.SemaphoreType.DMA((N_CHIP,))], start, finish, lambda outs: outs[0])


def _join(riders):
    def cut(seq, counts):
        out, off = [], 0
        for k in counts:
            out.append(seq[off:off + k])
            off += k
        return out

    def parts(ins, outs, scratch):
        return zip(riders, cut(ins, [len(r.arrays) for r in riders]), cut(outs, [len(r.out_shape) for r in riders]),
                   cut(scratch, [len(r.scratch) for r in riders]))

    def start(ins, outs, scratch):
        for r, i, o, s in parts(ins, outs, scratch):
            r.start(i, o, s)

    def finish(ins, outs, scratch):
        for r, i, o, s in parts(ins, outs, scratch):
            r.finish(i, o, s)

    def mid(ins, outs, scratch):
        for r, i, o, s in parts(ins, outs, scratch):
            if r.mid is not None:
                r.mid(i, o, s)

    return _Rider(
        [a for r in riders for a in r.arrays], [o for r in riders for o in r.out_shape], [s for r in riders for s in r.scratch],
        start, finish, lambda outs: [r.post(o) for r, o in zip(riders, cut(outs, [len(r.out_shape) for r in riders]))],
        mid=mid if any(r.mid is not None for r in riders) else None)


def _ride(fn, *args, riders, **kw):
    return fn(*args, rider=_join(riders), **kw)


def _pair_sum(name, x, got, core):
    _, r, c_ = x.shape
    br = _tile(r, max(256, (1 << 21) // c_), 2 * SUBLANE)

    def body(core_ref, x_ref, g_ref, o_ref):
        o_ref[...] = (x_ref[...].astype(F32) + g_ref[...].astype(F32)).astype(o_ref.dtype)

    return pl.pallas_call(
        body, name=name, out_shape=jax.ShapeDtypeStruct((N_CHIP, r, c_), x.dtype),
        grid_spec=pltpu.PrefetchScalarGridSpec(
            num_scalar_prefetch=1, grid=(N_CHIP, r // br),
            in_specs=[pl.BlockSpec((None, br, c_), lambda ch, i, core_ref: (2 * ch + core_ref[0], i, 0)),
                      pl.BlockSpec((None, br, c_), lambda ch, i, core_ref: (ch, i, 0))],
            out_specs=pl.BlockSpec((None, br, c_), lambda ch, i, core_ref: (ch, i, 0))),
        compiler_params=_params(2))(core, x, got)


def _mm(name, a, b, out_shape, out_dtype, grid, a_spec, b_spec, o_spec, dims, n_red, acc_shape, rider=None):
    red = tuple(range(len(grid) - n_red, len(grid)))
    out_type = jax.ShapeDtypeStruct(out_shape, out_dtype)
    if all(grid[ax] == 1 for ax in red):
        def single(a_ref, b_ref, o_ref):
            o_ref[...] = _dot(a_ref[...], b_ref[...], dims).astype(o_ref.dtype)

        return _call(single, name=name, out_shape=out_type, grid=grid, in_specs=[a_spec, b_spec], out_specs=o_spec,
                     args=(a, b), rider=rider)

    def body(a_ref, b_ref, o_ref, acc_ref):
        first = functools.reduce(jnp.logical_and, [pl.program_id(ax) == 0 for ax in red])
        last = functools.reduce(jnp.logical_and, [pl.program_id(ax) == grid[ax] - 1 for ax in red])

        @pl.when(first)
        def _():
            acc_ref[...] = jnp.zeros_like(acc_ref)

        acc_ref[...] += _dot(a_ref[...], b_ref[...], dims)

        @pl.when(last)
        def _():
            o_ref[...] = acc_ref[...].astype(o_ref.dtype)

    return _call(body, name=name, out_shape=out_type, grid=grid, in_specs=[a_spec, b_spec], out_specs=o_spec,
                 scratch_shapes=[pltpu.VMEM(acc_shape, F32)], args=(a, b), rider=rider)


def _mm_col(name, a, b, out_dtype=F32, rider=None):
    m, k = a.shape
    j, _, n = b.shape
    bm, bk = _tile(m, 1024), _tile(k, 2048)
    return _mm(name, a, b, (j, m, n), out_dtype, (j, m // bm, k // bk),
               pl.BlockSpec((bm, bk), lambda jj, mm, kk: (mm, kk)),
               pl.BlockSpec((None, bk, n), lambda jj, mm, kk: (jj, kk, 0)),
               pl.BlockSpec((None, bm, n), lambda jj, mm, kk: (jj, mm, 0)), NN, 1, (bm, n), rider)


def _mm_col_da(name, do, b, rider=None, out_dtype=F32):
    j, m, n = do.shape
    k = b.shape[1]
    bm, bk = _tile(m, 1024), _tile(k, 1024)
    return _mm(name, do, b, (m, k), out_dtype, (m // bm, k // bk, j),
               pl.BlockSpec((None, bm, n), lambda mm, kk, jj: (jj, mm, 0)),
               pl.BlockSpec((None, bk, n), lambda mm, kk, jj: (jj, kk, 0)),
               pl.BlockSpec((bm, bk), lambda mm, kk, jj: (mm, kk)), NT, 1, (bm, bk), rider)


def _mm_col_db(name, a, do, out_dtype, rider=None):
    m, k = a.shape
    j, _, n = do.shape
    bm, bk = _tile(m, 2048), _tile(k, 512)
    return _mm(name, a, do, (j, k, n), out_dtype, (j, k // bk, m // bm),
               pl.BlockSpec((bm, bk), lambda jj, kk, mm: (mm, kk)),
               pl.BlockSpec((None, bm, n), lambda jj, kk, mm: (jj, mm, 0)),
               pl.BlockSpec((None, bk, n), lambda jj, kk, mm: (jj, kk, 0)), TN, 1, (bk, n), rider)


def _row_bk(kq):
    return kq if (kq % LANE or kq // LANE in (11,)) else _tile(kq, 2048)


def _mm_row(name, a, b, out_dtype=F32, rider=None):
    q, m, kq = a.shape
    n = b.shape[1]
    bm, bn, bk = _tile(m, 1024), _tile(n, 1024), _row_bk(kq)
    nk = kq // bk
    return _mm(name, a, b, (m, n), out_dtype, (m // bm, n // bn, q, nk),
               pl.BlockSpec((None, bm, bk), lambda mm, nn, qq, kk: (qq, mm, kk)),
               pl.BlockSpec((bk, bn), lambda mm, nn, qq, kk: (qq * nk + kk, nn)),
               pl.BlockSpec((bm, bn), lambda mm, nn, qq, kk: (mm, nn)), NN, 2, (bm, bn), rider)


def _mm_row_da(name, do, b, q, rider=None, out_dtype=F32):
    m, n = do.shape
    kq = b.shape[0] // q
    bm, bn = _tile(m, 1024), _tile(n, 2048)
    return _mm(name, do, b, (q, m, kq), out_dtype, (q, m // bm, n // bn),
               pl.BlockSpec((bm, bn), lambda qq, mm, nn: (mm, nn)),
               pl.BlockSpec((kq, bn), lambda qq, mm, nn: (qq, nn)),
               pl.BlockSpec((None, bm, kq), lambda qq, mm, nn: (qq, mm, 0)), NT, 1, (bm, kq), rider)


def _mm_row_db(name, a, do, out_dtype):
    q, m, kq = a.shape
    n = do.shape[1]
    bm, bn = _tile(m, 2048), _tile(n, 512)
    return _mm(name, a, do, (q * kq, n), out_dtype, (q, n // bn, m // bm),
               pl.BlockSpec((None, bm, kq), lambda qq, nn, mm: (qq, mm, 0)),
               pl.BlockSpec((bm, bn), lambda qq, nn, mm: (mm, nn)),
               pl.BlockSpec((kq, bn), lambda qq, nn, mm: (qq, nn)), TN, 1, (kq, bn))


def _row_spec(bm, d):
    return pl.BlockSpec((bm, d), lambda i: (i, 0))


def _vec_spec(d):
    return pl.BlockSpec((1, d), lambda i: (0, 0))


def _norm_mod_fwd(name, x, gain, sc, sh):
    t, d = x.shape
    bm = _tile(t, 256, SUBLANE)

    def body(x_ref, g_ref, sc_ref, sh_ref, h_ref):
        xv = x_ref[...]
        rstd = lax.rsqrt(jnp.mean(xv * xv, axis=-1, keepdims=True) + EPS)
        h_ref[...] = ((xv * rstd) * g_ref[...] * (1.0 + sc_ref[...]) + sh_ref[...]).astype(h_ref.dtype)

    return pl.pallas_call(
        body, name=name, out_shape=jax.ShapeDtypeStruct((t, d), MXU_DTYPE), grid=(t // bm,),
        in_specs=[_row_spec(bm, d), _vec_spec(d), _vec_spec(d), _vec_spec(d)], out_specs=_row_spec(bm, d),
        compiler_params=_params(1))(x, gain, sc, sh)


def _norm_mod_bwd(name, x, dh, dres, gain, sc, rider=None):
    t, d = x.shape
    bm = _tile(t, 256, SUBLANE)

    def body(x_ref, dh_ref, dres_ref, g_ref, sc_ref, dx_ref, dg_ref, dsc_ref, dsh_ref):
        @pl.when(pl.program_id(0) == 0)
        def _():
            dg_ref[...] = jnp.zeros_like(dg_ref)
            dsc_ref[...] = jnp.zeros_like(dsc_ref)
            dsh_ref[...] = jnp.zeros_like(dsh_ref)

        xv, dh_ = x_ref[...], dh_ref[...]
        rstd = lax.rsqrt(jnp.mean(xv * xv, axis=-1, keepdims=True) + EPS)
        nrm = xv * rstd
        gain_ = g_ref[...]
        dsh_ref[...] += jnp.sum(dh_, axis=0, keepdims=True)
        dsc_ref[...] += jnp.sum(dh_ * (nrm * gain_), axis=0, keepdims=True)
        dhn = dh_ * (1.0 + sc_ref[...])
        dg_ref[...] += jnp.sum(dhn * nrm, axis=0, keepdims=True)
        dn = dhn * gain_
        dx_ref[...] = dres_ref[...] + rstd * (dn - nrm * jnp.mean(dn * nrm, axis=-1, keepdims=True))

    vec = jax.ShapeDtypeStruct((1, d), F32)
    return _call(
        body, name=name, out_shape=[jax.ShapeDtypeStruct((t, d), F32), vec, vec, vec], grid=(t // bm,),
        in_specs=[_row_spec(bm, d), _row_spec(bm, d), _row_spec(bm, d), _vec_spec(d), _vec_spec(d)],
        out_specs=[_row_spec(bm, d), _vec_spec(d), _vec_spec(d), _vec_spec(d)],
        args=(x, dh, dres, gain, sc), rider=rider)


def _loss_bwd(name, x, target, gain):
    t, d = x.shape
    bm = _tile(t, 256, SUBLANE)

    def body(x_ref, t_ref, g_ref, dx_ref, loss_ref, dg_ref):
        @pl.when(pl.program_id(0) == 0)
        def _():
            loss_ref[...] = jnp.zeros_like(loss_ref)
            dg_ref[...] = jnp.zeros_like(dg_ref)

        xv = x_ref[...]
        rstd = lax.rsqrt(jnp.mean(xv * xv, axis=-1, keepdims=True) + EPS)
        nrm = xv * rstd
        gain_ = g_ref[...]
        err = nrm * gain_ - t_ref[...]
        per_tok = jnp.mean(err * err, axis=-1, keepdims=True)
        loss_ref[...] += 0.5 * jnp.sum(per_tok, axis=0, keepdims=True)
        dout = err * (1.0 / d)
        dg_ref[...] += jnp.sum(dout * nrm, axis=0, keepdims=True)
        dn = dout * gain_
        dx_ref[...] = rstd * (dn - nrm * jnp.mean(dn * nrm, axis=-1, keepdims=True))

    return pl.pallas_call(
        body, name=name,
        out_shape=[jax.ShapeDtypeStruct((t, d), F32), jax.ShapeDtypeStruct((1, 1), F32),
                   jax.ShapeDtypeStruct((1, d), F32)],
        grid=(t // bm,), in_specs=[_row_spec(bm, d), _row_spec(bm, d), _vec_spec(d)],
        out_specs=[_row_spec(bm, d), pl.BlockSpec((1, 1), lambda i: (0, 0)), _vec_spec(d)],
        compiler_params=_params(1))(x, target, gain)


def _resid(name, x, y, g):
    t, d = x.shape
    bm = _tile(t, 256, SUBLANE)

    def body(x_ref, y_ref, g_ref, o_ref):
        o_ref[...] = x_ref[...] + g_ref[...] * y_ref[...]

    return pl.pallas_call(
        body, name=name, out_shape=jax.ShapeDtypeStruct((t, d), F32), grid=(t // bm,),
        in_specs=[_row_spec(bm, d), _row_spec(bm, d), _vec_spec(d)], out_specs=_row_spec(bm, d),
        compiler_params=_params(1))(x, y, g)


def _gate_bwd(name, dx, y, g):
    t, d = dx.shape
    bm = _tile(t, 256, SUBLANE)

    def body(dx_ref, y_ref, g_ref, dy_ref, dg_ref):
        @pl.when(pl.program_id(0) == 0)
        def _():
            dg_ref[...] = jnp.zeros_like(dg_ref)

        dxv = dx_ref[...]
        dy_ref[...] = (g_ref[...] * dxv).astype(dy_ref.dtype)
        dg_ref[...] += jnp.sum(dxv * y_ref[...], axis=0, keepdims=True)

    return pl.pallas_call(
        body, name=name, out_shape=[jax.ShapeDtypeStruct((t, d), MXU_DTYPE), jax.ShapeDtypeStruct((1, d), F32)],
        grid=(t // bm,), in_specs=[_row_spec(bm, d), _row_spec(bm, d), _vec_spec(d)],
        out_specs=[_row_spec(bm, d), _vec_spec(d)], compiler_params=_params(1))(dx, y, g)


def _glu_resid_fwd(name, z, x, g):
    _, t, n = z.shape
    d = x.shape[1]
    half = N_DEV // 2
    bm = _tile(t, 256, SUBLANE)

    def body(v_ref, gt_ref, x_ref, g_ref, o_ref):
        o_ref[...] = x_ref[...] + g_ref[...] * (v_ref[...] * jax.nn.sigmoid(gt_ref[...]))

    return pl.pallas_call(
        body, name=name, out_shape=jax.ShapeDtypeStruct((t, d), F32), grid=(half, t // bm),
        in_specs=[pl.BlockSpec((None, bm, n), lambda q, i: (q, i, 0)),
                  pl.BlockSpec((None, bm, n), lambda q, i: (q + half, i, 0)),
                  pl.BlockSpec((bm, n), lambda q, i: (i, q)), pl.BlockSpec((1, n), lambda q, i: (0, q))],
        out_specs=pl.BlockSpec((bm, n), lambda q, i: (i, q)), compiler_params=_params(2))(z, z, x, g)


def _glu_resid_bwd(name, z, dx, g, rider=None):
    _, t, n = z.shape
    d = dx.shape[1]
    half = N_DEV // 2
    bm = _tile(t, 256, SUBLANE)

    def body(z_ref, dx_ref, g_ref, dz_ref, dg_ref):
        @pl.when(pl.program_id(1) == 0)
        def _():
            dg_ref[...] = jnp.zeros_like(dg_ref)

        v, dxv = z_ref[0], dx_ref[...]
        sig = jax.nn.sigmoid(z_ref[1])
        dout = g_ref[...] * dxv
        dg_ref[...] += jnp.sum(dxv * (v * sig), axis=0, keepdims=True)
        dz_ref[0] = (dout * sig).astype(dz_ref.dtype)
        dz_ref[1] = (dout * v * (sig * (1.0 - sig))).astype(dz_ref.dtype)

    pair = pl.BlockSpec((2, None, bm, n), lambda q, i: (0, q, i, 0))
    res = _call(
        body, name=name,
        out_shape=[jax.ShapeDtypeStruct((2, half, t, n), MXU_DTYPE), jax.ShapeDtypeStruct((1, d), F32)],
        grid=(half, t // bm),
        in_specs=[pair, pl.BlockSpec((bm, n), lambda q, i: (i, q)), pl.BlockSpec((1, n), lambda q, i: (0, q))],
        out_specs=[pair, pl.BlockSpec((1, n), lambda q, i: (0, q))],
        args=(z.reshape(2, half, t, n), dx, g), rider=rider)
    (dz, dg), got = res if rider is not None else (res, None)
    dz = dz.reshape(N_DEV, t, n)
    return ((dz, dg), got) if rider is not None else (dz, dg)


def _swiglu_act_fwd(name, gu):
    _, t, n = gu.shape
    half = N_DEV // 2
    bm = _tile(t, 256, SUBLANE)

    def body(g_ref, u_ref, o_ref):
        gv = g_ref[...].astype(F32)
        o_ref[...] = (gv * jax.nn.sigmoid(gv) * u_ref[...].astype(F32)).astype(o_ref.dtype)

    return pl.pallas_call(
        body, name=name, out_shape=jax.ShapeDtypeStruct((half, t, n), MXU_DTYPE), grid=(half, t // bm),
        in_specs=[pl.BlockSpec((None, bm, n), lambda q, i: (q, i, 0)),
                  pl.BlockSpec((None, bm, n), lambda q, i: (q + half, i, 0))],
        out_specs=pl.BlockSpec((None, bm, n), lambda q, i: (q, i, 0)), compiler_params=_params(2))(gu, gu)


def _swiglu_act_bwd(name, gu, dact, rider=None):
    _, t, n = gu.shape
    half = N_DEV // 2
    bm = _tile(t, 256, SUBLANE)

    def body(gu_ref, da_ref, o_ref):
        gv, da = gu_ref[0].astype(F32), da_ref[...].astype(F32)
        sig = jax.nn.sigmoid(gv)
        o_ref[0] = (da * gu_ref[1].astype(F32) * (sig * (1.0 + gv * (1.0 - sig)))).astype(o_ref.dtype)
        o_ref[1] = (da * (gv * sig)).astype(o_ref.dtype)

    pair = pl.BlockSpec((2, None, bm, n), lambda q, i: (0, q, i, 0))
    res = _call(
        body, name=name, out_shape=jax.ShapeDtypeStruct((2, half, t, n), MXU_DTYPE), grid=(half, t // bm),
        in_specs=[pair, pl.BlockSpec((None, bm, n), lambda q, i: (q, i, 0))], out_specs=pair,
        args=(gu.reshape(2, half, t, n), dact), rider=rider)
    if rider is None:
        return res.reshape(N_DEV, t, n)
    return res[0].reshape(N_DEV, t, n), res[1]


def _ada_fwd(name, c16, w_ada, b_loc, rider=None):
    nl, d, n = w_ada.shape
    bn = _tile(n, 512)

    def body(c_ref, w_ref, b_ref, o_ref):
        cv = c_ref[...]
        o_ref[...] = _dot(cv * jax.nn.sigmoid(cv), w_ref[...], NN) + b_ref[...]

    return _call(
        body, name=name, out_shape=jax.ShapeDtypeStruct((nl, c16.shape[0], n), F32), grid=(nl, n // bn),
        in_specs=[pl.BlockSpec(c16.shape, lambda i, j: (0, 0)), pl.BlockSpec((None, d, bn), lambda i, j: (i, 0, j)),
                  pl.BlockSpec((None, 1, bn), lambda i, j: (i, 0, j))],
        out_specs=pl.BlockSpec((None, c16.shape[0], bn), lambda i, j: (i, 0, j)),
        args=(c16, w_ada, b_loc), rider=rider)


def _adam_update(g, w, m, v):
    m = ADAM_B1 * m + (1.0 - ADAM_B1) * g
    v = ADAM_B2 * v + (1.0 - ADAM_B2) * (g * g)
    m_hat = m / (1.0 - ADAM_B1 ** ADAM_STEP)
    v_hat = v / (1.0 - ADAM_B2 ** ADAM_STEP)
    delta = -ADAM_LR * (m_hat / (jnp.sqrt(v_hat) + ADAM_EPS) + ADAM_WD * w)
    return delta, m, v


def _adamw_w_ada(name, c16, dmod16, w, m, v, rider=None):
    nl, d, n = w.shape
    br = _tile(d, 256)

    def body(c_ref, dm_ref, w_ref, m_ref, v_ref, g_ref, dl_ref, mo_ref, vo_ref):
        cv = c_ref[...]
        g = _dot(cv * jax.nn.sigmoid(cv), dm_ref[...], TN)
        g_ref[...] = g
        dl_ref[...], mo_ref[...], vo_ref[...] = _adam_update(g, w_ref[...], m_ref[...], v_ref[...])

    blk = pl.BlockSpec((None, br, n), lambda i, r: (i, r, 0))
    shp = jax.ShapeDtypeStruct(w.shape, F32)
    return _call(
        body, name=name, out_shape=[shp] * 4, grid=(nl, d // br),
        in_specs=[pl.BlockSpec((c16.shape[0], br), lambda i, r: (0, r)),
                  pl.BlockSpec((None, dmod16.shape[1], n), lambda i, r: (i, 0, 0)), blk, blk, blk],
        out_specs=[blk] * 4, args=(c16, dmod16, w, m, v), rider=rider)


def _adamw_sum(name, parts, w, m, v, rider=None):
    nl = len(parts)
    p, r, c = parts[0].shape
    br = _tile(r, max(128, (1 << 17) // max(c, LANE)), 2 * SUBLANE)
    nb = r // br

    def body(*refs):
        p_refs, (w_ref, m_ref, v_ref, g_ref, dl_ref, mo_ref, vo_ref) = refs[:nl], refs[nl:]
        for l, p_ref in enumerate(p_refs):
            @pl.when(pl.program_id(0) == l)
            def _(p_ref=p_ref):
                gl = p_ref[0].astype(F32)
                for s in range(1, p):
                    gl = gl + p_ref[s].astype(F32)
                g_ref[...] = gl

        g = g_ref[...]
        dl_ref[...], mo_ref[...], vo_ref[...] = _adam_update(g, w_ref[...], m_ref[...], v_ref[...])

    blk = pl.BlockSpec((br, c), lambda l, i: (l * nb + i, 0))
    shp = jax.ShapeDtypeStruct((nl * r, c), F32)
    return _call(
        body, name=name, out_shape=[shp] * 4, grid=(nl, nb),
        in_specs=[pl.BlockSpec((p, br, c), lambda l, i, k=k: (0, jnp.where(l == k, i, 0), 0)) for k in range(nl)]
        + [blk, blk, blk], out_specs=[blk] * 4,
        args=(*parts, w, m, v), rider=rider)


def _sum_parts(name, parts):
    p, r, c = parts.shape

    def body(p_ref, o_ref):
        g = p_ref[0]
        for s in range(1, p):
            g = g + p_ref[s]
        o_ref[...] = g

    return pl.pallas_call(body, name=name, out_shape=jax.ShapeDtypeStruct((r, c), F32))(parts)


def _s5_disc(name, lam_re, lam_im, log_dt, b_re, b_im):
    def body(lr_ref, li_ref, ld_ref, br_ref, bi_ref, ar_ref, ai_ref, bbr_ref, bbi_ref):
        lr, li = lr_ref[...], li_ref[...]
        dt = jnp.exp(ld_ref[...])
        mag = jnp.exp(lr * dt)
        a_re, a_im = mag * jnp.cos(li * dt), mag * jnp.sin(li * dt)
        nr, ni = a_re - 1.0, a_im
        den = lr * lr + li * li
        f_re, f_im = (nr * lr + ni * li) / den, (ni * lr - nr * li) / den
        br, bi = br_ref[...], bi_ref[...]
        ar_ref[...], ai_ref[...] = a_re, a_im
        bbr_ref[...] = f_re * br - f_im * bi
        bbi_ref[...] = f_re * bi + f_im * br

    s_a, s_b = jax.ShapeDtypeStruct(lam_re.shape, F32), jax.ShapeDtypeStruct(b_re.shape, F32)
    return pl.pallas_call(body, name=name, out_shape=[s_a, s_a, s_b, s_b])(lam_re, lam_im, log_dt, b_re, b_im)


def _s5_disc_bwd(name, lam_re, lam_im, log_dt, b_re, b_im, dab_re, dab_im, dbb_re, dbb_im):
    def body(lr_ref, li_ref, ld_ref, br_ref, bi_ref, dar_ref, dai_ref, dbbr_ref, dbbi_ref,
             dlr_ref, dli_ref, dld_ref, dbr_ref, dbi_ref):
        lr, li = lr_ref[...], li_ref[...]
        dt = jnp.exp(ld_ref[...])
        mag = jnp.exp(lr * dt)
        a_re, a_im = mag * jnp.cos(li * dt), mag * jnp.sin(li * dt)
        nr, ni = a_re - 1.0, a_im
        den = lr * lr + li * li
        f_re, f_im = (nr * lr + ni * li) / den, (ni * lr - nr * li) / den
        br, bi = br_ref[...], bi_ref[...]
        dbbr, dbbi = dbbr_ref[...], dbbi_ref[...]
        dbr_ref[...] = f_re * dbbr + f_im * dbbi
        dbi_ref[...] = f_re * dbbi - f_im * dbbr
        df_re = jnp.sum(dbbr * br + dbbi * bi, axis=1, keepdims=True)
        df_im = jnp.sum(dbbi * br - dbbr * bi, axis=1, keepdims=True)
        dnr = (df_re * lr - df_im * li) / den
        dni = (df_re * li + df_im * lr) / den
        dden = -(df_re * f_re + df_im * f_im) / den
        dlr = (df_re * nr + df_im * ni) / den + 2.0 * lr * dden
        dli = (df_re * ni - df_im * nr) / den + 2.0 * li * dden
        da_re, da_im = dar_ref[...] + dnr, dai_ref[...] + dni
        dmag_mag = da_re * a_re + da_im * a_im
        dth = da_im * a_re - da_re * a_im
        dlr_ref[...] = dlr + dmag_mag * dt
        dli_ref[...] = dli + dth * dt
        ddt = jnp.sum(dmag_mag * lr + dth * li, axis=2, keepdims=True)
        dld_ref[...] = ddt * dt

    s_a, s_b = jax.ShapeDtypeStruct(lam_re.shape, F32), jax.ShapeDtypeStruct(b_re.shape, F32)
    return pl.pallas_call(
        body, name=name, out_shape=[s_a, s_a, jax.ShapeDtypeStruct(log_dt.shape, F32), s_b, s_b],
    )(lam_re, lam_im, log_dt, b_re, b_im, dab_re, dab_im, dbb_re, dbb_im)


def _s5_time_block(t):
    return _tile(t, 128, SUBLANE)


def _s5_scan_fwd(name, u, bb_re, bb_im, ab_re, ab_im, cc_re, cc_im, dskip, rider=None):
    t, d = u.shape
    nsg, cs, ns = bb_re.shape
    tb = _s5_time_block(t)

    def body(u_ref, bbr_hbm, bbi_hbm, ar_ref, ai_ref, ccr_hbm, cci_hbm, d_ref, sr_ref, si_ref, srm_ref, sim_ref, yp_ref,
             ya_ref, bbr, bbi, ccr, cci, cr_ref, ci_ref):
        @pl.when(pl.program_id(0) == 0)
        def _():
            pltpu.sync_copy(bbr_hbm, bbr)
            pltpu.sync_copy(bbi_hbm, bbi)
            pltpu.sync_copy(ccr_hbm, ccr)
            pltpu.sync_copy(cci_hbm, cci)
            cr_ref[...] = jnp.zeros_like(cr_ref)
            ci_ref[...] = jnp.zeros_like(ci_ref)

        for sg in range(nsg):
            us = u_ref[:, sg * cs:(sg + 1) * cs]
            sr_ref[:, sg, :] = _dot(us, bbr[sg], NN)
            si_ref[:, sg, :] = _dot(us, bbi[sg], NN)
        ar, ai = ar_ref[...], ai_ref[...]

        def step(i, carry):
            cr, ci = carry
            nr = ar * cr - ai * ci + sr_ref[i]
            ni = ar * ci + ai * cr + si_ref[i]
            sr_ref[i] = nr
            si_ref[i] = ni
            return nr, ni

        cr, ci = lax.fori_loop(0, tb, step, (cr_ref[...], ci_ref[...]), unroll=2)
        cr_ref[...], ci_ref[...] = cr, ci
        srm_ref[...] = jnp.swapaxes(sr_ref[...], 0, 1).astype(MXU_DTYPE)
        sim_ref[...] = jnp.swapaxes(si_ref[...], 0, 1).astype(MXU_DTYPE)
        for sg in range(nsg):
            cols = slice(sg * cs, (sg + 1) * cs)
            y = _dot(srm_ref[sg], ccr[sg], NN) - _dot(sim_ref[sg], cci[sg], NN) + d_ref[:, cols] * u_ref[:, cols]
            yp_ref[:, cols] = y
            ya_ref[:, cols] = _gelu(y).astype(ya_ref.dtype)

    scan = jax.ShapeDtypeStruct((t, nsg, ns), F32)
    mxu = jax.ShapeDtypeStruct((nsg, t, ns), MXU_DTYPE)
    hbm = pl.BlockSpec(memory_space=pltpu.HBM)
    full = pl.BlockSpec((nsg, ns), lambda i: (0, 0))
    return _call(
        body, name=name,
        out_shape=[scan, scan, mxu, mxu, jax.ShapeDtypeStruct((t, d), F32), jax.ShapeDtypeStruct((t, d), MXU_DTYPE)],
        grid=(t // tb,), in_specs=[_row_spec(tb, d), hbm, hbm, full, full, hbm, hbm, _vec_spec(d)],
        out_specs=[pl.BlockSpec((tb, nsg, ns), lambda i: (i, 0, 0))] * 2 + [pl.BlockSpec((nsg, tb, ns), lambda i: (0, i, 0))] * 2
        + [_row_spec(tb, d)] * 2,
        scratch_shapes=[pltpu.VMEM(bb_re.shape, bb_re.dtype), pltpu.VMEM(bb_im.shape, bb_im.dtype),
                        pltpu.VMEM(cc_re.shape, cc_re.dtype), pltpu.VMEM(cc_im.shape, cc_im.dtype),
                        pltpu.VMEM((nsg, ns), F32), pltpu.VMEM((nsg, ns), F32)],
        args=(u, bb_re, bb_im, ab_re, ab_im, cc_re, cc_im, dskip), rider=rider)


def _s5_scan_bwd(name, dyact, ypre, cc_re, cc_im, ab_re, ab_im, s_re, s_im, rider=None):
    t, d = dyact.shape
    nsg, ns, cs = cc_re.shape
    tb = _s5_time_block(t)
    nb = t // tb

    def body(dya_ref, yp_ref, ccr_hbm, cci_hbm, ar_ref, ai_ref, sr_ref, si_ref, dy_ref, lrm_ref, lim_ref, dar_ref, dai_ref,
             ccr, cci, lr_ref, li_ref, cr_ref, ci_ref):
        dy_ref[...] = (dya_ref[...].astype(F32) * _gelu_and_grad(yp_ref[...])[1]).astype(dy_ref.dtype)

        @pl.when(pl.program_id(0) == 0)
        def _():
            pltpu.sync_copy(ccr_hbm, ccr)
            pltpu.sync_copy(cci_hbm, cci)
            cr_ref[...] = jnp.zeros_like(cr_ref)
            ci_ref[...] = jnp.zeros_like(ci_ref)
            dar_ref[...] = jnp.zeros_like(dar_ref)
            dai_ref[...] = jnp.zeros_like(dai_ref)

        for sg in range(nsg):
            dys = dy_ref[:, sg * cs:(sg + 1) * cs]
            lr_ref[:, sg, :] = _dot(dys, ccr[sg], NT)
            li_ref[:, sg, :] = -_dot(dys, cci[sg], NT)
        ar, ai = ar_ref[...], ai_ref[...]

        def step(i, carry):
            cr, ci, dar, dai = carry
            j = tb - 1 - i
            sr, si = sr_ref[j], si_ref[j]
            dar = dar + (cr * sr + ci * si)
            dai = dai + (ci * sr - cr * si)
            nr = lr_ref[j] + (ar * cr + ai * ci)
            ni = li_ref[j] + (ar * ci - ai * cr)
            lr_ref[j] = nr
            li_ref[j] = ni
            return nr, ni, dar, dai

        cr, ci, dar, dai = lax.fori_loop(0, tb, step, (cr_ref[...], ci_ref[...], dar_ref[...], dai_ref[...]))
        cr_ref[...], ci_ref[...] = cr, ci
        dar_ref[...], dai_ref[...] = dar, dai
        lrm_ref[...] = jnp.swapaxes(lr_ref[...], 0, 1).astype(MXU_DTYPE)
        lim_ref[...] = jnp.swapaxes(li_ref[...], 0, 1).astype(MXU_DTYPE)

    hbm = pl.BlockSpec(memory_space=pltpu.HBM)
    full = pl.BlockSpec((nsg, ns), lambda i: (0, 0))
    mxu = jax.ShapeDtypeStruct((nsg, t, ns), MXU_DTYPE)
    acc = jax.ShapeDtypeStruct((nsg, ns), F32)
    scan_spec = pl.BlockSpec((tb, nsg, ns), lambda i: (nb - 1 - i, 0, 0))
    rows = pl.BlockSpec((tb, d), lambda i: (nb - 1 - i, 0))
    return _call(
        body, name=name, out_shape=[jax.ShapeDtypeStruct((t, d), MXU_DTYPE), mxu, mxu, acc, acc], grid=(nb,),
        in_specs=[rows, rows, hbm, hbm, full, full, scan_spec, scan_spec],
        out_specs=[rows] + [pl.BlockSpec((nsg, tb, ns), lambda i: (0, nb - 1 - i, 0))] * 2 + [full, full],
        scratch_shapes=[pltpu.VMEM(cc_re.shape, cc_re.dtype), pltpu.VMEM(cc_im.shape, cc_im.dtype),
                        pltpu.VMEM((tb, nsg, ns), F32), pltpu.VMEM((tb, nsg, ns), F32),
                        pltpu.VMEM((nsg, ns), F32), pltpu.VMEM((nsg, ns), F32)],
        args=(dyact, ypre, cc_re, cc_im, ab_re, ab_im, s_re, s_im), rider=rider)


def _s5_grads(name, lam_re, lam_im, s_re, s_im, u, dyp, bb_re, bb_im, dskip, rider=None):
    nsg, t, ns = lam_re.shape
    d = u.shape[1]
    cs = bb_re.shape[1]
    tb = _tile(t, 512, SUBLANE)

    def body(lr_ref, li_ref, sr_ref, si_ref, u_ref, dy_ref, bbr_ref, bbi_ref, d_ref,
             du_ref, dbbr_ref, dbbi_ref, dccr_ref, dcci_ref, dd_ref):
        @pl.when(pl.program_id(1) == 0)
        def _():
            for r in (dbbr_ref, dbbi_ref, dccr_ref, dcci_ref, dd_ref):
                r[...] = jnp.zeros_like(r)

        lr, li, uv, dy = lr_ref[...], li_ref[...], u_ref[...], dy_ref[...]
        dyf = dy.astype(F32)
        du_ref[...] = (_dot(lr, bbr_ref[...], NT) + _dot(li, bbi_ref[...], NT) + d_ref[...] * dyf).astype(du_ref.dtype)
        dbbr_ref[...] += _dot(uv, lr, TN)
        dbbi_ref[...] += _dot(uv, li, TN)
        dccr_ref[...] += _dot(sr_ref[...], dy, TN)
        dcci_ref[...] -= _dot(si_ref[...], dy, TN)
        dd_ref[...] += jnp.sum(dyf * uv, axis=0, keepdims=True)

    s_spec = pl.BlockSpec((None, tb, ns), lambda sg, i: (sg, i, 0))
    col = pl.BlockSpec((tb, cs), lambda sg, i: (i, sg))
    b_spec = pl.BlockSpec((None, cs, ns), lambda sg, i: (sg, 0, 0))
    c_spec = pl.BlockSpec((None, ns, cs), lambda sg, i: (sg, 0, 0))
    vec = pl.BlockSpec((1, cs), lambda sg, i: (0, sg))
    return _call(
        body, name=name,
        out_shape=[jax.ShapeDtypeStruct((t, d), MXU_DTYPE), jax.ShapeDtypeStruct(bb_re.shape, F32),
                   jax.ShapeDtypeStruct(bb_re.shape, F32), jax.ShapeDtypeStruct((nsg, ns, cs), F32),
                   jax.ShapeDtypeStruct((nsg, ns, cs), F32), jax.ShapeDtypeStruct((1, d), F32)],
        grid=(nsg, t // tb), in_specs=[s_spec, s_spec, s_spec, s_spec, col, col, b_spec, b_spec, vec],
        out_specs=[col, b_spec, b_spec, c_spec, c_spec, vec],
        args=(lam_re, lam_im, s_re, s_im, u, dyp, bb_re, bb_im, dskip), rider=rider)


def _shift_down(x, k, prev8):
    if k == 0:
        return x
    ext = jnp.concatenate([prev8, x], axis=0)
    return ext[SUBLANE - k:SUBLANE - k + x.shape[0]]


def _shift_up(x, k, next8):
    if k == 0:
        return x
    ext = jnp.concatenate([x, next8], axis=0)
    return ext[k:k + x.shape[0]]


def _lru_time_block(t):
    return _tile(t, 256, SUBLANE)


def _lru_gates(xp, prev8, cv_ref, wrg, wig):
    taps = cv_ref.shape[0] - 4
    row = lambda k: cv_ref[k:k + 1, :]
    xs = [_shift_down(xp, taps - 1 - k, prev8) for k in range(taps)]
    xb = row(taps)
    for k in range(taps):
        xb = xb + row(k) * xs[k]
    r = jax.nn.sigmoid(_dot(xb, wrg, NN) + row(taps + 1))
    ig = jax.nn.sigmoid(_dot(xb, wig, NN) + row(taps + 2))
    sp = jax.nn.softplus(-row(taps + 3))
    log_a = -LRU_C * r * sp
    a = jnp.exp(log_a)
    mult = jnp.sqrt(_neg_expm1(2.0 * log_a))
    return xs, xb, r, ig, sp, a, mult


def _lru_fwd(name, zz, cvec, wrg, wig, rider=None):
    _, t, w = zz.shape
    half = N_DEV // 2
    tb = _lru_time_block(t)

    def body(gb_ref, xp_ref, xprev_ref, cv_ref, wrg_ref, wig_ref, hs_ref, y_ref, a_scr, b_scr, carry):
        i = pl.program_id(1)

        @pl.when(i == 0)
        def _():
            carry[...] = jnp.zeros_like(carry)

        prev8 = jnp.where(i > 0, xprev_ref[...], 0.0)
        _, xb, _, ig, _, a, mult = _lru_gates(xp_ref[...], prev8, cv_ref, wrg_ref[...], wig_ref[...])
        a_scr[...] = a
        b_scr[...] = mult * (ig * xb)

        def step(j, h):
            h = a_scr[pl.ds(j, 1), :] * h + b_scr[pl.ds(j, 1), :]
            hs_ref[pl.ds(j, 1), :] = h
            return h

        carry[0:1, :] = lax.fori_loop(0, tb, step, carry[0:1, :], unroll=8)
        y_ref[...] = (hs_ref[...] * _gelu(gb_ref[...])).astype(y_ref.dtype)

    nrow = tb // SUBLANE
    blk = lambda off: pl.BlockSpec((None, tb, w), lambda q, i: (q + off, i, 0))
    return _call(
        body, name=name,
        out_shape=[jax.ShapeDtypeStruct((half, t, w), F32), jax.ShapeDtypeStruct((half, t, w), MXU_DTYPE)],
        grid=(half, t // tb),
        in_specs=[blk(0), blk(half),
                  pl.BlockSpec((None, SUBLANE, w), lambda q, i: (q + half, jnp.maximum(i * nrow - 1, 0), 0)),
                  pl.BlockSpec((None,) + cvec.shape[1:], lambda q, i: (q, 0, 0)),
                  pl.BlockSpec((None, w, w), lambda q, i: (q, 0, 0)), pl.BlockSpec((None, w, w), lambda q, i: (q, 0, 0))],
        out_specs=[blk(0), blk(0)],
        scratch_shapes=[pltpu.VMEM((tb, w), F32), pltpu.VMEM((tb, w), F32), pltpu.VMEM((SUBLANE, w), F32)],
        args=(zz, zz, zz, cvec, wrg, wig), rider=rider)


def _lru_bwd(name, zz, hs, dy, cvec, wrg, wig, rider=None):
    _, t, w = zz.shape
    half = N_DEV // 2
    tb = _lru_time_block(t)
    nb = t // tb
    taps = cvec.shape[1] - 4

    def body(gb_ref, xp_ref, xprev_ref, hs_ref, hprev_ref, dy_ref, cv_ref, wrg_ref, wig_ref,
             dgb_ref, dxp_ref, dcv_ref, dwrg_ref, dwig_ref, a_scr, l_scr, carry, dxb_next):
        i = pl.program_id(1)

        @pl.when(i == 0)
        def _():
            for r_ in (carry, dxb_next, dcv_ref, dwrg_ref, dwig_ref):
                r_[...] = jnp.zeros_like(r_)

        has_prev = i < nb - 1
        row = lambda k: cv_ref[k:k + 1, :]
        prev8 = jnp.where(has_prev, xprev_ref[...], 0.0)
        xs, xb, r, ig, sp, a, mult = _lru_gates(xp_ref[...], prev8, cv_ref, wrg_ref[...], wig_ref[...])
        hs_ = hs_ref[...]
        hs_m1 = _shift_down(hs_, 1, jnp.where(has_prev, hprev_ref[...], 0.0))
        gel, dgel = _gelu_and_grad(gb_ref[...])
        dy_ = dy_ref[...]
        dgb_ref[...] = (dy_ * hs_ * dgel).astype(dgb_ref.dtype)
        a_scr[...] = a
        l_scr[...] = dy_ * gel

        def step(k, c):
            j = tb - 1 - k
            lam = l_scr[pl.ds(j, 1), :] + c
            l_scr[pl.ds(j, 1), :] = lam
            return a_scr[pl.ds(j, 1), :] * lam

        carry[0:1, :] = lax.fori_loop(0, tb, step, carry[0:1, :], unroll=8)
        lam = l_scr[...]
        dmult = lam * (ig * xb)
        dig = lam * (mult * xb)
        dxb = lam * (mult * ig)
        dlog_a = (lam * hs_m1) * a - dmult * (a * a) / mult
        dr = dlog_a * (-LRU_C * sp)
        dsp = jnp.sum(dlog_a * (-LRU_C * r), axis=0, keepdims=True)
        dpr = dr * (r * (1.0 - r))
        dpi = dig * (ig * (1.0 - ig))
        dwrg_ref[...] += _dot(xb, dpr, TN)
        dwig_ref[...] += _dot(xb, dpi, TN)
        dxb = dxb + _dot(dpr, wrg_ref[...], NT) + _dot(dpi, wig_ref[...], NT)
        for k in range(taps):
            dcv_ref[k:k + 1, :] += jnp.sum(dxb * xs[k], axis=0, keepdims=True)
        dcv_ref[taps:taps + 1, :] += jnp.sum(dxb, axis=0, keepdims=True)
        dcv_ref[taps + 1:taps + 2, :] += jnp.sum(dpr, axis=0, keepdims=True)
        dcv_ref[taps + 2:taps + 3, :] += jnp.sum(dpi, axis=0, keepdims=True)
        dcv_ref[taps + 3:taps + 4, :] += dsp * (-jax.nn.sigmoid(-row(taps + 3)))
        nxt8 = dxb_next[...]
        dxp = row(taps - 1) * dxb
        for k in range(taps - 1):
            dxp = dxp + row(k) * _shift_up(dxb, taps - 1 - k, nxt8)
        dxp_ref[...] = dxp.astype(dxp_ref.dtype)
        dxb_next[...] = dxb[0:SUBLANE]

    nrow = tb // SUBLANE
    blk = lambda off: pl.BlockSpec((None, tb, w), lambda q, i: (q + off, nb - 1 - i, 0))
    halo = lambda off: pl.BlockSpec((None, SUBLANE, w), lambda q, i: (q + off, jnp.maximum((nb - 1 - i) * nrow - 1, 0), 0))
    wspec = pl.BlockSpec((None, w, w), lambda q, i: (q, 0, 0))
    cspec = pl.BlockSpec((None,) + cvec.shape[1:], lambda q, i: (q, 0, 0))
    act = jax.ShapeDtypeStruct((half, t, w), MXU_DTYPE)
    return _call(
        body, name=name,
        out_shape=[act, act, jax.ShapeDtypeStruct(cvec.shape, F32), jax.ShapeDtypeStruct(wrg.shape, F32),
                   jax.ShapeDtypeStruct(wig.shape, F32)],
        grid=(half, nb),
        in_specs=[blk(0), blk(half), halo(half), blk(0), halo(0), blk(0), cspec, wspec, wspec],
        out_specs=[blk(0), blk(0), cspec, wspec, wspec],
        scratch_shapes=[pltpu.VMEM((tb, w), F32), pltpu.VMEM((tb, w), F32), pltpu.VMEM((SUBLANE, w), F32),
                        pltpu.VMEM((SUBLANE, w), F32)],
        args=(zz, zz, zz, hs, hs, dy, cvec, wrg, wig), rider=rider)


def _band(name, blocks, per, dtype):
    n, a, b = blocks.shape

    def body(x_ref, o_ref):
        o_ref[...] = jnp.zeros_like(o_ref)
        for g in range(per):
            o_ref[g * a:(g + 1) * a, g * b:(g + 1) * b] = x_ref[g].astype(o_ref.dtype)

    return pl.pallas_call(
        body, name=name, out_shape=jax.ShapeDtypeStruct((n // per, per * a, per * b), dtype), grid=(n // per,),
        in_specs=[pl.BlockSpec((per, a, b), lambda s: (s, 0, 0))],
        out_specs=pl.BlockSpec((None, per * a, per * b), lambda s: (s, 0, 0)), compiler_params=_params(1))(blocks)


def _unband(name, bands, per):
    s, pa, pb = bands.shape
    a, b = pa // per, pb // per

    def body(x_ref, o_ref):
        for g in range(per):
            o_ref[g] = x_ref[g * a:(g + 1) * a, g * b:(g + 1) * b]

    return pl.pallas_call(
        body, name=name, out_shape=jax.ShapeDtypeStruct((s * per, a, b), bands.dtype), grid=(s,),
        in_specs=[pl.BlockSpec((None, pa, pb), lambda i: (i, 0, 0))],
        out_specs=pl.BlockSpec((per, a, b), lambda i: (i, 0, 0)), compiler_params=_params(1))(bands)


def _pack(arrays, rows_multiple, lanes=LANE):
    flat = [a.reshape(-1).astype(F32) for a in arrays]
    size = sum(a.shape[0] for a in flat)
    rows = -(-size // (lanes * rows_multiple)) * rows_multiple
    if rows * lanes > size:
        flat.append(jnp.zeros((rows * lanes - size,), F32))
    return jnp.concatenate(flat).reshape(rows, lanes)


def _unpack(packed, shapes):
    flat = packed.reshape(-1)
    out, off = [], 0
    for s in shapes:
        n = math.prod(s)
        out.append(flat[off:off + n].reshape(s))
        off += n
    return out


def kernel(x, c, norm_g, w_ada, b_ada, s5_w_in, s5_lam_re, s5_lam_im, s5_log_dt, s5_b_re, s5_b_im, s5_c_re, s5_c_im, s5_d, s5_w_glu, lru_w_in, lru_conv_w, lru_conv_b, lru_w_rg, lru_b_rg, lru_w_ig, lru_b_ig, lru_lam, lru_w_out, ffn_w_gu, ffn_w_down, final_g, loss_target, m_norm_g, m_w_ada, m_b_ada, m_s5_w_in, m_s5_lam_re, m_s5_lam_im, m_s5_log_dt, m_s5_b_re, m_s5_b_im, m_s5_c_re, m_s5_c_im, m_s5_d, m_s5_w_glu, m_lru_w_in, m_lru_conv_w, m_lru_conv_b, m_lru_w_rg, m_lru_b_rg, m_lru_w_ig, m_lru_b_ig, m_lru_lam, m_lru_w_out, m_ffn_w_gu, m_ffn_w_down, m_final_g, v_norm_g, v_w_ada, v_b_ada, v_s5_w_in, v_s5_lam_re, v_s5_lam_im, v_s5_log_dt, v_s5_b_re, v_s5_b_im, v_s5_c_re, v_s5_c_im, v_s5_d, v_s5_w_glu, v_lru_w_in, v_lru_conv_w, v_lru_conv_b, v_lru_w_rg, v_lru_b_rg, v_lru_w_ig, v_lru_b_ig, v_lru_lam, v_lru_w_out, v_ffn_w_gu, v_ffn_w_down, v_final_g):
    wv = dict(zip(WEIGHTS, (norm_g, w_ada, b_ada, s5_w_in, s5_lam_re, s5_lam_im, s5_log_dt, s5_b_re, s5_b_im, s5_c_re, s5_c_im, s5_d, s5_w_glu, lru_w_in, lru_conv_w, lru_conv_b, lru_w_rg, lru_b_rg, lru_w_ig, lru_b_ig, lru_lam, lru_w_out, ffn_w_gu, ffn_w_down, final_g)))
    mv = dict(zip(WEIGHTS, (m_norm_g, m_w_ada, m_b_ada, m_s5_w_in, m_s5_lam_re, m_s5_lam_im, m_s5_log_dt, m_s5_b_re, m_s5_b_im, m_s5_c_re, m_s5_c_im, m_s5_d, m_s5_w_glu, m_lru_w_in, m_lru_conv_w, m_lru_conv_b, m_lru_w_rg, m_lru_b_rg, m_lru_w_ig, m_lru_b_ig, m_lru_lam, m_lru_w_out, m_ffn_w_gu, m_ffn_w_down, m_final_g)))
    vv = dict(zip(WEIGHTS, (v_norm_g, v_w_ada, v_b_ada, v_s5_w_in, v_s5_lam_re, v_s5_lam_im, v_s5_log_dt, v_s5_b_re, v_s5_b_im, v_s5_c_re, v_s5_c_im, v_s5_d, v_s5_w_glu, v_lru_w_in, v_lru_conv_w, v_lru_conv_b, v_lru_w_rg, v_lru_b_rg, v_lru_w_ig, v_lru_b_ig, v_lru_lam, v_lru_w_out, v_ffn_w_gu, v_ffn_w_down, v_final_g)))

    me = 4 * lax.axis_index("x") + 2 * lax.axis_index("y") + lax.axis_index("c")
    x0 = x[0]
    tgt = loss_target[0]
    t, d = x0.shape
    depth = norm_g.shape[0]
    n_mod = w_ada.shape[2] * N_DEV // d
    groups, states = s5_lam_re.shape[1], s5_lam_re.shape[2]
    per_sg = S5_SUPER // S5_GROUP
    nsg = groups // per_sg
    lw = lru_lam.shape[1] * N_DEV
    lwc = lw // (N_DEV // 2)
    half = N_DEV // 2

    assert depth == 2, "the ride schedule below is written for one S5 layer followed by one RG-LRU layer"
    wire = lambda a: a.astype(WIRE_DTYPE)
    gw = {}

    def riding(job, fn, *args):
        res, (got,) = fn(*args, rider=_gather_rider([wire(job[1])]))
        gw[job[0]] = got
        return res

    sh_shapes = [wv[n].shape for n in SMALL_SHARDED] + [c.shape]
    sh_all = _all_gather("ag_small", _pack([wv[n] for n in SMALL_SHARDED] + [c], SUBLANE))
    sh_parts = [jnp.stack(p) for p in zip(*[_unpack(sh_all[s], sh_shapes) for s in range(N_DEV)])]
    full = {}
    for n, p in zip(SMALL_SHARDED, sh_parts[:-1]):
        full[n] = jnp.moveaxis(p, 0, -2).reshape(p.shape[1:-1] + (-1,))
    c_all = sh_parts[-1].reshape(N_DEV, d)
    c16 = jnp.pad(c_all, ((0, 2 * SUBLANE - N_DEV), (0, 0)))

    n_loc = w_ada.shape[2]
    b_loc = lax.dynamic_slice_in_dim(b_ada, me * n_loc, n_loc, axis=1)[:, None, :]
    mod_part = riding(('s5_in', s5_w_in[0]), _ada_fwd, "ada_fwd", c16, w_ada, b_loc)[:, :N_DEV]
    mod_mine = _chunk_exchange("x_mod", [mod_part.transpose(1, 0, 2)], ALL)
    mod = mod_mine.transpose(1, 0, 2).reshape(depth, n_mod, 1, d)

    lam3 = lambda a: a[0][:, None, :]
    p_lr, p_li, p_ld = lam3(s5_lam_re), lam3(s5_lam_im), s5_log_dt[0][:, None, None]
    p_br, p_bi = s5_b_re[0].transpose(0, 2, 1), s5_b_im[0].transpose(0, 2, 1)
    ab_re3, ab_im3, bb_re3, bb_im3 = _s5_disc("s5_disc", p_lr, p_li, p_ld, p_br, p_bi)
    ab_re, ab_im = ab_re3.reshape(nsg, per_sg * states), ab_im3.reshape(nsg, per_sg * states)
    bb_re = _band("band_bb_re", bb_re3, per_sg, MXU_DTYPE)
    bb_im = _band("band_bb_im", bb_im3, per_sg, MXU_DTYPE)
    cc_re = _band("band_cc_re", s5_c_re[0].transpose(0, 2, 1), per_sg, MXU_DTYPE)
    cc_im = _band("band_cc_im", s5_c_im[0].transpose(0, 2, 1), per_sg, MXU_DTYPE)

    taps = lru_conv_w.shape[1]
    cvec = jnp.concatenate([full['lru_conv_w'].reshape(taps, lw), full['lru_conv_b'], full['lru_b_rg'],
                            full['lru_b_ig'], full['lru_lam']], axis=0)
    cvec = cvec.reshape(taps + 4, half, lwc).transpose(1, 0, 2)
    wrg = _band("band_w_rg", lru_w_rg[0], LRU_BLOCKS_PER_CHUNK, MXU_DTYPE)
    wig = _band("band_w_ig", lru_w_ig[0], LRU_BLOCKS_PER_CHUNK, MXU_DTYPE)

    saved = []
    xc = x0
    for i in range(depth):
        sh1, sc1, g1, sh2, sc2, g2 = [mod[i, k] for k in range(n_mod)]
        gn = full['norm_g'][i]
        h1 = _norm_mod_fwd(f"norm1_fwd{i}", xc, gn[0:1], sc1, sh1)
        if i % 2 == 0:
            u = riding(('s5_glu', s5_w_glu[0]), _mm_row, f"s5_in{i}", h1[None], gw['s5_in'].reshape(d, d))
            s_re, s_im, s_rem, s_imm, ypre, yact = riding((('gu', i), ffn_w_gu[i]), _s5_scan_fwd, f"s5_scan{i}", u, bb_re,
                                                          bb_im, ab_re, ab_im, cc_re, cc_im, s5_d)
            z = riding((('down', i), ffn_w_down[i]), _mm_col, f"s5_glu{i}", yact, gw['s5_glu'])
            x1 = _glu_resid_fwd(f"s5_resid{i}", z, xc, g1)
            mix = (u, s_re, s_im, s_rem, s_imm, ypre, yact, z)
        else:
            zz = _mm_col(f"lru_in{i}", h1, gw['lru_in'])
            hs, ylru = riding((('gu', i), ffn_w_gu[i]), _lru_fwd, f"lru_core{i}", zz, cvec, wrg, wig)
            o = _mm_row(f"lru_out{i}", ylru, gw['lru_out'].reshape(lw, d))
            x1 = _resid(f"lru_resid{i}", xc, o, g1)
            mix = (zz, hs, ylru, o)
        h2 = _norm_mod_fwd(f"norm2_fwd{i}", x1, gn[1:2], sc2, sh2)
        if i % 2 == 0:
            gu = riding(('lru_in', lru_w_in[0]), _mm_col, f"ffn_gu{i}", h2, gw['gu', i], MXU_DTYPE)
            act = _swiglu_act_fwd(f"ffn_act{i}", gu)
            f = riding(('lru_out', lru_w_out[0]), _mm_row, f"ffn_down{i}", act, gw['down', i].reshape(-1, d))
        else:
            gu = riding((('down', i), ffn_w_down[i]), _mm_col, f"ffn_gu{i}", h2, gw['gu', i], MXU_DTYPE)
            act = _swiglu_act_fwd(f"ffn_act{i}", gu)
            f = _mm_row(f"ffn_down{i}", act, gw['down', i].reshape(-1, d))
        x2 = _resid(f"ffn_resid{i}", x1, f, g2)
        saved.append((xc, h1, mix, x1, h2, gu, act, f))
        xc = x2

    dx, loss_part, d_final_g = _loss_bwd("loss", xc, tgt, final_g[None])
    loss = lax.psum(loss_part[0, 0], ("x", "y", "c"))

    grads = {}
    parts = {}
    dmod = [None] * depth
    d_norm_g = [None] * depth
    core = lax.axis_index("c").astype(jnp.int32).reshape(1)
    chunked = lambda p: p.reshape(N_DEV, -1, p.shape[-1])
    to_sibling = lambda p: _sibling_rider(chunked(p))
    pair = lambda name, p, got: _pair_sum(name, chunked(p), got, core)
    over_ici = lambda sums: _chunk_rider([sums], SAME_CORE)
    quarter_over_ici = lambda sums, q: _chunk_rider([sums], SAME_CORE, rows=(q * (sums.shape[1] // 4), sums.shape[1] // 4))

    above = None
    for i in reversed(range(depth)):
        xin, h1, mix, x1, h2, gu, act, f = saved[i]
        sh1, sc1, g1, sh2, sc2, g2 = [mod[i, k] for k in range(n_mod)]
        gn = full['norm_g'][i]
        g_down = gw['down', i].reshape(-1, d)
        df, dg2 = _gate_bwd(f"ffn_gate_bwd{i}", dx, f, g2)
        if above is None:
            dact = _mm_row_da(f"ffn_down_da{i}", df, g_down, half, out_dtype=MXU_DTYPE)
        else:
            dact, (got, early_parts) = _ride(_mm_row_da, f"ffn_down_da{i}", df, g_down, half, out_dtype=MXU_DTYPE,
                                             riders=[to_sibling(above[1]), over_ici(s_early)])
            s_above = pair(f"x_{above[0][0]}_pair", above[1], got)
        p_down = _mm_row_db(f"ffn_down_db{i}", act, df, WIRE_DTYPE)
        if above is None:
            dgu, (got,) = _ride(_swiglu_act_bwd, f"ffn_act_bwd{i}", gu, dact, riders=[to_sibling(p_down)])
        else:
            dgu, (got, (early_total,)) = _ride(
                _swiglu_act_bwd, f"ffn_act_bwd{i}", gu, dact,
                riders=[to_sibling(p_down), _gather_rider([_sum_parts("sum_small_early", early_parts)])])
        s_down = pair(f"x_ffn_w_down{i}_pair", p_down, got)
        if above is None:
            dh2, (parts['ffn_w_down', i],) = _ride(_mm_col_da, f"ffn_gu_da{i}", dgu, gw['gu', i], riders=[over_ici(s_down)])
            p_gu = _mm_col_db(f"ffn_gu_db{i}", h2, dgu, WIRE_DTYPE)
        else:
            dh2, (parts[above[0]],) = _ride(_mm_col_da, f"ffn_gu_da{i}", dgu, gw['gu', i], riders=[over_ici(s_above)])
            p_gu, (parts['ffn_w_down', i],) = _ride(_mm_col_db, f"ffn_gu_db{i}", h2, dgu, WIRE_DTYPE,
                                                    riders=[over_ici(s_down)])
        dx, dgn2, dsc2, dsh2 = _norm_mod_bwd(f"norm2_bwd{i}", x1, dh2, dx, gn[1:2], sc2)
        if i % 2 == 0:
            u, s_re, s_im, s_rem, s_imm, ypre, yact, z = mix
            (dz, dg1), (got,) = _ride(_glu_resid_bwd, f"s5_resid_bwd{i}", z, dx, g1, riders=[to_sibling(p_gu)])
            s_gu = pair(f"x_ffn_w_gu{i}_pair", p_gu, got)
            dyact, (gu_0,) = _ride(_mm_col_da, f"s5_glu_da{i}", dz, gw['s5_glu'], out_dtype=MXU_DTYPE,
                                   riders=[quarter_over_ici(s_gu, 0)])
            p_glu, (gu_1,) = _ride(_mm_col_db, f"s5_glu_db{i}", yact, dz, WIRE_DTYPE, riders=[quarter_over_ici(s_gu, 1)])
            (dyp, l_rem, l_imm, dab_re, dab_im), (gu_2, gu_3, got) = _ride(
                _s5_scan_bwd, f"s5_scan_bwd{i}", dyact, ypre, cc_re, cc_im, ab_re, ab_im, s_re, s_im,
                riders=[quarter_over_ici(s_gu, 2), quarter_over_ici(s_gu, 3), to_sibling(p_glu)])
            parts['ffn_w_gu', i] = [gu_0, gu_1, gu_2, gu_3]
            s_glu = pair("x_s5_w_glu_pair", p_glu, got)
            (du, dbb_re, dbb_im, dcc_re, dcc_im, dd), (parts['s5_w_glu', 0],) = _ride(
                _s5_grads, f"s5_grads{i}", l_rem, l_imm, s_rem, s_imm, u, dyp, bb_re, bb_im, s5_d, riders=[over_ici(s_glu)])
            dlr, dli, dld, dbr, dbi = _s5_disc_bwd(
                "s5_disc_bwd", p_lr, p_li, p_ld, p_br, p_bi, dab_re.reshape(groups, 1, states),
                dab_im.reshape(groups, 1, states), _unband("unband_bb_re", dbb_re, per_sg),
                _unband("unband_bb_im", dbb_im, per_sg))
            grads['s5_lam_re'], grads['s5_lam_im'], grads['s5_log_dt'] = dlr[:, 0][None], dli[:, 0][None], dld[:, 0, 0][None]
            grads['s5_b_re'], grads['s5_b_im'] = dbr.transpose(0, 2, 1)[None], dbi.transpose(0, 2, 1)[None]
            grads['s5_c_re'] = _unband("unband_cc_re", dcc_re, per_sg).transpose(0, 2, 1)[None]
            grads['s5_c_im'] = _unband("unband_cc_im", dcc_im, per_sg).transpose(0, 2, 1)[None]
            grads['s5_d'] = dd
            dub = du
            p_s5_in = _mm_row_db(f"s5_in_db{i}", h1[None], dub, WIRE_DTYPE)
            dh1, (got,) = _ride(_mm_row_da, f"s5_in_da{i}", dub, gw['s5_in'].reshape(d, d), 1, riders=[to_sibling(p_s5_in)])
            dh1 = dh1[0]
            s_s5_in = pair("x_s5_w_in_pair", p_s5_in, got)
            dx, dgn1, dsc1, dsh1 = _norm_mod_bwd(f"norm1_bwd{i}", xin, dh1, dx, gn[0:1], sc1)
        else:
            zz, hs, ylru, o = mix
            g_lru_out = gw['lru_out'].reshape(lw, d)
            do, dg1 = _gate_bwd(f"lru_gate_bwd{i}", dx, o, g1)
            dyl, (got,) = _ride(_mm_row_da, f"lru_out_da{i}", do, g_lru_out, half, riders=[to_sibling(p_gu)])
            s_gu = pair(f"x_ffn_w_gu{i}_pair", p_gu, got)
            p_lru_out = _mm_row_db(f"lru_out_db{i}", ylru, do, WIRE_DTYPE)
            (dgb, dxp, dcv, dwrg, dwig), (*parts['ffn_w_gu', i], got) = _ride(
                _lru_bwd, f"lru_core_bwd{i}", zz, hs, dyl, cvec, wrg, wig,
                riders=[quarter_over_ici(s_gu, q) for q in range(4)] + [to_sibling(p_lru_out)])
            s_lru_out = pair("x_lru_w_out_pair", p_lru_out, got)
            dzz = jnp.concatenate([dgb, dxp], axis=0)
            dh1, (parts['lru_w_out', 0],) = _ride(_mm_col_da, f"lru_in_da{i}", dzz, gw['lru_in'],
                                                  riders=[over_ici(s_lru_out)])
            above = (('lru_w_in', 0), _mm_col_db(f"lru_in_db{i}", h1, dzz, WIRE_DTYPE))
            dcv = dcv.transpose(1, 0, 2).reshape(taps + 4, lw)
            grads['lru_conv_w'] = dcv[:taps].reshape(1, taps, 1, lw)
            grads['lru_conv_b'], grads['lru_b_rg'] = dcv[taps:taps + 1], dcv[taps + 1:taps + 2]
            grads['lru_b_ig'], grads['lru_lam'] = dcv[taps + 2:taps + 3], dcv[taps + 3:taps + 4]
            grads['lru_w_rg'] = _unband("unband_w_rg", dwrg, LRU_BLOCKS_PER_CHUNK)[None]
            grads['lru_w_ig'] = _unband("unband_w_ig", dwig, LRU_BLOCKS_PER_CHUNK)[None]
            early = _pack([grads[n] for n in SMALL_EARLY], SUBLANE * N_DEV).reshape(N_DEV, -1, LANE)
            (dx, dgn1, dsc1, dsh1), (got,) = _ride(_norm_mod_bwd, f"norm1_bwd{i}", xin, dh1, dx, gn[0:1], sc1,
                                                   riders=[_sibling_rider(early)])
            s_early = pair("x_small_early_pair", early, got)
        dmod[i] = jnp.concatenate([dsh1, dsc1, dg1, dsh2, dsc2, dg2], axis=1)
        d_norm_g[i] = jnp.concatenate([dgn1, dgn2], axis=0)
    grad_x = dx[None]
    dmod = jnp.concatenate(dmod, axis=0)
    grads['norm_g'] = jnp.stack(d_norm_g)
    grads['b_ada'] = dmod
    grads['final_g'] = d_final_g[0]

    small_partial = _pack([grads[n] for n in SMALL_LATE], SUBLANE * N_DEV)
    rows8 = small_partial.shape[0] // N_DEV
    small_partial = small_partial.reshape(N_DEV, rows8, LANE)
    s_small = pair("x_small_pair", small_partial, _ride_alone("x_small_d2d", _sibling_rider(small_partial)))
    parts['s5_w_in', 0], small_parts = _ride_alone("x_tail_ici", _join([over_ici(s_s5_in), over_ici(s_small)]))

    out = {}
    dmod_all = _all_gather("ag_dmod", dmod)
    dmod_loc = lax.dynamic_slice_in_dim(dmod_all, me * n_loc, n_loc, axis=2).transpose(1, 0, 2)
    dmod16 = jnp.pad(dmod_loc, ((0, 0), (0, 2 * SUBLANE - N_DEV), (0, 0)))
    out['w_ada'] = _adamw_w_ada("adamw_w_ada", c16, dmod16, w_ada, m_w_ada, v_w_ada)

    for name in BIG[1:]:
        w = wv[name]
        rows, cols = w.shape[-2] * w.shape[0], w.shape[-1]
        flat = lambda a: a.reshape(rows, cols)
        pieces = []
        for l in range(w.shape[0]):
            pieces += parts[name, l] if isinstance(parts[name, l], list) else [parts[name, l]]
        res = _adamw_sum("adamw_" + name, pieces, flat(w), flat(mv[name]), flat(vv[name]))
        out[name] = [r.reshape(w.shape) for r in res]

    summed = _sum_parts("sum_small", small_parts)
    small_total = _all_gather("ag_small_sum", summed).reshape(-1, LANE)
    small_grad = dict(zip(SMALL_LATE, _unpack(small_total, [grads[n].shape for n in SMALL_LATE])))
    small_grad.update(zip(SMALL_EARLY, _unpack(early_total.reshape(-1, LANE), [grads[n].shape for n in SMALL_EARLY])))
    for n in SMALL_SHARDED:
        shard = wv[n].shape[-1]
        small_grad[n] = lax.dynamic_slice_in_dim(small_grad[n], me * shard, shard, axis=small_grad[n].ndim - 1)
    for n in SMALL:
        w = wv[n]
        flat = lambda a: a.reshape(-1, w.shape[-1])
        res = _adamw_sum("adamw_" + n, [flat(small_grad[n])[None]], flat(w), flat(mv[n]), flat(vv[n]))
        out[n] = [r.reshape(w.shape) for r in res]

    return (loss, grad_x, *[out[n][0] for n in WEIGHTS], *[out[n][1] for n in WEIGHTS],
            *[out[n][2] for n in WEIGHTS], *[out[n][3] for n in WEIGHTS])
```
